```python
import jax
import jax.numpy as jnp
from jax import lax
import numpy as np

D_MODEL = 1024
BATCH = 8
SEQ = 4096
DEPTH = 4

CTX_LEN = 256
GRID_W = 64
N_EVEN = (DEPTH + 1) // 2
N_ODD = DEPTH // 2
EPS = 1e-6

RET_HEADS = 4
RET_HEAD_DIM = D_MODEL // 8
RET_W = RET_HEADS * RET_HEAD_DIM
RET_CHUNK = 128
ROPE_BASE = 10000.0
ROPE_PAIRS = (RET_HEAD_DIM // 8, 3 * RET_HEAD_DIM // 16, 3 * RET_HEAD_DIM // 16)
CONV_CH = D_MODEL // 2
CONV_K = 31
EVEN_IN = 4 * RET_W + 2 * CONV_CH
EVEN_MIX = RET_W + CONV_CH
POOL_CH = D_MODEL // 2
POOL_WINDOWS = (2, 4, 8, 16)
POOL_GROUPS = len(POOL_WINDOWS)
POOL_GC = POOL_CH // POOL_GROUPS
SG_CH = D_MODEL // 2
SG_GROUPS = 4
SG_GC = SG_CH // SG_GROUPS
SG_CHUNK = 128
ODD_IN = POOL_CH + 2 * SG_CH
ODD_MIX = POOL_CH + SG_CH
D_FF = ((8 * D_MODEL // 3 + 255) // 256) * 256

kernel_name = 'hybrid_retention_conformer_pool_gmlp_prefix_dit'


def rms_norm(x, w):
    x32 = x.astype(jnp.float32)
    y = x32 * lax.rsqrt(jnp.mean(x32 * x32, axis=-1, keepdims=True) + EPS)
    return (y * w.astype(jnp.float32)).astype(x.dtype)


def layer_norm(x, w, b):
    x32 = x.astype(jnp.float32)
    mu = jnp.mean(x32, axis=-1, keepdims=True)
    xc = x32 - mu
    y = xc * lax.rsqrt(jnp.mean(xc * xc, axis=-1, keepdims=True) + EPS)
    return (y * w.astype(jnp.float32) + b.astype(jnp.float32)).astype(x.dtype)


def modulate(h, shift, scale):
    return h * (1.0 + scale) + shift


def swiglu(h, w_gate, w_up, w_down):
    return (jax.nn.silu(h @ w_gate) * (h @ w_up)) @ w_down


def rope_angles(p_seq, p_row, p_col):
    parts = []
    for p, n in zip((p_seq, p_row, p_col), ROPE_PAIRS):
        freq = ROPE_BASE ** (-jnp.arange(n, dtype=jnp.float32) / n)
        parts.append(p[:, None] * freq[None, :])
    return jnp.concatenate(parts, axis=-1)


def apply_rope(t, ang):
    t32 = t.astype(jnp.float32)
    half = t32.shape[-1] // 2
    t1, t2 = t32[..., :half], t32[..., half:]
    cos, sin = jnp.cos(ang), jnp.sin(ang)
    return jnp.concatenate([t1 * cos - t2 * sin, t1 * sin + t2 * cos], axis=-1)


def split_heads(t):
    b, l, _ = t.shape
    return t.reshape(b, l, RET_HEADS, RET_HEAD_DIM).transpose(0, 2, 1, 3)


def retention_chunkwise(q, k, v, log_g, s0, inclusive):
    b, h, l, dk = q.shape
    dv = v.shape[-1]
    n = l // RET_CHUNK
    qc = q.reshape(b, h, n, RET_CHUNK, dk)
    kc = k.reshape(b, h, n, RET_CHUNK, dk)
    vc = v.reshape(b, h, n, RET_CHUNK, dv)
    pos = jnp.arange(RET_CHUNK, dtype=jnp.float32)
    diff = pos[:, None] - pos[None, :]
    if inclusive:
        mask = diff >= 0
        expo = diff
        xi_exp = pos + 1.0
    else:
        mask = diff > 0
        expo = diff - 1.0
        xi_exp = pos
    lg = log_g[:, None, None]
    dmat = jnp.where(mask[None], jnp.exp(lg * jnp.where(mask, expo, 0.0)[None]), 0.0)
    xi = jnp.exp(log_g[:, None] * xi_exp[None, :])
    zeta = jnp.exp(log_g[:, None] * (RET_CHUNK - 1.0 - pos)[None, :])
    chunk_decay = jnp.exp(log_g * RET_CHUNK)[None, :, None, None]
    scores = jnp.einsum('bhncd,bhnmd->bhncm', qc, kc) * dmat[None, :, None]
    intra = jnp.einsum('bhncm,bhnme->bhnce', scores, vc)
    kv = jnp.einsum('bhnmd,hm,bhnme->bhnde', kc, zeta, vc)

    def step(s, kv_n):
        return s * chunk_decay + kv_n, s

    _, s_prev = lax.scan(step, s0, jnp.moveaxis(kv, 2, 0))
    inter = jnp.einsum('bhncd,hc,nbhde->bhnce', qc, xi, s_prev)
    return (intra + inter).reshape(b, h, l, dv)


def retention_final_states(k, v, log_g2):
    l = k.shape[2]
    pos = jnp.arange(l, dtype=jnp.float32)
    w_f = jnp.exp(log_g2[0][:, None] * (l - 1.0 - pos)[None, :])
    w_b = jnp.exp(log_g2[1][:, None] * pos[None, :])
    s_f = jnp.einsum('bhld,hl,bhle->bhde', k, w_f, v)
    s_b = jnp.einsum('bhld,hl,bhle->bhde', k, w_b, v)
    return s_f, s_b


def bidir_retention(q, k, v, log_g2, s_f, s_b):
    flip = lambda t: jnp.flip(t, axis=2)
    fwd = retention_chunkwise(q, k, v, log_g2[0], s_f, True)
    bwd = flip(retention_chunkwise(flip(q), flip(k), flip(v), log_g2[1], s_b, False))
    return fwd + bwd


def retention_kv(h, w_in, ang):
    k = apply_rope(split_heads(h @ w_in[:, RET_W:2 * RET_W]), ang)
    v = split_heads(h @ w_in[:, 2 * RET_W:3 * RET_W]).astype(jnp.float32)
    return k, v


def even_project(h, w_in, ang):
    p = h @ w_in
    q, k, v, g, a, gb = jnp.split(p, [RET_W, 2 * RET_W, 3 * RET_W, 4 * RET_W, 4 * RET_W + CONV_CH], axis=-1)
    q = apply_rope(split_heads(q), ang) * (RET_HEAD_DIM ** -0.5)
    k = apply_rope(split_heads(k), ang)
    v = split_heads(v).astype(jnp.float32)
    return q, k, v, g, a * jax.nn.sigmoid(gb)


def depthwise_conv(u, w):
    return lax.conv_general_dilated(
        u, w.astype(u.dtype)[:, None, :], window_strides=(1,),
        padding=[(CONV_K // 2, CONV_K // 2)],
        dimension_numbers=('NWC', 'WIO', 'NWC'), feature_group_count=u.shape[-1])


def even_mix(q, k, v, g, u, s_f, s_b, log_g2, conv_w, ln_w, ln_b, w_out):
    b, _, l, _ = q.shape
    y = bidir_retention(q, k, v, log_g2, s_f, s_b)
    y = y * lax.rsqrt(jnp.mean(y * y, axis=-1, keepdims=True) + EPS)
    y = y.transpose(0, 2, 1, 3).reshape(b, l, RET_W).astype(g.dtype)
    ret_out = jax.nn.silu(g) * y
    conv_out = jax.nn.silu(layer_norm(depthwise_conv(u, conv_w), ln_w, ln_b))
    return jnp.concatenate([ret_out, conv_out], axis=-1) @ w_out


def pool_minus_token(h32):
    b, l, _ = h32.shape
    cs = jnp.concatenate([jnp.zeros((b, 1, POOL_CH), jnp.float32), lax.cumsum(h32, axis=1)], axis=1)
    t = jnp.arange(l)
    outs = []
    for gi, w in enumerate(POOL_WINDOWS):
        left = w // 2
        right = w - 1 - left
        lo = jnp.clip(t - left, 0, l - 1)
        hi = jnp.clip(t + right, 0, l - 1)
        csg = cs[..., gi * POOL_GC:(gi + 1) * POOL_GC]
        mean = (csg[:, hi + 1] - csg[:, lo]) / (hi - lo + 1).astype(jnp.float32)[None, :, None]
        outs.append(mean - h32[..., gi * POOL_GC:(gi + 1) * POOL_GC])
    return jnp.stack(outs, axis=2)


def odd_stream(h, w_in, w_out, pool_w, pool_scale, sg_ln_w, sg_ln_b, sg_w, sg_b):
    b, l, _ = h.shape
    p = h @ w_in
    pc, pd = p[..., :POOL_CH], p[..., POOL_CH:]
    m = pool_minus_token(pc.astype(jnp.float32))
    pool_out = jnp.einsum('blgc,gcd->blgd', m, pool_w.astype(jnp.float32)).reshape(b, l, POOL_CH)
    pool_out = pool_out.astype(h.dtype) * pool_scale
    z = jax.nn.gelu(pd, approximate=False)
    u, v = jnp.split(z, 2, axis=-1)
    v = layer_norm(v, sg_ln_w, sg_ln_b).reshape(b, l // SG_CHUNK, SG_CHUNK, SG_GROUPS, SG_GC)
    s = jnp.einsum('bnpgc,gqp->bnqgc', v, sg_w) + sg_b.T[None, None, :, :, None]
    sg_out = u * s.reshape(b, l, SG_CH)
    return jnp.concatenate([pool_out, sg_out], axis=-1) @ w_out


def _fwd_setup_inputs(seed: int = 0) -> dict:
    key = jax.random.key(seed)
    ks = jax.random.split(key, 25)
    f32 = jnp.float32
    nrm = lambda k, shape, s: jax.random.normal(k, shape, f32) * s
    gamma0 = 1.0 - 2.0 ** (-5.0 - np.arange(RET_HEADS))
    decay_base = jnp.asarray(np.log(gamma0 / (1.0 - gamma0)), f32)
    return {
        'x': nrm(ks[0], (BATCH, SEQ, D_MODEL), 1.0),
        'c': nrm(ks[1], (BATCH, D_MODEL), 1.0),
        'ctx': nrm(ks[2], (BATCH, CTX_LEN, D_MODEL), 1.0),
        'c_ctx': nrm(ks[3], (D_MODEL,), 1.0),
        'ada_w': nrm(ks[4], (DEPTH, D_MODEL, 6 * D_MODEL), 0.5 * D_MODEL ** -0.5),
        'ada_b': nrm(ks[5], (DEPTH, 6 * D_MODEL), 0.02),
        'norm_w': 1.0 + nrm(ks[6], (DEPTH, 2, D_MODEL), 0.02),
        'even_w_in': nrm(ks[7], (N_EVEN, D_MODEL, EVEN_IN), D_MODEL ** -0.5),
        'even_w_out': nrm(ks[8], (N_EVEN, EVEN_MIX, D_MODEL), EVEN_MIX ** -0.5),
        'ret_decay_logit': decay_base + nrm(ks[9], (N_EVEN, 2, RET_HEADS), 0.1),
        'conv_dw_w': nrm(ks[10], (N_EVEN, CONV_K, CONV_CH), CONV_K ** -0.5),
        'conv_ln_w': 1.0 + nrm(ks[11], (N_EVEN, CONV_CH), 0.02),
        'conv_ln_b': nrm(ks[12], (N_EVEN, CONV_CH), 0.02),
        'odd_w_in': nrm(ks[13], (N_ODD, D_MODEL, ODD_IN), D_MODEL ** -0.5),
        'odd_w_out': nrm(ks[14], (N_ODD, ODD_MIX, D_MODEL), ODD_MIX ** -0.5),
        'pool_w': nrm(ks[15], (N_ODD, POOL_GROUPS, POOL_GC, POOL_GC), POOL_GC ** -0.5),
        'pool_scale': 1.0 + nrm(ks[16], (N_ODD, POOL_CH), 0.02),
        'sg_ln_w': 1.0 + nrm(ks[17], (N_ODD, SG_CH), 0.02),
        'sg_ln_b': nrm(ks[18], (N_ODD, SG_CH), 0.02),
        'sg_w': nrm(ks[19], (N_ODD, SG_GROUPS, SG_CHUNK, SG_CHUNK), 0.5 * SG_CHUNK ** -0.5),
        'sg_b': 1.0 + nrm(ks[20], (N_ODD, SG_GROUPS, SG_CHUNK), 0.02),
        'ffn_w_gate': nrm(ks[21], (DEPTH, D_MODEL, D_FF), D_MODEL ** -0.5),
        'ffn_w_up': nrm(ks[22], (DEPTH, D_MODEL, D_FF), D_MODEL ** -0.5),
        'ffn_w_down': nrm(ks[23], (DEPTH, D_FF, D_MODEL), D_FF ** -0.5),
        'final_norm_w': 1.0 + nrm(ks[24], (D_MODEL,), 0.02),
    }


def _fwd_reference(x, c, ctx, c_ctx, ada_w, ada_b, norm_w, even_w_in, even_w_out, ret_decay_logit,
              conv_dw_w, conv_ln_w, conv_ln_b, odd_w_in, odd_w_out, pool_w, pool_scale,
              sg_ln_w, sg_ln_b, sg_w, sg_b, ffn_w_gate, ffn_w_up, ffn_w_down, final_norm_w):
    b, l, d = x.shape
    lc = ctx.shape[1]
    rows = l // GRID_W
    grid_r = jnp.broadcast_to(jnp.arange(rows, dtype=jnp.float32)[:, None], (rows, GRID_W)).reshape(-1)
    grid_c = jnp.broadcast_to(jnp.arange(GRID_W, dtype=jnp.float32)[None, :], (rows, GRID_W)).reshape(-1)
    ang_x = rope_angles(jnp.full((l,), lc, jnp.float32), grid_r, grid_c)
    zeros_c = jnp.zeros((lc,), jnp.float32)
    ang_c = rope_angles(jnp.arange(lc, dtype=jnp.float32), zeros_c, zeros_c)
    silu_c = jax.nn.silu(c)
    silu_cc = jax.nn.silu(c_ctx)

    for i in range(DEPTH):
        j = i // 2
        even = i % 2 == 0
        ctx_after = any(m % 2 == 0 for m in range(i + 1, DEPTH))
        use_ctx = even or ctx_after
        mod_x = (silu_c @ ada_w[i] + ada_b[i]).reshape(b, 6, 1, d)
        hx = modulate(rms_norm(x, norm_w[i, 0]), mod_x[:, 0], mod_x[:, 1])
        if use_ctx:
            mod_c = (silu_cc @ ada_w[i] + ada_b[i]).reshape(6, 1, d)
            hc = modulate(rms_norm(ctx, norm_w[i, 0]), mod_c[0], mod_c[1])
        if even:
            log_g2 = jax.nn.log_sigmoid(ret_decay_logit[j].astype(jnp.float32))
            if ctx_after:
                parts_c = even_project(hc, even_w_in[j], ang_c)
                kc, vc = parts_c[1], parts_c[2]
            else:
                kc, vc = retention_kv(hc, even_w_in[j], ang_c)
            s_f, s_b = retention_final_states(kc, vc, log_g2)
            yx = even_mix(*even_project(hx, even_w_in[j], ang_x), s_f, s_b, log_g2,
                          conv_dw_w[j], conv_ln_w[j], conv_ln_b[j], even_w_out[j])
            if ctx_after:
                s_zero = jnp.zeros_like(s_f)
                yc = even_mix(*parts_c, s_zero, s_zero, log_g2,
                              conv_dw_w[j], conv_ln_w[j], conv_ln_b[j], even_w_out[j])
        else:
            yx = odd_stream(hx, odd_w_in[j], odd_w_out[j], pool_w[j], pool_scale[j],
                            sg_ln_w[j], sg_ln_b[j], sg_w[j], sg_b[j])
            if ctx_after:
                yc = odd_stream(hc, odd_w_in[j], odd_w_out[j], pool_w[j], pool_scale[j],
                                sg_ln_w[j], sg_ln_b[j], sg_w[j], sg_b[j])
        x = x + mod_x[:, 2] * yx
        x = x + mod_x[:, 5] * swiglu(modulate(rms_norm(x, norm_w[i, 1]), mod_x[:, 3], mod_x[:, 4]),
                                     ffn_w_gate[i], ffn_w_up[i], ffn_w_down[i])
        if ctx_after:
            ctx = ctx + mod_c[2] * yc
            ctx = ctx + mod_c[5] * swiglu(modulate(rms_norm(ctx, norm_w[i, 1]), mod_c[3], mod_c[4]),
                                          ffn_w_gate[i], ffn_w_up[i], ffn_w_down[i])
    return rms_norm(x, final_norm_w)


import jax as _jax
import jax.numpy as _jnp

TWIN_FORMAT = 'train_step'
FWD_PARAMS = ['x', 'c', 'ctx', 'c_ctx', 'ada_w', 'ada_b', 'norm_w', 'even_w_in', 'even_w_out', 'ret_decay_logit', 'conv_dw_w', 'conv_ln_w', 'conv_ln_b', 'odd_w_in', 'odd_w_out', 'pool_w', 'pool_scale', 'sg_ln_w', 'sg_ln_b', 'sg_w', 'sg_b', 'ffn_w_gate', 'ffn_w_up', 'ffn_w_down', 'final_norm_w']
TWIN_WEIGHTS = ['c_ctx', 'ada_w', 'ada_b', 'norm_w', 'even_w_in', 'even_w_out', 'ret_decay_logit', 'conv_dw_w', 'conv_ln_w', 'conv_ln_b', 'odd_w_in', 'odd_w_out', 'pool_w', 'pool_scale', 'sg_ln_w', 'sg_ln_b', 'sg_w', 'sg_b', 'ffn_w_gate', 'ffn_w_up', 'ffn_w_down', 'final_norm_w']
TWIN_DIFF_INPUT = 'x'
TWIN_INPUTS = ['x', 'c', 'ctx', 'c_ctx', 'ada_w', 'ada_b', 'norm_w', 'even_w_in', 'even_w_out', 'ret_decay_logit', 'conv_dw_w', 'conv_ln_w', 'conv_ln_b', 'odd_w_in', 'odd_w_out', 'pool_w', 'pool_scale', 'sg_ln_w', 'sg_ln_b', 'sg_w', 'sg_b', 'ffn_w_gate', 'ffn_w_up', 'ffn_w_down', 'final_norm_w', 'loss_target', 'm_c_ctx', 'm_ada_w', 'm_ada_b', 'm_norm_w', 'm_even_w_in', 'm_even_w_out', 'm_ret_decay_logit', 'm_conv_dw_w', 'm_conv_ln_w', 'm_conv_ln_b', 'm_odd_w_in', 'm_odd_w_out', 'm_pool_w', 'm_pool_scale', 'm_sg_ln_w', 'm_sg_ln_b', 'm_sg_w', 'm_sg_b', 'm_ffn_w_gate', 'm_ffn_w_up', 'm_ffn_w_down', 'm_final_norm_w', 'v_c_ctx', 'v_ada_w', 'v_ada_b', 'v_norm_w', 'v_even_w_in', 'v_even_w_out', 'v_ret_decay_logit', 'v_conv_dw_w', 'v_conv_ln_w', 'v_conv_ln_b', 'v_odd_w_in', 'v_odd_w_out', 'v_pool_w', 'v_pool_scale', 'v_sg_ln_w', 'v_sg_ln_b', 'v_sg_w', 'v_sg_b', 'v_ffn_w_gate', 'v_ffn_w_up', 'v_ffn_w_down', 'v_final_norm_w']
TWIN_OUTPUTS = ['loss', 'grad_x', 'grad_c_ctx', 'grad_ada_w', 'grad_ada_b', 'grad_norm_w', 'grad_even_w_in', 'grad_even_w_out', 'grad_ret_decay_logit', 'grad_conv_dw_w', 'grad_conv_ln_w', 'grad_conv_ln_b', 'grad_odd_w_in', 'grad_odd_w_out', 'grad_pool_w', 'grad_pool_scale', 'grad_sg_ln_w', 'grad_sg_ln_b', 'grad_sg_w', 'grad_sg_b', 'grad_ffn_w_gate', 'grad_ffn_w_up', 'grad_ffn_w_down', 'grad_final_norm_w', 'delta_c_ctx', 'delta_ada_w', 'delta_ada_b', 'delta_norm_w', 'delta_even_w_in', 'delta_even_w_out', 'delta_ret_decay_logit', 'delta_conv_dw_w', 'delta_conv_ln_w', 'delta_conv_ln_b', 'delta_odd_w_in', 'delta_odd_w_out', 'delta_pool_w', 'delta_pool_scale', 'delta_sg_ln_w', 'delta_sg_ln_b', 'delta_sg_w', 'delta_sg_b', 'delta_ffn_w_gate', 'delta_ffn_w_up', 'delta_ffn_w_down', 'delta_final_norm_w', 'new_m_c_ctx', 'new_m_ada_w', 'new_m_ada_b', 'new_m_norm_w', 'new_m_even_w_in', 'new_m_even_w_out', 'new_m_ret_decay_logit', 'new_m_conv_dw_w', 'new_m_conv_ln_w', 'new_m_conv_ln_b', 'new_m_odd_w_in', 'new_m_odd_w_out', 'new_m_pool_w', 'new_m_pool_scale', 'new_m_sg_ln_w', 'new_m_sg_ln_b', 'new_m_sg_w', 'new_m_sg_b', 'new_m_ffn_w_gate', 'new_m_ffn_w_up', 'new_m_ffn_w_down', 'new_m_final_norm_w', 'new_v_c_ctx', 'new_v_ada_w', 'new_v_ada_b', 'new_v_norm_w', 'new_v_even_w_in', 'new_v_even_w_out', 'new_v_ret_decay_logit', 'new_v_conv_dw_w', 'new_v_conv_ln_w', 'new_v_conv_ln_b', 'new_v_odd_w_in', 'new_v_odd_w_out', 'new_v_pool_w', 'new_v_pool_scale', 'new_v_sg_ln_w', 'new_v_sg_ln_b', 'new_v_sg_w', 'new_v_sg_b', 'new_v_ffn_w_gate', 'new_v_ffn_w_up', 'new_v_ffn_w_down', 'new_v_final_norm_w']
TWIN_LEAF_KINDS = {'loss': 'loss', 'grad_x': 'grad_x', 'grad_c_ctx': 'grad_w', 'grad_ada_w': 'grad_w', 'grad_ada_b': 'grad_w', 'grad_norm_w': 'grad_w', 'grad_even_w_in': 'grad_w', 'grad_even_w_out': 'grad_w', 'grad_ret_decay_logit': 'grad_w', 'grad_conv_dw_w': 'grad_w', 'grad_conv_ln_w': 'grad_w', 'grad_conv_ln_b': 'grad_w', 'grad_odd_w_in': 'grad_w', 'grad_odd_w_out': 'grad_w', 'grad_pool_w': 'grad_w', 'grad_pool_scale': 'grad_w', 'grad_sg_ln_w': 'grad_w', 'grad_sg_ln_b': 'grad_w', 'grad_sg_w': 'grad_w', 'grad_sg_b': 'grad_w', 'grad_ffn_w_gate': 'grad_w', 'grad_ffn_w_up': 'grad_w', 'grad_ffn_w_down': 'grad_w', 'grad_final_norm_w': 'grad_w', 'delta_c_ctx': 'delta_w', 'delta_ada_w': 'delta_w', 'delta_ada_b': 'delta_w', 'delta_norm_w': 'delta_w', 'delta_even_w_in': 'delta_w', 'delta_even_w_out': 'delta_w', 'delta_ret_decay_logit': 'delta_w', 'delta_conv_dw_w': 'delta_w', 'delta_conv_ln_w': 'delta_w', 'delta_conv_ln_b': 'delta_w', 'delta_odd_w_in': 'delta_w', 'delta_odd_w_out': 'delta_w', 'delta_pool_w': 'delta_w', 'delta_pool_scale': 'delta_w', 'delta_sg_ln_w': 'delta_w', 'delta_sg_ln_b': 'delta_w', 'delta_sg_w': 'delta_w', 'delta_sg_b': 'delta_w', 'delta_ffn_w_gate': 'delta_w', 'delta_ffn_w_up': 'delta_w', 'delta_ffn_w_down': 'delta_w', 'delta_final_norm_w': 'delta_w', 'new_m_c_ctx': 'new_m', 'new_m_ada_w': 'new_m', 'new_m_ada_b': 'new_m', 'new_m_norm_w': 'new_m', 'new_m_even_w_in': 'new_m', 'new_m_even_w_out': 'new_m', 'new_m_ret_decay_logit': 'new_m', 'new_m_conv_dw_w': 'new_m', 'new_m_conv_ln_w': 'new_m', 'new_m_conv_ln_b': 'new_m', 'new_m_odd_w_in': 'new_m', 'new_m_odd_w_out': 'new_m', 'new_m_pool_w': 'new_m', 'new_m_pool_scale': 'new_m', 'new_m_sg_ln_w': 'new_m', 'new_m_sg_ln_b': 'new_m', 'new_m_sg_w': 'new_m', 'new_m_sg_b': 'new_m', 'new_m_ffn_w_gate': 'new_m', 'new_m_ffn_w_up': 'new_m', 'new_m_ffn_w_down': 'new_m', 'new_m_final_norm_w': 'new_m', 'new_v_c_ctx': 'new_v', 'new_v_ada_w': 'new_v', 'new_v_ada_b': 'new_v', 'new_v_norm_w': 'new_v', 'new_v_even_w_in': 'new_v', 'new_v_even_w_out': 'new_v', 'new_v_ret_decay_logit': 'new_v', 'new_v_conv_dw_w': 'new_v', 'new_v_conv_ln_w': 'new_v', 'new_v_conv_ln_b': 'new_v', 'new_v_odd_w_in': 'new_v', 'new_v_odd_w_out': 'new_v', 'new_v_pool_w': 'new_v', 'new_v_pool_scale': 'new_v', 'new_v_sg_ln_w': 'new_v', 'new_v_sg_ln_b': 'new_v', 'new_v_sg_w': 'new_v', 'new_v_sg_b': 'new_v', 'new_v_ffn_w_gate': 'new_v', 'new_v_ffn_w_up': 'new_v', 'new_v_ffn_w_down': 'new_v', 'new_v_final_norm_w': 'new_v'}


def _forward(args):
    return _fwd_reference(*[args[k] for k in FWD_PARAMS])


def _output_shape():
    def fwd():
        inp = _fwd_setup_inputs(0)
        return _fwd_reference(*[inp[k] for k in FWD_PARAMS])
    out = _jax.eval_shape(fwd)
    return out.shape, out.dtype

N_MICROBATCH = 1
ADAM_LR = 0.001
ADAM_B1 = 0.9
ADAM_B2 = 0.999
ADAM_EPS = 1e-08
ADAM_WD = 0.01
ADAM_STEP = 10
PER_EXAMPLE_BATCH_AXIS = {'x': 0, 'c': 0, 'ctx': 0, 'loss_target': 0}
SHARED_INPUTS = []
_WEIGHT_DTYPES = {'c_ctx': _jnp.float32, 'ada_w': _jnp.float32, 'ada_b': _jnp.float32, 'norm_w': _jnp.float32, 'even_w_in': _jnp.float32, 'even_w_out': _jnp.float32, 'ret_decay_logit': _jnp.float32, 'conv_dw_w': _jnp.float32, 'conv_ln_w': _jnp.float32, 'conv_ln_b': _jnp.float32, 'odd_w_in': _jnp.float32, 'odd_w_out': _jnp.float32, 'pool_w': _jnp.float32, 'pool_scale': _jnp.float32, 'sg_ln_w': _jnp.float32, 'sg_ln_b': _jnp.float32, 'sg_w': _jnp.float32, 'sg_b': _jnp.float32, 'ffn_w_gate': _jnp.float32, 'ffn_w_up': _jnp.float32, 'ffn_w_down': _jnp.float32, 'final_norm_w': _jnp.float32}
MOMENT_SCALE = {'c_ctx': 2.012575e-02, 'ada_w': 5.326389e-02, 'ada_b': 9.032365e-02, 'norm_w': 4.948944e-02, 'even_w_in': 3.533313e-02, 'even_w_out': 3.420564e-02, 'ret_decay_logit': 1.695773e-01, 'conv_dw_w': 3.384688e-02, 'conv_ln_w': 4.059643e-02, 'conv_ln_b': 3.624005e-02, 'odd_w_in': 3.765840e-02, 'odd_w_out': 4.353587e-02, 'pool_w': 4.560640e-02, 'pool_scale': 4.706844e-02, 'sg_ln_w': 1.791493e-02, 'sg_ln_b': 1.753848e-02, 'sg_w': 3.578766e-02, 'sg_b': 3.699494e-02, 'ffn_w_gate': 2.246150e-02, 'ffn_w_up': 2.173751e-02, 'ffn_w_down': 3.611630e-02, 'final_norm_w': 3.206666e+01}


def _to_microbatches(a, axis):
    t = _jnp.moveaxis(a, axis, 0)
    t = t.reshape((N_MICROBATCH, t.shape[0] // N_MICROBATCH) + t.shape[1:])
    return _jnp.moveaxis(t, 1, axis + 1)


def setup_inputs(seed: int = 0) -> dict:
    inp = _fwd_setup_inputs(seed)
    key = _jax.random.fold_in(_jax.random.key(seed), 7919)
    shape, _ = _output_shape()
    out = dict(inp)
    out["loss_target"] = _jax.random.normal(_jax.random.fold_in(key, 0), shape, _jnp.float32)
    for i, name in enumerate(TWIN_WEIGHTS):
        w = inp[name].astype(_jnp.float32)
        if MOMENT_SCALE is None:
            s = _jnp.sqrt(_jnp.mean(_jnp.square(w)) + 1e-30)
        else:
            s = MOMENT_SCALE[name]
        km, kv = _jax.random.split(_jax.random.fold_in(key, i + 1))
        out[name] = w
        out["m_" + name] = s * _jax.random.normal(km, w.shape, _jnp.float32)
        out["v_" + name] = (s * s) * _jax.random.uniform(kv, w.shape, _jnp.float32, 0.5, 1.5)
    if N_MICROBATCH > 1:
        for name, axis in PER_EXAMPLE_BATCH_AXIS.items():
            out[name] = _to_microbatches(out[name], axis)
    return {'x': out['x'], 'c': out['c'], 'ctx': out['ctx'], 'c_ctx': out['c_ctx'], 'ada_w': out['ada_w'], 'ada_b': out['ada_b'], 'norm_w': out['norm_w'], 'even_w_in': out['even_w_in'], 'even_w_out': out['even_w_out'], 'ret_decay_logit': out['ret_decay_logit'], 'conv_dw_w': out['conv_dw_w'], 'conv_ln_w': out['conv_ln_w'], 'conv_ln_b': out['conv_ln_b'], 'odd_w_in': out['odd_w_in'], 'odd_w_out': out['odd_w_out'], 'pool_w': out['pool_w'], 'pool_scale': out['pool_scale'], 'sg_ln_w': out['sg_ln_w'], 'sg_ln_b': out['sg_ln_b'], 'sg_w': out['sg_w'], 'sg_b': out['sg_b'], 'ffn_w_gate': out['ffn_w_gate'], 'ffn_w_up': out['ffn_w_up'], 'ffn_w_down': out['ffn_w_down'], 'final_norm_w': out['final_norm_w'], 'loss_target': out['loss_target'], 'm_c_ctx': out['m_c_ctx'], 'm_ada_w': out['m_ada_w'], 'm_ada_b': out['m_ada_b'], 'm_norm_w': out['m_norm_w'], 'm_even_w_in': out['m_even_w_in'], 'm_even_w_out': out['m_even_w_out'], 'm_ret_decay_logit': out['m_ret_decay_logit'], 'm_conv_dw_w': out['m_conv_dw_w'], 'm_conv_ln_w': out['m_conv_ln_w'], 'm_conv_ln_b': out['m_conv_ln_b'], 'm_odd_w_in': out['m_odd_w_in'], 'm_odd_w_out': out['m_odd_w_out'], 'm_pool_w': out['m_pool_w'], 'm_pool_scale': out['m_pool_scale'], 'm_sg_ln_w': out['m_sg_ln_w'], 'm_sg_ln_b': out['m_sg_ln_b'], 'm_sg_w': out['m_sg_w'], 'm_sg_b': out['m_sg_b'], 'm_ffn_w_gate': out['m_ffn_w_gate'], 'm_ffn_w_up': out['m_ffn_w_up'], 'm_ffn_w_down': out['m_ffn_w_down'], 'm_final_norm_w': out['m_final_norm_w'], 'v_c_ctx': out['v_c_ctx'], 'v_ada_w': out['v_ada_w'], 'v_ada_b': out['v_ada_b'], 'v_norm_w': out['v_norm_w'], 'v_even_w_in': out['v_even_w_in'], 'v_even_w_out': out['v_even_w_out'], 'v_ret_decay_logit': out['v_ret_decay_logit'], 'v_conv_dw_w': out['v_conv_dw_w'], 'v_conv_ln_w': out['v_conv_ln_w'], 'v_conv_ln_b': out['v_conv_ln_b'], 'v_odd_w_in': out['v_odd_w_in'], 'v_odd_w_out': out['v_odd_w_out'], 'v_pool_w': out['v_pool_w'], 'v_pool_scale': out['v_pool_scale'], 'v_sg_ln_w': out['v_sg_ln_w'], 'v_sg_ln_b': out['v_sg_ln_b'], 'v_sg_w': out['v_sg_w'], 'v_sg_b': out['v_sg_b'], 'v_ffn_w_gate': out['v_ffn_w_gate'], 'v_ffn_w_up': out['v_ffn_w_up'], 'v_ffn_w_down': out['v_ffn_w_down'], 'v_final_norm_w': out['v_final_norm_w']}


def _loss(weights, diff, rest, loss_target):
    with _jax.named_scope("forward"):
        args = {**rest, TWIN_DIFF_INPUT: diff, **{k: w.astype(_WEIGHT_DTYPES[k]) for k, w in weights.items()}}
        y = _forward(args)
    with _jax.named_scope("loss_head"):
        err = _jnp.square(y.astype(_jnp.float32) - loss_target)
        return 0.5 * _jnp.sum(_jnp.mean(err, axis=-1)) if err.ndim else 0.5 * err


def _adamw(w, g, m, v):
    m = ADAM_B1 * m + (1.0 - ADAM_B1) * g
    v = ADAM_B2 * v + (1.0 - ADAM_B2) * _jnp.square(g)
    m_hat = m / (1.0 - ADAM_B1 ** ADAM_STEP)
    v_hat = v / (1.0 - ADAM_B2 ** ADAM_STEP)
    delta = -ADAM_LR * (m_hat / (_jnp.sqrt(v_hat) + ADAM_EPS) + ADAM_WD * w)
    return delta, m, v


def reference(x, c, ctx, c_ctx, ada_w, ada_b, norm_w, even_w_in, even_w_out, ret_decay_logit, conv_dw_w, conv_ln_w, conv_ln_b, odd_w_in, odd_w_out, pool_w, pool_scale, sg_ln_w, sg_ln_b, sg_w, sg_b, ffn_w_gate, ffn_w_up, ffn_w_down, final_norm_w, loss_target, m_c_ctx, m_ada_w, m_ada_b, m_norm_w, m_even_w_in, m_even_w_out, m_ret_decay_logit, m_conv_dw_w, m_conv_ln_w, m_conv_ln_b, m_odd_w_in, m_odd_w_out, m_pool_w, m_pool_scale, m_sg_ln_w, m_sg_ln_b, m_sg_w, m_sg_b, m_ffn_w_gate, m_ffn_w_up, m_ffn_w_down, m_final_norm_w, v_c_ctx, v_ada_w, v_ada_b, v_norm_w, v_even_w_in, v_even_w_out, v_ret_decay_logit, v_conv_dw_w, v_conv_ln_w, v_conv_ln_b, v_odd_w_in, v_odd_w_out, v_pool_w, v_pool_scale, v_sg_ln_w, v_sg_ln_b, v_sg_w, v_sg_b, v_ffn_w_gate, v_ffn_w_up, v_ffn_w_down, v_final_norm_w):
    given = dict(x=x, c=c, ctx=ctx, c_ctx=c_ctx, ada_w=ada_w, ada_b=ada_b, norm_w=norm_w, even_w_in=even_w_in, even_w_out=even_w_out, ret_decay_logit=ret_decay_logit, conv_dw_w=conv_dw_w, conv_ln_w=conv_ln_w, conv_ln_b=conv_ln_b, odd_w_in=odd_w_in, odd_w_out=odd_w_out, pool_w=pool_w, pool_scale=pool_scale, sg_ln_w=sg_ln_w, sg_ln_b=sg_ln_b, sg_w=sg_w, sg_b=sg_b, ffn_w_gate=ffn_w_gate, ffn_w_up=ffn_w_up, ffn_w_down=ffn_w_down, final_norm_w=final_norm_w, loss_target=loss_target, m_c_ctx=m_c_ctx, m_ada_w=m_ada_w, m_ada_b=m_ada_b, m_norm_w=m_norm_w, m_even_w_in=m_even_w_in, m_even_w_out=m_even_w_out, m_ret_decay_logit=m_ret_decay_logit, m_conv_dw_w=m_conv_dw_w, m_conv_ln_w=m_conv_ln_w, m_conv_ln_b=m_conv_ln_b, m_odd_w_in=m_odd_w_in, m_odd_w_out=m_odd_w_out, m_pool_w=m_pool_w, m_pool_scale=m_pool_scale, m_sg_ln_w=m_sg_ln_w, m_sg_ln_b=m_sg_ln_b, m_sg_w=m_sg_w, m_sg_b=m_sg_b, m_ffn_w_gate=m_ffn_w_gate, m_ffn_w_up=m_ffn_w_up, m_ffn_w_down=m_ffn_w_down, m_final_norm_w=m_final_norm_w, v_c_ctx=v_c_ctx, v_ada_w=v_ada_w, v_ada_b=v_ada_b, v_norm_w=v_norm_w, v_even_w_in=v_even_w_in, v_even_w_out=v_even_w_out, v_ret_decay_logit=v_ret_decay_logit, v_conv_dw_w=v_conv_dw_w, v_conv_ln_w=v_conv_ln_w, v_conv_ln_b=v_conv_ln_b, v_odd_w_in=v_odd_w_in, v_odd_w_out=v_odd_w_out, v_pool_w=v_pool_w, v_pool_scale=v_pool_scale, v_sg_ln_w=v_sg_ln_w, v_sg_ln_b=v_sg_ln_b, v_sg_w=v_sg_w, v_sg_b=v_sg_b, v_ffn_w_gate=v_ffn_w_gate, v_ffn_w_up=v_ffn_w_up, v_ffn_w_down=v_ffn_w_down, v_final_norm_w=v_final_norm_w)
    weights = {n: given[n] for n in TWIN_WEIGHTS}
    shared = {n: given[n] for n in SHARED_INPUTS}
    per_example = {n: given[n] for n in ['x', 'c', 'ctx']}
    grad_fn = _jax.value_and_grad(_loss, argnums=(0, 1))

    def one_microbatch(ex, loss_target):
        ex = dict(ex)
        diff = ex.pop(TWIN_DIFF_INPUT)
        return grad_fn(weights, diff, {**shared, **ex}, loss_target)

    if N_MICROBATCH == 1:
        loss, (grad_w, grad_x) = one_microbatch(per_example, given["loss_target"])
    else:
        def body(carry, xs):
            loss_sum, grad_sum = carry
            l_k, (gw_k, gx_k) = one_microbatch(xs[0], xs[1])
            with _jax.named_scope("update"):
                return (loss_sum + l_k, _jax.tree.map(_jnp.add, grad_sum, gw_k)), gx_k

        init = (_jnp.zeros((), _jnp.float32), _jax.tree.map(_jnp.zeros_like, weights))
        (loss, grad_w), grad_x = _jax.lax.scan(body, init, (per_example, given["loss_target"]))
    with _jax.named_scope("update"):
        delta_w, new_m, new_v = {}, {}, {}
        for n in TWIN_WEIGHTS:
            delta_w[n], new_m[n], new_v[n] = _adamw(weights[n], grad_w[n], given["m_" + n], given["v_" + n])
    return (loss, grad_x, *[grad_w[n] for n in TWIN_WEIGHTS], *[delta_w[n] for n in TWIN_WEIGHTS],
            *[new_m[n] for n in TWIN_WEIGHTS], *[new_v[n] for n in TWIN_WEIGHTS])
```

```python
import functools
import math

import jax
import jax.numpy as jnp
from jax import lax
from jax.experimental import pallas as pl
from jax.experimental.pallas import tpu as pltpu

F32 = jnp.float32
BF16 = jnp.bfloat16
MESH = pl.DeviceIdType.MESH

EPS = 1e-6
GRID_W = 64
HEADS = 4
HEAD_DIM = 128
CHUNK = 128
CONV_K = 31
ROPE_BASE = 10000.0
ROPE_PAIRS = (16, 24, 24)
POOL_WINDOWS = (2, 4, 8, 16)
ADAM_LR, ADAM_B1, ADAM_B2, ADAM_EPS, ADAM_WD, ADAM_STEP = 0.001, 0.9, 0.999, 1e-08, 0.01, 10

ROW_TILE = 256
CONV_HALO = 16
POOL_HALO = 8
VMEM_LIMIT = 56 * 1024 * 1024


def _pcall(body, **kw):
    return pl.pallas_call(body, **kw)


def _cp(sem=None, vmem=VMEM_LIMIT):
    if sem is None:
        return pltpu.CompilerParams(vmem_limit_bytes=vmem)
    return pltpu.CompilerParams(dimension_semantics=sem, vmem_limit_bytes=vmem)


def _sds(shape, dtype=F32):
    return jax.ShapeDtypeStruct(tuple(shape), dtype)


def _full(shape):
    nd = len(shape)
    return pl.BlockSpec(tuple(shape), lambda *_: (0,) * nd)


def _sigmoid(x):
    return jax.nn.sigmoid(x)


def _silu(x):
    return x * _sigmoid(x)


def _dsilu(x):
    s = _sigmoid(x)
    return s * (1.0 + x * (1.0 - s))


def _colsum(a):
    return jnp.sum(a, axis=0, keepdims=True)


def _dot(a, b, dn):
    return lax.dot_general(a.astype(BF16), b.astype(BF16), dn, preferred_element_type=F32)


NN = (((1,), (0,)), ((), ()))
NT = (((1,), (1,)), ((), ()))
TN = (((0,), (0,)), ((), ()))


def _mm_tile(t):
    best = 16
    for d in range(16, min(t, 1152) + 1, 16):
        if t % d == 0:
            best = d
    return best


def _mm(name, pairs, grid, out_shape, out_spec, dn, alias=None):
    npairs = len(pairs)
    nk = grid[-1]
    kax = len(grid) - 1

    def body(*refs):
        ins = refs[:2 * npairs]
        o_ref = refs[2 * npairs + (1 if alias is not None else 0)]
        tot = None
        for p in range(npairs):
            d = _dot(ins[2 * p][...], ins[2 * p + 1][...], dn)
            tot = d if tot is None else tot + d
        if nk == 1:
            o_ref[...] = tot.astype(o_ref.dtype)
        else:
            acc = refs[-1]
            k = pl.program_id(kax)

            @pl.when(k == 0)
            def _():
                acc[...] = tot

            @pl.when(k != 0)
            def _():
                acc[...] += tot

            @pl.when(k == nk - 1)
            def _():
                o_ref[...] = acc[...].astype(o_ref.dtype)

    args, in_specs = [], []
    for a, a_spec, b, b_spec in pairs:
        args += [a, b]
        in_specs += [a_spec, b_spec]
    kw = {}
    if alias is not None:
        args.append(alias)
        in_specs.append(pl.BlockSpec(memory_space=pl.ANY))
        kw["input_output_aliases"] = {len(args) - 1: 0}
    blk = tuple(d for d in out_spec.block_shape if d is not None)
    scratch = [pltpu.VMEM(blk, F32)] if nk > 1 else []
    sem = ("parallel",) * kax + ("arbitrary",)
    return _pcall(body, name=name, grid=grid, in_specs=in_specs, out_specs=out_spec, out_shape=out_shape,
                  scratch_shapes=scratch, compiler_params=_cp(sem), **kw)(*args)


def _mm_cols(name, a, w, lyr, out_dtype=F32):
    t, k = a.shape
    j, _, _, n = w.shape
    tm = _mm_tile(t)
    return _mm(name, [(a, pl.BlockSpec((tm, k), lambda i, jj, kk: (i, 0)),
                       w, pl.BlockSpec((None, None, k, n), lambda i, jj, kk: (jj, lyr, 0, 0)))],
               (t // tm, j, 1), _sds((t, j * n), out_dtype), pl.BlockSpec((tm, n), lambda i, jj, kk: (i, jj)), NN)


def _mm_cols_bwd(name, d, w, lyr):
    t = d.shape[0]
    j, _, k, n = w.shape
    tm = _mm_tile(t)
    return _mm(name, [(d, pl.BlockSpec((tm, n), lambda i, u, kk: (i, kk)),
                       w, pl.BlockSpec((None, None, k, n), lambda i, u, kk: (kk, lyr, 0, 0)))],
               (t // tm, 1, j), _sds((t, k)), pl.BlockSpec((tm, k), lambda i, u, kk: (i, 0)), NT)


def _mm_rows(name, a, w, lyr):
    t = a.shape[0]
    j, _, kb, n = w.shape
    tm = _mm_tile(t)
    return _mm(name, [(a, pl.BlockSpec((tm, kb), lambda i, u, kk: (i, kk)),
                       w, pl.BlockSpec((None, None, kb, n), lambda i, u, kk: (kk, lyr, 0, 0)))],
               (t // tm, 1, j), _sds((t, n)), pl.BlockSpec((tm, n), lambda i, u, kk: (i, 0)), NN)


def _mm_rows_bwd(name, d, w, lyr):
    t, n = d.shape
    j, _, kb, _ = w.shape
    tm = _mm_tile(t)
    return _mm(name, [(d, pl.BlockSpec((tm, n), lambda i, jj, kk: (i, 0)),
                       w, pl.BlockSpec((None, None, kb, n), lambda i, jj, kk: (jj, lyr, 0, 0)))],
               (t // tm, j, 1), _sds((t, j * kb)), pl.BlockSpec((tm, kb), lambda i, jj, kk: (i, jj)), NT)


def _wgrad(name, a, a_cols, b, b_cols, g_buf, lyr):
    t = a.shape[0]
    j, _, r, n = g_buf.shape
    tt = _mm_tile(t)
    a_spec = (pl.BlockSpec((tt, a.shape[1]), lambda jj, u, kk: (kk, 0)) if a_cols is None
              else pl.BlockSpec((tt, a_cols), lambda jj, u, kk: (kk, jj)))
    b_spec = (pl.BlockSpec((tt, b.shape[1]), lambda jj, u, kk: (kk, 0)) if b_cols is None
              else pl.BlockSpec((tt, b_cols), lambda jj, u, kk: (kk, jj)))
    return _mm(name, [(a, a_spec, b, b_spec)], (j, 1, t // tt), _sds(g_buf.shape),
               pl.BlockSpec((None, None, r, n), lambda jj, u, kk: (jj, lyr, 0, 0)), TN, alias=g_buf)


def _wgrad3(name, a, a3, b, b3, g_buf, lyr):
    t = a.shape[1] if a3 else a.shape[0]
    j, _, r, n = g_buf.shape
    tt = _mm_tile(t)
    a_spec = (pl.BlockSpec((None, tt, a.shape[2]), lambda jj, u, kk: (jj, kk, 0)) if a3
              else pl.BlockSpec((tt, a.shape[1]), lambda jj, u, kk: (kk, 0)))
    b_spec = (pl.BlockSpec((None, tt, b.shape[2]), lambda jj, u, kk: (jj, kk, 0)) if b3
              else pl.BlockSpec((tt, b.shape[1]), lambda jj, u, kk: (kk, 0)))
    return _mm(name, [(a, a_spec, b, b_spec)], (j, 1, t // tt), _sds(g_buf.shape),
               pl.BlockSpec((None, None, r, n), lambda jj, u, kk: (jj, lyr, 0, 0)), TN, alias=g_buf)


def _ffn_up(name, h, wg, wu, lyr):
    t, k = h.shape
    j, _, _, n = wg.shape
    tm = _mm_tile(t)

    def body(h_ref, wg_ref, wu_ref, a_ref, gt_ref, up_ref):
        hv = h_ref[...]
        gt = _dot(hv, wg_ref[...], NN)
        up = _dot(hv, wu_ref[...], NN)
        a_ref[...] = (_silu(gt) * up).astype(BF16)
        gt_ref[...] = gt.astype(BF16)
        up_ref[...] = up.astype(BF16)

    wspec = pl.BlockSpec((None, None, k, n), lambda i, jj: (jj, lyr, 0, 0))
    ospec = pl.BlockSpec((None, tm, n), lambda i, jj: (jj, i, 0))
    o = _sds((j, t, n), BF16)
    return _pcall(body, name=name, grid=(t // tm, j),
                  in_specs=[pl.BlockSpec((tm, k), lambda i, jj: (i, 0)), wspec, wspec],
                  out_specs=[ospec, ospec, ospec], out_shape=[o, o, o],
                  compiler_params=_cp(("parallel", "parallel")))(h, wg, wu)


def _ffn_down(name, a, wd, lyr):
    j, t, n = a.shape
    dm = wd.shape[3]
    tm = _mm_tile(t)
    return _mm(name, [(a, pl.BlockSpec((None, tm, n), lambda i, u, kk: (kk, i, 0)),
                       wd, pl.BlockSpec((None, None, n, dm), lambda i, u, kk: (kk, lyr, 0, 0)))],
               (t // tm, 1, j), _sds((t, dm)), pl.BlockSpec((tm, dm), lambda i, u, kk: (i, 0)), NN)


def _ffn_down_bwd(name, df, wd, gt, up, lyr):
    t, dm = df.shape
    j, _, n, _ = wd.shape
    tm = _mm_tile(t)

    def body(df_ref, wd_ref, gt_ref, up_ref, dgt_ref, dup_ref):
        da = _dot(df_ref[...], wd_ref[...], NT)
        g = gt_ref[...].astype(F32)
        u = up_ref[...].astype(F32)
        dgt_ref[...] = (da * u * _dsilu(g)).astype(BF16)
        dup_ref[...] = (da * _silu(g)).astype(BF16)

    aspec = pl.BlockSpec((None, tm, n), lambda i, jj: (jj, i, 0))
    o = _sds((j, t, n), BF16)
    return _pcall(body, name=name, grid=(t // tm, j),
                  in_specs=[pl.BlockSpec((tm, dm), lambda i, jj: (i, 0)),
                            pl.BlockSpec((None, None, n, dm), lambda i, jj: (jj, lyr, 0, 0)), aspec, aspec],
                  out_specs=[aspec, aspec], out_shape=[o, o],
                  compiler_params=_cp(("parallel", "parallel")))(df, wd, gt, up)


def _ffn_in_bwd(name, dgt, dup, wg, wu, lyr):
    j, t, n = dgt.shape
    k = wg.shape[2]
    tm = _mm_tile(t)
    aspec = pl.BlockSpec((None, tm, n), lambda i, u, kk: (kk, i, 0))
    wspec = pl.BlockSpec((None, None, k, n), lambda i, u, kk: (kk, lyr, 0, 0))
    return _mm(name, [(dgt, aspec, wg, wspec), (dup, aspec, wu, wspec)], (t // tm, 1, j), _sds((t, k)),
               pl.BlockSpec((tm, k), lambda i, u, kk: (i, 0)), NT)


def _modrow(ref, row, is_ctx):
    return jnp.where(is_ctx, ref[0, row:row + 1, :], ref[1, row:row + 1, :])


def _rnm(name, x, delta, mod_g, g_row, mod_n, sh_row, sc_row, nw, nct):
    t, dm = x.shape
    tm = ROW_TILE
    has = delta is not None

    def body(*refs):
        if has:
            x_ref, d_ref, mg_ref, m_ref, nw_ref, xo_ref, h_ref = refs
        else:
            x_ref, m_ref, nw_ref, h_ref = refs
        is_ctx = pl.program_id(0) < nct
        xv = x_ref[...]
        if has:
            xv = xv + _modrow(mg_ref, g_row, is_ctx) * d_ref[...]
            xo_ref[...] = xv
        r = lax.rsqrt(jnp.mean(xv * xv, axis=-1, keepdims=True) + EPS)
        hv = (xv * r * nw_ref[...]) * (1.0 + _modrow(m_ref, sc_row, is_ctx)) + _modrow(m_ref, sh_row, is_ctx)
        h_ref[...] = hv.astype(BF16)

    row = pl.BlockSpec((tm, dm), lambda i: (i, 0))
    ins = [x] + ([delta, mod_g] if has else []) + [mod_n, nw]
    in_specs = [row] + ([row, _full(mod_g.shape)] if has else []) + [_full(mod_n.shape), _full(nw.shape)]
    outs = ([_sds((t, dm))] if has else []) + [_sds((t, dm), BF16)]
    out_specs = ([row] if has else []) + [row]
    res = _pcall(body, name=name, grid=(t // tm,), in_specs=in_specs, out_specs=out_specs, out_shape=outs,
                 compiler_params=_cp(("parallel",)))(*ins)
    return res if has else (None, res[0])


def _bnm(name, xn, dh, dup, yprev, mod_n, sh_row, sc_row, mod_g, g_row, nw, nct):
    t, dm = xn.shape
    tm = ROW_TILE
    has = yprev is not None

    def body(*refs):
        if has:
            x_ref, dh_ref, du_ref, y_ref, mn_ref, mg_ref, nw_ref, dx_ref, dd_ref, s_ref = refs
        else:
            x_ref, dh_ref, du_ref, mn_ref, nw_ref, dx_ref, s_ref = refs
        i = pl.program_id(0)
        is_ctx = i < nct

        @pl.when(i == 0)
        def _():
            s_ref[...] = jnp.zeros_like(s_ref)

        xv = x_ref[...]
        r = lax.rsqrt(jnp.mean(xv * xv, axis=-1, keepdims=True) + EPS)
        xh = xv * r
        w = nw_ref[...]
        sc1 = 1.0 + _modrow(mn_ref, sc_row, is_ctx)
        dhv = dh_ref[...]
        dxh = dhv * sc1 * w
        dx = r * (dxh - xh * jnp.mean(dxh * xh, axis=-1, keepdims=True)) + du_ref[...]
        dx_ref[...] = dx
        parts = [_colsum(dhv), _colsum(dhv * (xh * w))]
        if has:
            dd_ref[...] = (_modrow(mg_ref, g_row, is_ctx) * dx).astype(BF16)
            parts.append(_colsum(dx * y_ref[...]))
        else:
            parts.append(jnp.zeros((1, dm), F32))
        upd = jnp.concatenate(parts + [jnp.zeros((5, dm), F32)], axis=0)
        dnw = jnp.concatenate([jnp.zeros((3, dm), F32), _colsum(dhv * sc1 * xh), jnp.zeros((4, dm), F32)], axis=0)

        @pl.when(is_ctx)
        def _():
            s_ref[0] += upd
            s_ref[1] += dnw

        @pl.when(jnp.logical_not(is_ctx))
        def _():
            s_ref[1] += upd + dnw

    row = pl.BlockSpec((tm, dm), lambda i: (i, 0))
    ins = [xn, dh, dup] + ([yprev] if has else []) + [mod_n] + ([mod_g] if has else []) + [nw]
    in_specs = ([row, row, row] + ([row] if has else []) + [_full(mod_n.shape)]
                + ([_full(mod_g.shape)] if has else []) + [_full(nw.shape)])
    outs = [_sds((t, dm))] + ([_sds((t, dm), BF16)] if has else []) + [_sds((2, 8, dm))]
    out_specs = [row] + ([row] if has else []) + [_full((2, 8, dm))]
    res = _pcall(body, name=name, grid=(t // tm,), in_specs=in_specs, out_specs=out_specs, out_shape=outs,
                 compiler_params=_cp(("arbitrary",)))(*ins)
    return res if has else (res[0], None, res[1])


def _fin(name, x1, f, mod, g_row, fw, target, nct):
    t, dm = x1.shape
    tm = ROW_TILE

    def body(x_ref, f_ref, m_ref, fw_ref, t_ref, loss_ref, dx_ref, dd_ref, s_ref):
        i = pl.program_id(0)

        @pl.when(i == 0)
        def _():
            s_ref[...] = jnp.zeros_like(s_ref)
            loss_ref[...] = jnp.zeros_like(loss_ref)

        @pl.when(i < nct)
        def _():
            dx_ref[...] = jnp.zeros_like(dx_ref)
            dd_ref[...] = jnp.zeros_like(dd_ref)

        @pl.when(i >= nct)
        def _():
            g = m_ref[1, g_row:g_row + 1, :]
            fv = f_ref[...]
            xv = x_ref[...] + g * fv
            r = lax.rsqrt(jnp.mean(xv * xv, axis=-1, keepdims=True) + EPS)
            xh = xv * r
            w = fw_ref[...]
            err = xh * w - t_ref[...]
            loss_ref[...] += 0.5 * jnp.sum(err * err) / dm
            dout = err * (1.0 / dm)
            dxh = dout * w
            dx = r * (dxh - xh * jnp.mean(dxh * xh, axis=-1, keepdims=True))
            dx_ref[...] = dx
            dd_ref[...] = (g * dx).astype(BF16)
            s_ref[...] += jnp.concatenate([_colsum(dx * fv), _colsum(dout * xh), jnp.zeros((6, dm), F32)], axis=0)

    row = pl.BlockSpec((tm, dm), lambda i: (i, 0))
    trow = pl.BlockSpec((tm, dm), lambda i: (jnp.maximum(i - nct, 0), 0))
    return _pcall(body, name=name, grid=(t // tm,),
                  in_specs=[row, row, _full(mod.shape), _full(fw.shape), trow],
                  out_specs=[_full((8, 128)), row, row, _full((8, dm))],
                  out_shape=[_sds((8, 128)), _sds((t, dm)), _sds((t, dm), BF16), _sds((8, dm))],
                  compiler_params=_cp(("arbitrary",)))(x1, f, mod, fw, target)


def _rope_tables(t, lc):
    l = t - lc
    rows = l // GRID_W
    grid_r = jnp.broadcast_to(jnp.arange(rows, dtype=F32)[:, None], (rows, GRID_W)).reshape(-1)
    grid_c = jnp.broadcast_to(jnp.arange(GRID_W, dtype=F32)[None, :], (rows, GRID_W)).reshape(-1)

    def angles(p_seq, p_row, p_col):
        parts = []
        for p, n in zip((p_seq, p_row, p_col), ROPE_PAIRS):
            freq = ROPE_BASE ** (-jnp.arange(n, dtype=F32) / n)
            parts.append(p[:, None] * freq[None, :])
        return jnp.concatenate(parts, axis=-1)

    zc = jnp.zeros((lc,), F32)
    ang = jnp.concatenate([angles(jnp.arange(lc, dtype=F32), zc, zc),
                           angles(jnp.full((l,), lc, F32), grid_r, grid_c)], axis=0)
    cos, sin = jnp.cos(ang), jnp.sin(ang)
    return jnp.concatenate([cos, cos], axis=-1), jnp.concatenate([-sin, sin], axis=-1)


def _rope(u, cs, sn):
    return u * cs + pltpu.roll(u, HEAD_DIM // 2, 1) * sn


def _rope_t(d, cs, sn):
    return d * cs + pltpu.roll(d * sn, HEAD_DIM // 2, 1)


def _even_qkv(name, p, cs, sn):
    t = p.shape[0]
    tm = ROW_TILE
    w = HEADS * HEAD_DIM
    scale = HEAD_DIM ** -0.5

    def body(q_ref, k_ref, v_ref, cs_ref, sn_ref, qo_ref, ko_ref, vo_ref):
        c, s = cs_ref[...], sn_ref[...]
        for h in range(HEADS):
            sl = slice(h * HEAD_DIM, (h + 1) * HEAD_DIM)
            qo_ref[:, sl] = (_rope(q_ref[:, sl], c, s) * scale).astype(BF16)
            ko_ref[:, sl] = _rope(k_ref[:, sl], c, s).astype(BF16)
        vo_ref[...] = v_ref[...].astype(BF16)

    col = lambda j: pl.BlockSpec((tm, w), lambda i: (i, j))
    tab = pl.BlockSpec((tm, HEAD_DIM), lambda i: (i, 0))
    o = _sds((t, w), BF16)
    return _pcall(body, name=name, grid=(t // tm,), in_specs=[col(0), col(1), col(2), tab, tab],
                  out_specs=[col(0)] * 3, out_shape=[o, o, o], compiler_params=_cp(("parallel",)))(p, p, p, cs, sn)


def _log_sigmoid_row(x):
    e = jnp.exp(-jnp.abs(x))
    l1p = jnp.where(e < 0.01, e * (1.0 - e * (0.5 - e * (1.0 / 3.0))), jnp.log(1.0 + e))
    return jnp.minimum(x, 0.0) - l1p


def _ret_tables(lgb_ref, dm_ref, xi_ref, zt_ref):
    ri = lax.broadcasted_iota(jnp.int32, (CHUNK, CHUNK), 0).astype(F32)
    ci = lax.broadcasted_iota(jnp.int32, (CHUNK, CHUNK), 1).astype(F32)
    for d in range(2):
        for h in range(HEADS):
            idx = d * HEADS + h
            lg = _log_sigmoid_row(lgb_ref[idx:idx + 1, :])
            if d == 0:
                e, mask = ri - ci, ri >= ci
                xe, ze = ri + 1.0, (CHUNK - 1.0) - ri
            else:
                e, mask = ci - ri - 1.0, ci > ri
                xe, ze = (CHUNK - 1.0) - ri, ri
            dm_ref[idx] = jnp.where(mask, jnp.exp(lg * jnp.where(mask, e, 0.0)), 0.0)
            xi_ref[idx] = jnp.exp(lg * xe)
            zt_ref[idx] = jnp.exp(lg * ze)


def _ret_exponents(d):
    ri = lax.broadcasted_iota(jnp.int32, (CHUNK, CHUNK), 0).astype(F32)
    ci = lax.broadcasted_iota(jnp.int32, (CHUNK, CHUNK), 1).astype(F32)
    if d == 0:
        return ri - ci, ri + 1.0, (CHUNK - 1.0) - ri
    return ci - ri - 1.0, (CHUNK - 1.0) - ri, ri


def _bwd_chunk(n, ncc, nc):
    return jnp.where(n < ncc, ncc - 1 - n, nc - 1 - (n - ncc))


def _retention_fwd(name, q, k, v, lgb, lc):
    t, w = q.shape
    nc, ncc = t // CHUNK, lc // CHUNK
    nh = 2 * HEADS

    def body(qf_ref, kf_ref, vf_ref, qb_ref, kb_ref, vb_ref, lgb_ref, of_ref, ob_ref, ss_ref,
             s_ref, dm_ref, xi_ref, zt_ref):
        n = pl.program_id(0)

        @pl.when(n == 0)
        def _():
            s_ref[...] = jnp.zeros_like(s_ref)
            _ret_tables(lgb_ref, dm_ref, xi_ref, zt_ref)

        for d in range(2):
            q_ref, k_ref, v_ref, o_ref = (qf_ref, kf_ref, vf_ref, of_ref) if d == 0 else (qb_ref, kb_ref, vb_ref, ob_ref)
            for h in range(HEADS):
                idx = d * HEADS + h
                sl = slice(h * HEAD_DIM, (h + 1) * HEAD_DIM)
                qv, kv, vv = q_ref[:, sl], k_ref[:, sl], v_ref[:, sl]
                s = s_ref[idx]
                ss_ref[idx] = s
                a = _dot(qv, kv, NT) * dm_ref[idx]
                o = _dot(a, vv, NN) + _dot(qv.astype(F32) * xi_ref[idx], s, NN)
                o_ref[:, sl] = o
                gc = jnp.exp(_log_sigmoid_row(lgb_ref[idx:idx + 1, :]) * float(CHUNK))
                s_ref[idx] = gc * s + _dot(kv.astype(F32) * zt_ref[idx], vv, TN)

    fspec = pl.BlockSpec((CHUNK, w), lambda n: (n, 0))
    bspec = pl.BlockSpec((CHUNK, w), lambda n: (_bwd_chunk(n, ncc, nc), 0))
    tab = pltpu.VMEM((nh, CHUNK, CHUNK), F32)
    return _pcall(body, name=name, grid=(nc,),
                  in_specs=[fspec] * 3 + [bspec] * 3 + [_full((nh, HEAD_DIM))],
                  out_specs=[fspec, bspec, pl.BlockSpec((None, nh, CHUNK, CHUNK), lambda n: (n, 0, 0, 0))],
                  out_shape=[_sds((t, w)), _sds((t, w)), _sds((nc, nh, CHUNK, CHUNK))],
                  scratch_shapes=[tab, tab, tab, tab],
                  compiler_params=_cp(("arbitrary",)))(q, k, v, q, k, v, lgb)


def _retention_bwd(name, q, k, v, do, ss, lgb, lc):
    t, w = q.shape
    nc, ncc = t // CHUNK, lc // CHUNK
    nh = 2 * HEADS

    def body(qf_ref, kf_ref, vf_ref, gf_ref, qb_ref, kb_ref, vb_ref, gb_ref, ss_ref, lgb_ref,
             dqf_ref, dkf_ref, dvf_ref, dqb_ref, dkb_ref, dvb_ref, dl_ref,
             ds_ref, dm_ref, xi_ref, zt_ref, acc_ref):
        n = pl.program_id(0)

        @pl.when(n == 0)
        def _():
            ds_ref[...] = jnp.zeros_like(ds_ref)
            acc_ref[...] = jnp.zeros_like(acc_ref)
            _ret_tables(lgb_ref, dm_ref, xi_ref, zt_ref)

        for d in range(2):
            if d == 0:
                q_ref, k_ref, v_ref, g_ref, dq_ref, dk_ref, dv_ref = qf_ref, kf_ref, vf_ref, gf_ref, dqf_ref, dkf_ref, dvf_ref
            else:
                q_ref, k_ref, v_ref, g_ref, dq_ref, dk_ref, dv_ref = qb_ref, kb_ref, vb_ref, gb_ref, dqb_ref, dkb_ref, dvb_ref
            ee, xe, ze = _ret_exponents(d)
            for h in range(HEADS):
                idx = d * HEADS + h
                sl = slice(h * HEAD_DIM, (h + 1) * HEAD_DIM)
                qv, kv, vv, gv = q_ref[:, sl], k_ref[:, sl], v_ref[:, sl], g_ref[:, sl]
                s = ss_ref[idx]
                dsp = ds_ref[idx]
                dmat, xi, zt = dm_ref[idx], xi_ref[idx], zt_ref[idx]
                qf32, kf32 = qv.astype(F32), kv.astype(F32)
                a = _dot(qv, kv, NT) * dmat
                dar = _dot(gv, vv, NT)
                da = dar * dmat
                t1 = _dot(gv, s, NT)
                t2 = _dot(vv, dsp, NT)
                dq_ref[:, sl] = _dot(da, kv, NN) + xi * t1
                dk_ref[:, sl] = _dot(da, qv, TN) + zt * t2
                dv_ref[:, sl] = _dot(a, gv, TN) + _dot(kf32 * zt, dsp, NN)
                gc = jnp.exp(_log_sigmoid_row(lgb_ref[idx:idx + 1, :]) * float(CHUNK))
                ds_ref[idx] = gc * dsp + _dot(qf32 * xi, gv, TN)
                acc_ref[idx] += (ee * a * dar + xe * xi * qf32 * t1 + ze * zt * kf32 * t2
                                 + (float(CHUNK) * gc) * dsp * s)

        @pl.when(n == nc - 1)
        def _():
            for idx in range(nh):
                tot = jnp.sum(acc_ref[idx])
                dl_ref[idx:idx + 1, :] = tot * _sigmoid(-lgb_ref[idx:idx + 1, :])

    fmap = lambda n: (nc - 1 - n, 0)
    bmap = lambda n: (_bwd_chunk(nc - 1 - n, ncc, nc), 0)
    fspec = pl.BlockSpec((CHUNK, w), fmap)
    bspec = pl.BlockSpec((CHUNK, w), bmap)
    tab = pltpu.VMEM((nh, CHUNK, CHUNK), F32)
    o = _sds((t, w))
    return _pcall(body, name=name, grid=(nc,),
                  in_specs=[fspec] * 4 + [bspec] * 4
                  + [pl.BlockSpec((None, nh, CHUNK, CHUNK), lambda n: (nc - 1 - n, 0, 0, 0)), _full((nh, HEAD_DIM))],
                  out_specs=[fspec] * 3 + [bspec] * 3 + [_full((nh, HEAD_DIM))],
                  out_shape=[o] * 6 + [_sds((nh, HEAD_DIM))],
                  scratch_shapes=[tab, tab, tab, tab, tab],
                  compiler_params=_cp(("arbitrary",)))(q, k, v, do, q, k, v, do, ss, lgb)


def _halo_specs(tm, halo, t, width, col):
    hb = tm // halo
    last = t // halo - 1
    prev = pl.BlockSpec((halo, width), lambda i: (jnp.maximum(i * hb - 1, 0), col))
    nxt = pl.BlockSpec((halo, width), lambda i: (jnp.minimum((i + 1) * hb, last), col))
    return prev, nxt


def _halo_valid(i, nct, nt):
    vp = jnp.logical_and(i != 0, i != nct)
    vn = jnp.logical_and(i != nct - 1, i != nt - 1)
    return vp, vn


def _fill_window(win_ref, prev, cur, nxt, vp, vn, halo, tm):
    win_ref[0:halo, :] = jnp.where(vp, prev, 0.0)
    win_ref[halo:halo + tm, :] = cur
    win_ref[halo + tm:halo + tm + halo, :] = jnp.where(vn, nxt, 0.0)


CONV_SUB = 64


def _conv_taps(win_ref, w_ref, tm, flip):
    outs = []
    for r0 in range(0, tm, CONV_SUB):
        acc = None
        for kk in range(CONV_K):
            wk = (CONV_K - 1 - kk) if flip else kk
            term = w_ref[wk:wk + 1, :] * win_ref[r0 + kk + 1:r0 + kk + 1 + CONV_SUB, :]
            acc = term if acc is None else acc + term
        outs.append(acc)
    return jnp.concatenate(outs, axis=0)


def _head_norm(y):
    r = lax.rsqrt(jnp.mean(y * y, axis=-1, keepdims=True) + EPS)
    return y * r, r


def _ln_stats(y):
    mu = jnp.mean(y, axis=-1, keepdims=True)
    yc = y - mu
    rs = lax.rsqrt(jnp.mean(yc * yc, axis=-1, keepdims=True) + EPS)
    return yc * rs, rs


def _ln_bwd(dyh, yh, rs):
    return rs * (dyh - jnp.mean(dyh, axis=-1, keepdims=True) - yh * jnp.mean(dyh * yh, axis=-1, keepdims=True))


def _even_mix(name, p, of, ob, cw, lnw, lnb, nct):
    t = p.shape[0]
    tm, halo = ROW_TILE, CONV_HALO
    nt = t // tm
    w = HEADS * HEAD_DIM

    def body(g_ref, a_ref, gb_ref, ap_ref, gbp_ref, an_ref, gbn_ref, of_ref, ob_ref, cw_ref, lw_ref, lb_ref,
             mix_ref, yc_ref, win_ref):
        i = pl.program_id(0)
        vp, vn = _halo_valid(i, nct, nt)
        glu = lambda a, b: a * _sigmoid(b)
        _fill_window(win_ref, glu(ap_ref[...], gbp_ref[...]), glu(a_ref[...], gb_ref[...]),
                     glu(an_ref[...], gbn_ref[...]), vp, vn, halo, tm)
        yc = _conv_taps(win_ref, cw_ref, tm, False)
        yc_ref[...] = yc
        yh, _ = _ln_stats(yc)
        mix_ref[:, w:2 * w] = _silu(yh * lw_ref[...] + lb_ref[...]).astype(BF16)
        for h in range(HEADS):
            sl = slice(h * HEAD_DIM, (h + 1) * HEAD_DIM)
            yn, _ = _head_norm(of_ref[:, sl] + ob_ref[:, sl])
            mix_ref[:, sl] = (_silu(g_ref[:, sl]) * yn).astype(BF16)

    col = lambda j: pl.BlockSpec((tm, w), lambda i: (i, j))
    ap, an = _halo_specs(tm, halo, t, w, 4)
    gp, gn = _halo_specs(tm, halo, t, w, 5)
    row = pl.BlockSpec((tm, w), lambda i: (i, 0))
    return _pcall(body, name=name, grid=(nt,),
                  in_specs=[col(3), col(4), col(5), ap, gp, an, gn, row, row,
                            _full(cw.shape), _full(lnw.shape), _full(lnb.shape)],
                  out_specs=[pl.BlockSpec((tm, 2 * w), lambda i: (i, 0)), row],
                  out_shape=[_sds((t, 2 * w), BF16), _sds((t, w))],
                  scratch_shapes=[pltpu.VMEM((tm + 2 * halo, w), F32)],
                  compiler_params=_cp(("parallel",)))(p, p, p, p, p, p, p, of, ob, cw, lnw, lnb)


def _even_mix_bwd1(name, dmix, p, of, ob, yc, lnw, lnb):
    t = p.shape[0]
    tm = ROW_TILE
    w = HEADS * HEAD_DIM

    def body(dr_ref, dc_ref, g_ref, of_ref, ob_ref, yc_ref, lw_ref, lb_ref, do_ref, dg_ref, dyc_ref, s_ref):
        @pl.when(pl.program_id(0) == 0)
        def _():
            s_ref[...] = jnp.zeros_like(s_ref)

        for h in range(HEADS):
            sl = slice(h * HEAD_DIM, (h + 1) * HEAD_DIM)
            yn, r = _head_norm(of_ref[:, sl] + ob_ref[:, sl])
            gv = g_ref[:, sl]
            dr = dr_ref[:, sl]
            dg_ref[:, sl] = (dr * yn * _dsilu(gv)).astype(BF16)
            dyn = dr * _silu(gv)
            do_ref[:, sl] = (r * (dyn - yn * jnp.mean(dyn * yn, axis=-1, keepdims=True))).astype(BF16)
        yh, rs = _ln_stats(yc_ref[...])
        lw = lw_ref[...]
        dlo = dc_ref[...] * _dsilu(yh * lw + lb_ref[...])
        dyc_ref[...] = _ln_bwd(dlo * lw, yh, rs)
        s_ref[...] += jnp.concatenate([_colsum(dlo * yh), _colsum(dlo), jnp.zeros((6, w), F32)], axis=0)

    col = lambda j: pl.BlockSpec((tm, w), lambda i: (i, j))
    row = pl.BlockSpec((tm, w), lambda i: (i, 0))
    return _pcall(body, name=name, grid=(t // tm,),
                  in_specs=[col(0), col(1), col(3), row, row, row, _full(lnw.shape), _full(lnb.shape)],
                  out_specs=[row, row, row, _full((8, w))],
                  out_shape=[_sds((t, w), BF16), _sds((t, w), BF16), _sds((t, w)), _sds((8, w))],
                  compiler_params=_cp(("arbitrary",)))(dmix, dmix, p, of, ob, yc, lnw, lnb)


def _even_conv_bwd(name, dyc, p, cw, nct):
    t = p.shape[0]
    tm, halo = ROW_TILE, CONV_HALO
    nt = t // tm
    w = HEADS * HEAD_DIM

    def body(d_ref, dp_ref, dn_ref, a_ref, gb_ref, ap_ref, gbp_ref, an_ref, gbn_ref, cw_ref,
             da_ref, dgb_ref, dw_ref, dwin_ref, uwin_ref):
        i = pl.program_id(0)

        @pl.when(i == 0)
        def _():
            dw_ref[...] = jnp.zeros_like(dw_ref)

        vp, vn = _halo_valid(i, nct, nt)
        glu = lambda a, b: a * _sigmoid(b)
        dcur = d_ref[...]
        _fill_window(dwin_ref, dp_ref[...], dcur, dn_ref[...], vp, vn, halo, tm)
        _fill_window(uwin_ref, glu(ap_ref[...], gbp_ref[...]), glu(a_ref[...], gb_ref[...]),
                     glu(an_ref[...], gbn_ref[...]), vp, vn, halo, tm)
        du = _conv_taps(dwin_ref, cw_ref, tm, True)
        av = a_ref[...]
        sg = _sigmoid(gb_ref[...])
        da_ref[...] = (du * sg).astype(BF16)
        dgb_ref[...] = (du * av * sg * (1.0 - sg)).astype(BF16)
        rows = [_colsum(dcur * uwin_ref[kk + 1:kk + 1 + tm, :]) for kk in range(CONV_K)]
        dw_ref[...] += jnp.concatenate(rows + [jnp.zeros((1, w), F32)], axis=0)

    col = lambda j: pl.BlockSpec((tm, w), lambda i: (i, j))
    row = pl.BlockSpec((tm, w), lambda i: (i, 0))
    dp, dn = _halo_specs(tm, halo, t, w, 0)
    ap, an = _halo_specs(tm, halo, t, w, 4)
    gp, gn = _halo_specs(tm, halo, t, w, 5)
    win = pltpu.VMEM((tm + 2 * halo, w), F32)
    return _pcall(body, name=name, grid=(nt,),
                  in_specs=[row, dp, dn, col(4), col(5), ap, gp, an, gn, _full(cw.shape)],
                  out_specs=[row, row, _full((CONV_K + 1, w))],
                  out_shape=[_sds((t, w), BF16), _sds((t, w), BF16), _sds((CONV_K + 1, w))],
                  scratch_shapes=[win, win],
                  compiler_params=_cp(("arbitrary",)))(dyc, dyc, dyc, p, p, p, p, p, p, cw)


def _even_dp(name, dqs, dks, dvs, dg, da, dgb, cs, sn):
    t, w = dg.shape
    tm = ROW_TILE
    scale = HEAD_DIM ** -0.5

    def body(dqf_ref, dqb_ref, dkf_ref, dkb_ref, dvf_ref, dvb_ref, dg_ref, da_ref, dgb_ref, cs_ref, sn_ref, dp_ref):
        c, s = cs_ref[...], sn_ref[...]
        for h in range(HEADS):
            sl = slice(h * HEAD_DIM, (h + 1) * HEAD_DIM)
            dp_ref[:, sl] = (_rope_t(dqf_ref[:, sl] + dqb_ref[:, sl], c, s) * scale).astype(BF16)
            dp_ref[:, w + h * HEAD_DIM:w + (h + 1) * HEAD_DIM] = _rope_t(dkf_ref[:, sl] + dkb_ref[:, sl], c, s).astype(BF16)
        dp_ref[:, 2 * w:3 * w] = (dvf_ref[...] + dvb_ref[...]).astype(BF16)
        dp_ref[:, 3 * w:4 * w] = dg_ref[...]
        dp_ref[:, 4 * w:5 * w] = da_ref[...]
        dp_ref[:, 5 * w:6 * w] = dgb_ref[...]

    row = pl.BlockSpec((tm, w), lambda i: (i, 0))
    tab = pl.BlockSpec((tm, HEAD_DIM), lambda i: (i, 0))
    return _pcall(body, name=name, grid=(t // tm,), in_specs=[row] * 9 + [tab, tab],
                  out_specs=pl.BlockSpec((tm, 6 * w), lambda i: (i, 0)), out_shape=_sds((t, 6 * w), BF16),
                  compiler_params=_cp(("parallel",)))(dqs[0], dqs[1], dks[0], dks[1], dvs[0], dvs[1], dg, da, dgb, cs, sn)


GROUPS = 4
GC = 128
INV_SQRT2 = 0.7071067811865476
INV_SQRT_2PI = 0.3989422804014327


def _gelu(x):
    return 0.5 * x * (1.0 + lax.erf(x * INV_SQRT2))


def _dgelu(x):
    return 0.5 * (1.0 + lax.erf(x * INV_SQRT2)) + x * jnp.exp(-0.5 * x * x) * INV_SQRT_2PI


def _pool_count(i, nct, lc, t, tm, rows, row0, left, right):
    is_ctx = i < nct
    seg_start = jnp.where(is_ctx, 0, lc)
    seg_len = jnp.where(is_ctx, lc, t - lc)
    pos = i * tm + row0 - seg_start + lax.broadcasted_iota(jnp.int32, (rows, GC), 0)
    cnt = jnp.minimum(pos + right, seg_len - 1) - jnp.maximum(pos - left, 0) + 1
    return jnp.maximum(cnt, 1).astype(F32)


def _spatial_gate(vln, sgw_ref, sgb_ref, tm):
    cols = []
    for g in range(GROUPS):
        sl = slice(g * GC, (g + 1) * GC)
        parts = [_dot(sgw_ref[g], vln[r0:r0 + CHUNK, sl], NN) + sgb_ref[g] for r0 in range(0, tm, CHUNK)]
        cols.append(jnp.concatenate(parts, axis=0))
    return jnp.concatenate(cols, axis=1)


def _odd_mix(name, p, pw, pscale, lnw, lnb, sgw, sgb, nct, lc):
    t = p.shape[0]
    tm, halo = ROW_TILE, POOL_HALO
    nt = t // tm
    w = GROUPS * GC

    def body(pc_ref, pp_ref, pn_ref, pu_ref, pv_ref, pw_ref, ps_ref, lw_ref, lb_ref, sgw_ref, sgb_ref,
             mix_ref, m_ref, win_ref):
        i = pl.program_id(0)
        vp, vn = _halo_valid(i, nct, nt)
        pc = pc_ref[...]
        _fill_window(win_ref, pp_ref[...], pc, pn_ref[...], vp, vn, halo, tm)
        for g, wd in enumerate(POOL_WINDOWS):
            sl = slice(g * GC, (g + 1) * GC)
            left = wd // 2
            right = wd - 1 - left
            s = None
            for o in range(-left, right + 1):
                term = win_ref[halo + o:halo + o + tm, sl]
                s = term if s is None else s + term
            mg = s / _pool_count(i, nct, lc, t, tm, tm, 0, left, right) - pc[:, sl]
            m_ref[:, sl] = mg
            mix_ref[:, sl] = (_dot(mg, pw_ref[g], NN) * ps_ref[:, sl]).astype(BF16)
        u = _gelu(pu_ref[...])
        vh, _ = _ln_stats(_gelu(pv_ref[...]))
        s = _spatial_gate(vh * lw_ref[...] + lb_ref[...], sgw_ref, sgb_ref, tm)
        mix_ref[:, w:2 * w] = (u * s).astype(BF16)

    col = lambda j: pl.BlockSpec((tm, w), lambda i: (i, j))
    pp, pn = _halo_specs(tm, halo, t, w, 0)
    return _pcall(body, name=name, grid=(nt,),
                  in_specs=[col(0), pp, pn, col(1), col(2), _full(pw.shape), _full(pscale.shape),
                            _full(lnw.shape), _full(lnb.shape), _full(sgw.shape), _full(sgb.shape)],
                  out_specs=[pl.BlockSpec((tm, 2 * w), lambda i: (i, 0)), col(0)],
                  out_shape=[_sds((t, 2 * w), BF16), _sds((t, w))],
                  scratch_shapes=[pltpu.VMEM((tm + 2 * halo, w), F32)],
                  compiler_params=_cp(("parallel",)))(p, p, p, p, p, pw, pscale, lnw, lnb, sgw, sgb)


def _odd_mix_bwd1(name, dmix, p, m, pw, pscale, lnw, lnb, sgw, sgb):
    t = p.shape[0]
    tm = ROW_TILE
    w = GROUPS * GC

    def body(dpo_ref, dsg_ref, pu_ref, pv_ref, m_ref, pw_ref, ps_ref, lw_ref, lb_ref, sgw_ref, sgb_ref,
             dm_ref, dpd_ref, vec_ref, dpw_ref, dsgw_ref, dsgb_ref):
        @pl.when(pl.program_id(0) == 0)
        def _():
            vec_ref[...] = jnp.zeros_like(vec_ref)
            dpw_ref[...] = jnp.zeros_like(dpw_ref)
            dsgw_ref[...] = jnp.zeros_like(dsgw_ref)
            dsgb_ref[...] = jnp.zeros_like(dsgb_ref)

        dscale = []
        for g in range(GROUPS):
            sl = slice(g * GC, (g + 1) * GC)
            mg = m_ref[:, sl]
            dpo = dpo_ref[:, sl]
            dscale.append(_colsum(dpo * _dot(mg, pw_ref[g], NN)))
            dpo = dpo * ps_ref[:, sl]
            dm_ref[:, sl] = _dot(dpo, pw_ref[g], NT)
            dpw_ref[g] += _dot(mg, dpo, TN)
        pu, pv = pu_ref[...], pv_ref[...]
        u = _gelu(pu)
        vh, rs = _ln_stats(_gelu(pv))
        lw = lw_ref[...]
        vln = vh * lw + lb_ref[...]
        s = _spatial_gate(vln, sgw_ref, sgb_ref, tm)
        dsg = dsg_ref[...]
        dpd_ref[:, 0:w] = (dsg * s * _dgelu(pu)).astype(BF16)
        ds = dsg * u
        cols = []
        for g in range(GROUPS):
            sl = slice(g * GC, (g + 1) * GC)
            parts = []
            for r0 in range(0, tm, CHUNK):
                dsc = ds[r0:r0 + CHUNK, sl]
                parts.append(_dot(sgw_ref[g], dsc, TN))
                dsgw_ref[g] += _dot(dsc, vln[r0:r0 + CHUNK, sl], NT)
                dsgb_ref[g] += dsc
            cols.append(jnp.concatenate(parts, axis=0))
        dvln = jnp.concatenate(cols, axis=1)
        dpd_ref[:, w:2 * w] = (_ln_bwd(dvln * lw, vh, rs) * _dgelu(pv)).astype(BF16)
        vec_ref[...] += jnp.concatenate([jnp.concatenate(dscale, axis=1), _colsum(dvln * vh), _colsum(dvln),
                                         jnp.zeros((5, w), F32)], axis=0)

        @pl.when(pl.program_id(0) == t // tm - 1)
        def _():
            for g in range(GROUPS):
                dsgb_ref[g] = jnp.broadcast_to(jnp.sum(dsgb_ref[g], axis=1, keepdims=True), (GC, GC))

    col = lambda j: pl.BlockSpec((tm, w), lambda i: (i, j))
    mat = _full((GROUPS, GC, GC))
    return _pcall(body, name=name, grid=(t // tm,),
                  in_specs=[col(0), col(1), col(1), col(2), col(0), _full(pw.shape), _full(pscale.shape),
                            _full(lnw.shape), _full(lnb.shape), _full(sgw.shape), _full(sgb.shape)],
                  out_specs=[col(0), pl.BlockSpec((tm, 2 * w), lambda i: (i, 0)), _full((8, w)), mat, mat, mat],
                  out_shape=[_sds((t, w)), _sds((t, 2 * w), BF16), _sds((8, w)),
                             _sds((GROUPS, GC, GC)), _sds((GROUPS, GC, GC)), _sds((GROUPS, GC, GC))],
                  compiler_params=_cp(("arbitrary",)))(dmix, dmix, p, p, m, pw, pscale, lnw, lnb, sgw, sgb)


def _odd_dp(name, dm, dpd, nct, lc):
    t, w = dm.shape
    tm, halo = ROW_TILE, POOL_HALO
    nt = t // tm

    def body(d_ref, dp_ref, dn_ref, dpd_ref, o_ref, win_ref):
        i = pl.program_id(0)
        vp, vn = _halo_valid(i, nct, nt)
        dcur = d_ref[...]
        _fill_window(win_ref, dp_ref[...], dcur, dn_ref[...], vp, vn, halo, tm)
        for g, wd in enumerate(POOL_WINDOWS):
            sl = slice(g * GC, (g + 1) * GC)
            left = wd // 2
            right = wd - 1 - left
            win_ref[:, sl] = win_ref[:, sl] / _pool_count(i, nct, lc, t, tm, tm + 2 * halo, -halo, left, right)
            s = None
            for o in range(-right, left + 1):
                term = win_ref[halo + o:halo + o + tm, sl]
                s = term if s is None else s + term
            o_ref[:, sl] = (s - dcur[:, sl]).astype(BF16)
        o_ref[:, w:3 * w] = dpd_ref[...]

    row = pl.BlockSpec((tm, w), lambda i: (i, 0))
    pp, pn = _halo_specs(tm, halo, t, w, 0)
    return _pcall(body, name=name, grid=(nt,),
                  in_specs=[row, pp, pn, pl.BlockSpec((tm, 2 * w), lambda i: (i, 0))],
                  out_specs=pl.BlockSpec((tm, 3 * w), lambda i: (i, 0)), out_shape=_sds((t, 3 * w), BF16),
                  scratch_shapes=[pltpu.VMEM((tm + 2 * halo, w), F32)],
                  compiler_params=_cp(("parallel",)))(dm, dm, dm, dpd)


def _place():
    x, y, c = lax.axis_index("x"), lax.axis_index("y"), lax.axis_index("c")
    chips = [(1 - x, y), (x, 1 - y), (1 - x, 1 - y)]
    return x, y, c, chips


def _chip_index(cx, cy):
    return 2 * cx + cy


def _all_gather8(name, blk):
    m_per, n = blk.shape

    def body(x_ref, out_ref, send_sems, recv_sems, local_sem):
        x, y, c, chips = _place()
        me, sibling = (x, y, c), (x, y, 1 - c)

        def rows(px, py, pc):
            return out_ref.at[pl.ds((4 * px + 2 * py + pc) * m_per, m_per), :]

        def copy(k, block, to, src=None):
            return pltpu.make_async_remote_copy(
                src_ref=rows(*block) if src is None else src, dst_ref=rows(*block),
                send_sem=send_sems.at[k], recv_sem=recv_sems.at[k], device_id=to, device_id_type=MESH)

        mine = pltpu.make_async_copy(x_ref, rows(*me), local_sem)
        mine.start()
        first = [copy(0, me, sibling, src=x_ref)]
        first += [copy(1 + j, me, (*chip, c), src=x_ref) for j, chip in enumerate(chips)]
        for cp in first:
            cp.start()
        passed = [copy(4 + j, (*chip, c), sibling) for j, chip in enumerate(chips)]
        for j, chip in enumerate(chips):
            copy(1 + j, (*chip, c), me).wait_recv()
            passed[j].start()
        copy(0, sibling, me).wait_recv()
        for j, chip in enumerate(chips):
            copy(4 + j, (*chip, 1 - c), me).wait_recv()
        for cp in first + passed:
            cp.wait_send()
        mine.wait()

    return _pcall(body, name=name, out_shape=_sds((8 * m_per, n), blk.dtype),
                  in_specs=[pl.BlockSpec(memory_space=pltpu.VMEM)], out_specs=pl.BlockSpec(memory_space=pltpu.VMEM),
                  scratch_shapes=[pltpu.SemaphoreType.DMA((7,)), pltpu.SemaphoreType.DMA((7,)), pltpu.SemaphoreType.DMA],
                  compiler_params=_cp())(blk)


ANY = pl.BlockSpec(memory_space=pl.ANY)


def _gather_weights(name, ws):
    nw = len(ws)

    def body(*refs):
        w_refs, o_refs = refs[:nw], refs[nw:2 * nw]
        send_sems, recv_sems, local_sems = refs[2 * nw:]
        x, y, c, chips = _place()
        me_chip = _chip_index(x, y)
        sibling = (x, y, 1 - c)
        local, sends = [], []
        for t in range(nw):
            lh = w_refs[t].shape[0] // 2
            cp = pltpu.make_async_copy(w_refs[t], o_refs[t].at[me_chip], local_sems.at[t])
            cp.start()
            local.append(cp)
            for k, chip in enumerate(chips):
                cp = pltpu.make_async_remote_copy(
                    src_ref=w_refs[t].at[pl.ds(c * lh, lh)], dst_ref=o_refs[t].at[me_chip, pl.ds(c * lh, lh)],
                    send_sem=send_sems.at[t * 6 + k], recv_sem=recv_sems.at[t * 6 + k],
                    device_id=(*chip, c), device_id_type=MESH)
                cp.start()
                sends.append(cp)
        for t in range(nw):
            lh = w_refs[t].shape[0] // 2
            for k, chip in enumerate(chips):
                part = o_refs[t].at[_chip_index(*chip), pl.ds(c * lh, lh)]
                pltpu.make_async_remote_copy(
                    src_ref=part, dst_ref=part, send_sem=send_sems.at[t * 6 + k], recv_sem=recv_sems.at[t * 6 + k],
                    device_id=(*chip, c), device_id_type=MESH).wait_recv()
                cp = pltpu.make_async_remote_copy(
                    src_ref=part, dst_ref=part, send_sem=send_sems.at[t * 6 + 3 + k],
                    recv_sem=recv_sems.at[t * 6 + 3 + k], device_id=sibling, device_id_type=MESH)
                cp.start()
                sends.append(cp)
        for t in range(nw):
            lh = w_refs[t].shape[0] // 2
            for k, chip in enumerate(chips):
                part = o_refs[t].at[_chip_index(*chip), pl.ds((1 - c) * lh, lh)]
                pltpu.make_async_remote_copy(
                    src_ref=part, dst_ref=part, send_sem=send_sems.at[t * 6 + 3 + k],
                    recv_sem=recv_sems.at[t * 6 + 3 + k], device_id=sibling, device_id_type=MESH).wait_recv()
        for cp in sends:
            cp.wait_send()
        for cp in local:
            cp.wait()

    return _pcall(body, name=name, out_shape=[_sds((4,) + w.shape, w.dtype) for w in ws],
                  in_specs=[ANY] * nw, out_specs=[ANY] * nw,
                  scratch_shapes=[pltpu.SemaphoreType.DMA((6 * nw,)), pltpu.SemaphoreType.DMA((6 * nw,)),
                                  pltpu.SemaphoreType.DMA((nw,))],
                  compiler_params=_cp())(*ws)


def _rs_pair(name, gs):
    ng = len(gs)

    def body(*refs):
        g_refs, o_refs = refs[:ng], refs[ng:2 * ng]
        send_sems, recv_sems = refs[2 * ng:]
        x, y, c, _ = _place()
        cps = []
        for t in range(ng):
            lh = g_refs[t].shape[1] // 2
            cp = pltpu.make_async_remote_copy(
                src_ref=g_refs[t].at[:, pl.ds((1 - c) * lh, lh)], dst_ref=o_refs[t],
                send_sem=send_sems.at[t], recv_sem=recv_sems.at[t], device_id=(x, y, 1 - c), device_id_type=MESH)
            cp.start()
            cps.append(cp)
        for cp in cps:
            cp.wait()

    outs = [_sds((g.shape[0], g.shape[1] // 2) + g.shape[2:], g.dtype) for g in gs]
    return _pcall(body, name=name, out_shape=outs, in_specs=[ANY] * ng, out_specs=[ANY] * ng,
                  scratch_shapes=[pltpu.SemaphoreType.DMA((ng,)), pltpu.SemaphoreType.DMA((ng,))],
                  compiler_params=_cp())(*gs)


def _rs_chips(name, ps):
    ng = len(ps)

    def body(*refs):
        p_refs, o_refs = refs[:ng], refs[ng:2 * ng]
        send_sems, recv_sems, local_sems = refs[2 * ng:]
        x, y, c, chips = _place()
        cps = []
        for t in range(ng):
            cp = pltpu.make_async_copy(p_refs[t].at[_chip_index(x, y)], o_refs[t].at[3], local_sems.at[t])
            cp.start()
            cps.append(cp)
            for k, chip in enumerate(chips):
                cp = pltpu.make_async_remote_copy(
                    src_ref=p_refs[t].at[_chip_index(*chip)], dst_ref=o_refs[t].at[k],
                    send_sem=send_sems.at[t * 3 + k], recv_sem=recv_sems.at[t * 3 + k],
                    device_id=(*chip, c), device_id_type=MESH)
                cp.start()
                cps.append(cp)
        for cp in cps:
            cp.wait()

    return _pcall(body, name=name, out_shape=[_sds(p.shape, p.dtype) for p in ps],
                  in_specs=[ANY] * ng, out_specs=[ANY] * ng,
                  scratch_shapes=[pltpu.SemaphoreType.DMA((3 * ng,)), pltpu.SemaphoreType.DMA((3 * ng,)),
                                  pltpu.SemaphoreType.DMA((ng,))],
                  compiler_params=_cp())(*ps)


def _rs_share(name, rs):
    ng = len(rs)

    def body(*refs):
        r_refs, o_refs = refs[:ng], refs[ng:2 * ng]
        send_sems, recv_sems, local_sems = refs[2 * ng:]
        x, y, c, _ = _place()
        cps = []
        for t in range(ng):
            lh = r_refs[t].shape[0]
            mine = o_refs[t].at[pl.ds(c * lh, lh)]
            cp = pltpu.make_async_copy(r_refs[t], mine, local_sems.at[t])
            cp.start()
            cps.append(cp)
            cp = pltpu.make_async_remote_copy(
                src_ref=r_refs[t], dst_ref=mine, send_sem=send_sems.at[t], recv_sem=recv_sems.at[t],
                device_id=(x, y, 1 - c), device_id_type=MESH)
            cp.start()
            cps.append(cp)
        for cp in cps:
            cp.wait()

    outs = [_sds((2 * r.shape[0],) + r.shape[1:], r.dtype) for r in rs]
    return _pcall(body, name=name, out_shape=outs, in_specs=[ANY] * ng, out_specs=[ANY] * ng,
                  scratch_shapes=[pltpu.SemaphoreType.DMA((ng,)), pltpu.SemaphoreType.DMA((ng,)),
                                  pltpu.SemaphoreType.DMA((ng,))],
                  compiler_params=_cp())(*rs)


def _row_block(r, cn):
    if r % 8:
        return r
    best = 8
    for d in range(8, r + 1, 8):
        if r % d == 0 and d * cn * 4 <= (1 << 20):
            best = d
    return best


def _add_half(name, g, a, c_arr):
    j, lh, r, cn = a.shape
    tr = _row_block(r, cn)

    def body(c_ref, g_ref, a_ref, o_ref):
        o_ref[...] = g_ref[...] + a_ref[...]

    blk = (None, None, tr, cn)
    gs = pltpu.PrefetchScalarGridSpec(
        num_scalar_prefetch=1, grid=(j, lh, r // tr),
        in_specs=[pl.BlockSpec(blk, lambda jj, l, i, c_ref: (jj, c_ref[0] * lh + l, i, 0)),
                  pl.BlockSpec(blk, lambda jj, l, i, c_ref: (jj, l, i, 0))],
        out_specs=pl.BlockSpec(blk, lambda jj, l, i, c_ref: (jj, l, i, 0)))
    return _pcall(body, name=name, grid_spec=gs, out_shape=_sds(a.shape),
                  compiler_params=_cp(("parallel", "parallel", "parallel")))(c_arr, g, a)


def _sum_slots(name, b):
    _, lh, r, cn = b.shape
    tr = _row_block(r, cn)

    def body(b_ref, o_ref):
        o_ref[...] = (b_ref[0] + b_ref[1]) + (b_ref[2] + b_ref[3])

    return _pcall(body, name=name, grid=(lh, r // tr),
                  in_specs=[pl.BlockSpec((4, None, tr, cn), lambda l, i: (0, l, i, 0))],
                  out_specs=pl.BlockSpec((None, tr, cn), lambda l, i: (l, i, 0)), out_shape=_sds((lh, r, cn)),
                  compiler_params=_cp(("parallel", "parallel")))(b)


def _sum8(name, g):
    _, r, n = g.shape
    tr = 8

    def body(g_ref, o_ref):
        o_ref[...] = ((g_ref[0] + g_ref[1]) + (g_ref[2] + g_ref[3])) + ((g_ref[4] + g_ref[5]) + (g_ref[6] + g_ref[7]))

    return _pcall(body, name=name, grid=(r // tr,), in_specs=[pl.BlockSpec((8, tr, n), lambda i: (0, i, 0))],
                  out_specs=pl.BlockSpec((tr, n), lambda i: (i, 0)), out_shape=_sds((r, n)),
                  compiler_params=_cp(("parallel",)))(g)


def _ada_mod(name, c16, ada_w, bias):
    nl, dm, n = ada_w.shape

    def body(c_ref, w_ref, b_ref, o_ref):
        o_ref[...] = _dot(_silu(c_ref[...]), w_ref[...], NN) + b_ref[...]

    return _pcall(body, name=name, grid=(nl,),
                  in_specs=[_full(c16.shape), pl.BlockSpec((None, dm, n), lambda i: (i, 0, 0)),
                            pl.BlockSpec((None, 1, n), lambda i: (i, 0, 0))],
                  out_specs=pl.BlockSpec((None, 16, n), lambda i: (i, 0, 0)), out_shape=_sds((nl, 16, n)),
                  compiler_params=_cp(("parallel",)))(c16, ada_w, bias)


def _ada_bwd(name, c16, dmod, ada_w):
    nl, dm, n = ada_w.shape

    def body(c_ref, d_ref, w_ref, gw_ref, dc_ref):
        @pl.when(pl.program_id(0) == 0)
        def _():
            dc_ref[...] = jnp.zeros_like(dc_ref)

        dv = d_ref[...]
        gw_ref[...] = _dot(_silu(c_ref[...]), dv, TN)
        dc_ref[...] += _dot(dv, w_ref[...], NT)

    return _pcall(body, name=name, grid=(nl,),
                  in_specs=[_full(c16.shape), pl.BlockSpec((None, 16, n), lambda i: (i, 0, 0)),
                            pl.BlockSpec((None, dm, n), lambda i: (i, 0, 0))],
                  out_specs=[pl.BlockSpec((None, dm, n), lambda i: (i, 0, 0)), _full((16, dm))],
                  out_shape=[_sds((nl, dm, n)), _sds((16, dm))],
                  compiler_params=_cp(("arbitrary",)))(c16, dmod, ada_w)


def _rowsum16(name, dmod):
    nl, _, n = dmod.shape

    def body(d_ref, o_ref):
        o_ref[...] = _colsum(d_ref[...])

    return _pcall(body, name=name, grid=(nl,), in_specs=[pl.BlockSpec((None, 16, n), lambda i: (i, 0, 0))],
                  out_specs=pl.BlockSpec((None, 1, n), lambda i: (i, 0, 0)), out_shape=_sds((nl, 1, n)),
                  compiler_params=_cp(("parallel",)))(dmod)


def _cctx_grad(name, parts, c_ctx):
    def body(p_ref, c_ref, o_ref):
        tot = (p_ref[0:1, :] + p_ref[1:2, :]) + (p_ref[2:3, :] + p_ref[3:4, :])
        o_ref[...] = tot * _dsilu(c_ref[...])

    return _pcall(body, name=name, out_shape=_sds(c_ctx.shape), compiler_params=_cp())(parts, c_ctx)


def _adamw(name, w, g, m, v):
    shape = w.shape
    cn = shape[-1]
    r = math.prod(shape[:-1]) if len(shape) > 1 else 1
    tr = _row_block(r, cn)
    c1 = 1.0 - ADAM_B1 ** ADAM_STEP
    c2 = 1.0 - ADAM_B2 ** ADAM_STEP

    def body(w_ref, g_ref, m_ref, v_ref, d_ref, mo_ref, vo_ref):
        gv = g_ref[...]
        mn = ADAM_B1 * m_ref[...] + (1.0 - ADAM_B1) * gv
        vn = ADAM_B2 * v_ref[...] + (1.0 - ADAM_B2) * (gv * gv)
        d_ref[...] = -ADAM_LR * ((mn / c1) / (jnp.sqrt(vn / c2) + ADAM_EPS) + ADAM_WD * w_ref[...])
        mo_ref[...] = mn
        vo_ref[...] = vn

    blk = pl.BlockSpec((tr, cn), lambda i: (i, 0))
    o = _sds((r, cn))
    outs = _pcall(body, name=name, grid=(r // tr,), in_specs=[blk] * 4, out_specs=[blk] * 3, out_shape=[o, o, o],
                  compiler_params=_cp(("parallel",)))(*[a.reshape(r, cn) for a in (w, g, m, v)])
    return tuple(a.reshape(shape) for a in outs)


def _local_step(xs, target, modt, nw, fnw, wts, ev, od, lc):
    t, dm = xs.shape
    nct = lc // ROW_TILE
    depth = nw.shape[0]
    cs, sn = _rope_tables(t, lc)
    saved = []
    x_in, x1p, fp = xs, None, None
    for i in range(depth):
        j, even = i // 2, i % 2 == 0
        tag = f"l{i}"
        if i == 0:
            _, h = _rnm(tag + "_norm1", x_in, None, None, 0, modt[0], 0, 1, nw[0, 0], nct)
        else:
            x_in, h = _rnm(tag + "_norm1", x1p, fp, modt[i - 1], 5, modt[i], 0, 1, nw[i, 0], nct)
        s = dict(x=x_in, h=h)
        if even:
            p = _mm_cols(tag + "_in", h, wts["even_in"], j)
            q, k, v = _even_qkv(tag + "_qkv", p, cs, sn)
            of, ob, ss = _retention_fwd(tag + "_ret", q, k, v, ev["lgb"][j], lc)
            mix, yc = _even_mix(tag + "_mix", p, of, ob, ev["cw"][j], ev["lnw"][j], ev["lnb"][j], nct)
            y = _mm_rows(tag + "_out", mix, wts["even_out"], j)
            s.update(p=p, q=q, k=k, v=v, of=of, ob=ob, ss=ss, yc=yc)
        else:
            p = _mm_cols(tag + "_in", h, wts["odd_in"], j)
            mix, m = _odd_mix(tag + "_mix", p, od["pw"][j], od["ps"][j], od["lnw"][j], od["lnb"][j],
                              od["sgw"][j], od["sgb"][j], nct, lc)
            y = _mm_rows(tag + "_out", mix, wts["odd_out"], j)
            s.update(p=p, m=m)
        x1, h2 = _rnm(tag + "_norm2", x_in, y, modt[i], 2, modt[i], 3, 4, nw[i, 1], nct)
        a, gt, up = _ffn_up(tag + "_ffn_up", h2, wts["gate"], wts["up"], i)
        f = _ffn_down(tag + "_ffn_down", a, wts["down"], i)
        s.update(mix=mix, y=y, x1=x1, h2=h2, a=a, gt=gt, up=up, f=f)
        saved.append(s)
        x1p, fp = x1, f

    loss_blk, dx, df, fin_s = _fin("final", x1p, fp, modt[depth - 1], 5, fnw, target, nct)

    gb = {k: jnp.zeros(w.shape, F32) for k, w in wts.items()}
    dmod = [[None] * 6 for _ in range(depth)]
    dnw = [[None, None] for _ in range(depth)]
    zero2 = jnp.zeros((2, dm), F32)
    dmod[depth - 1][5] = jnp.stack([zero2[0], fin_s[0]])
    small = dict(dfnw=fin_s[1], ev=[], od=[])
    for i in reversed(range(depth)):
        j, even = i // 2, i % 2 == 0
        tag = f"l{i}b"
        s = saved[i]
        dgt, dup = _ffn_down_bwd(tag + "_ffn_down", df, wts["down"], s["gt"], s["up"], i)
        gb["down"] = _wgrad3(tag + "_gdown", s["a"], True, df, False, gb["down"], i)
        gb["gate"] = _wgrad3(tag + "_ggate", s["h2"], False, dgt, True, gb["gate"], i)
        gb["up"] = _wgrad3(tag + "_gup", s["h2"], False, dup, True, gb["up"], i)
        dh2 = _ffn_in_bwd(tag + "_ffn_in", dgt, dup, wts["gate"], wts["up"], i)
        dx1, dy, s2 = _bnm(tag + "_norm2", s["x1"], dh2, dx, s["y"], modt[i], 3, 4, modt[i], 2, nw[i, 1], nct)
        dmod[i][3], dmod[i][4], dmod[i][2] = s2[:, 0], s2[:, 1], s2[:, 2]
        dnw[i][1] = s2[1, 3]
        wo, wi = ("even_out", "even_in") if even else ("odd_out", "odd_in")
        dmix = _mm_rows_bwd(tag + "_out", dy, wts[wo], j)
        gb[wo] = _wgrad(tag + "_gout", s["mix"], wts[wo].shape[2], dy, None, gb[wo], j)
        if even:
            do, dg, dyc, lns = _even_mix_bwd1(tag + "_mix1", dmix, s["p"], s["of"], s["ob"], s["yc"],
                                              ev["lnw"][j], ev["lnb"][j])
            da, dgb, dcw = _even_conv_bwd(tag + "_conv", dyc, s["p"], ev["cw"][j], nct)
            dqf, dkf, dvf, dqb, dkb, dvb, dl = _retention_bwd(tag + "_ret", s["q"], s["k"], s["v"], do, s["ss"],
                                                              ev["lgb"][j], lc)
            dp = _even_dp(tag + "_dp", (dqf, dqb), (dkf, dkb), (dvf, dvb), dg, da, dgb, cs, sn)
            small["ev"].append(dict(lnw=lns[0], lnb=lns[1], cw=dcw, dl=dl[:, 0]))
        else:
            dm_, dpd, vec, dpw, dsgw, dsgb = _odd_mix_bwd1(tag + "_mix1", dmix, s["p"], s["m"], od["pw"][j], od["ps"][j],
                                                           od["lnw"][j], od["lnb"][j], od["sgw"][j], od["sgb"][j])
            dp = _odd_dp(tag + "_dp", dm_, dpd, nct, lc)
            small["od"].append(dict(ps=vec[0], lnw=vec[1], lnb=vec[2], pw=dpw, sgw=dsgw, sgb=dsgb[:, :, 0]))
        dh = _mm_cols_bwd(tag + "_in", dp, wts[wi], j)
        gb[wi] = _wgrad(tag + "_gin", s["h"], None, dp, wts[wi].shape[3], gb[wi], j)
        if i > 0:
            dx, df, s1 = _bnm(tag + "_norm1", s["x"], dh, dx1, saved[i - 1]["f"], modt[i], 0, 1, modt[i - 1], 5,
                              nw[i, 0], nct)
            dmod[i - 1][5] = s1[:, 2]
        else:
            dx, _, s1 = _bnm(tag + "_norm1", s["x"], dh, dx1, None, modt[0], 0, 1, None, 0, nw[0, 0], nct)
        dmod[i][0], dmod[i][1] = s1[:, 0], s1[:, 1]
        dnw[i][0] = s1[1, 3]
    small["ev"].reverse()
    small["od"].reverse()
    dmod_t = jnp.stack([jnp.concatenate([jnp.stack(rows, axis=1), jnp.zeros((2, 2, dm), F32)], axis=1) for rows in dmod])
    small["dmod"] = dmod_t
    small["dnw"] = jnp.stack([jnp.stack(r) for r in dnw])
    return loss_blk, dx, gb, small


WEIGHTS = ["c_ctx", "ada_w", "ada_b", "norm_w", "even_w_in", "even_w_out", "ret_decay_logit", "conv_dw_w",
           "conv_ln_w", "conv_ln_b", "odd_w_in", "odd_w_out", "pool_w", "pool_scale", "sg_ln_w", "sg_ln_b",
           "sg_w", "sg_b", "ffn_w_gate", "ffn_w_up", "ffn_w_down", "final_norm_w"]
BIG = dict(even_in="even_w_in", even_out="even_w_out", odd_in="odd_w_in", odd_out="odd_w_out",
           gate="ffn_w_gate", up="ffn_w_up", down="ffn_w_down")


def _rows(a, width=1024):
    flat = a.reshape(-1)
    n = flat.shape[0]
    per = 8 * width
    tot = -(-n // per) * per
    return jnp.pad(flat, (0, tot - n)).reshape(tot // width, width)


def _unshard(parts, lead):
    nl = len(lead)
    perm = tuple(range(1, nl + 1)) + (0, nl + 1)
    return parts.transpose(perm).reshape(tuple(lead) + (4 * parts.shape[-1],))


def _my_cols(a, chip, n):
    start = (0,) * (a.ndim - 1) + (chip * n,)
    return lax.dynamic_slice(a, start, a.shape[:-1] + (n,))


def kernel(x, c, ctx, c_ctx, ada_w, ada_b, norm_w, even_w_in, even_w_out, ret_decay_logit, conv_dw_w, conv_ln_w, conv_ln_b, odd_w_in, odd_w_out, pool_w, pool_scale, sg_ln_w, sg_ln_b, sg_w, sg_b, ffn_w_gate, ffn_w_up, ffn_w_down, final_norm_w, loss_target, m_c_ctx, m_ada_w, m_ada_b, m_norm_w, m_even_w_in, m_even_w_out, m_ret_decay_logit, m_conv_dw_w, m_conv_ln_w, m_conv_ln_b, m_odd_w_in, m_odd_w_out, m_pool_w, m_pool_scale, m_sg_ln_w, m_sg_ln_b, m_sg_w, m_sg_b, m_ffn_w_gate, m_ffn_w_up, m_ffn_w_down, m_final_norm_w, v_c_ctx, v_ada_w, v_ada_b, v_norm_w, v_even_w_in, v_even_w_out, v_ret_decay_logit, v_conv_dw_w, v_conv_ln_w, v_conv_ln_b, v_odd_w_in, v_odd_w_out, v_pool_w, v_pool_scale, v_sg_ln_w, v_sg_ln_b, v_sg_w, v_sg_b, v_ffn_w_gate, v_ffn_w_up, v_ffn_w_down, v_final_norm_w):
    wv = dict(c_ctx=c_ctx, ada_w=ada_w, ada_b=ada_b, norm_w=norm_w, even_w_in=even_w_in, even_w_out=even_w_out,
              ret_decay_logit=ret_decay_logit, conv_dw_w=conv_dw_w, conv_ln_w=conv_ln_w, conv_ln_b=conv_ln_b,
              odd_w_in=odd_w_in, odd_w_out=odd_w_out, pool_w=pool_w, pool_scale=pool_scale, sg_ln_w=sg_ln_w,
              sg_ln_b=sg_ln_b, sg_w=sg_w, sg_b=sg_b, ffn_w_gate=ffn_w_gate, ffn_w_up=ffn_w_up,
              ffn_w_down=ffn_w_down, final_norm_w=final_norm_w)
    mv = dict(zip(WEIGHTS, (m_c_ctx, m_ada_w, m_ada_b, m_norm_w, m_even_w_in, m_even_w_out, m_ret_decay_logit,
                            m_conv_dw_w, m_conv_ln_w, m_conv_ln_b, m_odd_w_in, m_odd_w_out, m_pool_w, m_pool_scale,
                            m_sg_ln_w, m_sg_ln_b, m_sg_w, m_sg_b, m_ffn_w_gate, m_ffn_w_up, m_ffn_w_down,
                            m_final_norm_w)))
    vv = dict(zip(WEIGHTS, (v_c_ctx, v_ada_w, v_ada_b, v_norm_w, v_even_w_in, v_even_w_out, v_ret_decay_logit,
                            v_conv_dw_w, v_conv_ln_w, v_conv_ln_b, v_odd_w_in, v_odd_w_out, v_pool_w, v_pool_scale,
                            v_sg_ln_w, v_sg_ln_b, v_sg_w, v_sg_b, v_ffn_w_gate, v_ffn_w_up, v_ffn_w_down,
                            v_final_norm_w)))
    xi, yi, ci = lax.axis_index("x"), lax.axis_index("y"), lax.axis_index("c")
    chip = 2 * xi + yi
    dev = 4 * xi + 2 * yi + ci
    dm = x.shape[-1]
    lc = ctx.shape[1]
    depth = ada_w.shape[0]
    n_ada = ada_w.shape[-1]

    cw_pad = jnp.pad(conv_dw_w, ((0, 0), (0, 1), (0, 0)))
    vec3 = jnp.stack([pool_scale, sg_ln_w, sg_ln_b])
    pack1 = jnp.concatenate([_rows(c), _rows(norm_w), _rows(cw_pad), _rows(vec3)], axis=0)
    g1 = _all_gather8("gather_small", pack1).reshape(8, 32, dm)
    c_all = g1[:, 0]
    per_chip = g1[0::2]
    norm_full = _unshard(per_chip[:, 8:10].reshape(4, depth, 2, dm // 4), (depth, 2))
    cw_full = _unshard(per_chip[:, 16:24].reshape(4, 2, CONV_K + 1, 128), (2, CONV_K + 1))
    vec_full = _unshard(per_chip[:, 24, :768].reshape(4, 3, 2, 128), (3, 2))

    c16 = jnp.concatenate([c_all, c_ctx[None, :], jnp.zeros((7, dm), F32)], axis=0)
    mod_sh = _ada_mod("ada_mod", c16, ada_w, _my_cols(ada_b, chip, n_ada)[:, None, :])
    g2 = _all_gather8("gather_mod", mod_sh.reshape(depth * 16, n_ada)).reshape(8, depth, 16, n_ada)
    mod_full = _unshard(g2[0::2], (depth, 16))
    mod_x = lax.dynamic_index_in_dim(mod_full, dev, axis=1, keepdims=False).reshape(depth, 6, dm)
    mod_c = mod_full[:, 8].reshape(depth, 6, dm)
    modt = jnp.pad(jnp.stack([mod_c, mod_x], axis=1), ((0, 0), (0, 0), (0, 2), (0, 0)))

    names = list(BIG)
    gathered = _gather_weights("gather_weights", [wv[BIG[k]].astype(BF16) for k in names])
    wts = dict(zip(names, gathered))

    ev = dict(lgb=jnp.broadcast_to(ret_decay_logit.reshape(-1, 2 * HEADS)[:, :, None], (ret_decay_logit.shape[0], 2 * HEADS, HEAD_DIM)),
              cw=cw_full, lnw=conv_ln_w[:, None, :], lnb=conv_ln_b[:, None, :])
    od = dict(pw=pool_w, ps=vec_full[0][:, None, :], lnw=vec_full[1][:, None, :], lnb=vec_full[2][:, None, :],
              sgw=sg_w, sgb=jnp.broadcast_to(sg_b[:, :, :, None], sg_b.shape + (GC,)))
    xs = jnp.concatenate([ctx[0], x[0]], axis=0)
    loss_blk, dxs, gb, small = _local_step(xs, loss_target[0], modt, norm_full[:, :, None, :], final_norm_w[None, :],
                                           wts, ev, od, lc)

    misc = jnp.stack([
        small["dfnw"], jnp.broadcast_to(loss_blk[0, 0], (dm,)),
        jnp.concatenate([e["lnw"] for e in small["ev"]]), jnp.concatenate([e["lnb"] for e in small["ev"]]),
        jnp.concatenate([o["ps"] for o in small["od"]]), jnp.concatenate([o["lnw"] for o in small["od"]]),
        jnp.concatenate([o["lnb"] for o in small["od"]]),
        jnp.pad(jnp.concatenate([e["dl"] for e in small["ev"]]), (0, dm - 4 * HEADS)),
        jnp.stack([o["sgb"] for o in small["od"]]).reshape(-1)])
    pack2 = jnp.concatenate([
        _rows(small["dmod"]), _rows(small["dnw"]), _rows(misc), _rows(jnp.stack([e["cw"] for e in small["ev"]])),
        _rows(jnp.stack([o["pw"] for o in small["od"]])), _rows(jnp.stack([o["sgw"] for o in small["od"]]))], axis=0)
    n2 = pack2.shape[0]
    g3 = _all_gather8("gather_grads", pack2)
    tot = _sum8("sum_grads", g3.reshape(8, n2, dm))
    r_mod = depth * 16
    o_nw, o_misc = r_mod, r_mod + 8
    o_cw = o_misc + 16
    o_pw = o_cw + 2 * (CONV_K + 1) // 2
    o_sgw = o_pw + 128
    dmod_sum = tot[:r_mod].reshape(depth, 2, 8, dm)
    dmod_dev = g3.reshape(8, n2, dm)[:, :r_mod].reshape(8, depth, 2, 8, dm)
    dm_x = dmod_dev[:, :, 1, :6].reshape(8, depth, 6 * dm).transpose(1, 0, 2)
    dm_c = dmod_sum[:, 0, :6].reshape(depth, 1, 6 * dm)
    dmod16 = jnp.concatenate([dm_x, dm_c, jnp.zeros((depth, 7, 6 * dm), F32)], axis=1)
    g_ada_b = _rowsum16("ada_b_grad", dmod16)[:, 0]
    g_ada_w, dc16 = _ada_bwd("ada_bwd", c16, _my_cols(dmod16, chip, n_ada), ada_w)
    g4 = _all_gather8("gather_cctx", dc16[8:16]).reshape(8, 8, dm)
    g_c_ctx = _cctx_grad("cctx_grad", g4[0::2, 0], c_ctx[None, :])[0]

    misc_t = tot[o_misc:o_misc + 16]
    half = lambda row: misc_t[row].reshape(2, dm // 2)
    grads = dict(
        c_ctx=g_c_ctx, ada_w=g_ada_w, ada_b=g_ada_b,
        norm_w=_my_cols(tot[o_nw:o_nw + 8].reshape(depth, 2, dm), chip, dm // 4),
        ret_decay_logit=misc_t[7, :4 * HEADS].reshape(ret_decay_logit.shape),
        conv_dw_w=_my_cols(tot[o_cw:o_cw + 2 * (CONV_K + 1) // 2].reshape(2, CONV_K + 1, dm // 2)[:, :CONV_K], chip, 128),
        conv_ln_w=half(2), conv_ln_b=half(3),
        pool_w=tot[o_pw:o_pw + 128].reshape(pool_w.shape),
        pool_scale=_my_cols(half(4), chip, 128), sg_ln_w=_my_cols(half(5), chip, 128), sg_ln_b=_my_cols(half(6), chip, 128),
        sg_w=tot[o_sgw:o_sgw + 128].reshape(sg_w.shape), sg_b=misc_t[8].reshape(sg_b.shape),
        final_norm_w=misc_t[0])
    loss = misc_t[1, 0]

    glist = [gb[k] for k in names]
    c_arr = ci.astype(jnp.int32).reshape(1)
    from_sib = _rs_pair("rs_pair", glist)
    pair = [_add_half(f"rs_add_{k}", g, a, c_arr) for k, g, a in zip(names, glist, from_sib)]
    slots = _rs_chips("rs_chips", pair)
    halves = [_sum_slots(f"rs_sum_{k}", b) for k, b in zip(names, slots)]
    shards = _rs_share("rs_share", halves)
    for k, sh in zip(names, shards):
        grads[BIG[k]] = sh

    deltas, new_m, new_v = {}, {}, {}
    for n in WEIGHTS:
        deltas[n], new_m[n], new_v[n] = _adamw("adamw_" + n, wv[n], grads[n], mv[n], vv[n])
    grad_x = dxs[lc:][None]
    return (loss, grad_x, *[grads[n] for n in WEIGHTS], *[deltas[n] for n in WEIGHTS],
            *[new_m[n] for n in WEIGHTS], *[new_v[n] for n in WEIGHTS])
```

```python
import functools
import math

import jax
import jax.numpy as jnp
from jax import lax
from jax.experimental import pallas as pl
from jax.experimental.pallas import tpu as pltpu

F32 = jnp.float32
BF16 = jnp.bfloat16
MESH = pl.DeviceIdType.MESH

EPS = 1e-6
GRID_W = 64
HEADS = 4
HEAD_DIM = 128
CHUNK = 128
CONV_K = 31
ROPE_BASE = 10000.0
ROPE_PAIRS = (16, 24, 24)
POOL_WINDOWS = (2, 4, 8, 16)
ADAM_LR, ADAM_B1, ADAM_B2, ADAM_EPS, ADAM_WD, ADAM_STEP = 0.001, 0.9, 0.999, 1e-08, 0.01, 10

ROW_TILE = 256
CONV_HALO = 16
POOL_HALO = 8
VMEM_LIMIT = 56 * 1024 * 1024


def _pcall(body, **kw):
    return pl.pallas_call(body, **kw)


def _cp(sem=None, vmem=VMEM_LIMIT):
    if sem is None:
        return pltpu.CompilerParams(vmem_limit_bytes=vmem)
    return pltpu.CompilerParams(dimension_semantics=sem, vmem_limit_bytes=vmem)


def _sds(shape, dtype=F32):
    return jax.ShapeDtypeStruct(tuple(shape), dtype)


def _full(shape):
    nd = len(shape)
    return pl.BlockSpec(tuple(shape), lambda *_: (0,) * nd)


def _sigmoid(x):
    return jax.nn.sigmoid(x)


def _silu(x):
    return x * _sigmoid(x)


def _dsilu(x):
    s = _sigmoid(x)
    return s * (1.0 + x * (1.0 - s))


def _colsum(a):
    return jnp.sum(a, axis=0, keepdims=True)


def _dot(a, b, dn):
    return lax.dot_general(a.astype(BF16), b.astype(BF16), dn, preferred_element_type=F32)


NN = (((1,), (0,)), ((), ()))
NT = (((1,), (1,)), ((), ()))
TN = (((0,), (0,)), ((), ()))


def _mm_tile(t):
    best = 16
    for d in range(16, min(t, 1152) + 1, 16):
        if t % d == 0:
            best = d
    return best


def _mm(name, pairs, grid, out_shape, out_spec, dn, alias=None):
    npairs = len(pairs)
    nk = grid[-1]
    kax = len(grid) - 1

    def body(*refs):
        ins = refs[:2 * npairs]
        o_ref = refs[2 * npairs + (1 if alias is not None else 0)]
        tot = None
        for p in range(npairs):
            d = _dot(ins[2 * p][...], ins[2 * p + 1][...], dn)
            tot = d if tot is None else tot + d
        if nk == 1:
            o_ref[...] = tot.astype(o_ref.dtype)
        else:
            acc = refs[-1]
            k = pl.program_id(kax)

            @pl.when(k == 0)
            def _():
                acc[...] = tot

            @pl.when(k != 0)
            def _():
                acc[...] += tot

            @pl.when(k == nk - 1)
            def _():
                o_ref[...] = acc[...].astype(o_ref.dtype)

    args, in_specs = [], []
    for a, a_spec, b, b_spec in pairs:
        args += [a, b]
        in_specs += [a_spec, b_spec]
    kw = {}
    if alias is not None:
        args.append(alias)
        in_specs.append(pl.BlockSpec(memory_space=pl.ANY))
        kw["input_output_aliases"] = {len(args) - 1: 0}
    blk = tuple(d for d in out_spec.block_shape if d is not None)
    scratch = [pltpu.VMEM(blk, F32)] if nk > 1 else []
    sem = ("parallel",) * kax + ("arbitrary",)
    return _pcall(body, name=name, grid=grid, in_specs=in_specs, out_specs=out_spec, out_shape=out_shape,
                  scratch_shapes=scratch, compiler_params=_cp(sem), **kw)(*args)


def _mm_cols(name, a, w, lyr, out_dtype=F32):
    t, k = a.shape
    j, _, _, n = w.shape
    tm = _mm_tile(t)
    return _mm(name, [(a, pl.BlockSpec((tm, k), lambda i, jj, kk: (i, 0)),
                       w, pl.BlockSpec((None, None, k, n), lambda i, jj, kk: (jj, lyr, 0, 0)))],
               (t // tm, j, 1), _sds((t, j * n), out_dtype), pl.BlockSpec((tm, n), lambda i, jj, kk: (i, jj)), NN)


def _mm_cols_bwd(name, d, w, lyr):
    t = d.shape[0]
    j, _, k, n = w.shape
    tm = _mm_tile(t)
    return _mm(name, [(d, pl.BlockSpec((tm, n), lambda i, u, kk: (i, kk)),
                       w, pl.BlockSpec((None, None, k, n), lambda i, u, kk: (kk, lyr, 0, 0)))],
               (t // tm, 1, j), _sds((t, k)), pl.BlockSpec((tm, k), lambda i, u, kk: (i, 0)), NT)


def _mm_rows(name, a, w, lyr):
    t = a.shape[0]
    j, _, kb, n = w.shape
    tm = _mm_tile(t)
    return _mm(name, [(a, pl.BlockSpec((tm, kb), lambda i, u, kk: (i, kk)),
                       w, pl.BlockSpec((None, None, kb, n), lambda i, u, kk: (kk, lyr, 0, 0)))],
               (t // tm, 1, j), _sds((t, n)), pl.BlockSpec((tm, n), lambda i, u, kk: (i, 0)), NN)


def _mm_rows_bwd(name, d, w, lyr):
    t, n = d.shape
    j, _, kb, _ = w.shape
    tm = _mm_tile(t)
    return _mm(name, [(d, pl.BlockSpec((tm, n), lambda i, jj, kk: (i, 0)),
                       w, pl.BlockSpec((None, None, kb, n), lambda i, jj, kk: (jj, lyr, 0, 0)))],
               (t // tm, j, 1), _sds((t, j * kb)), pl.BlockSpec((tm, kb), lambda i, jj, kk: (i, jj)), NT)


def _wgrad(name, a, a_cols, b, b_cols, g_buf, lyr):
    t = a.shape[0]
    j, _, r, n = g_buf.shape
    tt = _mm_tile(t)
    a_spec = (pl.BlockSpec((tt, a.shape[1]), lambda jj, u, kk: (kk, 0)) if a_cols is None
              else pl.BlockSpec((tt, a_cols), lambda jj, u, kk: (kk, jj)))
    b_spec = (pl.BlockSpec((tt, b.shape[1]), lambda jj, u, kk: (kk, 0)) if b_cols is None
              else pl.BlockSpec((tt, b_cols), lambda jj, u, kk: (kk, jj)))
    return _mm(name, [(a, a_spec, b, b_spec)], (j, 1, t // tt), _sds(g_buf.shape),
               pl.BlockSpec((None, None, r, n), lambda jj, u, kk: (jj, lyr, 0, 0)), TN, alias=g_buf)


def _wgrad3(name, a, a3, b, b3, g_buf, lyr):
    t = a.shape[1] if a3 else a.shape[0]
    j, _, r, n = g_buf.shape
    tt = _mm_tile(t)
    a_spec = (pl.BlockSpec((None, tt, a.shape[2]), lambda jj, u, kk: (jj, kk, 0)) if a3
              else pl.BlockSpec((tt, a.shape[1]), lambda jj, u, kk: (kk, 0)))
    b_spec = (pl.BlockSpec((None, tt, b.shape[2]), lambda jj, u, kk: (jj, kk, 0)) if b3
              else pl.BlockSpec((tt, b.shape[1]), lambda jj, u, kk: (kk, 0)))
    return _mm(name, [(a, a_spec, b, b_spec)], (j, 1, t // tt), _sds(g_buf.shape),
               pl.BlockSpec((None, None, r, n), lambda jj, u, kk: (jj, lyr, 0, 0)), TN, alias=g_buf)


def _ffn_up(name, h, wg, wu, lyr):
    t, k = h.shape
    j, _, _, n = wg.shape
    tm = _mm_tile(t)

    def body(h_ref, wg_ref, wu_ref, a_ref, gt_ref, up_ref):
        hv = h_ref[...]
        gt = _dot(hv, wg_ref[...], NN)
        up = _dot(hv, wu_ref[...], NN)
        a_ref[...] = (_silu(gt) * up).astype(BF16)
        gt_ref[...] = gt.astype(BF16)
        up_ref[...] = up.astype(BF16)

    wspec = pl.BlockSpec((None, None, k, n), lambda i, jj: (jj, lyr, 0, 0))
    ospec = pl.BlockSpec((None, tm, n), lambda i, jj: (jj, i, 0))
    o = _sds((j, t, n), BF16)
    return _pcall(body, name=name, grid=(t // tm, j),
                  in_specs=[pl.BlockSpec((tm, k), lambda i, jj: (i, 0)), wspec, wspec],
                  out_specs=[ospec, ospec, ospec], out_shape=[o, o, o],
                  compiler_params=_cp(("parallel", "parallel")))(h, wg, wu)


def _ffn_down(name, a, wd, lyr):
    j, t, n = a.shape
    dm = wd.shape[3]
    tm = _mm_tile(t)
    return _mm(name, [(a, pl.BlockSpec((None, tm, n), lambda i, u, kk: (kk, i, 0)),
                       wd, pl.BlockSpec((None, None, n, dm), lambda i, u, kk: (kk, lyr, 0, 0)))],
               (t // tm, 1, j), _sds((t, dm)), pl.BlockSpec((tm, dm), lambda i, u, kk: (i, 0)), NN)


def _ffn_down_bwd(name, df, wd, gt, up, lyr):
    t, dm = df.shape
    j, _, n, _ = wd.shape
    tm = _mm_tile(t)

    def body(df_ref, wd_ref, gt_ref, up_ref, dgt_ref, dup_ref):
        da = _dot(df_ref[...], wd_ref[...], NT)
        g = gt_ref[...].astype(F32)
        u = up_ref[...].astype(F32)
        dgt_ref[...] = (da * u * _dsilu(g)).astype(BF16)
        dup_ref[...] = (da * _silu(g)).astype(BF16)

    aspec = pl.BlockSpec((None, tm, n), lambda i, jj: (jj, i, 0))
    o = _sds((j, t, n), BF16)
    return _pcall(body, name=name, grid=(t // tm, j),
                  in_specs=[pl.BlockSpec((tm, dm), lambda i, jj: (i, 0)),
                            pl.BlockSpec((None, None, n, dm), lambda i, jj: (jj, lyr, 0, 0)), aspec, aspec],
                  out_specs=[aspec, aspec], out_shape=[o, o],
                  compiler_params=_cp(("parallel", "parallel")))(df, wd, gt, up)


def _ffn_in_bwd(name, dgt, dup, wg, wu, lyr):
    j, t, n = dgt.shape
    k = wg.shape[2]
    tm = _mm_tile(t)
    aspec = pl.BlockSpec((None, tm, n), lambda i, u, kk: (kk, i, 0))
    wspec = pl.BlockSpec((None, None, k, n), lambda i, u, kk: (kk, lyr, 0, 0))
    return _mm(name, [(dgt, aspec, wg, wspec), (dup, aspec, wu, wspec)], (t // tm, 1, j), _sds((t, k)),
               pl.BlockSpec((tm, k), lambda i, u, kk: (i, 0)), NT)


def _modrow(ref, row, is_ctx):
    return jnp.where(is_ctx, ref[0, row:row + 1, :], ref[1, row:row + 1, :])


def _rnm(name, x, delta, mod_g, g_row, mod_n, sh_row, sc_row, nw, nct):
    t, dm = x.shape
    tm = ROW_TILE
    has = delta is not None

    def body(*refs):
        if has:
            x_ref, d_ref, mg_ref, m_ref, nw_ref, xo_ref, h_ref = refs
        else:
            x_ref, m_ref, nw_ref, h_ref = refs
        is_ctx = pl.program_id(0) < nct
        xv = x_ref[...]
        if has:
            xv = xv + _modrow(mg_ref, g_row, is_ctx) * d_ref[...]
            xo_ref[...] = xv
        r = lax.rsqrt(jnp.mean(xv * xv, axis=-1, keepdims=True) + EPS)
        hv = (xv * r * nw_ref[...]) * (1.0 + _modrow(m_ref, sc_row, is_ctx)) + _modrow(m_ref, sh_row, is_ctx)
        h_ref[...] = hv.astype(BF16)

    row = pl.BlockSpec((tm, dm), lambda i: (i, 0))
    ins = [x] + ([delta, mod_g] if has else []) + [mod_n, nw]
    in_specs = [row] + ([row, _full(mod_g.shape)] if has else []) + [_full(mod_n.shape), _full(nw.shape)]
    outs = ([_sds((t, dm))] if has else []) + [_sds((t, dm), BF16)]
    out_specs = ([row] if has else []) + [row]
    res = _pcall(body, name=name, grid=(t // tm,), in_specs=in_specs, out_specs=out_specs, out_shape=outs,
                 compiler_params=_cp(("parallel",)))(*ins)
    return res if has else (None, res[0])


def _bnm(name, xn, dh, dup, yprev, mod_n, sh_row, sc_row, mod_g, g_row, nw, nct):
    t, dm = xn.shape
    tm = ROW_TILE
    has = yprev is not None

    def body(*refs):
        if has:
            x_ref, dh_ref, du_ref, y_ref, mn_ref, mg_ref, nw_ref, dx_ref, dd_ref, s_ref = refs
        else:
            x_ref, dh_ref, du_ref, mn_ref, nw_ref, dx_ref, s_ref = refs
        i = pl.program_id(0)
        is_ctx = i < nct

        @pl.when(i == 0)
        def _():
            s_ref[...] = jnp.zeros_like(s_ref)

        xv = x_ref[...]
        r = lax.rsqrt(jnp.mean(xv * xv, axis=-1, keepdims=True) + EPS)
        xh = xv * r
        w = nw_ref[...]
        sc1 = 1.0 + _modrow(mn_ref, sc_row, is_ctx)
        dhv = dh_ref[...]
        dxh = dhv * sc1 * w
        dx = r * (dxh - xh * jnp.mean(dxh * xh, axis=-1, keepdims=True)) + du_ref[...]
        dx_ref[...] = dx
        parts = [_colsum(dhv), _colsum(dhv * (xh * w))]
        if has:
            dd_ref[...] = (_modrow(mg_ref, g_row, is_ctx) * dx).astype(BF16)
            parts.append(_colsum(dx * y_ref[...]))
        else:
            parts.append(jnp.zeros((1, dm), F32))
        upd = jnp.concatenate(parts + [jnp.zeros((5, dm), F32)], axis=0)
        dnw = jnp.concatenate([jnp.zeros((3, dm), F32), _colsum(dhv * sc1 * xh), jnp.zeros((4, dm), F32)], axis=0)

        @pl.when(is_ctx)
        def _():
            s_ref[0] += upd
            s_ref[1] += dnw

        @pl.when(jnp.logical_not(is_ctx))
        def _():
            s_ref[1] += upd + dnw

    row = pl.BlockSpec((tm, dm), lambda i: (i, 0))
    ins = [xn, dh, dup] + ([yprev] if has else []) + [mod_n] + ([mod_g] if has else []) + [nw]
    in_specs = ([row, row, row] + ([row] if has else []) + [_full(mod_n.shape)]
                + ([_full(mod_g.shape)] if has else []) + [_full(nw.shape)])
    outs = [_sds((t, dm))] + ([_sds((t, dm), BF16)] if has else []) + [_sds((2, 8, dm))]
    out_specs = [row] + ([row] if has else []) + [_full((2, 8, dm))]
    res = _pcall(body, name=name, grid=(t // tm,), in_specs=in_specs, out_specs=out_specs, out_shape=outs,
                 compiler_params=_cp(("arbitrary",)))(*ins)
    return res if has else (res[0], None, res[1])


def _fin(name, x1, f, mod, g_row, fw, target, nct):
    t, dm = x1.shape
    tm = ROW_TILE

    def body(x_ref, f_ref, m_ref, fw_ref, t_ref, loss_ref, dx_ref, dd_ref, s_ref):
        i = pl.program_id(0)

        @pl.when(i == 0)
        def _():
            s_ref[...] = jnp.zeros_like(s_ref)
            loss_ref[...] = jnp.zeros_like(loss_ref)

        @pl.when(i < nct)
        def _():
            dx_ref[...] = jnp.zeros_like(dx_ref)
            dd_ref[...] = jnp.zeros_like(dd_ref)

        @pl.when(i >= nct)
        def _():
            g = m_ref[1, g_row:g_row + 1, :]
            fv = f_ref[...]
            xv = x_ref[...] + g * fv
            r = lax.rsqrt(jnp.mean(xv * xv, axis=-1, keepdims=True) + EPS)
            xh = xv * r
            w = fw_ref[...]
            err = xh * w - t_ref[...]
            loss_ref[...] += 0.5 * jnp.sum(err * err) / dm
            dout = err * (1.0 / dm)
            dxh = dout * w
            dx = r * (dxh - xh * jnp.mean(dxh * xh, axis=-1, keepdims=True))
            dx_ref[...] = dx
            dd_ref[...] = (g * dx).astype(BF16)
            s_ref[...] += jnp.concatenate([_colsum(dx * fv), _colsum(dout * xh), jnp.zeros((6, dm), F32)], axis=0)

    row = pl.BlockSpec((tm, dm), lambda i: (i, 0))
    trow = pl.BlockSpec((tm, dm), lambda i: (jnp.maximum(i - nct, 0), 0))
    return _pcall(body, name=name, grid=(t // tm,),
                  in_specs=[row, row, _full(mod.shape), _full(fw.shape), trow],
                  out_specs=[_full((8, 128)), row, row, _full((8, dm))],
                  out_shape=[_sds((8, 128)), _sds((t, dm)), _sds((t, dm), BF16), _sds((8, dm))],
                  compiler_params=_cp(("arbitrary",)))(x1, f, mod, fw, target)


def _rope_tables(t, lc):
    l = t - lc
    rows = l // GRID_W
    grid_r = jnp.broadcast_to(jnp.arange(rows, dtype=F32)[:, None], (rows, GRID_W)).reshape(-1)
    grid_c = jnp.broadcast_to(jnp.arange(GRID_W, dtype=F32)[None, :], (rows, GRID_W)).reshape(-1)

    def angles(p_seq, p_row, p_col):
        parts = []
        for p, n in zip((p_seq, p_row, p_col), ROPE_PAIRS):
            freq = ROPE_BASE ** (-jnp.arange(n, dtype=F32) / n)
            parts.append(p[:, None] * freq[None, :])
        return jnp.concatenate(parts, axis=-1)

    zc = jnp.zeros((lc,), F32)
    ang = jnp.concatenate([angles(jnp.arange(lc, dtype=F32), zc, zc),
                           angles(jnp.full((l,), lc, F32), grid_r, grid_c)], axis=0)
    cos, sin = jnp.cos(ang), jnp.sin(ang)
    return jnp.concatenate([cos, cos], axis=-1), jnp.concatenate([-sin, sin], axis=-1)


def _rope(u, cs, sn):
    return u * cs + pltpu.roll(u, HEAD_DIM // 2, 1) * sn


def _rope_t(d, cs, sn):
    return d * cs + pltpu.roll(d * sn, HEAD_DIM // 2, 1)


def _even_qkv(name, p, cs, sn):
    t = p.shape[0]
    tm = ROW_TILE
    w = HEADS * HEAD_DIM
    scale = HEAD_DIM ** -0.5

    def body(q_ref, k_ref, v_ref, cs_ref, sn_ref, qo_ref, ko_ref, vo_ref):
        c, s = cs_ref[...], sn_ref[...]
        for h in range(HEADS):
            sl = slice(h * HEAD_DIM, (h + 1) * HEAD_DIM)
            qo_ref[:, sl] = (_rope(q_ref[:, sl], c, s) * scale).astype(BF16)
            ko_ref[:, sl] = _rope(k_ref[:, sl], c, s).astype(BF16)
        vo_ref[...] = v_ref[...].astype(BF16)

    col = lambda j: pl.BlockSpec((tm, w), lambda i: (i, j))
    tab = pl.BlockSpec((tm, HEAD_DIM), lambda i: (i, 0))
    o = _sds((t, w), BF16)
    return _pcall(body, name=name, grid=(t // tm,), in_specs=[col(0), col(1), col(2), tab, tab],
                  out_specs=[col(0)] * 3, out_shape=[o, o, o], compiler_params=_cp(("parallel",)))(p, p, p, cs, sn)


def _log_sigmoid_row(x):
    e = jnp.exp(-jnp.abs(x))
    l1p = jnp.where(e < 0.01, e * (1.0 - e * (0.5 - e * (1.0 / 3.0))), jnp.log(1.0 + e))
    return jnp.minimum(x, 0.0) - l1p


def _ret_tables(lgb_ref, dm_ref, xi_ref, zt_ref):
    ri = lax.broadcasted_iota(jnp.int32, (CHUNK, CHUNK), 0).astype(F32)
    ci = lax.broadcasted_iota(jnp.int32, (CHUNK, CHUNK), 1).astype(F32)
    for d in range(2):
        for h in range(HEADS):
            idx = d * HEADS + h
            lg = _log_sigmoid_row(lgb_ref[idx:idx + 1, :])
            if d == 0:
                e, mask = ri - ci, ri >= ci
                xe, ze = ri + 1.0, (CHUNK - 1.0) - ri
            else:
                e, mask = ci - ri - 1.0, ci > ri
                xe, ze = (CHUNK - 1.0) - ri, ri
            dm_ref[idx] = jnp.where(mask, jnp.exp(lg * jnp.where(mask, e, 0.0)), 0.0)
            xi_ref[idx] = jnp.exp(lg * xe)
            zt_ref[idx] = jnp.exp(lg * ze)


def _ret_exponents(d):
    ri = lax.broadcasted_iota(jnp.int32, (CHUNK, CHUNK), 0).astype(F32)
    ci = lax.broadcasted_iota(jnp.int32, (CHUNK, CHUNK), 1).astype(F32)
    if d == 0:
        return ri - ci, ri + 1.0, (CHUNK - 1.0) - ri
    return ci - ri - 1.0, (CHUNK - 1.0) - ri, ri


def _bwd_chunk(n, ncc, nc):
    return jnp.where(n < ncc, ncc - 1 - n, nc - 1 - (n - ncc))


def _retention_fwd(name, q, k, v, lgb, lc):
    t, w = q.shape
    nc, ncc = t // CHUNK, lc // CHUNK
    nh = 2 * HEADS

    def body(qf_ref, kf_ref, vf_ref, qb_ref, kb_ref, vb_ref, lgb_ref, of_ref, ob_ref, ss_ref,
             s_ref, dm_ref, xi_ref, zt_ref):
        n = pl.program_id(0)

        @pl.when(n == 0)
        def _():
            s_ref[...] = jnp.zeros_like(s_ref)
            _ret_tables(lgb_ref, dm_ref, xi_ref, zt_ref)

        for d in range(2):
            q_ref, k_ref, v_ref, o_ref = (qf_ref, kf_ref, vf_ref, of_ref) if d == 0 else (qb_ref, kb_ref, vb_ref, ob_ref)
            for h in range(HEADS):
                idx = d * HEADS + h
                sl = slice(h * HEAD_DIM, (h + 1) * HEAD_DIM)
                qv, kv, vv = q_ref[:, sl], k_ref[:, sl], v_ref[:, sl]
                s = s_ref[idx]
                ss_ref[idx] = s
                a = _dot(qv, kv, NT) * dm_ref[idx]
                o = _dot(a, vv, NN) + _dot(qv.astype(F32) * xi_ref[idx], s, NN)
                o_ref[:, sl] = o
                gc = jnp.exp(_log_sigmoid_row(lgb_ref[idx:idx + 1, :]) * float(CHUNK))
                s_ref[idx] = gc * s + _dot(kv.astype(F32) * zt_ref[idx], vv, TN)

    fspec = pl.BlockSpec((CHUNK, w), lambda n: (n, 0))
    bspec = pl.BlockSpec((CHUNK, w), lambda n: (_bwd_chunk(n, ncc, nc), 0))
    tab = pltpu.VMEM((nh, CHUNK, CHUNK), F32)
    return _pcall(body, name=name, grid=(nc,),
                  in_specs=[fspec] * 3 + [bspec] * 3 + [_full((nh, HEAD_DIM))],
                  out_specs=[fspec, bspec, pl.BlockSpec((None, nh, CHUNK, CHUNK), lambda n: (n, 0, 0, 0))],
                  out_shape=[_sds((t, w)), _sds((t, w)), _sds((nc, nh, CHUNK, CHUNK))],
                  scratch_shapes=[tab, tab, tab, tab],
                  compiler_params=_cp(("arbitrary",)))(q, k, v, q, k, v, lgb)


def _retention_bwd(name, q, k, v, do, ss, lgb, lc):
    t, w = q.shape
    nc, ncc = t // CHUNK, lc // CHUNK
    nh = 2 * HEADS

    def body(qf_ref, kf_ref, vf_ref, gf_ref, qb_ref, kb_ref, vb_ref, gb_ref, ss_ref, lgb_ref,
             dqf_ref, dkf_ref, dvf_ref, dqb_ref, dkb_ref, dvb_ref, dl_ref,
             ds_ref, dm_ref, xi_ref, zt_ref, acc_ref):
        n = pl.program_id(0)

        @pl.when(n == 0)
        def _():
            ds_ref[...] = jnp.zeros_like(ds_ref)
            acc_ref[...] = jnp.zeros_like(acc_ref)
            _ret_tables(lgb_ref, dm_ref, xi_ref, zt_ref)

        for d in range(2):
            if d == 0:
                q_ref, k_ref, v_ref, g_ref, dq_ref, dk_ref, dv_ref = qf_ref, kf_ref, vf_ref, gf_ref, dqf_ref, dkf_ref, dvf_ref
            else:
                q_ref, k_ref, v_ref, g_ref, dq_ref, dk_ref, dv_ref = qb_ref, kb_ref, vb_ref, gb_ref, dqb_ref, dkb_ref, dvb_ref
            ee, xe, ze = _ret_exponents(d)
            for h in range(HEADS):
                idx = d * HEADS + h
                sl = slice(h * HEAD_DIM, (h + 1) * HEAD_DIM)
                qv, kv, vv, gv = q_ref[:, sl], k_ref[:, sl], v_ref[:, sl], g_ref[:, sl]
                s = ss_ref[idx]
                dsp = ds_ref[idx]
                dmat, xi, zt = dm_ref[idx], xi_ref[idx], zt_ref[idx]
                qf32, kf32 = qv.astype(F32), kv.astype(F32)
                a = _dot(qv, kv, NT) * dmat
                dar = _dot(gv, vv, NT)
                da = dar * dmat
                t1 = _dot(gv, s, NT)
                t2 = _dot(vv, dsp, NT)
                dq_ref[:, sl] = _dot(da, kv, NN) + xi * t1
                dk_ref[:, sl] = _dot(da, qv, TN) + zt * t2
                dv_ref[:, sl] = _dot(a, gv, TN) + _dot(kf32 * zt, dsp, NN)
                gc = jnp.exp(_log_sigmoid_row(lgb_ref[idx:idx + 1, :]) * float(CHUNK))
                ds_ref[idx] = gc * dsp + _dot(qf32 * xi, gv, TN)
                acc_ref[idx] += (ee * a * dar + xe * xi * qf32 * t1 + ze * zt * kf32 * t2
                                 + (float(CHUNK) * gc) * dsp * s)

        @pl.when(n == nc - 1)
        def _():
            for idx in range(nh):
                tot = jnp.sum(acc_ref[idx])
                dl_ref[idx:idx + 1, :] = tot * _sigmoid(-lgb_ref[idx:idx + 1, :])

    fmap = lambda n: (nc - 1 - n, 0)
    bmap = lambda n: (_bwd_chunk(nc - 1 - n, ncc, nc), 0)
    fspec = pl.BlockSpec((CHUNK, w), fmap)
    bspec = pl.BlockSpec((CHUNK, w), bmap)
    tab = pltpu.VMEM((nh, CHUNK, CHUNK), F32)
    o = _sds((t, w))
    return _pcall(body, name=name, grid=(nc,),
                  in_specs=[fspec] * 4 + [bspec] * 4
                  + [pl.BlockSpec((None, nh, CHUNK, CHUNK), lambda n: (nc - 1 - n, 0, 0, 0)), _full((nh, HEAD_DIM))],
                  out_specs=[fspec] * 3 + [bspec] * 3 + [_full((nh, HEAD_DIM))],
                  out_shape=[o] * 6 + [_sds((nh, HEAD_DIM))],
                  scratch_shapes=[tab, tab, tab, tab, tab],
                  compiler_params=_cp(("arbitrary",)))(q, k, v, do, q, k, v, do, ss, lgb)


def _halo_specs(tm, halo, t, width, col):
    hb = tm // halo
    last = t // halo - 1
    prev = pl.BlockSpec((halo, width), lambda i: (jnp.maximum(i * hb - 1, 0), col))
    nxt = pl.BlockSpec((halo, width), lambda i: (jnp.minimum((i + 1) * hb, last), col))
    return prev, nxt


def _halo_valid(i, nct, nt):
    vp = jnp.logical_and(i != 0, i != nct)
    vn = jnp.logical_and(i != nct - 1, i != nt - 1)
    return vp, vn


def _fill_window(win_ref, prev, cur, nxt, vp, vn, halo, tm):
    win_ref[0:halo, :] = jnp.where(vp, prev, 0.0)
    win_ref[halo:halo + tm, :] = cur
    win_ref[halo + tm:halo + tm + halo, :] = jnp.where(vn, nxt, 0.0)


CONV_SUB = 64


def _conv_taps(win_ref, w_ref, tm, flip):
    outs = []
    for r0 in range(0, tm, CONV_SUB):
        acc = None
        for kk in range(CONV_K):
            wk = (CONV_K - 1 - kk) if flip else kk
            term = w_ref[wk:wk + 1, :] * win_ref[r0 + kk + 1:r0 + kk + 1 + CONV_SUB, :]
            acc = term if acc is None else acc + term
        outs.append(acc)
    return jnp.concatenate(outs, axis=0)


def _head_norm(y):
    r = lax.rsqrt(jnp.mean(y * y, axis=-1, keepdims=True) + EPS)
    return y * r, r


def _ln_stats(y):
    mu = jnp.mean(y, axis=-1, keepdims=True)
    yc = y - mu
    rs = lax.rsqrt(jnp.mean(yc * yc, axis=-1, keepdims=True) + EPS)
    return yc * rs, rs


def _ln_bwd(dyh, yh, rs):
    return rs * (dyh - jnp.mean(dyh, axis=-1, keepdims=True) - yh * jnp.mean(dyh * yh, axis=-1, keepdims=True))


def _even_mix(name, p, of, ob, cw, lnw, lnb, nct):
    t = p.shape[0]
    tm, halo = ROW_TILE, CONV_HALO
    nt = t // tm
    w = HEADS * HEAD_DIM

    def body(g_ref, a_ref, gb_ref, ap_ref, gbp_ref, an_ref, gbn_ref, of_ref, ob_ref, cw_ref, lw_ref, lb_ref,
             mix_ref, yc_ref, win_ref):
        i = pl.program_id(0)
        vp, vn = _halo_valid(i, nct, nt)
        glu = lambda a, b: a * _sigmoid(b)
        _fill_window(win_ref, glu(ap_ref[...], gbp_ref[...]), glu(a_ref[...], gb_ref[...]),
                     glu(an_ref[...], gbn_ref[...]), vp, vn, halo, tm)
        yc = _conv_taps(win_ref, cw_ref, tm, False)
        yc_ref[...] = yc
        yh, _ = _ln_stats(yc)
        mix_ref[:, w:2 * w] = _silu(yh * lw_ref[...] + lb_ref[...]).astype(BF16)
        for h in range(HEADS):
            sl = slice(h * HEAD_DIM, (h + 1) * HEAD_DIM)
            yn, _ = _head_norm(of_ref[:, sl] + ob_ref[:, sl])
            mix_ref[:, sl] = (_silu(g_ref[:, sl]) * yn).astype(BF16)

    col = lambda j: pl.BlockSpec((tm, w), lambda i: (i, j))
    ap, an = _halo_specs(tm, halo, t, w, 4)
    gp, gn = _halo_specs(tm, halo, t, w, 5)
    row = pl.BlockSpec((tm, w), lambda i: (i, 0))
    return _pcall(body, name=name, grid=(nt,),
                  in_specs=[col(3), col(4), col(5), ap, gp, an, gn, row, row,
                            _full(cw.shape), _full(lnw.shape), _full(lnb.shape)],
                  out_specs=[pl.BlockSpec((tm, 2 * w), lambda i: (i, 0)), row],
                  out_shape=[_sds((t, 2 * w), BF16), _sds((t, w))],
                  scratch_shapes=[pltpu.VMEM((tm + 2 * halo, w), F32)],
                  compiler_params=_cp(("parallel",)))(p, p, p, p, p, p, p, of, ob, cw, lnw, lnb)


def _even_mix_bwd1(name, dmix, p, of, ob, yc, lnw, lnb):
    t = p.shape[0]
    tm = ROW_TILE
    w = HEADS * HEAD_DIM

    def body(dr_ref, dc_ref, g_ref, of_ref, ob_ref, yc_ref, lw_ref, lb_ref, do_ref, dg_ref, dyc_ref, s_ref):
        @pl.when(pl.program_id(0) == 0)
        def _():
            s_ref[...] = jnp.zeros_like(s_ref)

        for h in range(HEADS):
            sl = slice(h * HEAD_DIM, (h + 1) * HEAD_DIM)
            yn, r = _head_norm(of_ref[:, sl] + ob_ref[:, sl])
            gv = g_ref[:, sl]
            dr = dr_ref[:, sl]
            dg_ref[:, sl] = (dr * yn * _dsilu(gv)).astype(BF16)
            dyn = dr * _silu(gv)
            do_ref[:, sl] = (r * (dyn - yn * jnp.mean(dyn * yn, axis=-1, keepdims=True))).astype(BF16)
        yh, rs = _ln_stats(yc_ref[...])
        lw = lw_ref[...]
        dlo = dc_ref[...] * _dsilu(yh * lw + lb_ref[...])
        dyc_ref[...] = _ln_bwd(dlo * lw, yh, rs)
        s_ref[...] += jnp.concatenate([_colsum(dlo * yh), _colsum(dlo), jnp.zeros((6, w), F32)], axis=0)

    col = lambda j: pl.BlockSpec((tm, w), lambda i: (i, j))
    row = pl.BlockSpec((tm, w), lambda i: (i, 0))
    return _pcall(body, name=name, grid=(t // tm,),
                  in_specs=[col(0), col(1), col(3), row, row, row, _full(lnw.shape), _full(lnb.shape)],
                  out_specs=[row, row, row, _full((8, w))],
                  out_shape=[_sds((t, w), BF16), _sds((t, w), BF16), _sds((t, w)), _sds((8, w))],
                  compiler_params=_cp(("arbitrary",)))(dmix, dmix, p, of, ob, yc, lnw, lnb)


def _even_conv_bwd(name, dyc, p, cw, nct):
    t = p.shape[0]
    tm, halo = ROW_TILE, CONV_HALO
    nt = t // tm
    w = HEADS * HEAD_DIM

    def body(d_ref, dp_ref, dn_ref, a_ref, gb_ref, ap_ref, gbp_ref, an_ref, gbn_ref, cw_ref,
             da_ref, dgb_ref, dw_ref, dwin_ref, uwin_ref):
        i = pl.program_id(0)

        @pl.when(i == 0)
        def _():
            dw_ref[...] = jnp.zeros_like(dw_ref)

        vp, vn = _halo_valid(i, nct, nt)
        glu = lambda a, b: a * _sigmoid(b)
        dcur = d_ref[...]
        _fill_window(dwin_ref, dp_ref[...], dcur, dn_ref[...], vp, vn, halo, tm)
        _fill_window(uwin_ref, glu(ap_ref[...], gbp_ref[...]), glu(a_ref[...], gb_ref[...]),
                     glu(an_ref[...], gbn_ref[...]), vp, vn, halo, tm)
        du = _conv_taps(dwin_ref, cw_ref, tm, True)
        av = a_ref[...]
        sg = _sigmoid(gb_ref[...])
        da_ref[...] = (du * sg).astype(BF16)
        dgb_ref[...] = (du * av * sg * (1.0 - sg)).astype(BF16)
        rows = [_colsum(dcur * uwin_ref[kk + 1:kk + 1 + tm, :]) for kk in range(CONV_K)]
        dw_ref[...] += jnp.concatenate(rows + [jnp.zeros((1, w), F32)], axis=0)

    col = lambda j: pl.BlockSpec((tm, w), lambda i: (i, j))
    row = pl.BlockSpec((tm, w), lambda i: (i, 0))
    dp, dn = _halo_specs(tm, halo, t, w, 0)
    ap, an = _halo_specs(tm, halo, t, w, 4)
    gp, gn = _halo_specs(tm, halo, t, w, 5)
    win = pltpu.VMEM((tm + 2 * halo, w), F32)
    return _pcall(body, name=name, grid=(nt,),
                  in_specs=[row, dp, dn, col(4), col(5), ap, gp, an, gn, _full(cw.shape)],
                  out_specs=[row, row, _full((CONV_K + 1, w))],
                  out_shape=[_sds((t, w), BF16), _sds((t, w), BF16), _sds((CONV_K + 1, w))],
                  scratch_shapes=[win, win],
                  compiler_params=_cp(("arbitrary",)))(dyc, dyc, dyc, p, p, p, p, p, p, cw)


def _even_dp(name, dqs, dks, dvs, dg, da, dgb, cs, sn):
    t, w = dg.shape
    tm = ROW_TILE
    scale = HEAD_DIM ** -0.5

    def body(dqf_ref, dqb_ref, dkf_ref, dkb_ref, dvf_ref, dvb_ref, dg_ref, da_ref, dgb_ref, cs_ref, sn_ref, dp_ref):
        c, s = cs_ref[...], sn_ref[...]
        for h in range(HEADS):
            sl = slice(h * HEAD_DIM, (h + 1) * HEAD_DIM)
            dp_ref[:, sl] = (_rope_t(dqf_ref[:, sl] + dqb_ref[:, sl], c, s) * scale).astype(BF16)
            dp_ref[:, w + h * HEAD_DIM:w + (h + 1) * HEAD_DIM] = _rope_t(dkf_ref[:, sl] + dkb_ref[:, sl], c, s).astype(BF16)
        dp_ref[:, 2 * w:3 * w] = (dvf_ref[...] + dvb_ref[...]).astype(BF16)
        dp_ref[:, 3 * w:4 * w] = dg_ref[...]
        dp_ref[:, 4 * w:5 * w] = da_ref[...]
        dp_ref[:, 5 * w:6 * w] = dgb_ref[...]

    row = pl.BlockSpec((tm, w), lambda i: (i, 0))
    tab = pl.BlockSpec((tm, HEAD_DIM), lambda i: (i, 0))
    return _pcall(body, name=name, grid=(t // tm,), in_specs=[row] * 9 + [tab, tab],
                  out_specs=pl.BlockSpec((tm, 6 * w), lambda i: (i, 0)), out_shape=_sds((t, 6 * w), BF16),
                  compiler_params=_cp(("parallel",)))(dqs[0], dqs[1], dks[0], dks[1], dvs[0], dvs[1], dg, da, dgb, cs, sn)


GROUPS = 4
GC = 128
INV_SQRT2 = 0.7071067811865476
INV_SQRT_2PI = 0.3989422804014327


def _gelu(x):
    return 0.5 * x * (1.0 + lax.erf(x * INV_SQRT2))


def _dgelu(x):
    return 0.5 * (1.0 + lax.erf(x * INV_SQRT2)) + x * jnp.exp(-0.5 * x * x) * INV_SQRT_2PI


def _pool_count(i, nct, lc, t, tm, rows, row0, left, right):
    is_ctx = i < nct
    seg_start = jnp.where(is_ctx, 0, lc)
    seg_len = jnp.where(is_ctx, lc, t - lc)
    pos = i * tm + row0 - seg_start + lax.broadcasted_iota(jnp.int32, (rows, GC), 0)
    cnt = jnp.minimum(pos + right, seg_len - 1) - jnp.maximum(pos - left, 0) + 1
    return jnp.maximum(cnt, 1).astype(F32)


def _spatial_gate(vln, sgw_ref, sgb_ref, tm):
    cols = []
    for g in range(GROUPS):
        sl = slice(g * GC, (g + 1) * GC)
        parts = [_dot(sgw_ref[g], vln[r0:r0 + CHUNK, sl], NN) + sgb_ref[g] for r0 in range(0, tm, CHUNK)]
        cols.append(jnp.concatenate(parts, axis=0))
    return jnp.concatenate(cols, axis=1)


def _odd_mix(name, p, pw, pscale, lnw, lnb, sgw, sgb, nct, lc):
    t = p.shape[0]
    tm, halo = ROW_TILE, POOL_HALO
    nt = t // tm
    w = GROUPS * GC

    def body(pc_ref, pp_ref, pn_ref, pu_ref, pv_ref, pw_ref, ps_ref, lw_ref, lb_ref, sgw_ref, sgb_ref,
             mix_ref, m_ref, win_ref):
        i = pl.program_id(0)
        vp, vn = _halo_valid(i, nct, nt)
        pc = pc_ref[...]
        _fill_window(win_ref, pp_ref[...], pc, pn_ref[...], vp, vn, halo, tm)
        for g, wd in enumerate(POOL_WINDOWS):
            sl = slice(g * GC, (g + 1) * GC)
            left = wd // 2
            right = wd - 1 - left
            s = None
            for o in range(-left, right + 1):
                term = win_ref[halo + o:halo + o + tm, sl]
                s = term if s is None else s + term
            mg = s / _pool_count(i, nct, lc, t, tm, tm, 0, left, right) - pc[:, sl]
            m_ref[:, sl] = mg
            mix_ref[:, sl] = (_dot(mg, pw_ref[g], NN) * ps_ref[:, sl]).astype(BF16)
        u = _gelu(pu_ref[...])
        vh, _ = _ln_stats(_gelu(pv_ref[...]))
        s = _spatial_gate(vh * lw_ref[...] + lb_ref[...], sgw_ref, sgb_ref, tm)
        mix_ref[:, w:2 * w] = (u * s).astype(BF16)

    col = lambda j: pl.BlockSpec((tm, w), lambda i: (i, j))
    pp, pn = _halo_specs(tm, halo, t, w, 0)
    return _pcall(body, name=name, grid=(nt,),
                  in_specs=[col(0), pp, pn, col(1), col(2), _full(pw.shape), _full(pscale.shape),
                            _full(lnw.shape), _full(lnb.shape), _full(sgw.shape), _full(sgb.shape)],
                  out_specs=[pl.BlockSpec((tm, 2 * w), lambda i: (i, 0)), col(0)],
                  out_shape=[_sds((t, 2 * w), BF16), _sds((t, w))],
                  scratch_shapes=[pltpu.VMEM((tm + 2 * halo, w), F32)],
                  compiler_params=_cp(("parallel",)))(p, p, p, p, p, pw, pscale, lnw, lnb, sgw, sgb)


def _odd_mix_bwd1(name, dmix, p, m, pw, pscale, lnw, lnb, sgw, sgb):
    t = p.shape[0]
    tm = ROW_TILE
    w = GROUPS * GC

    def body(dpo_ref, dsg_ref, pu_ref, pv_ref, m_ref, pw_ref, ps_ref, lw_ref, lb_ref, sgw_ref, sgb_ref,
             dm_ref, dpd_ref, vec_ref, dpw_ref, dsgw_ref, dsgb_ref):
        @pl.when(pl.program_id(0) == 0)
        def _():
            vec_ref[...] = jnp.zeros_like(vec_ref)
            dpw_ref[...] = jnp.zeros_like(dpw_ref)
            dsgw_ref[...] = jnp.zeros_like(dsgw_ref)
            dsgb_ref[...] = jnp.zeros_like(dsgb_ref)

        dscale = []
        for g in range(GROUPS):
            sl = slice(g * GC, (g + 1) * GC)
            mg = m_ref[:, sl]
            dpo = dpo_ref[:, sl]
            dscale.append(_colsum(dpo * _dot(mg, pw_ref[g], NN)))
            dpo = dpo * ps_ref[:, sl]
            dm_ref[:, sl] = _dot(dpo, pw_ref[g], NT)
            dpw_ref[g] += _dot(mg, dpo, TN)
        pu, pv = pu_ref[...], pv_ref[...]
        u = _gelu(pu)
        vh, rs = _ln_stats(_gelu(pv))
        lw = lw_ref[...]
        vln = vh * lw + lb_ref[...]
        s = _spatial_gate(vln, sgw_ref, sgb_ref, tm)
        dsg = dsg_ref[...]
        dpd_ref[:, 0:w] = (dsg * s * _dgelu(pu)).astype(BF16)
        ds = dsg * u
        cols = []
        for g in range(GROUPS):
            sl = slice(g * GC, (g + 1) * GC)
            parts = []
            for r0 in range(0, tm, CHUNK):
                dsc = ds[r0:r0 + CHUNK, sl]
                parts.append(_dot(sgw_ref[g], dsc, TN))
                dsgw_ref[g] += _dot(dsc, vln[r0:r0 + CHUNK, sl], NT)
                dsgb_ref[g] += dsc
            cols.append(jnp.concatenate(parts, axis=0))
        dvln = jnp.concatenate(cols, axis=1)
        dpd_ref[:, w:2 * w] = (_ln_bwd(dvln * lw, vh, rs) * _dgelu(pv)).astype(BF16)
        vec_ref[...] += jnp.concatenate([jnp.concatenate(dscale, axis=1), _colsum(dvln * vh), _colsum(dvln),
                                         jnp.zeros((5, w), F32)], axis=0)

        @pl.when(pl.program_id(0) == t // tm - 1)
        def _():
            for g in range(GROUPS):
                dsgb_ref[g] = jnp.broadcast_to(jnp.sum(dsgb_ref[g], axis=1, keepdims=True), (GC, GC))

    col = lambda j: pl.BlockSpec((tm, w), lambda i: (i, j))
    mat = _full((GROUPS, GC, GC))
    return _pcall(body, name=name, grid=(t // tm,),
                  in_specs=[col(0), col(1), col(1), col(2), col(0), _full(pw.shape), _full(pscale.shape),
                            _full(lnw.shape), _full(lnb.shape), _full(sgw.shape), _full(sgb.shape)],
                  out_specs=[col(0), pl.BlockSpec((tm, 2 * w), lambda i: (i, 0)), _full((8, w)), mat, mat, mat],
                  out_shape=[_sds((t, w)), _sds((t, 2 * w), BF16), _sds((8, w)),
                             _sds((GROUPS, GC, GC)), _sds((GROUPS, GC, GC)), _sds((GROUPS, GC, GC))],
                  compiler_params=_cp(("arbitrary",)))(dmix, dmix, p, p, m, pw, pscale, lnw, lnb, sgw, sgb)


def _odd_dp(name, dm, dpd, nct, lc):
    t, w = dm.shape
    tm, halo = ROW_TILE, POOL_HALO
    nt = t // tm

    def body(d_ref, dp_ref, dn_ref, dpd_ref, o_ref, win_ref):
        i = pl.program_id(0)
        vp, vn = _halo_valid(i, nct, nt)
        dcur = d_ref[...]
        _fill_window(win_ref, dp_ref[...], dcur, dn_ref[...], vp, vn, halo, tm)
        for g, wd in enumerate(POOL_WINDOWS):
            sl = slice(g * GC, (g + 1) * GC)
            left = wd // 2
            right = wd - 1 - left
            win_ref[:, sl] = win_ref[:, sl] / _pool_count(i, nct, lc, t, tm, tm + 2 * halo, -halo, left, right)
            s = None
            for o in range(-right, left + 1):
                term = win_ref[halo + o:halo + o + tm, sl]
                s = term if s is None else s + term
            o_ref[:, sl] = (s - dcur[:, sl]).astype(BF16)
        o_ref[:, w:3 * w] = dpd_ref[...]

    row = pl.BlockSpec((tm, w), lambda i: (i, 0))
    pp, pn = _halo_specs(tm, halo, t, w, 0)
    return _pcall(body, name=name, grid=(nt,),
                  in_specs=[row, pp, pn, pl.BlockSpec((tm, 2 * w), lambda i: (i, 0))],
                  out_specs=pl.BlockSpec((tm, 3 * w), lambda i: (i, 0)), out_shape=_sds((t, 3 * w), BF16),
                  scratch_shapes=[pltpu.VMEM((tm + 2 * halo, w), F32)],
                  compiler_params=_cp(("parallel",)))(dm, dm, dm, dpd)


def _place():
    x, y, c = lax.axis_index("x"), lax.axis_index("y"), lax.axis_index("c")
    chips = [(1 - x, y), (x, 1 - y), (1 - x, 1 - y)]
    return x, y, c, chips


def _chip_index(cx, cy):
    return 2 * cx + cy


def _all_gather8(name, blk):
    m_per, n = blk.shape

    def body(x_ref, out_ref, send_sems, recv_sems, local_sem):
        x, y, c, chips = _place()
        me, sibling = (x, y, c), (x, y, 1 - c)

        def rows(px, py, pc):
            return out_ref.at[pl.ds((4 * px + 2 * py + pc) * m_per, m_per), :]

        def copy(k, block, to, src=None):
            return pltpu.make_async_remote_copy(
                src_ref=rows(*block) if src is None else src, dst_ref=rows(*block),
                send_sem=send_sems.at[k], recv_sem=recv_sems.at[k], device_id=to, device_id_type=MESH)

        mine = pltpu.make_async_copy(x_ref, rows(*me), local_sem)
        mine.start()
        first = [copy(0, me, sibling, src=x_ref)]
        first += [copy(1 + j, me, (*chip, c), src=x_ref) for j, chip in enumerate(chips)]
        for cp in first:
            cp.start()
        passed = [copy(4 + j, (*chip, c), sibling) for j, chip in enumerate(chips)]
        for j, chip in enumerate(chips):
            copy(1 + j, (*chip, c), me).wait_recv()
            passed[j].start()
        copy(0, sibling, me).wait_recv()
        for j, chip in enumerate(chips):
            copy(4 + j, (*chip, 1 - c), me).wait_recv()
        for cp in first + passed:
            cp.wait_send()
        mine.wait()

    return _pcall(body, name=name, out_shape=_sds((8 * m_per, n), blk.dtype),
                  in_specs=[pl.BlockSpec(memory_space=pltpu.VMEM)], out_specs=pl.BlockSpec(memory_space=pltpu.VMEM),
                  scratch_shapes=[pltpu.SemaphoreType.DMA((7,)), pltpu.SemaphoreType.DMA((7,)), pltpu.SemaphoreType.DMA],
                  compiler_params=_cp())(blk)


ANY = pl.BlockSpec(memory_space=pl.ANY)


def _gather_weights(name, ws):
    nw = len(ws)
    ns = 7

    def body(*refs):
        w_refs, o_refs = refs[:nw], refs[nw:2 * nw]
        send_sems, recv_sems = refs[2 * nw:]
        x, y, c, chips = _place()
        me_chip = _chip_index(x, y)
        sibling = (x, y, 1 - c)

        def rcopy(t, k, src, dst, to):
            return pltpu.make_async_remote_copy(src_ref=src, dst_ref=dst, send_sem=send_sems.at[t * ns + k],
                                                recv_sem=recv_sems.at[t * ns + k], device_id=to, device_id_type=MESH)

        sends = []
        for t in range(nw):
            lh = w_refs[t].shape[0] // 2
            for k, chip in enumerate(chips):
                sends.append(rcopy(t, k, w_refs[t].at[pl.ds(c * lh, lh)], o_refs[t].at[me_chip, pl.ds(c * lh, lh)], (*chip, c)))
                sends[-1].start()
            sends.append(rcopy(t, 6, w_refs[t], o_refs[t].at[me_chip], sibling))
            sends[-1].start()
        for t in range(nw):
            lh = w_refs[t].shape[0] // 2
            for k, chip in enumerate(chips):
                part = o_refs[t].at[_chip_index(*chip), pl.ds(c * lh, lh)]
                rcopy(t, k, part, part, (*chip, c)).wait_recv()
                sends.append(rcopy(t, 3 + k, part, part, sibling))
                sends[-1].start()
        for t in range(nw):
            lh = w_refs[t].shape[0] // 2
            own = o_refs[t].at[me_chip]
            rcopy(t, 6, own, own, sibling).wait_recv()
            for k, chip in enumerate(chips):
                part = o_refs[t].at[_chip_index(*chip), pl.ds((1 - c) * lh, lh)]
                rcopy(t, 3 + k, part, part, sibling).wait_recv()
        for cp in sends:
            cp.wait_send()

    return _pcall(body, name=name, out_shape=[_sds((4,) + w.shape, w.dtype) for w in ws],
                  in_specs=[ANY] * nw, out_specs=[ANY] * nw,
                  scratch_shapes=[pltpu.SemaphoreType.DMA((ns * nw,)), pltpu.SemaphoreType.DMA((ns * nw,))],
                  compiler_params=_cp())(*ws)


def _rs_pair(name, gs):
    ng = len(gs)

    def body(*refs):
        g_refs, o_refs = refs[:ng], refs[ng:2 * ng]
        send_sems, recv_sems = refs[2 * ng:]
        x, y, c, _ = _place()
        cps = []
        for t in range(ng):
            lh = g_refs[t].shape[1] // 2
            cp = pltpu.make_async_remote_copy(
                src_ref=g_refs[t].at[:, pl.ds((1 - c) * lh, lh)], dst_ref=o_refs[t],
                send_sem=send_sems.at[t], recv_sem=recv_sems.at[t], device_id=(x, y, 1 - c), device_id_type=MESH)
            cp.start()
            cps.append(cp)
        for cp in cps:
            cp.wait()

    outs = [_sds((g.shape[0], g.shape[1] // 2) + g.shape[2:], g.dtype) for g in gs]
    return _pcall(body, name=name, out_shape=outs, in_specs=[ANY] * ng, out_specs=[ANY] * ng,
                  scratch_shapes=[pltpu.SemaphoreType.DMA((ng,)), pltpu.SemaphoreType.DMA((ng,))],
                  compiler_params=_cp())(*gs)


def _rs_chips(name, ps):
    ng = len(ps)

    def body(*refs):
        p_refs, o_refs = refs[:ng], refs[ng:2 * ng]
        send_sems, recv_sems = refs[2 * ng:]
        x, y, c, chips = _place()
        cps = []
        for t in range(ng):
            for k, chip in enumerate(chips):
                cp = pltpu.make_async_remote_copy(
                    src_ref=p_refs[t].at[_chip_index(*chip)], dst_ref=o_refs[t].at[k],
                    send_sem=send_sems.at[t * 3 + k], recv_sem=recv_sems.at[t * 3 + k],
                    device_id=(*chip, c), device_id_type=MESH)
                cp.start()
                cps.append(cp)
        for cp in cps:
            cp.wait()

    return _pcall(body, name=name, out_shape=[_sds((3,) + p.shape[1:], p.dtype) for p in ps],
                  in_specs=[ANY] * ng, out_specs=[ANY] * ng,
                  scratch_shapes=[pltpu.SemaphoreType.DMA((3 * ng,)), pltpu.SemaphoreType.DMA((3 * ng,))],
                  compiler_params=_cp())(*ps)


def _rs_share(name, ss):
    ng = len(ss)

    def body(*refs):
        o_refs = refs[ng:2 * ng]
        send_sems, recv_sems = refs[2 * ng:]
        x, y, c, _ = _place()
        cps = []
        for t in range(ng):
            lh = o_refs[t].shape[0] // 2
            mine = o_refs[t].at[pl.ds(c * lh, lh)]
            cp = pltpu.make_async_remote_copy(
                src_ref=mine, dst_ref=mine, send_sem=send_sems.at[t], recv_sem=recv_sems.at[t],
                device_id=(x, y, 1 - c), device_id_type=MESH)
            cp.start()
            cps.append(cp)
        for t in range(ng):
            lh = o_refs[t].shape[0] // 2
            cps[t].wait_send()
            theirs = o_refs[t].at[pl.ds((1 - c) * lh, lh)]
            pltpu.make_async_remote_copy(
                src_ref=theirs, dst_ref=theirs, send_sem=send_sems.at[t], recv_sem=recv_sems.at[t],
                device_id=(x, y, 1 - c), device_id_type=MESH).wait_recv()

    return _pcall(body, name=name, out_shape=[_sds(s.shape, s.dtype) for s in ss],
                  in_specs=[ANY] * ng, out_specs=[ANY] * ng, input_output_aliases={t: t for t in range(ng)},
                  scratch_shapes=[pltpu.SemaphoreType.DMA((ng,)), pltpu.SemaphoreType.DMA((ng,))],
                  compiler_params=_cp())(*ss)


def _row_block(r, cn):
    if r % 8:
        return r
    best = 8
    for d in range(8, r + 1, 8):
        if r % d == 0 and d * cn * 4 <= (1 << 20):
            best = d
    return best


def _add_half(name, g, a, idx):
    j, lh, r, cn = a.shape
    tr = _row_block(r, cn)

    def body(i_ref, g_ref, a_ref, o_ref):
        o_ref[...] = (g_ref[...] + a_ref[...]).astype(BF16)

    blk = (None, None, tr, cn)
    gs = pltpu.PrefetchScalarGridSpec(
        num_scalar_prefetch=1, grid=(j, lh, r // tr),
        in_specs=[pl.BlockSpec(blk, lambda jj, l, i, i_ref: (jj, i_ref[0] * lh + l, i, 0)),
                  pl.BlockSpec(blk, lambda jj, l, i, i_ref: (jj, l, i, 0))],
        out_specs=pl.BlockSpec(blk, lambda jj, l, i, i_ref: (jj, l, i, 0)))
    return _pcall(body, name=name, grid_spec=gs, out_shape=_sds(a.shape, BF16),
                  compiler_params=_cp(("parallel", "parallel", "parallel")))(idx, g, a)


def _sum_final(name, g, a, b, idx):
    _, lyr, r, cn = g.shape
    lh = lyr // 2
    tr = _row_block(r, cn)

    def body(i_ref, g_ref, a_ref, b_ref, o_ref):
        own = g_ref[...] + a_ref[...]
        o_ref[...] = (own + b_ref[0].astype(F32)) + (b_ref[1].astype(F32) + b_ref[2].astype(F32))

    blk = (None, None, tr, cn)
    gs = pltpu.PrefetchScalarGridSpec(
        num_scalar_prefetch=1, grid=(lh, r // tr),
        in_specs=[pl.BlockSpec(blk, lambda l, i, i_ref: (i_ref[1], i_ref[0] * lh + l, i, 0)),
                  pl.BlockSpec(blk, lambda l, i, i_ref: (i_ref[1], l, i, 0)),
                  pl.BlockSpec((3, None, tr, cn), lambda l, i, i_ref: (0, l, i, 0))],
        out_specs=pl.BlockSpec((None, tr, cn), lambda l, i, i_ref: (i_ref[0] * lh + l, i, 0)))
    return _pcall(body, name=name, grid_spec=gs, out_shape=_sds((lyr, r, cn)),
                  compiler_params=_cp(("parallel", "parallel")))(idx, g, a, b)


def _sum8(name, g):
    _, r, n = g.shape
    tr = 8

    def body(g_ref, o_ref):
        o_ref[...] = ((g_ref[0] + g_ref[1]) + (g_ref[2] + g_ref[3])) + ((g_ref[4] + g_ref[5]) + (g_ref[6] + g_ref[7]))

    return _pcall(body, name=name, grid=(r // tr,), in_specs=[pl.BlockSpec((8, tr, n), lambda i: (0, i, 0))],
                  out_specs=pl.BlockSpec((tr, n), lambda i: (i, 0)), out_shape=_sds((r, n)),
                  compiler_params=_cp(("parallel",)))(g)


def _ada_mod(name, c16, ada_w, bias):
    nl, dm, n = ada_w.shape

    def body(c_ref, w_ref, b_ref, o_ref):
        o_ref[...] = _dot(_silu(c_ref[...]), w_ref[...], NN) + b_ref[...]

    return _pcall(body, name=name, grid=(nl,),
                  in_specs=[_full(c16.shape), pl.BlockSpec((None, dm, n), lambda i: (i, 0, 0)),
                            pl.BlockSpec((None, 1, n), lambda i: (i, 0, 0))],
                  out_specs=pl.BlockSpec((None, 16, n), lambda i: (i, 0, 0)), out_shape=_sds((nl, 16, n)),
                  compiler_params=_cp(("parallel",)))(c16, ada_w, bias)


def _ada_bwd(name, c16, dmod, ada_w):
    nl, dm, n = ada_w.shape

    def body(c_ref, d_ref, w_ref, gw_ref, dc_ref):
        @pl.when(pl.program_id(0) == 0)
        def _():
            dc_ref[...] = jnp.zeros_like(dc_ref)

        dv = d_ref[...]
        gw_ref[...] = _dot(_silu(c_ref[...]), dv, TN)
        dc_ref[...] += _dot(dv, w_ref[...], NT)

    return _pcall(body, name=name, grid=(nl,),
                  in_specs=[_full(c16.shape), pl.BlockSpec((None, 16, n), lambda i: (i, 0, 0)),
                            pl.BlockSpec((None, dm, n), lambda i: (i, 0, 0))],
                  out_specs=[pl.BlockSpec((None, dm, n), lambda i: (i, 0, 0)), _full((16, dm))],
                  out_shape=[_sds((nl, dm, n)), _sds((16, dm))],
                  compiler_params=_cp(("arbitrary",)))(c16, dmod, ada_w)


def _rowsum16(name, dmod):
    nl, _, n = dmod.shape

    def body(d_ref, o_ref):
        o_ref[...] = _colsum(d_ref[...])

    return _pcall(body, name=name, grid=(nl,), in_specs=[pl.BlockSpec((None, 16, n), lambda i: (i, 0, 0))],
                  out_specs=pl.BlockSpec((None, 1, n), lambda i: (i, 0, 0)), out_shape=_sds((nl, 1, n)),
                  compiler_params=_cp(("parallel",)))(dmod)


def _cctx_grad(name, parts, c_ctx):
    def body(p_ref, c_ref, o_ref):
        tot = (p_ref[0:1, :] + p_ref[1:2, :]) + (p_ref[2:3, :] + p_ref[3:4, :])
        o_ref[...] = tot * _dsilu(c_ref[...])

    return _pcall(body, name=name, out_shape=_sds(c_ctx.shape), compiler_params=_cp())(parts, c_ctx)


def _adamw(name, w, g, m, v):
    shape = w.shape
    cn = shape[-1]
    r = math.prod(shape[:-1]) if len(shape) > 1 else 1
    tr = _row_block(r, cn)
    c1 = 1.0 - ADAM_B1 ** ADAM_STEP
    c2 = 1.0 - ADAM_B2 ** ADAM_STEP

    def body(w_ref, g_ref, m_ref, v_ref, d_ref, mo_ref, vo_ref):
        gv = g_ref[...]
        mn = ADAM_B1 * m_ref[...] + (1.0 - ADAM_B1) * gv
        vn = ADAM_B2 * v_ref[...] + (1.0 - ADAM_B2) * (gv * gv)
        d_ref[...] = -ADAM_LR * ((mn / c1) / (jnp.sqrt(vn / c2) + ADAM_EPS) + ADAM_WD * w_ref[...])
        mo_ref[...] = mn
        vo_ref[...] = vn

    blk = pl.BlockSpec((tr, cn), lambda i: (i, 0))
    o = _sds((r, cn))
    outs = _pcall(body, name=name, grid=(r // tr,), in_specs=[blk] * 4, out_specs=[blk] * 3, out_shape=[o, o, o],
                  compiler_params=_cp(("parallel",)))(*[a.reshape(r, cn) for a in (w, g, m, v)])
    return tuple(a.reshape(shape) for a in outs)


def _local_step(xs, target, modt, nw, fnw, wts, ev, od, lc):
    t, dm = xs.shape
    nct = lc // ROW_TILE
    depth = nw.shape[0]
    cs, sn = _rope_tables(t, lc)
    saved = []
    x_in, x1p, fp = xs, None, None
    for i in range(depth):
        j, even = i // 2, i % 2 == 0
        tag = f"l{i}"
        if i == 0:
            _, h = _rnm(tag + "_norm1", x_in, None, None, 0, modt[0], 0, 1, nw[0, 0], nct)
        else:
            x_in, h = _rnm(tag + "_norm1", x1p, fp, modt[i - 1], 5, modt[i], 0, 1, nw[i, 0], nct)
        s = dict(x=x_in, h=h)
        if even:
            p = _mm_cols(tag + "_in", h, wts["even_in"], j)
            q, k, v = _even_qkv(tag + "_qkv", p, cs, sn)
            of, ob, ss = _retention_fwd(tag + "_ret", q, k, v, ev["lgb"][j], lc)
            mix, yc = _even_mix(tag + "_mix", p, of, ob, ev["cw"][j], ev["lnw"][j], ev["lnb"][j], nct)
            y = _mm_rows(tag + "_out", mix, wts["even_out"], j)
            s.update(p=p, q=q, k=k, v=v, of=of, ob=ob, ss=ss, yc=yc)
        else:
            p = _mm_cols(tag + "_in", h, wts["odd_in"], j)
            mix, m = _odd_mix(tag + "_mix", p, od["pw"][j], od["ps"][j], od["lnw"][j], od["lnb"][j],
                              od["sgw"][j], od["sgb"][j], nct, lc)
            y = _mm_rows(tag + "_out", mix, wts["odd_out"], j)
            s.update(p=p, m=m)
        x1, h2 = _rnm(tag + "_norm2", x_in, y, modt[i], 2, modt[i], 3, 4, nw[i, 1], nct)
        a, gt, up = _ffn_up(tag + "_ffn_up", h2, wts["gate"], wts["up"], i)
        f = _ffn_down(tag + "_ffn_down", a, wts["down"], i)
        s.update(mix=mix, y=y, x1=x1, h2=h2, a=a, gt=gt, up=up, f=f)
        saved.append(s)
        x1p, fp = x1, f

    loss_blk, dx, df, fin_s = _fin("final", x1p, fp, modt[depth - 1], 5, fnw, target, nct)

    gb = {k: jnp.zeros(w.shape, F32) for k, w in wts.items()}
    dmod = [[None] * 6 for _ in range(depth)]
    dnw = [[None, None] for _ in range(depth)]
    zero2 = jnp.zeros((2, dm), F32)
    dmod[depth - 1][5] = jnp.stack([zero2[0], fin_s[0]])
    small = dict(dfnw=fin_s[1], ev=[], od=[])
    for i in reversed(range(depth)):
        j, even = i // 2, i % 2 == 0
        tag = f"l{i}b"
        s = saved[i]
        dgt, dup = _ffn_down_bwd(tag + "_ffn_down", df, wts["down"], s["gt"], s["up"], i)
        gb["down"] = _wgrad3(tag + "_gdown", s["a"], True, df, False, gb["down"], i)
        gb["gate"] = _wgrad3(tag + "_ggate", s["h2"], False, dgt, True, gb["gate"], i)
        gb["up"] = _wgrad3(tag + "_gup", s["h2"], False, dup, True, gb["up"], i)
        dh2 = _ffn_in_bwd(tag + "_ffn_in", dgt, dup, wts["gate"], wts["up"], i)
        dx1, dy, s2 = _bnm(tag + "_norm2", s["x1"], dh2, dx, s["y"], modt[i], 3, 4, modt[i], 2, nw[i, 1], nct)
        dmod[i][3], dmod[i][4], dmod[i][2] = s2[:, 0], s2[:, 1], s2[:, 2]
        dnw[i][1] = s2[1, 3]
        wo, wi = ("even_out", "even_in") if even else ("odd_out", "odd_in")
        dmix = _mm_rows_bwd(tag + "_out", dy, wts[wo], j)
        gb[wo] = _wgrad(tag + "_gout", s["mix"], wts[wo].shape[2], dy, None, gb[wo], j)
        if even:
            do, dg, dyc, lns = _even_mix_bwd1(tag + "_mix1", dmix, s["p"], s["of"], s["ob"], s["yc"],
                                              ev["lnw"][j], ev["lnb"][j])
            da, dgb, dcw = _even_conv_bwd(tag + "_conv", dyc, s["p"], ev["cw"][j], nct)
            dqf, dkf, dvf, dqb, dkb, dvb, dl = _retention_bwd(tag + "_ret", s["q"], s["k"], s["v"], do, s["ss"],
                                                              ev["lgb"][j], lc)
            dp = _even_dp(tag + "_dp", (dqf, dqb), (dkf, dkb), (dvf, dvb), dg, da, dgb, cs, sn)
            small["ev"].append(dict(lnw=lns[0], lnb=lns[1], cw=dcw, dl=dl[:, 0]))
        else:
            dm_, dpd, vec, dpw, dsgw, dsgb = _odd_mix_bwd1(tag + "_mix1", dmix, s["p"], s["m"], od["pw"][j], od["ps"][j],
                                                           od["lnw"][j], od["lnb"][j], od["sgw"][j], od["sgb"][j])
            dp = _odd_dp(tag + "_dp", dm_, dpd, nct, lc)
            small["od"].append(dict(ps=vec[0], lnw=vec[1], lnb=vec[2], pw=dpw, sgw=dsgw, sgb=dsgb[:, :, 0]))
        dh = _mm_cols_bwd(tag + "_in", dp, wts[wi], j)
        gb[wi] = _wgrad(tag + "_gin", s["h"], None, dp, wts[wi].shape[3], gb[wi], j)
        if i > 0:
            dx, df, s1 = _bnm(tag + "_norm1", s["x"], dh, dx1, saved[i - 1]["f"], modt[i], 0, 1, modt[i - 1], 5,
                              nw[i, 0], nct)
            dmod[i - 1][5] = s1[:, 2]
        else:
            dx, _, s1 = _bnm(tag + "_norm1", s["x"], dh, dx1, None, modt[0], 0, 1, None, 0, nw[0, 0], nct)
        dmod[i][0], dmod[i][1] = s1[:, 0], s1[:, 1]
        dnw[i][0] = s1[1, 3]
    small["ev"].reverse()
    small["od"].reverse()
    dmod_t = jnp.stack([jnp.concatenate([jnp.stack(rows, axis=1), jnp.zeros((2, 2, dm), F32)], axis=1) for rows in dmod])
    small["dmod"] = dmod_t
    small["dnw"] = jnp.stack([jnp.stack(r) for r in dnw])
    return loss_blk, dx, gb, small


WEIGHTS = ["c_ctx", "ada_w", "ada_b", "norm_w", "even_w_in", "even_w_out", "ret_decay_logit", "conv_dw_w",
           "conv_ln_w", "conv_ln_b", "odd_w_in", "odd_w_out", "pool_w", "pool_scale", "sg_ln_w", "sg_ln_b",
           "sg_w", "sg_b", "ffn_w_gate", "ffn_w_up", "ffn_w_down", "final_norm_w"]
BIG = dict(even_in="even_w_in", even_out="even_w_out", odd_in="odd_w_in", odd_out="odd_w_out",
           gate="ffn_w_gate", up="ffn_w_up", down="ffn_w_down")


def _rows(a, width=1024):
    flat = a.reshape(-1)
    n = flat.shape[0]
    per = 8 * width
    tot = -(-n // per) * per
    return jnp.pad(flat, (0, tot - n)).reshape(tot // width, width)


def _unshard(parts, lead):
    nl = len(lead)
    perm = tuple(range(1, nl + 1)) + (0, nl + 1)
    return parts.transpose(perm).reshape(tuple(lead) + (4 * parts.shape[-1],))


def _my_cols(a, chip, n):
    start = (0,) * (a.ndim - 1) + (chip * n,)
    return lax.dynamic_slice(a, start, a.shape[:-1] + (n,))


def kernel(x, c, ctx, c_ctx, ada_w, ada_b, norm_w, even_w_in, even_w_out, ret_decay_logit, conv_dw_w, conv_ln_w, conv_ln_b, odd_w_in, odd_w_out, pool_w, pool_scale, sg_ln_w, sg_ln_b, sg_w, sg_b, ffn_w_gate, ffn_w_up, ffn_w_down, final_norm_w, loss_target, m_c_ctx, m_ada_w, m_ada_b, m_norm_w, m_even_w_in, m_even_w_out, m_ret_decay_logit, m_conv_dw_w, m_conv_ln_w, m_conv_ln_b, m_odd_w_in, m_odd_w_out, m_pool_w, m_pool_scale, m_sg_ln_w, m_sg_ln_b, m_sg_w, m_sg_b, m_ffn_w_gate, m_ffn_w_up, m_ffn_w_down, m_final_norm_w, v_c_ctx, v_ada_w, v_ada_b, v_norm_w, v_even_w_in, v_even_w_out, v_ret_decay_logit, v_conv_dw_w, v_conv_ln_w, v_conv_ln_b, v_odd_w_in, v_odd_w_out, v_pool_w, v_pool_scale, v_sg_ln_w, v_sg_ln_b, v_sg_w, v_sg_b, v_ffn_w_gate, v_ffn_w_up, v_ffn_w_down, v_final_norm_w):
    wv = dict(c_ctx=c_ctx, ada_w=ada_w, ada_b=ada_b, norm_w=norm_w, even_w_in=even_w_in, even_w_out=even_w_out,
              ret_decay_logit=ret_decay_logit, conv_dw_w=conv_dw_w, conv_ln_w=conv_ln_w, conv_ln_b=conv_ln_b,
              odd_w_in=odd_w_in, odd_w_out=odd_w_out, pool_w=pool_w, pool_scale=pool_scale, sg_ln_w=sg_ln_w,
              sg_ln_b=sg_ln_b, sg_w=sg_w, sg_b=sg_b, ffn_w_gate=ffn_w_gate, ffn_w_up=ffn_w_up,
              ffn_w_down=ffn_w_down, final_norm_w=final_norm_w)
    mv = dict(zip(WEIGHTS, (m_c_ctx, m_ada_w, m_ada_b, m_norm_w, m_even_w_in, m_even_w_out, m_ret_decay_logit,
                            m_conv_dw_w, m_conv_ln_w, m_conv_ln_b, m_odd_w_in, m_odd_w_out, m_pool_w, m_pool_scale,
                            m_sg_ln_w, m_sg_ln_b, m_sg_w, m_sg_b, m_ffn_w_gate, m_ffn_w_up, m_ffn_w_down,
                            m_final_norm_w)))
    vv = dict(zip(WEIGHTS, (v_c_ctx, v_ada_w, v_ada_b, v_norm_w, v_even_w_in, v_even_w_out, v_ret_decay_logit,
                            v_conv_dw_w, v_conv_ln_w, v_conv_ln_b, v_odd_w_in, v_odd_w_out, v_pool_w, v_pool_scale,
                            v_sg_ln_w, v_sg_ln_b, v_sg_w, v_sg_b, v_ffn_w_gate, v_ffn_w_up, v_ffn_w_down,
                            v_final_norm_w)))
    xi, yi, ci = lax.axis_index("x"), lax.axis_index("y"), lax.axis_index("c")
    chip = 2 * xi + yi
    dev = 4 * xi + 2 * yi + ci
    dm = x.shape[-1]
    lc = ctx.shape[1]
    depth = ada_w.shape[0]
    n_ada = ada_w.shape[-1]

    cw_pad = jnp.pad(conv_dw_w, ((0, 0), (0, 1), (0, 0)))
    vec3 = jnp.stack([pool_scale, sg_ln_w, sg_ln_b])
    pack1 = jnp.concatenate([_rows(c), _rows(norm_w), _rows(cw_pad), _rows(vec3)], axis=0)
    g1 = _all_gather8("gather_small", pack1).reshape(8, 32, dm)
    c_all = g1[:, 0]
    per_chip = g1[0::2]
    norm_full = _unshard(per_chip[:, 8:10].reshape(4, depth, 2, dm // 4), (depth, 2))
    cw_full = _unshard(per_chip[:, 16:24].reshape(4, 2, CONV_K + 1, 128), (2, CONV_K + 1))
    vec_full = _unshard(per_chip[:, 24, :768].reshape(4, 3, 2, 128), (3, 2))

    c16 = jnp.concatenate([c_all, c_ctx[None, :], jnp.zeros((7, dm), F32)], axis=0)
    mod_sh = _ada_mod("ada_mod", c16, ada_w, _my_cols(ada_b, chip, n_ada)[:, None, :])
    g2 = _all_gather8("gather_mod", mod_sh.reshape(depth * 16, n_ada)).reshape(8, depth, 16, n_ada)
    mod_full = _unshard(g2[0::2], (depth, 16))
    mod_x = lax.dynamic_index_in_dim(mod_full, dev, axis=1, keepdims=False).reshape(depth, 6, dm)
    mod_c = mod_full[:, 8].reshape(depth, 6, dm)
    modt = jnp.pad(jnp.stack([mod_c, mod_x], axis=1), ((0, 0), (0, 0), (0, 2), (0, 0)))

    names = list(BIG)
    gathered = _gather_weights("gather_weights", [wv[BIG[k]].astype(BF16) for k in names])
    wts = dict(zip(names, gathered))

    ev = dict(lgb=jnp.broadcast_to(ret_decay_logit.reshape(-1, 2 * HEADS)[:, :, None], (ret_decay_logit.shape[0], 2 * HEADS, HEAD_DIM)),
              cw=cw_full, lnw=conv_ln_w[:, None, :], lnb=conv_ln_b[:, None, :])
    od = dict(pw=pool_w, ps=vec_full[0][:, None, :], lnw=vec_full[1][:, None, :], lnb=vec_full[2][:, None, :],
              sgw=sg_w, sgb=jnp.broadcast_to(sg_b[:, :, :, None], sg_b.shape + (GC,)))
    xs = jnp.concatenate([ctx[0], x[0]], axis=0)
    loss_blk, dxs, gb, small = _local_step(xs, loss_target[0], modt, norm_full[:, :, None, :], final_norm_w[None, :],
                                           wts, ev, od, lc)

    misc = jnp.stack([
        small["dfnw"], jnp.broadcast_to(loss_blk[0, 0], (dm,)),
        jnp.concatenate([e["lnw"] for e in small["ev"]]), jnp.concatenate([e["lnb"] for e in small["ev"]]),
        jnp.concatenate([o["ps"] for o in small["od"]]), jnp.concatenate([o["lnw"] for o in small["od"]]),
        jnp.concatenate([o["lnb"] for o in small["od"]]),
        jnp.pad(jnp.concatenate([e["dl"] for e in small["ev"]]), (0, dm - 4 * HEADS)),
        jnp.stack([o["sgb"] for o in small["od"]]).reshape(-1)])
    pack2 = jnp.concatenate([
        _rows(small["dmod"]), _rows(small["dnw"]), _rows(misc), _rows(jnp.stack([e["cw"] for e in small["ev"]])),
        _rows(jnp.stack([o["pw"] for o in small["od"]])), _rows(jnp.stack([o["sgw"] for o in small["od"]]))], axis=0)
    n2 = pack2.shape[0]
    g3 = _all_gather8("gather_grads", pack2)
    tot = _sum8("sum_grads", g3.reshape(8, n2, dm))
    r_mod = depth * 16
    o_nw, o_misc = r_mod, r_mod + 8
    o_cw = o_misc + 16
    o_pw = o_cw + 2 * (CONV_K + 1) // 2
    o_sgw = o_pw + 128
    dmod_sum = tot[:r_mod].reshape(depth, 2, 8, dm)
    dmod_dev = g3.reshape(8, n2, dm)[:, :r_mod].reshape(8, depth, 2, 8, dm)
    dm_x = dmod_dev[:, :, 1, :6].reshape(8, depth, 6 * dm).transpose(1, 0, 2)
    dm_c = dmod_sum[:, 0, :6].reshape(depth, 1, 6 * dm)
    dmod16 = jnp.concatenate([dm_x, dm_c, jnp.zeros((depth, 7, 6 * dm), F32)], axis=1)
    g_ada_b = _rowsum16("ada_b_grad", dmod16)[:, 0]
    g_ada_w, dc16 = _ada_bwd("ada_bwd", c16, _my_cols(dmod16, chip, n_ada), ada_w)
    g4 = _all_gather8("gather_cctx", dc16[8:16]).reshape(8, 8, dm)
    g_c_ctx = _cctx_grad("cctx_grad", g4[0::2, 0], c_ctx[None, :])[0]

    misc_t = tot[o_misc:o_misc + 16]
    half = lambda row: misc_t[row].reshape(2, dm // 2)
    grads = dict(
        c_ctx=g_c_ctx, ada_w=g_ada_w, ada_b=g_ada_b,
        norm_w=_my_cols(tot[o_nw:o_nw + 8].reshape(depth, 2, dm), chip, dm // 4),
        ret_decay_logit=misc_t[7, :4 * HEADS].reshape(ret_decay_logit.shape),
        conv_dw_w=_my_cols(tot[o_cw:o_cw + 2 * (CONV_K + 1) // 2].reshape(2, CONV_K + 1, dm // 2)[:, :CONV_K], chip, 128),
        conv_ln_w=half(2), conv_ln_b=half(3),
        pool_w=tot[o_pw:o_pw + 128].reshape(pool_w.shape),
        pool_scale=_my_cols(half(4), chip, 128), sg_ln_w=_my_cols(half(5), chip, 128), sg_ln_b=_my_cols(half(6), chip, 128),
        sg_w=tot[o_sgw:o_sgw + 128].reshape(sg_w.shape), sg_b=misc_t[8].reshape(sg_b.shape),
        final_norm_w=misc_t[0])
    loss = misc_t[1, 0]

    glist = [gb[k] for k in names]
    idx = jnp.stack([ci, chip]).astype(jnp.int32)
    from_sib = _rs_pair("rs_pair", glist)
    pair = [_add_half(f"rs_add_{k}", g, a, idx) for k, g, a in zip(names, glist, from_sib)]
    slots = _rs_chips("rs_chips", pair)
    halves = [_sum_final(f"rs_sum_{k}", g, a, b, idx) for k, g, a, b in zip(names, glist, from_sib, slots)]
    shards = _rs_share("rs_share", halves)
    for k, sh in zip(names, shards):
        grads[BIG[k]] = sh

    deltas, new_m, new_v = {}, {}, {}
    for n in WEIGHTS:
        deltas[n], new_m[n], new_v[n] = _adamw("adamw_" + n, wv[n], grads[n], mv[n], vv[n])
    grad_x = dxs[lc:][None]
    return (loss, grad_x, *[grads[n] for n in WEIGHTS], *[deltas[n] for n in WEIGHTS],
            *[new_m[n] for n in WEIGHTS], *[new_v[n] for n in WEIGHTS])
```

```python
import functools
import math

import jax
import jax.numpy as jnp
from jax import lax
from jax.experimental import pallas as pl
from jax.experimental.pallas import tpu as pltpu

F32 = jnp.float32
BF16 = jnp.bfloat16
MESH = pl.DeviceIdType.MESH

EPS = 1e-6
GRID_W = 64
HEADS = 4
HEAD_DIM = 128
CHUNK = 128
CONV_K = 31
ROPE_BASE = 10000.0
ROPE_PAIRS = (16, 24, 24)
POOL_WINDOWS = (2, 4, 8, 16)
ADAM_LR, ADAM_B1, ADAM_B2, ADAM_EPS, ADAM_WD, ADAM_STEP = 0.001, 0.9, 0.999, 1e-08, 0.01, 10

ROW_TILE = 256
CONV_HALO = 16
POOL_HALO = 8
VMEM_LIMIT = 56 * 1024 * 1024


def _pcall(body, **kw):
    return pl.pallas_call(body, **kw)


def _cp(sem=None, vmem=VMEM_LIMIT):
    if sem is None:
        return pltpu.CompilerParams(vmem_limit_bytes=vmem)
    return pltpu.CompilerParams(dimension_semantics=sem, vmem_limit_bytes=vmem)


def _sds(shape, dtype=F32):
    return jax.ShapeDtypeStruct(tuple(shape), dtype)


def _full(shape):
    nd = len(shape)
    return pl.BlockSpec(tuple(shape), lambda *_: (0,) * nd)


def _sigmoid(x):
    return jax.nn.sigmoid(x)


def _silu(x):
    return x * _sigmoid(x)


def _dsilu(x):
    s = _sigmoid(x)
    return s * (1.0 + x * (1.0 - s))


def _colsum(a):
    return jnp.sum(a, axis=0, keepdims=True)


def _dot(a, b, dn):
    return lax.dot_general(a.astype(BF16), b.astype(BF16), dn, preferred_element_type=F32)


NN = (((1,), (0,)), ((), ()))
NT = (((1,), (1,)), ((), ()))
TN = (((0,), (0,)), ((), ()))


def _mm_tile(t):
    best = 16
    for d in range(16, min(t, 1152) + 1, 16):
        if t % d == 0:
            best = d
    return best


def _mm(name, pairs, grid, out_shape, out_spec, dn):
    npairs = len(pairs)
    nk = grid[-1]
    kax = len(grid) - 1
    assert nk == 1 or out_shape.dtype == F32

    def body(*refs):
        ins = refs[:2 * npairs]
        o_ref = refs[2 * npairs]
        tot = None
        for p in range(npairs):
            d = _dot(ins[2 * p][...], ins[2 * p + 1][...], dn)
            tot = d if tot is None else tot + d
        if nk == 1:
            o_ref[...] = tot.astype(o_ref.dtype)
        else:
            k = pl.program_id(kax)

            @pl.when(k == 0)
            def _():
                o_ref[...] = tot

            @pl.when(k != 0)
            def _():
                o_ref[...] += tot

    args, in_specs = [], []
    for a, a_spec, b, b_spec in pairs:
        args += [a, b]
        in_specs += [a_spec, b_spec]
    sem = ("parallel",) * kax + ("arbitrary",)
    return _pcall(body, name=name, grid=grid, in_specs=in_specs, out_specs=out_spec, out_shape=out_shape,
                  compiler_params=_cp(sem))(*args)


def _mm_cols(name, a, w, out_dtype=F32):
    t, k = a.shape
    j, _, n = w.shape
    tm = _mm_tile(t)
    return _mm(name, [(a, pl.BlockSpec((tm, k), lambda i, jj, kk: (i, 0)),
                       w, pl.BlockSpec((None, k, n), lambda i, jj, kk: (jj, 0, 0)))],
               (t // tm, j, 1), _sds((t, j * n), out_dtype), pl.BlockSpec((tm, n), lambda i, jj, kk: (i, jj)), NN)


def _mm_cols_bwd(name, d, w):
    t = d.shape[0]
    j, k, n = w.shape
    tm = _mm_tile(t)
    return _mm(name, [(d, pl.BlockSpec((tm, n), lambda i, u, kk: (i, kk)),
                       w, pl.BlockSpec((None, k, n), lambda i, u, kk: (kk, 0, 0)))],
               (t // tm, 1, j), _sds((t, k)), pl.BlockSpec((tm, k), lambda i, u, kk: (i, 0)), NT)


def _mm_full(name, a, w, dn, tm=None):
    t, k = a.shape
    n = w.shape[1] if dn is NN else w.shape[0]
    tm = tm or _mm_tile(t)
    return _mm(name, [(a, pl.BlockSpec((tm, k), lambda i, u, kk: (i, 0)), w, _full(w.shape))],
               (t // tm, 1, 1), _sds((t, n)), pl.BlockSpec((tm, n), lambda i, u, kk: (i, 0)), dn)


def _wgrad_cols(name, a, b, j):
    t, k = a.shape
    n = b.shape[1] // j
    tt = _mm_tile(t)
    return _mm(name, [(a, pl.BlockSpec((tt, k), lambda jj, u, kk: (kk, 0)),
                       b, pl.BlockSpec((tt, n), lambda jj, u, kk: (kk, jj)))],
               (j, 1, t // tt), _sds((j, k, n)), pl.BlockSpec((None, k, n), lambda jj, u, kk: (jj, 0, 0)), TN)


def _wgrad_rows(name, a, blk, b):
    t, f = a.shape
    n = b.shape[1]
    tt = _mm_tile(t)
    return _mm(name, [(a, pl.BlockSpec((tt, blk), lambda jj, u, kk: (kk, jj)),
                       b, pl.BlockSpec((tt, n), lambda jj, u, kk: (kk, 0)))],
               (f // blk, 1, t // tt), _sds((f, n)), pl.BlockSpec((blk, n), lambda jj, u, kk: (jj, 0)), TN)


def _ffn_tiles(t, f):
    return _mm_tile(t) // 2, f // 2


def _ffn_up(name, h, wgt, wut):
    t, k = h.shape
    f = wgt.shape[0]
    tm, tn = _ffn_tiles(t, f)

    def body(h_ref, wg_ref, wu_ref, a_ref, gt_ref, up_ref):
        hv = h_ref[...]
        gt = _dot(hv, wg_ref[...], NT)
        up = _dot(hv, wu_ref[...], NT)
        a_ref[...] = (_silu(gt) * up).astype(BF16)
        gt_ref[...] = gt.astype(BF16)
        up_ref[...] = up.astype(BF16)

    wspec = pl.BlockSpec((tn, k), lambda i, jj: (jj, 0))
    ospec = pl.BlockSpec((tm, tn), lambda i, jj: (i, jj))
    o = _sds((t, f), BF16)
    return _pcall(body, name=name, grid=(t // tm, f // tn),
                  in_specs=[pl.BlockSpec((tm, k), lambda i, jj: (i, 0)), wspec, wspec],
                  out_specs=[ospec, ospec, ospec], out_shape=[o, o, o],
                  compiler_params=_cp(("parallel", "parallel")))(h, wgt, wut)


def _ffn_down_bwd(name, df, wd, gt, up):
    t, dm = df.shape
    f = wd.shape[0]
    tm, tn = _ffn_tiles(t, f)

    def body(df_ref, wd_ref, gt_ref, up_ref, dgt_ref, dup_ref):
        da = _dot(df_ref[...], wd_ref[...], NT)
        g = gt_ref[...].astype(F32)
        u = up_ref[...].astype(F32)
        dgt_ref[...] = (da * u * _dsilu(g)).astype(BF16)
        dup_ref[...] = (da * _silu(g)).astype(BF16)

    aspec = pl.BlockSpec((tm, tn), lambda i, jj: (i, jj))
    o = _sds((t, f), BF16)
    return _pcall(body, name=name, grid=(t // tm, f // tn),
                  in_specs=[pl.BlockSpec((tm, dm), lambda i, jj: (i, 0)),
                            pl.BlockSpec((tn, dm), lambda i, jj: (jj, 0)), aspec, aspec],
                  out_specs=[aspec, aspec], out_shape=[o, o],
                  compiler_params=_cp(("parallel", "parallel")))(df, wd, gt, up)


def _ffn_in_bwd(name, dgt, dup, wgt, wut):
    t, f = dgt.shape
    k = wgt.shape[1]
    tm, _ = _ffn_tiles(t, f)
    aspec = pl.BlockSpec((tm, f), lambda i, u, kk: (i, 0))
    wspec = pl.BlockSpec((f, k), lambda i, u, kk: (0, 0))
    return _mm(name, [(dgt, aspec, wgt, wspec), (dup, aspec, wut, wspec)], (t // tm, 1, 1), _sds((t, k)),
               pl.BlockSpec((tm, k), lambda i, u, kk: (i, 0)), NN)


def _modrow(ref, row, is_ctx):
    return jnp.where(is_ctx, ref[0, row:row + 1, :], ref[1, row:row + 1, :])


def _rnm(name, x, delta, mod_g, g_row, mod_n, sh_row, sc_row, nw, nct):
    t, dm = x.shape
    tm = ROW_TILE
    has = delta is not None

    def body(*refs):
        if has:
            x_ref, d_ref, mg_ref, m_ref, nw_ref, xo_ref, h_ref = refs
        else:
            x_ref, m_ref, nw_ref, h_ref = refs
        is_ctx = pl.program_id(0) < nct
        xv = x_ref[...]
        if has:
            xv = xv + _modrow(mg_ref, g_row, is_ctx) * d_ref[...]
            xo_ref[...] = xv
        r = lax.rsqrt(jnp.mean(xv * xv, axis=-1, keepdims=True) + EPS)
        hv = (xv * r * nw_ref[...]) * (1.0 + _modrow(m_ref, sc_row, is_ctx)) + _modrow(m_ref, sh_row, is_ctx)
        h_ref[...] = hv.astype(BF16)

    row = pl.BlockSpec((tm, dm), lambda i: (i, 0))
    ins = [x] + ([delta, mod_g] if has else []) + [mod_n, nw]
    in_specs = [row] + ([row, _full(mod_g.shape)] if has else []) + [_full(mod_n.shape), _full(nw.shape)]
    outs = ([_sds((t, dm))] if has else []) + [_sds((t, dm), BF16)]
    out_specs = ([row] if has else []) + [row]
    res = _pcall(body, name=name, grid=(t // tm,), in_specs=in_specs, out_specs=out_specs, out_shape=outs,
                 compiler_params=_cp(("parallel",)))(*ins)
    return res if has else (None, res[0])


def _bnm(name, xn, dh, dup, yprev, mod_n, sh_row, sc_row, mod_g, g_row, nw, nct):
    t, dm = xn.shape
    tm = ROW_TILE
    has = yprev is not None

    def body(*refs):
        if has:
            x_ref, dh_ref, du_ref, y_ref, mn_ref, mg_ref, nw_ref, dx_ref, dd_ref, s_ref = refs
        else:
            x_ref, dh_ref, du_ref, mn_ref, nw_ref, dx_ref, s_ref = refs
        i = pl.program_id(0)
        is_ctx = i < nct

        @pl.when(i == 0)
        def _():
            s_ref[...] = jnp.zeros_like(s_ref)

        xv = x_ref[...]
        r = lax.rsqrt(jnp.mean(xv * xv, axis=-1, keepdims=True) + EPS)
        xh = xv * r
        w = nw_ref[...]
        sc1 = 1.0 + _modrow(mn_ref, sc_row, is_ctx)
        dhv = dh_ref[...]
        dxh = dhv * sc1 * w
        dx = r * (dxh - xh * jnp.mean(dxh * xh, axis=-1, keepdims=True)) + du_ref[...]
        dx_ref[...] = dx
        parts = [_colsum(dhv), _colsum(dhv * (xh * w))]
        if has:
            dd_ref[...] = (_modrow(mg_ref, g_row, is_ctx) * dx).astype(BF16)
            parts.append(_colsum(dx * y_ref[...]))
        else:
            parts.append(jnp.zeros((1, dm), F32))
        upd = jnp.concatenate(parts + [jnp.zeros((5, dm), F32)], axis=0)
        dnw = jnp.concatenate([jnp.zeros((3, dm), F32), _colsum(dhv * sc1 * xh), jnp.zeros((4, dm), F32)], axis=0)

        @pl.when(is_ctx)
        def _():
            s_ref[0] += upd
            s_ref[1] += dnw

        @pl.when(jnp.logical_not(is_ctx))
        def _():
            s_ref[1] += upd + dnw

    row = pl.BlockSpec((tm, dm), lambda i: (i, 0))
    ins = [xn, dh, dup] + ([yprev] if has else []) + [mod_n] + ([mod_g] if has else []) + [nw]
    in_specs = ([row, row, row] + ([row] if has else []) + [_full(mod_n.shape)]
                + ([_full(mod_g.shape)] if has else []) + [_full(nw.shape)])
    outs = [_sds((t, dm))] + ([_sds((t, dm), BF16)] if has else []) + [_sds((2, 8, dm))]
    out_specs = [row] + ([row] if has else []) + [_full((2, 8, dm))]
    res = _pcall(body, name=name, grid=(t // tm,), in_specs=in_specs, out_specs=out_specs, out_shape=outs,
                 compiler_params=_cp(("arbitrary",)))(*ins)
    return res if has else (res[0], None, res[1])


def _fin(name, x1, f, mod, g_row, fw, target, nct):
    t, dm = x1.shape
    tm = ROW_TILE

    def body(x_ref, f_ref, m_ref, fw_ref, t_ref, loss_ref, dx_ref, dd_ref, s_ref):
        i = pl.program_id(0)

        @pl.when(i == 0)
        def _():
            s_ref[...] = jnp.zeros_like(s_ref)
            loss_ref[...] = jnp.zeros_like(loss_ref)

        @pl.when(i < nct)
        def _():
            dx_ref[...] = jnp.zeros_like(dx_ref)
            dd_ref[...] = jnp.zeros_like(dd_ref)

        @pl.when(i >= nct)
        def _():
            g = m_ref[1, g_row:g_row + 1, :]
            fv = f_ref[...]
            xv = x_ref[...] + g * fv
            r = lax.rsqrt(jnp.mean(xv * xv, axis=-1, keepdims=True) + EPS)
            xh = xv * r
            w = fw_ref[...]
            err = xh * w - t_ref[...]
            loss_ref[...] += 0.5 * jnp.sum(err * err) / dm
            dout = err * (1.0 / dm)
            dxh = dout * w
            dx = r * (dxh - xh * jnp.mean(dxh * xh, axis=-1, keepdims=True))
            dx_ref[...] = dx
            dd_ref[...] = (g * dx).astype(BF16)
            s_ref[...] += jnp.concatenate([_colsum(dx * fv), _colsum(dout * xh), jnp.zeros((6, dm), F32)], axis=0)

    row = pl.BlockSpec((tm, dm), lambda i: (i, 0))
    trow = pl.BlockSpec((tm, dm), lambda i: (jnp.maximum(i - nct, 0), 0))
    return _pcall(body, name=name, grid=(t // tm,),
                  in_specs=[row, row, _full(mod.shape), _full(fw.shape), trow],
                  out_specs=[_full((8, 128)), row, row, _full((8, dm))],
                  out_shape=[_sds((8, 128)), _sds((t, dm)), _sds((t, dm), BF16), _sds((8, dm))],
                  compiler_params=_cp(("arbitrary",)))(x1, f, mod, fw, target)


def _rope_tables(t, lc):
    l = t - lc
    rows = l // GRID_W
    grid_r = jnp.broadcast_to(jnp.arange(rows, dtype=F32)[:, None], (rows, GRID_W)).reshape(-1)
    grid_c = jnp.broadcast_to(jnp.arange(GRID_W, dtype=F32)[None, :], (rows, GRID_W)).reshape(-1)

    def angles(p_seq, p_row, p_col):
        parts = []
        for p, n in zip((p_seq, p_row, p_col), ROPE_PAIRS):
            freq = ROPE_BASE ** (-jnp.arange(n, dtype=F32) / n)
            parts.append(p[:, None] * freq[None, :])
        return jnp.concatenate(parts, axis=-1)

    zc = jnp.zeros((lc,), F32)
    ang = jnp.concatenate([angles(jnp.arange(lc, dtype=F32), zc, zc),
                           angles(jnp.full((l,), lc, F32), grid_r, grid_c)], axis=0)
    cos, sin = jnp.cos(ang), jnp.sin(ang)
    return jnp.concatenate([cos, cos], axis=-1), jnp.concatenate([-sin, sin], axis=-1)


def _rope(u, cs, sn):
    return u * cs + pltpu.roll(u, HEAD_DIM // 2, 1) * sn


def _rope_t(d, cs, sn):
    return d * cs + pltpu.roll(d * sn, HEAD_DIM // 2, 1)


def _even_qkv(name, p, cs, sn):
    t = p.shape[0]
    tm = ROW_TILE
    w = HEADS * HEAD_DIM
    scale = HEAD_DIM ** -0.5

    def body(q_ref, k_ref, v_ref, cs_ref, sn_ref, qo_ref, ko_ref, vo_ref):
        c, s = cs_ref[...], sn_ref[...]
        for h in range(HEADS):
            sl = slice(h * HEAD_DIM, (h + 1) * HEAD_DIM)
            qo_ref[:, sl] = (_rope(q_ref[:, sl], c, s) * scale).astype(BF16)
            ko_ref[:, sl] = _rope(k_ref[:, sl], c, s).astype(BF16)
        vo_ref[...] = v_ref[...].astype(BF16)

    col = lambda j: pl.BlockSpec((tm, w), lambda i: (i, j))
    tab = pl.BlockSpec((tm, HEAD_DIM), lambda i: (i, 0))
    o = _sds((t, w), BF16)
    return _pcall(body, name=name, grid=(t // tm,), in_specs=[col(0), col(1), col(2), tab, tab],
                  out_specs=[col(0)] * 3, out_shape=[o, o, o], compiler_params=_cp(("parallel",)))(p, p, p, cs, sn)


def _log_sigmoid_row(x):
    e = jnp.exp(-jnp.abs(x))
    l1p = jnp.where(e < 0.01, e * (1.0 - e * (0.5 - e * (1.0 / 3.0))), jnp.log(1.0 + e))
    return jnp.minimum(x, 0.0) - l1p


def _ret_tables(lgb_ref, dm_ref, xi_ref, zt_ref):
    ri = lax.broadcasted_iota(jnp.int32, (CHUNK, CHUNK), 0).astype(F32)
    ci = lax.broadcasted_iota(jnp.int32, (CHUNK, CHUNK), 1).astype(F32)
    for d in range(2):
        for h in range(HEADS):
            idx = d * HEADS + h
            lg = _log_sigmoid_row(lgb_ref[idx:idx + 1, :])
            if d == 0:
                e, mask = ri - ci, ri >= ci
                xe, ze = ri + 1.0, (CHUNK - 1.0) - ri
            else:
                e, mask = ci - ri - 1.0, ci > ri
                xe, ze = (CHUNK - 1.0) - ri, ri
            dm_ref[idx] = jnp.where(mask, jnp.exp(lg * jnp.where(mask, e, 0.0)), 0.0)
            xi_ref[idx] = jnp.exp(lg * xe)
            zt_ref[idx] = jnp.exp(lg * ze)


def _ret_exponents(d):
    ri = lax.broadcasted_iota(jnp.int32, (CHUNK, CHUNK), 0).astype(F32)
    ci = lax.broadcasted_iota(jnp.int32, (CHUNK, CHUNK), 1).astype(F32)
    if d == 0:
        return ri - ci, ri + 1.0, (CHUNK - 1.0) - ri
    return ci - ri - 1.0, (CHUNK - 1.0) - ri, ri


def _bwd_chunk(n, ncc, nc):
    return jnp.where(n < ncc, ncc - 1 - n, nc - 1 - (n - ncc))


def _retention_fwd(name, q, k, v, lgb, lc):
    t, w = q.shape
    nc, ncc = t // CHUNK, lc // CHUNK
    nh = 2 * HEADS

    def body(qf_ref, kf_ref, vf_ref, qb_ref, kb_ref, vb_ref, lgb_ref, of_ref, ob_ref, ss_ref,
             s_ref, dm_ref, xi_ref, zt_ref):
        n = pl.program_id(0)

        @pl.when(n == 0)
        def _():
            s_ref[...] = jnp.zeros_like(s_ref)
            _ret_tables(lgb_ref, dm_ref, xi_ref, zt_ref)

        for d in range(2):
            q_ref, k_ref, v_ref, o_ref = (qf_ref, kf_ref, vf_ref, of_ref) if d == 0 else (qb_ref, kb_ref, vb_ref, ob_ref)
            for h in range(HEADS):
                idx = d * HEADS + h
                sl = slice(h * HEAD_DIM, (h + 1) * HEAD_DIM)
                qv, kv, vv = q_ref[:, sl], k_ref[:, sl], v_ref[:, sl]
                s = s_ref[idx]
                ss_ref[idx] = s
                a = _dot(qv, kv, NT) * dm_ref[idx]
                o = _dot(a, vv, NN) + _dot(qv.astype(F32) * xi_ref[idx], s, NN)
                o_ref[:, sl] = o
                gc = jnp.exp(_log_sigmoid_row(lgb_ref[idx:idx + 1, :]) * float(CHUNK))
                s_ref[idx] = gc * s + _dot(kv.astype(F32) * zt_ref[idx], vv, TN)

    fspec = pl.BlockSpec((CHUNK, w), lambda n: (n, 0))
    bspec = pl.BlockSpec((CHUNK, w), lambda n: (_bwd_chunk(n, ncc, nc), 0))
    tab = pltpu.VMEM((nh, CHUNK, CHUNK), F32)
    return _pcall(body, name=name, grid=(nc,),
                  in_specs=[fspec] * 3 + [bspec] * 3 + [_full((nh, HEAD_DIM))],
                  out_specs=[fspec, bspec, pl.BlockSpec((None, nh, CHUNK, CHUNK), lambda n: (n, 0, 0, 0))],
                  out_shape=[_sds((t, w)), _sds((t, w)), _sds((nc, nh, CHUNK, CHUNK))],
                  scratch_shapes=[tab, tab, tab, tab],
                  compiler_params=_cp(("arbitrary",)))(q, k, v, q, k, v, lgb)


def _retention_bwd(name, q, k, v, do, ss, lgb, lc):
    t, w = q.shape
    nc, ncc = t // CHUNK, lc // CHUNK
    nh = 2 * HEADS

    def body(qf_ref, kf_ref, vf_ref, gf_ref, qb_ref, kb_ref, vb_ref, gb_ref, ss_ref, lgb_ref,
             dqf_ref, dkf_ref, dvf_ref, dqb_ref, dkb_ref, dvb_ref, dl_ref,
             ds_ref, dm_ref, xi_ref, zt_ref, acc_ref):
        n = pl.program_id(0)

        @pl.when(n == 0)
        def _():
            ds_ref[...] = jnp.zeros_like(ds_ref)
            acc_ref[...] = jnp.zeros_like(acc_ref)
            _ret_tables(lgb_ref, dm_ref, xi_ref, zt_ref)

        for d in range(2):
            if d == 0:
                q_ref, k_ref, v_ref, g_ref, dq_ref, dk_ref, dv_ref = qf_ref, kf_ref, vf_ref, gf_ref, dqf_ref, dkf_ref, dvf_ref
            else:
                q_ref, k_ref, v_ref, g_ref, dq_ref, dk_ref, dv_ref = qb_ref, kb_ref, vb_ref, gb_ref, dqb_ref, dkb_ref, dvb_ref
            ee, xe, ze = _ret_exponents(d)
            for h in range(HEADS):
                idx = d * HEADS + h
                sl = slice(h * HEAD_DIM, (h + 1) * HEAD_DIM)
                qv, kv, vv, gv = q_ref[:, sl], k_ref[:, sl], v_ref[:, sl], g_ref[:, sl]
                s = ss_ref[idx]
                dsp = ds_ref[idx]
                dmat, xi, zt = dm_ref[idx], xi_ref[idx], zt_ref[idx]
                qf32, kf32 = qv.astype(F32), kv.astype(F32)
                a = _dot(qv, kv, NT) * dmat
                dar = _dot(gv, vv, NT)
                da = dar * dmat
                t1 = _dot(gv, s, NT)
                t2 = _dot(vv, dsp, NT)
                dq_ref[:, sl] = _dot(da, kv, NN) + xi * t1
                dk_ref[:, sl] = _dot(da, qv, TN) + zt * t2
                dv_ref[:, sl] = _dot(a, gv, TN) + _dot(kf32 * zt, dsp, NN)
                gc = jnp.exp(_log_sigmoid_row(lgb_ref[idx:idx + 1, :]) * float(CHUNK))
                ds_ref[idx] = gc * dsp + _dot(qf32 * xi, gv, TN)
                acc_ref[idx] += (ee * a * dar + xe * xi * qf32 * t1 + ze * zt * kf32 * t2
                                 + (float(CHUNK) * gc) * dsp * s)

        @pl.when(n == nc - 1)
        def _():
            for idx in range(nh):
                tot = jnp.sum(acc_ref[idx])
                dl_ref[idx:idx + 1, :] = tot * _sigmoid(-lgb_ref[idx:idx + 1, :])

    fmap = lambda n: (nc - 1 - n, 0)
    bmap = lambda n: (_bwd_chunk(nc - 1 - n, ncc, nc), 0)
    fspec = pl.BlockSpec((CHUNK, w), fmap)
    bspec = pl.BlockSpec((CHUNK, w), bmap)
    tab = pltpu.VMEM((nh, CHUNK, CHUNK), F32)
    o = _sds((t, w))
    return _pcall(body, name=name, grid=(nc,),
                  in_specs=[fspec] * 4 + [bspec] * 4
                  + [pl.BlockSpec((None, nh, CHUNK, CHUNK), lambda n: (nc - 1 - n, 0, 0, 0)), _full((nh, HEAD_DIM))],
                  out_specs=[fspec] * 3 + [bspec] * 3 + [_full((nh, HEAD_DIM))],
                  out_shape=[o] * 6 + [_sds((nh, HEAD_DIM))],
                  scratch_shapes=[tab, tab, tab, tab, tab],
                  compiler_params=_cp(("arbitrary",)))(q, k, v, do, q, k, v, do, ss, lgb)


def _halo_specs(tm, halo, t, width, col):
    hb = tm // halo
    last = t // halo - 1
    prev = pl.BlockSpec((halo, width), lambda i: (jnp.maximum(i * hb - 1, 0), col))
    nxt = pl.BlockSpec((halo, width), lambda i: (jnp.minimum((i + 1) * hb, last), col))
    return prev, nxt


def _halo_valid(i, nct, nt):
    vp = jnp.logical_and(i != 0, i != nct)
    vn = jnp.logical_and(i != nct - 1, i != nt - 1)
    return vp, vn


def _fill_window(win_ref, prev, cur, nxt, vp, vn, halo, tm):
    win_ref[0:halo, :] = jnp.where(vp, prev, 0.0)
    win_ref[halo:halo + tm, :] = cur
    win_ref[halo + tm:halo + tm + halo, :] = jnp.where(vn, nxt, 0.0)


CONV_SUB = 64


def _conv_taps(win_ref, w_ref, tm, flip):
    outs = []
    for r0 in range(0, tm, CONV_SUB):
        acc = None
        for kk in range(CONV_K):
            wk = (CONV_K - 1 - kk) if flip else kk
            term = w_ref[wk:wk + 1, :] * win_ref[r0 + kk + 1:r0 + kk + 1 + CONV_SUB, :]
            acc = term if acc is None else acc + term
        outs.append(acc)
    return jnp.concatenate(outs, axis=0)


def _head_norm(y):
    r = lax.rsqrt(jnp.mean(y * y, axis=-1, keepdims=True) + EPS)
    return y * r, r


def _ln_stats(y):
    mu = jnp.mean(y, axis=-1, keepdims=True)
    yc = y - mu
    rs = lax.rsqrt(jnp.mean(yc * yc, axis=-1, keepdims=True) + EPS)
    return yc * rs, rs


def _ln_bwd(dyh, yh, rs):
    return rs * (dyh - jnp.mean(dyh, axis=-1, keepdims=True) - yh * jnp.mean(dyh * yh, axis=-1, keepdims=True))


def _even_mix(name, p, of, ob, cw, lnw, lnb, nct):
    t = p.shape[0]
    tm, halo = ROW_TILE, CONV_HALO
    nt = t // tm
    w = HEADS * HEAD_DIM

    def body(g_ref, a_ref, gb_ref, ap_ref, gbp_ref, an_ref, gbn_ref, of_ref, ob_ref, cw_ref, lw_ref, lb_ref,
             mix_ref, yc_ref, win_ref):
        i = pl.program_id(0)
        vp, vn = _halo_valid(i, nct, nt)
        glu = lambda a, b: a * _sigmoid(b)
        _fill_window(win_ref, glu(ap_ref[...], gbp_ref[...]), glu(a_ref[...], gb_ref[...]),
                     glu(an_ref[...], gbn_ref[...]), vp, vn, halo, tm)
        yc = _conv_taps(win_ref, cw_ref, tm, False)
        yc_ref[...] = yc
        yh, _ = _ln_stats(yc)
        mix_ref[:, w:2 * w] = _silu(yh * lw_ref[...] + lb_ref[...]).astype(BF16)
        for h in range(HEADS):
            sl = slice(h * HEAD_DIM, (h + 1) * HEAD_DIM)
            yn, _ = _head_norm(of_ref[:, sl] + ob_ref[:, sl])
            mix_ref[:, sl] = (_silu(g_ref[:, sl]) * yn).astype(BF16)

    col = lambda j: pl.BlockSpec((tm, w), lambda i: (i, j))
    ap, an = _halo_specs(tm, halo, t, w, 4)
    gp, gn = _halo_specs(tm, halo, t, w, 5)
    row = pl.BlockSpec((tm, w), lambda i: (i, 0))
    return _pcall(body, name=name, grid=(nt,),
                  in_specs=[col(3), col(4), col(5), ap, gp, an, gn, row, row,
                            _full(cw.shape), _full(lnw.shape), _full(lnb.shape)],
                  out_specs=[pl.BlockSpec((tm, 2 * w), lambda i: (i, 0)), row],
                  out_shape=[_sds((t, 2 * w), BF16), _sds((t, w))],
                  scratch_shapes=[pltpu.VMEM((tm + 2 * halo, w), F32)],
                  compiler_params=_cp(("parallel",)))(p, p, p, p, p, p, p, of, ob, cw, lnw, lnb)


def _even_mix_bwd1(name, dmix, p, of, ob, yc, lnw, lnb):
    t = p.shape[0]
    tm = ROW_TILE
    w = HEADS * HEAD_DIM

    def body(dr_ref, dc_ref, g_ref, of_ref, ob_ref, yc_ref, lw_ref, lb_ref, do_ref, dg_ref, dyc_ref, s_ref):
        @pl.when(pl.program_id(0) == 0)
        def _():
            s_ref[...] = jnp.zeros_like(s_ref)

        for h in range(HEADS):
            sl = slice(h * HEAD_DIM, (h + 1) * HEAD_DIM)
            yn, r = _head_norm(of_ref[:, sl] + ob_ref[:, sl])
            gv = g_ref[:, sl]
            dr = dr_ref[:, sl]
            dg_ref[:, sl] = (dr * yn * _dsilu(gv)).astype(BF16)
            dyn = dr * _silu(gv)
            do_ref[:, sl] = (r * (dyn - yn * jnp.mean(dyn * yn, axis=-1, keepdims=True))).astype(BF16)
        yh, rs = _ln_stats(yc_ref[...])
        lw = lw_ref[...]
        dlo = dc_ref[...] * _dsilu(yh * lw + lb_ref[...])
        dyc_ref[...] = _ln_bwd(dlo * lw, yh, rs)
        s_ref[...] += jnp.concatenate([_colsum(dlo * yh), _colsum(dlo), jnp.zeros((6, w), F32)], axis=0)

    col = lambda j: pl.BlockSpec((tm, w), lambda i: (i, j))
    row = pl.BlockSpec((tm, w), lambda i: (i, 0))
    return _pcall(body, name=name, grid=(t // tm,),
                  in_specs=[col(0), col(1), col(3), row, row, row, _full(lnw.shape), _full(lnb.shape)],
                  out_specs=[row, row, row, _full((8, w))],
                  out_shape=[_sds((t, w), BF16), _sds((t, w), BF16), _sds((t, w)), _sds((8, w))],
                  compiler_params=_cp(("arbitrary",)))(dmix, dmix, p, of, ob, yc, lnw, lnb)


def _even_conv_bwd(name, dyc, p, cw, nct):
    t = p.shape[0]
    tm, halo = ROW_TILE, CONV_HALO
    nt = t // tm
    w = HEADS * HEAD_DIM

    def body(d_ref, dp_ref, dn_ref, a_ref, gb_ref, ap_ref, gbp_ref, an_ref, gbn_ref, cw_ref,
             da_ref, dgb_ref, dw_ref, dwin_ref, uwin_ref):
        i = pl.program_id(0)

        @pl.when(i == 0)
        def _():
            dw_ref[...] = jnp.zeros_like(dw_ref)

        vp, vn = _halo_valid(i, nct, nt)
        glu = lambda a, b: a * _sigmoid(b)
        dcur = d_ref[...]
        _fill_window(dwin_ref, dp_ref[...], dcur, dn_ref[...], vp, vn, halo, tm)
        _fill_window(uwin_ref, glu(ap_ref[...], gbp_ref[...]), glu(a_ref[...], gb_ref[...]),
                     glu(an_ref[...], gbn_ref[...]), vp, vn, halo, tm)
        du = _conv_taps(dwin_ref, cw_ref, tm, True)
        av = a_ref[...]
        sg = _sigmoid(gb_ref[...])
        da_ref[...] = (du * sg).astype(BF16)
        dgb_ref[...] = (du * av * sg * (1.0 - sg)).astype(BF16)
        rows = [_colsum(dcur * uwin_ref[kk + 1:kk + 1 + tm, :]) for kk in range(CONV_K)]
        dw_ref[...] += jnp.concatenate(rows + [jnp.zeros((1, w), F32)], axis=0)

    col = lambda j: pl.BlockSpec((tm, w), lambda i: (i, j))
    row = pl.BlockSpec((tm, w), lambda i: (i, 0))
    dp, dn = _halo_specs(tm, halo, t, w, 0)
    ap, an = _halo_specs(tm, halo, t, w, 4)
    gp, gn = _halo_specs(tm, halo, t, w, 5)
    win = pltpu.VMEM((tm + 2 * halo, w), F32)
    return _pcall(body, name=name, grid=(nt,),
                  in_specs=[row, dp, dn, col(4), col(5), ap, gp, an, gn, _full(cw.shape)],
                  out_specs=[row, row, _full((CONV_K + 1, w))],
                  out_shape=[_sds((t, w), BF16), _sds((t, w), BF16), _sds((CONV_K + 1, w))],
                  scratch_shapes=[win, win],
                  compiler_params=_cp(("arbitrary",)))(dyc, dyc, dyc, p, p, p, p, p, p, cw)


def _even_dp(name, dqs, dks, dvs, dg, da, dgb, cs, sn):
    t, w = dg.shape
    tm = ROW_TILE
    scale = HEAD_DIM ** -0.5

    def body(dqf_ref, dqb_ref, dkf_ref, dkb_ref, dvf_ref, dvb_ref, dg_ref, da_ref, dgb_ref, cs_ref, sn_ref, dp_ref):
        c, s = cs_ref[...], sn_ref[...]
        for h in range(HEADS):
            sl = slice(h * HEAD_DIM, (h + 1) * HEAD_DIM)
            dp_ref[:, sl] = (_rope_t(dqf_ref[:, sl] + dqb_ref[:, sl], c, s) * scale).astype(BF16)
            dp_ref[:, w + h * HEAD_DIM:w + (h + 1) * HEAD_DIM] = _rope_t(dkf_ref[:, sl] + dkb_ref[:, sl], c, s).astype(BF16)
        dp_ref[:, 2 * w:3 * w] = (dvf_ref[...] + dvb_ref[...]).astype(BF16)
        dp_ref[:, 3 * w:4 * w] = dg_ref[...]
        dp_ref[:, 4 * w:5 * w] = da_ref[...]
        dp_ref[:, 5 * w:6 * w] = dgb_ref[...]

    row = pl.BlockSpec((tm, w), lambda i: (i, 0))
    tab = pl.BlockSpec((tm, HEAD_DIM), lambda i: (i, 0))
    return _pcall(body, name=name, grid=(t // tm,), in_specs=[row] * 9 + [tab, tab],
                  out_specs=pl.BlockSpec((tm, 6 * w), lambda i: (i, 0)), out_shape=_sds((t, 6 * w), BF16),
                  compiler_params=_cp(("parallel",)))(dqs[0], dqs[1], dks[0], dks[1], dvs[0], dvs[1], dg, da, dgb, cs, sn)


GROUPS = 4
GC = 128
INV_SQRT2 = 0.7071067811865476
INV_SQRT_2PI = 0.3989422804014327


def _gelu(x):
    return 0.5 * x * (1.0 + lax.erf(x * INV_SQRT2))


def _dgelu(x):
    return 0.5 * (1.0 + lax.erf(x * INV_SQRT2)) + x * jnp.exp(-0.5 * x * x) * INV_SQRT_2PI


def _pool_count(i, nct, lc, t, tm, rows, row0, left, right):
    is_ctx = i < nct
    seg_start = jnp.where(is_ctx, 0, lc)
    seg_len = jnp.where(is_ctx, lc, t - lc)
    pos = i * tm + row0 - seg_start + lax.broadcasted_iota(jnp.int32, (rows, GC), 0)
    cnt = jnp.minimum(pos + right, seg_len - 1) - jnp.maximum(pos - left, 0) + 1
    return jnp.maximum(cnt, 1).astype(F32)


def _spatial_gate(vln, sgw_ref, sgb_ref, tm):
    cols = []
    for g in range(GROUPS):
        sl = slice(g * GC, (g + 1) * GC)
        parts = [_dot(sgw_ref[g], vln[r0:r0 + CHUNK, sl], NN) + sgb_ref[g] for r0 in range(0, tm, CHUNK)]
        cols.append(jnp.concatenate(parts, axis=0))
    return jnp.concatenate(cols, axis=1)


def _odd_mix(name, p, pw, pscale, lnw, lnb, sgw, sgb, nct, lc):
    t = p.shape[0]
    tm, halo = ROW_TILE, POOL_HALO
    nt = t // tm
    w = GROUPS * GC

    def body(pc_ref, pp_ref, pn_ref, pu_ref, pv_ref, pw_ref, ps_ref, lw_ref, lb_ref, sgw_ref, sgb_ref,
             mix_ref, m_ref, win_ref):
        i = pl.program_id(0)
        vp, vn = _halo_valid(i, nct, nt)
        pc = pc_ref[...]
        _fill_window(win_ref, pp_ref[...], pc, pn_ref[...], vp, vn, halo, tm)
        for g, wd in enumerate(POOL_WINDOWS):
            sl = slice(g * GC, (g + 1) * GC)
            left = wd // 2
            right = wd - 1 - left
            s = None
            for o in range(-left, right + 1):
                term = win_ref[halo + o:halo + o + tm, sl]
                s = term if s is None else s + term
            mg = s / _pool_count(i, nct, lc, t, tm, tm, 0, left, right) - pc[:, sl]
            m_ref[:, sl] = mg
            mix_ref[:, sl] = (_dot(mg, pw_ref[g], NN) * ps_ref[:, sl]).astype(BF16)
        u = _gelu(pu_ref[...])
        vh, _ = _ln_stats(_gelu(pv_ref[...]))
        s = _spatial_gate(vh * lw_ref[...] + lb_ref[...], sgw_ref, sgb_ref, tm)
        mix_ref[:, w:2 * w] = (u * s).astype(BF16)

    col = lambda j: pl.BlockSpec((tm, w), lambda i: (i, j))
    pp, pn = _halo_specs(tm, halo, t, w, 0)
    return _pcall(body, name=name, grid=(nt,),
                  in_specs=[col(0), pp, pn, col(1), col(2), _full(pw.shape), _full(pscale.shape),
                            _full(lnw.shape), _full(lnb.shape), _full(sgw.shape), _full(sgb.shape)],
                  out_specs=[pl.BlockSpec((tm, 2 * w), lambda i: (i, 0)), col(0)],
                  out_shape=[_sds((t, 2 * w), BF16), _sds((t, w))],
                  scratch_shapes=[pltpu.VMEM((tm + 2 * halo, w), F32)],
                  compiler_params=_cp(("parallel",)))(p, p, p, p, p, pw, pscale, lnw, lnb, sgw, sgb)


def _odd_mix_bwd1(name, dmix, p, m, pw, pscale, lnw, lnb, sgw, sgb):
    t = p.shape[0]
    tm = ROW_TILE
    w = GROUPS * GC

    def body(dpo_ref, dsg_ref, pu_ref, pv_ref, m_ref, pw_ref, ps_ref, lw_ref, lb_ref, sgw_ref, sgb_ref,
             dm_ref, dpd_ref, vec_ref, dpw_ref, dsgw_ref, dsgb_ref):
        @pl.when(pl.program_id(0) == 0)
        def _():
            vec_ref[...] = jnp.zeros_like(vec_ref)
            dpw_ref[...] = jnp.zeros_like(dpw_ref)
            dsgw_ref[...] = jnp.zeros_like(dsgw_ref)
            dsgb_ref[...] = jnp.zeros_like(dsgb_ref)

        dscale = []
        for g in range(GROUPS):
            sl = slice(g * GC, (g + 1) * GC)
            mg = m_ref[:, sl]
            dpo = dpo_ref[:, sl]
            dscale.append(_colsum(dpo * _dot(mg, pw_ref[g], NN)))
            dpo = dpo * ps_ref[:, sl]
            dm_ref[:, sl] = _dot(dpo, pw_ref[g], NT)
            dpw_ref[g] += _dot(mg, dpo, TN)
        pu, pv = pu_ref[...], pv_ref[...]
        u = _gelu(pu)
        vh, rs = _ln_stats(_gelu(pv))
        lw = lw_ref[...]
        vln = vh * lw + lb_ref[...]
        s = _spatial_gate(vln, sgw_ref, sgb_ref, tm)
        dsg = dsg_ref[...]
        dpd_ref[:, 0:w] = (dsg * s * _dgelu(pu)).astype(BF16)
        ds = dsg * u
        cols = []
        for g in range(GROUPS):
            sl = slice(g * GC, (g + 1) * GC)
            parts = []
            for r0 in range(0, tm, CHUNK):
                dsc = ds[r0:r0 + CHUNK, sl]
                parts.append(_dot(sgw_ref[g], dsc, TN))
                dsgw_ref[g] += _dot(dsc, vln[r0:r0 + CHUNK, sl], NT)
                dsgb_ref[g] += dsc
            cols.append(jnp.concatenate(parts, axis=0))
        dvln = jnp.concatenate(cols, axis=1)
        dpd_ref[:, w:2 * w] = (_ln_bwd(dvln * lw, vh, rs) * _dgelu(pv)).astype(BF16)
        vec_ref[...] += jnp.concatenate([jnp.concatenate(dscale, axis=1), _colsum(dvln * vh), _colsum(dvln),
                                         jnp.zeros((5, w), F32)], axis=0)

        @pl.when(pl.program_id(0) == t // tm - 1)
        def _():
            for g in range(GROUPS):
                dsgb_ref[g] = jnp.broadcast_to(jnp.sum(dsgb_ref[g], axis=1, keepdims=True), (GC, GC))

    col = lambda j: pl.BlockSpec((tm, w), lambda i: (i, j))
    mat = _full((GROUPS, GC, GC))
    return _pcall(body, name=name, grid=(t // tm,),
                  in_specs=[col(0), col(1), col(1), col(2), col(0), _full(pw.shape), _full(pscale.shape),
                            _full(lnw.shape), _full(lnb.shape), _full(sgw.shape), _full(sgb.shape)],
                  out_specs=[col(0), pl.BlockSpec((tm, 2 * w), lambda i: (i, 0)), _full((8, w)), mat, mat, mat],
                  out_shape=[_sds((t, w)), _sds((t, 2 * w), BF16), _sds((8, w)),
                             _sds((GROUPS, GC, GC)), _sds((GROUPS, GC, GC)), _sds((GROUPS, GC, GC))],
                  compiler_params=_cp(("arbitrary",)))(dmix, dmix, p, p, m, pw, pscale, lnw, lnb, sgw, sgb)


def _odd_dp(name, dm, dpd, nct, lc):
    t, w = dm.shape
    tm, halo = ROW_TILE, POOL_HALO
    nt = t // tm

    def body(d_ref, dp_ref, dn_ref, dpd_ref, o_ref, win_ref):
        i = pl.program_id(0)
        vp, vn = _halo_valid(i, nct, nt)
        dcur = d_ref[...]
        _fill_window(win_ref, dp_ref[...], dcur, dn_ref[...], vp, vn, halo, tm)
        for g, wd in enumerate(POOL_WINDOWS):
            sl = slice(g * GC, (g + 1) * GC)
            left = wd // 2
            right = wd - 1 - left
            win_ref[:, sl] = win_ref[:, sl] / _pool_count(i, nct, lc, t, tm, tm + 2 * halo, -halo, left, right)
            s = None
            for o in range(-right, left + 1):
                term = win_ref[halo + o:halo + o + tm, sl]
                s = term if s is None else s + term
            o_ref[:, sl] = (s - dcur[:, sl]).astype(BF16)
        o_ref[:, w:3 * w] = dpd_ref[...]

    row = pl.BlockSpec((tm, w), lambda i: (i, 0))
    pp, pn = _halo_specs(tm, halo, t, w, 0)
    return _pcall(body, name=name, grid=(nt,),
                  in_specs=[row, pp, pn, pl.BlockSpec((tm, 2 * w), lambda i: (i, 0))],
                  out_specs=pl.BlockSpec((tm, 3 * w), lambda i: (i, 0)), out_shape=_sds((t, 3 * w), BF16),
                  scratch_shapes=[pltpu.VMEM((tm + 2 * halo, w), F32)],
                  compiler_params=_cp(("parallel",)))(dm, dm, dm, dpd)


def _place():
    x, y, c = lax.axis_index("x"), lax.axis_index("y"), lax.axis_index("c")
    chips = [(1 - x, y), (x, 1 - y), (1 - x, 1 - y)]
    return x, y, c, chips


def _chip_index(cx, cy):
    return 2 * cx + cy


def _all_gather8(name, blk):
    m_per, n = blk.shape

    def body(x_ref, out_ref, send_sems, recv_sems, local_sem):
        x, y, c, chips = _place()
        me, sibling = (x, y, c), (x, y, 1 - c)

        def rows(px, py, pc):
            return out_ref.at[pl.ds((4 * px + 2 * py + pc) * m_per, m_per), :]

        def copy(k, block, to, src=None):
            return pltpu.make_async_remote_copy(
                src_ref=rows(*block) if src is None else src, dst_ref=rows(*block),
                send_sem=send_sems.at[k], recv_sem=recv_sems.at[k], device_id=to, device_id_type=MESH)

        mine = pltpu.make_async_copy(x_ref, rows(*me), local_sem)
        mine.start()
        first = [copy(0, me, sibling, src=x_ref)]
        first += [copy(1 + j, me, (*chip, c), src=x_ref) for j, chip in enumerate(chips)]
        for cp in first:
            cp.start()
        passed = [copy(4 + j, (*chip, c), sibling) for j, chip in enumerate(chips)]
        for j, chip in enumerate(chips):
            copy(1 + j, (*chip, c), me).wait_recv()
            passed[j].start()
        copy(0, sibling, me).wait_recv()
        for j, chip in enumerate(chips):
            copy(4 + j, (*chip, 1 - c), me).wait_recv()
        for cp in first + passed:
            cp.wait_send()
        mine.wait()

    return _pcall(body, name=name, out_shape=_sds((8 * m_per, n), blk.dtype),
                  in_specs=[pl.BlockSpec(memory_space=pltpu.VMEM)], out_specs=pl.BlockSpec(memory_space=pltpu.VMEM),
                  scratch_shapes=[pltpu.SemaphoreType.DMA((7,)), pltpu.SemaphoreType.DMA((7,)), pltpu.SemaphoreType.DMA],
                  compiler_params=_cp())(blk)


ANY = pl.BlockSpec(memory_space=pl.ANY)


def _half(which, rows):
    return pl.ds(pl.multiple_of(which * rows, 16), rows)


def _gather_weights(name, ws):
    nw = len(ws)
    ns = 7

    def body(*refs):
        w_refs, o_refs = refs[:nw], refs[nw:2 * nw]
        send_sems, recv_sems = refs[2 * nw:]
        x, y, c, chips = _place()
        me_chip = _chip_index(x, y)
        sibling = (x, y, 1 - c)

        def rcopy(t, k, src, dst, to):
            return pltpu.make_async_remote_copy(src_ref=src, dst_ref=dst, send_sem=send_sems.at[t * ns + k],
                                                recv_sem=recv_sems.at[t * ns + k], device_id=to, device_id_type=MESH)

        sends = []
        for t in range(nw):
            lh = w_refs[t].shape[0] // 2
            for k, chip in enumerate(chips):
                sends.append(rcopy(t, k, w_refs[t].at[_half(c, lh)], o_refs[t].at[me_chip, _half(c, lh)], (*chip, c)))
                sends[-1].start()
            sends.append(rcopy(t, 6, w_refs[t], o_refs[t].at[me_chip], sibling))
            sends[-1].start()
        for t in range(nw):
            lh = w_refs[t].shape[0] // 2
            for k, chip in enumerate(chips):
                part = o_refs[t].at[_chip_index(*chip), _half(c, lh)]
                rcopy(t, k, part, part, (*chip, c)).wait_recv()
                sends.append(rcopy(t, 3 + k, part, part, sibling))
                sends[-1].start()
        for t in range(nw):
            lh = w_refs[t].shape[0] // 2
            own = o_refs[t].at[me_chip]
            rcopy(t, 6, own, own, sibling).wait_recv()
            for k, chip in enumerate(chips):
                part = o_refs[t].at[_chip_index(*chip), _half(1 - c, lh)]
                rcopy(t, 3 + k, part, part, sibling).wait_recv()
        for cp in sends:
            cp.wait_send()

    return _pcall(body, name=name, out_shape=[_sds((4,) + w.shape, w.dtype) for w in ws],
                  in_specs=[ANY] * nw, out_specs=[ANY] * nw,
                  scratch_shapes=[pltpu.SemaphoreType.DMA((ns * nw,)), pltpu.SemaphoreType.DMA((ns * nw,))],
                  compiler_params=_cp())(*ws)


def _rs_pair(name, gs):
    ng = len(gs)

    def body(*refs):
        g_refs, o_refs = refs[:ng], refs[ng:2 * ng]
        send_sems, recv_sems = refs[2 * ng:]
        x, y, c, _ = _place()
        cps = []
        for t in range(ng):
            lh = g_refs[t].shape[1] // 2
            cp = pltpu.make_async_remote_copy(
                src_ref=g_refs[t].at[:, _half(1 - c, lh)], dst_ref=o_refs[t],
                send_sem=send_sems.at[t], recv_sem=recv_sems.at[t], device_id=(x, y, 1 - c), device_id_type=MESH)
            cp.start()
            cps.append(cp)
        for cp in cps:
            cp.wait()

    outs = [_sds((g.shape[0], g.shape[1] // 2) + g.shape[2:], g.dtype) for g in gs]
    return _pcall(body, name=name, out_shape=outs, in_specs=[ANY] * ng, out_specs=[ANY] * ng,
                  scratch_shapes=[pltpu.SemaphoreType.DMA((ng,)), pltpu.SemaphoreType.DMA((ng,))],
                  compiler_params=_cp())(*gs)


def _rs_chips(name, ps):
    ng = len(ps)

    def body(*refs):
        p_refs, o_refs = refs[:ng], refs[ng:2 * ng]
        send_sems, recv_sems = refs[2 * ng:]
        x, y, c, chips = _place()
        cps = []
        for t in range(ng):
            for k, chip in enumerate(chips):
                cp = pltpu.make_async_remote_copy(
                    src_ref=p_refs[t].at[_chip_index(*chip)], dst_ref=o_refs[t].at[k],
                    send_sem=send_sems.at[t * 3 + k], recv_sem=recv_sems.at[t * 3 + k],
                    device_id=(*chip, c), device_id_type=MESH)
                cp.start()
                cps.append(cp)
        for cp in cps:
            cp.wait()

    return _pcall(body, name=name, out_shape=[_sds((3,) + p.shape[1:], p.dtype) for p in ps],
                  in_specs=[ANY] * ng, out_specs=[ANY] * ng,
                  scratch_shapes=[pltpu.SemaphoreType.DMA((3 * ng,)), pltpu.SemaphoreType.DMA((3 * ng,))],
                  compiler_params=_cp())(*ps)


def _rs_share(name, ss):
    ng = len(ss)

    def body(*refs):
        o_refs = refs[ng:2 * ng]
        send_sems, recv_sems = refs[2 * ng:]
        x, y, c, _ = _place()
        cps = []
        for t in range(ng):
            lh = o_refs[t].shape[1] // 2
            mine = o_refs[t].at[:, _half(c, lh)]
            cp = pltpu.make_async_remote_copy(
                src_ref=mine, dst_ref=mine, send_sem=send_sems.at[t], recv_sem=recv_sems.at[t],
                device_id=(x, y, 1 - c), device_id_type=MESH)
            cp.start()
            cps.append(cp)
        for t in range(ng):
            lh = o_refs[t].shape[1] // 2
            cps[t].wait_send()
            theirs = o_refs[t].at[:, _half(1 - c, lh)]
            pltpu.make_async_remote_copy(
                src_ref=theirs, dst_ref=theirs, send_sem=send_sems.at[t], recv_sem=recv_sems.at[t],
                device_id=(x, y, 1 - c), device_id_type=MESH).wait_recv()

    return _pcall(body, name=name, out_shape=[_sds(s.shape, s.dtype) for s in ss],
                  in_specs=[ANY] * ng, out_specs=[ANY] * ng, input_output_aliases={t: t for t in range(ng)},
                  scratch_shapes=[pltpu.SemaphoreType.DMA((ng,)), pltpu.SemaphoreType.DMA((ng,))],
                  compiler_params=_cp())(*ss)


def _row_block(r, cn):
    if r % 8:
        return r
    best = 8
    for d in range(8, r + 1, 8):
        if r % d == 0 and d * cn * 4 <= (1 << 20):
            best = d
    return best


def _add_half(name, g, a, idx):
    j, rh, cn = a.shape
    tr = _row_block(rh, cn)
    nb = rh // tr

    def body(i_ref, g_ref, a_ref, o_ref):
        o_ref[...] = (g_ref[...] + a_ref[...]).astype(BF16)

    blk = (None, tr, cn)
    gs = pltpu.PrefetchScalarGridSpec(
        num_scalar_prefetch=1, grid=(j, nb),
        in_specs=[pl.BlockSpec(blk, lambda jj, i, i_ref: (jj, i_ref[0] * nb + i, 0)),
                  pl.BlockSpec(blk, lambda jj, i, i_ref: (jj, i, 0))],
        out_specs=pl.BlockSpec(blk, lambda jj, i, i_ref: (jj, i, 0)))
    return _pcall(body, name=name, grid_spec=gs, out_shape=_sds(a.shape, BF16),
                  compiler_params=_cp(("parallel", "parallel")))(idx, g, a)


def _sum_final(name, g, a, b, idx, buf, lyr, nlyr):
    _, r, cn = g.shape
    rh = r // 2
    tr = _row_block(rh, cn)
    nb = rh // tr

    def body(*refs):
        g_ref, a_ref, b_ref = refs[1:4]
        o_ref = refs[-1]
        own = g_ref[...] + a_ref[...]
        o_ref[...] = (own + b_ref[0].astype(F32)) + (b_ref[1].astype(F32) + b_ref[2].astype(F32))

    blk = (None, tr, cn)
    in_specs = [pl.BlockSpec(blk, lambda i, i_ref: (i_ref[1], i_ref[0] * nb + i, 0)),
                pl.BlockSpec(blk, lambda i, i_ref: (i_ref[1], i, 0)),
                pl.BlockSpec((3, tr, cn), lambda i, i_ref: (0, i, 0))]
    args = [idx, g, a, b]
    kw = {}
    if buf is not None:
        in_specs.append(ANY)
        args.append(buf)
        kw["input_output_aliases"] = {4: 0}
    gs = pltpu.PrefetchScalarGridSpec(
        num_scalar_prefetch=1, grid=(nb,), in_specs=in_specs,
        out_specs=pl.BlockSpec(blk, lambda i, i_ref: (lyr, i_ref[0] * nb + i, 0)))
    return _pcall(body, name=name, grid_spec=gs, out_shape=_sds((nlyr, r, cn)),
                  compiler_params=_cp(("parallel",)), **kw)(*args)


def _sum8(name, g):
    _, r, n = g.shape
    tr = 8

    def body(g_ref, o_ref):
        o_ref[...] = ((g_ref[0] + g_ref[1]) + (g_ref[2] + g_ref[3])) + ((g_ref[4] + g_ref[5]) + (g_ref[6] + g_ref[7]))

    return _pcall(body, name=name, grid=(r // tr,), in_specs=[pl.BlockSpec((8, tr, n), lambda i: (0, i, 0))],
                  out_specs=pl.BlockSpec((tr, n), lambda i: (i, 0)), out_shape=_sds((r, n)),
                  compiler_params=_cp(("parallel",)))(g)


def _ada_mod(name, c16, ada_w, bias):
    nl, dm, n = ada_w.shape

    def body(c_ref, w_ref, b_ref, o_ref):
        o_ref[...] = _dot(_silu(c_ref[...]), w_ref[...], NN) + b_ref[...]

    return _pcall(body, name=name, grid=(nl,),
                  in_specs=[_full(c16.shape), pl.BlockSpec((None, dm, n), lambda i: (i, 0, 0)),
                            pl.BlockSpec((None, 1, n), lambda i: (i, 0, 0))],
                  out_specs=pl.BlockSpec((None, 16, n), lambda i: (i, 0, 0)), out_shape=_sds((nl, 16, n)),
                  compiler_params=_cp(("parallel",)))(c16, ada_w, bias)


def _ada_bwd(name, c16, dmod, ada_w):
    nl, dm, n = ada_w.shape

    def body(c_ref, d_ref, w_ref, gw_ref, dc_ref):
        @pl.when(pl.program_id(0) == 0)
        def _():
            dc_ref[...] = jnp.zeros_like(dc_ref)

        dv = d_ref[...]
        gw_ref[...] = _dot(_silu(c_ref[...]), dv, TN)
        dc_ref[...] += _dot(dv, w_ref[...], NT)

    return _pcall(body, name=name, grid=(nl,),
                  in_specs=[_full(c16.shape), pl.BlockSpec((None, 16, n), lambda i: (i, 0, 0)),
                            pl.BlockSpec((None, dm, n), lambda i: (i, 0, 0))],
                  out_specs=[pl.BlockSpec((None, dm, n), lambda i: (i, 0, 0)), _full((16, dm))],
                  out_shape=[_sds((nl, dm, n)), _sds((16, dm))],
                  compiler_params=_cp(("arbitrary",)))(c16, dmod, ada_w)


def _rowsum16(name, dmod):
    nl, _, n = dmod.shape

    def body(d_ref, o_ref):
        o_ref[...] = _colsum(d_ref[...])

    return _pcall(body, name=name, grid=(nl,), in_specs=[pl.BlockSpec((None, 16, n), lambda i: (i, 0, 0))],
                  out_specs=pl.BlockSpec((None, 1, n), lambda i: (i, 0, 0)), out_shape=_sds((nl, 1, n)),
                  compiler_params=_cp(("parallel",)))(dmod)


def _cctx_grad(name, parts, c_ctx):
    def body(p_ref, c_ref, o_ref):
        tot = (p_ref[0:1, :] + p_ref[1:2, :]) + (p_ref[2:3, :] + p_ref[3:4, :])
        o_ref[...] = tot * _dsilu(c_ref[...])

    return _pcall(body, name=name, out_shape=_sds(c_ctx.shape), compiler_params=_cp())(parts, c_ctx)


def _adamw(name, w, g, m, v):
    shape = w.shape
    cn = shape[-1]
    r = math.prod(shape[:-1]) if len(shape) > 1 else 1
    tr = _row_block(r, cn)
    c1 = 1.0 - ADAM_B1 ** ADAM_STEP
    c2 = 1.0 - ADAM_B2 ** ADAM_STEP

    def body(w_ref, g_ref, m_ref, v_ref, d_ref, mo_ref, vo_ref):
        gv = g_ref[...]
        mn = ADAM_B1 * m_ref[...] + (1.0 - ADAM_B1) * gv
        vn = ADAM_B2 * v_ref[...] + (1.0 - ADAM_B2) * (gv * gv)
        d_ref[...] = -ADAM_LR * ((mn / c1) / (jnp.sqrt(vn / c2) + ADAM_EPS) + ADAM_WD * w_ref[...])
        mo_ref[...] = mn
        vo_ref[...] = vn

    blk = pl.BlockSpec((tr, cn), lambda i: (i, 0))
    o = _sds((r, cn))
    outs = _pcall(body, name=name, grid=(r // tr,), in_specs=[blk] * 4, out_specs=[blk] * 3, out_shape=[o, o, o],
                  compiler_params=_cp(("parallel",)))(*[a.reshape(r, cn) for a in (w, g, m, v)])
    return tuple(a.reshape(shape) for a in outs)


def _local_step(xs, target, modt, nw, fnw, wts, ev, od, lc):
    t, dm = xs.shape
    nct = lc // ROW_TILE
    depth = nw.shape[0]
    cs, sn = _rope_tables(t, lc)
    saved = []
    x_in, x1p, fp = xs, None, None
    for i in range(depth):
        j, even = i // 2, i % 2 == 0
        tag = f"l{i}"
        if i == 0:
            _, h = _rnm(tag + "_norm1", x_in, None, None, 0, modt[0], 0, 1, nw[0, 0], nct)
        else:
            x_in, h = _rnm(tag + "_norm1", x1p, fp, modt[i - 1], 5, modt[i], 0, 1, nw[i, 0], nct)
        s = dict(x=x_in, h=h)
        if even:
            p = _mm_cols(tag + "_in", h, wts["even_in"][j])
            q, k, v = _even_qkv(tag + "_qkv", p, cs, sn)
            of, ob, ss = _retention_fwd(tag + "_ret", q, k, v, ev["lgb"][j], lc)
            mix, yc = _even_mix(tag + "_mix", p, of, ob, ev["cw"][j], ev["lnw"][j], ev["lnb"][j], nct)
            y = _mm_full(tag + "_out", mix, wts["even_out"][j], NN)
            s.update(p=p, q=q, k=k, v=v, of=of, ob=ob, ss=ss, yc=yc)
        else:
            p = _mm_cols(tag + "_in", h, wts["odd_in"][j])
            mix, m = _odd_mix(tag + "_mix", p, od["pw"][j], od["ps"][j], od["lnw"][j], od["lnb"][j],
                              od["sgw"][j], od["sgb"][j], nct, lc)
            y = _mm_full(tag + "_out", mix, wts["odd_out"][j], NN)
            s.update(p=p, m=m)
        x1, h2 = _rnm(tag + "_norm2", x_in, y, modt[i], 2, modt[i], 3, 4, nw[i, 1], nct)
        a, gt, up = _ffn_up(tag + "_ffn_up", h2, wts["gate"][i], wts["up"][i])
        f = _mm_full(tag + "_ffn_down", a, wts["down"][i], NN)
        s.update(mix=mix, y=y, x1=x1, h2=h2, a=a, gt=gt, up=up, f=f)
        saved.append(s)
        x1p, fp = x1, f

    loss_blk, dx, df, fin_s = _fin("final", x1p, fp, modt[depth - 1], 5, fnw, target, nct)

    gb = {k: [None] * len(w) for k, w in wts.items()}
    fh = wts["down"][0].shape[0] // 2
    dmod = [[None] * 6 for _ in range(depth)]
    dnw = [[None, None] for _ in range(depth)]
    zero2 = jnp.zeros((2, dm), F32)
    dmod[depth - 1][5] = jnp.stack([zero2[0], fin_s[0]])
    small = dict(dfnw=fin_s[1], ev=[], od=[])
    for i in reversed(range(depth)):
        j, even = i // 2, i % 2 == 0
        tag = f"l{i}b"
        s = saved[i]
        dgt, dup = _ffn_down_bwd(tag + "_ffn_down", df, wts["down"][i], s["gt"], s["up"])
        gb["down"][i] = _wgrad_rows(tag + "_gdown", s["a"], fh, df)
        gb["gate"][i] = _wgrad_rows(tag + "_ggate", dgt, fh, s["h2"])
        gb["up"][i] = _wgrad_rows(tag + "_gup", dup, fh, s["h2"])
        dh2 = _ffn_in_bwd(tag + "_ffn_in", dgt, dup, wts["gate"][i], wts["up"][i])
        dx1, dy, s2 = _bnm(tag + "_norm2", s["x1"], dh2, dx, s["y"], modt[i], 3, 4, modt[i], 2, nw[i, 1], nct)
        dmod[i][3], dmod[i][4], dmod[i][2] = s2[:, 0], s2[:, 1], s2[:, 2]
        dnw[i][1] = s2[1, 3]
        wo, wi = ("even_out", "even_in") if even else ("odd_out", "odd_in")
        dmix = _mm_full(tag + "_out", dy, wts[wo][j], NT)
        gb[wo][j] = _wgrad_rows(tag + "_gout", s["mix"], wts[wo][j].shape[0] // 4, dy)
        if even:
            do, dg, dyc, lns = _even_mix_bwd1(tag + "_mix1", dmix, s["p"], s["of"], s["ob"], s["yc"],
                                              ev["lnw"][j], ev["lnb"][j])
            da, dgb, dcw = _even_conv_bwd(tag + "_conv", dyc, s["p"], ev["cw"][j], nct)
            dqf, dkf, dvf, dqb, dkb, dvb, dl = _retention_bwd(tag + "_ret", s["q"], s["k"], s["v"], do, s["ss"],
                                                              ev["lgb"][j], lc)
            dp = _even_dp(tag + "_dp", (dqf, dqb), (dkf, dkb), (dvf, dvb), dg, da, dgb, cs, sn)
            small["ev"].append(dict(lnw=lns[0], lnb=lns[1], cw=dcw, dl=dl[:, 0]))
        else:
            dm_, dpd, vec, dpw, dsgw, dsgb = _odd_mix_bwd1(tag + "_mix1", dmix, s["p"], s["m"], od["pw"][j], od["ps"][j],
                                                           od["lnw"][j], od["lnb"][j], od["sgw"][j], od["sgb"][j])
            dp = _odd_dp(tag + "_dp", dm_, dpd, nct, lc)
            small["od"].append(dict(ps=vec[0], lnw=vec[1], lnb=vec[2], pw=dpw, sgw=dsgw, sgb=dsgb[:, :, 0]))
        dh = _mm_cols_bwd(tag + "_in", dp, wts[wi][j])
        gb[wi][j] = _wgrad_cols(tag + "_gin", s["h"], dp, wts[wi][j].shape[0])
        if i > 0:
            dx, df, s1 = _bnm(tag + "_norm1", s["x"], dh, dx1, saved[i - 1]["f"], modt[i], 0, 1, modt[i - 1], 5,
                              nw[i, 0], nct)
            dmod[i - 1][5] = s1[:, 2]
        else:
            dx, _, s1 = _bnm(tag + "_norm1", s["x"], dh, dx1, None, modt[0], 0, 1, None, 0, nw[0, 0], nct)
        dmod[i][0], dmod[i][1] = s1[:, 0], s1[:, 1]
        dnw[i][0] = s1[1, 3]
    small["ev"].reverse()
    small["od"].reverse()
    dmod_t = jnp.stack([jnp.concatenate([jnp.stack(rows, axis=1), jnp.zeros((2, 2, dm), F32)], axis=1) for rows in dmod])
    small["dmod"] = dmod_t
    small["dnw"] = jnp.stack([jnp.stack(r) for r in dnw])
    return loss_blk, dx, gb, small


WEIGHTS = ["c_ctx", "ada_w", "ada_b", "norm_w", "even_w_in", "even_w_out", "ret_decay_logit", "conv_dw_w",
           "conv_ln_w", "conv_ln_b", "odd_w_in", "odd_w_out", "pool_w", "pool_scale", "sg_ln_w", "sg_ln_b",
           "sg_w", "sg_b", "ffn_w_gate", "ffn_w_up", "ffn_w_down", "final_norm_w"]
BIG = dict(even_in="even_w_in", even_out="even_w_out", odd_in="odd_w_in", odd_out="odd_w_out",
           gate="ffn_w_gate", up="ffn_w_up", down="ffn_w_down")


def _rows(a, width=1024):
    flat = a.reshape(-1)
    n = flat.shape[0]
    per = 8 * width
    tot = -(-n // per) * per
    return jnp.pad(flat, (0, tot - n)).reshape(tot // width, width)


def _unshard(parts, lead):
    nl = len(lead)
    perm = tuple(range(1, nl + 1)) + (0, nl + 1)
    return parts.transpose(perm).reshape(tuple(lead) + (4 * parts.shape[-1],))


def _my_cols(a, chip, n):
    start = (0,) * (a.ndim - 1) + (chip * n,)
    return lax.dynamic_slice(a, start, a.shape[:-1] + (n,))


def kernel(x, c, ctx, c_ctx, ada_w, ada_b, norm_w, even_w_in, even_w_out, ret_decay_logit, conv_dw_w, conv_ln_w, conv_ln_b, odd_w_in, odd_w_out, pool_w, pool_scale, sg_ln_w, sg_ln_b, sg_w, sg_b, ffn_w_gate, ffn_w_up, ffn_w_down, final_norm_w, loss_target, m_c_ctx, m_ada_w, m_ada_b, m_norm_w, m_even_w_in, m_even_w_out, m_ret_decay_logit, m_conv_dw_w, m_conv_ln_w, m_conv_ln_b, m_odd_w_in, m_odd_w_out, m_pool_w, m_pool_scale, m_sg_ln_w, m_sg_ln_b, m_sg_w, m_sg_b, m_ffn_w_gate, m_ffn_w_up, m_ffn_w_down, m_final_norm_w, v_c_ctx, v_ada_w, v_ada_b, v_norm_w, v_even_w_in, v_even_w_out, v_ret_decay_logit, v_conv_dw_w, v_conv_ln_w, v_conv_ln_b, v_odd_w_in, v_odd_w_out, v_pool_w, v_pool_scale, v_sg_ln_w, v_sg_ln_b, v_sg_w, v_sg_b, v_ffn_w_gate, v_ffn_w_up, v_ffn_w_down, v_final_norm_w):
    wv = dict(c_ctx=c_ctx, ada_w=ada_w, ada_b=ada_b, norm_w=norm_w, even_w_in=even_w_in, even_w_out=even_w_out,
              ret_decay_logit=ret_decay_logit, conv_dw_w=conv_dw_w, conv_ln_w=conv_ln_w, conv_ln_b=conv_ln_b,
              odd_w_in=odd_w_in, odd_w_out=odd_w_out, pool_w=pool_w, pool_scale=pool_scale, sg_ln_w=sg_ln_w,
              sg_ln_b=sg_ln_b, sg_w=sg_w, sg_b=sg_b, ffn_w_gate=ffn_w_gate, ffn_w_up=ffn_w_up,
              ffn_w_down=ffn_w_down, final_norm_w=final_norm_w)
    mv = dict(zip(WEIGHTS, (m_c_ctx, m_ada_w, m_ada_b, m_norm_w, m_even_w_in, m_even_w_out, m_ret_decay_logit,
                            m_conv_dw_w, m_conv_ln_w, m_conv_ln_b, m_odd_w_in, m_odd_w_out, m_pool_w, m_pool_scale,
                            m_sg_ln_w, m_sg_ln_b, m_sg_w, m_sg_b, m_ffn_w_gate, m_ffn_w_up, m_ffn_w_down,
                            m_final_norm_w)))
    vv = dict(zip(WEIGHTS, (v_c_ctx, v_ada_w, v_ada_b, v_norm_w, v_even_w_in, v_even_w_out, v_ret_decay_logit,
                            v_conv_dw_w, v_conv_ln_w, v_conv_ln_b, v_odd_w_in, v_odd_w_out, v_pool_w, v_pool_scale,
                            v_sg_ln_w, v_sg_ln_b, v_sg_w, v_sg_b, v_ffn_w_gate, v_ffn_w_up, v_ffn_w_down,
                            v_final_norm_w)))
    xi, yi, ci = lax.axis_index("x"), lax.axis_index("y"), lax.axis_index("c")
    chip = 2 * xi + yi
    dev = 4 * xi + 2 * yi + ci
    dm = x.shape[-1]
    lc = ctx.shape[1]
    depth = ada_w.shape[0]
    n_ada = ada_w.shape[-1]

    cw_pad = jnp.pad(conv_dw_w, ((0, 0), (0, 1), (0, 0)))
    vec3 = jnp.stack([pool_scale, sg_ln_w, sg_ln_b])
    pack1 = jnp.concatenate([_rows(c), _rows(norm_w), _rows(cw_pad), _rows(vec3)], axis=0)
    g1 = _all_gather8("gather_small", pack1).reshape(8, 32, dm)
    c_all = g1[:, 0]
    per_chip = g1[0::2]
    norm_full = _unshard(per_chip[:, 8:10].reshape(4, depth, 2, dm // 4), (depth, 2))
    cw_full = _unshard(per_chip[:, 16:24].reshape(4, 2, CONV_K + 1, 128), (2, CONV_K + 1))
    vec_full = _unshard(per_chip[:, 24, :768].reshape(4, 3, 2, 128), (3, 2))

    c16 = jnp.concatenate([c_all, c_ctx[None, :], jnp.zeros((7, dm), F32)], axis=0)
    mod_sh = _ada_mod("ada_mod", c16, ada_w, _my_cols(ada_b, chip, n_ada)[:, None, :])
    g2 = _all_gather8("gather_mod", mod_sh.reshape(depth * 16, n_ada)).reshape(8, depth, 16, n_ada)
    mod_full = _unshard(g2[0::2], (depth, 16))
    mod_x = lax.dynamic_index_in_dim(mod_full, dev, axis=1, keepdims=False).reshape(depth, 6, dm)
    mod_c = mod_full[:, 8].reshape(depth, 6, dm)
    modt = jnp.pad(jnp.stack([mod_c, mod_x], axis=1), ((0, 0), (0, 0), (0, 2), (0, 0)))

    names = list(BIG)
    tr_names = ("gate", "up")
    shard = {k: (jnp.swapaxes(wv[BIG[k]], 1, 2) if k in tr_names else wv[BIG[k]]).astype(BF16) for k in names}
    wts = {k: [None] * shard[k].shape[0] for k in names}
    for i in range(depth):
        mixer = ("even_in", "even_out") if i % 2 == 0 else ("odd_in", "odd_out")
        keys = [(k, i // 2) for k in mixer] + [(k, i) for k in ("gate", "up", "down")]
        got = _gather_weights(f"gather_w{i}", [shard[k][l] for k, l in keys])
        for (k, l), g in zip(keys, got):
            wts[k][l] = g if k.endswith("_in") else g.reshape(4 * g.shape[1], g.shape[2])

    ev = dict(lgb=jnp.broadcast_to(ret_decay_logit.reshape(-1, 2 * HEADS)[:, :, None], (ret_decay_logit.shape[0], 2 * HEADS, HEAD_DIM)),
              cw=cw_full, lnw=conv_ln_w[:, None, :], lnb=conv_ln_b[:, None, :])
    od = dict(pw=pool_w, ps=vec_full[0][:, None, :], lnw=vec_full[1][:, None, :], lnb=vec_full[2][:, None, :],
              sgw=sg_w, sgb=jnp.broadcast_to(sg_b[:, :, :, None], sg_b.shape + (GC,)))
    xs = jnp.concatenate([ctx[0], x[0]], axis=0)
    loss_blk, dxs, gb, small = _local_step(xs, loss_target[0], modt, norm_full[:, :, None, :], final_norm_w[None, :],
                                           wts, ev, od, lc)

    misc = jnp.stack([
        small["dfnw"], jnp.broadcast_to(loss_blk[0, 0], (dm,)),
        jnp.concatenate([e["lnw"] for e in small["ev"]]), jnp.concatenate([e["lnb"] for e in small["ev"]]),
        jnp.concatenate([o["ps"] for o in small["od"]]), jnp.concatenate([o["lnw"] for o in small["od"]]),
        jnp.concatenate([o["lnb"] for o in small["od"]]),
        jnp.pad(jnp.concatenate([e["dl"] for e in small["ev"]]), (0, dm - 4 * HEADS)),
        jnp.stack([o["sgb"] for o in small["od"]]).reshape(-1)])
    pack2 = jnp.concatenate([
        _rows(small["dmod"]), _rows(small["dnw"]), _rows(misc), _rows(jnp.stack([e["cw"] for e in small["ev"]])),
        _rows(jnp.stack([o["pw"] for o in small["od"]])), _rows(jnp.stack([o["sgw"] for o in small["od"]]))], axis=0)
    n2 = pack2.shape[0]
    g3 = _all_gather8("gather_grads", pack2)
    tot = _sum8("sum_grads", g3.reshape(8, n2, dm))
    r_mod = depth * 16
    o_nw, o_misc = r_mod, r_mod + 8
    o_cw = o_misc + 16
    o_pw = o_cw + 2 * (CONV_K + 1) // 2
    o_sgw = o_pw + 128
    dmod_sum = tot[:r_mod].reshape(depth, 2, 8, dm)
    dmod_dev = g3.reshape(8, n2, dm)[:, :r_mod].reshape(8, depth, 2, 8, dm)
    dm_x = dmod_dev[:, :, 1, :6].reshape(8, depth, 6 * dm).transpose(1, 0, 2)
    dm_c = dmod_sum[:, 0, :6].reshape(depth, 1, 6 * dm)
    dmod16 = jnp.concatenate([dm_x, dm_c, jnp.zeros((depth, 7, 6 * dm), F32)], axis=1)
    g_ada_b = _rowsum16("ada_b_grad", dmod16)[:, 0]
    g_ada_w, dc16 = _ada_bwd("ada_bwd", c16, _my_cols(dmod16, chip, n_ada), ada_w)
    g4 = _all_gather8("gather_cctx", dc16[8:16]).reshape(8, 8, dm)
    g_c_ctx = _cctx_grad("cctx_grad", g4[0::2, 0], c_ctx[None, :])[0]

    misc_t = tot[o_misc:o_misc + 16]
    half = lambda row: misc_t[row].reshape(2, dm // 2)
    grads = dict(
        c_ctx=g_c_ctx, ada_w=g_ada_w, ada_b=g_ada_b,
        norm_w=_my_cols(tot[o_nw:o_nw + 8].reshape(depth, 2, dm), chip, dm // 4),
        ret_decay_logit=misc_t[7, :4 * HEADS].reshape(ret_decay_logit.shape),
        conv_dw_w=_my_cols(tot[o_cw:o_cw + 2 * (CONV_K + 1) // 2].reshape(2, CONV_K + 1, dm // 2)[:, :CONV_K], chip, 128),
        conv_ln_w=half(2), conv_ln_b=half(3),
        pool_w=tot[o_pw:o_pw + 128].reshape(pool_w.shape),
        pool_scale=_my_cols(half(4), chip, 128), sg_ln_w=_my_cols(half(5), chip, 128), sg_ln_b=_my_cols(half(6), chip, 128),
        sg_w=tot[o_sgw:o_sgw + 128].reshape(sg_w.shape), sg_b=misc_t[8].reshape(sg_b.shape),
        final_norm_w=misc_t[0])
    loss = misc_t[1, 0]

    keys = [(k, l) for k in names for l in range(len(gb[k]))]
    glist = [gb[k][l].reshape(4, -1, gb[k][l].shape[-1]) for k, l in keys]
    idx = jnp.stack([ci, chip]).astype(jnp.int32)
    from_sib = _rs_pair("rs_pair", glist)
    pair = [_add_half(f"rs_add_{k}{l}", g, a, idx) for (k, l), g, a in zip(keys, glist, from_sib)]
    slots = _rs_chips("rs_chips", pair)
    reduced = {k: None for k in names}
    for (k, l), g, a, b in zip(keys, glist, from_sib, slots):
        reduced[k] = _sum_final(f"rs_sum_{k}{l}", g, a, b, idx, reduced[k], l, len(gb[k]))
    shards = dict(zip(names, _rs_share("rs_share", [reduced[k] for k in names])))

    deltas, new_m, new_v = {}, {}, {}
    for k in names:
        n = BIG[k]
        if k in tr_names:
            tr = lambda a: jnp.swapaxes(a, 1, 2)
            outs = _adamw("adamw_" + n, tr(wv[n]), shards[k], tr(mv[n]), tr(vv[n]))
            grads[n] = tr(shards[k])
            deltas[n], new_m[n], new_v[n] = (tr(o) for o in outs)
        else:
            grads[n] = shards[k]
    for n in WEIGHTS:
        if n not in deltas:
            deltas[n], new_m[n], new_v[n] = _adamw("adamw_" + n, wv[n], grads[n], mv[n], vv[n])
    grad_x = dxs[lc:][None]
    return (loss, grad_x, *[grads[n] for n in WEIGHTS], *[deltas[n] for n in WEIGHTS],
            *[new_m[n] for n in WEIGHTS], *[new_v[n] for n in WEIGHTS])
```

```python
import functools
import math

import jax
import jax.numpy as jnp
from jax import lax
from jax.experimental import pallas as pl
from jax.experimental.pallas import tpu as pltpu

F32 = jnp.float32
BF16 = jnp.bfloat16
MESH = pl.DeviceIdType.MESH

EPS = 1e-6
GRID_W = 64
HEADS = 4
HEAD_DIM = 128
CHUNK = 128
CONV_K = 31
ROPE_BASE = 10000.0
ROPE_PAIRS = (16, 24, 24)
POOL_WINDOWS = (2, 4, 8, 16)
ADAM_LR, ADAM_B1, ADAM_B2, ADAM_EPS, ADAM_WD, ADAM_STEP = 0.001, 0.9, 0.999, 1e-08, 0.01, 10

ROW_TILE = 256
CONV_HALO = 16
POOL_HALO = 8
VMEM_LIMIT = 56 * 1024 * 1024


def _pcall(body, **kw):
    return pl.pallas_call(body, **kw)


def _cp(sem=None, vmem=VMEM_LIMIT):
    if sem is None:
        return pltpu.CompilerParams(vmem_limit_bytes=vmem)
    return pltpu.CompilerParams(dimension_semantics=sem, vmem_limit_bytes=vmem)


def _sds(shape, dtype=F32):
    return jax.ShapeDtypeStruct(tuple(shape), dtype)


def _full(shape):
    nd = len(shape)
    return pl.BlockSpec(tuple(shape), lambda *_: (0,) * nd)


def _sigmoid(x):
    return jax.nn.sigmoid(x)


def _silu(x):
    return x * _sigmoid(x)


def _dsilu(x):
    s = _sigmoid(x)
    return s * (1.0 + x * (1.0 - s))


def _colsum(a):
    return jnp.sum(a, axis=0, keepdims=True)


def _dot(a, b, dn):
    return lax.dot_general(a.astype(BF16), b.astype(BF16), dn, preferred_element_type=F32)


NN = (((1,), (0,)), ((), ()))
NT = (((1,), (1,)), ((), ()))
TN = (((0,), (0,)), ((), ()))


def _mm_tile(t):
    best = 16
    for d in range(16, min(t, 1152) + 1, 16):
        if t % d == 0:
            best = d
    return best


def _mm(name, pairs, grid, out_shape, out_spec, dn):
    npairs = len(pairs)
    nk = grid[-1]
    kax = len(grid) - 1
    assert nk == 1 or out_shape.dtype == F32

    def body(*refs):
        ins = refs[:2 * npairs]
        o_ref = refs[2 * npairs]
        tot = None
        for p in range(npairs):
            d = _dot(ins[2 * p][...], ins[2 * p + 1][...], dn)
            tot = d if tot is None else tot + d
        if nk == 1:
            o_ref[...] = tot.astype(o_ref.dtype)
        else:
            k = pl.program_id(kax)

            @pl.when(k == 0)
            def _():
                o_ref[...] = tot

            @pl.when(k != 0)
            def _():
                o_ref[...] += tot

    args, in_specs = [], []
    for a, a_spec, b, b_spec in pairs:
        args += [a, b]
        in_specs += [a_spec, b_spec]
    sem = ("parallel",) * kax + ("arbitrary",)
    return _pcall(body, name=name, grid=grid, in_specs=in_specs, out_specs=out_spec, out_shape=out_shape,
                  compiler_params=_cp(sem))(*args)


def _mm_cols(name, a, w, out_dtype=F32):
    t, k = a.shape
    j, _, n = w.shape
    tm = _mm_tile(t)
    return _mm(name, [(a, pl.BlockSpec((tm, k), lambda i, jj, kk: (i, 0)),
                       w, pl.BlockSpec((None, k, n), lambda i, jj, kk: (jj, 0, 0)))],
               (t // tm, j, 1), _sds((t, j * n), out_dtype), pl.BlockSpec((tm, n), lambda i, jj, kk: (i, jj)), NN)


def _mm_cols_bwd(name, d, w):
    t = d.shape[0]
    j, k, n = w.shape
    tm = _mm_tile(t)
    return _mm(name, [(d, pl.BlockSpec((tm, n), lambda i, u, kk: (i, kk)),
                       w, pl.BlockSpec((None, k, n), lambda i, u, kk: (kk, 0, 0)))],
               (t // tm, 1, j), _sds((t, k)), pl.BlockSpec((tm, k), lambda i, u, kk: (i, 0)), NT)


def _mm_full(name, a, w, dn, tm=None):
    t, k = a.shape
    n = w.shape[1] if dn is NN else w.shape[0]
    tm = tm or _mm_tile(t)
    return _mm(name, [(a, pl.BlockSpec((tm, k), lambda i, u, kk: (i, 0)), w, _full(w.shape))],
               (t // tm, 1, 1), _sds((t, n)), pl.BlockSpec((tm, n), lambda i, u, kk: (i, 0)), dn)


def _wgrad_cols(name, a, b, j):
    t, k = a.shape
    n = b.shape[1] // j
    tt = _mm_tile(t)
    return _mm(name, [(a, pl.BlockSpec((tt, k), lambda jj, u, kk: (kk, 0)),
                       b, pl.BlockSpec((tt, n), lambda jj, u, kk: (kk, jj)))],
               (j, 1, t // tt), _sds((j, k, n)), pl.BlockSpec((None, k, n), lambda jj, u, kk: (jj, 0, 0)), TN)


def _wgrad_rows(name, a, blk, b):
    t, f = a.shape
    n = b.shape[1]
    tt = _mm_tile(t)
    return _mm(name, [(a, pl.BlockSpec((tt, blk), lambda jj, u, kk: (kk, jj)),
                       b, pl.BlockSpec((tt, n), lambda jj, u, kk: (kk, 0)))],
               (f // blk, 1, t // tt), _sds((f, n)), pl.BlockSpec((blk, n), lambda jj, u, kk: (jj, 0)), TN)


def _ffn_tiles(t, f):
    return _mm_tile(t) // 2, f // 2


def _ffn_up(name, h, wgt, wut):
    t, k = h.shape
    f = wgt.shape[0]
    tm, tn = _ffn_tiles(t, f)

    def body(h_ref, wg_ref, wu_ref, a_ref, gt_ref, up_ref):
        hv = h_ref[...]
        gt = _dot(hv, wg_ref[...], NT)
        up = _dot(hv, wu_ref[...], NT)
        a_ref[...] = (_silu(gt) * up).astype(BF16)
        gt_ref[...] = gt.astype(BF16)
        up_ref[...] = up.astype(BF16)

    wspec = pl.BlockSpec((tn, k), lambda i, jj: (jj, 0))
    ospec = pl.BlockSpec((tm, tn), lambda i, jj: (i, jj))
    o = _sds((t, f), BF16)
    return _pcall(body, name=name, grid=(t // tm, f // tn),
                  in_specs=[pl.BlockSpec((tm, k), lambda i, jj: (i, 0)), wspec, wspec],
                  out_specs=[ospec, ospec, ospec], out_shape=[o, o, o],
                  compiler_params=_cp(("parallel", "parallel")))(h, wgt, wut)


def _ffn_down_bwd(name, df, wd, gt, up, deps=()):
    t, dm = df.shape
    f = wd.shape[0]
    tm, tn = _ffn_tiles(t, f)
    nd = len(deps)

    def body(*refs):
        df_ref, wd_ref, gt_ref, up_ref = refs[:4]
        dgt_ref, dup_ref = refs[4 + nd:]
        da = _dot(df_ref[...], wd_ref[...], NT)
        g = gt_ref[...].astype(F32)
        u = up_ref[...].astype(F32)
        dgt_ref[...] = (da * u * _dsilu(g)).astype(BF16)
        dup_ref[...] = (da * _silu(g)).astype(BF16)

    aspec = pl.BlockSpec((tm, tn), lambda i, jj: (i, jj))
    o = _sds((t, f), BF16)
    return _pcall(body, name=name, grid=(t // tm, f // tn),
                  in_specs=[pl.BlockSpec((tm, dm), lambda i, jj: (i, 0)),
                            pl.BlockSpec((tn, dm), lambda i, jj: (jj, 0)), aspec, aspec]
                  + [pl.BlockSpec(d.shape, lambda i, jj: (0, 0)) for d in deps],
                  out_specs=[aspec, aspec], out_shape=[o, o],
                  compiler_params=_cp(("parallel", "parallel")))(df, wd, gt, up, *deps)


def _ffn_in_bwd(name, dgt, dup, wgt, wut):
    t, f = dgt.shape
    k = wgt.shape[1]
    tm, _ = _ffn_tiles(t, f)
    aspec = pl.BlockSpec((tm, f), lambda i, u, kk: (i, 0))
    wspec = pl.BlockSpec((f, k), lambda i, u, kk: (0, 0))
    return _mm(name, [(dgt, aspec, wgt, wspec), (dup, aspec, wut, wspec)], (t // tm, 1, 1), _sds((t, k)),
               pl.BlockSpec((tm, k), lambda i, u, kk: (i, 0)), NN)


def _modrow(ref, row, is_ctx):
    return jnp.where(is_ctx, ref[0, row:row + 1, :], ref[1, row:row + 1, :])


def _rnm(name, x, delta, mod_g, g_row, mod_n, sh_row, sc_row, nw, nct, deps=()):
    t, dm = x.shape
    tm = ROW_TILE
    has = delta is not None
    nd = len(deps)

    def body(*refs):
        refs = refs[:len(refs) - nd - (2 if has else 1)] + refs[len(refs) - (2 if has else 1):]
        if has:
            x_ref, d_ref, mg_ref, m_ref, nw_ref, xo_ref, h_ref = refs
        else:
            x_ref, m_ref, nw_ref, h_ref = refs
        is_ctx = pl.program_id(0) < nct
        xv = x_ref[...]
        if has:
            xv = xv + _modrow(mg_ref, g_row, is_ctx) * d_ref[...]
            xo_ref[...] = xv
        r = lax.rsqrt(jnp.mean(xv * xv, axis=-1, keepdims=True) + EPS)
        hv = (xv * r * nw_ref[...]) * (1.0 + _modrow(m_ref, sc_row, is_ctx)) + _modrow(m_ref, sh_row, is_ctx)
        h_ref[...] = hv.astype(BF16)

    row = pl.BlockSpec((tm, dm), lambda i: (i, 0))
    ins = [x] + ([delta, mod_g] if has else []) + [mod_n, nw] + list(deps)
    in_specs = ([row] + ([row, _full(mod_g.shape)] if has else []) + [_full(mod_n.shape), _full(nw.shape)]
                + [_full(d.shape) for d in deps])
    outs = ([_sds((t, dm))] if has else []) + [_sds((t, dm), BF16)]
    out_specs = ([row] if has else []) + [row]
    res = _pcall(body, name=name, grid=(t // tm,), in_specs=in_specs, out_specs=out_specs, out_shape=outs,
                 compiler_params=_cp(("parallel",)))(*ins)
    return res if has else (None, res[0])


def _bnm(name, xn, dh, dup, yprev, mod_n, sh_row, sc_row, mod_g, g_row, nw, nct, deps=()):
    t, dm = xn.shape
    tm = ROW_TILE
    has = yprev is not None
    nd = len(deps)

    def body(*refs):
        nout = 3 if has else 2
        refs = refs[:len(refs) - nd - nout] + refs[len(refs) - nout:]
        if has:
            x_ref, dh_ref, du_ref, y_ref, mn_ref, mg_ref, nw_ref, dx_ref, dd_ref, s_ref = refs
        else:
            x_ref, dh_ref, du_ref, mn_ref, nw_ref, dx_ref, s_ref = refs
        i = pl.program_id(0)
        is_ctx = i < nct

        @pl.when(i == 0)
        def _():
            s_ref[...] = jnp.zeros_like(s_ref)

        xv = x_ref[...]
        r = lax.rsqrt(jnp.mean(xv * xv, axis=-1, keepdims=True) + EPS)
        xh = xv * r
        w = nw_ref[...]
        sc1 = 1.0 + _modrow(mn_ref, sc_row, is_ctx)
        dhv = dh_ref[...]
        dxh = dhv * sc1 * w
        dx = r * (dxh - xh * jnp.mean(dxh * xh, axis=-1, keepdims=True)) + du_ref[...]
        dx_ref[...] = dx
        parts = [_colsum(dhv), _colsum(dhv * (xh * w))]
        if has:
            dd_ref[...] = (_modrow(mg_ref, g_row, is_ctx) * dx).astype(BF16)
            parts.append(_colsum(dx * y_ref[...]))
        else:
            parts.append(jnp.zeros((1, dm), F32))
        upd = jnp.concatenate(parts + [jnp.zeros((5, dm), F32)], axis=0)
        dnw = jnp.concatenate([jnp.zeros((3, dm), F32), _colsum(dhv * sc1 * xh), jnp.zeros((4, dm), F32)], axis=0)

        @pl.when(is_ctx)
        def _():
            s_ref[0] += upd
            s_ref[1] += dnw

        @pl.when(jnp.logical_not(is_ctx))
        def _():
            s_ref[1] += upd + dnw

    row = pl.BlockSpec((tm, dm), lambda i: (i, 0))
    ins = [xn, dh, dup] + ([yprev] if has else []) + [mod_n] + ([mod_g] if has else []) + [nw] + list(deps)
    in_specs = ([row, row, row] + ([row] if has else []) + [_full(mod_n.shape)]
                + ([_full(mod_g.shape)] if has else []) + [_full(nw.shape)] + [_full(d.shape) for d in deps])
    outs = [_sds((t, dm))] + ([_sds((t, dm), BF16)] if has else []) + [_sds((2, 8, dm))]
    out_specs = [row] + ([row] if has else []) + [_full((2, 8, dm))]
    res = _pcall(body, name=name, grid=(t // tm,), in_specs=in_specs, out_specs=out_specs, out_shape=outs,
                 compiler_params=_cp(("arbitrary",)))(*ins)
    return res if has else (res[0], None, res[1])


def _fin(name, x1, f, mod, g_row, fw, target, nct):
    t, dm = x1.shape
    tm = ROW_TILE

    def body(x_ref, f_ref, m_ref, fw_ref, t_ref, loss_ref, dx_ref, dd_ref, s_ref):
        i = pl.program_id(0)

        @pl.when(i == 0)
        def _():
            s_ref[...] = jnp.zeros_like(s_ref)
            loss_ref[...] = jnp.zeros_like(loss_ref)

        @pl.when(i < nct)
        def _():
            dx_ref[...] = jnp.zeros_like(dx_ref)
            dd_ref[...] = jnp.zeros_like(dd_ref)

        @pl.when(i >= nct)
        def _():
            g = m_ref[1, g_row:g_row + 1, :]
            fv = f_ref[...]
            xv = x_ref[...] + g * fv
            r = lax.rsqrt(jnp.mean(xv * xv, axis=-1, keepdims=True) + EPS)
            xh = xv * r
            w = fw_ref[...]
            err = xh * w - t_ref[...]
            loss_ref[...] += 0.5 * jnp.sum(err * err) / dm
            dout = err * (1.0 / dm)
            dxh = dout * w
            dx = r * (dxh - xh * jnp.mean(dxh * xh, axis=-1, keepdims=True))
            dx_ref[...] = dx
            dd_ref[...] = (g * dx).astype(BF16)
            s_ref[...] += jnp.concatenate([_colsum(dx * fv), _colsum(dout * xh), jnp.zeros((6, dm), F32)], axis=0)

    row = pl.BlockSpec((tm, dm), lambda i: (i, 0))
    trow = pl.BlockSpec((tm, dm), lambda i: (jnp.maximum(i - nct, 0), 0))
    return _pcall(body, name=name, grid=(t // tm,),
                  in_specs=[row, row, _full(mod.shape), _full(fw.shape), trow],
                  out_specs=[_full((8, 128)), row, row, _full((8, dm))],
                  out_shape=[_sds((8, 128)), _sds((t, dm)), _sds((t, dm), BF16), _sds((8, dm))],
                  compiler_params=_cp(("arbitrary",)))(x1, f, mod, fw, target)


def _rope_tables(t, lc):
    l = t - lc
    rows = l // GRID_W
    grid_r = jnp.broadcast_to(jnp.arange(rows, dtype=F32)[:, None], (rows, GRID_W)).reshape(-1)
    grid_c = jnp.broadcast_to(jnp.arange(GRID_W, dtype=F32)[None, :], (rows, GRID_W)).reshape(-1)

    def angles(p_seq, p_row, p_col):
        parts = []
        for p, n in zip((p_seq, p_row, p_col), ROPE_PAIRS):
            freq = ROPE_BASE ** (-jnp.arange(n, dtype=F32) / n)
            parts.append(p[:, None] * freq[None, :])
        return jnp.concatenate(parts, axis=-1)

    zc = jnp.zeros((lc,), F32)
    ang = jnp.concatenate([angles(jnp.arange(lc, dtype=F32), zc, zc),
                           angles(jnp.full((l,), lc, F32), grid_r, grid_c)], axis=0)
    cos, sin = jnp.cos(ang), jnp.sin(ang)
    return jnp.concatenate([cos, cos], axis=-1), jnp.concatenate([-sin, sin], axis=-1)


def _rope(u, cs, sn):
    return u * cs + pltpu.roll(u, HEAD_DIM // 2, 1) * sn


def _rope_t(d, cs, sn):
    return d * cs + pltpu.roll(d * sn, HEAD_DIM // 2, 1)


def _even_qkv(name, p, cs, sn):
    t = p.shape[0]
    tm = ROW_TILE
    w = HEADS * HEAD_DIM
    scale = HEAD_DIM ** -0.5

    def body(q_ref, k_ref, v_ref, cs_ref, sn_ref, qo_ref, ko_ref, vo_ref):
        c, s = cs_ref[...], sn_ref[...]
        for h in range(HEADS):
            sl = slice(h * HEAD_DIM, (h + 1) * HEAD_DIM)
            qo_ref[:, sl] = (_rope(q_ref[:, sl], c, s) * scale).astype(BF16)
            ko_ref[:, sl] = _rope(k_ref[:, sl], c, s).astype(BF16)
        vo_ref[...] = v_ref[...].astype(BF16)

    col = lambda j: pl.BlockSpec((tm, w), lambda i: (i, j))
    tab = pl.BlockSpec((tm, HEAD_DIM), lambda i: (i, 0))
    o = _sds((t, w), BF16)
    return _pcall(body, name=name, grid=(t // tm,), in_specs=[col(0), col(1), col(2), tab, tab],
                  out_specs=[col(0)] * 3, out_shape=[o, o, o], compiler_params=_cp(("parallel",)))(p, p, p, cs, sn)


def _log_sigmoid_row(x):
    e = jnp.exp(-jnp.abs(x))
    l1p = jnp.where(e < 0.01, e * (1.0 - e * (0.5 - e * (1.0 / 3.0))), jnp.log(1.0 + e))
    return jnp.minimum(x, 0.0) - l1p


def _ret_tables(lgb_ref, dm_ref, xi_ref, zt_ref):
    ri = lax.broadcasted_iota(jnp.int32, (CHUNK, CHUNK), 0).astype(F32)
    ci = lax.broadcasted_iota(jnp.int32, (CHUNK, CHUNK), 1).astype(F32)
    for d in range(2):
        for h in range(HEADS):
            idx = d * HEADS + h
            lg = _log_sigmoid_row(lgb_ref[idx:idx + 1, :])
            if d == 0:
                e, mask = ri - ci, ri >= ci
                xe, ze = ri + 1.0, (CHUNK - 1.0) - ri
            else:
                e, mask = ci - ri - 1.0, ci > ri
                xe, ze = (CHUNK - 1.0) - ri, ri
            dm_ref[idx] = jnp.where(mask, jnp.exp(lg * jnp.where(mask, e, 0.0)), 0.0)
            xi_ref[idx] = jnp.exp(lg * xe)
            zt_ref[idx] = jnp.exp(lg * ze)


def _ret_exponents(d):
    ri = lax.broadcasted_iota(jnp.int32, (CHUNK, CHUNK), 0).astype(F32)
    ci = lax.broadcasted_iota(jnp.int32, (CHUNK, CHUNK), 1).astype(F32)
    if d == 0:
        return ri - ci, ri + 1.0, (CHUNK - 1.0) - ri
    return ci - ri - 1.0, (CHUNK - 1.0) - ri, ri


def _bwd_chunk(n, ncc, nc):
    return jnp.where(n < ncc, ncc - 1 - n, nc - 1 - (n - ncc))


def _retention_fwd(name, q, k, v, lgb, lc):
    t, w = q.shape
    nc, ncc = t // CHUNK, lc // CHUNK
    nh = 2 * HEADS

    def body(qf_ref, kf_ref, vf_ref, qb_ref, kb_ref, vb_ref, lgb_ref, of_ref, ob_ref, ss_ref,
             s_ref, dm_ref, xi_ref, zt_ref):
        n = pl.program_id(0)

        @pl.when(n == 0)
        def _():
            s_ref[...] = jnp.zeros_like(s_ref)
            _ret_tables(lgb_ref, dm_ref, xi_ref, zt_ref)

        for d in range(2):
            q_ref, k_ref, v_ref, o_ref = (qf_ref, kf_ref, vf_ref, of_ref) if d == 0 else (qb_ref, kb_ref, vb_ref, ob_ref)
            for h in range(HEADS):
                idx = d * HEADS + h
                sl = slice(h * HEAD_DIM, (h + 1) * HEAD_DIM)
                qv, kv, vv = q_ref[:, sl], k_ref[:, sl], v_ref[:, sl]
                s = s_ref[idx]
                ss_ref[idx] = s
                a = _dot(qv, kv, NT) * dm_ref[idx]
                o = _dot(a, vv, NN) + _dot(qv.astype(F32) * xi_ref[idx], s, NN)
                o_ref[:, sl] = o
                gc = jnp.exp(_log_sigmoid_row(lgb_ref[idx:idx + 1, :]) * float(CHUNK))
                s_ref[idx] = gc * s + _dot(kv.astype(F32) * zt_ref[idx], vv, TN)

    fspec = pl.BlockSpec((CHUNK, w), lambda n: (n, 0))
    bspec = pl.BlockSpec((CHUNK, w), lambda n: (_bwd_chunk(n, ncc, nc), 0))
    tab = pltpu.VMEM((nh, CHUNK, CHUNK), F32)
    return _pcall(body, name=name, grid=(nc,),
                  in_specs=[fspec] * 3 + [bspec] * 3 + [_full((nh, HEAD_DIM))],
                  out_specs=[fspec, bspec, pl.BlockSpec((None, nh, CHUNK, CHUNK), lambda n: (n, 0, 0, 0))],
                  out_shape=[_sds((t, w)), _sds((t, w)), _sds((nc, nh, CHUNK, CHUNK))],
                  scratch_shapes=[tab, tab, tab, tab],
                  compiler_params=_cp(("arbitrary",)))(q, k, v, q, k, v, lgb)


def _retention_bwd(name, q, k, v, do, ss, lgb, lc):
    t, w = q.shape
    nc, ncc = t // CHUNK, lc // CHUNK
    nh = 2 * HEADS

    def body(qf_ref, kf_ref, vf_ref, gf_ref, qb_ref, kb_ref, vb_ref, gb_ref, ss_ref, lgb_ref,
             dqf_ref, dkf_ref, dvf_ref, dqb_ref, dkb_ref, dvb_ref, dl_ref,
             ds_ref, dm_ref, xi_ref, zt_ref, acc_ref):
        n = pl.program_id(0)

        @pl.when(n == 0)
        def _():
            ds_ref[...] = jnp.zeros_like(ds_ref)
            acc_ref[...] = jnp.zeros_like(acc_ref)
            _ret_tables(lgb_ref, dm_ref, xi_ref, zt_ref)

        for d in range(2):
            if d == 0:
                q_ref, k_ref, v_ref, g_ref, dq_ref, dk_ref, dv_ref = qf_ref, kf_ref, vf_ref, gf_ref, dqf_ref, dkf_ref, dvf_ref
            else:
                q_ref, k_ref, v_ref, g_ref, dq_ref, dk_ref, dv_ref = qb_ref, kb_ref, vb_ref, gb_ref, dqb_ref, dkb_ref, dvb_ref
            ee, xe, ze = _ret_exponents(d)
            for h in range(HEADS):
                idx = d * HEADS + h
                sl = slice(h * HEAD_DIM, (h + 1) * HEAD_DIM)
                qv, kv, vv, gv = q_ref[:, sl], k_ref[:, sl], v_ref[:, sl], g_ref[:, sl]
                s = ss_ref[idx]
                dsp = ds_ref[idx]
                dmat, xi, zt = dm_ref[idx], xi_ref[idx], zt_ref[idx]
                qf32, kf32 = qv.astype(F32), kv.astype(F32)
                a = _dot(qv, kv, NT) * dmat
                dar = _dot(gv, vv, NT)
                da = dar * dmat
                t1 = _dot(gv, s, NT)
                t2 = _dot(vv, dsp, NT)
                dq_ref[:, sl] = _dot(da, kv, NN) + xi * t1
                dk_ref[:, sl] = _dot(da, qv, TN) + zt * t2
                dv_ref[:, sl] = _dot(a, gv, TN) + _dot(kf32 * zt, dsp, NN)
                gc = jnp.exp(_log_sigmoid_row(lgb_ref[idx:idx + 1, :]) * float(CHUNK))
                ds_ref[idx] = gc * dsp + _dot(qf32 * xi, gv, TN)
                acc_ref[idx] += (ee * a * dar + xe * xi * qf32 * t1 + ze * zt * kf32 * t2
                                 + (float(CHUNK) * gc) * dsp * s)

        @pl.when(n == nc - 1)
        def _():
            for idx in range(nh):
                tot = jnp.sum(acc_ref[idx])
                dl_ref[idx:idx + 1, :] = tot * _sigmoid(-lgb_ref[idx:idx + 1, :])

    fmap = lambda n: (nc - 1 - n, 0)
    bmap = lambda n: (_bwd_chunk(nc - 1 - n, ncc, nc), 0)
    fspec = pl.BlockSpec((CHUNK, w), fmap)
    bspec = pl.BlockSpec((CHUNK, w), bmap)
    tab = pltpu.VMEM((nh, CHUNK, CHUNK), F32)
    o = _sds((t, w))
    return _pcall(body, name=name, grid=(nc,),
                  in_specs=[fspec] * 4 + [bspec] * 4
                  + [pl.BlockSpec((None, nh, CHUNK, CHUNK), lambda n: (nc - 1 - n, 0, 0, 0)), _full((nh, HEAD_DIM))],
                  out_specs=[fspec] * 3 + [bspec] * 3 + [_full((nh, HEAD_DIM))],
                  out_shape=[o] * 6 + [_sds((nh, HEAD_DIM))],
                  scratch_shapes=[tab, tab, tab, tab, tab],
                  compiler_params=_cp(("arbitrary",)))(q, k, v, do, q, k, v, do, ss, lgb)


def _halo_specs(tm, halo, t, width, col):
    hb = tm // halo
    last = t // halo - 1
    prev = pl.BlockSpec((halo, width), lambda i: (jnp.maximum(i * hb - 1, 0), col))
    nxt = pl.BlockSpec((halo, width), lambda i: (jnp.minimum((i + 1) * hb, last), col))
    return prev, nxt


def _halo_valid(i, nct, nt):
    vp = jnp.logical_and(i != 0, i != nct)
    vn = jnp.logical_and(i != nct - 1, i != nt - 1)
    return vp, vn


def _fill_window(win_ref, prev, cur, nxt, vp, vn, halo, tm):
    win_ref[0:halo, :] = jnp.where(vp, prev, 0.0)
    win_ref[halo:halo + tm, :] = cur
    win_ref[halo + tm:halo + tm + halo, :] = jnp.where(vn, nxt, 0.0)


CONV_SUB = 64


def _conv_taps(win_ref, w_ref, tm, flip):
    outs = []
    for r0 in range(0, tm, CONV_SUB):
        acc = None
        for kk in range(CONV_K):
            wk = (CONV_K - 1 - kk) if flip else kk
            term = w_ref[wk:wk + 1, :] * win_ref[r0 + kk + 1:r0 + kk + 1 + CONV_SUB, :]
            acc = term if acc is None else acc + term
        outs.append(acc)
    return jnp.concatenate(outs, axis=0)


def _head_norm(y):
    r = lax.rsqrt(jnp.mean(y * y, axis=-1, keepdims=True) + EPS)
    return y * r, r


def _ln_stats(y):
    mu = jnp.mean(y, axis=-1, keepdims=True)
    yc = y - mu
    rs = lax.rsqrt(jnp.mean(yc * yc, axis=-1, keepdims=True) + EPS)
    return yc * rs, rs


def _ln_bwd(dyh, yh, rs):
    return rs * (dyh - jnp.mean(dyh, axis=-1, keepdims=True) - yh * jnp.mean(dyh * yh, axis=-1, keepdims=True))


def _even_mix(name, p, of, ob, cw, lnw, lnb, nct):
    t = p.shape[0]
    tm, halo = ROW_TILE, CONV_HALO
    nt = t // tm
    w = HEADS * HEAD_DIM

    def body(g_ref, a_ref, gb_ref, ap_ref, gbp_ref, an_ref, gbn_ref, of_ref, ob_ref, cw_ref, lw_ref, lb_ref,
             mix_ref, yc_ref, win_ref):
        i = pl.program_id(0)
        vp, vn = _halo_valid(i, nct, nt)
        glu = lambda a, b: a * _sigmoid(b)
        _fill_window(win_ref, glu(ap_ref[...], gbp_ref[...]), glu(a_ref[...], gb_ref[...]),
                     glu(an_ref[...], gbn_ref[...]), vp, vn, halo, tm)
        yc = _conv_taps(win_ref, cw_ref, tm, False)
        yc_ref[...] = yc
        yh, _ = _ln_stats(yc)
        mix_ref[:, w:2 * w] = _silu(yh * lw_ref[...] + lb_ref[...]).astype(BF16)
        for h in range(HEADS):
            sl = slice(h * HEAD_DIM, (h + 1) * HEAD_DIM)
            yn, _ = _head_norm(of_ref[:, sl] + ob_ref[:, sl])
            mix_ref[:, sl] = (_silu(g_ref[:, sl]) * yn).astype(BF16)

    col = lambda j: pl.BlockSpec((tm, w), lambda i: (i, j))
    ap, an = _halo_specs(tm, halo, t, w, 4)
    gp, gn = _halo_specs(tm, halo, t, w, 5)
    row = pl.BlockSpec((tm, w), lambda i: (i, 0))
    return _pcall(body, name=name, grid=(nt,),
                  in_specs=[col(3), col(4), col(5), ap, gp, an, gn, row, row,
                            _full(cw.shape), _full(lnw.shape), _full(lnb.shape)],
                  out_specs=[pl.BlockSpec((tm, 2 * w), lambda i: (i, 0)), row],
                  out_shape=[_sds((t, 2 * w), BF16), _sds((t, w))],
                  scratch_shapes=[pltpu.VMEM((tm + 2 * halo, w), F32)],
                  compiler_params=_cp(("parallel",)))(p, p, p, p, p, p, p, of, ob, cw, lnw, lnb)


def _even_mix_bwd1(name, dmix, p, of, ob, yc, lnw, lnb):
    t = p.shape[0]
    tm = ROW_TILE
    w = HEADS * HEAD_DIM

    def body(dr_ref, dc_ref, g_ref, of_ref, ob_ref, yc_ref, lw_ref, lb_ref, do_ref, dg_ref, dyc_ref, s_ref):
        @pl.when(pl.program_id(0) == 0)
        def _():
            s_ref[...] = jnp.zeros_like(s_ref)

        for h in range(HEADS):
            sl = slice(h * HEAD_DIM, (h + 1) * HEAD_DIM)
            yn, r = _head_norm(of_ref[:, sl] + ob_ref[:, sl])
            gv = g_ref[:, sl]
            dr = dr_ref[:, sl]
            dg_ref[:, sl] = (dr * yn * _dsilu(gv)).astype(BF16)
            dyn = dr * _silu(gv)
            do_ref[:, sl] = (r * (dyn - yn * jnp.mean(dyn * yn, axis=-1, keepdims=True))).astype(BF16)
        yh, rs = _ln_stats(yc_ref[...])
        lw = lw_ref[...]
        dlo = dc_ref[...] * _dsilu(yh * lw + lb_ref[...])
        dyc_ref[...] = _ln_bwd(dlo * lw, yh, rs)
        s_ref[...] += jnp.concatenate([_colsum(dlo * yh), _colsum(dlo), jnp.zeros((6, w), F32)], axis=0)

    col = lambda j: pl.BlockSpec((tm, w), lambda i: (i, j))
    row = pl.BlockSpec((tm, w), lambda i: (i, 0))
    return _pcall(body, name=name, grid=(t // tm,),
                  in_specs=[col(0), col(1), col(3), row, row, row, _full(lnw.shape), _full(lnb.shape)],
                  out_specs=[row, row, row, _full((8, w))],
                  out_shape=[_sds((t, w), BF16), _sds((t, w), BF16), _sds((t, w)), _sds((8, w))],
                  compiler_params=_cp(("arbitrary",)))(dmix, dmix, p, of, ob, yc, lnw, lnb)


def _even_conv_bwd(name, dyc, p, cw, nct):
    t = p.shape[0]
    tm, halo = ROW_TILE, CONV_HALO
    nt = t // tm
    w = HEADS * HEAD_DIM

    def body(d_ref, dp_ref, dn_ref, a_ref, gb_ref, ap_ref, gbp_ref, an_ref, gbn_ref, cw_ref,
             da_ref, dgb_ref, dw_ref, dwin_ref, uwin_ref):
        i = pl.program_id(0)

        @pl.when(i == 0)
        def _():
            dw_ref[...] = jnp.zeros_like(dw_ref)

        vp, vn = _halo_valid(i, nct, nt)
        glu = lambda a, b: a * _sigmoid(b)
        dcur = d_ref[...]
        _fill_window(dwin_ref, dp_ref[...], dcur, dn_ref[...], vp, vn, halo, tm)
        _fill_window(uwin_ref, glu(ap_ref[...], gbp_ref[...]), glu(a_ref[...], gb_ref[...]),
                     glu(an_ref[...], gbn_ref[...]), vp, vn, halo, tm)
        du = _conv_taps(dwin_ref, cw_ref, tm, True)
        av = a_ref[...]
        sg = _sigmoid(gb_ref[...])
        da_ref[...] = (du * sg).astype(BF16)
        dgb_ref[...] = (du * av * sg * (1.0 - sg)).astype(BF16)
        rows = [_colsum(dcur * uwin_ref[kk + 1:kk + 1 + tm, :]) for kk in range(CONV_K)]
        dw_ref[...] += jnp.concatenate(rows + [jnp.zeros((1, w), F32)], axis=0)

    col = lambda j: pl.BlockSpec((tm, w), lambda i: (i, j))
    row = pl.BlockSpec((tm, w), lambda i: (i, 0))
    dp, dn = _halo_specs(tm, halo, t, w, 0)
    ap, an = _halo_specs(tm, halo, t, w, 4)
    gp, gn = _halo_specs(tm, halo, t, w, 5)
    win = pltpu.VMEM((tm + 2 * halo, w), F32)
    return _pcall(body, name=name, grid=(nt,),
                  in_specs=[row, dp, dn, col(4), col(5), ap, gp, an, gn, _full(cw.shape)],
                  out_specs=[row, row, _full((CONV_K + 1, w))],
                  out_shape=[_sds((t, w), BF16), _sds((t, w), BF16), _sds((CONV_K + 1, w))],
                  scratch_shapes=[win, win],
                  compiler_params=_cp(("arbitrary",)))(dyc, dyc, dyc, p, p, p, p, p, p, cw)


def _even_dp(name, dqs, dks, dvs, dg, da, dgb, cs, sn):
    t, w = dg.shape
    tm = ROW_TILE
    scale = HEAD_DIM ** -0.5

    def body(dqf_ref, dqb_ref, dkf_ref, dkb_ref, dvf_ref, dvb_ref, dg_ref, da_ref, dgb_ref, cs_ref, sn_ref, dp_ref):
        c, s = cs_ref[...], sn_ref[...]
        for h in range(HEADS):
            sl = slice(h * HEAD_DIM, (h + 1) * HEAD_DIM)
            dp_ref[:, sl] = (_rope_t(dqf_ref[:, sl] + dqb_ref[:, sl], c, s) * scale).astype(BF16)
            dp_ref[:, w + h * HEAD_DIM:w + (h + 1) * HEAD_DIM] = _rope_t(dkf_ref[:, sl] + dkb_ref[:, sl], c, s).astype(BF16)
        dp_ref[:, 2 * w:3 * w] = (dvf_ref[...] + dvb_ref[...]).astype(BF16)
        dp_ref[:, 3 * w:4 * w] = dg_ref[...]
        dp_ref[:, 4 * w:5 * w] = da_ref[...]
        dp_ref[:, 5 * w:6 * w] = dgb_ref[...]

    row = pl.BlockSpec((tm, w), lambda i: (i, 0))
    tab = pl.BlockSpec((tm, HEAD_DIM), lambda i: (i, 0))
    return _pcall(body, name=name, grid=(t // tm,), in_specs=[row] * 9 + [tab, tab],
                  out_specs=pl.BlockSpec((tm, 6 * w), lambda i: (i, 0)), out_shape=_sds((t, 6 * w), BF16),
                  compiler_params=_cp(("parallel",)))(dqs[0], dqs[1], dks[0], dks[1], dvs[0], dvs[1], dg, da, dgb, cs, sn)


GROUPS = 4
GC = 128
INV_SQRT2 = 0.7071067811865476
INV_SQRT_2PI = 0.3989422804014327


def _gelu(x):
    return 0.5 * x * (1.0 + lax.erf(x * INV_SQRT2))


def _dgelu(x):
    return 0.5 * (1.0 + lax.erf(x * INV_SQRT2)) + x * jnp.exp(-0.5 * x * x) * INV_SQRT_2PI


def _pool_count(i, nct, lc, t, tm, rows, row0, left, right):
    is_ctx = i < nct
    seg_start = jnp.where(is_ctx, 0, lc)
    seg_len = jnp.where(is_ctx, lc, t - lc)
    pos = i * tm + row0 - seg_start + lax.broadcasted_iota(jnp.int32, (rows, GC), 0)
    cnt = jnp.minimum(pos + right, seg_len - 1) - jnp.maximum(pos - left, 0) + 1
    return jnp.maximum(cnt, 1).astype(F32)


def _spatial_gate(vln, sgw_ref, sgb_ref, tm):
    cols = []
    for g in range(GROUPS):
        sl = slice(g * GC, (g + 1) * GC)
        parts = [_dot(sgw_ref[g], vln[r0:r0 + CHUNK, sl], NN) + sgb_ref[g] for r0 in range(0, tm, CHUNK)]
        cols.append(jnp.concatenate(parts, axis=0))
    return jnp.concatenate(cols, axis=1)


def _odd_mix(name, p, pw, pscale, lnw, lnb, sgw, sgb, nct, lc):
    t = p.shape[0]
    tm, halo = ROW_TILE, POOL_HALO
    nt = t // tm
    w = GROUPS * GC

    def body(pc_ref, pp_ref, pn_ref, pu_ref, pv_ref, pw_ref, ps_ref, lw_ref, lb_ref, sgw_ref, sgb_ref,
             mix_ref, m_ref, win_ref):
        i = pl.program_id(0)
        vp, vn = _halo_valid(i, nct, nt)
        pc = pc_ref[...]
        _fill_window(win_ref, pp_ref[...], pc, pn_ref[...], vp, vn, halo, tm)
        for g, wd in enumerate(POOL_WINDOWS):
            sl = slice(g * GC, (g + 1) * GC)
            left = wd // 2
            right = wd - 1 - left
            s = None
            for o in range(-left, right + 1):
                term = win_ref[halo + o:halo + o + tm, sl]
                s = term if s is None else s + term
            mg = s / _pool_count(i, nct, lc, t, tm, tm, 0, left, right) - pc[:, sl]
            m_ref[:, sl] = mg
            mix_ref[:, sl] = (_dot(mg, pw_ref[g], NN) * ps_ref[:, sl]).astype(BF16)
        u = _gelu(pu_ref[...])
        vh, _ = _ln_stats(_gelu(pv_ref[...]))
        s = _spatial_gate(vh * lw_ref[...] + lb_ref[...], sgw_ref, sgb_ref, tm)
        mix_ref[:, w:2 * w] = (u * s).astype(BF16)

    col = lambda j: pl.BlockSpec((tm, w), lambda i: (i, j))
    pp, pn = _halo_specs(tm, halo, t, w, 0)
    return _pcall(body, name=name, grid=(nt,),
                  in_specs=[col(0), pp, pn, col(1), col(2), _full(pw.shape), _full(pscale.shape),
                            _full(lnw.shape), _full(lnb.shape), _full(sgw.shape), _full(sgb.shape)],
                  out_specs=[pl.BlockSpec((tm, 2 * w), lambda i: (i, 0)), col(0)],
                  out_shape=[_sds((t, 2 * w), BF16), _sds((t, w))],
                  scratch_shapes=[pltpu.VMEM((tm + 2 * halo, w), F32)],
                  compiler_params=_cp(("parallel",)))(p, p, p, p, p, pw, pscale, lnw, lnb, sgw, sgb)


def _odd_mix_bwd1(name, dmix, p, m, pw, pscale, lnw, lnb, sgw, sgb):
    t = p.shape[0]
    tm = ROW_TILE
    w = GROUPS * GC

    def body(dpo_ref, dsg_ref, pu_ref, pv_ref, m_ref, pw_ref, ps_ref, lw_ref, lb_ref, sgw_ref, sgb_ref,
             dm_ref, dpd_ref, vec_ref, dpw_ref, dsgw_ref, dsgb_ref):
        @pl.when(pl.program_id(0) == 0)
        def _():
            vec_ref[...] = jnp.zeros_like(vec_ref)
            dpw_ref[...] = jnp.zeros_like(dpw_ref)
            dsgw_ref[...] = jnp.zeros_like(dsgw_ref)
            dsgb_ref[...] = jnp.zeros_like(dsgb_ref)

        dscale = []
        for g in range(GROUPS):
            sl = slice(g * GC, (g + 1) * GC)
            mg = m_ref[:, sl]
            dpo = dpo_ref[:, sl]
            dscale.append(_colsum(dpo * _dot(mg, pw_ref[g], NN)))
            dpo = dpo * ps_ref[:, sl]
            dm_ref[:, sl] = _dot(dpo, pw_ref[g], NT)
            dpw_ref[g] += _dot(mg, dpo, TN)
        pu, pv = pu_ref[...], pv_ref[...]
        u = _gelu(pu)
        vh, rs = _ln_stats(_gelu(pv))
        lw = lw_ref[...]
        vln = vh * lw + lb_ref[...]
        s = _spatial_gate(vln, sgw_ref, sgb_ref, tm)
        dsg = dsg_ref[...]
        dpd_ref[:, 0:w] = (dsg * s * _dgelu(pu)).astype(BF16)
        ds = dsg * u
        cols = []
        for g in range(GROUPS):
            sl = slice(g * GC, (g + 1) * GC)
            parts = []
            for r0 in range(0, tm, CHUNK):
                dsc = ds[r0:r0 + CHUNK, sl]
                parts.append(_dot(sgw_ref[g], dsc, TN))
                dsgw_ref[g] += _dot(dsc, vln[r0:r0 + CHUNK, sl], NT)
                dsgb_ref[g] += dsc
            cols.append(jnp.concatenate(parts, axis=0))
        dvln = jnp.concatenate(cols, axis=1)
        dpd_ref[:, w:2 * w] = (_ln_bwd(dvln * lw, vh, rs) * _dgelu(pv)).astype(BF16)
        vec_ref[...] += jnp.concatenate([jnp.concatenate(dscale, axis=1), _colsum(dvln * vh), _colsum(dvln),
                                         jnp.zeros((5, w), F32)], axis=0)

        @pl.when(pl.program_id(0) == t // tm - 1)
        def _():
            for g in range(GROUPS):
                dsgb_ref[g] = jnp.broadcast_to(jnp.sum(dsgb_ref[g], axis=1, keepdims=True), (GC, GC))

    col = lambda j: pl.BlockSpec((tm, w), lambda i: (i, j))
    mat = _full((GROUPS, GC, GC))
    return _pcall(body, name=name, grid=(t // tm,),
                  in_specs=[col(0), col(1), col(1), col(2), col(0), _full(pw.shape), _full(pscale.shape),
                            _full(lnw.shape), _full(lnb.shape), _full(sgw.shape), _full(sgb.shape)],
                  out_specs=[col(0), pl.BlockSpec((tm, 2 * w), lambda i: (i, 0)), _full((8, w)), mat, mat, mat],
                  out_shape=[_sds((t, w)), _sds((t, 2 * w), BF16), _sds((8, w)),
                             _sds((GROUPS, GC, GC)), _sds((GROUPS, GC, GC)), _sds((GROUPS, GC, GC))],
                  compiler_params=_cp(("arbitrary",)))(dmix, dmix, p, p, m, pw, pscale, lnw, lnb, sgw, sgb)


def _odd_dp(name, dm, dpd, nct, lc):
    t, w = dm.shape
    tm, halo = ROW_TILE, POOL_HALO
    nt = t // tm

    def body(d_ref, dp_ref, dn_ref, dpd_ref, o_ref, win_ref):
        i = pl.program_id(0)
        vp, vn = _halo_valid(i, nct, nt)
        dcur = d_ref[...]
        _fill_window(win_ref, dp_ref[...], dcur, dn_ref[...], vp, vn, halo, tm)
        for g, wd in enumerate(POOL_WINDOWS):
            sl = slice(g * GC, (g + 1) * GC)
            left = wd // 2
            right = wd - 1 - left
            win_ref[:, sl] = win_ref[:, sl] / _pool_count(i, nct, lc, t, tm, tm + 2 * halo, -halo, left, right)
            s = None
            for o in range(-right, left + 1):
                term = win_ref[halo + o:halo + o + tm, sl]
                s = term if s is None else s + term
            o_ref[:, sl] = (s - dcur[:, sl]).astype(BF16)
        o_ref[:, w:3 * w] = dpd_ref[...]

    row = pl.BlockSpec((tm, w), lambda i: (i, 0))
    pp, pn = _halo_specs(tm, halo, t, w, 0)
    return _pcall(body, name=name, grid=(nt,),
                  in_specs=[row, pp, pn, pl.BlockSpec((tm, 2 * w), lambda i: (i, 0))],
                  out_specs=pl.BlockSpec((tm, 3 * w), lambda i: (i, 0)), out_shape=_sds((t, 3 * w), BF16),
                  scratch_shapes=[pltpu.VMEM((tm + 2 * halo, w), F32)],
                  compiler_params=_cp(("parallel",)))(dm, dm, dm, dpd)


def _place():
    x, y, c = lax.axis_index("x"), lax.axis_index("y"), lax.axis_index("c")
    chips = [(1 - x, y), (x, 1 - y), (1 - x, 1 - y)]
    return x, y, c, chips


def _chip_index(cx, cy):
    return 2 * cx + cy


def _all_gather8(name, blk):
    m_per, n = blk.shape

    def body(x_ref, out_ref, send_sems, recv_sems, local_sem):
        x, y, c, chips = _place()
        me, sibling = (x, y, c), (x, y, 1 - c)

        def rows(px, py, pc):
            return out_ref.at[pl.ds((4 * px + 2 * py + pc) * m_per, m_per), :]

        def copy(k, block, to, src=None):
            return pltpu.make_async_remote_copy(
                src_ref=rows(*block) if src is None else src, dst_ref=rows(*block),
                send_sem=send_sems.at[k], recv_sem=recv_sems.at[k], device_id=to, device_id_type=MESH)

        mine = pltpu.make_async_copy(x_ref, rows(*me), local_sem)
        mine.start()
        first = [copy(0, me, sibling, src=x_ref)]
        first += [copy(1 + j, me, (*chip, c), src=x_ref) for j, chip in enumerate(chips)]
        for cp in first:
            cp.start()
        passed = [copy(4 + j, (*chip, c), sibling) for j, chip in enumerate(chips)]
        for j, chip in enumerate(chips):
            copy(1 + j, (*chip, c), me).wait_recv()
            passed[j].start()
        copy(0, sibling, me).wait_recv()
        for j, chip in enumerate(chips):
            copy(4 + j, (*chip, 1 - c), me).wait_recv()
        for cp in first + passed:
            cp.wait_send()
        mine.wait()

    return _pcall(body, name=name, out_shape=_sds((8 * m_per, n), blk.dtype),
                  in_specs=[pl.BlockSpec(memory_space=pltpu.VMEM)], out_specs=pl.BlockSpec(memory_space=pltpu.VMEM),
                  scratch_shapes=[pltpu.SemaphoreType.DMA((7,)), pltpu.SemaphoreType.DMA((7,)), pltpu.SemaphoreType.DMA],
                  compiler_params=_cp())(blk)


ANY = pl.BlockSpec(memory_space=pl.ANY)


def _half(which, rows):
    return pl.ds(pl.multiple_of(which * rows, 16), rows)


def _gather_weights(name, ws):
    nw = len(ws)
    ns = 7

    def body(*refs):
        w_refs, o_refs = refs[:nw], refs[nw:2 * nw]
        send_sems, recv_sems = refs[2 * nw:]
        x, y, c, chips = _place()
        me_chip = _chip_index(x, y)
        sibling = (x, y, 1 - c)

        def rcopy(t, k, src, dst, to):
            return pltpu.make_async_remote_copy(src_ref=src, dst_ref=dst, send_sem=send_sems.at[t * ns + k],
                                                recv_sem=recv_sems.at[t * ns + k], device_id=to, device_id_type=MESH)

        sends = []
        for t in range(nw):
            lh = w_refs[t].shape[0] // 2
            for k, chip in enumerate(chips):
                sends.append(rcopy(t, k, w_refs[t].at[_half(c, lh)], o_refs[t].at[me_chip, _half(c, lh)], (*chip, c)))
                sends[-1].start()
            sends.append(rcopy(t, 6, w_refs[t], o_refs[t].at[me_chip], sibling))
            sends[-1].start()
        for t in range(nw):
            lh = w_refs[t].shape[0] // 2
            for k, chip in enumerate(chips):
                part = o_refs[t].at[_chip_index(*chip), _half(c, lh)]
                rcopy(t, k, part, part, (*chip, c)).wait_recv()
                sends.append(rcopy(t, 3 + k, part, part, sibling))
                sends[-1].start()
        for t in range(nw):
            lh = w_refs[t].shape[0] // 2
            own = o_refs[t].at[me_chip]
            rcopy(t, 6, own, own, sibling).wait_recv()
            for k, chip in enumerate(chips):
                part = o_refs[t].at[_chip_index(*chip), _half(1 - c, lh)]
                rcopy(t, 3 + k, part, part, sibling).wait_recv()
        for cp in sends:
            cp.wait_send()

    return _pcall(body, name=name, out_shape=[_sds((4,) + w.shape, w.dtype) for w in ws],
                  in_specs=[ANY] * nw, out_specs=[ANY] * nw,
                  scratch_shapes=[pltpu.SemaphoreType.DMA((ns * nw,)), pltpu.SemaphoreType.DMA((ns * nw,))],
                  compiler_params=_cp())(*ws)


def _rs_pair(name, gs):
    ng = len(gs)

    def body(*refs):
        g_refs, o_refs = refs[:ng], refs[ng:2 * ng]
        send_sems, recv_sems = refs[2 * ng:]
        x, y, c, _ = _place()
        cps = []
        for t in range(ng):
            lh = g_refs[t].shape[1] // 2
            cp = pltpu.make_async_remote_copy(
                src_ref=g_refs[t].at[:, _half(1 - c, lh)], dst_ref=o_refs[t],
                send_sem=send_sems.at[t], recv_sem=recv_sems.at[t], device_id=(x, y, 1 - c), device_id_type=MESH)
            cp.start()
            cps.append(cp)
        for cp in cps:
            cp.wait()

    outs = [_sds((g.shape[0], g.shape[1] // 2) + g.shape[2:], g.dtype) for g in gs]
    return _pcall(body, name=name, out_shape=outs, in_specs=[ANY] * ng, out_specs=[ANY] * ng,
                  scratch_shapes=[pltpu.SemaphoreType.DMA((ng,)), pltpu.SemaphoreType.DMA((ng,))],
                  compiler_params=_cp())(*gs)


def _rs_chips(name, ps):
    ng = len(ps)

    def body(*refs):
        p_refs, o_refs = refs[:ng], refs[ng:2 * ng]
        send_sems, recv_sems = refs[2 * ng:]
        x, y, c, chips = _place()
        cps = []
        for t in range(ng):
            for k, chip in enumerate(chips):
                cp = pltpu.make_async_remote_copy(
                    src_ref=p_refs[t].at[_chip_index(*chip)], dst_ref=o_refs[t].at[k],
                    send_sem=send_sems.at[t * 3 + k], recv_sem=recv_sems.at[t * 3 + k],
                    device_id=(*chip, c), device_id_type=MESH)
                cp.start()
                cps.append(cp)
        for cp in cps:
            cp.wait()

    return _pcall(body, name=name, out_shape=[_sds((3,) + p.shape[1:], p.dtype) for p in ps],
                  in_specs=[ANY] * ng, out_specs=[ANY] * ng,
                  scratch_shapes=[pltpu.SemaphoreType.DMA((3 * ng,)), pltpu.SemaphoreType.DMA((3 * ng,))],
                  compiler_params=_cp())(*ps)


def _rs_share(name, ss):
    ng = len(ss)

    def body(*refs):
        o_refs = refs[ng:2 * ng]
        send_sems, recv_sems = refs[2 * ng:]
        x, y, c, _ = _place()
        cps = []
        for t in range(ng):
            lh = o_refs[t].shape[1] // 2
            mine = o_refs[t].at[:, _half(c, lh)]
            cp = pltpu.make_async_remote_copy(
                src_ref=mine, dst_ref=mine, send_sem=send_sems.at[t], recv_sem=recv_sems.at[t],
                device_id=(x, y, 1 - c), device_id_type=MESH)
            cp.start()
            cps.append(cp)
        for t in range(ng):
            lh = o_refs[t].shape[1] // 2
            cps[t].wait_send()
            theirs = o_refs[t].at[:, _half(1 - c, lh)]
            pltpu.make_async_remote_copy(
                src_ref=theirs, dst_ref=theirs, send_sem=send_sems.at[t], recv_sem=recv_sems.at[t],
                device_id=(x, y, 1 - c), device_id_type=MESH).wait_recv()

    return _pcall(body, name=name, out_shape=[_sds(s.shape, s.dtype) for s in ss],
                  in_specs=[ANY] * ng, out_specs=[ANY] * ng, input_output_aliases={t: t for t in range(ng)},
                  scratch_shapes=[pltpu.SemaphoreType.DMA((ng,)), pltpu.SemaphoreType.DMA((ng,))],
                  compiler_params=_cp())(*ss)


HBM = pl.BlockSpec(memory_space=pltpu.HBM)
SEMS = pl.BlockSpec(memory_space=pltpu.SEMAPHORE)
EFFECT = pltpu.SideEffectType.DATAFLOW_SIDE_EFFECTING
TOKEN = (8, 128)


def _in_hbm(a):
    return pltpu.with_memory_space_constraint(a, pltpu.HBM)


def _split_start(name, srcs, lands, copies, after):
    ns, nl, na = len(srcs), len(lands), len(after)
    ncopies = len(copies([s for s in srcs], [l for l in lands], probe=True))

    def body(*refs):
        src_refs, land_refs = refs[:ns], refs[ns:ns + nl]
        send_sems, recv_sems = refs[ns + nl + na], refs[ns + nl + na + 1]
        token = refs[-1]
        for k, (src, dst, to) in enumerate(copies(src_refs, land_refs)):
            pltpu.make_async_remote_copy(src_ref=src, dst_ref=dst, send_sem=send_sems.at[k], recv_sem=recv_sems.at[k],
                                         device_id=to, device_id_type=MESH).start()
        token[...] = jnp.zeros_like(token)

    thru = [pltpu.HBM(a.shape, a.dtype) for a in list(srcs) + list(lands)]
    outs = _pcall(body, name=name,
                  out_shape=(pltpu.SemaphoreType.DMA((ncopies,)), pltpu.SemaphoreType.DMA((ncopies,)), *thru, _sds(TOKEN)),
                  in_specs=[HBM] * (ns + nl) + [ANY] * na,
                  out_specs=(SEMS, SEMS, *([HBM] * (ns + nl)), pl.BlockSpec(memory_space=pltpu.VMEM)),
                  input_output_aliases={t: 2 + t for t in range(ns + nl)},
                  compiler_params=pltpu.CompilerParams(has_side_effects=EFFECT))(
        *[_in_hbm(a) for a in list(srcs) + list(lands)], *after)
    return outs[0], outs[1], list(outs[2:2 + ns]), list(outs[2 + ns:2 + ns + nl]), outs[-1]


def _split_wait(name, started, copies, after):
    send_sems, recv_sems, srcs, lands, _ = started
    ns, nl, na = len(srcs), len(lands), len(after)

    def body(*refs):
        src_refs, land_refs = refs[:ns], refs[ns:ns + nl]
        send_sems_ref, recv_sems_ref = refs[ns + nl], refs[ns + nl + 1]
        for k, (src, dst, to) in enumerate(copies(src_refs, land_refs)):
            cp = pltpu.make_async_remote_copy(src_ref=src, dst_ref=dst, send_sem=send_sems_ref.at[k],
                                              recv_sem=recv_sems_ref.at[k], device_id=to, device_id_type=MESH)
            cp.wait_send()
            cp.wait_recv()

    thru = [pltpu.HBM(a.shape, a.dtype) for a in list(srcs) + list(lands)]
    outs = _pcall(body, name=name, out_shape=tuple(thru),
                  in_specs=[HBM] * (ns + nl) + [SEMS, SEMS] + [ANY] * na, out_specs=tuple([HBM] * (ns + nl)),
                  input_output_aliases={t: t for t in range(ns + nl)},
                  compiler_params=pltpu.CompilerParams(has_side_effects=EFFECT))(
        *srcs, *lands, send_sems, recv_sems, *after)
    return list(outs[ns:])


def _gather_copies(n):
    def copies(src_refs, land_refs, probe=False):
        if probe:
            return [None] * (4 * n)
        x, y, c, chips = _place()
        me_chip = _chip_index(x, y)
        out = []
        for t in range(n):
            for to in [(*chip, c) for chip in chips] + [(x, y, 1 - c)]:
                out.append((src_refs[t], land_refs[t].at[me_chip], to))
        return out
    return copies


def _scatter_copies(n):
    def copies(src_refs, land_refs, probe=False):
        if probe:
            return [None] * (3 * n)
        x, y, c, chips = _place()
        out = []
        for t in range(n):
            for k, chip in enumerate(chips):
                out.append((src_refs[t].at[_chip_index(*chip)], land_refs[t].at[k], (*chip, c)))
        return out
    return copies


def _row_block(r, cn):
    if r % 8:
        return r
    best = 8
    for d in range(8, r + 1, 8):
        if r % d == 0 and d * cn * 4 <= (1 << 20):
            best = d
    return best


def _add_half(name, g, a, idx):
    j, rh, cn = a.shape
    tr = _row_block(rh, cn)
    nb = rh // tr

    def body(i_ref, g_ref, a_ref, o_ref):
        o_ref[...] = (g_ref[...] + a_ref[...]).astype(BF16)

    blk = (None, tr, cn)
    gs = pltpu.PrefetchScalarGridSpec(
        num_scalar_prefetch=1, grid=(j, nb),
        in_specs=[pl.BlockSpec(blk, lambda jj, i, i_ref: (jj, i_ref[0] * nb + i, 0)),
                  pl.BlockSpec(blk, lambda jj, i, i_ref: (jj, i, 0))],
        out_specs=pl.BlockSpec(blk, lambda jj, i, i_ref: (jj, i, 0)))
    return _pcall(body, name=name, grid_spec=gs, out_shape=_sds(a.shape, BF16),
                  compiler_params=_cp(("parallel", "parallel")))(idx, g, a)


def _sum_final(name, g, a, b, idx, buf, lyr, nlyr):
    _, r, cn = g.shape
    rh = r // 2
    tr = _row_block(rh, cn)
    nb = rh // tr

    def body(*refs):
        g_ref, a_ref, b_ref = refs[1:4]
        o_ref = refs[-1]
        own = g_ref[...] + a_ref[...]
        o_ref[...] = (own + b_ref[0].astype(F32)) + (b_ref[1].astype(F32) + b_ref[2].astype(F32))

    blk = (None, tr, cn)
    in_specs = [pl.BlockSpec(blk, lambda i, i_ref: (i_ref[1], i_ref[0] * nb + i, 0)),
                pl.BlockSpec(blk, lambda i, i_ref: (i_ref[1], i, 0)),
                pl.BlockSpec((3, tr, cn), lambda i, i_ref: (0, i, 0))]
    args = [idx, g, a, b]
    kw = {}
    if buf is not None:
        in_specs.append(ANY)
        args.append(buf)
        kw["input_output_aliases"] = {4: 0}
    gs = pltpu.PrefetchScalarGridSpec(
        num_scalar_prefetch=1, grid=(nb,), in_specs=in_specs,
        out_specs=pl.BlockSpec(blk, lambda i, i_ref: (lyr, i_ref[0] * nb + i, 0)))
    return _pcall(body, name=name, grid_spec=gs, out_shape=_sds((nlyr, r, cn)),
                  compiler_params=_cp(("parallel",)), **kw)(*args)


def _sum8(name, g):
    _, r, n = g.shape
    tr = 8

    def body(g_ref, o_ref):
        o_ref[...] = ((g_ref[0] + g_ref[1]) + (g_ref[2] + g_ref[3])) + ((g_ref[4] + g_ref[5]) + (g_ref[6] + g_ref[7]))

    return _pcall(body, name=name, grid=(r // tr,), in_specs=[pl.BlockSpec((8, tr, n), lambda i: (0, i, 0))],
                  out_specs=pl.BlockSpec((tr, n), lambda i: (i, 0)), out_shape=_sds((r, n)),
                  compiler_params=_cp(("parallel",)))(g)


def _ada_mod(name, c16, ada_w, bias):
    nl, dm, n = ada_w.shape

    def body(c_ref, w_ref, b_ref, o_ref):
        o_ref[...] = _dot(_silu(c_ref[...]), w_ref[...], NN) + b_ref[...]

    return _pcall(body, name=name, grid=(nl,),
                  in_specs=[_full(c16.shape), pl.BlockSpec((None, dm, n), lambda i: (i, 0, 0)),
                            pl.BlockSpec((None, 1, n), lambda i: (i, 0, 0))],
                  out_specs=pl.BlockSpec((None, 16, n), lambda i: (i, 0, 0)), out_shape=_sds((nl, 16, n)),
                  compiler_params=_cp(("parallel",)))(c16, ada_w, bias)


def _ada_bwd(name, c16, dmod, ada_w):
    nl, dm, n = ada_w.shape

    def body(c_ref, d_ref, w_ref, gw_ref, dc_ref):
        @pl.when(pl.program_id(0) == 0)
        def _():
            dc_ref[...] = jnp.zeros_like(dc_ref)

        dv = d_ref[...]
        gw_ref[...] = _dot(_silu(c_ref[...]), dv, TN)
        dc_ref[...] += _dot(dv, w_ref[...], NT)

    return _pcall(body, name=name, grid=(nl,),
                  in_specs=[_full(c16.shape), pl.BlockSpec((None, 16, n), lambda i: (i, 0, 0)),
                            pl.BlockSpec((None, dm, n), lambda i: (i, 0, 0))],
                  out_specs=[pl.BlockSpec((None, dm, n), lambda i: (i, 0, 0)), _full((16, dm))],
                  out_shape=[_sds((nl, dm, n)), _sds((16, dm))],
                  compiler_params=_cp(("arbitrary",)))(c16, dmod, ada_w)


def _rowsum16(name, dmod):
    nl, _, n = dmod.shape

    def body(d_ref, o_ref):
        o_ref[...] = _colsum(d_ref[...])

    return _pcall(body, name=name, grid=(nl,), in_specs=[pl.BlockSpec((None, 16, n), lambda i: (i, 0, 0))],
                  out_specs=pl.BlockSpec((None, 1, n), lambda i: (i, 0, 0)), out_shape=_sds((nl, 1, n)),
                  compiler_params=_cp(("parallel",)))(dmod)


def _cctx_grad(name, parts, c_ctx):
    def body(p_ref, c_ref, o_ref):
        tot = (p_ref[0:1, :] + p_ref[1:2, :]) + (p_ref[2:3, :] + p_ref[3:4, :])
        o_ref[...] = tot * _dsilu(c_ref[...])

    return _pcall(body, name=name, out_shape=_sds(c_ctx.shape), compiler_params=_cp())(parts, c_ctx)


def _adamw(name, w, g, m, v):
    shape = w.shape
    cn = shape[-1]
    r = math.prod(shape[:-1]) if len(shape) > 1 else 1
    tr = _row_block(r, cn)
    c1 = 1.0 - ADAM_B1 ** ADAM_STEP
    c2 = 1.0 - ADAM_B2 ** ADAM_STEP

    def body(w_ref, g_ref, m_ref, v_ref, d_ref, mo_ref, vo_ref):
        gv = g_ref[...]
        mn = ADAM_B1 * m_ref[...] + (1.0 - ADAM_B1) * gv
        vn = ADAM_B2 * v_ref[...] + (1.0 - ADAM_B2) * (gv * gv)
        d_ref[...] = -ADAM_LR * ((mn / c1) / (jnp.sqrt(vn / c2) + ADAM_EPS) + ADAM_WD * w_ref[...])
        mo_ref[...] = mn
        vo_ref[...] = vn

    blk = pl.BlockSpec((tr, cn), lambda i: (i, 0))
    o = _sds((r, cn))
    outs = _pcall(body, name=name, grid=(r // tr,), in_specs=[blk] * 4, out_specs=[blk] * 3, out_shape=[o, o, o],
                  compiler_params=_cp(("parallel",)))(*[a.reshape(r, cn) for a in (w, g, m, v)])
    return tuple(a.reshape(shape) for a in outs)


def _local_step(xs, target, modt, nw, fnw, get_w, put_g, ev, od, lc):
    t, dm = xs.shape
    nct = lc // ROW_TILE
    depth = nw.shape[0]
    cs, sn = _rope_tables(t, lc)
    saved = []
    x_in, x1p, fp = xs, None, None
    for i in range(depth):
        j, even = i // 2, i % 2 == 0
        tag = f"l{i}"
        w, deps = get_w(i, [fp] if i else [])
        if i == 0:
            _, h = _rnm(tag + "_norm1", x_in, None, None, 0, modt[0], 0, 1, nw[0, 0], nct, deps)
        else:
            x_in, h = _rnm(tag + "_norm1", x1p, fp, modt[i - 1], 5, modt[i], 0, 1, nw[i, 0], nct, deps)
        s = dict(x=x_in, h=h, w=w)
        if even:
            p = _mm_cols(tag + "_in", h, w["in"])
            q, k, v = _even_qkv(tag + "_qkv", p, cs, sn)
            of, ob, ss = _retention_fwd(tag + "_ret", q, k, v, ev["lgb"][j], lc)
            mix, yc = _even_mix(tag + "_mix", p, of, ob, ev["cw"][j], ev["lnw"][j], ev["lnb"][j], nct)
            y = _mm_full(tag + "_out", mix, w["out"], NN)
            s.update(p=p, q=q, k=k, v=v, of=of, ob=ob, ss=ss, yc=yc)
        else:
            p = _mm_cols(tag + "_in", h, w["in"])
            mix, m = _odd_mix(tag + "_mix", p, od["pw"][j], od["ps"][j], od["lnw"][j], od["lnb"][j],
                              od["sgw"][j], od["sgb"][j], nct, lc)
            y = _mm_full(tag + "_out", mix, w["out"], NN)
            s.update(p=p, m=m)
        x1, h2 = _rnm(tag + "_norm2", x_in, y, modt[i], 2, modt[i], 3, 4, nw[i, 1], nct)
        a, gt, up = _ffn_up(tag + "_ffn_up", h2, w["gate"], w["up"])
        f = _mm_full(tag + "_ffn_down", a, w["down"], NN)
        s.update(mix=mix, y=y, x1=x1, h2=h2, a=a, gt=gt, up=up, f=f)
        saved.append(s)
        x1p, fp = x1, f

    loss_blk, dx, df, fin_s = _fin("final", x1p, fp, modt[depth - 1], 5, fnw, target, nct)

    deps = []
    dmod = [[None] * 6 for _ in range(depth)]
    dnw = [[None, None] for _ in range(depth)]
    zero2 = jnp.zeros((2, dm), F32)
    dmod[depth - 1][5] = jnp.stack([zero2[0], fin_s[0]])
    small = dict(dfnw=fin_s[1], ev=[], od=[])
    for i in reversed(range(depth)):
        j, even = i // 2, i % 2 == 0
        tag = f"l{i}b"
        s = saved[i]
        w = s["w"]
        fh = w["down"].shape[0] // 2
        g = {}
        dgt, dup = _ffn_down_bwd(tag + "_ffn_down", df, w["down"], s["gt"], s["up"])
        g["down"] = _wgrad_rows(tag + "_gdown", s["a"], fh, df)
        g["gate"] = _wgrad_rows(tag + "_ggate", dgt, fh, s["h2"])
        g["up"] = _wgrad_rows(tag + "_gup", dup, fh, s["h2"])
        dh2 = _ffn_in_bwd(tag + "_ffn_in", dgt, dup, w["gate"], w["up"])
        dx1, dy, s2 = _bnm(tag + "_norm2", s["x1"], dh2, dx, s["y"], modt[i], 3, 4, modt[i], 2, nw[i, 1], nct)
        dmod[i][3], dmod[i][4], dmod[i][2] = s2[:, 0], s2[:, 1], s2[:, 2]
        dnw[i][1] = s2[1, 3]
        dmix = _mm_full(tag + "_out", dy, w["out"], NT)
        g["out"] = _wgrad_rows(tag + "_gout", s["mix"], w["out"].shape[0] // 4, dy)
        if even:
            do, dg, dyc, lns = _even_mix_bwd1(tag + "_mix1", dmix, s["p"], s["of"], s["ob"], s["yc"],
                                              ev["lnw"][j], ev["lnb"][j])
            da, dgb, dcw = _even_conv_bwd(tag + "_conv", dyc, s["p"], ev["cw"][j], nct)
            dqf, dkf, dvf, dqb, dkb, dvb, dl = _retention_bwd(tag + "_ret", s["q"], s["k"], s["v"], do, s["ss"],
                                                              ev["lgb"][j], lc)
            dp = _even_dp(tag + "_dp", (dqf, dqb), (dkf, dkb), (dvf, dvb), dg, da, dgb, cs, sn)
            small["ev"].append(dict(lnw=lns[0], lnb=lns[1], cw=dcw, dl=dl[:, 0]))
        else:
            dm_, dpd, vec, dpw, dsgw, dsgb = _odd_mix_bwd1(tag + "_mix1", dmix, s["p"], s["m"], od["pw"][j], od["ps"][j],
                                                           od["lnw"][j], od["lnb"][j], od["sgw"][j], od["sgb"][j])
            dp = _odd_dp(tag + "_dp", dm_, dpd, nct, lc)
            small["od"].append(dict(ps=vec[0], lnw=vec[1], lnb=vec[2], pw=dpw, sgw=dsgw, sgb=dsgb[:, :, 0]))
        dh = _mm_cols_bwd(tag + "_in", dp, w["in"])
        g["in"] = _wgrad_cols(tag + "_gin", s["h"], dp, w["in"].shape[0])
        deps = put_g(i, g)
        if i > 0:
            dx, df, s1 = _bnm(tag + "_norm1", s["x"], dh, dx1, saved[i - 1]["f"], modt[i], 0, 1, modt[i - 1], 5,
                              nw[i, 0], nct, deps)
            dmod[i - 1][5] = s1[:, 2]
        else:
            dx, _, s1 = _bnm(tag + "_norm1", s["x"], dh, dx1, None, modt[0], 0, 1, None, 0, nw[0, 0], nct, deps)
        dmod[i][0], dmod[i][1] = s1[:, 0], s1[:, 1]
        dnw[i][0] = s1[1, 3]
    small["ev"].reverse()
    small["od"].reverse()
    dmod_t = jnp.stack([jnp.concatenate([jnp.stack(rows, axis=1), jnp.zeros((2, 2, dm), F32)], axis=1) for rows in dmod])
    small["dmod"] = dmod_t
    small["dnw"] = jnp.stack([jnp.stack(r) for r in dnw])
    return loss_blk, dx, small


WEIGHTS = ["c_ctx", "ada_w", "ada_b", "norm_w", "even_w_in", "even_w_out", "ret_decay_logit", "conv_dw_w",
           "conv_ln_w", "conv_ln_b", "odd_w_in", "odd_w_out", "pool_w", "pool_scale", "sg_ln_w", "sg_ln_b",
           "sg_w", "sg_b", "ffn_w_gate", "ffn_w_up", "ffn_w_down", "final_norm_w"]
BIG = dict(even_in="even_w_in", even_out="even_w_out", odd_in="odd_w_in", odd_out="odd_w_out",
           gate="ffn_w_gate", up="ffn_w_up", down="ffn_w_down")


def _rows(a, width=1024):
    flat = a.reshape(-1)
    n = flat.shape[0]
    per = 8 * width
    tot = -(-n // per) * per
    return jnp.pad(flat, (0, tot - n)).reshape(tot // width, width)


def _unshard(parts, lead):
    nl = len(lead)
    perm = tuple(range(1, nl + 1)) + (0, nl + 1)
    return parts.transpose(perm).reshape(tuple(lead) + (4 * parts.shape[-1],))


def _my_cols(a, chip, n):
    start = (0,) * (a.ndim - 1) + (chip * n,)
    return lax.dynamic_slice(a, start, a.shape[:-1] + (n,))


def kernel(x, c, ctx, c_ctx, ada_w, ada_b, norm_w, even_w_in, even_w_out, ret_decay_logit, conv_dw_w, conv_ln_w, conv_ln_b, odd_w_in, odd_w_out, pool_w, pool_scale, sg_ln_w, sg_ln_b, sg_w, sg_b, ffn_w_gate, ffn_w_up, ffn_w_down, final_norm_w, loss_target, m_c_ctx, m_ada_w, m_ada_b, m_norm_w, m_even_w_in, m_even_w_out, m_ret_decay_logit, m_conv_dw_w, m_conv_ln_w, m_conv_ln_b, m_odd_w_in, m_odd_w_out, m_pool_w, m_pool_scale, m_sg_ln_w, m_sg_ln_b, m_sg_w, m_sg_b, m_ffn_w_gate, m_ffn_w_up, m_ffn_w_down, m_final_norm_w, v_c_ctx, v_ada_w, v_ada_b, v_norm_w, v_even_w_in, v_even_w_out, v_ret_decay_logit, v_conv_dw_w, v_conv_ln_w, v_conv_ln_b, v_odd_w_in, v_odd_w_out, v_pool_w, v_pool_scale, v_sg_ln_w, v_sg_ln_b, v_sg_w, v_sg_b, v_ffn_w_gate, v_ffn_w_up, v_ffn_w_down, v_final_norm_w):
    wv = dict(c_ctx=c_ctx, ada_w=ada_w, ada_b=ada_b, norm_w=norm_w, even_w_in=even_w_in, even_w_out=even_w_out,
              ret_decay_logit=ret_decay_logit, conv_dw_w=conv_dw_w, conv_ln_w=conv_ln_w, conv_ln_b=conv_ln_b,
              odd_w_in=odd_w_in, odd_w_out=odd_w_out, pool_w=pool_w, pool_scale=pool_scale, sg_ln_w=sg_ln_w,
              sg_ln_b=sg_ln_b, sg_w=sg_w, sg_b=sg_b, ffn_w_gate=ffn_w_gate, ffn_w_up=ffn_w_up,
              ffn_w_down=ffn_w_down, final_norm_w=final_norm_w)
    mv = dict(zip(WEIGHTS, (m_c_ctx, m_ada_w, m_ada_b, m_norm_w, m_even_w_in, m_even_w_out, m_ret_decay_logit,
                            m_conv_dw_w, m_conv_ln_w, m_conv_ln_b, m_odd_w_in, m_odd_w_out, m_pool_w, m_pool_scale,
                            m_sg_ln_w, m_sg_ln_b, m_sg_w, m_sg_b, m_ffn_w_gate, m_ffn_w_up, m_ffn_w_down,
                            m_final_norm_w)))
    vv = dict(zip(WEIGHTS, (v_c_ctx, v_ada_w, v_ada_b, v_norm_w, v_even_w_in, v_even_w_out, v_ret_decay_logit,
                            v_conv_dw_w, v_conv_ln_w, v_conv_ln_b, v_odd_w_in, v_odd_w_out, v_pool_w, v_pool_scale,
                            v_sg_ln_w, v_sg_ln_b, v_sg_w, v_sg_b, v_ffn_w_gate, v_ffn_w_up, v_ffn_w_down,
                            v_final_norm_w)))
    xi, yi, ci = lax.axis_index("x"), lax.axis_index("y"), lax.axis_index("c")
    chip = 2 * xi + yi
    dev = 4 * xi + 2 * yi + ci
    dm = x.shape[-1]
    lc = ctx.shape[1]
    depth = ada_w.shape[0]
    n_ada = ada_w.shape[-1]

    cw_pad = jnp.pad(conv_dw_w, ((0, 0), (0, 1), (0, 0)))
    vec3 = jnp.stack([pool_scale, sg_ln_w, sg_ln_b])
    pack1 = jnp.concatenate([_rows(c), _rows(norm_w), _rows(cw_pad), _rows(vec3)], axis=0)
    g1 = _all_gather8("gather_small", pack1).reshape(8, 32, dm)
    c_all = g1[:, 0]
    per_chip = g1[0::2]
    norm_full = _unshard(per_chip[:, 8:10].reshape(4, depth, 2, dm // 4), (depth, 2))
    cw_full = _unshard(per_chip[:, 16:24].reshape(4, 2, CONV_K + 1, 128), (2, CONV_K + 1))
    vec_full = _unshard(per_chip[:, 24, :768].reshape(4, 3, 2, 128), (3, 2))

    c16 = jnp.concatenate([c_all, c_ctx[None, :], jnp.zeros((7, dm), F32)], axis=0)
    mod_sh = _ada_mod("ada_mod", c16, ada_w, _my_cols(ada_b, chip, n_ada)[:, None, :])
    g2 = _all_gather8("gather_mod", mod_sh.reshape(depth * 16, n_ada)).reshape(8, depth, 16, n_ada)
    mod_full = _unshard(g2[0::2], (depth, 16))
    mod_x = lax.dynamic_index_in_dim(mod_full, dev, axis=1, keepdims=False).reshape(depth, 6, dm)
    mod_c = mod_full[:, 8].reshape(depth, 6, dm)
    modt = jnp.pad(jnp.stack([mod_c, mod_x], axis=1), ((0, 0), (0, 0), (0, 2), (0, 0)))

    names = list(BIG)
    tr_names = ("gate", "up")
    shard = {k: (jnp.swapaxes(wv[BIG[k]], 1, 2) if k in tr_names else wv[BIG[k]]).astype(BF16) for k in names}
    roles = ("in", "out", "gate", "up", "down")

    def layer_keys(i):
        mixer = ("even_in", "even_out") if i % 2 == 0 else ("odd_in", "odd_out")
        return [(k, i // 2) for k in mixer] + [(k, i) for k in ("gate", "up", "down")]

    def as_used(got):
        return {r: (g if r == "in" else g.reshape(4 * g.shape[1], g.shape[2])) for r, g in zip(roles, got)}

    started = {}

    def get_w(i, after):
        if i > 0:
            return as_used(_split_wait(f"gather_wait{i}", started[i], _gather_copies(len(roles)), after)), []
        got = _gather_weights("gather_w0", [shard[k][l] for k, l in layer_keys(0)])
        for li in range(1, depth):
            srcs = [shard[k][l] for k, l in layer_keys(li)]
            lands = [lax.empty((4,) + s.shape, s.dtype) for s in srcs]
            before = [got[0]] + ([started[li - 1][4]] if li > 1 else [])
            started[li] = _split_start(f"gather_start{li}", srcs, lands, _gather_copies(len(roles)), before)
        return as_used(got), [started[li][4] for li in range(1, depth)]

    idx = jnp.stack([ci, chip]).astype(jnp.int32)
    pending = {}

    def put_g(i, g):
        glist = [g[r].reshape(4, -1, g[r].shape[-1]) for r in roles]
        from_sib = _rs_pair(f"rs_pair{i}", glist)
        pair = [_add_half(f"rs_add{i}_{r}", gl, a, idx) for r, gl, a in zip(roles, glist, from_sib)]
        lands = [lax.empty((3,) + p.shape[1:], p.dtype) for p in pair]
        st = _split_start(f"rs_start{i}", pair, lands, _scatter_copies(len(roles)), [])
        pending[i] = (glist, from_sib, st)
        return [st[4]]

    ev = dict(lgb=jnp.broadcast_to(ret_decay_logit.reshape(-1, 2 * HEADS)[:, :, None], (ret_decay_logit.shape[0], 2 * HEADS, HEAD_DIM)),
              cw=cw_full, lnw=conv_ln_w[:, None, :], lnb=conv_ln_b[:, None, :])
    od = dict(pw=pool_w, ps=vec_full[0][:, None, :], lnw=vec_full[1][:, None, :], lnb=vec_full[2][:, None, :],
              sgw=sg_w, sgb=jnp.broadcast_to(sg_b[:, :, :, None], sg_b.shape + (GC,)))
    xs = jnp.concatenate([ctx[0], x[0]], axis=0)
    loss_blk, dxs, small = _local_step(xs, loss_target[0], modt, norm_full[:, :, None, :], final_norm_w[None, :],
                                       get_w, put_g, ev, od, lc)

    misc = jnp.stack([
        small["dfnw"], jnp.broadcast_to(loss_blk[0, 0], (dm,)),
        jnp.concatenate([e["lnw"] for e in small["ev"]]), jnp.concatenate([e["lnb"] for e in small["ev"]]),
        jnp.concatenate([o["ps"] for o in small["od"]]), jnp.concatenate([o["lnw"] for o in small["od"]]),
        jnp.concatenate([o["lnb"] for o in small["od"]]),
        jnp.pad(jnp.concatenate([e["dl"] for e in small["ev"]]), (0, dm - 4 * HEADS)),
        jnp.stack([o["sgb"] for o in small["od"]]).reshape(-1)])
    pack2 = jnp.concatenate([
        _rows(small["dmod"]), _rows(small["dnw"]), _rows(misc), _rows(jnp.stack([e["cw"] for e in small["ev"]])),
        _rows(jnp.stack([o["pw"] for o in small["od"]])), _rows(jnp.stack([o["sgw"] for o in small["od"]]))], axis=0)
    n2 = pack2.shape[0]
    g3 = _all_gather8("gather_grads", pack2)
    tot = _sum8("sum_grads", g3.reshape(8, n2, dm))
    r_mod = depth * 16
    o_nw, o_misc = r_mod, r_mod + 8
    o_cw = o_misc + 16
    o_pw = o_cw + 2 * (CONV_K + 1) // 2
    o_sgw = o_pw + 128
    dmod_sum = tot[:r_mod].reshape(depth, 2, 8, dm)
    dmod_dev = g3.reshape(8, n2, dm)[:, :r_mod].reshape(8, depth, 2, 8, dm)
    dm_x = dmod_dev[:, :, 1, :6].reshape(8, depth, 6 * dm).transpose(1, 0, 2)
    dm_c = dmod_sum[:, 0, :6].reshape(depth, 1, 6 * dm)
    dmod16 = jnp.concatenate([dm_x, dm_c, jnp.zeros((depth, 7, 6 * dm), F32)], axis=1)
    g_ada_b = _rowsum16("ada_b_grad", dmod16)[:, 0]
    g_ada_w, dc16 = _ada_bwd("ada_bwd", c16, _my_cols(dmod16, chip, n_ada), ada_w)
    g4 = _all_gather8("gather_cctx", dc16[8:16]).reshape(8, 8, dm)
    g_c_ctx = _cctx_grad("cctx_grad", g4[0::2, 0], c_ctx[None, :])[0]

    misc_t = tot[o_misc:o_misc + 16]
    half = lambda row: misc_t[row].reshape(2, dm // 2)
    grads = dict(
        c_ctx=g_c_ctx, ada_w=g_ada_w, ada_b=g_ada_b,
        norm_w=_my_cols(tot[o_nw:o_nw + 8].reshape(depth, 2, dm), chip, dm // 4),
        ret_decay_logit=misc_t[7, :4 * HEADS].reshape(ret_decay_logit.shape),
        conv_dw_w=_my_cols(tot[o_cw:o_cw + 2 * (CONV_K + 1) // 2].reshape(2, CONV_K + 1, dm // 2)[:, :CONV_K], chip, 128),
        conv_ln_w=half(2), conv_ln_b=half(3),
        pool_w=tot[o_pw:o_pw + 128].reshape(pool_w.shape),
        pool_scale=_my_cols(half(4), chip, 128), sg_ln_w=_my_cols(half(5), chip, 128), sg_ln_b=_my_cols(half(6), chip, 128),
        sg_w=tot[o_sgw:o_sgw + 128].reshape(sg_w.shape), sg_b=misc_t[8].reshape(sg_b.shape),
        final_norm_w=misc_t[0])
    loss = misc_t[1, 0]

    reduced = {k: None for k in names}
    for i in reversed(range(depth)):
        glist, from_sib, st = pending[i]
        slots = _split_wait(f"rs_wait{i}", st, _scatter_copies(len(roles)), [g_c_ctx])
        for (k, l), g, a, b in zip(layer_keys(i), glist, from_sib, slots):
            reduced[k] = _sum_final(f"rs_sum_{k}{l}", g, a, b, idx, reduced[k], l, shard[k].shape[0])
    shards = dict(zip(names, _rs_share("rs_share", [reduced[k] for k in names])))

    deltas, new_m, new_v = {}, {}, {}
    for k in names:
        n = BIG[k]
        if k in tr_names:
            tr = lambda a: jnp.swapaxes(a, 1, 2)
            outs = _adamw("adamw_" + n, tr(wv[n]), shards[k], tr(mv[n]), tr(vv[n]))
            grads[n] = tr(shards[k])
            deltas[n], new_m[n], new_v[n] = (tr(o) for o in outs)
        else:
            grads[n] = shards[k]
    for n in WEIGHTS:
        if n not in deltas:
            deltas[n], new_m[n], new_v[n] = _adamw("adamw_" + n, wv[n], grads[n], mv[n], vv[n])
    grad_x = dxs[lc:][None]
    return (loss, grad_x, *[grads[n] for n in WEIGHTS], *[deltas[n] for n in WEIGHTS],
            *[new_m[n] for n in WEIGHTS], *[new_v[n] for n in WEIGHTS])
```

```python
import functools
import math

import jax
import jax.numpy as jnp
from jax import lax
from jax.experimental import pallas as pl
from jax.experimental.pallas import tpu as pltpu

F32 = jnp.float32
BF16 = jnp.bfloat16
MESH = pl.DeviceIdType.MESH

EPS = 1e-6
GRID_W = 64
HEADS = 4
HEAD_DIM = 128
CHUNK = 128
CONV_K = 31
ROPE_BASE = 10000.0
ROPE_PAIRS = (16, 24, 24)
POOL_WINDOWS = (2, 4, 8, 16)
ADAM_LR, ADAM_B1, ADAM_B2, ADAM_EPS, ADAM_WD, ADAM_STEP = 0.001, 0.9, 0.999, 1e-08, 0.01, 10

ROW_TILE = 256
CONV_HALO = 16
POOL_HALO = 8
VMEM_LIMIT = 56 * 1024 * 1024


def _pcall(body, **kw):
    return pl.pallas_call(body, **kw)


def _cp(sem=None, vmem=VMEM_LIMIT):
    if sem is None:
        return pltpu.CompilerParams(vmem_limit_bytes=vmem)
    return pltpu.CompilerParams(dimension_semantics=sem, vmem_limit_bytes=vmem)


def _sds(shape, dtype=F32):
    return jax.ShapeDtypeStruct(tuple(shape), dtype)


def _full(shape):
    nd = len(shape)
    return pl.BlockSpec(tuple(shape), lambda *_: (0,) * nd)


def _sigmoid(x):
    return jax.nn.sigmoid(x)


def _silu(x):
    return x * _sigmoid(x)


def _dsilu(x):
    s = _sigmoid(x)
    return s * (1.0 + x * (1.0 - s))


def _colsum(a):
    return jnp.sum(a, axis=0, keepdims=True)


def _dot(a, b, dn):
    return lax.dot_general(a.astype(BF16), b.astype(BF16), dn, preferred_element_type=F32)


NN = (((1,), (0,)), ((), ()))
NT = (((1,), (1,)), ((), ()))
TN = (((0,), (0,)), ((), ()))


def _mm_tile(t):
    best = 16
    for d in range(16, min(t, 1152) + 1, 16):
        if t % d == 0:
            best = d
    return best


def _mm(name, pairs, grid, out_shape, out_spec, dn):
    npairs = len(pairs)
    nk = grid[-1]
    kax = len(grid) - 1
    assert nk == 1 or out_shape.dtype == F32

    def body(*refs):
        ins = refs[:2 * npairs]
        o_ref = refs[2 * npairs]
        tot = None
        for p in range(npairs):
            d = _dot(ins[2 * p][...], ins[2 * p + 1][...], dn)
            tot = d if tot is None else tot + d
        if nk == 1:
            o_ref[...] = tot.astype(o_ref.dtype)
        else:
            k = pl.program_id(kax)

            @pl.when(k == 0)
            def _():
                o_ref[...] = tot

            @pl.when(k != 0)
            def _():
                o_ref[...] += tot

    args, in_specs = [], []
    for a, a_spec, b, b_spec in pairs:
        args += [a, b]
        in_specs += [a_spec, b_spec]
    sem = ("parallel",) * kax + ("arbitrary",)
    return _pcall(body, name=name, grid=grid, in_specs=in_specs, out_specs=out_spec, out_shape=out_shape,
                  compiler_params=_cp(sem))(*args)


def _mm_cols(name, a, w, out_dtype=F32):
    t, k = a.shape
    j, _, n = w.shape
    tm = _mm_tile(t)
    return _mm(name, [(a, pl.BlockSpec((tm, k), lambda i, jj, kk: (i, 0)),
                       w, pl.BlockSpec((None, k, n), lambda i, jj, kk: (jj, 0, 0)))],
               (t // tm, j, 1), _sds((t, j * n), out_dtype), pl.BlockSpec((tm, n), lambda i, jj, kk: (i, jj)), NN)


def _mm_cols_bwd(name, d, w):
    t = d.shape[0]
    j, k, n = w.shape
    tm = _mm_tile(t)
    return _mm(name, [(d, pl.BlockSpec((tm, n), lambda i, u, kk: (i, kk)),
                       w, pl.BlockSpec((None, k, n), lambda i, u, kk: (kk, 0, 0)))],
               (t // tm, 1, j), _sds((t, k)), pl.BlockSpec((tm, k), lambda i, u, kk: (i, 0)), NT)


def _mm_full(name, a, w, dn, tm=None):
    t, k = a.shape
    n = w.shape[1] if dn is NN else w.shape[0]
    tm = tm or _mm_tile(t)
    return _mm(name, [(a, pl.BlockSpec((tm, k), lambda i, u, kk: (i, 0)), w, _full(w.shape))],
               (t // tm, 1, 1), _sds((t, n)), pl.BlockSpec((tm, n), lambda i, u, kk: (i, 0)), dn)


def _wgrad_cols(name, a, b, j):
    t, k = a.shape
    n = b.shape[1] // j
    tt = _mm_tile(t)
    return _mm(name, [(a, pl.BlockSpec((tt, k), lambda jj, u, kk: (kk, 0)),
                       b, pl.BlockSpec((tt, n), lambda jj, u, kk: (kk, jj)))],
               (j, 1, t // tt), _sds((j, k, n)), pl.BlockSpec((None, k, n), lambda jj, u, kk: (jj, 0, 0)), TN)


def _wgrad_rows(name, a, blk, b):
    t, f = a.shape
    n = b.shape[1]
    tt = _mm_tile(t)
    return _mm(name, [(a, pl.BlockSpec((tt, blk), lambda jj, u, kk: (kk, jj)),
                       b, pl.BlockSpec((tt, n), lambda jj, u, kk: (kk, 0)))],
               (f // blk, 1, t // tt), _sds((f, n)), pl.BlockSpec((blk, n), lambda jj, u, kk: (jj, 0)), TN)


def _ffn_tiles(t, f):
    return _mm_tile(t) // 2, f // 2


def _ffn_up(name, h, wgt, wut):
    t, k = h.shape
    f = wgt.shape[0]
    tm, tn = _ffn_tiles(t, f)

    def body(h_ref, wg_ref, wu_ref, a_ref, gt_ref, up_ref):
        hv = h_ref[...]
        gt = _dot(hv, wg_ref[...], NT)
        up = _dot(hv, wu_ref[...], NT)
        a_ref[...] = (_silu(gt) * up).astype(BF16)
        gt_ref[...] = gt.astype(BF16)
        up_ref[...] = up.astype(BF16)

    wspec = pl.BlockSpec((tn, k), lambda i, jj: (jj, 0))
    ospec = pl.BlockSpec((tm, tn), lambda i, jj: (i, jj))
    o = _sds((t, f), BF16)
    return _pcall(body, name=name, grid=(t // tm, f // tn),
                  in_specs=[pl.BlockSpec((tm, k), lambda i, jj: (i, 0)), wspec, wspec],
                  out_specs=[ospec, ospec, ospec], out_shape=[o, o, o],
                  compiler_params=_cp(("parallel", "parallel")))(h, wgt, wut)


def _ffn_down_bwd(name, df, wd, gt, up, deps=()):
    t, dm = df.shape
    f = wd.shape[0]
    tm, tn = _ffn_tiles(t, f)
    nd = len(deps)

    def body(*refs):
        df_ref, wd_ref, gt_ref, up_ref = refs[:4]
        dgt_ref, dup_ref = refs[4 + nd:]
        da = _dot(df_ref[...], wd_ref[...], NT)
        g = gt_ref[...].astype(F32)
        u = up_ref[...].astype(F32)
        dgt_ref[...] = (da * u * _dsilu(g)).astype(BF16)
        dup_ref[...] = (da * _silu(g)).astype(BF16)

    aspec = pl.BlockSpec((tm, tn), lambda i, jj: (i, jj))
    o = _sds((t, f), BF16)
    return _pcall(body, name=name, grid=(t // tm, f // tn),
                  in_specs=[pl.BlockSpec((tm, dm), lambda i, jj: (i, 0)),
                            pl.BlockSpec((tn, dm), lambda i, jj: (jj, 0)), aspec, aspec]
                  + [pl.BlockSpec(d.shape, lambda i, jj: (0, 0)) for d in deps],
                  out_specs=[aspec, aspec], out_shape=[o, o],
                  compiler_params=_cp(("parallel", "parallel")))(df, wd, gt, up, *deps)


def _ffn_in_bwd(name, dgt, dup, wgt, wut):
    t, f = dgt.shape
    k = wgt.shape[1]
    tm, _ = _ffn_tiles(t, f)
    aspec = pl.BlockSpec((tm, f), lambda i, u, kk: (i, 0))
    wspec = pl.BlockSpec((f, k), lambda i, u, kk: (0, 0))
    return _mm(name, [(dgt, aspec, wgt, wspec), (dup, aspec, wut, wspec)], (t // tm, 1, 1), _sds((t, k)),
               pl.BlockSpec((tm, k), lambda i, u, kk: (i, 0)), NN)


def _modrow(ref, row, is_ctx):
    return jnp.where(is_ctx, ref[0, row:row + 1, :], ref[1, row:row + 1, :])


def _rnm(name, x, delta, mod_g, g_row, mod_n, sh_row, sc_row, nw, nct, deps=()):
    t, dm = x.shape
    tm = ROW_TILE
    has = delta is not None
    nd = len(deps)

    def body(*refs):
        refs = refs[:len(refs) - nd - (2 if has else 1)] + refs[len(refs) - (2 if has else 1):]
        if has:
            x_ref, d_ref, mg_ref, m_ref, nw_ref, xo_ref, h_ref = refs
        else:
            x_ref, m_ref, nw_ref, h_ref = refs
        is_ctx = pl.program_id(0) < nct
        xv = x_ref[...]
        if has:
            xv = xv + _modrow(mg_ref, g_row, is_ctx) * d_ref[...]
            xo_ref[...] = xv
        r = lax.rsqrt(jnp.mean(xv * xv, axis=-1, keepdims=True) + EPS)
        hv = (xv * r * nw_ref[...]) * (1.0 + _modrow(m_ref, sc_row, is_ctx)) + _modrow(m_ref, sh_row, is_ctx)
        h_ref[...] = hv.astype(BF16)

    row = pl.BlockSpec((tm, dm), lambda i: (i, 0))
    ins = [x] + ([delta, mod_g] if has else []) + [mod_n, nw] + list(deps)
    in_specs = ([row] + ([row, _full(mod_g.shape)] if has else []) + [_full(mod_n.shape), _full(nw.shape)]
                + [_full(d.shape) for d in deps])
    outs = ([_sds((t, dm))] if has else []) + [_sds((t, dm), BF16)]
    out_specs = ([row] if has else []) + [row]
    res = _pcall(body, name=name, grid=(t // tm,), in_specs=in_specs, out_specs=out_specs, out_shape=outs,
                 compiler_params=_cp(("parallel",)))(*ins)
    return res if has else (None, res[0])


def _bnm(name, xn, dh, dup, yprev, mod_n, sh_row, sc_row, mod_g, g_row, nw, nct, deps=()):
    t, dm = xn.shape
    tm = ROW_TILE
    has = yprev is not None
    nd = len(deps)

    def body(*refs):
        nout = 3 if has else 2
        refs = refs[:len(refs) - nd - nout] + refs[len(refs) - nout:]
        if has:
            x_ref, dh_ref, du_ref, y_ref, mn_ref, mg_ref, nw_ref, dx_ref, dd_ref, s_ref = refs
        else:
            x_ref, dh_ref, du_ref, mn_ref, nw_ref, dx_ref, s_ref = refs
        i = pl.program_id(0)
        is_ctx = i < nct

        @pl.when(i == 0)
        def _():
            s_ref[...] = jnp.zeros_like(s_ref)

        xv = x_ref[...]
        r = lax.rsqrt(jnp.mean(xv * xv, axis=-1, keepdims=True) + EPS)
        xh = xv * r
        w = nw_ref[...]
        sc1 = 1.0 + _modrow(mn_ref, sc_row, is_ctx)
        dhv = dh_ref[...]
        dxh = dhv * sc1 * w
        dx = r * (dxh - xh * jnp.mean(dxh * xh, axis=-1, keepdims=True)) + du_ref[...]
        dx_ref[...] = dx
        parts = [_colsum(dhv), _colsum(dhv * (xh * w))]
        if has:
            dd_ref[...] = (_modrow(mg_ref, g_row, is_ctx) * dx).astype(BF16)
            parts.append(_colsum(dx * y_ref[...]))
        else:
            parts.append(jnp.zeros((1, dm), F32))
        upd = jnp.concatenate(parts + [jnp.zeros((5, dm), F32)], axis=0)
        dnw = jnp.concatenate([jnp.zeros((3, dm), F32), _colsum(dhv * sc1 * xh), jnp.zeros((4, dm), F32)], axis=0)

        @pl.when(is_ctx)
        def _():
            s_ref[0] += upd
            s_ref[1] += dnw

        @pl.when(jnp.logical_not(is_ctx))
        def _():
            s_ref[1] += upd + dnw

    row = pl.BlockSpec((tm, dm), lambda i: (i, 0))
    ins = [xn, dh, dup] + ([yprev] if has else []) + [mod_n] + ([mod_g] if has else []) + [nw] + list(deps)
    in_specs = ([row, row, row] + ([row] if has else []) + [_full(mod_n.shape)]
                + ([_full(mod_g.shape)] if has else []) + [_full(nw.shape)] + [_full(d.shape) for d in deps])
    outs = [_sds((t, dm))] + ([_sds((t, dm), BF16)] if has else []) + [_sds((2, 8, dm))]
    out_specs = [row] + ([row] if has else []) + [_full((2, 8, dm))]
    res = _pcall(body, name=name, grid=(t // tm,), in_specs=in_specs, out_specs=out_specs, out_shape=outs,
                 compiler_params=_cp(("arbitrary",)))(*ins)
    return res if has else (res[0], None, res[1])


def _fin(name, x1, f, mod, g_row, fw, target, nct):
    t, dm = x1.shape
    tm = ROW_TILE

    def body(x_ref, f_ref, m_ref, fw_ref, t_ref, loss_ref, dx_ref, dd_ref, s_ref):
        i = pl.program_id(0)

        @pl.when(i == 0)
        def _():
            s_ref[...] = jnp.zeros_like(s_ref)
            loss_ref[...] = jnp.zeros_like(loss_ref)

        @pl.when(i < nct)
        def _():
            dx_ref[...] = jnp.zeros_like(dx_ref)
            dd_ref[...] = jnp.zeros_like(dd_ref)

        @pl.when(i >= nct)
        def _():
            g = m_ref[1, g_row:g_row + 1, :]
            fv = f_ref[...]
            xv = x_ref[...] + g * fv
            r = lax.rsqrt(jnp.mean(xv * xv, axis=-1, keepdims=True) + EPS)
            xh = xv * r
            w = fw_ref[...]
            err = xh * w - t_ref[...]
            loss_ref[...] += 0.5 * jnp.sum(err * err) / dm
            dout = err * (1.0 / dm)
            dxh = dout * w
            dx = r * (dxh - xh * jnp.mean(dxh * xh, axis=-1, keepdims=True))
            dx_ref[...] = dx
            dd_ref[...] = (g * dx).astype(BF16)
            s_ref[...] += jnp.concatenate([_colsum(dx * fv), _colsum(dout * xh), jnp.zeros((6, dm), F32)], axis=0)

    row = pl.BlockSpec((tm, dm), lambda i: (i, 0))
    trow = pl.BlockSpec((tm, dm), lambda i: (jnp.maximum(i - nct, 0), 0))
    return _pcall(body, name=name, grid=(t // tm,),
                  in_specs=[row, row, _full(mod.shape), _full(fw.shape), trow],
                  out_specs=[_full((8, 128)), row, row, _full((8, dm))],
                  out_shape=[_sds((8, 128)), _sds((t, dm)), _sds((t, dm), BF16), _sds((8, dm))],
                  compiler_params=_cp(("arbitrary",)))(x1, f, mod, fw, target)


def _rope_tables(t, lc):
    l = t - lc
    rows = l // GRID_W
    grid_r = jnp.broadcast_to(jnp.arange(rows, dtype=F32)[:, None], (rows, GRID_W)).reshape(-1)
    grid_c = jnp.broadcast_to(jnp.arange(GRID_W, dtype=F32)[None, :], (rows, GRID_W)).reshape(-1)

    def angles(p_seq, p_row, p_col):
        parts = []
        for p, n in zip((p_seq, p_row, p_col), ROPE_PAIRS):
            freq = ROPE_BASE ** (-jnp.arange(n, dtype=F32) / n)
            parts.append(p[:, None] * freq[None, :])
        return jnp.concatenate(parts, axis=-1)

    zc = jnp.zeros((lc,), F32)
    ang = jnp.concatenate([angles(jnp.arange(lc, dtype=F32), zc, zc),
                           angles(jnp.full((l,), lc, F32), grid_r, grid_c)], axis=0)
    cos, sin = jnp.cos(ang), jnp.sin(ang)
    return jnp.concatenate([cos, cos], axis=-1), jnp.concatenate([-sin, sin], axis=-1)


def _rope(u, cs, sn):
    return u * cs + pltpu.roll(u, HEAD_DIM // 2, 1) * sn


def _rope_t(d, cs, sn):
    return d * cs + pltpu.roll(d * sn, HEAD_DIM // 2, 1)


def _even_qkv(name, p, cs, sn):
    t = p.shape[0]
    tm = ROW_TILE
    w = HEADS * HEAD_DIM
    scale = HEAD_DIM ** -0.5

    def body(q_ref, k_ref, v_ref, cs_ref, sn_ref, qo_ref, ko_ref, vo_ref):
        c, s = cs_ref[...], sn_ref[...]
        for h in range(HEADS):
            sl = slice(h * HEAD_DIM, (h + 1) * HEAD_DIM)
            qo_ref[:, sl] = (_rope(q_ref[:, sl], c, s) * scale).astype(BF16)
            ko_ref[:, sl] = _rope(k_ref[:, sl], c, s).astype(BF16)
        vo_ref[...] = v_ref[...].astype(BF16)

    col = lambda j: pl.BlockSpec((tm, w), lambda i: (i, j))
    tab = pl.BlockSpec((tm, HEAD_DIM), lambda i: (i, 0))
    o = _sds((t, w), BF16)
    return _pcall(body, name=name, grid=(t // tm,), in_specs=[col(0), col(1), col(2), tab, tab],
                  out_specs=[col(0)] * 3, out_shape=[o, o, o], compiler_params=_cp(("parallel",)))(p, p, p, cs, sn)


def _log_sigmoid_row(x):
    e = jnp.exp(-jnp.abs(x))
    l1p = jnp.where(e < 0.01, e * (1.0 - e * (0.5 - e * (1.0 / 3.0))), jnp.log(1.0 + e))
    return jnp.minimum(x, 0.0) - l1p


def _ret_tables(lgb_ref, dm_ref, xi_ref, zt_ref):
    ri = lax.broadcasted_iota(jnp.int32, (CHUNK, CHUNK), 0).astype(F32)
    ci = lax.broadcasted_iota(jnp.int32, (CHUNK, CHUNK), 1).astype(F32)
    for d in range(2):
        for h in range(HEADS):
            idx = d * HEADS + h
            lg = _log_sigmoid_row(lgb_ref[idx:idx + 1, :])
            if d == 0:
                e, mask = ri - ci, ri >= ci
                xe, ze = ri + 1.0, (CHUNK - 1.0) - ri
            else:
                e, mask = ci - ri - 1.0, ci > ri
                xe, ze = (CHUNK - 1.0) - ri, ri
            dm_ref[idx] = jnp.where(mask, jnp.exp(lg * jnp.where(mask, e, 0.0)), 0.0)
            xi_ref[idx] = jnp.exp(lg * xe)
            zt_ref[idx] = jnp.exp(lg * ze)


def _ret_exponents(d):
    ri = lax.broadcasted_iota(jnp.int32, (CHUNK, CHUNK), 0).astype(F32)
    ci = lax.broadcasted_iota(jnp.int32, (CHUNK, CHUNK), 1).astype(F32)
    if d == 0:
        return ri - ci, ri + 1.0, (CHUNK - 1.0) - ri
    return ci - ri - 1.0, (CHUNK - 1.0) - ri, ri


def _bwd_chunk(n, ncc, nc):
    return jnp.where(n < ncc, ncc - 1 - n, nc - 1 - (n - ncc))


def _retention_fwd(name, q, k, v, lgb, lc):
    t, w = q.shape
    nc, ncc = t // CHUNK, lc // CHUNK
    nh = 2 * HEADS

    def body(qf_ref, kf_ref, vf_ref, qb_ref, kb_ref, vb_ref, lgb_ref, of_ref, ob_ref, ss_ref,
             s_ref, dm_ref, xi_ref, zt_ref):
        n = pl.program_id(0)

        @pl.when(n == 0)
        def _():
            s_ref[...] = jnp.zeros_like(s_ref)
            _ret_tables(lgb_ref, dm_ref, xi_ref, zt_ref)

        for d in range(2):
            q_ref, k_ref, v_ref, o_ref = (qf_ref, kf_ref, vf_ref, of_ref) if d == 0 else (qb_ref, kb_ref, vb_ref, ob_ref)
            for h in range(HEADS):
                idx = d * HEADS + h
                sl = slice(h * HEAD_DIM, (h + 1) * HEAD_DIM)
                qv, kv, vv = q_ref[:, sl], k_ref[:, sl], v_ref[:, sl]
                s = s_ref[idx]
                ss_ref[idx] = s
                a = _dot(qv, kv, NT) * dm_ref[idx]
                o = _dot(a, vv, NN) + _dot(qv.astype(F32) * xi_ref[idx], s, NN)
                o_ref[:, sl] = o
                gc = jnp.exp(_log_sigmoid_row(lgb_ref[idx:idx + 1, :]) * float(CHUNK))
                s_ref[idx] = gc * s + _dot(kv.astype(F32) * zt_ref[idx], vv, TN)

    fspec = pl.BlockSpec((CHUNK, w), lambda n: (n, 0))
    bspec = pl.BlockSpec((CHUNK, w), lambda n: (_bwd_chunk(n, ncc, nc), 0))
    tab = pltpu.VMEM((nh, CHUNK, CHUNK), F32)
    return _pcall(body, name=name, grid=(nc,),
                  in_specs=[fspec] * 3 + [bspec] * 3 + [_full((nh, HEAD_DIM))],
                  out_specs=[fspec, bspec, pl.BlockSpec((None, nh, CHUNK, CHUNK), lambda n: (n, 0, 0, 0))],
                  out_shape=[_sds((t, w)), _sds((t, w)), _sds((nc, nh, CHUNK, CHUNK))],
                  scratch_shapes=[tab, tab, tab, tab],
                  compiler_params=_cp(("arbitrary",)))(q, k, v, q, k, v, lgb)


def _retention_bwd(name, q, k, v, do, ss, lgb, lc):
    t, w = q.shape
    nc, ncc = t // CHUNK, lc // CHUNK
    nh = 2 * HEADS

    def body(qf_ref, kf_ref, vf_ref, gf_ref, qb_ref, kb_ref, vb_ref, gb_ref, ss_ref, lgb_ref,
             dqf_ref, dkf_ref, dvf_ref, dqb_ref, dkb_ref, dvb_ref, dl_ref,
             ds_ref, dm_ref, xi_ref, zt_ref, acc_ref):
        n = pl.program_id(0)

        @pl.when(n == 0)
        def _():
            ds_ref[...] = jnp.zeros_like(ds_ref)
            acc_ref[...] = jnp.zeros_like(acc_ref)
            _ret_tables(lgb_ref, dm_ref, xi_ref, zt_ref)

        for d in range(2):
            if d == 0:
                q_ref, k_ref, v_ref, g_ref, dq_ref, dk_ref, dv_ref = qf_ref, kf_ref, vf_ref, gf_ref, dqf_ref, dkf_ref, dvf_ref
            else:
                q_ref, k_ref, v_ref, g_ref, dq_ref, dk_ref, dv_ref = qb_ref, kb_ref, vb_ref, gb_ref, dqb_ref, dkb_ref, dvb_ref
            ee, xe, ze = _ret_exponents(d)
            for h in range(HEADS):
                idx = d * HEADS + h
                sl = slice(h * HEAD_DIM, (h + 1) * HEAD_DIM)
                qv, kv, vv, gv = q_ref[:, sl], k_ref[:, sl], v_ref[:, sl], g_ref[:, sl]
                s = ss_ref[idx]
                dsp = ds_ref[idx]
                dmat, xi, zt = dm_ref[idx], xi_ref[idx], zt_ref[idx]
                qf32, kf32 = qv.astype(F32), kv.astype(F32)
                a = _dot(qv, kv, NT) * dmat
                dar = _dot(gv, vv, NT)
                da = dar * dmat
                t1 = _dot(gv, s, NT)
                t2 = _dot(vv, dsp, NT)
                dq_ref[:, sl] = _dot(da, kv, NN) + xi * t1
                dk_ref[:, sl] = _dot(da, qv, TN) + zt * t2
                dv_ref[:, sl] = _dot(a, gv, TN) + _dot(kf32 * zt, dsp, NN)
                gc = jnp.exp(_log_sigmoid_row(lgb_ref[idx:idx + 1, :]) * float(CHUNK))
                ds_ref[idx] = gc * dsp + _dot(qf32 * xi, gv, TN)
                acc_ref[idx] += (ee * a * dar + xe * xi * qf32 * t1 + ze * zt * kf32 * t2
                                 + (float(CHUNK) * gc) * dsp * s)

        @pl.when(n == nc - 1)
        def _():
            for idx in range(nh):
                tot = jnp.sum(acc_ref[idx])
                dl_ref[idx:idx + 1, :] = tot * _sigmoid(-lgb_ref[idx:idx + 1, :])

    fmap = lambda n: (nc - 1 - n, 0)
    bmap = lambda n: (_bwd_chunk(nc - 1 - n, ncc, nc), 0)
    fspec = pl.BlockSpec((CHUNK, w), fmap)
    bspec = pl.BlockSpec((CHUNK, w), bmap)
    tab = pltpu.VMEM((nh, CHUNK, CHUNK), F32)
    o = _sds((t, w))
    return _pcall(body, name=name, grid=(nc,),
                  in_specs=[fspec] * 4 + [bspec] * 4
                  + [pl.BlockSpec((None, nh, CHUNK, CHUNK), lambda n: (nc - 1 - n, 0, 0, 0)), _full((nh, HEAD_DIM))],
                  out_specs=[fspec] * 3 + [bspec] * 3 + [_full((nh, HEAD_DIM))],
                  out_shape=[o] * 6 + [_sds((nh, HEAD_DIM))],
                  scratch_shapes=[tab, tab, tab, tab, tab],
                  compiler_params=_cp(("arbitrary",)))(q, k, v, do, q, k, v, do, ss, lgb)


def _halo_specs(tm, halo, t, width, col):
    hb = tm // halo
    last = t // halo - 1
    prev = pl.BlockSpec((halo, width), lambda i: (jnp.maximum(i * hb - 1, 0), col))
    nxt = pl.BlockSpec((halo, width), lambda i: (jnp.minimum((i + 1) * hb, last), col))
    return prev, nxt


def _halo_valid(i, nct, nt):
    vp = jnp.logical_and(i != 0, i != nct)
    vn = jnp.logical_and(i != nct - 1, i != nt - 1)
    return vp, vn


def _fill_window(win_ref, prev, cur, nxt, vp, vn, halo, tm):
    win_ref[0:halo, :] = jnp.where(vp, prev, 0.0)
    win_ref[halo:halo + tm, :] = cur
    win_ref[halo + tm:halo + tm + halo, :] = jnp.where(vn, nxt, 0.0)


CONV_SUB = 64


def _conv_taps(win_ref, w_ref, tm, flip):
    outs = []
    for r0 in range(0, tm, CONV_SUB):
        acc = None
        for kk in range(CONV_K):
            wk = (CONV_K - 1 - kk) if flip else kk
            term = w_ref[wk:wk + 1, :] * win_ref[r0 + kk + 1:r0 + kk + 1 + CONV_SUB, :]
            acc = term if acc is None else acc + term
        outs.append(acc)
    return jnp.concatenate(outs, axis=0)


def _head_norm(y):
    r = lax.rsqrt(jnp.mean(y * y, axis=-1, keepdims=True) + EPS)
    return y * r, r


def _ln_stats(y):
    mu = jnp.mean(y, axis=-1, keepdims=True)
    yc = y - mu
    rs = lax.rsqrt(jnp.mean(yc * yc, axis=-1, keepdims=True) + EPS)
    return yc * rs, rs


def _ln_bwd(dyh, yh, rs):
    return rs * (dyh - jnp.mean(dyh, axis=-1, keepdims=True) - yh * jnp.mean(dyh * yh, axis=-1, keepdims=True))


def _even_mix(name, p, of, ob, cw, lnw, lnb, nct):
    t = p.shape[0]
    tm, halo = ROW_TILE, CONV_HALO
    nt = t // tm
    w = HEADS * HEAD_DIM

    def body(g_ref, a_ref, gb_ref, ap_ref, gbp_ref, an_ref, gbn_ref, of_ref, ob_ref, cw_ref, lw_ref, lb_ref,
             mix_ref, yc_ref, win_ref):
        i = pl.program_id(0)
        vp, vn = _halo_valid(i, nct, nt)
        glu = lambda a, b: a * _sigmoid(b)
        _fill_window(win_ref, glu(ap_ref[...], gbp_ref[...]), glu(a_ref[...], gb_ref[...]),
                     glu(an_ref[...], gbn_ref[...]), vp, vn, halo, tm)
        yc = _conv_taps(win_ref, cw_ref, tm, False)
        yc_ref[...] = yc
        yh, _ = _ln_stats(yc)
        mix_ref[:, w:2 * w] = _silu(yh * lw_ref[...] + lb_ref[...]).astype(BF16)
        for h in range(HEADS):
            sl = slice(h * HEAD_DIM, (h + 1) * HEAD_DIM)
            yn, _ = _head_norm(of_ref[:, sl] + ob_ref[:, sl])
            mix_ref[:, sl] = (_silu(g_ref[:, sl]) * yn).astype(BF16)

    col = lambda j: pl.BlockSpec((tm, w), lambda i: (i, j))
    ap, an = _halo_specs(tm, halo, t, w, 4)
    gp, gn = _halo_specs(tm, halo, t, w, 5)
    row = pl.BlockSpec((tm, w), lambda i: (i, 0))
    return _pcall(body, name=name, grid=(nt,),
                  in_specs=[col(3), col(4), col(5), ap, gp, an, gn, row, row,
                            _full(cw.shape), _full(lnw.shape), _full(lnb.shape)],
                  out_specs=[pl.BlockSpec((tm, 2 * w), lambda i: (i, 0)), row],
                  out_shape=[_sds((t, 2 * w), BF16), _sds((t, w))],
                  scratch_shapes=[pltpu.VMEM((tm + 2 * halo, w), F32)],
                  compiler_params=_cp(("parallel",)))(p, p, p, p, p, p, p, of, ob, cw, lnw, lnb)


def _even_mix_bwd1(name, dmix, p, of, ob, yc, lnw, lnb):
    t = p.shape[0]
    tm = ROW_TILE
    w = HEADS * HEAD_DIM

    def body(dr_ref, dc_ref, g_ref, of_ref, ob_ref, yc_ref, lw_ref, lb_ref, do_ref, dg_ref, dyc_ref, s_ref):
        @pl.when(pl.program_id(0) == 0)
        def _():
            s_ref[...] = jnp.zeros_like(s_ref)

        for h in range(HEADS):
            sl = slice(h * HEAD_DIM, (h + 1) * HEAD_DIM)
            yn, r = _head_norm(of_ref[:, sl] + ob_ref[:, sl])
            gv = g_ref[:, sl]
            dr = dr_ref[:, sl]
            dg_ref[:, sl] = (dr * yn * _dsilu(gv)).astype(BF16)
            dyn = dr * _silu(gv)
            do_ref[:, sl] = (r * (dyn - yn * jnp.mean(dyn * yn, axis=-1, keepdims=True))).astype(BF16)
        yh, rs = _ln_stats(yc_ref[...])
        lw = lw_ref[...]
        dlo = dc_ref[...] * _dsilu(yh * lw + lb_ref[...])
        dyc_ref[...] = _ln_bwd(dlo * lw, yh, rs)
        s_ref[...] += jnp.concatenate([_colsum(dlo * yh), _colsum(dlo), jnp.zeros((6, w), F32)], axis=0)

    col = lambda j: pl.BlockSpec((tm, w), lambda i: (i, j))
    row = pl.BlockSpec((tm, w), lambda i: (i, 0))
    return _pcall(body, name=name, grid=(t // tm,),
                  in_specs=[col(0), col(1), col(3), row, row, row, _full(lnw.shape), _full(lnb.shape)],
                  out_specs=[row, row, row, _full((8, w))],
                  out_shape=[_sds((t, w), BF16), _sds((t, w), BF16), _sds((t, w)), _sds((8, w))],
                  compiler_params=_cp(("arbitrary",)))(dmix, dmix, p, of, ob, yc, lnw, lnb)


def _even_conv_bwd(name, dyc, p, cw, nct):
    t = p.shape[0]
    tm, halo = ROW_TILE, CONV_HALO
    nt = t // tm
    w = HEADS * HEAD_DIM

    def body(d_ref, dp_ref, dn_ref, a_ref, gb_ref, ap_ref, gbp_ref, an_ref, gbn_ref, cw_ref,
             da_ref, dgb_ref, dw_ref, dwin_ref, uwin_ref):
        i = pl.program_id(0)

        @pl.when(i == 0)
        def _():
            dw_ref[...] = jnp.zeros_like(dw_ref)

        vp, vn = _halo_valid(i, nct, nt)
        glu = lambda a, b: a * _sigmoid(b)
        dcur = d_ref[...]
        _fill_window(dwin_ref, dp_ref[...], dcur, dn_ref[...], vp, vn, halo, tm)
        _fill_window(uwin_ref, glu(ap_ref[...], gbp_ref[...]), glu(a_ref[...], gb_ref[...]),
                     glu(an_ref[...], gbn_ref[...]), vp, vn, halo, tm)
        du = _conv_taps(dwin_ref, cw_ref, tm, True)
        av = a_ref[...]
        sg = _sigmoid(gb_ref[...])
        da_ref[...] = (du * sg).astype(BF16)
        dgb_ref[...] = (du * av * sg * (1.0 - sg)).astype(BF16)
        rows = [_colsum(dcur * uwin_ref[kk + 1:kk + 1 + tm, :]) for kk in range(CONV_K)]
        dw_ref[...] += jnp.concatenate(rows + [jnp.zeros((1, w), F32)], axis=0)

    col = lambda j: pl.BlockSpec((tm, w), lambda i: (i, j))
    row = pl.BlockSpec((tm, w), lambda i: (i, 0))
    dp, dn = _halo_specs(tm, halo, t, w, 0)
    ap, an = _halo_specs(tm, halo, t, w, 4)
    gp, gn = _halo_specs(tm, halo, t, w, 5)
    win = pltpu.VMEM((tm + 2 * halo, w), F32)
    return _pcall(body, name=name, grid=(nt,),
                  in_specs=[row, dp, dn, col(4), col(5), ap, gp, an, gn, _full(cw.shape)],
                  out_specs=[row, row, _full((CONV_K + 1, w))],
                  out_shape=[_sds((t, w), BF16), _sds((t, w), BF16), _sds((CONV_K + 1, w))],
                  scratch_shapes=[win, win],
                  compiler_params=_cp(("arbitrary",)))(dyc, dyc, dyc, p, p, p, p, p, p, cw)


def _even_dp(name, dqs, dks, dvs, dg, da, dgb, cs, sn):
    t, w = dg.shape
    tm = ROW_TILE
    scale = HEAD_DIM ** -0.5

    def body(dqf_ref, dqb_ref, dkf_ref, dkb_ref, dvf_ref, dvb_ref, dg_ref, da_ref, dgb_ref, cs_ref, sn_ref, dp_ref):
        c, s = cs_ref[...], sn_ref[...]
        for h in range(HEADS):
            sl = slice(h * HEAD_DIM, (h + 1) * HEAD_DIM)
            dp_ref[:, sl] = (_rope_t(dqf_ref[:, sl] + dqb_ref[:, sl], c, s) * scale).astype(BF16)
            dp_ref[:, w + h * HEAD_DIM:w + (h + 1) * HEAD_DIM] = _rope_t(dkf_ref[:, sl] + dkb_ref[:, sl], c, s).astype(BF16)
        dp_ref[:, 2 * w:3 * w] = (dvf_ref[...] + dvb_ref[...]).astype(BF16)
        dp_ref[:, 3 * w:4 * w] = dg_ref[...]
        dp_ref[:, 4 * w:5 * w] = da_ref[...]
        dp_ref[:, 5 * w:6 * w] = dgb_ref[...]

    row = pl.BlockSpec((tm, w), lambda i: (i, 0))
    tab = pl.BlockSpec((tm, HEAD_DIM), lambda i: (i, 0))
    return _pcall(body, name=name, grid=(t // tm,), in_specs=[row] * 9 + [tab, tab],
                  out_specs=pl.BlockSpec((tm, 6 * w), lambda i: (i, 0)), out_shape=_sds((t, 6 * w), BF16),
                  compiler_params=_cp(("parallel",)))(dqs[0], dqs[1], dks[0], dks[1], dvs[0], dvs[1], dg, da, dgb, cs, sn)


GROUPS = 4
GC = 128
INV_SQRT2 = 0.7071067811865476
INV_SQRT_2PI = 0.3989422804014327


def _gelu(x):
    return 0.5 * x * (1.0 + lax.erf(x * INV_SQRT2))


def _dgelu(x):
    return 0.5 * (1.0 + lax.erf(x * INV_SQRT2)) + x * jnp.exp(-0.5 * x * x) * INV_SQRT_2PI


def _pool_count(i, nct, lc, t, tm, rows, row0, left, right):
    is_ctx = i < nct
    seg_start = jnp.where(is_ctx, 0, lc)
    seg_len = jnp.where(is_ctx, lc, t - lc)
    pos = i * tm + row0 - seg_start + lax.broadcasted_iota(jnp.int32, (rows, GC), 0)
    cnt = jnp.minimum(pos + right, seg_len - 1) - jnp.maximum(pos - left, 0) + 1
    return jnp.maximum(cnt, 1).astype(F32)


def _spatial_gate(vln, sgw_ref, sgb_ref, tm):
    cols = []
    for g in range(GROUPS):
        sl = slice(g * GC, (g + 1) * GC)
        parts = [_dot(sgw_ref[g], vln[r0:r0 + CHUNK, sl], NN) + sgb_ref[g] for r0 in range(0, tm, CHUNK)]
        cols.append(jnp.concatenate(parts, axis=0))
    return jnp.concatenate(cols, axis=1)


def _odd_mix(name, p, pw, pscale, lnw, lnb, sgw, sgb, nct, lc):
    t = p.shape[0]
    tm, halo = ROW_TILE, POOL_HALO
    nt = t // tm
    w = GROUPS * GC

    def body(pc_ref, pp_ref, pn_ref, pu_ref, pv_ref, pw_ref, ps_ref, lw_ref, lb_ref, sgw_ref, sgb_ref,
             mix_ref, m_ref, win_ref):
        i = pl.program_id(0)
        vp, vn = _halo_valid(i, nct, nt)
        pc = pc_ref[...]
        _fill_window(win_ref, pp_ref[...], pc, pn_ref[...], vp, vn, halo, tm)
        for g, wd in enumerate(POOL_WINDOWS):
            sl = slice(g * GC, (g + 1) * GC)
            left = wd // 2
            right = wd - 1 - left
            s = None
            for o in range(-left, right + 1):
                term = win_ref[halo + o:halo + o + tm, sl]
                s = term if s is None else s + term
            mg = s / _pool_count(i, nct, lc, t, tm, tm, 0, left, right) - pc[:, sl]
            m_ref[:, sl] = mg
            mix_ref[:, sl] = (_dot(mg, pw_ref[g], NN) * ps_ref[:, sl]).astype(BF16)
        u = _gelu(pu_ref[...])
        vh, _ = _ln_stats(_gelu(pv_ref[...]))
        s = _spatial_gate(vh * lw_ref[...] + lb_ref[...], sgw_ref, sgb_ref, tm)
        mix_ref[:, w:2 * w] = (u * s).astype(BF16)

    col = lambda j: pl.BlockSpec((tm, w), lambda i: (i, j))
    pp, pn = _halo_specs(tm, halo, t, w, 0)
    return _pcall(body, name=name, grid=(nt,),
                  in_specs=[col(0), pp, pn, col(1), col(2), _full(pw.shape), _full(pscale.shape),
                            _full(lnw.shape), _full(lnb.shape), _full(sgw.shape), _full(sgb.shape)],
                  out_specs=[pl.BlockSpec((tm, 2 * w), lambda i: (i, 0)), col(0)],
                  out_shape=[_sds((t, 2 * w), BF16), _sds((t, w))],
                  scratch_shapes=[pltpu.VMEM((tm + 2 * halo, w), F32)],
                  compiler_params=_cp(("parallel",)))(p, p, p, p, p, pw, pscale, lnw, lnb, sgw, sgb)


def _odd_mix_bwd1(name, dmix, p, m, pw, pscale, lnw, lnb, sgw, sgb):
    t = p.shape[0]
    tm = ROW_TILE
    w = GROUPS * GC

    def body(dpo_ref, dsg_ref, pu_ref, pv_ref, m_ref, pw_ref, ps_ref, lw_ref, lb_ref, sgw_ref, sgb_ref,
             dm_ref, dpd_ref, vec_ref, dpw_ref, dsgw_ref, dsgb_ref):
        @pl.when(pl.program_id(0) == 0)
        def _():
            vec_ref[...] = jnp.zeros_like(vec_ref)
            dpw_ref[...] = jnp.zeros_like(dpw_ref)
            dsgw_ref[...] = jnp.zeros_like(dsgw_ref)
            dsgb_ref[...] = jnp.zeros_like(dsgb_ref)

        dscale = []
        for g in range(GROUPS):
            sl = slice(g * GC, (g + 1) * GC)
            mg = m_ref[:, sl]
            dpo = dpo_ref[:, sl]
            dscale.append(_colsum(dpo * _dot(mg, pw_ref[g], NN)))
            dpo = dpo * ps_ref[:, sl]
            dm_ref[:, sl] = _dot(dpo, pw_ref[g], NT)
            dpw_ref[g] += _dot(mg, dpo, TN)
        pu, pv = pu_ref[...], pv_ref[...]
        u = _gelu(pu)
        vh, rs = _ln_stats(_gelu(pv))
        lw = lw_ref[...]
        vln = vh * lw + lb_ref[...]
        s = _spatial_gate(vln, sgw_ref, sgb_ref, tm)
        dsg = dsg_ref[...]
        dpd_ref[:, 0:w] = (dsg * s * _dgelu(pu)).astype(BF16)
        ds = dsg * u
        cols = []
        for g in range(GROUPS):
            sl = slice(g * GC, (g + 1) * GC)
            parts = []
            for r0 in range(0, tm, CHUNK):
                dsc = ds[r0:r0 + CHUNK, sl]
                parts.append(_dot(sgw_ref[g], dsc, TN))
                dsgw_ref[g] += _dot(dsc, vln[r0:r0 + CHUNK, sl], NT)
                dsgb_ref[g] += dsc
            cols.append(jnp.concatenate(parts, axis=0))
        dvln = jnp.concatenate(cols, axis=1)
        dpd_ref[:, w:2 * w] = (_ln_bwd(dvln * lw, vh, rs) * _dgelu(pv)).astype(BF16)
        vec_ref[...] += jnp.concatenate([jnp.concatenate(dscale, axis=1), _colsum(dvln * vh), _colsum(dvln),
                                         jnp.zeros((5, w), F32)], axis=0)

        @pl.when(pl.program_id(0) == t // tm - 1)
        def _():
            for g in range(GROUPS):
                dsgb_ref[g] = jnp.broadcast_to(jnp.sum(dsgb_ref[g], axis=1, keepdims=True), (GC, GC))

    col = lambda j: pl.BlockSpec((tm, w), lambda i: (i, j))
    mat = _full((GROUPS, GC, GC))
    return _pcall(body, name=name, grid=(t // tm,),
                  in_specs=[col(0), col(1), col(1), col(2), col(0), _full(pw.shape), _full(pscale.shape),
                            _full(lnw.shape), _full(lnb.shape), _full(sgw.shape), _full(sgb.shape)],
                  out_specs=[col(0), pl.BlockSpec((tm, 2 * w), lambda i: (i, 0)), _full((8, w)), mat, mat, mat],
                  out_shape=[_sds((t, w)), _sds((t, 2 * w), BF16), _sds((8, w)),
                             _sds((GROUPS, GC, GC)), _sds((GROUPS, GC, GC)), _sds((GROUPS, GC, GC))],
                  compiler_params=_cp(("arbitrary",)))(dmix, dmix, p, p, m, pw, pscale, lnw, lnb, sgw, sgb)


def _odd_dp(name, dm, dpd, nct, lc):
    t, w = dm.shape
    tm, halo = ROW_TILE, POOL_HALO
    nt = t // tm

    def body(d_ref, dp_ref, dn_ref, dpd_ref, o_ref, win_ref):
        i = pl.program_id(0)
        vp, vn = _halo_valid(i, nct, nt)
        dcur = d_ref[...]
        _fill_window(win_ref, dp_ref[...], dcur, dn_ref[...], vp, vn, halo, tm)
        for g, wd in enumerate(POOL_WINDOWS):
            sl = slice(g * GC, (g + 1) * GC)
            left = wd // 2
            right = wd - 1 - left
            win_ref[:, sl] = win_ref[:, sl] / _pool_count(i, nct, lc, t, tm, tm + 2 * halo, -halo, left, right)
            s = None
            for o in range(-right, left + 1):
                term = win_ref[halo + o:halo + o + tm, sl]
                s = term if s is None else s + term
            o_ref[:, sl] = (s - dcur[:, sl]).astype(BF16)
        o_ref[:, w:3 * w] = dpd_ref[...]

    row = pl.BlockSpec((tm, w), lambda i: (i, 0))
    pp, pn = _halo_specs(tm, halo, t, w, 0)
    return _pcall(body, name=name, grid=(nt,),
                  in_specs=[row, pp, pn, pl.BlockSpec((tm, 2 * w), lambda i: (i, 0))],
                  out_specs=pl.BlockSpec((tm, 3 * w), lambda i: (i, 0)), out_shape=_sds((t, 3 * w), BF16),
                  scratch_shapes=[pltpu.VMEM((tm + 2 * halo, w), F32)],
                  compiler_params=_cp(("parallel",)))(dm, dm, dm, dpd)


def _place():
    x, y, c = lax.axis_index("x"), lax.axis_index("y"), lax.axis_index("c")
    chips = [(1 - x, y), (x, 1 - y), (1 - x, 1 - y)]
    return x, y, c, chips


def _chip_index(cx, cy):
    return 2 * cx + cy


def _all_gather8(name, blk):
    m_per, n = blk.shape

    def body(x_ref, out_ref, send_sems, recv_sems, local_sem):
        x, y, c, chips = _place()
        me, sibling = (x, y, c), (x, y, 1 - c)

        def rows(px, py, pc):
            return out_ref.at[pl.ds((4 * px + 2 * py + pc) * m_per, m_per), :]

        def copy(k, block, to, src=None):
            return pltpu.make_async_remote_copy(
                src_ref=rows(*block) if src is None else src, dst_ref=rows(*block),
                send_sem=send_sems.at[k], recv_sem=recv_sems.at[k], device_id=to, device_id_type=MESH)

        mine = pltpu.make_async_copy(x_ref, rows(*me), local_sem)
        mine.start()
        first = [copy(0, me, sibling, src=x_ref)]
        first += [copy(1 + j, me, (*chip, c), src=x_ref) for j, chip in enumerate(chips)]
        for cp in first:
            cp.start()
        passed = [copy(4 + j, (*chip, c), sibling) for j, chip in enumerate(chips)]
        for j, chip in enumerate(chips):
            copy(1 + j, (*chip, c), me).wait_recv()
            passed[j].start()
        copy(0, sibling, me).wait_recv()
        for j, chip in enumerate(chips):
            copy(4 + j, (*chip, 1 - c), me).wait_recv()
        for cp in first + passed:
            cp.wait_send()
        mine.wait()

    return _pcall(body, name=name, out_shape=_sds((8 * m_per, n), blk.dtype),
                  in_specs=[pl.BlockSpec(memory_space=pltpu.VMEM)], out_specs=pl.BlockSpec(memory_space=pltpu.VMEM),
                  scratch_shapes=[pltpu.SemaphoreType.DMA((7,)), pltpu.SemaphoreType.DMA((7,)), pltpu.SemaphoreType.DMA],
                  compiler_params=_cp())(blk)


ANY = pl.BlockSpec(memory_space=pl.ANY)


def _half(which, rows):
    return pl.ds(pl.multiple_of(which * rows, 16), rows)


def _gather_weights(name, ws, after=()):
    nw = len(ws)
    na = len(after)
    ns = 7

    def body(*refs):
        w_refs, o_refs = refs[:nw], refs[nw + na:2 * nw + na]
        send_sems, recv_sems = refs[2 * nw + na:]
        x, y, c, chips = _place()
        me_chip = _chip_index(x, y)
        sibling = (x, y, 1 - c)

        def rcopy(t, k, src, dst, to):
            return pltpu.make_async_remote_copy(src_ref=src, dst_ref=dst, send_sem=send_sems.at[t * ns + k],
                                                recv_sem=recv_sems.at[t * ns + k], device_id=to, device_id_type=MESH)

        sends = []
        for t in range(nw):
            lh = w_refs[t].shape[0] // 2
            for k, chip in enumerate(chips):
                sends.append(rcopy(t, k, w_refs[t].at[_half(c, lh)], o_refs[t].at[me_chip, _half(c, lh)], (*chip, c)))
                sends[-1].start()
            sends.append(rcopy(t, 6, w_refs[t], o_refs[t].at[me_chip], sibling))
            sends[-1].start()
        for t in range(nw):
            lh = w_refs[t].shape[0] // 2
            for k, chip in enumerate(chips):
                part = o_refs[t].at[_chip_index(*chip), _half(c, lh)]
                rcopy(t, k, part, part, (*chip, c)).wait_recv()
                sends.append(rcopy(t, 3 + k, part, part, sibling))
                sends[-1].start()
        for t in range(nw):
            lh = w_refs[t].shape[0] // 2
            own = o_refs[t].at[me_chip]
            rcopy(t, 6, own, own, sibling).wait_recv()
            for k, chip in enumerate(chips):
                part = o_refs[t].at[_chip_index(*chip), _half(1 - c, lh)]
                rcopy(t, 3 + k, part, part, sibling).wait_recv()
        for cp in sends:
            cp.wait_send()

    return _pcall(body, name=name, out_shape=[_sds((4,) + w.shape, w.dtype) for w in ws],
                  in_specs=[ANY] * (nw + na), out_specs=[ANY] * nw,
                  scratch_shapes=[pltpu.SemaphoreType.DMA((ns * nw,)), pltpu.SemaphoreType.DMA((ns * nw,))],
                  compiler_params=_cp())(*ws, *after)


def _rs_pair(name, gs):
    ng = len(gs)

    def body(*refs):
        g_refs, o_refs = refs[:ng], refs[ng:2 * ng]
        send_sems, recv_sems = refs[2 * ng:]
        x, y, c, _ = _place()
        cps = []
        for t in range(ng):
            lh = g_refs[t].shape[1] // 2
            cp = pltpu.make_async_remote_copy(
                src_ref=g_refs[t].at[:, _half(1 - c, lh)], dst_ref=o_refs[t],
                send_sem=send_sems.at[t], recv_sem=recv_sems.at[t], device_id=(x, y, 1 - c), device_id_type=MESH)
            cp.start()
            cps.append(cp)
        for cp in cps:
            cp.wait()

    outs = [_sds((g.shape[0], g.shape[1] // 2) + g.shape[2:], g.dtype) for g in gs]
    return _pcall(body, name=name, out_shape=outs, in_specs=[ANY] * ng, out_specs=[ANY] * ng,
                  scratch_shapes=[pltpu.SemaphoreType.DMA((ng,)), pltpu.SemaphoreType.DMA((ng,))],
                  compiler_params=_cp())(*gs)


def _rs_chips(name, ps):
    ng = len(ps)

    def body(*refs):
        p_refs, o_refs = refs[:ng], refs[ng:2 * ng]
        send_sems, recv_sems = refs[2 * ng:]
        x, y, c, chips = _place()
        cps = []
        for t in range(ng):
            for k, chip in enumerate(chips):
                cp = pltpu.make_async_remote_copy(
                    src_ref=p_refs[t].at[_chip_index(*chip)], dst_ref=o_refs[t].at[k],
                    send_sem=send_sems.at[t * 3 + k], recv_sem=recv_sems.at[t * 3 + k],
                    device_id=(*chip, c), device_id_type=MESH)
                cp.start()
                cps.append(cp)
        for cp in cps:
            cp.wait()

    return _pcall(body, name=name, out_shape=[_sds((3,) + p.shape[1:], p.dtype) for p in ps],
                  in_specs=[ANY] * ng, out_specs=[ANY] * ng,
                  scratch_shapes=[pltpu.SemaphoreType.DMA((3 * ng,)), pltpu.SemaphoreType.DMA((3 * ng,))],
                  compiler_params=_cp())(*ps)


def _rs_share(name, ss):
    ng = len(ss)

    def body(*refs):
        o_refs = refs[ng:2 * ng]
        send_sems, recv_sems = refs[2 * ng:]
        x, y, c, _ = _place()
        cps = []
        for t in range(ng):
            lh = o_refs[t].shape[1] // 2
            mine = o_refs[t].at[:, _half(c, lh)]
            cp = pltpu.make_async_remote_copy(
                src_ref=mine, dst_ref=mine, send_sem=send_sems.at[t], recv_sem=recv_sems.at[t],
                device_id=(x, y, 1 - c), device_id_type=MESH)
            cp.start()
            cps.append(cp)
        for t in range(ng):
            lh = o_refs[t].shape[1] // 2
            cps[t].wait_send()
            theirs = o_refs[t].at[:, _half(1 - c, lh)]
            pltpu.make_async_remote_copy(
                src_ref=theirs, dst_ref=theirs, send_sem=send_sems.at[t], recv_sem=recv_sems.at[t],
                device_id=(x, y, 1 - c), device_id_type=MESH).wait_recv()

    return _pcall(body, name=name, out_shape=[_sds(s.shape, s.dtype) for s in ss],
                  in_specs=[ANY] * ng, out_specs=[ANY] * ng, input_output_aliases={t: t for t in range(ng)},
                  scratch_shapes=[pltpu.SemaphoreType.DMA((ng,)), pltpu.SemaphoreType.DMA((ng,))],
                  compiler_params=_cp())(*ss)


HBM = pl.BlockSpec(memory_space=pltpu.HBM)
SEMS = pl.BlockSpec(memory_space=pltpu.SEMAPHORE)
EFFECT = pltpu.SideEffectType.DATAFLOW_SIDE_EFFECTING
TOKEN = (8, 128)


def _in_hbm(a):
    return pltpu.with_memory_space_constraint(a, pltpu.HBM)


def _split_start(name, srcs, lands, copies, after):
    ns, nl, na = len(srcs), len(lands), len(after)
    ncopies = len(copies([s for s in srcs], [l for l in lands], probe=True))

    def body(*refs):
        src_refs, land_refs = refs[:ns], refs[ns:ns + nl]
        send_sems, recv_sems = refs[ns + nl + na], refs[ns + nl + na + 1]
        token = refs[-1]
        for k, (src, dst, to) in enumerate(copies(src_refs, land_refs)):
            pltpu.make_async_remote_copy(src_ref=src, dst_ref=dst, send_sem=send_sems.at[k], recv_sem=recv_sems.at[k],
                                         device_id=to, device_id_type=MESH).start()
        token[...] = jnp.zeros_like(token)

    thru = [pltpu.HBM(a.shape, a.dtype) for a in list(srcs) + list(lands)]
    outs = _pcall(body, name=name,
                  out_shape=(pltpu.SemaphoreType.DMA((ncopies,)), pltpu.SemaphoreType.DMA((ncopies,)), *thru, _sds(TOKEN)),
                  in_specs=[HBM] * (ns + nl) + [ANY] * na,
                  out_specs=(SEMS, SEMS, *([HBM] * (ns + nl)), pl.BlockSpec(memory_space=pltpu.VMEM)),
                  input_output_aliases={t: 2 + t for t in range(ns + nl)},
                  compiler_params=pltpu.CompilerParams(has_side_effects=EFFECT))(
        *[_in_hbm(a) for a in list(srcs) + list(lands)], *after)
    return outs[0], outs[1], list(outs[2:2 + ns]), list(outs[2 + ns:2 + ns + nl]), outs[-1]


def _split_wait(name, started, copies, after):
    send_sems, recv_sems, srcs, lands, _ = started
    ns, nl, na = len(srcs), len(lands), len(after)

    def body(*refs):
        src_refs, land_refs = refs[:ns], refs[ns:ns + nl]
        send_sems_ref, recv_sems_ref = refs[ns + nl], refs[ns + nl + 1]
        for k, (src, dst, to) in enumerate(copies(src_refs, land_refs)):
            cp = pltpu.make_async_remote_copy(src_ref=src, dst_ref=dst, send_sem=send_sems_ref.at[k],
                                              recv_sem=recv_sems_ref.at[k], device_id=to, device_id_type=MESH)
            cp.wait_send()
            cp.wait_recv()

    thru = [pltpu.HBM(a.shape, a.dtype) for a in list(srcs) + list(lands)]
    outs = _pcall(body, name=name, out_shape=tuple(thru),
                  in_specs=[HBM] * (ns + nl) + [SEMS, SEMS] + [ANY] * na, out_specs=tuple([HBM] * (ns + nl)),
                  input_output_aliases={t: t for t in range(ns + nl)},
                  compiler_params=pltpu.CompilerParams(has_side_effects=EFFECT))(
        *srcs, *lands, send_sems, recv_sems, *after)
    return list(outs[ns:])


def _gather_copies(n):
    def copies(src_refs, land_refs, probe=False):
        if probe:
            return [None] * (4 * n)
        x, y, c, chips = _place()
        me_chip = _chip_index(x, y)
        out = []
        for t in range(n):
            for to in [(*chip, c) for chip in chips] + [(x, y, 1 - c)]:
                out.append((src_refs[t], land_refs[t].at[me_chip], to))
        return out
    return copies


def _scatter_copies(n):
    def copies(src_refs, land_refs, probe=False):
        if probe:
            return [None] * (3 * n)
        x, y, c, chips = _place()
        out = []
        for t in range(n):
            for k, chip in enumerate(chips):
                out.append((src_refs[t].at[_chip_index(*chip)], land_refs[t].at[k], (*chip, c)))
        return out
    return copies


def _row_block(r, cn):
    if r % 8:
        return r
    best = 8
    for d in range(8, r + 1, 8):
        if r % d == 0 and d * cn * 4 <= (1 << 20):
            best = d
    return best


def _add_half(name, g, a, idx):
    j, rh, cn = a.shape
    tr = _row_block(rh, cn)
    nb = rh // tr

    def body(i_ref, g_ref, a_ref, o_ref):
        o_ref[...] = (g_ref[...] + a_ref[...]).astype(BF16)

    blk = (None, tr, cn)
    gs = pltpu.PrefetchScalarGridSpec(
        num_scalar_prefetch=1, grid=(j, nb),
        in_specs=[pl.BlockSpec(blk, lambda jj, i, i_ref: (jj, i_ref[0] * nb + i, 0)),
                  pl.BlockSpec(blk, lambda jj, i, i_ref: (jj, i, 0))],
        out_specs=pl.BlockSpec(blk, lambda jj, i, i_ref: (jj, i, 0)))
    return _pcall(body, name=name, grid_spec=gs, out_shape=_sds(a.shape, BF16),
                  compiler_params=_cp(("parallel", "parallel")))(idx, g, a)


def _sum_final(name, g, a, b, idx, buf, lyr, nlyr):
    _, r, cn = g.shape
    rh = r // 2
    tr = _row_block(rh, cn)
    nb = rh // tr

    def body(*refs):
        g_ref, a_ref, b_ref = refs[1:4]
        o_ref = refs[-1]
        own = g_ref[...] + a_ref[...]
        o_ref[...] = (own + b_ref[0].astype(F32)) + (b_ref[1].astype(F32) + b_ref[2].astype(F32))

    blk = (None, tr, cn)
    in_specs = [pl.BlockSpec(blk, lambda i, i_ref: (i_ref[1], i_ref[0] * nb + i, 0)),
                pl.BlockSpec(blk, lambda i, i_ref: (i_ref[1], i, 0)),
                pl.BlockSpec((3, tr, cn), lambda i, i_ref: (0, i, 0))]
    args = [idx, g, a, b]
    kw = {}
    if buf is not None:
        in_specs.append(ANY)
        args.append(buf)
        kw["input_output_aliases"] = {4: 0}
    gs = pltpu.PrefetchScalarGridSpec(
        num_scalar_prefetch=1, grid=(nb,), in_specs=in_specs,
        out_specs=pl.BlockSpec(blk, lambda i, i_ref: (lyr, i_ref[0] * nb + i, 0)))
    return _pcall(body, name=name, grid_spec=gs, out_shape=_sds((nlyr, r, cn)),
                  compiler_params=_cp(("parallel",)), **kw)(*args)


def _sum8(name, g):
    _, r, n = g.shape
    tr = 8

    def body(g_ref, o_ref):
        o_ref[...] = ((g_ref[0] + g_ref[1]) + (g_ref[2] + g_ref[3])) + ((g_ref[4] + g_ref[5]) + (g_ref[6] + g_ref[7]))

    return _pcall(body, name=name, grid=(r // tr,), in_specs=[pl.BlockSpec((8, tr, n), lambda i: (0, i, 0))],
                  out_specs=pl.BlockSpec((tr, n), lambda i: (i, 0)), out_shape=_sds((r, n)),
                  compiler_params=_cp(("parallel",)))(g)


def _ada_mod(name, c16, ada_w, bias):
    nl, dm, n = ada_w.shape

    def body(c_ref, w_ref, b_ref, o_ref):
        o_ref[...] = _dot(_silu(c_ref[...]), w_ref[...], NN) + b_ref[...]

    return _pcall(body, name=name, grid=(nl,),
                  in_specs=[_full(c16.shape), pl.BlockSpec((None, dm, n), lambda i: (i, 0, 0)),
                            pl.BlockSpec((None, 1, n), lambda i: (i, 0, 0))],
                  out_specs=pl.BlockSpec((None, 16, n), lambda i: (i, 0, 0)), out_shape=_sds((nl, 16, n)),
                  compiler_params=_cp(("parallel",)))(c16, ada_w, bias)


def _ada_bwd(name, c16, dmod, ada_w):
    nl, dm, n = ada_w.shape

    def body(c_ref, d_ref, w_ref, gw_ref, dc_ref):
        @pl.when(pl.program_id(0) == 0)
        def _():
            dc_ref[...] = jnp.zeros_like(dc_ref)

        dv = d_ref[...]
        gw_ref[...] = _dot(_silu(c_ref[...]), dv, TN)
        dc_ref[...] += _dot(dv, w_ref[...], NT)

    return _pcall(body, name=name, grid=(nl,),
                  in_specs=[_full(c16.shape), pl.BlockSpec((None, 16, n), lambda i: (i, 0, 0)),
                            pl.BlockSpec((None, dm, n), lambda i: (i, 0, 0))],
                  out_specs=[pl.BlockSpec((None, dm, n), lambda i: (i, 0, 0)), _full((16, dm))],
                  out_shape=[_sds((nl, dm, n)), _sds((16, dm))],
                  compiler_params=_cp(("arbitrary",)))(c16, dmod, ada_w)


def _rowsum16(name, dmod):
    nl, _, n = dmod.shape

    def body(d_ref, o_ref):
        o_ref[...] = _colsum(d_ref[...])

    return _pcall(body, name=name, grid=(nl,), in_specs=[pl.BlockSpec((None, 16, n), lambda i: (i, 0, 0))],
                  out_specs=pl.BlockSpec((None, 1, n), lambda i: (i, 0, 0)), out_shape=_sds((nl, 1, n)),
                  compiler_params=_cp(("parallel",)))(dmod)


def _cctx_grad(name, parts, c_ctx):
    def body(p_ref, c_ref, o_ref):
        tot = (p_ref[0:1, :] + p_ref[1:2, :]) + (p_ref[2:3, :] + p_ref[3:4, :])
        o_ref[...] = tot * _dsilu(c_ref[...])

    return _pcall(body, name=name, out_shape=_sds(c_ctx.shape), compiler_params=_cp())(parts, c_ctx)


def _adamw(name, w, g, m, v):
    shape = w.shape
    cn = shape[-1]
    r = math.prod(shape[:-1]) if len(shape) > 1 else 1
    tr = _row_block(r, cn)
    c1 = 1.0 - ADAM_B1 ** ADAM_STEP
    c2 = 1.0 - ADAM_B2 ** ADAM_STEP

    def body(w_ref, g_ref, m_ref, v_ref, d_ref, mo_ref, vo_ref):
        gv = g_ref[...]
        mn = ADAM_B1 * m_ref[...] + (1.0 - ADAM_B1) * gv
        vn = ADAM_B2 * v_ref[...] + (1.0 - ADAM_B2) * (gv * gv)
        d_ref[...] = -ADAM_LR * ((mn / c1) / (jnp.sqrt(vn / c2) + ADAM_EPS) + ADAM_WD * w_ref[...])
        mo_ref[...] = mn
        vo_ref[...] = vn

    blk = pl.BlockSpec((tr, cn), lambda i: (i, 0))
    o = _sds((r, cn))
    outs = _pcall(body, name=name, grid=(r // tr,), in_specs=[blk] * 4, out_specs=[blk] * 3, out_shape=[o, o, o],
                  compiler_params=_cp(("parallel",)))(*[a.reshape(r, cn) for a in (w, g, m, v)])
    return tuple(a.reshape(shape) for a in outs)


def _local_step(xs, target, modt, nw, fnw, get_w, put_g, ev, od, lc):
    t, dm = xs.shape
    nct = lc // ROW_TILE
    depth = nw.shape[0]
    cs, sn = _rope_tables(t, lc)
    saved = []
    x_in, x1p, fp = xs, None, None
    for i in range(depth):
        j, even = i // 2, i % 2 == 0
        tag = f"l{i}"
        w, deps = get_w(i, [fp] if i else [])
        if i == 0:
            _, h = _rnm(tag + "_norm1", x_in, None, None, 0, modt[0], 0, 1, nw[0, 0], nct, deps)
        else:
            x_in, h = _rnm(tag + "_norm1", x1p, fp, modt[i - 1], 5, modt[i], 0, 1, nw[i, 0], nct, deps)
        s = dict(x=x_in, h=h, w=w)
        if even:
            p = _mm_cols(tag + "_in", h, w["in"])
            q, k, v = _even_qkv(tag + "_qkv", p, cs, sn)
            of, ob, ss = _retention_fwd(tag + "_ret", q, k, v, ev["lgb"][j], lc)
            mix, yc = _even_mix(tag + "_mix", p, of, ob, ev["cw"][j], ev["lnw"][j], ev["lnb"][j], nct)
            y = _mm_full(tag + "_out", mix, w["out"], NN)
            s.update(p=p, q=q, k=k, v=v, of=of, ob=ob, ss=ss, yc=yc)
        else:
            p = _mm_cols(tag + "_in", h, w["in"])
            mix, m = _odd_mix(tag + "_mix", p, od["pw"][j], od["ps"][j], od["lnw"][j], od["lnb"][j],
                              od["sgw"][j], od["sgb"][j], nct, lc)
            y = _mm_full(tag + "_out", mix, w["out"], NN)
            s.update(p=p, m=m)
        x1, h2 = _rnm(tag + "_norm2", x_in, y, modt[i], 2, modt[i], 3, 4, nw[i, 1], nct)
        a, gt, up = _ffn_up(tag + "_ffn_up", h2, w["gate"], w["up"])
        f = _mm_full(tag + "_ffn_down", a, w["down"], NN)
        s.update(mix=mix, y=y, x1=x1, h2=h2, a=a, gt=gt, up=up, f=f)
        saved.append(s)
        x1p, fp = x1, f

    loss_blk, dx, df, fin_s = _fin("final", x1p, fp, modt[depth - 1], 5, fnw, target, nct)

    deps = []
    dmod = [[None] * 6 for _ in range(depth)]
    dnw = [[None, None] for _ in range(depth)]
    zero2 = jnp.zeros((2, dm), F32)
    dmod[depth - 1][5] = jnp.stack([zero2[0], fin_s[0]])
    small = dict(dfnw=fin_s[1], ev=[], od=[])
    for i in reversed(range(depth)):
        j, even = i // 2, i % 2 == 0
        tag = f"l{i}b"
        s = saved[i]
        w = s["w"]
        fh = w["down"].shape[0] // 2
        g = {}
        dgt, dup = _ffn_down_bwd(tag + "_ffn_down", df, w["down"], s["gt"], s["up"])
        g["down"] = _wgrad_rows(tag + "_gdown", s["a"], fh, df)
        g["gate"] = _wgrad_rows(tag + "_ggate", dgt, fh, s["h2"])
        g["up"] = _wgrad_rows(tag + "_gup", dup, fh, s["h2"])
        dh2 = _ffn_in_bwd(tag + "_ffn_in", dgt, dup, w["gate"], w["up"])
        dx1, dy, s2 = _bnm(tag + "_norm2", s["x1"], dh2, dx, s["y"], modt[i], 3, 4, modt[i], 2, nw[i, 1], nct)
        dmod[i][3], dmod[i][4], dmod[i][2] = s2[:, 0], s2[:, 1], s2[:, 2]
        dnw[i][1] = s2[1, 3]
        dmix = _mm_full(tag + "_out", dy, w["out"], NT)
        g["out"] = _wgrad_rows(tag + "_gout", s["mix"], w["out"].shape[0] // 4, dy)
        if even:
            do, dg, dyc, lns = _even_mix_bwd1(tag + "_mix1", dmix, s["p"], s["of"], s["ob"], s["yc"],
                                              ev["lnw"][j], ev["lnb"][j])
            da, dgb, dcw = _even_conv_bwd(tag + "_conv", dyc, s["p"], ev["cw"][j], nct)
            dqf, dkf, dvf, dqb, dkb, dvb, dl = _retention_bwd(tag + "_ret", s["q"], s["k"], s["v"], do, s["ss"],
                                                              ev["lgb"][j], lc)
            dp = _even_dp(tag + "_dp", (dqf, dqb), (dkf, dkb), (dvf, dvb), dg, da, dgb, cs, sn)
            small["ev"].append(dict(lnw=lns[0], lnb=lns[1], cw=dcw, dl=dl[:, 0]))
        else:
            dm_, dpd, vec, dpw, dsgw, dsgb = _odd_mix_bwd1(tag + "_mix1", dmix, s["p"], s["m"], od["pw"][j], od["ps"][j],
                                                           od["lnw"][j], od["lnb"][j], od["sgw"][j], od["sgb"][j])
            dp = _odd_dp(tag + "_dp", dm_, dpd, nct, lc)
            small["od"].append(dict(ps=vec[0], lnw=vec[1], lnb=vec[2], pw=dpw, sgw=dsgw, sgb=dsgb[:, :, 0]))
        dh = _mm_cols_bwd(tag + "_in", dp, w["in"])
        g["in"] = _wgrad_cols(tag + "_gin", s["h"], dp, w["in"].shape[0])
        deps = put_g(i, g)
        if i > 0:
            dx, df, s1 = _bnm(tag + "_norm1", s["x"], dh, dx1, saved[i - 1]["f"], modt[i], 0, 1, modt[i - 1], 5,
                              nw[i, 0], nct, deps)
            dmod[i - 1][5] = s1[:, 2]
        else:
            dx, _, s1 = _bnm(tag + "_norm1", s["x"], dh, dx1, None, modt[0], 0, 1, None, 0, nw[0, 0], nct, deps)
        dmod[i][0], dmod[i][1] = s1[:, 0], s1[:, 1]
        dnw[i][0] = s1[1, 3]
    small["ev"].reverse()
    small["od"].reverse()
    dmod_t = jnp.stack([jnp.concatenate([jnp.stack(rows, axis=1), jnp.zeros((2, 2, dm), F32)], axis=1) for rows in dmod])
    small["dmod"] = dmod_t
    small["dnw"] = jnp.stack([jnp.stack(r) for r in dnw])
    return loss_blk, dx, small


WEIGHTS = ["c_ctx", "ada_w", "ada_b", "norm_w", "even_w_in", "even_w_out", "ret_decay_logit", "conv_dw_w",
           "conv_ln_w", "conv_ln_b", "odd_w_in", "odd_w_out", "pool_w", "pool_scale", "sg_ln_w", "sg_ln_b",
           "sg_w", "sg_b", "ffn_w_gate", "ffn_w_up", "ffn_w_down", "final_norm_w"]
BIG = dict(even_in="even_w_in", even_out="even_w_out", odd_in="odd_w_in", odd_out="odd_w_out",
           gate="ffn_w_gate", up="ffn_w_up", down="ffn_w_down")


def _rows(a, width=1024):
    flat = a.reshape(-1)
    n = flat.shape[0]
    per = 8 * width
    tot = -(-n // per) * per
    return jnp.pad(flat, (0, tot - n)).reshape(tot // width, width)


def _unshard(parts, lead):
    nl = len(lead)
    perm = tuple(range(1, nl + 1)) + (0, nl + 1)
    return parts.transpose(perm).reshape(tuple(lead) + (4 * parts.shape[-1],))


def _my_cols(a, chip, n):
    start = (0,) * (a.ndim - 1) + (chip * n,)
    return lax.dynamic_slice(a, start, a.shape[:-1] + (n,))


def kernel(x, c, ctx, c_ctx, ada_w, ada_b, norm_w, even_w_in, even_w_out, ret_decay_logit, conv_dw_w, conv_ln_w, conv_ln_b, odd_w_in, odd_w_out, pool_w, pool_scale, sg_ln_w, sg_ln_b, sg_w, sg_b, ffn_w_gate, ffn_w_up, ffn_w_down, final_norm_w, loss_target, m_c_ctx, m_ada_w, m_ada_b, m_norm_w, m_even_w_in, m_even_w_out, m_ret_decay_logit, m_conv_dw_w, m_conv_ln_w, m_conv_ln_b, m_odd_w_in, m_odd_w_out, m_pool_w, m_pool_scale, m_sg_ln_w, m_sg_ln_b, m_sg_w, m_sg_b, m_ffn_w_gate, m_ffn_w_up, m_ffn_w_down, m_final_norm_w, v_c_ctx, v_ada_w, v_ada_b, v_norm_w, v_even_w_in, v_even_w_out, v_ret_decay_logit, v_conv_dw_w, v_conv_ln_w, v_conv_ln_b, v_odd_w_in, v_odd_w_out, v_pool_w, v_pool_scale, v_sg_ln_w, v_sg_ln_b, v_sg_w, v_sg_b, v_ffn_w_gate, v_ffn_w_up, v_ffn_w_down, v_final_norm_w):
    wv = dict(c_ctx=c_ctx, ada_w=ada_w, ada_b=ada_b, norm_w=norm_w, even_w_in=even_w_in, even_w_out=even_w_out,
              ret_decay_logit=ret_decay_logit, conv_dw_w=conv_dw_w, conv_ln_w=conv_ln_w, conv_ln_b=conv_ln_b,
              odd_w_in=odd_w_in, odd_w_out=odd_w_out, pool_w=pool_w, pool_scale=pool_scale, sg_ln_w=sg_ln_w,
              sg_ln_b=sg_ln_b, sg_w=sg_w, sg_b=sg_b, ffn_w_gate=ffn_w_gate, ffn_w_up=ffn_w_up,
              ffn_w_down=ffn_w_down, final_norm_w=final_norm_w)
    mv = dict(zip(WEIGHTS, (m_c_ctx, m_ada_w, m_ada_b, m_norm_w, m_even_w_in, m_even_w_out, m_ret_decay_logit,
                            m_conv_dw_w, m_conv_ln_w, m_conv_ln_b, m_odd_w_in, m_odd_w_out, m_pool_w, m_pool_scale,
                            m_sg_ln_w, m_sg_ln_b, m_sg_w, m_sg_b, m_ffn_w_gate, m_ffn_w_up, m_ffn_w_down,
                            m_final_norm_w)))
    vv = dict(zip(WEIGHTS, (v_c_ctx, v_ada_w, v_ada_b, v_norm_w, v_even_w_in, v_even_w_out, v_ret_decay_logit,
                            v_conv_dw_w, v_conv_ln_w, v_conv_ln_b, v_odd_w_in, v_odd_w_out, v_pool_w, v_pool_scale,
                            v_sg_ln_w, v_sg_ln_b, v_sg_w, v_sg_b, v_ffn_w_gate, v_ffn_w_up, v_ffn_w_down,
                            v_final_norm_w)))
    xi, yi, ci = lax.axis_index("x"), lax.axis_index("y"), lax.axis_index("c")
    chip = 2 * xi + yi
    dev = 4 * xi + 2 * yi + ci
    dm = x.shape[-1]
    lc = ctx.shape[1]
    depth = ada_w.shape[0]
    n_ada = ada_w.shape[-1]

    cw_pad = jnp.pad(conv_dw_w, ((0, 0), (0, 1), (0, 0)))
    vec3 = jnp.stack([pool_scale, sg_ln_w, sg_ln_b])
    pack1 = jnp.concatenate([_rows(c), _rows(norm_w), _rows(cw_pad), _rows(vec3)], axis=0)
    g1 = _all_gather8("gather_small", pack1).reshape(8, 32, dm)
    c_all = g1[:, 0]
    per_chip = g1[0::2]
    norm_full = _unshard(per_chip[:, 8:10].reshape(4, depth, 2, dm // 4), (depth, 2))
    cw_full = _unshard(per_chip[:, 16:24].reshape(4, 2, CONV_K + 1, 128), (2, CONV_K + 1))
    vec_full = _unshard(per_chip[:, 24, :768].reshape(4, 3, 2, 128), (3, 2))

    c16 = jnp.concatenate([c_all, c_ctx[None, :], jnp.zeros((7, dm), F32)], axis=0)
    mod_sh = _ada_mod("ada_mod", c16, ada_w, _my_cols(ada_b, chip, n_ada)[:, None, :])
    g2 = _all_gather8("gather_mod", mod_sh.reshape(depth * 16, n_ada)).reshape(8, depth, 16, n_ada)
    mod_full = _unshard(g2[0::2], (depth, 16))
    mod_x = lax.dynamic_index_in_dim(mod_full, dev, axis=1, keepdims=False).reshape(depth, 6, dm)
    mod_c = mod_full[:, 8].reshape(depth, 6, dm)
    modt = jnp.pad(jnp.stack([mod_c, mod_x], axis=1), ((0, 0), (0, 0), (0, 2), (0, 0)))

    names = list(BIG)
    tr_names = ("gate", "up")
    shard = {k: (jnp.swapaxes(wv[BIG[k]], 1, 2) if k in tr_names else wv[BIG[k]]).astype(BF16) for k in names}
    roles = ("in", "out", "gate", "up", "down")

    def layer_keys(i):
        mixer = ("even_in", "even_out") if i % 2 == 0 else ("odd_in", "odd_out")
        return [(k, i // 2) for k in mixer] + [(k, i) for k in ("gate", "up", "down")]

    def as_used(got):
        return {r: (g if r == "in" else g.reshape(4 * g.shape[1], g.shape[2])) for r, g in zip(roles, got)}

    started = {}

    def get_w(i, after):
        if i > 0:
            return as_used(_split_wait(f"gather_wait{i}", started[i], _gather_copies(len(roles)), after)), []
        got = _gather_weights("gather_w0", [shard[k][l] for k, l in layer_keys(0)], [modt])
        for li in range(1, depth):
            srcs = [shard[k][l] for k, l in layer_keys(li)]
            lands = [lax.empty((4,) + s.shape, s.dtype) for s in srcs]
            before = [got[0]] + ([started[li - 1][4]] if li > 1 else [])
            started[li] = _split_start(f"gather_start{li}", srcs, lands, _gather_copies(len(roles)), before)
        return as_used(got), [started[li][4] for li in range(1, depth)]

    idx = jnp.stack([ci, chip]).astype(jnp.int32)
    pending = {}

    def put_g(i, g):
        glist = [g[r].reshape(4, -1, g[r].shape[-1]) for r in roles]
        from_sib = _rs_pair(f"rs_pair{i}", glist)
        pair = [_add_half(f"rs_add{i}_{r}", gl, a, idx) for r, gl, a in zip(roles, glist, from_sib)]
        lands = [lax.empty((3,) + p.shape[1:], p.dtype) for p in pair]
        st = _split_start(f"rs_start{i}", pair, lands, _scatter_copies(len(roles)), [])
        pending[i] = (glist, from_sib, st)
        return [st[4]]

    ev = dict(lgb=jnp.broadcast_to(ret_decay_logit.reshape(-1, 2 * HEADS)[:, :, None], (ret_decay_logit.shape[0], 2 * HEADS, HEAD_DIM)),
              cw=cw_full, lnw=conv_ln_w[:, None, :], lnb=conv_ln_b[:, None, :])
    od = dict(pw=pool_w, ps=vec_full[0][:, None, :], lnw=vec_full[1][:, None, :], lnb=vec_full[2][:, None, :],
              sgw=sg_w, sgb=jnp.broadcast_to(sg_b[:, :, :, None], sg_b.shape + (GC,)))
    xs = jnp.concatenate([ctx[0], x[0]], axis=0)
    loss_blk, dxs, small = _local_step(xs, loss_target[0], modt, norm_full[:, :, None, :], final_norm_w[None, :],
                                       get_w, put_g, ev, od, lc)

    misc = jnp.stack([
        small["dfnw"], jnp.broadcast_to(loss_blk[0, 0], (dm,)),
        jnp.concatenate([e["lnw"] for e in small["ev"]]), jnp.concatenate([e["lnb"] for e in small["ev"]]),
        jnp.concatenate([o["ps"] for o in small["od"]]), jnp.concatenate([o["lnw"] for o in small["od"]]),
        jnp.concatenate([o["lnb"] for o in small["od"]]),
        jnp.pad(jnp.concatenate([e["dl"] for e in small["ev"]]), (0, dm - 4 * HEADS)),
        jnp.stack([o["sgb"] for o in small["od"]]).reshape(-1)])
    pack2 = jnp.concatenate([
        _rows(small["dmod"]), _rows(small["dnw"]), _rows(misc), _rows(jnp.stack([e["cw"] for e in small["ev"]])),
        _rows(jnp.stack([o["pw"] for o in small["od"]])), _rows(jnp.stack([o["sgw"] for o in small["od"]]))], axis=0)
    n2 = pack2.shape[0]
    g3 = _all_gather8("gather_grads", pack2)
    tot = _sum8("sum_grads", g3.reshape(8, n2, dm))
    r_mod = depth * 16
    o_nw, o_misc = r_mod, r_mod + 8
    o_cw = o_misc + 16
    o_pw = o_cw + 2 * (CONV_K + 1) // 2
    o_sgw = o_pw + 128
    dmod_sum = tot[:r_mod].reshape(depth, 2, 8, dm)
    dmod_dev = g3.reshape(8, n2, dm)[:, :r_mod].reshape(8, depth, 2, 8, dm)
    dm_x = dmod_dev[:, :, 1, :6].reshape(8, depth, 6 * dm).transpose(1, 0, 2)
    dm_c = dmod_sum[:, 0, :6].reshape(depth, 1, 6 * dm)
    dmod16 = jnp.concatenate([dm_x, dm_c, jnp.zeros((depth, 7, 6 * dm), F32)], axis=1)
    g_ada_b = _rowsum16("ada_b_grad", dmod16)[:, 0]
    g_ada_w, dc16 = _ada_bwd("ada_bwd", c16, _my_cols(dmod16, chip, n_ada), ada_w)
    g4 = _all_gather8("gather_cctx", dc16[8:16]).reshape(8, 8, dm)
    g_c_ctx = _cctx_grad("cctx_grad", g4[0::2, 0], c_ctx[None, :])[0]

    misc_t = tot[o_misc:o_misc + 16]
    half = lambda row: misc_t[row].reshape(2, dm // 2)
    grads = dict(
        c_ctx=g_c_ctx, ada_w=g_ada_w, ada_b=g_ada_b,
        norm_w=_my_cols(tot[o_nw:o_nw + 8].reshape(depth, 2, dm), chip, dm // 4),
        ret_decay_logit=misc_t[7, :4 * HEADS].reshape(ret_decay_logit.shape),
        conv_dw_w=_my_cols(tot[o_cw:o_cw + 2 * (CONV_K + 1) // 2].reshape(2, CONV_K + 1, dm // 2)[:, :CONV_K], chip, 128),
        conv_ln_w=half(2), conv_ln_b=half(3),
        pool_w=tot[o_pw:o_pw + 128].reshape(pool_w.shape),
        pool_scale=_my_cols(half(4), chip, 128), sg_ln_w=_my_cols(half(5), chip, 128), sg_ln_b=_my_cols(half(6), chip, 128),
        sg_w=tot[o_sgw:o_sgw + 128].reshape(sg_w.shape), sg_b=misc_t[8].reshape(sg_b.shape),
        final_norm_w=misc_t[0])
    loss = misc_t[1, 0]

    reduced = {k: None for k in names}
    for i in reversed(range(depth)):
        glist, from_sib, st = pending[i]
        slots = _split_wait(f"rs_wait{i}", st, _scatter_copies(len(roles)), [g_c_ctx])
        for (k, l), g, a, b in zip(layer_keys(i), glist, from_sib, slots):
            reduced[k] = _sum_final(f"rs_sum_{k}{l}", g, a, b, idx, reduced[k], l, shard[k].shape[0])
    shards = dict(zip(names, _rs_share("rs_share", [reduced[k] for k in names])))

    deltas, new_m, new_v = {}, {}, {}
    for k in names:
        n = BIG[k]
        if k in tr_names:
            tr = lambda a: jnp.swapaxes(a, 1, 2)
            outs = _adamw("adamw_" + n, tr(wv[n]), shards[k], tr(mv[n]), tr(vv[n]))
            grads[n] = tr(shards[k])
            deltas[n], new_m[n], new_v[n] = (tr(o) for o in outs)
        else:
            grads[n] = shards[k]
    for n in WEIGHTS:
        if n not in deltas:
            deltas[n], new_m[n], new_v[n] = _adamw("adamw_" + n, wv[n], grads[n], mv[n], vv[n])
    grad_x = dxs[lc:][None]
    return (loss, grad_x, *[grads[n] for n in WEIGHTS], *[deltas[n] for n in WEIGHTS],
            *[new_m[n] for n in WEIGHTS], *[new_v[n] for n in WEIGHTS])
```

```python
import functools
import math

import jax
import jax.numpy as jnp
from jax import lax
from jax.experimental import pallas as pl
from jax.experimental.pallas import tpu as pltpu

F32 = jnp.float32
BF16 = jnp.bfloat16
MESH = pl.DeviceIdType.MESH

EPS = 1e-6
GRID_W = 64
HEADS = 4
HEAD_DIM = 128
CHUNK = 128
CONV_K = 31
ROPE_BASE = 10000.0
ROPE_PAIRS = (16, 24, 24)
POOL_WINDOWS = (2, 4, 8, 16)
ADAM_LR, ADAM_B1, ADAM_B2, ADAM_EPS, ADAM_WD, ADAM_STEP = 0.001, 0.9, 0.999, 1e-08, 0.01, 10

ROW_TILE = 256
CONV_HALO = 16
POOL_HALO = 8
VMEM_LIMIT = 56 * 1024 * 1024
WGRAD_ROWS = 2304


def _pcall(body, **kw):
    return pl.pallas_call(body, **kw)


def _cp(sem=None, vmem=VMEM_LIMIT):
    if sem is None:
        return pltpu.CompilerParams(vmem_limit_bytes=vmem)
    return pltpu.CompilerParams(dimension_semantics=sem, vmem_limit_bytes=vmem)


def _sds(shape, dtype=F32):
    return jax.ShapeDtypeStruct(tuple(shape), dtype)


def _full(shape):
    nd = len(shape)
    return pl.BlockSpec(tuple(shape), lambda *_: (0,) * nd)


def _sigmoid(x):
    return jax.nn.sigmoid(x)


def _silu(x):
    return x * _sigmoid(x)


def _dsilu(x):
    s = _sigmoid(x)
    return s * (1.0 + x * (1.0 - s))


def _colsum(a):
    return jnp.sum(a, axis=0, keepdims=True)


def _dot(a, b, dn):
    return lax.dot_general(a.astype(BF16), b.astype(BF16), dn, preferred_element_type=F32)


NN = (((1,), (0,)), ((), ()))
NT = (((1,), (1,)), ((), ()))
TN = (((0,), (0,)), ((), ()))


def _mm_tile(t, cap=1152):
    best = 16
    for d in range(16, min(t, cap) + 1, 16):
        if t % d == 0:
            best = d
    return best


def _mm(name, pairs, grid, out_shape, out_spec, dn):
    npairs = len(pairs)
    nk = grid[-1]
    kax = len(grid) - 1
    assert nk == 1 or out_shape.dtype == F32

    def body(*refs):
        ins = refs[:2 * npairs]
        o_ref = refs[2 * npairs]
        tot = None
        for p in range(npairs):
            d = _dot(ins[2 * p][...], ins[2 * p + 1][...], dn)
            tot = d if tot is None else tot + d
        if nk == 1:
            o_ref[...] = tot.astype(o_ref.dtype)
        else:
            k = pl.program_id(kax)

            @pl.when(k == 0)
            def _():
                o_ref[...] = tot

            @pl.when(k != 0)
            def _():
                o_ref[...] += tot

    args, in_specs = [], []
    for a, a_spec, b, b_spec in pairs:
        args += [a, b]
        in_specs += [a_spec, b_spec]
    sem = ("parallel",) * kax + ("arbitrary",)
    return _pcall(body, name=name, grid=grid, in_specs=in_specs, out_specs=out_spec, out_shape=out_shape,
                  compiler_params=_cp(sem))(*args)


def _mm_cols(name, a, w, out_dtype=F32):
    t, k = a.shape
    j, _, n = w.shape
    tm = _mm_tile(t)
    return _mm(name, [(a, pl.BlockSpec((tm, k), lambda i, jj, kk: (i, 0)),
                       w, pl.BlockSpec((None, k, n), lambda i, jj, kk: (jj, 0, 0)))],
               (t // tm, j, 1), _sds((t, j * n), out_dtype), pl.BlockSpec((tm, n), lambda i, jj, kk: (i, jj)), NN)


def _mm_cols_bwd(name, d, w):
    t = d.shape[0]
    j, k, n = w.shape
    tm = _mm_tile(t)
    pairs = [(d, pl.BlockSpec((tm, n), functools.partial(lambda jj, i, u, kk: (i, jj), jj)),
              w, pl.BlockSpec((None, k, n), functools.partial(lambda jj, i, u, kk: (jj, 0, 0), jj))) for jj in range(j)]
    return _mm(name, pairs, (t // tm, 1, 1), _sds((t, k)), pl.BlockSpec((tm, k), lambda i, u, kk: (i, 0)), NT)


def _mm_full(name, a, w, dn, tm=None):
    t, k = a.shape
    n = w.shape[1] if dn is NN else w.shape[0]
    tm = tm or _mm_tile(t)
    return _mm(name, [(a, pl.BlockSpec((tm, k), lambda i, u, kk: (i, 0)), w, _full(w.shape))],
               (t // tm, 1, 1), _sds((t, n)), pl.BlockSpec((tm, n), lambda i, u, kk: (i, 0)), dn)


def _wgrad_cols(name, a, b, j):
    t, k = a.shape
    n = b.shape[1] // j
    tt = _mm_tile(t, WGRAD_ROWS)
    return _mm(name, [(a, pl.BlockSpec((tt, k), lambda jj, u, kk: (kk, 0)),
                       b, pl.BlockSpec((tt, n), lambda jj, u, kk: (kk, jj)))],
               (j, 1, t // tt), _sds((j, k, n)), pl.BlockSpec((None, k, n), lambda jj, u, kk: (jj, 0, 0)), TN)


def _wgrad_rows(name, a, blk, b):
    t, f = a.shape
    n = b.shape[1]
    tt = _mm_tile(t, WGRAD_ROWS)
    return _mm(name, [(a, pl.BlockSpec((tt, blk), lambda jj, u, kk: (kk, jj)),
                       b, pl.BlockSpec((tt, n), lambda jj, u, kk: (kk, 0)))],
               (f // blk, 1, t // tt), _sds((f, n)), pl.BlockSpec((blk, n), lambda jj, u, kk: (jj, 0)), TN)


def _ffn_tiles(t, f):
    return _mm_tile(t, 288), f


def _ffn_up(name, h, wgt, wut):
    t, k = h.shape
    f = wgt.shape[0]
    tm, tn = _ffn_tiles(t, f)

    def body(h_ref, wg_ref, wu_ref, a_ref, gt_ref, up_ref):
        hv = h_ref[...]
        gt = _dot(hv, wg_ref[...], NT)
        up = _dot(hv, wu_ref[...], NT)
        a_ref[...] = (_silu(gt) * up).astype(BF16)
        gt_ref[...] = gt.astype(BF16)
        up_ref[...] = up.astype(BF16)

    wspec = pl.BlockSpec((tn, k), lambda i, jj: (jj, 0))
    ospec = pl.BlockSpec((tm, tn), lambda i, jj: (i, jj))
    o = _sds((t, f), BF16)
    return _pcall(body, name=name, grid=(t // tm, f // tn),
                  in_specs=[pl.BlockSpec((tm, k), lambda i, jj: (i, 0)), wspec, wspec],
                  out_specs=[ospec, ospec, ospec], out_shape=[o, o, o],
                  compiler_params=_cp(("parallel", "parallel")))(h, wgt, wut)


def _ffn_down_bwd(name, df, wd, gt, up, deps=()):
    t, dm = df.shape
    f = wd.shape[0]
    tm, tn = _ffn_tiles(t, f)
    nd = len(deps)

    def body(*refs):
        df_ref, wd_ref, gt_ref, up_ref = refs[:4]
        dgt_ref, dup_ref = refs[4 + nd:]
        da = _dot(df_ref[...], wd_ref[...], NT)
        g = gt_ref[...].astype(F32)
        u = up_ref[...].astype(F32)
        dgt_ref[...] = (da * u * _dsilu(g)).astype(BF16)
        dup_ref[...] = (da * _silu(g)).astype(BF16)

    aspec = pl.BlockSpec((tm, tn), lambda i, jj: (i, jj))
    o = _sds((t, f), BF16)
    return _pcall(body, name=name, grid=(t // tm, f // tn),
                  in_specs=[pl.BlockSpec((tm, dm), lambda i, jj: (i, 0)),
                            pl.BlockSpec((tn, dm), lambda i, jj: (jj, 0)), aspec, aspec]
                  + [pl.BlockSpec(d.shape, lambda i, jj: (0, 0)) for d in deps],
                  out_specs=[aspec, aspec], out_shape=[o, o],
                  compiler_params=_cp(("parallel", "parallel")))(df, wd, gt, up, *deps)


def _ffn_in_bwd(name, dgt, dup, wgt, wut):
    t, f = dgt.shape
    k = wgt.shape[1]
    tm = _mm_tile(t, 576)
    aspec = pl.BlockSpec((tm, f), lambda i, u, kk: (i, 0))
    wspec = pl.BlockSpec((f, k), lambda i, u, kk: (0, 0))
    return _mm(name, [(dgt, aspec, wgt, wspec), (dup, aspec, wut, wspec)], (t // tm, 1, 1), _sds((t, k)),
               pl.BlockSpec((tm, k), lambda i, u, kk: (i, 0)), NN)


def _modrow(ref, row, is_ctx):
    return jnp.where(is_ctx, ref[0, row:row + 1, :], ref[1, row:row + 1, :])


def _rnm(name, x, delta, mod_g, g_row, mod_n, sh_row, sc_row, nw, nct, deps=()):
    t, dm = x.shape
    tm = ROW_TILE
    has = delta is not None
    nd = len(deps)

    def body(*refs):
        refs = refs[:len(refs) - nd - (2 if has else 1)] + refs[len(refs) - (2 if has else 1):]
        if has:
            x_ref, d_ref, mg_ref, m_ref, nw_ref, xo_ref, h_ref = refs
        else:
            x_ref, m_ref, nw_ref, h_ref = refs
        is_ctx = pl.program_id(0) < nct
        xv = x_ref[...]
        if has:
            xv = xv + _modrow(mg_ref, g_row, is_ctx) * d_ref[...]
            xo_ref[...] = xv
        r = lax.rsqrt(jnp.mean(xv * xv, axis=-1, keepdims=True) + EPS)
        hv = (xv * r * nw_ref[...]) * (1.0 + _modrow(m_ref, sc_row, is_ctx)) + _modrow(m_ref, sh_row, is_ctx)
        h_ref[...] = hv.astype(BF16)

    row = pl.BlockSpec((tm, dm), lambda i: (i, 0))
    ins = [x] + ([delta, mod_g] if has else []) + [mod_n, nw] + list(deps)
    in_specs = ([row] + ([row, _full(mod_g.shape)] if has else []) + [_full(mod_n.shape), _full(nw.shape)]
                + [_full(d.shape) for d in deps])
    outs = ([_sds((t, dm))] if has else []) + [_sds((t, dm), BF16)]
    out_specs = ([row] if has else []) + [row]
    res = _pcall(body, name=name, grid=(t // tm,), in_specs=in_specs, out_specs=out_specs, out_shape=outs,
                 compiler_params=_cp(("parallel",)))(*ins)
    return res if has else (None, res[0])


def _bnm(name, xn, dh, dup, yprev, mod_n, sh_row, sc_row, mod_g, g_row, nw, nct, deps=()):
    t, dm = xn.shape
    tm = ROW_TILE
    has = yprev is not None
    nd = len(deps)

    def body(*refs):
        nout = 3 if has else 2
        refs = refs[:len(refs) - nd - nout] + refs[len(refs) - nout:]
        if has:
            x_ref, dh_ref, du_ref, y_ref, mn_ref, mg_ref, nw_ref, dx_ref, dd_ref, s_ref = refs
        else:
            x_ref, dh_ref, du_ref, mn_ref, nw_ref, dx_ref, s_ref = refs
        i = pl.program_id(0)
        is_ctx = i < nct

        @pl.when(i == 0)
        def _():
            s_ref[...] = jnp.zeros_like(s_ref)

        xv = x_ref[...]
        r = lax.rsqrt(jnp.mean(xv * xv, axis=-1, keepdims=True) + EPS)
        xh = xv * r
        w = nw_ref[...]
        sc1 = 1.0 + _modrow(mn_ref, sc_row, is_ctx)
        dhv = dh_ref[...]
        dxh = dhv * sc1 * w
        dx = r * (dxh - xh * jnp.mean(dxh * xh, axis=-1, keepdims=True)) + du_ref[...]
        dx_ref[...] = dx
        parts = [_colsum(dhv), _colsum(dhv * (xh * w))]
        if has:
            dd_ref[...] = (_modrow(mg_ref, g_row, is_ctx) * dx).astype(BF16)
            parts.append(_colsum(dx * y_ref[...]))
        else:
            parts.append(jnp.zeros((1, dm), F32))
        upd = jnp.concatenate(parts + [jnp.zeros((5, dm), F32)], axis=0)
        dnw = jnp.concatenate([jnp.zeros((3, dm), F32), _colsum(dhv * sc1 * xh), jnp.zeros((4, dm), F32)], axis=0)

        @pl.when(is_ctx)
        def _():
            s_ref[0] += upd
            s_ref[1] += dnw

        @pl.when(jnp.logical_not(is_ctx))
        def _():
            s_ref[1] += upd + dnw

    row = pl.BlockSpec((tm, dm), lambda i: (i, 0))
    ins = [xn, dh, dup] + ([yprev] if has else []) + [mod_n] + ([mod_g] if has else []) + [nw] + list(deps)
    in_specs = ([row, row, row] + ([row] if has else []) + [_full(mod_n.shape)]
                + ([_full(mod_g.shape)] if has else []) + [_full(nw.shape)] + [_full(d.shape) for d in deps])
    outs = [_sds((t, dm))] + ([_sds((t, dm), BF16)] if has else []) + [_sds((2, 8, dm))]
    out_specs = [row] + ([row] if has else []) + [_full((2, 8, dm))]
    res = _pcall(body, name=name, grid=(t // tm,), in_specs=in_specs, out_specs=out_specs, out_shape=outs,
                 compiler_params=_cp(("arbitrary",)))(*ins)
    return res if has else (res[0], None, res[1])


def _fin(name, x1, f, mod, g_row, fw, target, nct):
    t, dm = x1.shape
    tm = ROW_TILE

    def body(x_ref, f_ref, m_ref, fw_ref, t_ref, loss_ref, dx_ref, dd_ref, s_ref):
        i = pl.program_id(0)

        @pl.when(i == 0)
        def _():
            s_ref[...] = jnp.zeros_like(s_ref)
            loss_ref[...] = jnp.zeros_like(loss_ref)

        @pl.when(i < nct)
        def _():
            dx_ref[...] = jnp.zeros_like(dx_ref)
            dd_ref[...] = jnp.zeros_like(dd_ref)

        @pl.when(i >= nct)
        def _():
            g = m_ref[1, g_row:g_row + 1, :]
            fv = f_ref[...]
            xv = x_ref[...] + g * fv
            r = lax.rsqrt(jnp.mean(xv * xv, axis=-1, keepdims=True) + EPS)
            xh = xv * r
            w = fw_ref[...]
            err = xh * w - t_ref[...]
            loss_ref[...] += 0.5 * jnp.sum(err * err) / dm
            dout = err * (1.0 / dm)
            dxh = dout * w
            dx = r * (dxh - xh * jnp.mean(dxh * xh, axis=-1, keepdims=True))
            dx_ref[...] = dx
            dd_ref[...] = (g * dx).astype(BF16)
            s_ref[...] += jnp.concatenate([_colsum(dx * fv), _colsum(dout * xh), jnp.zeros((6, dm), F32)], axis=0)

    row = pl.BlockSpec((tm, dm), lambda i: (i, 0))
    trow = pl.BlockSpec((tm, dm), lambda i: (jnp.maximum(i - nct, 0), 0))
    return _pcall(body, name=name, grid=(t // tm,),
                  in_specs=[row, row, _full(mod.shape), _full(fw.shape), trow],
                  out_specs=[_full((8, 128)), row, row, _full((8, dm))],
                  out_shape=[_sds((8, 128)), _sds((t, dm)), _sds((t, dm), BF16), _sds((8, dm))],
                  compiler_params=_cp(("arbitrary",)))(x1, f, mod, fw, target)


def _rope_tables(t, lc):
    l = t - lc
    rows = l // GRID_W
    grid_r = jnp.broadcast_to(jnp.arange(rows, dtype=F32)[:, None], (rows, GRID_W)).reshape(-1)
    grid_c = jnp.broadcast_to(jnp.arange(GRID_W, dtype=F32)[None, :], (rows, GRID_W)).reshape(-1)

    def angles(p_seq, p_row, p_col):
        parts = []
        for p, n in zip((p_seq, p_row, p_col), ROPE_PAIRS):
            freq = ROPE_BASE ** (-jnp.arange(n, dtype=F32) / n)
            parts.append(p[:, None] * freq[None, :])
        return jnp.concatenate(parts, axis=-1)

    zc = jnp.zeros((lc,), F32)
    ang = jnp.concatenate([angles(jnp.arange(lc, dtype=F32), zc, zc),
                           angles(jnp.full((l,), lc, F32), grid_r, grid_c)], axis=0)
    cos, sin = jnp.cos(ang), jnp.sin(ang)
    return jnp.concatenate([cos, cos], axis=-1), jnp.concatenate([-sin, sin], axis=-1)


def _rope(u, cs, sn):
    return u * cs + pltpu.roll(u, HEAD_DIM // 2, 1) * sn


def _rope_t(d, cs, sn):
    return d * cs + pltpu.roll(d * sn, HEAD_DIM // 2, 1)


def _even_qkv(name, p, cs, sn):
    t = p.shape[0]
    tm = ROW_TILE
    w = HEADS * HEAD_DIM
    scale = HEAD_DIM ** -0.5

    def body(q_ref, k_ref, v_ref, cs_ref, sn_ref, qo_ref, ko_ref, vo_ref):
        c, s = cs_ref[...], sn_ref[...]
        for h in range(HEADS):
            sl = slice(h * HEAD_DIM, (h + 1) * HEAD_DIM)
            qo_ref[:, sl] = (_rope(q_ref[:, sl], c, s) * scale).astype(BF16)
            ko_ref[:, sl] = _rope(k_ref[:, sl], c, s).astype(BF16)
        vo_ref[...] = v_ref[...].astype(BF16)

    col = lambda j: pl.BlockSpec((tm, w), lambda i: (i, j))
    tab = pl.BlockSpec((tm, HEAD_DIM), lambda i: (i, 0))
    o = _sds((t, w), BF16)
    return _pcall(body, name=name, grid=(t // tm,), in_specs=[col(0), col(1), col(2), tab, tab],
                  out_specs=[col(0)] * 3, out_shape=[o, o, o], compiler_params=_cp(("parallel",)))(p, p, p, cs, sn)


def _log_sigmoid_row(x):
    e = jnp.exp(-jnp.abs(x))
    l1p = jnp.where(e < 0.01, e * (1.0 - e * (0.5 - e * (1.0 / 3.0))), jnp.log(1.0 + e))
    return jnp.minimum(x, 0.0) - l1p


def _ret_tables(lgb_ref, dm_ref, xi_ref, zt_ref):
    ri = lax.broadcasted_iota(jnp.int32, (CHUNK, CHUNK), 0).astype(F32)
    ci = lax.broadcasted_iota(jnp.int32, (CHUNK, CHUNK), 1).astype(F32)
    for d in range(2):
        for h in range(HEADS):
            idx = d * HEADS + h
            lg = _log_sigmoid_row(lgb_ref[idx:idx + 1, :])
            if d == 0:
                e, mask = ri - ci, ri >= ci
                xe, ze = ri + 1.0, (CHUNK - 1.0) - ri
            else:
                e, mask = ci - ri - 1.0, ci > ri
                xe, ze = (CHUNK - 1.0) - ri, ri
            dm_ref[idx] = jnp.where(mask, jnp.exp(lg * jnp.where(mask, e, 0.0)), 0.0)
            xi_ref[idx] = jnp.exp(lg * xe)
            zt_ref[idx] = jnp.exp(lg * ze)


def _ret_exponents(d):
    ri = lax.broadcasted_iota(jnp.int32, (CHUNK, CHUNK), 0).astype(F32)
    ci = lax.broadcasted_iota(jnp.int32, (CHUNK, CHUNK), 1).astype(F32)
    if d == 0:
        return ri - ci, ri + 1.0, (CHUNK - 1.0) - ri
    return ci - ri - 1.0, (CHUNK - 1.0) - ri, ri


def _bwd_chunk(n, ncc, nc):
    return jnp.where(n < ncc, ncc - 1 - n, nc - 1 - (n - ncc))


def _retention_fwd(name, q, k, v, lgb, lc):
    t, w = q.shape
    nc, ncc = t // CHUNK, lc // CHUNK
    nh = 2 * HEADS

    def body(qf_ref, kf_ref, vf_ref, qb_ref, kb_ref, vb_ref, lgb_ref, of_ref, ob_ref, ss_ref,
             s_ref, dm_ref, xi_ref, zt_ref):
        n = pl.program_id(0)

        @pl.when(n == 0)
        def _():
            s_ref[...] = jnp.zeros_like(s_ref)
            _ret_tables(lgb_ref, dm_ref, xi_ref, zt_ref)

        for d in range(2):
            q_ref, k_ref, v_ref, o_ref = (qf_ref, kf_ref, vf_ref, of_ref) if d == 0 else (qb_ref, kb_ref, vb_ref, ob_ref)
            for h in range(HEADS):
                idx = d * HEADS + h
                sl = slice(h * HEAD_DIM, (h + 1) * HEAD_DIM)
                qv, kv, vv = q_ref[:, sl], k_ref[:, sl], v_ref[:, sl]
                s = s_ref[idx]
                ss_ref[idx] = s
                a = _dot(qv, kv, NT) * dm_ref[idx]
                o = _dot(a, vv, NN) + _dot(qv.astype(F32) * xi_ref[idx], s, NN)
                o_ref[:, sl] = o
                gc = jnp.exp(_log_sigmoid_row(lgb_ref[idx:idx + 1, :]) * float(CHUNK))
                s_ref[idx] = gc * s + _dot(kv.astype(F32) * zt_ref[idx], vv, TN)

    fspec = pl.BlockSpec((CHUNK, w), lambda n: (n, 0))
    bspec = pl.BlockSpec((CHUNK, w), lambda n: (_bwd_chunk(n, ncc, nc), 0))
    tab = pltpu.VMEM((nh, CHUNK, CHUNK), F32)
    return _pcall(body, name=name, grid=(nc,),
                  in_specs=[fspec] * 3 + [bspec] * 3 + [_full((nh, HEAD_DIM))],
                  out_specs=[fspec, bspec, pl.BlockSpec((None, nh, CHUNK, CHUNK), lambda n: (n, 0, 0, 0))],
                  out_shape=[_sds((t, w)), _sds((t, w)), _sds((nc, nh, CHUNK, CHUNK))],
                  scratch_shapes=[tab, tab, tab, tab],
                  compiler_params=_cp(("arbitrary",)))(q, k, v, q, k, v, lgb)


def _retention_bwd(name, q, k, v, do, ss, lgb, lc):
    t, w = q.shape
    nc, ncc = t // CHUNK, lc // CHUNK
    nh = 2 * HEADS

    def body(qf_ref, kf_ref, vf_ref, gf_ref, qb_ref, kb_ref, vb_ref, gb_ref, ss_ref, lgb_ref,
             dqf_ref, dkf_ref, dvf_ref, dqb_ref, dkb_ref, dvb_ref, dl_ref,
             ds_ref, dm_ref, xi_ref, zt_ref, acc_ref):
        n = pl.program_id(0)

        @pl.when(n == 0)
        def _():
            ds_ref[...] = jnp.zeros_like(ds_ref)
            acc_ref[...] = jnp.zeros_like(acc_ref)
            _ret_tables(lgb_ref, dm_ref, xi_ref, zt_ref)

        for d in range(2):
            if d == 0:
                q_ref, k_ref, v_ref, g_ref, dq_ref, dk_ref, dv_ref = qf_ref, kf_ref, vf_ref, gf_ref, dqf_ref, dkf_ref, dvf_ref
            else:
                q_ref, k_ref, v_ref, g_ref, dq_ref, dk_ref, dv_ref = qb_ref, kb_ref, vb_ref, gb_ref, dqb_ref, dkb_ref, dvb_ref
            ee, xe, ze = _ret_exponents(d)
            for h in range(HEADS):
                idx = d * HEADS + h
                sl = slice(h * HEAD_DIM, (h + 1) * HEAD_DIM)
                qv, kv, vv, gv = q_ref[:, sl], k_ref[:, sl], v_ref[:, sl], g_ref[:, sl]
                s = ss_ref[idx]
                dsp = ds_ref[idx]
                dmat, xi, zt = dm_ref[idx], xi_ref[idx], zt_ref[idx]
                qf32, kf32 = qv.astype(F32), kv.astype(F32)
                a = _dot(qv, kv, NT) * dmat
                dar = _dot(gv, vv, NT)
                da = dar * dmat
                t1 = _dot(gv, s, NT)
                t2 = _dot(vv, dsp, NT)
                dq_ref[:, sl] = _dot(da, kv, NN) + xi * t1
                dk_ref[:, sl] = _dot(da, qv, TN) + zt * t2
                dv_ref[:, sl] = _dot(a, gv, TN) + _dot(kf32 * zt, dsp, NN)
                gc = jnp.exp(_log_sigmoid_row(lgb_ref[idx:idx + 1, :]) * float(CHUNK))
                ds_ref[idx] = gc * dsp + _dot(qf32 * xi, gv, TN)
                acc_ref[idx] += (ee * a * dar + xe * xi * qf32 * t1 + ze * zt * kf32 * t2
                                 + (float(CHUNK) * gc) * dsp * s)

        @pl.when(n == nc - 1)
        def _():
            for idx in range(nh):
                tot = jnp.sum(acc_ref[idx])
                dl_ref[idx:idx + 1, :] = tot * _sigmoid(-lgb_ref[idx:idx + 1, :])

    fmap = lambda n: (nc - 1 - n, 0)
    bmap = lambda n: (_bwd_chunk(nc - 1 - n, ncc, nc), 0)
    fspec = pl.BlockSpec((CHUNK, w), fmap)
    bspec = pl.BlockSpec((CHUNK, w), bmap)
    tab = pltpu.VMEM((nh, CHUNK, CHUNK), F32)
    o = _sds((t, w))
    return _pcall(body, name=name, grid=(nc,),
                  in_specs=[fspec] * 4 + [bspec] * 4
                  + [pl.BlockSpec((None, nh, CHUNK, CHUNK), lambda n: (nc - 1 - n, 0, 0, 0)), _full((nh, HEAD_DIM))],
                  out_specs=[fspec] * 3 + [bspec] * 3 + [_full((nh, HEAD_DIM))],
                  out_shape=[o] * 6 + [_sds((nh, HEAD_DIM))],
                  scratch_shapes=[tab, tab, tab, tab, tab],
                  compiler_params=_cp(("arbitrary",)))(q, k, v, do, q, k, v, do, ss, lgb)


def _halo_specs(tm, halo, t, width, col):
    hb = tm // halo
    last = t // halo - 1
    prev = pl.BlockSpec((halo, width), lambda i: (jnp.maximum(i * hb - 1, 0), col))
    nxt = pl.BlockSpec((halo, width), lambda i: (jnp.minimum((i + 1) * hb, last), col))
    return prev, nxt


def _halo_valid(i, nct, nt):
    vp = jnp.logical_and(i != 0, i != nct)
    vn = jnp.logical_and(i != nct - 1, i != nt - 1)
    return vp, vn


def _fill_window(win_ref, prev, cur, nxt, vp, vn, halo, tm):
    win_ref[0:halo, :] = jnp.where(vp, prev, 0.0)
    win_ref[halo:halo + tm, :] = cur
    win_ref[halo + tm:halo + tm + halo, :] = jnp.where(vn, nxt, 0.0)


CONV_SUB = 64


def _conv_taps(win_ref, w_ref, tm, flip):
    outs = []
    for r0 in range(0, tm, CONV_SUB):
        acc = None
        for kk in range(CONV_K):
            wk = (CONV_K - 1 - kk) if flip else kk
            term = w_ref[wk:wk + 1, :] * win_ref[r0 + kk + 1:r0 + kk + 1 + CONV_SUB, :]
            acc = term if acc is None else acc + term
        outs.append(acc)
    return jnp.concatenate(outs, axis=0)


def _head_norm(y):
    r = lax.rsqrt(jnp.mean(y * y, axis=-1, keepdims=True) + EPS)
    return y * r, r


def _ln_stats(y):
    mu = jnp.mean(y, axis=-1, keepdims=True)
    yc = y - mu
    rs = lax.rsqrt(jnp.mean(yc * yc, axis=-1, keepdims=True) + EPS)
    return yc * rs, rs


def _ln_bwd(dyh, yh, rs):
    return rs * (dyh - jnp.mean(dyh, axis=-1, keepdims=True) - yh * jnp.mean(dyh * yh, axis=-1, keepdims=True))


def _even_mix(name, p, of, ob, cw, lnw, lnb, nct):
    t = p.shape[0]
    tm, halo = ROW_TILE, CONV_HALO
    nt = t // tm
    w = HEADS * HEAD_DIM

    def body(g_ref, a_ref, gb_ref, ap_ref, gbp_ref, an_ref, gbn_ref, of_ref, ob_ref, cw_ref, lw_ref, lb_ref,
             mix_ref, yc_ref, win_ref):
        i = pl.program_id(0)
        vp, vn = _halo_valid(i, nct, nt)
        glu = lambda a, b: a * _sigmoid(b)
        _fill_window(win_ref, glu(ap_ref[...], gbp_ref[...]), glu(a_ref[...], gb_ref[...]),
                     glu(an_ref[...], gbn_ref[...]), vp, vn, halo, tm)
        yc = _conv_taps(win_ref, cw_ref, tm, False)
        yc_ref[...] = yc
        yh, _ = _ln_stats(yc)
        mix_ref[:, w:2 * w] = _silu(yh * lw_ref[...] + lb_ref[...]).astype(BF16)
        for h in range(HEADS):
            sl = slice(h * HEAD_DIM, (h + 1) * HEAD_DIM)
            yn, _ = _head_norm(of_ref[:, sl] + ob_ref[:, sl])
            mix_ref[:, sl] = (_silu(g_ref[:, sl]) * yn).astype(BF16)

    col = lambda j: pl.BlockSpec((tm, w), lambda i: (i, j))
    ap, an = _halo_specs(tm, halo, t, w, 4)
    gp, gn = _halo_specs(tm, halo, t, w, 5)
    row = pl.BlockSpec((tm, w), lambda i: (i, 0))
    return _pcall(body, name=name, grid=(nt,),
                  in_specs=[col(3), col(4), col(5), ap, gp, an, gn, row, row,
                            _full(cw.shape), _full(lnw.shape), _full(lnb.shape)],
                  out_specs=[pl.BlockSpec((tm, 2 * w), lambda i: (i, 0)), row],
                  out_shape=[_sds((t, 2 * w), BF16), _sds((t, w))],
                  scratch_shapes=[pltpu.VMEM((tm + 2 * halo, w), F32)],
                  compiler_params=_cp(("parallel",)))(p, p, p, p, p, p, p, of, ob, cw, lnw, lnb)


def _even_mix_bwd1(name, dmix, p, of, ob, yc, lnw, lnb):
    t = p.shape[0]
    tm = ROW_TILE
    w = HEADS * HEAD_DIM

    def body(dr_ref, dc_ref, g_ref, of_ref, ob_ref, yc_ref, lw_ref, lb_ref, do_ref, dg_ref, dyc_ref, s_ref):
        @pl.when(pl.program_id(0) == 0)
        def _():
            s_ref[...] = jnp.zeros_like(s_ref)

        for h in range(HEADS):
            sl = slice(h * HEAD_DIM, (h + 1) * HEAD_DIM)
            yn, r = _head_norm(of_ref[:, sl] + ob_ref[:, sl])
            gv = g_ref[:, sl]
            dr = dr_ref[:, sl]
            dg_ref[:, sl] = (dr * yn * _dsilu(gv)).astype(BF16)
            dyn = dr * _silu(gv)
            do_ref[:, sl] = (r * (dyn - yn * jnp.mean(dyn * yn, axis=-1, keepdims=True))).astype(BF16)
        yh, rs = _ln_stats(yc_ref[...])
        lw = lw_ref[...]
        dlo = dc_ref[...] * _dsilu(yh * lw + lb_ref[...])
        dyc_ref[...] = _ln_bwd(dlo * lw, yh, rs)
        s_ref[...] += jnp.concatenate([_colsum(dlo * yh), _colsum(dlo), jnp.zeros((6, w), F32)], axis=0)

    col = lambda j: pl.BlockSpec((tm, w), lambda i: (i, j))
    row = pl.BlockSpec((tm, w), lambda i: (i, 0))
    return _pcall(body, name=name, grid=(t // tm,),
                  in_specs=[col(0), col(1), col(3), row, row, row, _full(lnw.shape), _full(lnb.shape)],
                  out_specs=[row, row, row, _full((8, w))],
                  out_shape=[_sds((t, w), BF16), _sds((t, w), BF16), _sds((t, w)), _sds((8, w))],
                  compiler_params=_cp(("arbitrary",)))(dmix, dmix, p, of, ob, yc, lnw, lnb)


def _even_conv_bwd(name, dyc, p, cw, nct):
    t = p.shape[0]
    tm, halo = ROW_TILE, CONV_HALO
    nt = t // tm
    w = HEADS * HEAD_DIM

    def body(d_ref, dp_ref, dn_ref, a_ref, gb_ref, ap_ref, gbp_ref, an_ref, gbn_ref, cw_ref,
             da_ref, dgb_ref, dw_ref, dwin_ref, uwin_ref):
        i = pl.program_id(0)

        @pl.when(i == 0)
        def _():
            dw_ref[...] = jnp.zeros_like(dw_ref)

        vp, vn = _halo_valid(i, nct, nt)
        glu = lambda a, b: a * _sigmoid(b)
        dcur = d_ref[...]
        _fill_window(dwin_ref, dp_ref[...], dcur, dn_ref[...], vp, vn, halo, tm)
        _fill_window(uwin_ref, glu(ap_ref[...], gbp_ref[...]), glu(a_ref[...], gb_ref[...]),
                     glu(an_ref[...], gbn_ref[...]), vp, vn, halo, tm)
        du = _conv_taps(dwin_ref, cw_ref, tm, True)
        av = a_ref[...]
        sg = _sigmoid(gb_ref[...])
        da_ref[...] = (du * sg).astype(BF16)
        dgb_ref[...] = (du * av * sg * (1.0 - sg)).astype(BF16)
        rows = [_colsum(dcur * uwin_ref[kk + 1:kk + 1 + tm, :]) for kk in range(CONV_K)]
        dw_ref[...] += jnp.concatenate(rows + [jnp.zeros((1, w), F32)], axis=0)

    col = lambda j: pl.BlockSpec((tm, w), lambda i: (i, j))
    row = pl.BlockSpec((tm, w), lambda i: (i, 0))
    dp, dn = _halo_specs(tm, halo, t, w, 0)
    ap, an = _halo_specs(tm, halo, t, w, 4)
    gp, gn = _halo_specs(tm, halo, t, w, 5)
    win = pltpu.VMEM((tm + 2 * halo, w), F32)
    return _pcall(body, name=name, grid=(nt,),
                  in_specs=[row, dp, dn, col(4), col(5), ap, gp, an, gn, _full(cw.shape)],
                  out_specs=[row, row, _full((CONV_K + 1, w))],
                  out_shape=[_sds((t, w), BF16), _sds((t, w), BF16), _sds((CONV_K + 1, w))],
                  scratch_shapes=[win, win],
                  compiler_params=_cp(("arbitrary",)))(dyc, dyc, dyc, p, p, p, p, p, p, cw)


def _even_dp(name, dqs, dks, dvs, dg, da, dgb, cs, sn):
    t, w = dg.shape
    tm = ROW_TILE
    scale = HEAD_DIM ** -0.5

    def body(dqf_ref, dqb_ref, dkf_ref, dkb_ref, dvf_ref, dvb_ref, dg_ref, da_ref, dgb_ref, cs_ref, sn_ref, dp_ref):
        c, s = cs_ref[...], sn_ref[...]
        for h in range(HEADS):
            sl = slice(h * HEAD_DIM, (h + 1) * HEAD_DIM)
            dp_ref[:, sl] = (_rope_t(dqf_ref[:, sl] + dqb_ref[:, sl], c, s) * scale).astype(BF16)
            dp_ref[:, w + h * HEAD_DIM:w + (h + 1) * HEAD_DIM] = _rope_t(dkf_ref[:, sl] + dkb_ref[:, sl], c, s).astype(BF16)
        dp_ref[:, 2 * w:3 * w] = (dvf_ref[...] + dvb_ref[...]).astype(BF16)
        dp_ref[:, 3 * w:4 * w] = dg_ref[...]
        dp_ref[:, 4 * w:5 * w] = da_ref[...]
        dp_ref[:, 5 * w:6 * w] = dgb_ref[...]

    row = pl.BlockSpec((tm, w), lambda i: (i, 0))
    tab = pl.BlockSpec((tm, HEAD_DIM), lambda i: (i, 0))
    return _pcall(body, name=name, grid=(t // tm,), in_specs=[row] * 9 + [tab, tab],
                  out_specs=pl.BlockSpec((tm, 6 * w), lambda i: (i, 0)), out_shape=_sds((t, 6 * w), BF16),
                  compiler_params=_cp(("parallel",)))(dqs[0], dqs[1], dks[0], dks[1], dvs[0], dvs[1], dg, da, dgb, cs, sn)


GROUPS = 4
GC = 128
INV_SQRT2 = 0.7071067811865476
INV_SQRT_2PI = 0.3989422804014327


def _gelu(x):
    return 0.5 * x * (1.0 + lax.erf(x * INV_SQRT2))


def _dgelu(x):
    return 0.5 * (1.0 + lax.erf(x * INV_SQRT2)) + x * jnp.exp(-0.5 * x * x) * INV_SQRT_2PI


def _pool_count(i, nct, lc, t, tm, rows, row0, left, right):
    is_ctx = i < nct
    seg_start = jnp.where(is_ctx, 0, lc)
    seg_len = jnp.where(is_ctx, lc, t - lc)
    pos = i * tm + row0 - seg_start + lax.broadcasted_iota(jnp.int32, (rows, GC), 0)
    cnt = jnp.minimum(pos + right, seg_len - 1) - jnp.maximum(pos - left, 0) + 1
    return jnp.maximum(cnt, 1).astype(F32)


def _spatial_gate(vln, sgw_ref, sgb_ref, tm):
    cols = []
    for g in range(GROUPS):
        sl = slice(g * GC, (g + 1) * GC)
        parts = [_dot(sgw_ref[g], vln[r0:r0 + CHUNK, sl], NN) + sgb_ref[g] for r0 in range(0, tm, CHUNK)]
        cols.append(jnp.concatenate(parts, axis=0))
    return jnp.concatenate(cols, axis=1)


def _odd_mix(name, p, pw, pscale, lnw, lnb, sgw, sgb, nct, lc):
    t = p.shape[0]
    tm, halo = ROW_TILE, POOL_HALO
    nt = t // tm
    w = GROUPS * GC

    def body(pc_ref, pp_ref, pn_ref, pu_ref, pv_ref, pw_ref, ps_ref, lw_ref, lb_ref, sgw_ref, sgb_ref,
             mix_ref, m_ref, win_ref):
        i = pl.program_id(0)
        vp, vn = _halo_valid(i, nct, nt)
        pc = pc_ref[...]
        _fill_window(win_ref, pp_ref[...], pc, pn_ref[...], vp, vn, halo, tm)
        for g, wd in enumerate(POOL_WINDOWS):
            sl = slice(g * GC, (g + 1) * GC)
            left = wd // 2
            right = wd - 1 - left
            s = None
            for o in range(-left, right + 1):
                term = win_ref[halo + o:halo + o + tm, sl]
                s = term if s is None else s + term
            mg = s / _pool_count(i, nct, lc, t, tm, tm, 0, left, right) - pc[:, sl]
            m_ref[:, sl] = mg
            mix_ref[:, sl] = (_dot(mg, pw_ref[g], NN) * ps_ref[:, sl]).astype(BF16)
        u = _gelu(pu_ref[...])
        vh, _ = _ln_stats(_gelu(pv_ref[...]))
        s = _spatial_gate(vh * lw_ref[...] + lb_ref[...], sgw_ref, sgb_ref, tm)
        mix_ref[:, w:2 * w] = (u * s).astype(BF16)

    col = lambda j: pl.BlockSpec((tm, w), lambda i: (i, j))
    pp, pn = _halo_specs(tm, halo, t, w, 0)
    return _pcall(body, name=name, grid=(nt,),
                  in_specs=[col(0), pp, pn, col(1), col(2), _full(pw.shape), _full(pscale.shape),
                            _full(lnw.shape), _full(lnb.shape), _full(sgw.shape), _full(sgb.shape)],
                  out_specs=[pl.BlockSpec((tm, 2 * w), lambda i: (i, 0)), col(0)],
                  out_shape=[_sds((t, 2 * w), BF16), _sds((t, w))],
                  scratch_shapes=[pltpu.VMEM((tm + 2 * halo, w), F32)],
                  compiler_params=_cp(("parallel",)))(p, p, p, p, p, pw, pscale, lnw, lnb, sgw, sgb)


def _odd_mix_bwd1(name, dmix, p, m, pw, pscale, lnw, lnb, sgw, sgb):
    t = p.shape[0]
    tm = ROW_TILE
    w = GROUPS * GC

    def body(dpo_ref, dsg_ref, pu_ref, pv_ref, m_ref, pw_ref, ps_ref, lw_ref, lb_ref, sgw_ref, sgb_ref,
             dm_ref, dpd_ref, vec_ref, dpw_ref, dsgw_ref, dsgb_ref):
        @pl.when(pl.program_id(0) == 0)
        def _():
            vec_ref[...] = jnp.zeros_like(vec_ref)
            dpw_ref[...] = jnp.zeros_like(dpw_ref)
            dsgw_ref[...] = jnp.zeros_like(dsgw_ref)
            dsgb_ref[...] = jnp.zeros_like(dsgb_ref)

        dscale = []
        for g in range(GROUPS):
            sl = slice(g * GC, (g + 1) * GC)
            mg = m_ref[:, sl]
            dpo = dpo_ref[:, sl]
            dscale.append(_colsum(dpo * _dot(mg, pw_ref[g], NN)))
            dpo = dpo * ps_ref[:, sl]
            dm_ref[:, sl] = _dot(dpo, pw_ref[g], NT)
            dpw_ref[g] += _dot(mg, dpo, TN)
        pu, pv = pu_ref[...], pv_ref[...]
        u = _gelu(pu)
        vh, rs = _ln_stats(_gelu(pv))
        lw = lw_ref[...]
        vln = vh * lw + lb_ref[...]
        s = _spatial_gate(vln, sgw_ref, sgb_ref, tm)
        dsg = dsg_ref[...]
        dpd_ref[:, 0:w] = (dsg * s * _dgelu(pu)).astype(BF16)
        ds = dsg * u
        cols = []
        for g in range(GROUPS):
            sl = slice(g * GC, (g + 1) * GC)
            parts = []
            for r0 in range(0, tm, CHUNK):
                dsc = ds[r0:r0 + CHUNK, sl]
                parts.append(_dot(sgw_ref[g], dsc, TN))
                dsgw_ref[g] += _dot(dsc, vln[r0:r0 + CHUNK, sl], NT)
                dsgb_ref[g] += dsc
            cols.append(jnp.concatenate(parts, axis=0))
        dvln = jnp.concatenate(cols, axis=1)
        dpd_ref[:, w:2 * w] = (_ln_bwd(dvln * lw, vh, rs) * _dgelu(pv)).astype(BF16)
        vec_ref[...] += jnp.concatenate([jnp.concatenate(dscale, axis=1), _colsum(dvln * vh), _colsum(dvln),
                                         jnp.zeros((5, w), F32)], axis=0)

        @pl.when(pl.program_id(0) == t // tm - 1)
        def _():
            for g in range(GROUPS):
                dsgb_ref[g] = jnp.broadcast_to(jnp.sum(dsgb_ref[g], axis=1, keepdims=True), (GC, GC))

    col = lambda j: pl.BlockSpec((tm, w), lambda i: (i, j))
    mat = _full((GROUPS, GC, GC))
    return _pcall(body, name=name, grid=(t // tm,),
                  in_specs=[col(0), col(1), col(1), col(2), col(0), _full(pw.shape), _full(pscale.shape),
                            _full(lnw.shape), _full(lnb.shape), _full(sgw.shape), _full(sgb.shape)],
                  out_specs=[col(0), pl.BlockSpec((tm, 2 * w), lambda i: (i, 0)), _full((8, w)), mat, mat, mat],
                  out_shape=[_sds((t, w)), _sds((t, 2 * w), BF16), _sds((8, w)),
                             _sds((GROUPS, GC, GC)), _sds((GROUPS, GC, GC)), _sds((GROUPS, GC, GC))],
                  compiler_params=_cp(("arbitrary",)))(dmix, dmix, p, p, m, pw, pscale, lnw, lnb, sgw, sgb)


def _odd_dp(name, dm, dpd, nct, lc):
    t, w = dm.shape
    tm, halo = ROW_TILE, POOL_HALO
    nt = t // tm

    def body(d_ref, dp_ref, dn_ref, dpd_ref, o_ref, win_ref):
        i = pl.program_id(0)
        vp, vn = _halo_valid(i, nct, nt)
        dcur = d_ref[...]
        _fill_window(win_ref, dp_ref[...], dcur, dn_ref[...], vp, vn, halo, tm)
        for g, wd in enumerate(POOL_WINDOWS):
            sl = slice(g * GC, (g + 1) * GC)
            left = wd // 2
            right = wd - 1 - left
            win_ref[:, sl] = win_ref[:, sl] / _pool_count(i, nct, lc, t, tm, tm + 2 * halo, -halo, left, right)
            s = None
            for o in range(-right, left + 1):
                term = win_ref[halo + o:halo + o + tm, sl]
                s = term if s is None else s + term
            o_ref[:, sl] = (s - dcur[:, sl]).astype(BF16)
        o_ref[:, w:3 * w] = dpd_ref[...]

    row = pl.BlockSpec((tm, w), lambda i: (i, 0))
    pp, pn = _halo_specs(tm, halo, t, w, 0)
    return _pcall(body, name=name, grid=(nt,),
                  in_specs=[row, pp, pn, pl.BlockSpec((tm, 2 * w), lambda i: (i, 0))],
                  out_specs=pl.BlockSpec((tm, 3 * w), lambda i: (i, 0)), out_shape=_sds((t, 3 * w), BF16),
                  scratch_shapes=[pltpu.VMEM((tm + 2 * halo, w), F32)],
                  compiler_params=_cp(("parallel",)))(dm, dm, dm, dpd)


def _place():
    x, y, c = lax.axis_index("x"), lax.axis_index("y"), lax.axis_index("c")
    chips = [(1 - x, y), (x, 1 - y), (1 - x, 1 - y)]
    return x, y, c, chips


def _chip_index(cx, cy):
    return 2 * cx + cy


def _all_gather8(name, blk):
    m_per, n = blk.shape

    def body(x_ref, out_ref, send_sems, recv_sems, local_sem):
        x, y, c, chips = _place()
        me, sibling = (x, y, c), (x, y, 1 - c)

        def rows(px, py, pc):
            return out_ref.at[pl.ds((4 * px + 2 * py + pc) * m_per, m_per), :]

        def copy(k, block, to, src=None):
            return pltpu.make_async_remote_copy(
                src_ref=rows(*block) if src is None else src, dst_ref=rows(*block),
                send_sem=send_sems.at[k], recv_sem=recv_sems.at[k], device_id=to, device_id_type=MESH)

        mine = pltpu.make_async_copy(x_ref, rows(*me), local_sem)
        mine.start()
        first = [copy(0, me, sibling, src=x_ref)]
        first += [copy(1 + j, me, (*chip, c), src=x_ref) for j, chip in enumerate(chips)]
        for cp in first:
            cp.start()
        passed = [copy(4 + j, (*chip, c), sibling) for j, chip in enumerate(chips)]
        for j, chip in enumerate(chips):
            copy(1 + j, (*chip, c), me).wait_recv()
            passed[j].start()
        copy(0, sibling, me).wait_recv()
        for j, chip in enumerate(chips):
            copy(4 + j, (*chip, 1 - c), me).wait_recv()
        for cp in first + passed:
            cp.wait_send()
        mine.wait()

    return _pcall(body, name=name, out_shape=_sds((8 * m_per, n), blk.dtype),
                  in_specs=[pl.BlockSpec(memory_space=pltpu.VMEM)], out_specs=pl.BlockSpec(memory_space=pltpu.VMEM),
                  scratch_shapes=[pltpu.SemaphoreType.DMA((7,)), pltpu.SemaphoreType.DMA((7,)), pltpu.SemaphoreType.DMA],
                  compiler_params=_cp())(blk)


ANY = pl.BlockSpec(memory_space=pl.ANY)


def _half(which, rows):
    return pl.ds(pl.multiple_of(which * rows, 16), rows)


def _gather_weights(name, ws, after=()):
    nw = len(ws)
    na = len(after)
    ns = 7

    def body(*refs):
        w_refs, o_refs = refs[:nw], refs[nw + na:2 * nw + na]
        send_sems, recv_sems = refs[2 * nw + na:]
        x, y, c, chips = _place()
        me_chip = _chip_index(x, y)
        sibling = (x, y, 1 - c)

        def rcopy(t, k, src, dst, to):
            return pltpu.make_async_remote_copy(src_ref=src, dst_ref=dst, send_sem=send_sems.at[t * ns + k],
                                                recv_sem=recv_sems.at[t * ns + k], device_id=to, device_id_type=MESH)

        sends = []
        for t in range(nw):
            lh = w_refs[t].shape[0] // 2
            for k, chip in enumerate(chips):
                sends.append(rcopy(t, k, w_refs[t].at[_half(c, lh)], o_refs[t].at[me_chip, _half(c, lh)], (*chip, c)))
                sends[-1].start()
            sends.append(rcopy(t, 6, w_refs[t], o_refs[t].at[me_chip], sibling))
            sends[-1].start()
        for t in range(nw):
            lh = w_refs[t].shape[0] // 2
            for k, chip in enumerate(chips):
                part = o_refs[t].at[_chip_index(*chip), _half(c, lh)]
                rcopy(t, k, part, part, (*chip, c)).wait_recv()
                sends.append(rcopy(t, 3 + k, part, part, sibling))
                sends[-1].start()
        for t in range(nw):
            lh = w_refs[t].shape[0] // 2
            own = o_refs[t].at[me_chip]
            rcopy(t, 6, own, own, sibling).wait_recv()
            for k, chip in enumerate(chips):
                part = o_refs[t].at[_chip_index(*chip), _half(1 - c, lh)]
                rcopy(t, 3 + k, part, part, sibling).wait_recv()
        for cp in sends:
            cp.wait_send()

    return _pcall(body, name=name, out_shape=[_sds((4,) + w.shape, w.dtype) for w in ws],
                  in_specs=[ANY] * (nw + na), out_specs=[ANY] * nw,
                  scratch_shapes=[pltpu.SemaphoreType.DMA((ns * nw,)), pltpu.SemaphoreType.DMA((ns * nw,))],
                  compiler_params=_cp())(*ws, *after)


def _rs_pair(name, gs):
    ng = len(gs)

    def body(*refs):
        g_refs, o_refs = refs[:ng], refs[ng:2 * ng]
        send_sems, recv_sems = refs[2 * ng:]
        x, y, c, _ = _place()
        cps = []
        for t in range(ng):
            lh = g_refs[t].shape[1] // 2
            cp = pltpu.make_async_remote_copy(
                src_ref=g_refs[t].at[:, _half(1 - c, lh)], dst_ref=o_refs[t],
                send_sem=send_sems.at[t], recv_sem=recv_sems.at[t], device_id=(x, y, 1 - c), device_id_type=MESH)
            cp.start()
            cps.append(cp)
        for cp in cps:
            cp.wait()

    outs = [_sds((g.shape[0], g.shape[1] // 2) + g.shape[2:], g.dtype) for g in gs]
    return _pcall(body, name=name, out_shape=outs, in_specs=[ANY] * ng, out_specs=[ANY] * ng,
                  scratch_shapes=[pltpu.SemaphoreType.DMA((ng,)), pltpu.SemaphoreType.DMA((ng,))],
                  compiler_params=_cp())(*gs)


def _rs_chips(name, ps):
    ng = len(ps)

    def body(*refs):
        p_refs, o_refs = refs[:ng], refs[ng:2 * ng]
        send_sems, recv_sems = refs[2 * ng:]
        x, y, c, chips = _place()
        cps = []
        for t in range(ng):
            for k, chip in enumerate(chips):
                cp = pltpu.make_async_remote_copy(
                    src_ref=p_refs[t].at[_chip_index(*chip)], dst_ref=o_refs[t].at[k],
                    send_sem=send_sems.at[t * 3 + k], recv_sem=recv_sems.at[t * 3 + k],
                    device_id=(*chip, c), device_id_type=MESH)
                cp.start()
                cps.append(cp)
        for cp in cps:
            cp.wait()

    return _pcall(body, name=name, out_shape=[_sds((3,) + p.shape[1:], p.dtype) for p in ps],
                  in_specs=[ANY] * ng, out_specs=[ANY] * ng,
                  scratch_shapes=[pltpu.SemaphoreType.DMA((3 * ng,)), pltpu.SemaphoreType.DMA((3 * ng,))],
                  compiler_params=_cp())(*ps)


def _rs_share(name, ss):
    ng = len(ss)

    def body(*refs):
        o_refs = refs[ng:2 * ng]
        send_sems, recv_sems = refs[2 * ng:]
        x, y, c, _ = _place()
        cps = []
        for t in range(ng):
            lh = o_refs[t].shape[1] // 2
            mine = o_refs[t].at[:, _half(c, lh)]
            cp = pltpu.make_async_remote_copy(
                src_ref=mine, dst_ref=mine, send_sem=send_sems.at[t], recv_sem=recv_sems.at[t],
                device_id=(x, y, 1 - c), device_id_type=MESH)
            cp.start()
            cps.append(cp)
        for t in range(ng):
            lh = o_refs[t].shape[1] // 2
            cps[t].wait_send()
            theirs = o_refs[t].at[:, _half(1 - c, lh)]
            pltpu.make_async_remote_copy(
                src_ref=theirs, dst_ref=theirs, send_sem=send_sems.at[t], recv_sem=recv_sems.at[t],
                device_id=(x, y, 1 - c), device_id_type=MESH).wait_recv()

    return _pcall(body, name=name, out_shape=[_sds(s.shape, s.dtype) for s in ss],
                  in_specs=[ANY] * ng, out_specs=[ANY] * ng, input_output_aliases={t: t for t in range(ng)},
                  scratch_shapes=[pltpu.SemaphoreType.DMA((ng,)), pltpu.SemaphoreType.DMA((ng,))],
                  compiler_params=_cp())(*ss)


HBM = pl.BlockSpec(memory_space=pltpu.HBM)
SEMS = pl.BlockSpec(memory_space=pltpu.SEMAPHORE)
EFFECT = pltpu.SideEffectType.DATAFLOW_SIDE_EFFECTING
TOKEN = (8, 128)


def _in_hbm(a):
    return pltpu.with_memory_space_constraint(a, pltpu.HBM)


def _split_start(name, srcs, lands, copies, after):
    ns, nl, na = len(srcs), len(lands), len(after)
    ncopies = len(copies([s for s in srcs], [l for l in lands], probe=True))

    def body(*refs):
        src_refs, land_refs = refs[:ns], refs[ns:ns + nl]
        send_sems, recv_sems = refs[ns + nl + na], refs[ns + nl + na + 1]
        token = refs[-1]
        for k, (src, dst, to) in enumerate(copies(src_refs, land_refs)):
            pltpu.make_async_remote_copy(src_ref=src, dst_ref=dst, send_sem=send_sems.at[k], recv_sem=recv_sems.at[k],
                                         device_id=to, device_id_type=MESH).start()
        token[...] = jnp.zeros_like(token)

    thru = [pltpu.HBM(a.shape, a.dtype) for a in list(srcs) + list(lands)]
    outs = _pcall(body, name=name,
                  out_shape=(pltpu.SemaphoreType.DMA((ncopies,)), pltpu.SemaphoreType.DMA((ncopies,)), *thru, _sds(TOKEN)),
                  in_specs=[HBM] * (ns + nl) + [ANY] * na,
                  out_specs=(SEMS, SEMS, *([HBM] * (ns + nl)), pl.BlockSpec(memory_space=pltpu.VMEM)),
                  input_output_aliases={t: 2 + t for t in range(ns + nl)},
                  compiler_params=pltpu.CompilerParams(has_side_effects=EFFECT))(
        *[_in_hbm(a) for a in list(srcs) + list(lands)], *after)
    return outs[0], outs[1], list(outs[2:2 + ns]), list(outs[2 + ns:2 + ns + nl]), outs[-1]


def _split_wait(name, started, copies, after):
    send_sems, recv_sems, srcs, lands, _ = started
    ns, nl, na = len(srcs), len(lands), len(after)

    def body(*refs):
        src_refs, land_refs = refs[:ns], refs[ns:ns + nl]
        send_sems_ref, recv_sems_ref = refs[ns + nl], refs[ns + nl + 1]
        for k, (src, dst, to) in enumerate(copies(src_refs, land_refs)):
            cp = pltpu.make_async_remote_copy(src_ref=src, dst_ref=dst, send_sem=send_sems_ref.at[k],
                                              recv_sem=recv_sems_ref.at[k], device_id=to, device_id_type=MESH)
            cp.wait_send()
            cp.wait_recv()

    thru = [pltpu.HBM(a.shape, a.dtype) for a in list(srcs) + list(lands)]
    outs = _pcall(body, name=name, out_shape=tuple(thru),
                  in_specs=[HBM] * (ns + nl) + [SEMS, SEMS] + [ANY] * na, out_specs=tuple([HBM] * (ns + nl)),
                  input_output_aliases={t: t for t in range(ns + nl)},
                  compiler_params=pltpu.CompilerParams(has_side_effects=EFFECT))(
        *srcs, *lands, send_sems, recv_sems, *after)
    return list(outs[ns:])


def _gather_copies(n):
    def copies(src_refs, land_refs, probe=False):
        if probe:
            return [None] * (4 * n)
        x, y, c, chips = _place()
        me_chip = _chip_index(x, y)
        out = []
        for t in range(n):
            for to in [(*chip, c) for chip in chips] + [(x, y, 1 - c)]:
                out.append((src_refs[t], land_refs[t].at[me_chip], to))
        return out
    return copies


def _scatter_copies(n):
    def copies(src_refs, land_refs, probe=False):
        if probe:
            return [None] * (3 * n)
        x, y, c, chips = _place()
        out = []
        for t in range(n):
            for k, chip in enumerate(chips):
                out.append((src_refs[t].at[_chip_index(*chip)], land_refs[t].at[k], (*chip, c)))
        return out
    return copies


def _row_block(r, cn):
    if r % 8:
        return r
    best = 8
    for d in range(8, r + 1, 8):
        if r % d == 0 and d * cn * 4 <= (1 << 20):
            best = d
    return best


def _add_half(name, g, a, idx):
    j, rh, cn = a.shape
    tr = _row_block(rh, cn)
    nb = rh // tr

    def body(i_ref, g_ref, a_ref, o_ref):
        o_ref[...] = (g_ref[...] + a_ref[...]).astype(BF16)

    blk = (None, tr, cn)
    gs = pltpu.PrefetchScalarGridSpec(
        num_scalar_prefetch=1, grid=(j, nb),
        in_specs=[pl.BlockSpec(blk, lambda jj, i, i_ref: (jj, i_ref[0] * nb + i, 0)),
                  pl.BlockSpec(blk, lambda jj, i, i_ref: (jj, i, 0))],
        out_specs=pl.BlockSpec(blk, lambda jj, i, i_ref: (jj, i, 0)))
    return _pcall(body, name=name, grid_spec=gs, out_shape=_sds(a.shape, BF16),
                  compiler_params=_cp(("parallel", "parallel")))(idx, g, a)


def _sum_final(name, g, a, b, idx, buf, lyr, nlyr):
    _, r, cn = g.shape
    rh = r // 2
    tr = _row_block(rh, cn)
    nb = rh // tr

    def body(*refs):
        g_ref, a_ref, b_ref = refs[1:4]
        o_ref = refs[-1]
        own = g_ref[...] + a_ref[...]
        o_ref[...] = (own + b_ref[0].astype(F32)) + (b_ref[1].astype(F32) + b_ref[2].astype(F32))

    blk = (None, tr, cn)
    in_specs = [pl.BlockSpec(blk, lambda i, i_ref: (i_ref[1], i_ref[0] * nb + i, 0)),
                pl.BlockSpec(blk, lambda i, i_ref: (i_ref[1], i, 0)),
                pl.BlockSpec((3, tr, cn), lambda i, i_ref: (0, i, 0))]
    args = [idx, g, a, b]
    kw = {}
    if buf is not None:
        in_specs.append(ANY)
        args.append(buf)
        kw["input_output_aliases"] = {4: 0}
    gs = pltpu.PrefetchScalarGridSpec(
        num_scalar_prefetch=1, grid=(nb,), in_specs=in_specs,
        out_specs=pl.BlockSpec(blk, lambda i, i_ref: (lyr, i_ref[0] * nb + i, 0)))
    return _pcall(body, name=name, grid_spec=gs, out_shape=_sds((nlyr, r, cn)),
                  compiler_params=_cp(("parallel",)), **kw)(*args)


def _sum8(name, g):
    _, r, n = g.shape
    tr = 8

    def body(g_ref, o_ref):
        o_ref[...] = ((g_ref[0] + g_ref[1]) + (g_ref[2] + g_ref[3])) + ((g_ref[4] + g_ref[5]) + (g_ref[6] + g_ref[7]))

    return _pcall(body, name=name, grid=(r // tr,), in_specs=[pl.BlockSpec((8, tr, n), lambda i: (0, i, 0))],
                  out_specs=pl.BlockSpec((tr, n), lambda i: (i, 0)), out_shape=_sds((r, n)),
                  compiler_params=_cp(("parallel",)))(g)


def _ada_mod(name, c16, ada_w, bias):
    nl, dm, n = ada_w.shape

    def body(c_ref, w_ref, b_ref, o_ref):
        o_ref[...] = _dot(_silu(c_ref[...]), w_ref[...], NN) + b_ref[...]

    return _pcall(body, name=name, grid=(nl,),
                  in_specs=[_full(c16.shape), pl.BlockSpec((None, dm, n), lambda i: (i, 0, 0)),
                            pl.BlockSpec((None, 1, n), lambda i: (i, 0, 0))],
                  out_specs=pl.BlockSpec((None, 16, n), lambda i: (i, 0, 0)), out_shape=_sds((nl, 16, n)),
                  compiler_params=_cp(("parallel",)))(c16, ada_w, bias)


def _ada_bwd(name, c16, dmod, ada_w):
    nl, dm, n = ada_w.shape

    def body(c_ref, d_ref, w_ref, gw_ref, dc_ref):
        @pl.when(pl.program_id(0) == 0)
        def _():
            dc_ref[...] = jnp.zeros_like(dc_ref)

        dv = d_ref[...]
        gw_ref[...] = _dot(_silu(c_ref[...]), dv, TN)
        dc_ref[...] += _dot(dv, w_ref[...], NT)

    return _pcall(body, name=name, grid=(nl,),
                  in_specs=[_full(c16.shape), pl.BlockSpec((None, 16, n), lambda i: (i, 0, 0)),
                            pl.BlockSpec((None, dm, n), lambda i: (i, 0, 0))],
                  out_specs=[pl.BlockSpec((None, dm, n), lambda i: (i, 0, 0)), _full((16, dm))],
                  out_shape=[_sds((nl, dm, n)), _sds((16, dm))],
                  compiler_params=_cp(("arbitrary",)))(c16, dmod, ada_w)


def _rowsum16(name, dmod):
    nl, _, n = dmod.shape

    def body(d_ref, o_ref):
        o_ref[...] = _colsum(d_ref[...])

    return _pcall(body, name=name, grid=(nl,), in_specs=[pl.BlockSpec((None, 16, n), lambda i: (i, 0, 0))],
                  out_specs=pl.BlockSpec((None, 1, n), lambda i: (i, 0, 0)), out_shape=_sds((nl, 1, n)),
                  compiler_params=_cp(("parallel",)))(dmod)


def _cctx_grad(name, parts, c_ctx):
    def body(p_ref, c_ref, o_ref):
        tot = (p_ref[0:1, :] + p_ref[1:2, :]) + (p_ref[2:3, :] + p_ref[3:4, :])
        o_ref[...] = tot * _dsilu(c_ref[...])

    return _pcall(body, name=name, out_shape=_sds(c_ctx.shape), compiler_params=_cp())(parts, c_ctx)


def _adamw(name, w, g, m, v):
    shape = w.shape
    cn = shape[-1]
    r = math.prod(shape[:-1]) if len(shape) > 1 else 1
    tr = _row_block(r, cn)
    c1 = 1.0 - ADAM_B1 ** ADAM_STEP
    c2 = 1.0 - ADAM_B2 ** ADAM_STEP

    def body(w_ref, g_ref, m_ref, v_ref, d_ref, mo_ref, vo_ref):
        gv = g_ref[...]
        mn = ADAM_B1 * m_ref[...] + (1.0 - ADAM_B1) * gv
        vn = ADAM_B2 * v_ref[...] + (1.0 - ADAM_B2) * (gv * gv)
        d_ref[...] = -ADAM_LR * ((mn / c1) / (jnp.sqrt(vn / c2) + ADAM_EPS) + ADAM_WD * w_ref[...])
        mo_ref[...] = mn
        vo_ref[...] = vn

    blk = pl.BlockSpec((tr, cn), lambda i: (i, 0))
    o = _sds((r, cn))
    outs = _pcall(body, name=name, grid=(r // tr,), in_specs=[blk] * 4, out_specs=[blk] * 3, out_shape=[o, o, o],
                  compiler_params=_cp(("parallel",)))(*[a.reshape(r, cn) for a in (w, g, m, v)])
    return tuple(a.reshape(shape) for a in outs)


def _local_step(xs, target, modt, nw, fnw, get_w, put_g, ev, od, lc):
    t, dm = xs.shape
    nct = lc // ROW_TILE
    depth = nw.shape[0]
    cs, sn = _rope_tables(t, lc)
    saved = []
    x_in, x1p, fp = xs, None, None
    for i in range(depth):
        j, even = i // 2, i % 2 == 0
        tag = f"l{i}"
        w, deps = get_w(i, [fp] if i else [])
        if i == 0:
            _, h = _rnm(tag + "_norm1", x_in, None, None, 0, modt[0], 0, 1, nw[0, 0], nct, deps)
        else:
            x_in, h = _rnm(tag + "_norm1", x1p, fp, modt[i - 1], 5, modt[i], 0, 1, nw[i, 0], nct, deps)
        s = dict(x=x_in, h=h, w=w)
        if even:
            p = _mm_cols(tag + "_in", h, w["in"])
            q, k, v = _even_qkv(tag + "_qkv", p, cs, sn)
            of, ob, ss = _retention_fwd(tag + "_ret", q, k, v, ev["lgb"][j], lc)
            mix, yc = _even_mix(tag + "_mix", p, of, ob, ev["cw"][j], ev["lnw"][j], ev["lnb"][j], nct)
            y = _mm_full(tag + "_out", mix, w["out"], NN)
            s.update(p=p, q=q, k=k, v=v, of=of, ob=ob, ss=ss, yc=yc)
        else:
            p = _mm_cols(tag + "_in", h, w["in"])
            mix, m = _odd_mix(tag + "_mix", p, od["pw"][j], od["ps"][j], od["lnw"][j], od["lnb"][j],
                              od["sgw"][j], od["sgb"][j], nct, lc)
            y = _mm_full(tag + "_out", mix, w["out"], NN)
            s.update(p=p, m=m)
        x1, h2 = _rnm(tag + "_norm2", x_in, y, modt[i], 2, modt[i], 3, 4, nw[i, 1], nct)
        a, gt, up = _ffn_up(tag + "_ffn_up", h2, w["gate"], w["up"])
        f = _mm_full(tag + "_ffn_down", a, w["down"], NN)
        s.update(mix=mix, y=y, x1=x1, h2=h2, a=a, gt=gt, up=up, f=f)
        saved.append(s)
        x1p, fp = x1, f

    loss_blk, dx, df, fin_s = _fin("final", x1p, fp, modt[depth - 1], 5, fnw, target, nct)

    deps = []
    dmod = [[None] * 6 for _ in range(depth)]
    dnw = [[None, None] for _ in range(depth)]
    zero2 = jnp.zeros((2, dm), F32)
    dmod[depth - 1][5] = jnp.stack([zero2[0], fin_s[0]])
    small = dict(dfnw=fin_s[1], ev=[], od=[])
    for i in reversed(range(depth)):
        j, even = i // 2, i % 2 == 0
        tag = f"l{i}b"
        s = saved[i]
        w = s["w"]
        fh = w["down"].shape[0] // 2
        g = {}
        dgt, dup = _ffn_down_bwd(tag + "_ffn_down", df, w["down"], s["gt"], s["up"])
        g["down"] = _wgrad_rows(tag + "_gdown", s["a"], fh, df)
        g["gate"] = _wgrad_rows(tag + "_ggate", dgt, fh, s["h2"])
        g["up"] = _wgrad_rows(tag + "_gup", dup, fh, s["h2"])
        dh2 = _ffn_in_bwd(tag + "_ffn_in", dgt, dup, w["gate"], w["up"])
        dx1, dy, s2 = _bnm(tag + "_norm2", s["x1"], dh2, dx, s["y"], modt[i], 3, 4, modt[i], 2, nw[i, 1], nct)
        dmod[i][3], dmod[i][4], dmod[i][2] = s2[:, 0], s2[:, 1], s2[:, 2]
        dnw[i][1] = s2[1, 3]
        dmix = _mm_full(tag + "_out", dy, w["out"], NT)
        g["out"] = _wgrad_rows(tag + "_gout", s["mix"], w["out"].shape[0] // 2, dy)
        if even:
            do, dg, dyc, lns = _even_mix_bwd1(tag + "_mix1", dmix, s["p"], s["of"], s["ob"], s["yc"],
                                              ev["lnw"][j], ev["lnb"][j])
            da, dgb, dcw = _even_conv_bwd(tag + "_conv", dyc, s["p"], ev["cw"][j], nct)
            dqf, dkf, dvf, dqb, dkb, dvb, dl = _retention_bwd(tag + "_ret", s["q"], s["k"], s["v"], do, s["ss"],
                                                              ev["lgb"][j], lc)
            dp = _even_dp(tag + "_dp", (dqf, dqb), (dkf, dkb), (dvf, dvb), dg, da, dgb, cs, sn)
            small["ev"].append(dict(lnw=lns[0], lnb=lns[1], cw=dcw, dl=dl[:, 0]))
        else:
            dm_, dpd, vec, dpw, dsgw, dsgb = _odd_mix_bwd1(tag + "_mix1", dmix, s["p"], s["m"], od["pw"][j], od["ps"][j],
                                                           od["lnw"][j], od["lnb"][j], od["sgw"][j], od["sgb"][j])
            dp = _odd_dp(tag + "_dp", dm_, dpd, nct, lc)
            small["od"].append(dict(ps=vec[0], lnw=vec[1], lnb=vec[2], pw=dpw, sgw=dsgw, sgb=dsgb[:, :, 0]))
        dh = _mm_cols_bwd(tag + "_in", dp, w["in"])
        g["in"] = _wgrad_cols(tag + "_gin", s["h"], dp, w["in"].shape[0])
        deps = put_g(i, g)
        if i > 0:
            dx, df, s1 = _bnm(tag + "_norm1", s["x"], dh, dx1, saved[i - 1]["f"], modt[i], 0, 1, modt[i - 1], 5,
                              nw[i, 0], nct, deps)
            dmod[i - 1][5] = s1[:, 2]
        else:
            dx, _, s1 = _bnm(tag + "_norm1", s["x"], dh, dx1, None, modt[0], 0, 1, None, 0, nw[0, 0], nct, deps)
        dmod[i][0], dmod[i][1] = s1[:, 0], s1[:, 1]
        dnw[i][0] = s1[1, 3]
    small["ev"].reverse()
    small["od"].reverse()
    dmod_t = jnp.stack([jnp.concatenate([jnp.stack(rows, axis=1), jnp.zeros((2, 2, dm), F32)], axis=1) for rows in dmod])
    small["dmod"] = dmod_t
    small["dnw"] = jnp.stack([jnp.stack(r) for r in dnw])
    return loss_blk, dx, small


WEIGHTS = ["c_ctx", "ada_w", "ada_b", "norm_w", "even_w_in", "even_w_out", "ret_decay_logit", "conv_dw_w",
           "conv_ln_w", "conv_ln_b", "odd_w_in", "odd_w_out", "pool_w", "pool_scale", "sg_ln_w", "sg_ln_b",
           "sg_w", "sg_b", "ffn_w_gate", "ffn_w_up", "ffn_w_down", "final_norm_w"]
BIG = dict(even_in="even_w_in", even_out="even_w_out", odd_in="odd_w_in", odd_out="odd_w_out",
           gate="ffn_w_gate", up="ffn_w_up", down="ffn_w_down")


def _rows(a, width=1024):
    flat = a.reshape(-1)
    n = flat.shape[0]
    per = 8 * width
    tot = -(-n // per) * per
    return jnp.pad(flat, (0, tot - n)).reshape(tot // width, width)


def _unshard(parts, lead):
    nl = len(lead)
    perm = tuple(range(1, nl + 1)) + (0, nl + 1)
    return parts.transpose(perm).reshape(tuple(lead) + (4 * parts.shape[-1],))


def _my_cols(a, chip, n):
    start = (0,) * (a.ndim - 1) + (chip * n,)
    return lax.dynamic_slice(a, start, a.shape[:-1] + (n,))


def kernel(x, c, ctx, c_ctx, ada_w, ada_b, norm_w, even_w_in, even_w_out, ret_decay_logit, conv_dw_w, conv_ln_w, conv_ln_b, odd_w_in, odd_w_out, pool_w, pool_scale, sg_ln_w, sg_ln_b, sg_w, sg_b, ffn_w_gate, ffn_w_up, ffn_w_down, final_norm_w, loss_target, m_c_ctx, m_ada_w, m_ada_b, m_norm_w, m_even_w_in, m_even_w_out, m_ret_decay_logit, m_conv_dw_w, m_conv_ln_w, m_conv_ln_b, m_odd_w_in, m_odd_w_out, m_pool_w, m_pool_scale, m_sg_ln_w, m_sg_ln_b, m_sg_w, m_sg_b, m_ffn_w_gate, m_ffn_w_up, m_ffn_w_down, m_final_norm_w, v_c_ctx, v_ada_w, v_ada_b, v_norm_w, v_even_w_in, v_even_w_out, v_ret_decay_logit, v_conv_dw_w, v_conv_ln_w, v_conv_ln_b, v_odd_w_in, v_odd_w_out, v_pool_w, v_pool_scale, v_sg_ln_w, v_sg_ln_b, v_sg_w, v_sg_b, v_ffn_w_gate, v_ffn_w_up, v_ffn_w_down, v_final_norm_w):
    wv = dict(c_ctx=c_ctx, ada_w=ada_w, ada_b=ada_b, norm_w=norm_w, even_w_in=even_w_in, even_w_out=even_w_out,
              ret_decay_logit=ret_decay_logit, conv_dw_w=conv_dw_w, conv_ln_w=conv_ln_w, conv_ln_b=conv_ln_b,
              odd_w_in=odd_w_in, odd_w_out=odd_w_out, pool_w=pool_w, pool_scale=pool_scale, sg_ln_w=sg_ln_w,
              sg_ln_b=sg_ln_b, sg_w=sg_w, sg_b=sg_b, ffn_w_gate=ffn_w_gate, ffn_w_up=ffn_w_up,
              ffn_w_down=ffn_w_down, final_norm_w=final_norm_w)
    mv = dict(zip(WEIGHTS, (m_c_ctx, m_ada_w, m_ada_b, m_norm_w, m_even_w_in, m_even_w_out, m_ret_decay_logit,
                            m_conv_dw_w, m_conv_ln_w, m_conv_ln_b, m_odd_w_in, m_odd_w_out, m_pool_w, m_pool_scale,
                            m_sg_ln_w, m_sg_ln_b, m_sg_w, m_sg_b, m_ffn_w_gate, m_ffn_w_up, m_ffn_w_down,
                            m_final_norm_w)))
    vv = dict(zip(WEIGHTS, (v_c_ctx, v_ada_w, v_ada_b, v_norm_w, v_even_w_in, v_even_w_out, v_ret_decay_logit,
                            v_conv_dw_w, v_conv_ln_w, v_conv_ln_b, v_odd_w_in, v_odd_w_out, v_pool_w, v_pool_scale,
                            v_sg_ln_w, v_sg_ln_b, v_sg_w, v_sg_b, v_ffn_w_gate, v_ffn_w_up, v_ffn_w_down,
                            v_final_norm_w)))
    xi, yi, ci = lax.axis_index("x"), lax.axis_index("y"), lax.axis_index("c")
    chip = 2 * xi + yi
    dev = 4 * xi + 2 * yi + ci
    dm = x.shape[-1]
    lc = ctx.shape[1]
    depth = ada_w.shape[0]
    n_ada = ada_w.shape[-1]

    cw_pad = jnp.pad(conv_dw_w, ((0, 0), (0, 1), (0, 0)))
    vec3 = jnp.stack([pool_scale, sg_ln_w, sg_ln_b])
    pack1 = jnp.concatenate([_rows(c), _rows(norm_w), _rows(cw_pad), _rows(vec3)], axis=0)
    g1 = _all_gather8("gather_small", pack1).reshape(8, 32, dm)
    c_all = g1[:, 0]
    per_chip = g1[0::2]
    norm_full = _unshard(per_chip[:, 8:10].reshape(4, depth, 2, dm // 4), (depth, 2))
    cw_full = _unshard(per_chip[:, 16:24].reshape(4, 2, CONV_K + 1, 128), (2, CONV_K + 1))
    vec_full = _unshard(per_chip[:, 24, :768].reshape(4, 3, 2, 128), (3, 2))

    c16 = jnp.concatenate([c_all, c_ctx[None, :], jnp.zeros((7, dm), F32)], axis=0)
    mod_sh = _ada_mod("ada_mod", c16, ada_w, _my_cols(ada_b, chip, n_ada)[:, None, :])
    g2 = _all_gather8("gather_mod", mod_sh.reshape(depth * 16, n_ada)).reshape(8, depth, 16, n_ada)
    mod_full = _unshard(g2[0::2], (depth, 16))
    mod_x = lax.dynamic_index_in_dim(mod_full, dev, axis=1, keepdims=False).reshape(depth, 6, dm)
    mod_c = mod_full[:, 8].reshape(depth, 6, dm)
    modt = jnp.pad(jnp.stack([mod_c, mod_x], axis=1), ((0, 0), (0, 0), (0, 2), (0, 0)))

    names = list(BIG)
    tr_names = ("gate", "up")
    shard = {k: (jnp.swapaxes(wv[BIG[k]], 1, 2) if k in tr_names else wv[BIG[k]]).astype(BF16) for k in names}
    roles = ("in", "out", "gate", "up", "down")

    def layer_keys(i):
        mixer = ("even_in", "even_out") if i % 2 == 0 else ("odd_in", "odd_out")
        return [(k, i // 2) for k in mixer] + [(k, i) for k in ("gate", "up", "down")]

    def as_used(got):
        return {r: (g if r == "in" else g.reshape(4 * g.shape[1], g.shape[2])) for r, g in zip(roles, got)}

    started = {}

    def get_w(i, after):
        if i > 0:
            return as_used(_split_wait(f"gather_wait{i}", started[i], _gather_copies(len(roles)), after)), []
        got = _gather_weights("gather_w0", [shard[k][l] for k, l in layer_keys(0)], [modt])
        for li in range(1, depth):
            srcs = [shard[k][l] for k, l in layer_keys(li)]
            lands = [lax.empty((4,) + s.shape, s.dtype) for s in srcs]
            before = [got[0]] + ([started[li - 1][4]] if li > 1 else [])
            started[li] = _split_start(f"gather_start{li}", srcs, lands, _gather_copies(len(roles)), before)
        return as_used(got), [started[li][4] for li in range(1, depth)]

    idx = jnp.stack([ci, chip]).astype(jnp.int32)
    pending = {}

    def put_g(i, g):
        glist = [g[r].reshape(4, -1, g[r].shape[-1]) for r in roles]
        from_sib = _rs_pair(f"rs_pair{i}", glist)
        pair = [_add_half(f"rs_add{i}_{r}", gl, a, idx) for r, gl, a in zip(roles, glist, from_sib)]
        lands = [lax.empty((3,) + p.shape[1:], p.dtype) for p in pair]
        st = _split_start(f"rs_start{i}", pair, lands, _scatter_copies(len(roles)), [])
        pending[i] = (glist, from_sib, st)
        return [st[4]]

    ev = dict(lgb=jnp.broadcast_to(ret_decay_logit.reshape(-1, 2 * HEADS)[:, :, None], (ret_decay_logit.shape[0], 2 * HEADS, HEAD_DIM)),
              cw=cw_full, lnw=conv_ln_w[:, None, :], lnb=conv_ln_b[:, None, :])
    od = dict(pw=pool_w, ps=vec_full[0][:, None, :], lnw=vec_full[1][:, None, :], lnb=vec_full[2][:, None, :],
              sgw=sg_w, sgb=jnp.broadcast_to(sg_b[:, :, :, None], sg_b.shape + (GC,)))
    xs = jnp.concatenate([ctx[0], x[0]], axis=0)
    loss_blk, dxs, small = _local_step(xs, loss_target[0], modt, norm_full[:, :, None, :], final_norm_w[None, :],
                                       get_w, put_g, ev, od, lc)

    misc = jnp.stack([
        small["dfnw"], jnp.broadcast_to(loss_blk[0, 0], (dm,)),
        jnp.concatenate([e["lnw"] for e in small["ev"]]), jnp.concatenate([e["lnb"] for e in small["ev"]]),
        jnp.concatenate([o["ps"] for o in small["od"]]), jnp.concatenate([o["lnw"] for o in small["od"]]),
        jnp.concatenate([o["lnb"] for o in small["od"]]),
        jnp.pad(jnp.concatenate([e["dl"] for e in small["ev"]]), (0, dm - 4 * HEADS)),
        jnp.stack([o["sgb"] for o in small["od"]]).reshape(-1)])
    pack2 = jnp.concatenate([
        _rows(small["dmod"]), _rows(small["dnw"]), _rows(misc), _rows(jnp.stack([e["cw"] for e in small["ev"]])),
        _rows(jnp.stack([o["pw"] for o in small["od"]])), _rows(jnp.stack([o["sgw"] for o in small["od"]]))], axis=0)
    n2 = pack2.shape[0]
    g3 = _all_gather8("gather_grads", pack2)
    tot = _sum8("sum_grads", g3.reshape(8, n2, dm))
    r_mod = depth * 16
    o_nw, o_misc = r_mod, r_mod + 8
    o_cw = o_misc + 16
    o_pw = o_cw + 2 * (CONV_K + 1) // 2
    o_sgw = o_pw + 128
    dmod_sum = tot[:r_mod].reshape(depth, 2, 8, dm)
    dmod_dev = g3.reshape(8, n2, dm)[:, :r_mod].reshape(8, depth, 2, 8, dm)
    dm_x = dmod_dev[:, :, 1, :6].reshape(8, depth, 6 * dm).transpose(1, 0, 2)
    dm_c = dmod_sum[:, 0, :6].reshape(depth, 1, 6 * dm)
    dmod16 = jnp.concatenate([dm_x, dm_c, jnp.zeros((depth, 7, 6 * dm), F32)], axis=1)
    g_ada_b = _rowsum16("ada_b_grad", dmod16)[:, 0]
    g_ada_w, dc16 = _ada_bwd("ada_bwd", c16, _my_cols(dmod16, chip, n_ada), ada_w)
    g4 = _all_gather8("gather_cctx", dc16[8:16]).reshape(8, 8, dm)
    g_c_ctx = _cctx_grad("cctx_grad", g4[0::2, 0], c_ctx[None, :])[0]

    misc_t = tot[o_misc:o_misc + 16]
    half = lambda row: misc_t[row].reshape(2, dm // 2)
    grads = dict(
        c_ctx=g_c_ctx, ada_w=g_ada_w, ada_b=g_ada_b,
        norm_w=_my_cols(tot[o_nw:o_nw + 8].reshape(depth, 2, dm), chip, dm // 4),
        ret_decay_logit=misc_t[7, :4 * HEADS].reshape(ret_decay_logit.shape),
        conv_dw_w=_my_cols(tot[o_cw:o_cw + 2 * (CONV_K + 1) // 2].reshape(2, CONV_K + 1, dm // 2)[:, :CONV_K], chip, 128),
        conv_ln_w=half(2), conv_ln_b=half(3),
        pool_w=tot[o_pw:o_pw + 128].reshape(pool_w.shape),
        pool_scale=_my_cols(half(4), chip, 128), sg_ln_w=_my_cols(half(5), chip, 128), sg_ln_b=_my_cols(half(6), chip, 128),
        sg_w=tot[o_sgw:o_sgw + 128].reshape(sg_w.shape), sg_b=misc_t[8].reshape(sg_b.shape),
        final_norm_w=misc_t[0])
    loss = misc_t[1, 0]

    reduced = {k: None for k in names}
    for i in reversed(range(depth)):
        glist, from_sib, st = pending[i]
        slots = _split_wait(f"rs_wait{i}", st, _scatter_copies(len(roles)), [g_c_ctx])
        for (k, l), g, a, b in zip(layer_keys(i), glist, from_sib, slots):
            reduced[k] = _sum_final(f"rs_sum_{k}{l}", g, a, b, idx, reduced[k], l, shard[k].shape[0])
    shards = dict(zip(names, _rs_share("rs_share", [reduced[k] for k in names])))

    deltas, new_m, new_v = {}, {}, {}
    for k in names:
        n = BIG[k]
        if k in tr_names:
            tr = lambda a: jnp.swapaxes(a, 1, 2)
            outs = _adamw("adamw_" + n, tr(wv[n]), shards[k], tr(mv[n]), tr(vv[n]))
            grads[n] = tr(shards[k])
            deltas[n], new_m[n], new_v[n] = (tr(o) for o in outs)
        else:
            grads[n] = shards[k]
    for n in WEIGHTS:
        if n not in deltas:
            deltas[n], new_m[n], new_v[n] = _adamw("adamw_" + n, wv[n], grads[n], mv[n], vv[n])
    grad_x = dxs[lc:][None]
    return (loss, grad_x, *[grads[n] for n in WEIGHTS], *[deltas[n] for n in WEIGHTS],
            *[new_m[n] for n in WEIGHTS], *[new_v[n] for n in WEIGHTS])
```

```python
import functools
import math

import jax
import jax.numpy as jnp
from jax import lax
from jax.experimental import pallas as pl
from jax.experimental.pallas import tpu as pltpu

F32 = jnp.float32
BF16 = jnp.bfloat16
MESH = pl.DeviceIdType.MESH

EPS = 1e-6
GRID_W = 64
HEADS = 4
HEAD_DIM = 128
CHUNK = 128
CONV_K = 31
ROPE_BASE = 10000.0
ROPE_PAIRS = (16, 24, 24)
POOL_WINDOWS = (2, 4, 8, 16)
ADAM_LR, ADAM_B1, ADAM_B2, ADAM_EPS, ADAM_WD, ADAM_STEP = 0.001, 0.9, 0.999, 1e-08, 0.01, 10

ROW_TILE = 256
CONV_HALO = 16
POOL_HALO = 8
VMEM_LIMIT = 56 * 1024 * 1024
WGRAD_ROWS = 2304


def _pcall(body, **kw):
    return pl.pallas_call(body, **kw)


def _cp(sem=None, vmem=VMEM_LIMIT):
    if sem is None:
        return pltpu.CompilerParams(vmem_limit_bytes=vmem)
    return pltpu.CompilerParams(dimension_semantics=sem, vmem_limit_bytes=vmem)


def _sds(shape, dtype=F32):
    return jax.ShapeDtypeStruct(tuple(shape), dtype)


def _full(shape):
    nd = len(shape)
    return pl.BlockSpec(tuple(shape), lambda *_: (0,) * nd)


def _sigmoid(x):
    return jax.nn.sigmoid(x)


def _silu(x):
    return x * _sigmoid(x)


def _dsilu(x):
    s = _sigmoid(x)
    return s * (1.0 + x * (1.0 - s))


def _colsum(a):
    return jnp.sum(a, axis=0, keepdims=True)


def _dot(a, b, dn):
    return lax.dot_general(a.astype(BF16), b.astype(BF16), dn, preferred_element_type=F32)


NN = (((1,), (0,)), ((), ()))
NT = (((1,), (1,)), ((), ()))
TN = (((0,), (0,)), ((), ()))


def _mm_tile(t, cap=1152):
    best = 16
    for d in range(16, min(t, cap) + 1, 16):
        if t % d == 0:
            best = d
    return best


def _mm(name, pairs, grid, out_shape, out_spec, dn):
    npairs = len(pairs)
    nk = grid[-1]
    kax = len(grid) - 1
    assert nk == 1 or out_shape.dtype == F32

    def body(*refs):
        ins = refs[:2 * npairs]
        o_ref = refs[2 * npairs]
        tot = None
        for p in range(npairs):
            d = _dot(ins[2 * p][...], ins[2 * p + 1][...], dn)
            tot = d if tot is None else tot + d
        if nk == 1:
            o_ref[...] = tot.astype(o_ref.dtype)
        else:
            k = pl.program_id(kax)

            @pl.when(k == 0)
            def _():
                o_ref[...] = tot

            @pl.when(k != 0)
            def _():
                o_ref[...] += tot

    args, in_specs = [], []
    for a, a_spec, b, b_spec in pairs:
        args += [a, b]
        in_specs += [a_spec, b_spec]
    sem = ("parallel",) * kax + ("arbitrary",)
    return _pcall(body, name=name, grid=grid, in_specs=in_specs, out_specs=out_spec, out_shape=out_shape,
                  compiler_params=_cp(sem))(*args)


def _mm_cols(name, a, w, out_dtype=F32):
    t, k = a.shape
    j, _, n = w.shape
    tm = _mm_tile(t)
    return _mm(name, [(a, pl.BlockSpec((tm, k), lambda i, jj, kk: (i, 0)),
                       w, pl.BlockSpec((None, k, n), lambda i, jj, kk: (jj, 0, 0)))],
               (t // tm, j, 1), _sds((t, j * n), out_dtype), pl.BlockSpec((tm, n), lambda i, jj, kk: (i, jj)), NN)


def _mm_cols_bwd(name, d, w):
    t = d.shape[0]
    j, k, n = w.shape
    tm = _mm_tile(t)
    pairs = [(d, pl.BlockSpec((tm, n), functools.partial(lambda jj, i, u, kk: (i, jj), jj)),
              w, pl.BlockSpec((None, k, n), functools.partial(lambda jj, i, u, kk: (jj, 0, 0), jj))) for jj in range(j)]
    return _mm(name, pairs, (t // tm, 1, 1), _sds((t, k)), pl.BlockSpec((tm, k), lambda i, u, kk: (i, 0)), NT)


def _mm_full(name, a, w, dn, tm=None):
    t, k = a.shape
    n = w.shape[1] if dn is NN else w.shape[0]
    tm = tm or _mm_tile(t)
    return _mm(name, [(a, pl.BlockSpec((tm, k), lambda i, u, kk: (i, 0)), w, _full(w.shape))],
               (t // tm, 1, 1), _sds((t, n)), pl.BlockSpec((tm, n), lambda i, u, kk: (i, 0)), dn)


def _wgrad_cols(name, a, b, j):
    t, k = a.shape
    n = b.shape[1] // j
    tt = _mm_tile(t, WGRAD_ROWS)
    return _mm(name, [(a, pl.BlockSpec((tt, k), lambda jj, u, kk: (kk, 0)),
                       b, pl.BlockSpec((tt, n), lambda jj, u, kk: (kk, jj)))],
               (j, 1, t // tt), _sds((j, k, n)), pl.BlockSpec((None, k, n), lambda jj, u, kk: (jj, 0, 0)), TN)


def _wgrad_rows(name, a, blk, b):
    t, f = a.shape
    n = b.shape[1]
    tt = _mm_tile(t, WGRAD_ROWS)
    return _mm(name, [(a, pl.BlockSpec((tt, blk), lambda jj, u, kk: (kk, jj)),
                       b, pl.BlockSpec((tt, n), lambda jj, u, kk: (kk, 0)))],
               (f // blk, 1, t // tt), _sds((f, n)), pl.BlockSpec((blk, n), lambda jj, u, kk: (jj, 0)), TN)


def _ffn_tiles(t, f):
    return _mm_tile(t, 288), f


def _ffn_up(name, h, wgt, wut):
    t, k = h.shape
    f = wgt.shape[0]
    tm, tn = _ffn_tiles(t, f)

    def body(h_ref, wg_ref, wu_ref, a_ref, gt_ref, up_ref):
        hv = h_ref[...]
        gt = _dot(hv, wg_ref[...], NT)
        up = _dot(hv, wu_ref[...], NT)
        a_ref[...] = (_silu(gt) * up).astype(BF16)
        gt_ref[...] = gt.astype(BF16)
        up_ref[...] = up.astype(BF16)

    wspec = pl.BlockSpec((tn, k), lambda i, jj: (jj, 0))
    ospec = pl.BlockSpec((tm, tn), lambda i, jj: (i, jj))
    o = _sds((t, f), BF16)
    return _pcall(body, name=name, grid=(t // tm, f // tn),
                  in_specs=[pl.BlockSpec((tm, k), lambda i, jj: (i, 0)), wspec, wspec],
                  out_specs=[ospec, ospec, ospec], out_shape=[o, o, o],
                  compiler_params=_cp(("parallel", "parallel")))(h, wgt, wut)


def _ffn_down_bwd(name, df, wd, gt, up, deps=()):
    t, dm = df.shape
    f = wd.shape[0]
    tm, tn = _ffn_tiles(t, f)
    nd = len(deps)

    def body(*refs):
        df_ref, wd_ref, gt_ref, up_ref = refs[:4]
        dgt_ref, dup_ref = refs[4 + nd:]
        da = _dot(df_ref[...], wd_ref[...], NT)
        g = gt_ref[...].astype(F32)
        u = up_ref[...].astype(F32)
        dgt_ref[...] = (da * u * _dsilu(g)).astype(BF16)
        dup_ref[...] = (da * _silu(g)).astype(BF16)

    aspec = pl.BlockSpec((tm, tn), lambda i, jj: (i, jj))
    o = _sds((t, f), BF16)
    return _pcall(body, name=name, grid=(t // tm, f // tn),
                  in_specs=[pl.BlockSpec((tm, dm), lambda i, jj: (i, 0)),
                            pl.BlockSpec((tn, dm), lambda i, jj: (jj, 0)), aspec, aspec]
                  + [pl.BlockSpec(d.shape, lambda i, jj: (0, 0)) for d in deps],
                  out_specs=[aspec, aspec], out_shape=[o, o],
                  compiler_params=_cp(("parallel", "parallel")))(df, wd, gt, up, *deps)


def _ffn_in_bwd(name, dgt, dup, wgt, wut):
    t, f = dgt.shape
    k = wgt.shape[1]
    tm = _mm_tile(t, 576)
    aspec = pl.BlockSpec((tm, f), lambda i, u, kk: (i, 0))
    wspec = pl.BlockSpec((f, k), lambda i, u, kk: (0, 0))
    return _mm(name, [(dgt, aspec, wgt, wspec), (dup, aspec, wut, wspec)], (t // tm, 1, 1), _sds((t, k)),
               pl.BlockSpec((tm, k), lambda i, u, kk: (i, 0)), NN)


def _modrow(ref, row, is_ctx):
    return jnp.where(is_ctx, ref[0, row:row + 1, :], ref[1, row:row + 1, :])


def _rnm(name, x, delta, mod_g, g_row, mod_n, sh_row, sc_row, nw, nct, deps=()):
    t, dm = x.shape
    tm = ROW_TILE
    has = delta is not None
    nd = len(deps)

    def body(*refs):
        refs = refs[:len(refs) - nd - (2 if has else 1)] + refs[len(refs) - (2 if has else 1):]
        if has:
            x_ref, d_ref, mg_ref, m_ref, nw_ref, xo_ref, h_ref = refs
        else:
            x_ref, m_ref, nw_ref, h_ref = refs
        is_ctx = pl.program_id(0) < nct
        xv = x_ref[...]
        if has:
            xv = xv + _modrow(mg_ref, g_row, is_ctx) * d_ref[...]
            xo_ref[...] = xv
        r = lax.rsqrt(jnp.mean(xv * xv, axis=-1, keepdims=True) + EPS)
        hv = (xv * r * nw_ref[...]) * (1.0 + _modrow(m_ref, sc_row, is_ctx)) + _modrow(m_ref, sh_row, is_ctx)
        h_ref[...] = hv.astype(BF16)

    row = pl.BlockSpec((tm, dm), lambda i: (i, 0))
    ins = [x] + ([delta, mod_g] if has else []) + [mod_n, nw] + list(deps)
    in_specs = ([row] + ([row, _full(mod_g.shape)] if has else []) + [_full(mod_n.shape), _full(nw.shape)]
                + [_full(d.shape) for d in deps])
    outs = ([_sds((t, dm))] if has else []) + [_sds((t, dm), BF16)]
    out_specs = ([row] if has else []) + [row]
    res = _pcall(body, name=name, grid=(t // tm,), in_specs=in_specs, out_specs=out_specs, out_shape=outs,
                 compiler_params=_cp(("parallel",)))(*ins)
    return res if has else (None, res[0])


def _bnm(name, xn, dh, dup, yprev, mod_n, sh_row, sc_row, mod_g, g_row, nw, nct, deps=()):
    t, dm = xn.shape
    tm = ROW_TILE
    has = yprev is not None
    nd = len(deps)

    def body(*refs):
        nout = 3 if has else 2
        refs = refs[:len(refs) - nd - nout] + refs[len(refs) - nout:]
        if has:
            x_ref, dh_ref, du_ref, y_ref, mn_ref, mg_ref, nw_ref, dx_ref, dd_ref, s_ref = refs
        else:
            x_ref, dh_ref, du_ref, mn_ref, nw_ref, dx_ref, s_ref = refs
        i = pl.program_id(0)
        is_ctx = i < nct

        @pl.when(i == 0)
        def _():
            s_ref[...] = jnp.zeros_like(s_ref)

        xv = x_ref[...]
        r = lax.rsqrt(jnp.mean(xv * xv, axis=-1, keepdims=True) + EPS)
        xh = xv * r
        w = nw_ref[...]
        sc1 = 1.0 + _modrow(mn_ref, sc_row, is_ctx)
        dhv = dh_ref[...]
        dxh = dhv * sc1 * w
        dx = r * (dxh - xh * jnp.mean(dxh * xh, axis=-1, keepdims=True)) + du_ref[...]
        dx_ref[...] = dx
        parts = [_colsum(dhv), _colsum(dhv * (xh * w))]
        if has:
            dd_ref[...] = (_modrow(mg_ref, g_row, is_ctx) * dx).astype(BF16)
            parts.append(_colsum(dx * y_ref[...]))
        else:
            parts.append(jnp.zeros((1, dm), F32))
        upd = jnp.concatenate(parts + [jnp.zeros((5, dm), F32)], axis=0)
        dnw = jnp.concatenate([jnp.zeros((3, dm), F32), _colsum(dhv * sc1 * xh), jnp.zeros((4, dm), F32)], axis=0)

        @pl.when(is_ctx)
        def _():
            s_ref[0] += upd
            s_ref[1] += dnw

        @pl.when(jnp.logical_not(is_ctx))
        def _():
            s_ref[1] += upd + dnw

    row = pl.BlockSpec((tm, dm), lambda i: (i, 0))
    ins = [xn, dh, dup] + ([yprev] if has else []) + [mod_n] + ([mod_g] if has else []) + [nw] + list(deps)
    in_specs = ([row, row, row] + ([row] if has else []) + [_full(mod_n.shape)]
                + ([_full(mod_g.shape)] if has else []) + [_full(nw.shape)] + [_full(d.shape) for d in deps])
    outs = [_sds((t, dm))] + ([_sds((t, dm), BF16)] if has else []) + [_sds((2, 8, dm))]
    out_specs = [row] + ([row] if has else []) + [_full((2, 8, dm))]
    res = _pcall(body, name=name, grid=(t // tm,), in_specs=in_specs, out_specs=out_specs, out_shape=outs,
                 compiler_params=_cp(("arbitrary",)))(*ins)
    return res if has else (res[0], None, res[1])


def _fin(name, x1, f, mod, g_row, fw, target, nct):
    t, dm = x1.shape
    tm = ROW_TILE

    def body(x_ref, f_ref, m_ref, fw_ref, t_ref, loss_ref, dx_ref, dd_ref, s_ref):
        i = pl.program_id(0)

        @pl.when(i == 0)
        def _():
            s_ref[...] = jnp.zeros_like(s_ref)
            loss_ref[...] = jnp.zeros_like(loss_ref)

        @pl.when(i < nct)
        def _():
            dx_ref[...] = jnp.zeros_like(dx_ref)
            dd_ref[...] = jnp.zeros_like(dd_ref)

        @pl.when(i >= nct)
        def _():
            g = m_ref[1, g_row:g_row + 1, :]
            fv = f_ref[...]
            xv = x_ref[...] + g * fv
            r = lax.rsqrt(jnp.mean(xv * xv, axis=-1, keepdims=True) + EPS)
            xh = xv * r
            w = fw_ref[...]
            err = xh * w - t_ref[...]
            loss_ref[...] += 0.5 * jnp.sum(err * err) / dm
            dout = err * (1.0 / dm)
            dxh = dout * w
            dx = r * (dxh - xh * jnp.mean(dxh * xh, axis=-1, keepdims=True))
            dx_ref[...] = dx
            dd_ref[...] = (g * dx).astype(BF16)
            s_ref[...] += jnp.concatenate([_colsum(dx * fv), _colsum(dout * xh), jnp.zeros((6, dm), F32)], axis=0)

    row = pl.BlockSpec((tm, dm), lambda i: (i, 0))
    trow = pl.BlockSpec((tm, dm), lambda i: (jnp.maximum(i - nct, 0), 0))
    return _pcall(body, name=name, grid=(t // tm,),
                  in_specs=[row, row, _full(mod.shape), _full(fw.shape), trow],
                  out_specs=[_full((8, 128)), row, row, _full((8, dm))],
                  out_shape=[_sds((8, 128)), _sds((t, dm)), _sds((t, dm), BF16), _sds((8, dm))],
                  compiler_params=_cp(("arbitrary",)))(x1, f, mod, fw, target)


def _rope_tables(t, lc):
    l = t - lc
    rows = l // GRID_W
    grid_r = jnp.broadcast_to(jnp.arange(rows, dtype=F32)[:, None], (rows, GRID_W)).reshape(-1)
    grid_c = jnp.broadcast_to(jnp.arange(GRID_W, dtype=F32)[None, :], (rows, GRID_W)).reshape(-1)

    def angles(p_seq, p_row, p_col):
        parts = []
        for p, n in zip((p_seq, p_row, p_col), ROPE_PAIRS):
            freq = ROPE_BASE ** (-jnp.arange(n, dtype=F32) / n)
            parts.append(p[:, None] * freq[None, :])
        return jnp.concatenate(parts, axis=-1)

    zc = jnp.zeros((lc,), F32)
    ang = jnp.concatenate([angles(jnp.arange(lc, dtype=F32), zc, zc),
                           angles(jnp.full((l,), lc, F32), grid_r, grid_c)], axis=0)
    cos, sin = jnp.cos(ang), jnp.sin(ang)
    return jnp.concatenate([cos, cos], axis=-1), jnp.concatenate([-sin, sin], axis=-1)


def _rope(u, cs, sn):
    return u * cs + pltpu.roll(u, HEAD_DIM // 2, 1) * sn


def _rope_t(d, cs, sn):
    return d * cs + pltpu.roll(d * sn, HEAD_DIM // 2, 1)


def _even_qkv(name, p, cs, sn):
    t = p.shape[0]
    tm = ROW_TILE
    w = HEADS * HEAD_DIM
    scale = HEAD_DIM ** -0.5

    def body(q_ref, k_ref, v_ref, cs_ref, sn_ref, qo_ref, ko_ref, vo_ref):
        c, s = cs_ref[...], sn_ref[...]
        for h in range(HEADS):
            sl = slice(h * HEAD_DIM, (h + 1) * HEAD_DIM)
            qo_ref[:, sl] = (_rope(q_ref[:, sl], c, s) * scale).astype(BF16)
            ko_ref[:, sl] = _rope(k_ref[:, sl], c, s).astype(BF16)
        vo_ref[...] = v_ref[...].astype(BF16)

    col = lambda j: pl.BlockSpec((tm, w), lambda i: (i, j))
    tab = pl.BlockSpec((tm, HEAD_DIM), lambda i: (i, 0))
    o = _sds((t, w), BF16)
    return _pcall(body, name=name, grid=(t // tm,), in_specs=[col(0), col(1), col(2), tab, tab],
                  out_specs=[col(0)] * 3, out_shape=[o, o, o], compiler_params=_cp(("parallel",)))(p, p, p, cs, sn)


def _log_sigmoid_row(x):
    e = jnp.exp(-jnp.abs(x))
    l1p = jnp.where(e < 0.01, e * (1.0 - e * (0.5 - e * (1.0 / 3.0))), jnp.log(1.0 + e))
    return jnp.minimum(x, 0.0) - l1p


def _ret_tables(lgb_ref, dm_ref, xi_ref, zt_ref):
    ri = lax.broadcasted_iota(jnp.int32, (CHUNK, CHUNK), 0).astype(F32)
    ci = lax.broadcasted_iota(jnp.int32, (CHUNK, CHUNK), 1).astype(F32)
    for d in range(2):
        for h in range(HEADS):
            idx = d * HEADS + h
            lg = _log_sigmoid_row(lgb_ref[idx:idx + 1, :])
            if d == 0:
                e, mask = ri - ci, ri >= ci
                xe, ze = ri + 1.0, (CHUNK - 1.0) - ri
            else:
                e, mask = ci - ri - 1.0, ci > ri
                xe, ze = (CHUNK - 1.0) - ri, ri
            dm_ref[idx] = jnp.where(mask, jnp.exp(lg * jnp.where(mask, e, 0.0)), 0.0)
            xi_ref[idx] = jnp.exp(lg * xe)
            zt_ref[idx] = jnp.exp(lg * ze)


def _ret_exponents(d):
    ri = lax.broadcasted_iota(jnp.int32, (CHUNK, CHUNK), 0).astype(F32)
    ci = lax.broadcasted_iota(jnp.int32, (CHUNK, CHUNK), 1).astype(F32)
    if d == 0:
        return ri - ci, ri + 1.0, (CHUNK - 1.0) - ri
    return ci - ri - 1.0, (CHUNK - 1.0) - ri, ri


def _bwd_chunk(n, ncc, nc):
    return jnp.where(n < ncc, ncc - 1 - n, nc - 1 - (n - ncc))


def _retention_fwd(name, q, k, v, lgb, lc):
    t, w = q.shape
    nc, ncc = t // CHUNK, lc // CHUNK
    nh = 2 * HEADS

    def body(qf_ref, kf_ref, vf_ref, qb_ref, kb_ref, vb_ref, lgb_ref, of_ref, ob_ref, ss_ref,
             s_ref, dm_ref, xi_ref, zt_ref):
        n = pl.program_id(0)

        @pl.when(n == 0)
        def _():
            s_ref[...] = jnp.zeros_like(s_ref)
            _ret_tables(lgb_ref, dm_ref, xi_ref, zt_ref)

        for d in range(2):
            q_ref, k_ref, v_ref, o_ref = (qf_ref, kf_ref, vf_ref, of_ref) if d == 0 else (qb_ref, kb_ref, vb_ref, ob_ref)
            for h in range(HEADS):
                idx = d * HEADS + h
                sl = slice(h * HEAD_DIM, (h + 1) * HEAD_DIM)
                qv, kv, vv = q_ref[:, sl], k_ref[:, sl], v_ref[:, sl]
                s = s_ref[idx]
                ss_ref[idx] = s
                a = _dot(qv, kv, NT) * dm_ref[idx]
                o = _dot(a, vv, NN) + _dot(qv.astype(F32) * xi_ref[idx], s, NN)
                o_ref[:, sl] = o
                gc = jnp.exp(_log_sigmoid_row(lgb_ref[idx:idx + 1, :]) * float(CHUNK))
                s_ref[idx] = gc * s + _dot(kv.astype(F32) * zt_ref[idx], vv, TN)

    fspec = pl.BlockSpec((CHUNK, w), lambda n: (n, 0))
    bspec = pl.BlockSpec((CHUNK, w), lambda n: (_bwd_chunk(n, ncc, nc), 0))
    tab = pltpu.VMEM((nh, CHUNK, CHUNK), F32)
    return _pcall(body, name=name, grid=(nc,),
                  in_specs=[fspec] * 3 + [bspec] * 3 + [_full((nh, HEAD_DIM))],
                  out_specs=[fspec, bspec, pl.BlockSpec((None, nh, CHUNK, CHUNK), lambda n: (n, 0, 0, 0))],
                  out_shape=[_sds((t, w)), _sds((t, w)), _sds((nc, nh, CHUNK, CHUNK))],
                  scratch_shapes=[tab, tab, tab, tab],
                  compiler_params=_cp(("arbitrary",)))(q, k, v, q, k, v, lgb)


def _retention_bwd(name, q, k, v, do, ss, lgb, lc):
    t, w = q.shape
    nc, ncc = t // CHUNK, lc // CHUNK
    nh = 2 * HEADS

    def body(qf_ref, kf_ref, vf_ref, gf_ref, qb_ref, kb_ref, vb_ref, gb_ref, ss_ref, lgb_ref,
             dqf_ref, dkf_ref, dvf_ref, dqb_ref, dkb_ref, dvb_ref, dl_ref,
             ds_ref, dm_ref, xi_ref, zt_ref, acc_ref):
        n = pl.program_id(0)

        @pl.when(n == 0)
        def _():
            ds_ref[...] = jnp.zeros_like(ds_ref)
            acc_ref[...] = jnp.zeros_like(acc_ref)
            _ret_tables(lgb_ref, dm_ref, xi_ref, zt_ref)

        for d in range(2):
            if d == 0:
                q_ref, k_ref, v_ref, g_ref, dq_ref, dk_ref, dv_ref = qf_ref, kf_ref, vf_ref, gf_ref, dqf_ref, dkf_ref, dvf_ref
            else:
                q_ref, k_ref, v_ref, g_ref, dq_ref, dk_ref, dv_ref = qb_ref, kb_ref, vb_ref, gb_ref, dqb_ref, dkb_ref, dvb_ref
            ee, xe, ze = _ret_exponents(d)
            for h in range(HEADS):
                idx = d * HEADS + h
                sl = slice(h * HEAD_DIM, (h + 1) * HEAD_DIM)
                qv, kv, vv, gv = q_ref[:, sl], k_ref[:, sl], v_ref[:, sl], g_ref[:, sl]
                s = ss_ref[idx]
                dsp = ds_ref[idx]
                dmat, xi, zt = dm_ref[idx], xi_ref[idx], zt_ref[idx]
                qf32, kf32 = qv.astype(F32), kv.astype(F32)
                a = _dot(qv, kv, NT) * dmat
                dar = _dot(gv, vv, NT)
                da = dar * dmat
                t1 = _dot(gv, s, NT)
                t2 = _dot(vv, dsp, NT)
                dq_ref[:, sl] = _dot(da, kv, NN) + xi * t1
                dk_ref[:, sl] = _dot(da, qv, TN) + zt * t2
                dv_ref[:, sl] = _dot(a, gv, TN) + _dot(kf32 * zt, dsp, NN)
                gc = jnp.exp(_log_sigmoid_row(lgb_ref[idx:idx + 1, :]) * float(CHUNK))
                ds_ref[idx] = gc * dsp + _dot(qf32 * xi, gv, TN)
                acc_ref[idx] += (ee * a * dar + xe * xi * qf32 * t1 + ze * zt * kf32 * t2
                                 + (float(CHUNK) * gc) * dsp * s)

        @pl.when(n == nc - 1)
        def _():
            for idx in range(nh):
                tot = jnp.sum(acc_ref[idx])
                dl_ref[idx:idx + 1, :] = tot * _sigmoid(-lgb_ref[idx:idx + 1, :])

    fmap = lambda n: (nc - 1 - n, 0)
    bmap = lambda n: (_bwd_chunk(nc - 1 - n, ncc, nc), 0)
    fspec = pl.BlockSpec((CHUNK, w), fmap)
    bspec = pl.BlockSpec((CHUNK, w), bmap)
    tab = pltpu.VMEM((nh, CHUNK, CHUNK), F32)
    o = _sds((t, w))
    return _pcall(body, name=name, grid=(nc,),
                  in_specs=[fspec] * 4 + [bspec] * 4
                  + [pl.BlockSpec((None, nh, CHUNK, CHUNK), lambda n: (nc - 1 - n, 0, 0, 0)), _full((nh, HEAD_DIM))],
                  out_specs=[fspec] * 3 + [bspec] * 3 + [_full((nh, HEAD_DIM))],
                  out_shape=[o] * 6 + [_sds((nh, HEAD_DIM))],
                  scratch_shapes=[tab, tab, tab, tab, tab],
                  compiler_params=_cp(("arbitrary",)))(q, k, v, do, q, k, v, do, ss, lgb)


def _halo_specs(tm, halo, t, width, col):
    hb = tm // halo
    last = t // halo - 1
    prev = pl.BlockSpec((halo, width), lambda i: (jnp.maximum(i * hb - 1, 0), col))
    nxt = pl.BlockSpec((halo, width), lambda i: (jnp.minimum((i + 1) * hb, last), col))
    return prev, nxt


def _halo_valid(i, nct, nt):
    vp = jnp.logical_and(i != 0, i != nct)
    vn = jnp.logical_and(i != nct - 1, i != nt - 1)
    return vp, vn


def _fill_window(win_ref, prev, cur, nxt, vp, vn, halo, tm):
    win_ref[0:halo, :] = jnp.where(vp, prev, 0.0)
    win_ref[halo:halo + tm, :] = cur
    win_ref[halo + tm:halo + tm + halo, :] = jnp.where(vn, nxt, 0.0)


CONV_SUB = 64


def _conv_taps(win_ref, w_ref, tm, flip):
    outs = []
    for r0 in range(0, tm, CONV_SUB):
        acc = None
        for kk in range(CONV_K):
            wk = (CONV_K - 1 - kk) if flip else kk
            term = w_ref[wk:wk + 1, :] * win_ref[r0 + kk + 1:r0 + kk + 1 + CONV_SUB, :]
            acc = term if acc is None else acc + term
        outs.append(acc)
    return jnp.concatenate(outs, axis=0)


def _head_norm(y):
    r = lax.rsqrt(jnp.mean(y * y, axis=-1, keepdims=True) + EPS)
    return y * r, r


def _ln_stats(y):
    mu = jnp.mean(y, axis=-1, keepdims=True)
    yc = y - mu
    rs = lax.rsqrt(jnp.mean(yc * yc, axis=-1, keepdims=True) + EPS)
    return yc * rs, rs


def _ln_bwd(dyh, yh, rs):
    return rs * (dyh - jnp.mean(dyh, axis=-1, keepdims=True) - yh * jnp.mean(dyh * yh, axis=-1, keepdims=True))


def _even_mix(name, p, of, ob, cw, lnw, lnb, nct):
    t = p.shape[0]
    tm, halo = ROW_TILE, CONV_HALO
    nt = t // tm
    w = HEADS * HEAD_DIM

    def body(g_ref, a_ref, gb_ref, ap_ref, gbp_ref, an_ref, gbn_ref, of_ref, ob_ref, cw_ref, lw_ref, lb_ref,
             mix_ref, yc_ref, win_ref):
        i = pl.program_id(0)
        vp, vn = _halo_valid(i, nct, nt)
        glu = lambda a, b: a * _sigmoid(b)
        _fill_window(win_ref, glu(ap_ref[...], gbp_ref[...]), glu(a_ref[...], gb_ref[...]),
                     glu(an_ref[...], gbn_ref[...]), vp, vn, halo, tm)
        yc = _conv_taps(win_ref, cw_ref, tm, False)
        yc_ref[...] = yc
        yh, _ = _ln_stats(yc)
        mix_ref[:, w:2 * w] = _silu(yh * lw_ref[...] + lb_ref[...]).astype(BF16)
        for h in range(HEADS):
            sl = slice(h * HEAD_DIM, (h + 1) * HEAD_DIM)
            yn, _ = _head_norm(of_ref[:, sl] + ob_ref[:, sl])
            mix_ref[:, sl] = (_silu(g_ref[:, sl]) * yn).astype(BF16)

    col = lambda j: pl.BlockSpec((tm, w), lambda i: (i, j))
    ap, an = _halo_specs(tm, halo, t, w, 4)
    gp, gn = _halo_specs(tm, halo, t, w, 5)
    row = pl.BlockSpec((tm, w), lambda i: (i, 0))
    return _pcall(body, name=name, grid=(nt,),
                  in_specs=[col(3), col(4), col(5), ap, gp, an, gn, row, row,
                            _full(cw.shape), _full(lnw.shape), _full(lnb.shape)],
                  out_specs=[pl.BlockSpec((tm, 2 * w), lambda i: (i, 0)), row],
                  out_shape=[_sds((t, 2 * w), BF16), _sds((t, w))],
                  scratch_shapes=[pltpu.VMEM((tm + 2 * halo, w), F32)],
                  compiler_params=_cp(("parallel",)))(p, p, p, p, p, p, p, of, ob, cw, lnw, lnb)


def _even_mix_bwd1(name, dmix, p, of, ob, yc, lnw, lnb):
    t = p.shape[0]
    tm = ROW_TILE
    w = HEADS * HEAD_DIM

    def body(dr_ref, dc_ref, g_ref, of_ref, ob_ref, yc_ref, lw_ref, lb_ref, do_ref, dg_ref, dyc_ref, s_ref):
        @pl.when(pl.program_id(0) == 0)
        def _():
            s_ref[...] = jnp.zeros_like(s_ref)

        for h in range(HEADS):
            sl = slice(h * HEAD_DIM, (h + 1) * HEAD_DIM)
            yn, r = _head_norm(of_ref[:, sl] + ob_ref[:, sl])
            gv = g_ref[:, sl]
            dr = dr_ref[:, sl]
            dg_ref[:, sl] = (dr * yn * _dsilu(gv)).astype(BF16)
            dyn = dr * _silu(gv)
            do_ref[:, sl] = (r * (dyn - yn * jnp.mean(dyn * yn, axis=-1, keepdims=True))).astype(BF16)
        yh, rs = _ln_stats(yc_ref[...])
        lw = lw_ref[...]
        dlo = dc_ref[...] * _dsilu(yh * lw + lb_ref[...])
        dyc_ref[...] = _ln_bwd(dlo * lw, yh, rs)
        s_ref[...] += jnp.concatenate([_colsum(dlo * yh), _colsum(dlo), jnp.zeros((6, w), F32)], axis=0)

    col = lambda j: pl.BlockSpec((tm, w), lambda i: (i, j))
    row = pl.BlockSpec((tm, w), lambda i: (i, 0))
    return _pcall(body, name=name, grid=(t // tm,),
                  in_specs=[col(0), col(1), col(3), row, row, row, _full(lnw.shape), _full(lnb.shape)],
                  out_specs=[row, row, row, _full((8, w))],
                  out_shape=[_sds((t, w), BF16), _sds((t, w), BF16), _sds((t, w)), _sds((8, w))],
                  compiler_params=_cp(("arbitrary",)))(dmix, dmix, p, of, ob, yc, lnw, lnb)


def _even_conv_bwd(name, dyc, p, cw, nct):
    t = p.shape[0]
    tm, halo = ROW_TILE, CONV_HALO
    nt = t // tm
    w = HEADS * HEAD_DIM

    def body(d_ref, dp_ref, dn_ref, a_ref, gb_ref, ap_ref, gbp_ref, an_ref, gbn_ref, cw_ref,
             da_ref, dgb_ref, dw_ref, dwin_ref, uwin_ref):
        i = pl.program_id(0)

        @pl.when(i == 0)
        def _():
            dw_ref[...] = jnp.zeros_like(dw_ref)

        vp, vn = _halo_valid(i, nct, nt)
        glu = lambda a, b: a * _sigmoid(b)
        dcur = d_ref[...]
        _fill_window(dwin_ref, dp_ref[...], dcur, dn_ref[...], vp, vn, halo, tm)
        _fill_window(uwin_ref, glu(ap_ref[...], gbp_ref[...]), glu(a_ref[...], gb_ref[...]),
                     glu(an_ref[...], gbn_ref[...]), vp, vn, halo, tm)
        du = _conv_taps(dwin_ref, cw_ref, tm, True)
        av = a_ref[...]
        sg = _sigmoid(gb_ref[...])
        da_ref[...] = (du * sg).astype(BF16)
        dgb_ref[...] = (du * av * sg * (1.0 - sg)).astype(BF16)
        rows = [_colsum(dcur * uwin_ref[kk + 1:kk + 1 + tm, :]) for kk in range(CONV_K)]
        dw_ref[...] += jnp.concatenate(rows + [jnp.zeros((1, w), F32)], axis=0)

    col = lambda j: pl.BlockSpec((tm, w), lambda i: (i, j))
    row = pl.BlockSpec((tm, w), lambda i: (i, 0))
    dp, dn = _halo_specs(tm, halo, t, w, 0)
    ap, an = _halo_specs(tm, halo, t, w, 4)
    gp, gn = _halo_specs(tm, halo, t, w, 5)
    win = pltpu.VMEM((tm + 2 * halo, w), F32)
    return _pcall(body, name=name, grid=(nt,),
                  in_specs=[row, dp, dn, col(4), col(5), ap, gp, an, gn, _full(cw.shape)],
                  out_specs=[row, row, _full((CONV_K + 1, w))],
                  out_shape=[_sds((t, w), BF16), _sds((t, w), BF16), _sds((CONV_K + 1, w))],
                  scratch_shapes=[win, win],
                  compiler_params=_cp(("arbitrary",)))(dyc, dyc, dyc, p, p, p, p, p, p, cw)


def _even_dp(name, dqs, dks, dvs, dg, da, dgb, cs, sn):
    t, w = dg.shape
    tm = ROW_TILE
    scale = HEAD_DIM ** -0.5

    def body(dqf_ref, dqb_ref, dkf_ref, dkb_ref, dvf_ref, dvb_ref, dg_ref, da_ref, dgb_ref, cs_ref, sn_ref, dp_ref):
        c, s = cs_ref[...], sn_ref[...]
        for h in range(HEADS):
            sl = slice(h * HEAD_DIM, (h + 1) * HEAD_DIM)
            dp_ref[:, sl] = (_rope_t(dqf_ref[:, sl] + dqb_ref[:, sl], c, s) * scale).astype(BF16)
            dp_ref[:, w + h * HEAD_DIM:w + (h + 1) * HEAD_DIM] = _rope_t(dkf_ref[:, sl] + dkb_ref[:, sl], c, s).astype(BF16)
        dp_ref[:, 2 * w:3 * w] = (dvf_ref[...] + dvb_ref[...]).astype(BF16)
        dp_ref[:, 3 * w:4 * w] = dg_ref[...]
        dp_ref[:, 4 * w:5 * w] = da_ref[...]
        dp_ref[:, 5 * w:6 * w] = dgb_ref[...]

    row = pl.BlockSpec((tm, w), lambda i: (i, 0))
    tab = pl.BlockSpec((tm, HEAD_DIM), lambda i: (i, 0))
    return _pcall(body, name=name, grid=(t // tm,), in_specs=[row] * 9 + [tab, tab],
                  out_specs=pl.BlockSpec((tm, 6 * w), lambda i: (i, 0)), out_shape=_sds((t, 6 * w), BF16),
                  compiler_params=_cp(("parallel",)))(dqs[0], dqs[1], dks[0], dks[1], dvs[0], dvs[1], dg, da, dgb, cs, sn)


GROUPS = 4
GC = 128
INV_SQRT2 = 0.7071067811865476
INV_SQRT_2PI = 0.3989422804014327


def _gelu(x):
    return 0.5 * x * (1.0 + lax.erf(x * INV_SQRT2))


def _dgelu(x):
    return 0.5 * (1.0 + lax.erf(x * INV_SQRT2)) + x * jnp.exp(-0.5 * x * x) * INV_SQRT_2PI


def _pool_count(i, nct, lc, t, tm, rows, row0, left, right):
    is_ctx = i < nct
    seg_start = jnp.where(is_ctx, 0, lc)
    seg_len = jnp.where(is_ctx, lc, t - lc)
    pos = i * tm + row0 - seg_start + lax.broadcasted_iota(jnp.int32, (rows, GC), 0)
    cnt = jnp.minimum(pos + right, seg_len - 1) - jnp.maximum(pos - left, 0) + 1
    return jnp.maximum(cnt, 1).astype(F32)


def _spatial_gate(vln, sgw_ref, sgb_ref, tm):
    cols = []
    for g in range(GROUPS):
        sl = slice(g * GC, (g + 1) * GC)
        parts = [_dot(sgw_ref[g], vln[r0:r0 + CHUNK, sl], NN) + sgb_ref[g] for r0 in range(0, tm, CHUNK)]
        cols.append(jnp.concatenate(parts, axis=0))
    return jnp.concatenate(cols, axis=1)


def _odd_mix(name, p, pw, pscale, lnw, lnb, sgw, sgb, nct, lc):
    t = p.shape[0]
    tm, halo = ROW_TILE, POOL_HALO
    nt = t // tm
    w = GROUPS * GC

    def body(pc_ref, pp_ref, pn_ref, pu_ref, pv_ref, pw_ref, ps_ref, lw_ref, lb_ref, sgw_ref, sgb_ref,
             mix_ref, m_ref, win_ref):
        i = pl.program_id(0)
        vp, vn = _halo_valid(i, nct, nt)
        pc = pc_ref[...]
        _fill_window(win_ref, pp_ref[...], pc, pn_ref[...], vp, vn, halo, tm)
        for g, wd in enumerate(POOL_WINDOWS):
            sl = slice(g * GC, (g + 1) * GC)
            left = wd // 2
            right = wd - 1 - left
            s = None
            for o in range(-left, right + 1):
                term = win_ref[halo + o:halo + o + tm, sl]
                s = term if s is None else s + term
            mg = s / _pool_count(i, nct, lc, t, tm, tm, 0, left, right) - pc[:, sl]
            m_ref[:, sl] = mg
            mix_ref[:, sl] = (_dot(mg, pw_ref[g], NN) * ps_ref[:, sl]).astype(BF16)
        u = _gelu(pu_ref[...])
        vh, _ = _ln_stats(_gelu(pv_ref[...]))
        s = _spatial_gate(vh * lw_ref[...] + lb_ref[...], sgw_ref, sgb_ref, tm)
        mix_ref[:, w:2 * w] = (u * s).astype(BF16)

    col = lambda j: pl.BlockSpec((tm, w), lambda i: (i, j))
    pp, pn = _halo_specs(tm, halo, t, w, 0)
    return _pcall(body, name=name, grid=(nt,),
                  in_specs=[col(0), pp, pn, col(1), col(2), _full(pw.shape), _full(pscale.shape),
                            _full(lnw.shape), _full(lnb.shape), _full(sgw.shape), _full(sgb.shape)],
                  out_specs=[pl.BlockSpec((tm, 2 * w), lambda i: (i, 0)), col(0)],
                  out_shape=[_sds((t, 2 * w), BF16), _sds((t, w))],
                  scratch_shapes=[pltpu.VMEM((tm + 2 * halo, w), F32)],
                  compiler_params=_cp(("parallel",)))(p, p, p, p, p, pw, pscale, lnw, lnb, sgw, sgb)


def _odd_mix_bwd1(name, dmix, p, m, pw, pscale, lnw, lnb, sgw, sgb):
    t = p.shape[0]
    tm = ROW_TILE
    w = GROUPS * GC

    def body(dpo_ref, dsg_ref, pu_ref, pv_ref, m_ref, pw_ref, ps_ref, lw_ref, lb_ref, sgw_ref, sgb_ref,
             dm_ref, dpd_ref, vec_ref, dpw_ref, dsgw_ref, dsgb_ref):
        @pl.when(pl.program_id(0) == 0)
        def _():
            vec_ref[...] = jnp.zeros_like(vec_ref)
            dpw_ref[...] = jnp.zeros_like(dpw_ref)
            dsgw_ref[...] = jnp.zeros_like(dsgw_ref)
            dsgb_ref[...] = jnp.zeros_like(dsgb_ref)

        dscale = []
        for g in range(GROUPS):
            sl = slice(g * GC, (g + 1) * GC)
            mg = m_ref[:, sl]
            dpo = dpo_ref[:, sl]
            dscale.append(_colsum(dpo * _dot(mg, pw_ref[g], NN)))
            dpo = dpo * ps_ref[:, sl]
            dm_ref[:, sl] = _dot(dpo, pw_ref[g], NT)
            dpw_ref[g] += _dot(mg, dpo, TN)
        pu, pv = pu_ref[...], pv_ref[...]
        u = _gelu(pu)
        vh, rs = _ln_stats(_gelu(pv))
        lw = lw_ref[...]
        vln = vh * lw + lb_ref[...]
        s = _spatial_gate(vln, sgw_ref, sgb_ref, tm)
        dsg = dsg_ref[...]
        dpd_ref[:, 0:w] = (dsg * s * _dgelu(pu)).astype(BF16)
        ds = dsg * u
        cols = []
        for g in range(GROUPS):
            sl = slice(g * GC, (g + 1) * GC)
            parts = []
            for r0 in range(0, tm, CHUNK):
                dsc = ds[r0:r0 + CHUNK, sl]
                parts.append(_dot(sgw_ref[g], dsc, TN))
                dsgw_ref[g] += _dot(dsc, vln[r0:r0 + CHUNK, sl], NT)
                dsgb_ref[g] += dsc
            cols.append(jnp.concatenate(parts, axis=0))
        dvln = jnp.concatenate(cols, axis=1)
        dpd_ref[:, w:2 * w] = (_ln_bwd(dvln * lw, vh, rs) * _dgelu(pv)).astype(BF16)
        vec_ref[...] += jnp.concatenate([jnp.concatenate(dscale, axis=1), _colsum(dvln * vh), _colsum(dvln),
                                         jnp.zeros((5, w), F32)], axis=0)

        @pl.when(pl.program_id(0) == t // tm - 1)
        def _():
            for g in range(GROUPS):
                dsgb_ref[g] = jnp.broadcast_to(jnp.sum(dsgb_ref[g], axis=1, keepdims=True), (GC, GC))

    col = lambda j: pl.BlockSpec((tm, w), lambda i: (i, j))
    mat = _full((GROUPS, GC, GC))
    return _pcall(body, name=name, grid=(t // tm,),
                  in_specs=[col(0), col(1), col(1), col(2), col(0), _full(pw.shape), _full(pscale.shape),
                            _full(lnw.shape), _full(lnb.shape), _full(sgw.shape), _full(sgb.shape)],
                  out_specs=[col(0), pl.BlockSpec((tm, 2 * w), lambda i: (i, 0)), _full((8, w)), mat, mat, mat],
                  out_shape=[_sds((t, w)), _sds((t, 2 * w), BF16), _sds((8, w)),
                             _sds((GROUPS, GC, GC)), _sds((GROUPS, GC, GC)), _sds((GROUPS, GC, GC))],
                  compiler_params=_cp(("arbitrary",)))(dmix, dmix, p, p, m, pw, pscale, lnw, lnb, sgw, sgb)


def _odd_dp(name, dm, dpd, nct, lc):
    t, w = dm.shape
    tm, halo = ROW_TILE, POOL_HALO
    nt = t // tm

    def body(d_ref, dp_ref, dn_ref, dpd_ref, o_ref, win_ref):
        i = pl.program_id(0)
        vp, vn = _halo_valid(i, nct, nt)
        dcur = d_ref[...]
        _fill_window(win_ref, dp_ref[...], dcur, dn_ref[...], vp, vn, halo, tm)
        for g, wd in enumerate(POOL_WINDOWS):
            sl = slice(g * GC, (g + 1) * GC)
            left = wd // 2
            right = wd - 1 - left
            win_ref[:, sl] = win_ref[:, sl] / _pool_count(i, nct, lc, t, tm, tm + 2 * halo, -halo, left, right)
            s = None
            for o in range(-right, left + 1):
                term = win_ref[halo + o:halo + o + tm, sl]
                s = term if s is None else s + term
            o_ref[:, sl] = (s - dcur[:, sl]).astype(BF16)
        o_ref[:, w:3 * w] = dpd_ref[...]

    row = pl.BlockSpec((tm, w), lambda i: (i, 0))
    pp, pn = _halo_specs(tm, halo, t, w, 0)
    return _pcall(body, name=name, grid=(nt,),
                  in_specs=[row, pp, pn, pl.BlockSpec((tm, 2 * w), lambda i: (i, 0))],
                  out_specs=pl.BlockSpec((tm, 3 * w), lambda i: (i, 0)), out_shape=_sds((t, 3 * w), BF16),
                  scratch_shapes=[pltpu.VMEM((tm + 2 * halo, w), F32)],
                  compiler_params=_cp(("parallel",)))(dm, dm, dm, dpd)


def _place():
    x, y, c = lax.axis_index("x"), lax.axis_index("y"), lax.axis_index("c")
    chips = [(1 - x, y), (x, 1 - y), (1 - x, 1 - y)]
    return x, y, c, chips


def _chip_index(cx, cy):
    return 2 * cx + cy


def _all_gather8(name, blk, after=()):
    m_per, n = blk.shape
    na = len(after)

    def body(x_ref, *rest):
        out_ref, send_sems, recv_sems, local_sem = rest[na:]
        x, y, c, chips = _place()
        me, sibling = (x, y, c), (x, y, 1 - c)

        def rows(px, py, pc):
            return out_ref.at[pl.ds((4 * px + 2 * py + pc) * m_per, m_per), :]

        def copy(k, block, to, src=None):
            return pltpu.make_async_remote_copy(
                src_ref=rows(*block) if src is None else src, dst_ref=rows(*block),
                send_sem=send_sems.at[k], recv_sem=recv_sems.at[k], device_id=to, device_id_type=MESH)

        mine = pltpu.make_async_copy(x_ref, rows(*me), local_sem)
        mine.start()
        first = [copy(0, me, sibling, src=x_ref)]
        first += [copy(1 + j, me, (*chip, c), src=x_ref) for j, chip in enumerate(chips)]
        for cp in first:
            cp.start()
        passed = [copy(4 + j, (*chip, c), sibling) for j, chip in enumerate(chips)]
        for j, chip in enumerate(chips):
            copy(1 + j, (*chip, c), me).wait_recv()
            passed[j].start()
        copy(0, sibling, me).wait_recv()
        for j, chip in enumerate(chips):
            copy(4 + j, (*chip, 1 - c), me).wait_recv()
        for cp in first + passed:
            cp.wait_send()
        mine.wait()

    return _pcall(body, name=name, out_shape=_sds((8 * m_per, n), blk.dtype),
                  in_specs=[pl.BlockSpec(memory_space=pltpu.VMEM)] + [pl.BlockSpec(memory_space=pl.ANY)] * na,
                  out_specs=pl.BlockSpec(memory_space=pltpu.VMEM),
                  scratch_shapes=[pltpu.SemaphoreType.DMA((7,)), pltpu.SemaphoreType.DMA((7,)), pltpu.SemaphoreType.DMA],
                  compiler_params=_cp())(blk, *after)


ANY = pl.BlockSpec(memory_space=pl.ANY)


def _half(which, rows):
    return pl.ds(pl.multiple_of(which * rows, 16), rows)


def _gather_weights(name, ws, after=()):
    nw = len(ws)
    na = len(after)
    ns = 7

    def body(*refs):
        w_refs, o_refs = refs[:nw], refs[nw + na:2 * nw + na]
        send_sems, recv_sems = refs[2 * nw + na:]
        x, y, c, chips = _place()
        me_chip = _chip_index(x, y)
        sibling = (x, y, 1 - c)

        def rcopy(t, k, src, dst, to):
            return pltpu.make_async_remote_copy(src_ref=src, dst_ref=dst, send_sem=send_sems.at[t * ns + k],
                                                recv_sem=recv_sems.at[t * ns + k], device_id=to, device_id_type=MESH)

        sends = []
        for t in range(nw):
            lh = w_refs[t].shape[0] // 2
            for k, chip in enumerate(chips):
                sends.append(rcopy(t, k, w_refs[t].at[_half(c, lh)], o_refs[t].at[me_chip, _half(c, lh)], (*chip, c)))
                sends[-1].start()
            sends.append(rcopy(t, 6, w_refs[t], o_refs[t].at[me_chip], sibling))
            sends[-1].start()
        for t in range(nw):
            lh = w_refs[t].shape[0] // 2
            for k, chip in enumerate(chips):
                part = o_refs[t].at[_chip_index(*chip), _half(c, lh)]
                rcopy(t, k, part, part, (*chip, c)).wait_recv()
                sends.append(rcopy(t, 3 + k, part, part, sibling))
                sends[-1].start()
        for t in range(nw):
            lh = w_refs[t].shape[0] // 2
            own = o_refs[t].at[me_chip]
            rcopy(t, 6, own, own, sibling).wait_recv()
            for k, chip in enumerate(chips):
                part = o_refs[t].at[_chip_index(*chip), _half(1 - c, lh)]
                rcopy(t, 3 + k, part, part, sibling).wait_recv()
        for cp in sends:
            cp.wait_send()

    return _pcall(body, name=name, out_shape=[_sds((4,) + w.shape, w.dtype) for w in ws],
                  in_specs=[ANY] * (nw + na), out_specs=[ANY] * nw,
                  scratch_shapes=[pltpu.SemaphoreType.DMA((ns * nw,)), pltpu.SemaphoreType.DMA((ns * nw,))],
                  compiler_params=_cp())(*ws, *after)


def _rs_pair(name, gs):
    ng = len(gs)

    def body(*refs):
        g_refs, o_refs = refs[:ng], refs[ng:2 * ng]
        send_sems, recv_sems = refs[2 * ng:]
        x, y, c, _ = _place()
        cps = []
        for t in range(ng):
            lh = g_refs[t].shape[1] // 2
            cp = pltpu.make_async_remote_copy(
                src_ref=g_refs[t].at[:, _half(1 - c, lh)], dst_ref=o_refs[t],
                send_sem=send_sems.at[t], recv_sem=recv_sems.at[t], device_id=(x, y, 1 - c), device_id_type=MESH)
            cp.start()
            cps.append(cp)
        for cp in cps:
            cp.wait()

    outs = [_sds((g.shape[0], g.shape[1] // 2) + g.shape[2:], g.dtype) for g in gs]
    return _pcall(body, name=name, out_shape=outs, in_specs=[ANY] * ng, out_specs=[ANY] * ng,
                  scratch_shapes=[pltpu.SemaphoreType.DMA((ng,)), pltpu.SemaphoreType.DMA((ng,))],
                  compiler_params=_cp())(*gs)


def _rs_chips(name, ps):
    ng = len(ps)

    def body(*refs):
        p_refs, o_refs = refs[:ng], refs[ng:2 * ng]
        send_sems, recv_sems = refs[2 * ng:]
        x, y, c, chips = _place()
        cps = []
        for t in range(ng):
            for k, chip in enumerate(chips):
                cp = pltpu.make_async_remote_copy(
                    src_ref=p_refs[t].at[_chip_index(*chip)], dst_ref=o_refs[t].at[k],
                    send_sem=send_sems.at[t * 3 + k], recv_sem=recv_sems.at[t * 3 + k],
                    device_id=(*chip, c), device_id_type=MESH)
                cp.start()
                cps.append(cp)
        for cp in cps:
            cp.wait()

    return _pcall(body, name=name, out_shape=[_sds((3,) + p.shape[1:], p.dtype) for p in ps],
                  in_specs=[ANY] * ng, out_specs=[ANY] * ng,
                  scratch_shapes=[pltpu.SemaphoreType.DMA((3 * ng,)), pltpu.SemaphoreType.DMA((3 * ng,))],
                  compiler_params=_cp())(*ps)


def _rs_share(name, ss):
    ng = len(ss)

    def body(*refs):
        o_refs = refs[ng:2 * ng]
        send_sems, recv_sems = refs[2 * ng:]
        x, y, c, _ = _place()
        cps = []
        for t in range(ng):
            lh = o_refs[t].shape[1] // 2
            mine = o_refs[t].at[:, _half(c, lh)]
            cp = pltpu.make_async_remote_copy(
                src_ref=mine, dst_ref=mine, send_sem=send_sems.at[t], recv_sem=recv_sems.at[t],
                device_id=(x, y, 1 - c), device_id_type=MESH)
            cp.start()
            cps.append(cp)
        for t in range(ng):
            lh = o_refs[t].shape[1] // 2
            cps[t].wait_send()
            theirs = o_refs[t].at[:, _half(1 - c, lh)]
            pltpu.make_async_remote_copy(
                src_ref=theirs, dst_ref=theirs, send_sem=send_sems.at[t], recv_sem=recv_sems.at[t],
                device_id=(x, y, 1 - c), device_id_type=MESH).wait_recv()

    return _pcall(body, name=name, out_shape=[_sds(s.shape, s.dtype) for s in ss],
                  in_specs=[ANY] * ng, out_specs=[ANY] * ng, input_output_aliases={t: t for t in range(ng)},
                  scratch_shapes=[pltpu.SemaphoreType.DMA((ng,)), pltpu.SemaphoreType.DMA((ng,))],
                  compiler_params=_cp())(*ss)


HBM = pl.BlockSpec(memory_space=pltpu.HBM)
SEMS = pl.BlockSpec(memory_space=pltpu.SEMAPHORE)
EFFECT = pltpu.SideEffectType.DATAFLOW_SIDE_EFFECTING
TOKEN = (8, 128)


def _in_hbm(a):
    return pltpu.with_memory_space_constraint(a, pltpu.HBM)


def _split_start(name, srcs, lands, copies, after):
    ns, nl, na = len(srcs), len(lands), len(after)
    ncopies = len(copies([s for s in srcs], [l for l in lands], probe=True))

    def body(*refs):
        src_refs, land_refs = refs[:ns], refs[ns:ns + nl]
        send_sems, recv_sems = refs[ns + nl + na], refs[ns + nl + na + 1]
        token = refs[-1]
        for k, (src, dst, to) in enumerate(copies(src_refs, land_refs)):
            pltpu.make_async_remote_copy(src_ref=src, dst_ref=dst, send_sem=send_sems.at[k], recv_sem=recv_sems.at[k],
                                         device_id=to, device_id_type=MESH).start()
        token[...] = jnp.zeros_like(token)

    thru = [pltpu.HBM(a.shape, a.dtype) for a in list(srcs) + list(lands)]
    outs = _pcall(body, name=name,
                  out_shape=(pltpu.SemaphoreType.DMA((ncopies,)), pltpu.SemaphoreType.DMA((ncopies,)), *thru, _sds(TOKEN)),
                  in_specs=[HBM] * (ns + nl) + [ANY] * na,
                  out_specs=(SEMS, SEMS, *([HBM] * (ns + nl)), pl.BlockSpec(memory_space=pltpu.VMEM)),
                  input_output_aliases={t: 2 + t for t in range(ns + nl)},
                  compiler_params=pltpu.CompilerParams(has_side_effects=EFFECT))(
        *[_in_hbm(a) for a in list(srcs) + list(lands)], *after)
    return outs[0], outs[1], list(outs[2:2 + ns]), list(outs[2 + ns:2 + ns + nl]), outs[-1]


def _split_wait(name, started, copies, after):
    send_sems, recv_sems, srcs, lands, _ = started
    ns, nl, na = len(srcs), len(lands), len(after)

    def body(*refs):
        src_refs, land_refs = refs[:ns], refs[ns:ns + nl]
        send_sems_ref, recv_sems_ref = refs[ns + nl], refs[ns + nl + 1]
        for k, (src, dst, to) in enumerate(copies(src_refs, land_refs)):
            cp = pltpu.make_async_remote_copy(src_ref=src, dst_ref=dst, send_sem=send_sems_ref.at[k],
                                              recv_sem=recv_sems_ref.at[k], device_id=to, device_id_type=MESH)
            cp.wait_send()
            cp.wait_recv()

    thru = [pltpu.HBM(a.shape, a.dtype) for a in list(srcs) + list(lands)]
    outs = _pcall(body, name=name, out_shape=tuple(thru),
                  in_specs=[HBM] * (ns + nl) + [SEMS, SEMS] + [ANY] * na, out_specs=tuple([HBM] * (ns + nl)),
                  input_output_aliases={t: t for t in range(ns + nl)},
                  compiler_params=pltpu.CompilerParams(has_side_effects=EFFECT))(
        *srcs, *lands, send_sems, recv_sems, *after)
    return list(outs[:ns]), list(outs[ns:])


def _pair_copies(n):
    def copies(src_refs, land_refs, probe=False):
        if probe:
            return [None] * n
        x, y, c, _ = _place()
        return [(src_refs[t].at[:, _half(1 - c, src_refs[t].shape[1] // 2)], land_refs[t], (x, y, 1 - c))
                for t in range(n)]
    return copies


def _gather_copies(n):
    def copies(src_refs, land_refs, probe=False):
        if probe:
            return [None] * (4 * n)
        x, y, c, chips = _place()
        me_chip = _chip_index(x, y)
        out = []
        for t in range(n):
            for to in [(*chip, c) for chip in chips] + [(x, y, 1 - c)]:
                out.append((src_refs[t], land_refs[t].at[me_chip], to))
        return out
    return copies


def _scatter_copies(n):
    def copies(src_refs, land_refs, probe=False):
        if probe:
            return [None] * (3 * n)
        x, y, c, chips = _place()
        out = []
        for t in range(n):
            for k, chip in enumerate(chips):
                out.append((src_refs[t].at[_chip_index(*chip)], land_refs[t].at[k], (*chip, c)))
        return out
    return copies


def _row_block(r, cn):
    if r % 8:
        return r
    best = 8
    for d in range(8, r + 1, 8):
        if r % d == 0 and d * cn * 4 <= (1 << 20):
            best = d
    return best


def _add_half(name, g, a, idx):
    j, rh, cn = a.shape
    tr = _row_block(rh, cn)
    nb = rh // tr

    def body(i_ref, g_ref, a_ref, o_ref):
        o_ref[...] = (g_ref[...] + a_ref[...]).astype(BF16)

    blk = (None, tr, cn)
    gs = pltpu.PrefetchScalarGridSpec(
        num_scalar_prefetch=1, grid=(j, nb),
        in_specs=[pl.BlockSpec(blk, lambda jj, i, i_ref: (jj, i_ref[0] * nb + i, 0)),
                  pl.BlockSpec(blk, lambda jj, i, i_ref: (jj, i, 0))],
        out_specs=pl.BlockSpec(blk, lambda jj, i, i_ref: (jj, i, 0)))
    return _pcall(body, name=name, grid_spec=gs, out_shape=_sds(a.shape, BF16),
                  compiler_params=_cp(("parallel", "parallel")))(idx, g, a)


def _sum_final(name, g, a, b, idx, buf, lyr, nlyr):
    _, r, cn = g.shape
    rh = r // 2
    tr = _row_block(rh, cn)
    nb = rh // tr

    def body(*refs):
        g_ref, a_ref, b_ref = refs[1:4]
        o_ref = refs[-1]
        own = g_ref[...] + a_ref[...]
        o_ref[...] = (own + b_ref[0].astype(F32)) + (b_ref[1].astype(F32) + b_ref[2].astype(F32))

    blk = (None, tr, cn)
    in_specs = [pl.BlockSpec(blk, lambda i, i_ref: (i_ref[1], i_ref[0] * nb + i, 0)),
                pl.BlockSpec(blk, lambda i, i_ref: (i_ref[1], i, 0)),
                pl.BlockSpec((3, tr, cn), lambda i, i_ref: (0, i, 0))]
    args = [idx, g, a, b]
    kw = {}
    if buf is not None:
        in_specs.append(ANY)
        args.append(buf)
        kw["input_output_aliases"] = {4: 0}
    gs = pltpu.PrefetchScalarGridSpec(
        num_scalar_prefetch=1, grid=(nb,), in_specs=in_specs,
        out_specs=pl.BlockSpec(blk, lambda i, i_ref: (lyr, i_ref[0] * nb + i, 0)))
    return _pcall(body, name=name, grid_spec=gs, out_shape=_sds((nlyr, r, cn)),
                  compiler_params=_cp(("parallel",)), **kw)(*args)


def _sum8(name, g):
    _, r, n = g.shape
    tr = 8

    def body(g_ref, o_ref):
        o_ref[...] = ((g_ref[0] + g_ref[1]) + (g_ref[2] + g_ref[3])) + ((g_ref[4] + g_ref[5]) + (g_ref[6] + g_ref[7]))

    return _pcall(body, name=name, grid=(r // tr,), in_specs=[pl.BlockSpec((8, tr, n), lambda i: (0, i, 0))],
                  out_specs=pl.BlockSpec((tr, n), lambda i: (i, 0)), out_shape=_sds((r, n)),
                  compiler_params=_cp(("parallel",)))(g)


def _ada_mod(name, c16, ada_w, bias):
    nl, dm, n = ada_w.shape

    def body(c_ref, w_ref, b_ref, o_ref):
        o_ref[...] = _dot(_silu(c_ref[...]), w_ref[...], NN) + b_ref[...]

    return _pcall(body, name=name, grid=(nl,),
                  in_specs=[_full(c16.shape), pl.BlockSpec((None, dm, n), lambda i: (i, 0, 0)),
                            pl.BlockSpec((None, 1, n), lambda i: (i, 0, 0))],
                  out_specs=pl.BlockSpec((None, 16, n), lambda i: (i, 0, 0)), out_shape=_sds((nl, 16, n)),
                  compiler_params=_cp(("parallel",)))(c16, ada_w, bias)


def _ada_bwd(name, c16, dmod, ada_w):
    nl, dm, n = ada_w.shape

    def body(c_ref, d_ref, w_ref, gw_ref, dc_ref):
        @pl.when(pl.program_id(0) == 0)
        def _():
            dc_ref[...] = jnp.zeros_like(dc_ref)

        dv = d_ref[...]
        gw_ref[...] = _dot(_silu(c_ref[...]), dv, TN)
        dc_ref[...] += _dot(dv, w_ref[...], NT)

    return _pcall(body, name=name, grid=(nl,),
                  in_specs=[_full(c16.shape), pl.BlockSpec((None, 16, n), lambda i: (i, 0, 0)),
                            pl.BlockSpec((None, dm, n), lambda i: (i, 0, 0))],
                  out_specs=[pl.BlockSpec((None, dm, n), lambda i: (i, 0, 0)), _full((16, dm))],
                  out_shape=[_sds((nl, dm, n)), _sds((16, dm))],
                  compiler_params=_cp(("arbitrary",)))(c16, dmod, ada_w)


def _rowsum16(name, dmod):
    nl, _, n = dmod.shape

    def body(d_ref, o_ref):
        o_ref[...] = _colsum(d_ref[...])

    return _pcall(body, name=name, grid=(nl,), in_specs=[pl.BlockSpec((None, 16, n), lambda i: (i, 0, 0))],
                  out_specs=pl.BlockSpec((None, 1, n), lambda i: (i, 0, 0)), out_shape=_sds((nl, 1, n)),
                  compiler_params=_cp(("parallel",)))(dmod)


def _cctx_grad(name, parts, c_ctx):
    def body(p_ref, c_ref, o_ref):
        tot = (p_ref[0:1, :] + p_ref[1:2, :]) + (p_ref[2:3, :] + p_ref[3:4, :])
        o_ref[...] = tot * _dsilu(c_ref[...])

    return _pcall(body, name=name, out_shape=_sds(c_ctx.shape), compiler_params=_cp())(parts, c_ctx)


def _adamw(name, w, g, m, v):
    shape = w.shape
    cn = shape[-1]
    r = math.prod(shape[:-1]) if len(shape) > 1 else 1
    tr = _row_block(r, cn)
    c1 = 1.0 - ADAM_B1 ** ADAM_STEP
    c2 = 1.0 - ADAM_B2 ** ADAM_STEP

    def body(w_ref, g_ref, m_ref, v_ref, d_ref, mo_ref, vo_ref):
        gv = g_ref[...]
        mn = ADAM_B1 * m_ref[...] + (1.0 - ADAM_B1) * gv
        vn = ADAM_B2 * v_ref[...] + (1.0 - ADAM_B2) * (gv * gv)
        d_ref[...] = -ADAM_LR * ((mn / c1) / (jnp.sqrt(vn / c2) + ADAM_EPS) + ADAM_WD * w_ref[...])
        mo_ref[...] = mn
        vo_ref[...] = vn

    blk = pl.BlockSpec((tr, cn), lambda i: (i, 0))
    o = _sds((r, cn))
    outs = _pcall(body, name=name, grid=(r // tr,), in_specs=[blk] * 4, out_specs=[blk] * 3, out_shape=[o, o, o],
                  compiler_params=_cp(("parallel",)))(*[a.reshape(r, cn) for a in (w, g, m, v)])
    return tuple(a.reshape(shape) for a in outs)


def _local_step(xs, target, modt, nw, fnw, get_w, get_ffn, put_g, ev, od, lc):
    t, dm = xs.shape
    nct = lc // ROW_TILE
    depth = nw.shape[0]
    cs, sn = _rope_tables(t, lc)
    saved = []
    x_in, x1p, fp = xs, None, None
    for i in range(depth):
        j, even = i // 2, i % 2 == 0
        tag = f"l{i}"
        w, deps = get_w(i, [fp] if i else [])
        if i == 0:
            _, h = _rnm(tag + "_norm1", x_in, None, None, 0, modt[0], 0, 1, nw[0, 0], nct, deps)
        else:
            x_in, h = _rnm(tag + "_norm1", x1p, fp, modt[i - 1], 5, modt[i], 0, 1, nw[i, 0], nct, deps)
        s = dict(x=x_in, h=h, w=w)
        if even:
            p = _mm_cols(tag + "_in", h, w["in"])
            q, k, v = _even_qkv(tag + "_qkv", p, cs, sn)
            of, ob, ss = _retention_fwd(tag + "_ret", q, k, v, ev["lgb"][j], lc)
            mix, yc = _even_mix(tag + "_mix", p, of, ob, ev["cw"][j], ev["lnw"][j], ev["lnb"][j], nct)
            y = _mm_full(tag + "_out", mix, w["out"], NN)
            s.update(p=p, q=q, k=k, v=v, of=of, ob=ob, ss=ss, yc=yc)
        else:
            p = _mm_cols(tag + "_in", h, w["in"])
            mix, m = _odd_mix(tag + "_mix", p, od["pw"][j], od["ps"][j], od["lnw"][j], od["lnb"][j],
                              od["sgw"][j], od["sgb"][j], nct, lc)
            y = _mm_full(tag + "_out", mix, w["out"], NN)
            s.update(p=p, m=m)
        x1, h2 = _rnm(tag + "_norm2", x_in, y, modt[i], 2, modt[i], 3, 4, nw[i, 1], nct)
        w.update(get_ffn(i, [y]))
        a, gt, up = _ffn_up(tag + "_ffn_up", h2, w["gate"], w["up"])
        f = _mm_full(tag + "_ffn_down", a, w["down"], NN)
        s.update(mix=mix, y=y, x1=x1, h2=h2, a=a, gt=gt, up=up, f=f)
        saved.append(s)
        x1p, fp = x1, f

    loss_blk, dx, df, fin_s = _fin("final", x1p, fp, modt[depth - 1], 5, fnw, target, nct)

    deps = []
    dmod = [[None] * 6 for _ in range(depth)]
    dnw = [[None, None] for _ in range(depth)]
    zero2 = jnp.zeros((2, dm), F32)
    dmod[depth - 1][5] = jnp.stack([zero2[0], fin_s[0]])
    small = dict(dfnw=fin_s[1], ev=[], od=[])
    for i in reversed(range(depth)):
        j, even = i // 2, i % 2 == 0
        tag = f"l{i}b"
        s = saved[i]
        w = s["w"]
        fh = w["down"].shape[0] // 2
        g = {}
        dgt, dup = _ffn_down_bwd(tag + "_ffn_down", df, w["down"], s["gt"], s["up"])
        g["down"] = _wgrad_rows(tag + "_gdown", s["a"], fh, df)
        g["gate"] = _wgrad_rows(tag + "_ggate", dgt, fh, s["h2"])
        g["up"] = _wgrad_rows(tag + "_gup", dup, fh, s["h2"])
        dh2 = _ffn_in_bwd(tag + "_ffn_in", dgt, dup, w["gate"], w["up"])
        dx1, dy, s2 = _bnm(tag + "_norm2", s["x1"], dh2, dx, s["y"], modt[i], 3, 4, modt[i], 2, nw[i, 1], nct)
        dmod[i][3], dmod[i][4], dmod[i][2] = s2[:, 0], s2[:, 1], s2[:, 2]
        dnw[i][1] = s2[1, 3]
        dmix = _mm_full(tag + "_out", dy, w["out"], NT)
        g["out"] = _wgrad_rows(tag + "_gout", s["mix"], w["out"].shape[0] // 2, dy)
        if even:
            do, dg, dyc, lns = _even_mix_bwd1(tag + "_mix1", dmix, s["p"], s["of"], s["ob"], s["yc"],
                                              ev["lnw"][j], ev["lnb"][j])
            da, dgb, dcw = _even_conv_bwd(tag + "_conv", dyc, s["p"], ev["cw"][j], nct)
            dqf, dkf, dvf, dqb, dkb, dvb, dl = _retention_bwd(tag + "_ret", s["q"], s["k"], s["v"], do, s["ss"],
                                                              ev["lgb"][j], lc)
            dp = _even_dp(tag + "_dp", (dqf, dqb), (dkf, dkb), (dvf, dvb), dg, da, dgb, cs, sn)
            small["ev"].append(dict(lnw=lns[0], lnb=lns[1], cw=dcw, dl=dl[:, 0]))
        else:
            dm_, dpd, vec, dpw, dsgw, dsgb = _odd_mix_bwd1(tag + "_mix1", dmix, s["p"], s["m"], od["pw"][j], od["ps"][j],
                                                           od["lnw"][j], od["lnb"][j], od["sgw"][j], od["sgb"][j])
            dp = _odd_dp(tag + "_dp", dm_, dpd, nct, lc)
            small["od"].append(dict(ps=vec[0], lnw=vec[1], lnb=vec[2], pw=dpw, sgw=dsgw, sgb=dsgb[:, :, 0]))
        dh = _mm_cols_bwd(tag + "_in", dp, w["in"])
        g["in"] = _wgrad_cols(tag + "_gin", s["h"], dp, w["in"].shape[0])
        deps = put_g(i, g)
        if i > 0:
            dx, df, s1 = _bnm(tag + "_norm1", s["x"], dh, dx1, saved[i - 1]["f"], modt[i], 0, 1, modt[i - 1], 5,
                              nw[i, 0], nct, deps)
            dmod[i - 1][5] = s1[:, 2]
        else:
            dx, _, s1 = _bnm(tag + "_norm1", s["x"], dh, dx1, None, modt[0], 0, 1, None, 0, nw[0, 0], nct, deps)
        dmod[i][0], dmod[i][1] = s1[:, 0], s1[:, 1]
        dnw[i][0] = s1[1, 3]
    small["ev"].reverse()
    small["od"].reverse()
    dmod_t = jnp.stack([jnp.concatenate([jnp.stack(rows, axis=1), jnp.zeros((2, 2, dm), F32)], axis=1) for rows in dmod])
    small["dmod"] = dmod_t
    small["dnw"] = jnp.stack([jnp.stack(r) for r in dnw])
    return loss_blk, dx, small


WEIGHTS = ["c_ctx", "ada_w", "ada_b", "norm_w", "even_w_in", "even_w_out", "ret_decay_logit", "conv_dw_w",
           "conv_ln_w", "conv_ln_b", "odd_w_in", "odd_w_out", "pool_w", "pool_scale", "sg_ln_w", "sg_ln_b",
           "sg_w", "sg_b", "ffn_w_gate", "ffn_w_up", "ffn_w_down", "final_norm_w"]
BIG = dict(even_in="even_w_in", even_out="even_w_out", odd_in="odd_w_in", odd_out="odd_w_out",
           gate="ffn_w_gate", up="ffn_w_up", down="ffn_w_down")


def _rows(a, width=1024):
    flat = a.reshape(-1)
    n = flat.shape[0]
    per = 8 * width
    tot = -(-n // per) * per
    return jnp.pad(flat, (0, tot - n)).reshape(tot // width, width)


def _unshard(parts, lead):
    nl = len(lead)
    perm = tuple(range(1, nl + 1)) + (0, nl + 1)
    return parts.transpose(perm).reshape(tuple(lead) + (4 * parts.shape[-1],))


def _my_cols(a, chip, n):
    start = (0,) * (a.ndim - 1) + (chip * n,)
    return lax.dynamic_slice(a, start, a.shape[:-1] + (n,))


def kernel(x, c, ctx, c_ctx, ada_w, ada_b, norm_w, even_w_in, even_w_out, ret_decay_logit, conv_dw_w, conv_ln_w, conv_ln_b, odd_w_in, odd_w_out, pool_w, pool_scale, sg_ln_w, sg_ln_b, sg_w, sg_b, ffn_w_gate, ffn_w_up, ffn_w_down, final_norm_w, loss_target, m_c_ctx, m_ada_w, m_ada_b, m_norm_w, m_even_w_in, m_even_w_out, m_ret_decay_logit, m_conv_dw_w, m_conv_ln_w, m_conv_ln_b, m_odd_w_in, m_odd_w_out, m_pool_w, m_pool_scale, m_sg_ln_w, m_sg_ln_b, m_sg_w, m_sg_b, m_ffn_w_gate, m_ffn_w_up, m_ffn_w_down, m_final_norm_w, v_c_ctx, v_ada_w, v_ada_b, v_norm_w, v_even_w_in, v_even_w_out, v_ret_decay_logit, v_conv_dw_w, v_conv_ln_w, v_conv_ln_b, v_odd_w_in, v_odd_w_out, v_pool_w, v_pool_scale, v_sg_ln_w, v_sg_ln_b, v_sg_w, v_sg_b, v_ffn_w_gate, v_ffn_w_up, v_ffn_w_down, v_final_norm_w):
    wv = dict(c_ctx=c_ctx, ada_w=ada_w, ada_b=ada_b, norm_w=norm_w, even_w_in=even_w_in, even_w_out=even_w_out,
              ret_decay_logit=ret_decay_logit, conv_dw_w=conv_dw_w, conv_ln_w=conv_ln_w, conv_ln_b=conv_ln_b,
              odd_w_in=odd_w_in, odd_w_out=odd_w_out, pool_w=pool_w, pool_scale=pool_scale, sg_ln_w=sg_ln_w,
              sg_ln_b=sg_ln_b, sg_w=sg_w, sg_b=sg_b, ffn_w_gate=ffn_w_gate, ffn_w_up=ffn_w_up,
              ffn_w_down=ffn_w_down, final_norm_w=final_norm_w)
    mv = dict(zip(WEIGHTS, (m_c_ctx, m_ada_w, m_ada_b, m_norm_w, m_even_w_in, m_even_w_out, m_ret_decay_logit,
                            m_conv_dw_w, m_conv_ln_w, m_conv_ln_b, m_odd_w_in, m_odd_w_out, m_pool_w, m_pool_scale,
                            m_sg_ln_w, m_sg_ln_b, m_sg_w, m_sg_b, m_ffn_w_gate, m_ffn_w_up, m_ffn_w_down,
                            m_final_norm_w)))
    vv = dict(zip(WEIGHTS, (v_c_ctx, v_ada_w, v_ada_b, v_norm_w, v_even_w_in, v_even_w_out, v_ret_decay_logit,
                            v_conv_dw_w, v_conv_ln_w, v_conv_ln_b, v_odd_w_in, v_odd_w_out, v_pool_w, v_pool_scale,
                            v_sg_ln_w, v_sg_ln_b, v_sg_w, v_sg_b, v_ffn_w_gate, v_ffn_w_up, v_ffn_w_down,
                            v_final_norm_w)))
    xi, yi, ci = lax.axis_index("x"), lax.axis_index("y"), lax.axis_index("c")
    chip = 2 * xi + yi
    dev = 4 * xi + 2 * yi + ci
    dm = x.shape[-1]
    lc = ctx.shape[1]
    depth = ada_w.shape[0]
    n_ada = ada_w.shape[-1]

    cw_pad = jnp.pad(conv_dw_w, ((0, 0), (0, 1), (0, 0)))
    vec3 = jnp.stack([pool_scale, sg_ln_w, sg_ln_b])
    pack1 = jnp.concatenate([_rows(c), _rows(norm_w), _rows(cw_pad), _rows(vec3)], axis=0)
    g1 = _all_gather8("gather_small", pack1).reshape(8, 32, dm)
    c_all = g1[:, 0]
    per_chip = g1[0::2]
    norm_full = _unshard(per_chip[:, 8:10].reshape(4, depth, 2, dm // 4), (depth, 2))
    cw_full = _unshard(per_chip[:, 16:24].reshape(4, 2, CONV_K + 1, 128), (2, CONV_K + 1))
    vec_full = _unshard(per_chip[:, 24, :768].reshape(4, 3, 2, 128), (3, 2))

    c16 = jnp.concatenate([c_all, c_ctx[None, :], jnp.zeros((7, dm), F32)], axis=0)
    mod_sh = _ada_mod("ada_mod", c16, ada_w, _my_cols(ada_b, chip, n_ada)[:, None, :])
    g2 = _all_gather8("gather_mod", mod_sh.reshape(depth * 16, n_ada)).reshape(8, depth, 16, n_ada)
    mod_full = _unshard(g2[0::2], (depth, 16))
    mod_x = lax.dynamic_index_in_dim(mod_full, dev, axis=1, keepdims=False).reshape(depth, 6, dm)
    mod_c = mod_full[:, 8].reshape(depth, 6, dm)
    modt = jnp.pad(jnp.stack([mod_c, mod_x], axis=1), ((0, 0), (0, 0), (0, 2), (0, 0)))

    names = list(BIG)
    tr_names = ("gate", "up")
    shard = {k: (jnp.swapaxes(wv[BIG[k]], 1, 2) if k in tr_names else wv[BIG[k]]).astype(BF16) for k in names}
    roles = ("in", "out", "gate", "up", "down")

    def layer_keys(i):
        mixer = ("even_in", "even_out") if i % 2 == 0 else ("odd_in", "odd_out")
        return [(k, i // 2) for k in mixer] + [(k, i) for k in ("gate", "up", "down")]

    def as_used(got):
        return {r: (g if r == "in" else g.reshape(4 * g.shape[1], g.shape[2])) for r, g in zip(roles, got)}

    started = {}

    def start_gather(tag, keys, before):
        srcs = [shard[k][l] for k, l in keys]
        lands = [lax.empty((4,) + s.shape, s.dtype) for s in srcs]
        return _split_start(f"gather_start{tag}", srcs, lands, _gather_copies(len(srcs)), before)

    def get_w(i, after):
        if i > 0:
            got = _split_wait(f"gather_wait{i}", started[i], _gather_copies(len(roles)), after)[1]
            return as_used(got), []
        got = _gather_weights("gather_w0", [shard[k][l] for k, l in layer_keys(0)[:2]], [modt])
        started["0f"] = start_gather("0f", layer_keys(0)[2:], [got[0]])
        last = started["0f"]
        for li in range(1, depth):
            started[li] = last = start_gather(li, layer_keys(li), [got[0], last[4]])
        return as_used(got), [last[4]]

    def get_ffn(i, after):
        if i > 0:
            return {}
        got = _split_wait("gather_wait0f", started["0f"], _gather_copies(3), after)[1]
        return {r: g.reshape(4 * g.shape[1], g.shape[2]) for r, g in zip(roles[2:], got)}

    idx = jnp.stack([ci, chip]).astype(jnp.int32)
    pairs, pending = {}, {}

    def finish_pair(i, after):
        glist, from_sib = _split_wait(f"pair_wait{i}", pairs[i], _pair_copies(len(roles)), after)
        pair = [_add_half(f"rs_add{i}_{r}", gl, a, idx) for r, gl, a in zip(roles, glist, from_sib)]
        lands = [lax.empty((3,) + p.shape[1:], p.dtype) for p in pair]
        st = _split_start(f"rs_start{i}", pair, lands, _scatter_copies(len(roles)), [])
        pending[i] = (glist, from_sib, st)
        return [st[4]]

    def put_g(i, g):
        glist = [g[r].reshape(4, -1, g[r].shape[-1]) for r in roles]
        lands = [lax.empty((4, gl.shape[1] // 2, gl.shape[2]), gl.dtype) for gl in glist]
        pairs[i] = _split_start(f"pair_start{i}", glist, lands, _pair_copies(len(roles)), [])
        tokens = [pairs[i][4]]
        if i + 1 < depth:
            tokens += finish_pair(i + 1, [pairs[i][4]])
        return tokens

    ev = dict(lgb=jnp.broadcast_to(ret_decay_logit.reshape(-1, 2 * HEADS)[:, :, None], (ret_decay_logit.shape[0], 2 * HEADS, HEAD_DIM)),
              cw=cw_full, lnw=conv_ln_w[:, None, :], lnb=conv_ln_b[:, None, :])
    od = dict(pw=pool_w, ps=vec_full[0][:, None, :], lnw=vec_full[1][:, None, :], lnb=vec_full[2][:, None, :],
              sgw=sg_w, sgb=jnp.broadcast_to(sg_b[:, :, :, None], sg_b.shape + (GC,)))
    xs = jnp.concatenate([ctx[0], x[0]], axis=0)
    loss_blk, dxs, small = _local_step(xs, loss_target[0], modt, norm_full[:, :, None, :], final_norm_w[None, :],
                                       get_w, get_ffn, put_g, ev, od, lc)
    last_tokens = finish_pair(0, [dxs])

    misc = jnp.stack([
        small["dfnw"], jnp.broadcast_to(loss_blk[0, 0], (dm,)),
        jnp.concatenate([e["lnw"] for e in small["ev"]]), jnp.concatenate([e["lnb"] for e in small["ev"]]),
        jnp.concatenate([o["ps"] for o in small["od"]]), jnp.concatenate([o["lnw"] for o in small["od"]]),
        jnp.concatenate([o["lnb"] for o in small["od"]]),
        jnp.pad(jnp.concatenate([e["dl"] for e in small["ev"]]), (0, dm - 4 * HEADS)),
        jnp.stack([o["sgb"] for o in small["od"]]).reshape(-1)])
    pack2 = jnp.concatenate([
        _rows(small["dmod"]), _rows(small["dnw"]), _rows(misc), _rows(jnp.stack([e["cw"] for e in small["ev"]])),
        _rows(jnp.stack([o["pw"] for o in small["od"]])), _rows(jnp.stack([o["sgw"] for o in small["od"]]))], axis=0)
    n2 = pack2.shape[0]
    g3 = _all_gather8("gather_grads", pack2, last_tokens)
    tot = _sum8("sum_grads", g3.reshape(8, n2, dm))
    r_mod = depth * 16
    o_nw, o_misc = r_mod, r_mod + 8
    o_cw = o_misc + 16
    o_pw = o_cw + 2 * (CONV_K + 1) // 2
    o_sgw = o_pw + 128
    dmod_sum = tot[:r_mod].reshape(depth, 2, 8, dm)
    dmod_dev = g3.reshape(8, n2, dm)[:, :r_mod].reshape(8, depth, 2, 8, dm)
    dm_x = dmod_dev[:, :, 1, :6].reshape(8, depth, 6 * dm).transpose(1, 0, 2)
    dm_c = dmod_sum[:, 0, :6].reshape(depth, 1, 6 * dm)
    dmod16 = jnp.concatenate([dm_x, dm_c, jnp.zeros((depth, 7, 6 * dm), F32)], axis=1)
    g_ada_b = _rowsum16("ada_b_grad", dmod16)[:, 0]
    g_ada_w, dc16 = _ada_bwd("ada_bwd", c16, _my_cols(dmod16, chip, n_ada), ada_w)
    g4 = _all_gather8("gather_cctx", dc16[8:16]).reshape(8, 8, dm)
    g_c_ctx = _cctx_grad("cctx_grad", g4[0::2, 0], c_ctx[None, :])[0]

    misc_t = tot[o_misc:o_misc + 16]
    half = lambda row: misc_t[row].reshape(2, dm // 2)
    grads = dict(
        c_ctx=g_c_ctx, ada_w=g_ada_w, ada_b=g_ada_b,
        norm_w=_my_cols(tot[o_nw:o_nw + 8].reshape(depth, 2, dm), chip, dm // 4),
        ret_decay_logit=misc_t[7, :4 * HEADS].reshape(ret_decay_logit.shape),
        conv_dw_w=_my_cols(tot[o_cw:o_cw + 2 * (CONV_K + 1) // 2].reshape(2, CONV_K + 1, dm // 2)[:, :CONV_K], chip, 128),
        conv_ln_w=half(2), conv_ln_b=half(3),
        pool_w=tot[o_pw:o_pw + 128].reshape(pool_w.shape),
        pool_scale=_my_cols(half(4), chip, 128), sg_ln_w=_my_cols(half(5), chip, 128), sg_ln_b=_my_cols(half(6), chip, 128),
        sg_w=tot[o_sgw:o_sgw + 128].reshape(sg_w.shape), sg_b=misc_t[8].reshape(sg_b.shape),
        final_norm_w=misc_t[0])
    loss = misc_t[1, 0]

    reduced = {k: None for k in names}
    for i in reversed(range(depth)):
        glist, from_sib, st = pending[i]
        slots = _split_wait(f"rs_wait{i}", st, _scatter_copies(len(roles)), [g_c_ctx])[1]
        for (k, l), g, a, b in zip(layer_keys(i), glist, from_sib, slots):
            reduced[k] = _sum_final(f"rs_sum_{k}{l}", g, a, b, idx, reduced[k], l, shard[k].shape[0])
    shards = dict(zip(names, _rs_share("rs_share", [reduced[k] for k in names])))

    deltas, new_m, new_v = {}, {}, {}
    for k in names:
        n = BIG[k]
        if k in tr_names:
            tr = lambda a: jnp.swapaxes(a, 1, 2)
            outs = _adamw("adamw_" + n, tr(wv[n]), shards[k], tr(mv[n]), tr(vv[n]))
            grads[n] = tr(shards[k])
            deltas[n], new_m[n], new_v[n] = (tr(o) for o in outs)
        else:
            grads[n] = shards[k]
    for n in WEIGHTS:
        if n not in deltas:
            deltas[n], new_m[n], new_v[n] = _adamw("adamw_" + n, wv[n], grads[n], mv[n], vv[n])
    grad_x = dxs[lc:][None]
    return (loss, grad_x, *[grads[n] for n in WEIGHTS], *[deltas[n] for n in WEIGHTS],
            *[new_m[n] for n in WEIGHTS], *[new_v[n] for n in WEIGHTS])
```

```python
import functools
import math

import jax
import jax.numpy as jnp
from jax import lax
from jax.experimental import pallas as pl
from jax.experimental.pallas import tpu as pltpu

F32 = jnp.float32
BF16 = jnp.bfloat16
MESH = pl.DeviceIdType.MESH

EPS = 1e-6
GRID_W = 64
HEADS = 4
HEAD_DIM = 128
CHUNK = 128
CONV_K = 31
ROPE_BASE = 10000.0
ROPE_PAIRS = (16, 24, 24)
POOL_WINDOWS = (2, 4, 8, 16)
ADAM_LR, ADAM_B1, ADAM_B2, ADAM_EPS, ADAM_WD, ADAM_STEP = 0.001, 0.9, 0.999, 1e-08, 0.01, 10

ROW_TILE = 256
CONV_HALO = 16
POOL_HALO = 8
VMEM_LIMIT = 56 * 1024 * 1024
WGRAD_ROWS = 2304


def _pcall(body, **kw):
    return pl.pallas_call(body, **kw)


def _cp(sem=None, vmem=VMEM_LIMIT):
    if sem is None:
        return pltpu.CompilerParams(vmem_limit_bytes=vmem)
    return pltpu.CompilerParams(dimension_semantics=sem, vmem_limit_bytes=vmem)


def _sds(shape, dtype=F32):
    return jax.ShapeDtypeStruct(tuple(shape), dtype)


def _full(shape):
    nd = len(shape)
    return pl.BlockSpec(tuple(shape), lambda *_: (0,) * nd)


def _sigmoid(x):
    return jax.nn.sigmoid(x)


def _silu(x):
    return x * _sigmoid(x)


def _dsilu(x):
    s = _sigmoid(x)
    return s * (1.0 + x * (1.0 - s))


def _colsum(a):
    return jnp.sum(a, axis=0, keepdims=True)


def _dot(a, b, dn):
    return lax.dot_general(a.astype(BF16), b.astype(BF16), dn, preferred_element_type=F32)


NN = (((1,), (0,)), ((), ()))
NT = (((1,), (1,)), ((), ()))
TN = (((0,), (0,)), ((), ()))


def _mm_tile(t, cap=1152):
    best = 16
    for d in range(16, min(t, cap) + 1, 16):
        if t % d == 0:
            best = d
    return best


def _mm(name, pairs, grid, out_shape, out_spec, dn):
    npairs = len(pairs)
    nk = grid[-1]
    kax = len(grid) - 1
    assert nk == 1 or out_shape.dtype == F32

    def body(*refs):
        ins = refs[:2 * npairs]
        o_ref = refs[2 * npairs]
        tot = None
        for p in range(npairs):
            d = _dot(ins[2 * p][...], ins[2 * p + 1][...], dn)
            tot = d if tot is None else tot + d
        if nk == 1:
            o_ref[...] = tot.astype(o_ref.dtype)
        else:
            k = pl.program_id(kax)

            @pl.when(k == 0)
            def _():
                o_ref[...] = tot

            @pl.when(k != 0)
            def _():
                o_ref[...] += tot

    args, in_specs = [], []
    for a, a_spec, b, b_spec in pairs:
        args += [a, b]
        in_specs += [a_spec, b_spec]
    sem = ("parallel",) * kax + ("arbitrary",)
    return _pcall(body, name=name, grid=grid, in_specs=in_specs, out_specs=out_spec, out_shape=out_shape,
                  compiler_params=_cp(sem))(*args)


def _mm_cols(name, a, w, out_dtype=F32):
    t, k = a.shape
    j, _, n = w.shape
    tm = _mm_tile(t)
    return _mm(name, [(a, pl.BlockSpec((tm, k), lambda i, jj, kk: (i, 0)),
                       w, pl.BlockSpec((None, k, n), lambda i, jj, kk: (jj, 0, 0)))],
               (t // tm, j, 1), _sds((t, j * n), out_dtype), pl.BlockSpec((tm, n), lambda i, jj, kk: (i, jj)), NN)


def _mm_cols_bwd(name, d, w):
    t = d.shape[0]
    j, k, n = w.shape
    tm = _mm_tile(t)
    pairs = [(d, pl.BlockSpec((tm, n), functools.partial(lambda jj, i, u, kk: (i, jj), jj)),
              w, pl.BlockSpec((None, k, n), functools.partial(lambda jj, i, u, kk: (jj, 0, 0), jj))) for jj in range(j)]
    return _mm(name, pairs, (t // tm, 1, 1), _sds((t, k), BF16), pl.BlockSpec((tm, k), lambda i, u, kk: (i, 0)), NT)


def _mm_full(name, a, w, dn, tm=None):
    t, k = a.shape
    n = w.shape[1] if dn is NN else w.shape[0]
    tm = tm or _mm_tile(t)
    return _mm(name, [(a, pl.BlockSpec((tm, k), lambda i, u, kk: (i, 0)), w, _full(w.shape))],
               (t // tm, 1, 1), _sds((t, n)), pl.BlockSpec((tm, n), lambda i, u, kk: (i, 0)), dn)


def _wgrad_cols(name, a, b, j):
    t, k = a.shape
    n = b.shape[1] // j
    tt = _mm_tile(t, WGRAD_ROWS)
    return _mm(name, [(a, pl.BlockSpec((tt, k), lambda jj, u, kk: (kk, 0)),
                       b, pl.BlockSpec((tt, n), lambda jj, u, kk: (kk, jj)))],
               (j, 1, t // tt), _sds((j, k, n)), pl.BlockSpec((None, k, n), lambda jj, u, kk: (jj, 0, 0)), TN)


def _wgrad_rows(name, a, blk, b):
    t, f = a.shape
    n = b.shape[1]
    tt = _mm_tile(t, WGRAD_ROWS)
    return _mm(name, [(a, pl.BlockSpec((tt, blk), lambda jj, u, kk: (kk, jj)),
                       b, pl.BlockSpec((tt, n), lambda jj, u, kk: (kk, 0)))],
               (f // blk, 1, t // tt), _sds((f, n)), pl.BlockSpec((blk, n), lambda jj, u, kk: (jj, 0)), TN)


def _ffn_tiles(t, f):
    return _mm_tile(t, 288), f


def _ffn_up(name, h, wgt, wut):
    t, k = h.shape
    f = wgt.shape[0]
    tm, tn = _ffn_tiles(t, f)

    def body(h_ref, wg_ref, wu_ref, a_ref, gt_ref, up_ref):
        hv = h_ref[...]
        gt = _dot(hv, wg_ref[...], NT)
        up = _dot(hv, wu_ref[...], NT)
        a_ref[...] = (_silu(gt) * up).astype(BF16)
        gt_ref[...] = gt.astype(BF16)
        up_ref[...] = up.astype(BF16)

    wspec = pl.BlockSpec((tn, k), lambda i, jj: (jj, 0))
    ospec = pl.BlockSpec((tm, tn), lambda i, jj: (i, jj))
    o = _sds((t, f), BF16)
    return _pcall(body, name=name, grid=(t // tm, f // tn),
                  in_specs=[pl.BlockSpec((tm, k), lambda i, jj: (i, 0)), wspec, wspec],
                  out_specs=[ospec, ospec, ospec], out_shape=[o, o, o],
                  compiler_params=_cp(("parallel", "parallel")))(h, wgt, wut)


def _ffn_down_bwd(name, df, wd, gt, up):
    t, dm = df.shape
    f = wd.shape[0]
    tm, tn = _ffn_tiles(t, f)

    def body(df_ref, wd_ref, gt_ref, up_ref, dgt_ref, dup_ref):
        da = _dot(df_ref[...], wd_ref[...], NT)
        g = gt_ref[...].astype(F32)
        u = up_ref[...].astype(F32)
        dgt_ref[...] = (da * u * _dsilu(g)).astype(BF16)
        dup_ref[...] = (da * _silu(g)).astype(BF16)

    aspec = pl.BlockSpec((tm, tn), lambda i, jj: (i, jj))
    o = _sds((t, f), BF16)
    return _pcall(body, name=name, grid=(t // tm, f // tn),
                  in_specs=[pl.BlockSpec((tm, dm), lambda i, jj: (i, 0)),
                            pl.BlockSpec((tn, dm), lambda i, jj: (jj, 0)), aspec, aspec],
                  out_specs=[aspec, aspec], out_shape=[o, o],
                  compiler_params=_cp(("parallel", "parallel")))(df, wd, gt, up)


def _ffn_in_bwd(name, dgt, dup, wgt, wut):
    t, f = dgt.shape
    k = wgt.shape[1]
    tm = _mm_tile(t, 576)
    aspec = pl.BlockSpec((tm, f), lambda i, u, kk: (i, 0))
    wspec = pl.BlockSpec((f, k), lambda i, u, kk: (0, 0))
    return _mm(name, [(dgt, aspec, wgt, wspec), (dup, aspec, wut, wspec)], (t // tm, 1, 1), _sds((t, k), BF16),
               pl.BlockSpec((tm, k), lambda i, u, kk: (i, 0)), NN)


def _modrow(ref, row, is_ctx):
    return jnp.where(is_ctx, ref[0, row:row + 1, :], ref[1, row:row + 1, :])


def _rnm(name, x, delta, mod_g, g_row, mod_n, sh_row, sc_row, nw, nct, deps=()):
    t, dm = x.shape
    tm = ROW_TILE
    has = delta is not None
    nd = len(deps)

    def body(*refs):
        refs = refs[:len(refs) - nd - (2 if has else 1)] + refs[len(refs) - (2 if has else 1):]
        if has:
            x_ref, d_ref, mg_ref, m_ref, nw_ref, xo_ref, h_ref = refs
        else:
            x_ref, m_ref, nw_ref, h_ref = refs
        is_ctx = pl.program_id(0) < nct
        xv = x_ref[...]
        if has:
            xv = xv + _modrow(mg_ref, g_row, is_ctx) * d_ref[...]
            xo_ref[...] = xv
        r = lax.rsqrt(jnp.mean(xv * xv, axis=-1, keepdims=True) + EPS)
        hv = (xv * r * nw_ref[...]) * (1.0 + _modrow(m_ref, sc_row, is_ctx)) + _modrow(m_ref, sh_row, is_ctx)
        h_ref[...] = hv.astype(BF16)

    row = pl.BlockSpec((tm, dm), lambda i: (i, 0))
    ins = [x] + ([delta, mod_g] if has else []) + [mod_n, nw] + list(deps)
    in_specs = ([row] + ([row, _full(mod_g.shape)] if has else []) + [_full(mod_n.shape), _full(nw.shape)]
                + [_full(d.shape) for d in deps])
    outs = ([_sds((t, dm))] if has else []) + [_sds((t, dm), BF16)]
    out_specs = ([row] if has else []) + [row]
    res = _pcall(body, name=name, grid=(t // tm,), in_specs=in_specs, out_specs=out_specs, out_shape=outs,
                 compiler_params=_cp(("parallel",)))(*ins)
    return res if has else (None, res[0])


def _bnm(name, xn, dh, dup, yprev, mod_n, sh_row, sc_row, mod_g, g_row, nw, nct, deps=()):
    t, dm = xn.shape
    tm = ROW_TILE
    has = yprev is not None
    nd = len(deps)

    def body(*refs):
        nout = 3 if has else 2
        refs = refs[:len(refs) - nd - nout] + refs[len(refs) - nout:]
        if has:
            x_ref, dh_ref, du_ref, y_ref, mn_ref, mg_ref, nw_ref, dx_ref, dd_ref, s_ref = refs
        else:
            x_ref, dh_ref, du_ref, mn_ref, nw_ref, dx_ref, s_ref = refs
        i = pl.program_id(0)
        is_ctx = i < nct

        @pl.when(i == 0)
        def _():
            s_ref[...] = jnp.zeros_like(s_ref)

        xv = x_ref[...]
        r = lax.rsqrt(jnp.mean(xv * xv, axis=-1, keepdims=True) + EPS)
        xh = xv * r
        w = nw_ref[...]
        sc1 = 1.0 + _modrow(mn_ref, sc_row, is_ctx)
        dhv = dh_ref[...].astype(F32)
        dxh = dhv * sc1 * w
        dx = r * (dxh - xh * jnp.mean(dxh * xh, axis=-1, keepdims=True)) + du_ref[...]
        dx_ref[...] = dx
        parts = [_colsum(dhv), _colsum(dhv * (xh * w))]
        if has:
            dd_ref[...] = (_modrow(mg_ref, g_row, is_ctx) * dx).astype(BF16)
            parts.append(_colsum(dx * y_ref[...]))
        else:
            parts.append(jnp.zeros((1, dm), F32))
        upd = jnp.concatenate(parts + [jnp.zeros((5, dm), F32)], axis=0)
        dnw = jnp.concatenate([jnp.zeros((3, dm), F32), _colsum(dhv * sc1 * xh), jnp.zeros((4, dm), F32)], axis=0)

        @pl.when(is_ctx)
        def _():
            s_ref[0] += upd
            s_ref[1] += dnw

        @pl.when(jnp.logical_not(is_ctx))
        def _():
            s_ref[1] += upd + dnw

    row = pl.BlockSpec((tm, dm), lambda i: (i, 0))
    ins = [xn, dh, dup] + ([yprev] if has else []) + [mod_n] + ([mod_g] if has else []) + [nw] + list(deps)
    in_specs = ([row, row, row] + ([row] if has else []) + [_full(mod_n.shape)]
                + ([_full(mod_g.shape)] if has else []) + [_full(nw.shape)] + [_full(d.shape) for d in deps])
    outs = [_sds((t, dm))] + ([_sds((t, dm), BF16)] if has else []) + [_sds((2, 8, dm))]
    out_specs = [row] + ([row] if has else []) + [_full((2, 8, dm))]
    res = _pcall(body, name=name, grid=(t // tm,), in_specs=in_specs, out_specs=out_specs, out_shape=outs,
                 compiler_params=_cp(("arbitrary",)))(*ins)
    return res if has else (res[0], None, res[1])


def _fin(name, x1, f, mod, g_row, fw, target, nct):
    t, dm = x1.shape
    tm = ROW_TILE

    def body(x_ref, f_ref, m_ref, fw_ref, t_ref, loss_ref, dx_ref, dd_ref, s_ref):
        i = pl.program_id(0)

        @pl.when(i == 0)
        def _():
            s_ref[...] = jnp.zeros_like(s_ref)
            loss_ref[...] = jnp.zeros_like(loss_ref)

        @pl.when(i < nct)
        def _():
            dx_ref[...] = jnp.zeros_like(dx_ref)
            dd_ref[...] = jnp.zeros_like(dd_ref)

        @pl.when(i >= nct)
        def _():
            g = m_ref[1, g_row:g_row + 1, :]
            fv = f_ref[...]
            xv = x_ref[...] + g * fv
            r = lax.rsqrt(jnp.mean(xv * xv, axis=-1, keepdims=True) + EPS)
            xh = xv * r
            w = fw_ref[...]
            err = xh * w - t_ref[...]
            loss_ref[...] += 0.5 * jnp.sum(err * err) / dm
            dout = err * (1.0 / dm)
            dxh = dout * w
            dx = r * (dxh - xh * jnp.mean(dxh * xh, axis=-1, keepdims=True))
            dx_ref[...] = dx
            dd_ref[...] = (g * dx).astype(BF16)
            s_ref[...] += jnp.concatenate([_colsum(dx * fv), _colsum(dout * xh), jnp.zeros((6, dm), F32)], axis=0)

    row = pl.BlockSpec((tm, dm), lambda i: (i, 0))
    trow = pl.BlockSpec((tm, dm), lambda i: (jnp.maximum(i - nct, 0), 0))
    return _pcall(body, name=name, grid=(t // tm,),
                  in_specs=[row, row, _full(mod.shape), _full(fw.shape), trow],
                  out_specs=[_full((8, 128)), row, row, _full((8, dm))],
                  out_shape=[_sds((8, 128)), _sds((t, dm)), _sds((t, dm), BF16), _sds((8, dm))],
                  compiler_params=_cp(("arbitrary",)))(x1, f, mod, fw, target)


def _rope_tables(t, lc):
    l = t - lc
    rows = l // GRID_W
    grid_r = jnp.broadcast_to(jnp.arange(rows, dtype=F32)[:, None], (rows, GRID_W)).reshape(-1)
    grid_c = jnp.broadcast_to(jnp.arange(GRID_W, dtype=F32)[None, :], (rows, GRID_W)).reshape(-1)

    def angles(p_seq, p_row, p_col):
        parts = []
        for p, n in zip((p_seq, p_row, p_col), ROPE_PAIRS):
            freq = ROPE_BASE ** (-jnp.arange(n, dtype=F32) / n)
            parts.append(p[:, None] * freq[None, :])
        return jnp.concatenate(parts, axis=-1)

    zc = jnp.zeros((lc,), F32)
    ang = jnp.concatenate([angles(jnp.arange(lc, dtype=F32), zc, zc),
                           angles(jnp.full((l,), lc, F32), grid_r, grid_c)], axis=0)
    cos, sin = jnp.cos(ang), jnp.sin(ang)
    return jnp.concatenate([cos, cos], axis=-1), jnp.concatenate([-sin, sin], axis=-1)


def _rope(u, cs, sn):
    return u * cs + pltpu.roll(u, HEAD_DIM // 2, 1) * sn


def _rope_t(d, cs, sn):
    return d * cs + pltpu.roll(d * sn, HEAD_DIM // 2, 1)


def _even_qkv(name, p, cs, sn):
    t = p.shape[0]
    tm = ROW_TILE
    w = HEADS * HEAD_DIM
    scale = HEAD_DIM ** -0.5

    def body(q_ref, k_ref, v_ref, cs_ref, sn_ref, qo_ref, ko_ref, vo_ref):
        c, s = cs_ref[...], sn_ref[...]
        for h in range(HEADS):
            sl = slice(h * HEAD_DIM, (h + 1) * HEAD_DIM)
            qo_ref[:, sl] = (_rope(q_ref[:, sl], c, s) * scale).astype(BF16)
            ko_ref[:, sl] = _rope(k_ref[:, sl], c, s).astype(BF16)
        vo_ref[...] = v_ref[...].astype(BF16)

    col = lambda j: pl.BlockSpec((tm, w), lambda i: (i, j))
    tab = pl.BlockSpec((tm, HEAD_DIM), lambda i: (i, 0))
    o = _sds((t, w), BF16)
    return _pcall(body, name=name, grid=(t // tm,), in_specs=[col(0), col(1), col(2), tab, tab],
                  out_specs=[col(0)] * 3, out_shape=[o, o, o], compiler_params=_cp(("parallel",)))(p, p, p, cs, sn)


def _log_sigmoid_row(x):
    e = jnp.exp(-jnp.abs(x))
    l1p = jnp.where(e < 0.01, e * (1.0 - e * (0.5 - e * (1.0 / 3.0))), jnp.log(1.0 + e))
    return jnp.minimum(x, 0.0) - l1p


def _ret_tables(lgb_ref, dm_ref, xi_ref, zt_ref):
    ri = lax.broadcasted_iota(jnp.int32, (CHUNK, CHUNK), 0).astype(F32)
    ci = lax.broadcasted_iota(jnp.int32, (CHUNK, CHUNK), 1).astype(F32)
    for d in range(2):
        for h in range(HEADS):
            idx = d * HEADS + h
            lg = _log_sigmoid_row(lgb_ref[idx:idx + 1, :])
            if d == 0:
                e, mask = ri - ci, ri >= ci
                xe, ze = ri + 1.0, (CHUNK - 1.0) - ri
            else:
                e, mask = ci - ri - 1.0, ci > ri
                xe, ze = (CHUNK - 1.0) - ri, ri
            dm_ref[idx] = jnp.where(mask, jnp.exp(lg * jnp.where(mask, e, 0.0)), 0.0)
            xi_ref[idx] = jnp.exp(lg * xe)
            zt_ref[idx] = jnp.exp(lg * ze)


def _ret_exponents(d):
    ri = lax.broadcasted_iota(jnp.int32, (CHUNK, CHUNK), 0).astype(F32)
    ci = lax.broadcasted_iota(jnp.int32, (CHUNK, CHUNK), 1).astype(F32)
    if d == 0:
        return ri - ci, ri + 1.0, (CHUNK - 1.0) - ri
    return ci - ri - 1.0, (CHUNK - 1.0) - ri, ri


def _bwd_chunk(n, ncc, nc):
    return jnp.where(n < ncc, ncc - 1 - n, nc - 1 - (n - ncc))


def _retention_fwd(name, q, k, v, lgb, lc):
    t, w = q.shape
    nc, ncc = t // CHUNK, lc // CHUNK
    nh = 2 * HEADS

    def body(qf_ref, kf_ref, vf_ref, qb_ref, kb_ref, vb_ref, lgb_ref, of_ref, ob_ref, ss_ref,
             s_ref, dm_ref, xi_ref, zt_ref):
        n = pl.program_id(0)

        @pl.when(n == 0)
        def _():
            s_ref[...] = jnp.zeros_like(s_ref)
            _ret_tables(lgb_ref, dm_ref, xi_ref, zt_ref)

        for d in range(2):
            q_ref, k_ref, v_ref, o_ref = (qf_ref, kf_ref, vf_ref, of_ref) if d == 0 else (qb_ref, kb_ref, vb_ref, ob_ref)
            for h in range(HEADS):
                idx = d * HEADS + h
                sl = slice(h * HEAD_DIM, (h + 1) * HEAD_DIM)
                qv, kv, vv = q_ref[:, sl], k_ref[:, sl], v_ref[:, sl]
                s = s_ref[idx]
                ss_ref[idx] = s
                a = _dot(qv, kv, NT) * dm_ref[idx]
                o = _dot(a, vv, NN) + _dot(qv.astype(F32) * xi_ref[idx], s, NN)
                o_ref[:, sl] = o
                gc = jnp.exp(_log_sigmoid_row(lgb_ref[idx:idx + 1, :]) * float(CHUNK))
                s_ref[idx] = gc * s + _dot(kv.astype(F32) * zt_ref[idx], vv, TN)

    fspec = pl.BlockSpec((CHUNK, w), lambda n: (n, 0))
    bspec = pl.BlockSpec((CHUNK, w), lambda n: (_bwd_chunk(n, ncc, nc), 0))
    tab = pltpu.VMEM((nh, CHUNK, CHUNK), F32)
    return _pcall(body, name=name, grid=(nc,),
                  in_specs=[fspec] * 3 + [bspec] * 3 + [_full((nh, HEAD_DIM))],
                  out_specs=[fspec, bspec, pl.BlockSpec((None, nh, CHUNK, CHUNK), lambda n: (n, 0, 0, 0))],
                  out_shape=[_sds((t, w)), _sds((t, w)), _sds((nc, nh, CHUNK, CHUNK))],
                  scratch_shapes=[tab, tab, tab, tab],
                  compiler_params=_cp(("arbitrary",)))(q, k, v, q, k, v, lgb)


def _retention_bwd(name, q, k, v, do, ss, lgb, lc):
    t, w = q.shape
    nc, ncc = t // CHUNK, lc // CHUNK
    nh = 2 * HEADS

    def body(qf_ref, kf_ref, vf_ref, gf_ref, qb_ref, kb_ref, vb_ref, gb_ref, ss_ref, lgb_ref,
             dqf_ref, dkf_ref, dvf_ref, dqb_ref, dkb_ref, dvb_ref, dl_ref,
             ds_ref, dm_ref, xi_ref, zt_ref, acc_ref):
        n = pl.program_id(0)

        @pl.when(n == 0)
        def _():
            ds_ref[...] = jnp.zeros_like(ds_ref)
            acc_ref[...] = jnp.zeros_like(acc_ref)
            _ret_tables(lgb_ref, dm_ref, xi_ref, zt_ref)

        for d in range(2):
            if d == 0:
                q_ref, k_ref, v_ref, g_ref, dq_ref, dk_ref, dv_ref = qf_ref, kf_ref, vf_ref, gf_ref, dqf_ref, dkf_ref, dvf_ref
            else:
                q_ref, k_ref, v_ref, g_ref, dq_ref, dk_ref, dv_ref = qb_ref, kb_ref, vb_ref, gb_ref, dqb_ref, dkb_ref, dvb_ref
            ee, xe, ze = _ret_exponents(d)
            for h in range(HEADS):
                idx = d * HEADS + h
                sl = slice(h * HEAD_DIM, (h + 1) * HEAD_DIM)
                qv, kv, vv, gv = q_ref[:, sl], k_ref[:, sl], v_ref[:, sl], g_ref[:, sl]
                s = ss_ref[idx]
                dsp = ds_ref[idx]
                dmat, xi, zt = dm_ref[idx], xi_ref[idx], zt_ref[idx]
                qf32, kf32 = qv.astype(F32), kv.astype(F32)
                a = _dot(qv, kv, NT) * dmat
                dar = _dot(gv, vv, NT)
                da = dar * dmat
                t1 = _dot(gv, s, NT)
                t2 = _dot(vv, dsp, NT)
                dq_ref[:, sl] = _dot(da, kv, NN) + xi * t1
                dk_ref[:, sl] = _dot(da, qv, TN) + zt * t2
                dv_ref[:, sl] = _dot(a, gv, TN) + _dot(kf32 * zt, dsp, NN)
                gc = jnp.exp(_log_sigmoid_row(lgb_ref[idx:idx + 1, :]) * float(CHUNK))
                ds_ref[idx] = gc * dsp + _dot(qf32 * xi, gv, TN)
                acc_ref[idx] += (ee * a * dar + xe * xi * qf32 * t1 + ze * zt * kf32 * t2
                                 + (float(CHUNK) * gc) * dsp * s)

        @pl.when(n == nc - 1)
        def _():
            for idx in range(nh):
                tot = jnp.sum(acc_ref[idx])
                dl_ref[idx:idx + 1, :] = tot * _sigmoid(-lgb_ref[idx:idx + 1, :])

    fmap = lambda n: (nc - 1 - n, 0)
    bmap = lambda n: (_bwd_chunk(nc - 1 - n, ncc, nc), 0)
    fspec = pl.BlockSpec((CHUNK, w), fmap)
    bspec = pl.BlockSpec((CHUNK, w), bmap)
    tab = pltpu.VMEM((nh, CHUNK, CHUNK), F32)
    o = _sds((t, w))
    return _pcall(body, name=name, grid=(nc,),
                  in_specs=[fspec] * 4 + [bspec] * 4
                  + [pl.BlockSpec((None, nh, CHUNK, CHUNK), lambda n: (nc - 1 - n, 0, 0, 0)), _full((nh, HEAD_DIM))],
                  out_specs=[fspec] * 3 + [bspec] * 3 + [_full((nh, HEAD_DIM))],
                  out_shape=[o] * 6 + [_sds((nh, HEAD_DIM))],
                  scratch_shapes=[tab, tab, tab, tab, tab],
                  compiler_params=_cp(("arbitrary",)))(q, k, v, do, q, k, v, do, ss, lgb)


def _halo_specs(tm, halo, t, width, col):
    hb = tm // halo
    last = t // halo - 1
    prev = pl.BlockSpec((halo, width), lambda i: (jnp.maximum(i * hb - 1, 0), col))
    nxt = pl.BlockSpec((halo, width), lambda i: (jnp.minimum((i + 1) * hb, last), col))
    return prev, nxt


def _halo_valid(i, nct, nt):
    vp = jnp.logical_and(i != 0, i != nct)
    vn = jnp.logical_and(i != nct - 1, i != nt - 1)
    return vp, vn


def _fill_window(win_ref, prev, cur, nxt, vp, vn, halo, tm):
    win_ref[0:halo, :] = jnp.where(vp, prev, 0.0)
    win_ref[halo:halo + tm, :] = cur
    win_ref[halo + tm:halo + tm + halo, :] = jnp.where(vn, nxt, 0.0)


CONV_SUB = 64


SUBLANES = 8


def _shift_window(win_ref, sh_ref, tm):
    rows = tm + 2 * CONV_HALO - SUBLANES
    for s in range(SUBLANES):
        sh_ref[s, 0:rows, :] = win_ref[s:s + rows, :]


def _window_rows(sh_ref, start, rows):
    s = start % SUBLANES
    return sh_ref[s, start - s:start - s + rows, :]


def _conv_taps(sh_ref, w_ref, tm, flip):
    outs = []
    for r0 in range(0, tm, CONV_SUB):
        acc = None
        for kk in range(CONV_K):
            wk = (CONV_K - 1 - kk) if flip else kk
            term = w_ref[wk:wk + 1, :] * _window_rows(sh_ref, r0 + kk + 1, CONV_SUB)
            acc = term if acc is None else acc + term
        outs.append(acc)
    return jnp.concatenate(outs, axis=0)


def _head_norm(y):
    r = lax.rsqrt(jnp.mean(y * y, axis=-1, keepdims=True) + EPS)
    return y * r, r


def _ln_stats(y):
    mu = jnp.mean(y, axis=-1, keepdims=True)
    yc = y - mu
    rs = lax.rsqrt(jnp.mean(yc * yc, axis=-1, keepdims=True) + EPS)
    return yc * rs, rs


def _ln_bwd(dyh, yh, rs):
    return rs * (dyh - jnp.mean(dyh, axis=-1, keepdims=True) - yh * jnp.mean(dyh * yh, axis=-1, keepdims=True))


def _even_mix(name, p, of, ob, cw, lnw, lnb, nct):
    t = p.shape[0]
    tm, halo = ROW_TILE, CONV_HALO
    nt = t // tm
    w = HEADS * HEAD_DIM

    def body(g_ref, a_ref, gb_ref, ap_ref, gbp_ref, an_ref, gbn_ref, of_ref, ob_ref, cw_ref, lw_ref, lb_ref,
             mix_ref, yc_ref, win_ref, sh_ref):
        i = pl.program_id(0)
        vp, vn = _halo_valid(i, nct, nt)
        glu = lambda a, b: a * _sigmoid(b)
        _fill_window(win_ref, glu(ap_ref[...], gbp_ref[...]), glu(a_ref[...], gb_ref[...]),
                     glu(an_ref[...], gbn_ref[...]), vp, vn, halo, tm)
        _shift_window(win_ref, sh_ref, tm)
        yc = _conv_taps(sh_ref, cw_ref, tm, False)
        yc_ref[...] = yc
        yh, _ = _ln_stats(yc)
        mix_ref[:, w:2 * w] = _silu(yh * lw_ref[...] + lb_ref[...]).astype(BF16)
        for h in range(HEADS):
            sl = slice(h * HEAD_DIM, (h + 1) * HEAD_DIM)
            yn, _ = _head_norm(of_ref[:, sl] + ob_ref[:, sl])
            mix_ref[:, sl] = (_silu(g_ref[:, sl]) * yn).astype(BF16)

    col = lambda j: pl.BlockSpec((tm, w), lambda i: (i, j))
    ap, an = _halo_specs(tm, halo, t, w, 4)
    gp, gn = _halo_specs(tm, halo, t, w, 5)
    row = pl.BlockSpec((tm, w), lambda i: (i, 0))
    return _pcall(body, name=name, grid=(nt,),
                  in_specs=[col(3), col(4), col(5), ap, gp, an, gn, row, row,
                            _full(cw.shape), _full(lnw.shape), _full(lnb.shape)],
                  out_specs=[pl.BlockSpec((tm, 2 * w), lambda i: (i, 0)), row],
                  out_shape=[_sds((t, 2 * w), BF16), _sds((t, w))],
                  scratch_shapes=[pltpu.VMEM((tm + 2 * halo, w), F32), pltpu.VMEM((SUBLANES, tm + 2 * halo, w), F32)],
                  compiler_params=_cp(("parallel",)))(p, p, p, p, p, p, p, of, ob, cw, lnw, lnb)


def _even_mix_bwd1(name, dmix, p, of, ob, yc, lnw, lnb):
    t = p.shape[0]
    tm = ROW_TILE
    w = HEADS * HEAD_DIM

    def body(dr_ref, dc_ref, g_ref, of_ref, ob_ref, yc_ref, lw_ref, lb_ref, do_ref, dg_ref, dyc_ref, s_ref):
        @pl.when(pl.program_id(0) == 0)
        def _():
            s_ref[...] = jnp.zeros_like(s_ref)

        for h in range(HEADS):
            sl = slice(h * HEAD_DIM, (h + 1) * HEAD_DIM)
            yn, r = _head_norm(of_ref[:, sl] + ob_ref[:, sl])
            gv = g_ref[:, sl]
            dr = dr_ref[:, sl]
            dg_ref[:, sl] = (dr * yn * _dsilu(gv)).astype(BF16)
            dyn = dr * _silu(gv)
            do_ref[:, sl] = (r * (dyn - yn * jnp.mean(dyn * yn, axis=-1, keepdims=True))).astype(BF16)
        yh, rs = _ln_stats(yc_ref[...])
        lw = lw_ref[...]
        dlo = dc_ref[...] * _dsilu(yh * lw + lb_ref[...])
        dyc_ref[...] = _ln_bwd(dlo * lw, yh, rs)
        s_ref[...] += jnp.concatenate([_colsum(dlo * yh), _colsum(dlo), jnp.zeros((6, w), F32)], axis=0)

    col = lambda j: pl.BlockSpec((tm, w), lambda i: (i, j))
    row = pl.BlockSpec((tm, w), lambda i: (i, 0))
    return _pcall(body, name=name, grid=(t // tm,),
                  in_specs=[col(0), col(1), col(3), row, row, row, _full(lnw.shape), _full(lnb.shape)],
                  out_specs=[row, row, row, _full((8, w))],
                  out_shape=[_sds((t, w), BF16), _sds((t, w), BF16), _sds((t, w)), _sds((8, w))],
                  compiler_params=_cp(("arbitrary",)))(dmix, dmix, p, of, ob, yc, lnw, lnb)


def _even_conv_bwd(name, dyc, p, cw, nct):
    t = p.shape[0]
    tm, halo = ROW_TILE, CONV_HALO
    nt = t // tm
    w = HEADS * HEAD_DIM

    def body(d_ref, dp_ref, dn_ref, a_ref, gb_ref, ap_ref, gbp_ref, an_ref, gbn_ref, cw_ref,
             da_ref, dgb_ref, dw_ref, dwin_ref, uwin_ref, dsh_ref, ush_ref):
        i = pl.program_id(0)

        @pl.when(i == 0)
        def _():
            dw_ref[...] = jnp.zeros_like(dw_ref)

        vp, vn = _halo_valid(i, nct, nt)
        glu = lambda a, b: a * _sigmoid(b)
        dcur = d_ref[...]
        _fill_window(dwin_ref, dp_ref[...], dcur, dn_ref[...], vp, vn, halo, tm)
        _fill_window(uwin_ref, glu(ap_ref[...], gbp_ref[...]), glu(a_ref[...], gb_ref[...]),
                     glu(an_ref[...], gbn_ref[...]), vp, vn, halo, tm)
        _shift_window(dwin_ref, dsh_ref, tm)
        _shift_window(uwin_ref, ush_ref, tm)
        du = _conv_taps(dsh_ref, cw_ref, tm, True)
        av = a_ref[...]
        sg = _sigmoid(gb_ref[...])
        da_ref[...] = (du * sg).astype(BF16)
        dgb_ref[...] = (du * av * sg * (1.0 - sg)).astype(BF16)
        rows = [_colsum(dcur * _window_rows(ush_ref, kk + 1, tm)) for kk in range(CONV_K)]
        dw_ref[...] += jnp.concatenate(rows + [jnp.zeros((1, w), F32)], axis=0)

    col = lambda j: pl.BlockSpec((tm, w), lambda i: (i, j))
    row = pl.BlockSpec((tm, w), lambda i: (i, 0))
    dp, dn = _halo_specs(tm, halo, t, w, 0)
    ap, an = _halo_specs(tm, halo, t, w, 4)
    gp, gn = _halo_specs(tm, halo, t, w, 5)
    win = pltpu.VMEM((tm + 2 * halo, w), F32)
    shifted = pltpu.VMEM((SUBLANES, tm + 2 * halo, w), F32)
    return _pcall(body, name=name, grid=(nt,),
                  in_specs=[row, dp, dn, col(4), col(5), ap, gp, an, gn, _full(cw.shape)],
                  out_specs=[row, row, _full((CONV_K + 1, w))],
                  out_shape=[_sds((t, w), BF16), _sds((t, w), BF16), _sds((CONV_K + 1, w))],
                  scratch_shapes=[win, win, shifted, shifted],
                  compiler_params=_cp(("arbitrary",)))(dyc, dyc, dyc, p, p, p, p, p, p, cw)


def _even_dp(name, dqs, dks, dvs, dg, da, dgb, cs, sn):
    t, w = dg.shape
    tm = ROW_TILE
    scale = HEAD_DIM ** -0.5

    def body(dqf_ref, dqb_ref, dkf_ref, dkb_ref, dvf_ref, dvb_ref, dg_ref, da_ref, dgb_ref, cs_ref, sn_ref, dp_ref):
        c, s = cs_ref[...], sn_ref[...]
        for h in range(HEADS):
            sl = slice(h * HEAD_DIM, (h + 1) * HEAD_DIM)
            dp_ref[:, sl] = (_rope_t(dqf_ref[:, sl] + dqb_ref[:, sl], c, s) * scale).astype(BF16)
            dp_ref[:, w + h * HEAD_DIM:w + (h + 1) * HEAD_DIM] = _rope_t(dkf_ref[:, sl] + dkb_ref[:, sl], c, s).astype(BF16)
        dp_ref[:, 2 * w:3 * w] = (dvf_ref[...] + dvb_ref[...]).astype(BF16)
        dp_ref[:, 3 * w:4 * w] = dg_ref[...]
        dp_ref[:, 4 * w:5 * w] = da_ref[...]
        dp_ref[:, 5 * w:6 * w] = dgb_ref[...]

    row = pl.BlockSpec((tm, w), lambda i: (i, 0))
    tab = pl.BlockSpec((tm, HEAD_DIM), lambda i: (i, 0))
    return _pcall(body, name=name, grid=(t // tm,), in_specs=[row] * 9 + [tab, tab],
                  out_specs=pl.BlockSpec((tm, 6 * w), lambda i: (i, 0)), out_shape=_sds((t, 6 * w), BF16),
                  compiler_params=_cp(("parallel",)))(dqs[0], dqs[1], dks[0], dks[1], dvs[0], dvs[1], dg, da, dgb, cs, sn)


GROUPS = 4
GC = 128
INV_SQRT2 = 0.7071067811865476
INV_SQRT_2PI = 0.3989422804014327


def _gelu(x):
    return 0.5 * x * (1.0 + lax.erf(x * INV_SQRT2))


def _dgelu(x):
    return 0.5 * (1.0 + lax.erf(x * INV_SQRT2)) + x * jnp.exp(-0.5 * x * x) * INV_SQRT_2PI


def _pool_count(i, nct, lc, t, tm, rows, row0, left, right):
    is_ctx = i < nct
    seg_start = jnp.where(is_ctx, 0, lc)
    seg_len = jnp.where(is_ctx, lc, t - lc)
    pos = i * tm + row0 - seg_start + lax.broadcasted_iota(jnp.int32, (rows, GC), 0)
    cnt = jnp.minimum(pos + right, seg_len - 1) - jnp.maximum(pos - left, 0) + 1
    return jnp.maximum(cnt, 1).astype(F32)


def _spatial_gate(vln, sgw_ref, sgb_ref, tm):
    cols = []
    for g in range(GROUPS):
        sl = slice(g * GC, (g + 1) * GC)
        parts = [_dot(sgw_ref[g], vln[r0:r0 + CHUNK, sl], NN) + sgb_ref[g] for r0 in range(0, tm, CHUNK)]
        cols.append(jnp.concatenate(parts, axis=0))
    return jnp.concatenate(cols, axis=1)


def _odd_mix(name, p, pw, pscale, lnw, lnb, sgw, sgb, nct, lc):
    t = p.shape[0]
    tm, halo = ROW_TILE, POOL_HALO
    nt = t // tm
    w = GROUPS * GC

    def body(pc_ref, pp_ref, pn_ref, pu_ref, pv_ref, pw_ref, ps_ref, lw_ref, lb_ref, sgw_ref, sgb_ref,
             mix_ref, m_ref, win_ref):
        i = pl.program_id(0)
        vp, vn = _halo_valid(i, nct, nt)
        pc = pc_ref[...]
        _fill_window(win_ref, pp_ref[...], pc, pn_ref[...], vp, vn, halo, tm)
        for g, wd in enumerate(POOL_WINDOWS):
            sl = slice(g * GC, (g + 1) * GC)
            left = wd // 2
            right = wd - 1 - left
            s = None
            for o in range(-left, right + 1):
                term = win_ref[halo + o:halo + o + tm, sl]
                s = term if s is None else s + term
            mg = s / _pool_count(i, nct, lc, t, tm, tm, 0, left, right) - pc[:, sl]
            m_ref[:, sl] = mg
            mix_ref[:, sl] = (_dot(mg, pw_ref[g], NN) * ps_ref[:, sl]).astype(BF16)
        u = _gelu(pu_ref[...])
        vh, _ = _ln_stats(_gelu(pv_ref[...]))
        s = _spatial_gate(vh * lw_ref[...] + lb_ref[...], sgw_ref, sgb_ref, tm)
        mix_ref[:, w:2 * w] = (u * s).astype(BF16)

    col = lambda j: pl.BlockSpec((tm, w), lambda i: (i, j))
    pp, pn = _halo_specs(tm, halo, t, w, 0)
    return _pcall(body, name=name, grid=(nt,),
                  in_specs=[col(0), pp, pn, col(1), col(2), _full(pw.shape), _full(pscale.shape),
                            _full(lnw.shape), _full(lnb.shape), _full(sgw.shape), _full(sgb.shape)],
                  out_specs=[pl.BlockSpec((tm, 2 * w), lambda i: (i, 0)), col(0)],
                  out_shape=[_sds((t, 2 * w), BF16), _sds((t, w))],
                  scratch_shapes=[pltpu.VMEM((tm + 2 * halo, w), F32)],
                  compiler_params=_cp(("parallel",)))(p, p, p, p, p, pw, pscale, lnw, lnb, sgw, sgb)


def _odd_mix_bwd1(name, dmix, p, m, pw, pscale, lnw, lnb, sgw, sgb):
    t = p.shape[0]
    tm = ROW_TILE
    w = GROUPS * GC

    def body(dpo_ref, dsg_ref, pu_ref, pv_ref, m_ref, pw_ref, ps_ref, lw_ref, lb_ref, sgw_ref, sgb_ref,
             dm_ref, dpd_ref, vec_ref, dpw_ref, dsgw_ref, dsgb_ref):
        @pl.when(pl.program_id(0) == 0)
        def _():
            vec_ref[...] = jnp.zeros_like(vec_ref)
            dpw_ref[...] = jnp.zeros_like(dpw_ref)
            dsgw_ref[...] = jnp.zeros_like(dsgw_ref)
            dsgb_ref[...] = jnp.zeros_like(dsgb_ref)

        dscale = []
        for g in range(GROUPS):
            sl = slice(g * GC, (g + 1) * GC)
            mg = m_ref[:, sl]
            dpo = dpo_ref[:, sl]
            dscale.append(_colsum(dpo * _dot(mg, pw_ref[g], NN)))
            dpo = dpo * ps_ref[:, sl]
            dm_ref[:, sl] = _dot(dpo, pw_ref[g], NT)
            dpw_ref[g] += _dot(mg, dpo, TN)
        pu, pv = pu_ref[...], pv_ref[...]
        u = _gelu(pu)
        vh, rs = _ln_stats(_gelu(pv))
        lw = lw_ref[...]
        vln = vh * lw + lb_ref[...]
        s = _spatial_gate(vln, sgw_ref, sgb_ref, tm)
        dsg = dsg_ref[...]
        dpd_ref[:, 0:w] = (dsg * s * _dgelu(pu)).astype(BF16)
        ds = dsg * u
        cols = []
        for g in range(GROUPS):
            sl = slice(g * GC, (g + 1) * GC)
            parts = []
            for r0 in range(0, tm, CHUNK):
                dsc = ds[r0:r0 + CHUNK, sl]
                parts.append(_dot(sgw_ref[g], dsc, TN))
                dsgw_ref[g] += _dot(dsc, vln[r0:r0 + CHUNK, sl], NT)
                dsgb_ref[g] += dsc
            cols.append(jnp.concatenate(parts, axis=0))
        dvln = jnp.concatenate(cols, axis=1)
        dpd_ref[:, w:2 * w] = (_ln_bwd(dvln * lw, vh, rs) * _dgelu(pv)).astype(BF16)
        vec_ref[...] += jnp.concatenate([jnp.concatenate(dscale, axis=1), _colsum(dvln * vh), _colsum(dvln),
                                         jnp.zeros((5, w), F32)], axis=0)

        @pl.when(pl.program_id(0) == t // tm - 1)
        def _():
            for g in range(GROUPS):
                dsgb_ref[g] = jnp.broadcast_to(jnp.sum(dsgb_ref[g], axis=1, keepdims=True), (GC, GC))

    col = lambda j: pl.BlockSpec((tm, w), lambda i: (i, j))
    mat = _full((GROUPS, GC, GC))
    return _pcall(body, name=name, grid=(t // tm,),
                  in_specs=[col(0), col(1), col(1), col(2), col(0), _full(pw.shape), _full(pscale.shape),
                            _full(lnw.shape), _full(lnb.shape), _full(sgw.shape), _full(sgb.shape)],
                  out_specs=[col(0), pl.BlockSpec((tm, 2 * w), lambda i: (i, 0)), _full((8, w)), mat, mat, mat],
                  out_shape=[_sds((t, w)), _sds((t, 2 * w), BF16), _sds((8, w)),
                             _sds((GROUPS, GC, GC)), _sds((GROUPS, GC, GC)), _sds((GROUPS, GC, GC))],
                  compiler_params=_cp(("arbitrary",)))(dmix, dmix, p, p, m, pw, pscale, lnw, lnb, sgw, sgb)


def _odd_dp(name, dm, dpd, nct, lc):
    t, w = dm.shape
    tm, halo = ROW_TILE, POOL_HALO
    nt = t // tm

    def body(d_ref, dp_ref, dn_ref, dpd_ref, o_ref, win_ref):
        i = pl.program_id(0)
        vp, vn = _halo_valid(i, nct, nt)
        dcur = d_ref[...]
        _fill_window(win_ref, dp_ref[...], dcur, dn_ref[...], vp, vn, halo, tm)
        for g, wd in enumerate(POOL_WINDOWS):
            sl = slice(g * GC, (g + 1) * GC)
            left = wd // 2
            right = wd - 1 - left
            win_ref[:, sl] = win_ref[:, sl] / _pool_count(i, nct, lc, t, tm, tm + 2 * halo, -halo, left, right)
            s = None
            for o in range(-right, left + 1):
                term = win_ref[halo + o:halo + o + tm, sl]
                s = term if s is None else s + term
            o_ref[:, sl] = (s - dcur[:, sl]).astype(BF16)
        o_ref[:, w:3 * w] = dpd_ref[...]

    row = pl.BlockSpec((tm, w), lambda i: (i, 0))
    pp, pn = _halo_specs(tm, halo, t, w, 0)
    return _pcall(body, name=name, grid=(nt,),
                  in_specs=[row, pp, pn, pl.BlockSpec((tm, 2 * w), lambda i: (i, 0))],
                  out_specs=pl.BlockSpec((tm, 3 * w), lambda i: (i, 0)), out_shape=_sds((t, 3 * w), BF16),
                  scratch_shapes=[pltpu.VMEM((tm + 2 * halo, w), F32)],
                  compiler_params=_cp(("parallel",)))(dm, dm, dm, dpd)


def _place():
    x, y, c = lax.axis_index("x"), lax.axis_index("y"), lax.axis_index("c")
    chips = [(1 - x, y), (x, 1 - y), (1 - x, 1 - y)]
    return x, y, c, chips


def _chip_index(cx, cy):
    return 2 * cx + cy


def _all_gather8(name, blk, after=()):
    m_per, n = blk.shape
    na = len(after)

    def body(x_ref, *rest):
        out_ref, send_sems, recv_sems, local_sem = rest[na:]
        x, y, c, chips = _place()
        me, sibling = (x, y, c), (x, y, 1 - c)

        def rows(px, py, pc):
            return out_ref.at[pl.ds((4 * px + 2 * py + pc) * m_per, m_per), :]

        def copy(k, block, to, src=None):
            return pltpu.make_async_remote_copy(
                src_ref=rows(*block) if src is None else src, dst_ref=rows(*block),
                send_sem=send_sems.at[k], recv_sem=recv_sems.at[k], device_id=to, device_id_type=MESH)

        mine = pltpu.make_async_copy(x_ref, rows(*me), local_sem)
        mine.start()
        first = [copy(0, me, sibling, src=x_ref)]
        first += [copy(1 + j, me, (*chip, c), src=x_ref) for j, chip in enumerate(chips)]
        for cp in first:
            cp.start()
        passed = [copy(4 + j, (*chip, c), sibling) for j, chip in enumerate(chips)]
        for j, chip in enumerate(chips):
            copy(1 + j, (*chip, c), me).wait_recv()
            passed[j].start()
        copy(0, sibling, me).wait_recv()
        for j, chip in enumerate(chips):
            copy(4 + j, (*chip, 1 - c), me).wait_recv()
        for cp in first + passed:
            cp.wait_send()
        mine.wait()

    return _pcall(body, name=name, out_shape=_sds((8 * m_per, n), blk.dtype),
                  in_specs=[pl.BlockSpec(memory_space=pltpu.VMEM)] + [pl.BlockSpec(memory_space=pl.ANY)] * na,
                  out_specs=pl.BlockSpec(memory_space=pltpu.VMEM),
                  scratch_shapes=[pltpu.SemaphoreType.DMA((7,)), pltpu.SemaphoreType.DMA((7,)), pltpu.SemaphoreType.DMA],
                  compiler_params=_cp())(blk, *after)


ANY = pl.BlockSpec(memory_space=pl.ANY)


def _half(which, rows):
    return pl.ds(pl.multiple_of(which * rows, 16), rows)


def _gather_weights(name, ws, after=()):
    nw = len(ws)
    na = len(after)
    ns = 7

    def body(*refs):
        w_refs, o_refs = refs[:nw], refs[nw + na:2 * nw + na]
        send_sems, recv_sems = refs[2 * nw + na:]
        x, y, c, chips = _place()
        me_chip = _chip_index(x, y)
        sibling = (x, y, 1 - c)

        def rcopy(t, k, src, dst, to):
            return pltpu.make_async_remote_copy(src_ref=src, dst_ref=dst, send_sem=send_sems.at[t * ns + k],
                                                recv_sem=recv_sems.at[t * ns + k], device_id=to, device_id_type=MESH)

        sends = []
        for t in range(nw):
            lh = w_refs[t].shape[0] // 2
            for k, chip in enumerate(chips):
                sends.append(rcopy(t, k, w_refs[t].at[_half(c, lh)], o_refs[t].at[me_chip, _half(c, lh)], (*chip, c)))
                sends[-1].start()
            sends.append(rcopy(t, 6, w_refs[t], o_refs[t].at[me_chip], sibling))
            sends[-1].start()
        for t in range(nw):
            lh = w_refs[t].shape[0] // 2
            for k, chip in enumerate(chips):
                part = o_refs[t].at[_chip_index(*chip), _half(c, lh)]
                rcopy(t, k, part, part, (*chip, c)).wait_recv()
                sends.append(rcopy(t, 3 + k, part, part, sibling))
                sends[-1].start()
        for t in range(nw):
            lh = w_refs[t].shape[0] // 2
            own = o_refs[t].at[me_chip]
            rcopy(t, 6, own, own, sibling).wait_recv()
            for k, chip in enumerate(chips):
                part = o_refs[t].at[_chip_index(*chip), _half(1 - c, lh)]
                rcopy(t, 3 + k, part, part, sibling).wait_recv()
        for cp in sends:
            cp.wait_send()

    return _pcall(body, name=name, out_shape=[_sds((4,) + w.shape, w.dtype) for w in ws],
                  in_specs=[ANY] * (nw + na), out_specs=[ANY] * nw,
                  scratch_shapes=[pltpu.SemaphoreType.DMA((ns * nw,)), pltpu.SemaphoreType.DMA((ns * nw,))],
                  compiler_params=_cp())(*ws, *after)


def _rs_share(name, ss):
    ng = len(ss)

    def body(*refs):
        o_refs = refs[ng:2 * ng]
        send_sems, recv_sems = refs[2 * ng:]
        x, y, c, _ = _place()
        cps = []
        for t in range(ng):
            lh = o_refs[t].shape[1] // 2
            mine = o_refs[t].at[:, _half(c, lh)]
            cp = pltpu.make_async_remote_copy(
                src_ref=mine, dst_ref=mine, send_sem=send_sems.at[t], recv_sem=recv_sems.at[t],
                device_id=(x, y, 1 - c), device_id_type=MESH)
            cp.start()
            cps.append(cp)
        for t in range(ng):
            lh = o_refs[t].shape[1] // 2
            cps[t].wait_send()
            theirs = o_refs[t].at[:, _half(1 - c, lh)]
            pltpu.make_async_remote_copy(
                src_ref=theirs, dst_ref=theirs, send_sem=send_sems.at[t], recv_sem=recv_sems.at[t],
                device_id=(x, y, 1 - c), device_id_type=MESH).wait_recv()

    return _pcall(body, name=name, out_shape=[_sds(s.shape, s.dtype) for s in ss],
                  in_specs=[ANY] * ng, out_specs=[ANY] * ng, input_output_aliases={t: t for t in range(ng)},
                  scratch_shapes=[pltpu.SemaphoreType.DMA((ng,)), pltpu.SemaphoreType.DMA((ng,))],
                  compiler_params=_cp())(*ss)


HBM = pl.BlockSpec(memory_space=pltpu.HBM)
SEMS = pl.BlockSpec(memory_space=pltpu.SEMAPHORE)
EFFECT = pltpu.SideEffectType.DATAFLOW_SIDE_EFFECTING
TOKEN = (8, 128)


def _in_hbm(a):
    return pltpu.with_memory_space_constraint(a, pltpu.HBM)


def _split_start(name, srcs, lands, copies, after):
    ns, nl, na = len(srcs), len(lands), len(after)
    ncopies = len(copies([s for s in srcs], [l for l in lands], probe=True))

    def body(*refs):
        src_refs, land_refs = refs[:ns], refs[ns:ns + nl]
        send_sems, recv_sems = refs[ns + nl + na], refs[ns + nl + na + 1]
        token = refs[-1]
        for k, (src, dst, to) in enumerate(copies(src_refs, land_refs)):
            pltpu.make_async_remote_copy(src_ref=src, dst_ref=dst, send_sem=send_sems.at[k], recv_sem=recv_sems.at[k],
                                         device_id=to, device_id_type=MESH).start()
        token[...] = jnp.zeros_like(token)

    thru = [pltpu.HBM(a.shape, a.dtype) for a in list(srcs) + list(lands)]
    outs = _pcall(body, name=name,
                  out_shape=(pltpu.SemaphoreType.DMA((ncopies,)), pltpu.SemaphoreType.DMA((ncopies,)), *thru, _sds(TOKEN)),
                  in_specs=[HBM] * (ns + nl) + [ANY] * na,
                  out_specs=(SEMS, SEMS, *([HBM] * (ns + nl)), pl.BlockSpec(memory_space=pltpu.VMEM)),
                  input_output_aliases={t: 2 + t for t in range(ns + nl)},
                  compiler_params=pltpu.CompilerParams(has_side_effects=EFFECT))(
        *[_in_hbm(a) for a in list(srcs) + list(lands)], *after)
    return outs[0], outs[1], list(outs[2:2 + ns]), list(outs[2 + ns:2 + ns + nl]), outs[-1]


def _split_wait(name, started, copies, after):
    send_sems, recv_sems, srcs, lands, _ = started
    ns, nl, na = len(srcs), len(lands), len(after)

    def body(*refs):
        src_refs, land_refs = refs[:ns], refs[ns:ns + nl]
        send_sems_ref, recv_sems_ref = refs[ns + nl], refs[ns + nl + 1]
        for k, (src, dst, to) in enumerate(copies(src_refs, land_refs)):
            cp = pltpu.make_async_remote_copy(src_ref=src, dst_ref=dst, send_sem=send_sems_ref.at[k],
                                              recv_sem=recv_sems_ref.at[k], device_id=to, device_id_type=MESH)
            cp.wait_send()
            cp.wait_recv()

    thru = [pltpu.HBM(a.shape, a.dtype) for a in list(srcs) + list(lands)]
    outs = _pcall(body, name=name, out_shape=tuple(thru),
                  in_specs=[HBM] * (ns + nl) + [SEMS, SEMS] + [ANY] * na, out_specs=tuple([HBM] * (ns + nl)),
                  input_output_aliases={t: t for t in range(ns + nl)},
                  compiler_params=pltpu.CompilerParams(has_side_effects=EFFECT))(
        *srcs, *lands, send_sems, recv_sems, *after)
    return list(outs[:ns]), list(outs[ns:])


def _pair_copies(n):
    def copies(src_refs, land_refs, probe=False):
        if probe:
            return [None] * n
        x, y, c, _ = _place()
        return [(src_refs[t].at[:, _half(1 - c, src_refs[t].shape[1] // 2)], land_refs[t], (x, y, 1 - c))
                for t in range(n)]
    return copies


def _gather_copies(n):
    def copies(src_refs, land_refs, probe=False):
        if probe:
            return [None] * (4 * n)
        x, y, c, chips = _place()
        me_chip = _chip_index(x, y)
        out = []
        for t in range(n):
            for to in [(*chip, c) for chip in chips] + [(x, y, 1 - c)]:
                out.append((src_refs[t], land_refs[t].at[me_chip], to))
        return out
    return copies


def _scatter_copies(n):
    def copies(src_refs, land_refs, probe=False):
        if probe:
            return [None] * (3 * n)
        x, y, c, chips = _place()
        out = []
        for t in range(n):
            for k, chip in enumerate(chips):
                out.append((src_refs[t].at[_chip_index(*chip)], land_refs[t].at[k], (*chip, c)))
        return out
    return copies


def _row_block(r, cn):
    if r % 8:
        return r
    best = 8
    for d in range(8, r + 1, 8):
        if r % d == 0 and d * cn * 4 <= (2 << 20):
            best = d
    return best


def _add_half(name, g, a, idx):
    j, rh, cn = a.shape
    tr = _row_block(rh, cn)
    nb = rh // tr

    def body(i_ref, g_ref, a_ref, o_ref):
        o_ref[...] = (g_ref[...] + a_ref[...]).astype(BF16)

    blk = (None, tr, cn)
    gs = pltpu.PrefetchScalarGridSpec(
        num_scalar_prefetch=1, grid=(j, nb),
        in_specs=[pl.BlockSpec(blk, lambda jj, i, i_ref: (jj, i_ref[0] * nb + i, 0)),
                  pl.BlockSpec(blk, lambda jj, i, i_ref: (jj, i, 0))],
        out_specs=pl.BlockSpec(blk, lambda jj, i, i_ref: (jj, i, 0)))
    return _pcall(body, name=name, grid_spec=gs, out_shape=_sds(a.shape, BF16),
                  compiler_params=_cp(("parallel", "parallel")))(idx, g, a)


def _sum_final(name, g, a, b, idx, buf, lyr, nlyr):
    _, r, cn = g.shape
    rh = r // 2
    tr = _row_block(rh, cn)
    nb = rh // tr

    def body(*refs):
        g_ref, a_ref, b_ref = refs[1:4]
        o_ref = refs[-1]
        own = g_ref[...] + a_ref[...]
        o_ref[...] = (own + b_ref[0].astype(F32)) + (b_ref[1].astype(F32) + b_ref[2].astype(F32))

    blk = (None, tr, cn)
    in_specs = [pl.BlockSpec(blk, lambda i, i_ref: (i_ref[1], i_ref[0] * nb + i, 0)),
                pl.BlockSpec(blk, lambda i, i_ref: (i_ref[1], i, 0)),
                pl.BlockSpec((3, tr, cn), lambda i, i_ref: (0, i, 0))]
    args = [idx, g, a, b]
    kw = {}
    if buf is not None:
        in_specs.append(ANY)
        args.append(buf)
        kw["input_output_aliases"] = {4: 0}
    gs = pltpu.PrefetchScalarGridSpec(
        num_scalar_prefetch=1, grid=(nb,), in_specs=in_specs,
        out_specs=pl.BlockSpec(blk, lambda i, i_ref: (lyr, i_ref[0] * nb + i, 0)))
    return _pcall(body, name=name, grid_spec=gs, out_shape=_sds((nlyr, r, cn)),
                  compiler_params=_cp(("parallel",)), **kw)(*args)


def _sum8(name, g):
    _, r, n = g.shape
    tr = 8

    def body(g_ref, o_ref):
        o_ref[...] = ((g_ref[0] + g_ref[1]) + (g_ref[2] + g_ref[3])) + ((g_ref[4] + g_ref[5]) + (g_ref[6] + g_ref[7]))

    return _pcall(body, name=name, grid=(r // tr,), in_specs=[pl.BlockSpec((8, tr, n), lambda i: (0, i, 0))],
                  out_specs=pl.BlockSpec((tr, n), lambda i: (i, 0)), out_shape=_sds((r, n)),
                  compiler_params=_cp(("parallel",)))(g)


def _ada_mod(name, c16, ada_w, bias):
    nl, dm, n = ada_w.shape

    def body(c_ref, w_ref, b_ref, o_ref):
        o_ref[...] = _dot(_silu(c_ref[...]), w_ref[...], NN) + b_ref[...]

    return _pcall(body, name=name, grid=(nl,),
                  in_specs=[_full(c16.shape), pl.BlockSpec((None, dm, n), lambda i: (i, 0, 0)),
                            pl.BlockSpec((None, 1, n), lambda i: (i, 0, 0))],
                  out_specs=pl.BlockSpec((None, 16, n), lambda i: (i, 0, 0)), out_shape=_sds((nl, 16, n)),
                  compiler_params=_cp(("parallel",)))(c16, ada_w, bias)


def _ada_bwd(name, c16, dmod, ada_w):
    nl, dm, n = ada_w.shape

    def body(c_ref, d_ref, w_ref, gw_ref, dc_ref):
        @pl.when(pl.program_id(0) == 0)
        def _():
            dc_ref[...] = jnp.zeros_like(dc_ref)

        dv = d_ref[...]
        gw_ref[...] = _dot(_silu(c_ref[...]), dv, TN)
        dc_ref[...] += _dot(dv, w_ref[...], NT)

    return _pcall(body, name=name, grid=(nl,),
                  in_specs=[_full(c16.shape), pl.BlockSpec((None, 16, n), lambda i: (i, 0, 0)),
                            pl.BlockSpec((None, dm, n), lambda i: (i, 0, 0))],
                  out_specs=[pl.BlockSpec((None, dm, n), lambda i: (i, 0, 0)), _full((16, dm))],
                  out_shape=[_sds((nl, dm, n)), _sds((16, dm))],
                  compiler_params=_cp(("arbitrary",)))(c16, dmod, ada_w)


def _rowsum16(name, dmod):
    nl, _, n = dmod.shape

    def body(d_ref, o_ref):
        o_ref[...] = _colsum(d_ref[...])

    return _pcall(body, name=name, grid=(nl,), in_specs=[pl.BlockSpec((None, 16, n), lambda i: (i, 0, 0))],
                  out_specs=pl.BlockSpec((None, 1, n), lambda i: (i, 0, 0)), out_shape=_sds((nl, 1, n)),
                  compiler_params=_cp(("parallel",)))(dmod)


def _cctx_grad(name, parts, c_ctx):
    def body(p_ref, c_ref, o_ref):
        tot = (p_ref[0:1, :] + p_ref[1:2, :]) + (p_ref[2:3, :] + p_ref[3:4, :])
        o_ref[...] = tot * _dsilu(c_ref[...])

    return _pcall(body, name=name, out_shape=_sds(c_ctx.shape), compiler_params=_cp())(parts, c_ctx)


def _adamw(name, w, g, m, v):
    shape = w.shape
    cn = shape[-1]
    r = math.prod(shape[:-1]) if len(shape) > 1 else 1
    tr = _row_block(r, cn)
    c1 = 1.0 - ADAM_B1 ** ADAM_STEP
    c2 = 1.0 - ADAM_B2 ** ADAM_STEP

    def body(w_ref, g_ref, m_ref, v_ref, d_ref, mo_ref, vo_ref):
        gv = g_ref[...]
        mn = ADAM_B1 * m_ref[...] + (1.0 - ADAM_B1) * gv
        vn = ADAM_B2 * v_ref[...] + (1.0 - ADAM_B2) * (gv * gv)
        d_ref[...] = -ADAM_LR * ((mn / c1) / (jnp.sqrt(vn / c2) + ADAM_EPS) + ADAM_WD * w_ref[...])
        mo_ref[...] = mn
        vo_ref[...] = vn

    blk = pl.BlockSpec((tr, cn), lambda i: (i, 0))
    o = _sds((r, cn))
    outs = _pcall(body, name=name, grid=(r // tr,), in_specs=[blk] * 4, out_specs=[blk] * 3, out_shape=[o, o, o],
                  compiler_params=_cp(("parallel",)))(*[a.reshape(r, cn) for a in (w, g, m, v)])
    return tuple(a.reshape(shape) for a in outs)


def _local_step(xs, target, modt, nw, fnw, get_w, get_ffn, put_g, ev, od, lc):
    t, dm = xs.shape
    nct = lc // ROW_TILE
    depth = nw.shape[0]
    cs, sn = _rope_tables(t, lc)
    saved = []
    x_in, x1p, fp = xs, None, None
    for i in range(depth):
        j, even = i // 2, i % 2 == 0
        tag = f"l{i}"
        w, deps = get_w(i, [fp] if i else [])
        if i == 0:
            _, h = _rnm(tag + "_norm1", x_in, None, None, 0, modt[0], 0, 1, nw[0, 0], nct, deps)
        else:
            x_in, h = _rnm(tag + "_norm1", x1p, fp, modt[i - 1], 5, modt[i], 0, 1, nw[i, 0], nct, deps)
        s = dict(x=x_in, h=h, w=w)
        if even:
            p = _mm_cols(tag + "_in", h, w["in"])
            q, k, v = _even_qkv(tag + "_qkv", p, cs, sn)
            of, ob, ss = _retention_fwd(tag + "_ret", q, k, v, ev["lgb"][j], lc)
            mix, yc = _even_mix(tag + "_mix", p, of, ob, ev["cw"][j], ev["lnw"][j], ev["lnb"][j], nct)
            y = _mm_full(tag + "_out", mix, w["out"], NN)
            s.update(p=p, q=q, k=k, v=v, of=of, ob=ob, ss=ss, yc=yc)
        else:
            p = _mm_cols(tag + "_in", h, w["in"])
            mix, m = _odd_mix(tag + "_mix", p, od["pw"][j], od["ps"][j], od["lnw"][j], od["lnb"][j],
                              od["sgw"][j], od["sgb"][j], nct, lc)
            y = _mm_full(tag + "_out", mix, w["out"], NN)
            s.update(p=p, m=m)
        x1, h2 = _rnm(tag + "_norm2", x_in, y, modt[i], 2, modt[i], 3, 4, nw[i, 1], nct)
        w.update(get_ffn(i, [y]))
        a, gt, up = _ffn_up(tag + "_ffn_up", h2, w["gate"], w["up"])
        f = _mm_full(tag + "_ffn_down", a, w["down"], NN)
        s.update(mix=mix, y=y, x1=x1, h2=h2, a=a, gt=gt, up=up, f=f)
        saved.append(s)
        x1p, fp = x1, f

    loss_blk, dx, df, fin_s = _fin("final", x1p, fp, modt[depth - 1], 5, fnw, target, nct)

    deps = []
    dmod = [[None] * 6 for _ in range(depth)]
    dnw = [[None, None] for _ in range(depth)]
    zero2 = jnp.zeros((2, dm), F32)
    dmod[depth - 1][5] = jnp.stack([zero2[0], fin_s[0]])
    small = dict(dfnw=fin_s[1], ev=[], od=[])
    for i in reversed(range(depth)):
        j, even = i // 2, i % 2 == 0
        tag = f"l{i}b"
        s = saved[i]
        w = s["w"]
        fh = w["down"].shape[0] // 2
        g = {}
        dgt, dup = _ffn_down_bwd(tag + "_ffn_down", df, w["down"], s["gt"], s["up"])
        g["down"] = _wgrad_rows(tag + "_gdown", s["a"], fh, df)
        g["gate"] = _wgrad_rows(tag + "_ggate", dgt, fh, s["h2"])
        g["up"] = _wgrad_rows(tag + "_gup", dup, fh, s["h2"])
        dh2 = _ffn_in_bwd(tag + "_ffn_in", dgt, dup, w["gate"], w["up"])
        dx1, dy, s2 = _bnm(tag + "_norm2", s["x1"], dh2, dx, s["y"], modt[i], 3, 4, modt[i], 2, nw[i, 1], nct)
        dmod[i][3], dmod[i][4], dmod[i][2] = s2[:, 0], s2[:, 1], s2[:, 2]
        dnw[i][1] = s2[1, 3]
        dmix = _mm_full(tag + "_out", dy, w["out"], NT)
        g["out"] = _wgrad_rows(tag + "_gout", s["mix"], w["out"].shape[0] // 2, dy)
        if even:
            do, dg, dyc, lns = _even_mix_bwd1(tag + "_mix1", dmix, s["p"], s["of"], s["ob"], s["yc"],
                                              ev["lnw"][j], ev["lnb"][j])
            da, dgb, dcw = _even_conv_bwd(tag + "_conv", dyc, s["p"], ev["cw"][j], nct)
            dqf, dkf, dvf, dqb, dkb, dvb, dl = _retention_bwd(tag + "_ret", s["q"], s["k"], s["v"], do, s["ss"],
                                                              ev["lgb"][j], lc)
            dp = _even_dp(tag + "_dp", (dqf, dqb), (dkf, dkb), (dvf, dvb), dg, da, dgb, cs, sn)
            small["ev"].append(dict(lnw=lns[0], lnb=lns[1], cw=dcw, dl=dl[:, 0]))
        else:
            dm_, dpd, vec, dpw, dsgw, dsgb = _odd_mix_bwd1(tag + "_mix1", dmix, s["p"], s["m"], od["pw"][j], od["ps"][j],
                                                           od["lnw"][j], od["lnb"][j], od["sgw"][j], od["sgb"][j])
            dp = _odd_dp(tag + "_dp", dm_, dpd, nct, lc)
            small["od"].append(dict(ps=vec[0], lnw=vec[1], lnb=vec[2], pw=dpw, sgw=dsgw, sgb=dsgb[:, :, 0]))
        dh = _mm_cols_bwd(tag + "_in", dp, w["in"])
        g["in"] = _wgrad_cols(tag + "_gin", s["h"], dp, w["in"].shape[0])
        deps = put_g(i, g)
        if i > 0:
            dx, df, s1 = _bnm(tag + "_norm1", s["x"], dh, dx1, saved[i - 1]["f"], modt[i], 0, 1, modt[i - 1], 5,
                              nw[i, 0], nct, deps)
            dmod[i - 1][5] = s1[:, 2]
        else:
            dx, _, s1 = _bnm(tag + "_norm1", s["x"], dh, dx1, None, modt[0], 0, 1, None, 0, nw[0, 0], nct, deps)
        dmod[i][0], dmod[i][1] = s1[:, 0], s1[:, 1]
        dnw[i][0] = s1[1, 3]
    small["ev"].reverse()
    small["od"].reverse()
    dmod_t = jnp.stack([jnp.concatenate([jnp.stack(rows, axis=1), jnp.zeros((2, 2, dm), F32)], axis=1) for rows in dmod])
    small["dmod"] = dmod_t
    small["dnw"] = jnp.stack([jnp.stack(r) for r in dnw])
    return loss_blk, dx, small


WEIGHTS = ["c_ctx", "ada_w", "ada_b", "norm_w", "even_w_in", "even_w_out", "ret_decay_logit", "conv_dw_w",
           "conv_ln_w", "conv_ln_b", "odd_w_in", "odd_w_out", "pool_w", "pool_scale", "sg_ln_w", "sg_ln_b",
           "sg_w", "sg_b", "ffn_w_gate", "ffn_w_up", "ffn_w_down", "final_norm_w"]
BIG = dict(even_in="even_w_in", even_out="even_w_out", odd_in="odd_w_in", odd_out="odd_w_out",
           gate="ffn_w_gate", up="ffn_w_up", down="ffn_w_down")


def _rows(a, width=1024):
    flat = a.reshape(-1)
    n = flat.shape[0]
    per = 8 * width
    tot = -(-n // per) * per
    return jnp.pad(flat, (0, tot - n)).reshape(tot // width, width)


def _unshard(parts, lead):
    nl = len(lead)
    perm = tuple(range(1, nl + 1)) + (0, nl + 1)
    return parts.transpose(perm).reshape(tuple(lead) + (4 * parts.shape[-1],))


def _my_cols(a, chip, n):
    start = (0,) * (a.ndim - 1) + (chip * n,)
    return lax.dynamic_slice(a, start, a.shape[:-1] + (n,))


def kernel(x, c, ctx, c_ctx, ada_w, ada_b, norm_w, even_w_in, even_w_out, ret_decay_logit, conv_dw_w, conv_ln_w, conv_ln_b, odd_w_in, odd_w_out, pool_w, pool_scale, sg_ln_w, sg_ln_b, sg_w, sg_b, ffn_w_gate, ffn_w_up, ffn_w_down, final_norm_w, loss_target, m_c_ctx, m_ada_w, m_ada_b, m_norm_w, m_even_w_in, m_even_w_out, m_ret_decay_logit, m_conv_dw_w, m_conv_ln_w, m_conv_ln_b, m_odd_w_in, m_odd_w_out, m_pool_w, m_pool_scale, m_sg_ln_w, m_sg_ln_b, m_sg_w, m_sg_b, m_ffn_w_gate, m_ffn_w_up, m_ffn_w_down, m_final_norm_w, v_c_ctx, v_ada_w, v_ada_b, v_norm_w, v_even_w_in, v_even_w_out, v_ret_decay_logit, v_conv_dw_w, v_conv_ln_w, v_conv_ln_b, v_odd_w_in, v_odd_w_out, v_pool_w, v_pool_scale, v_sg_ln_w, v_sg_ln_b, v_sg_w, v_sg_b, v_ffn_w_gate, v_ffn_w_up, v_ffn_w_down, v_final_norm_w):
    wv = dict(c_ctx=c_ctx, ada_w=ada_w, ada_b=ada_b, norm_w=norm_w, even_w_in=even_w_in, even_w_out=even_w_out,
              ret_decay_logit=ret_decay_logit, conv_dw_w=conv_dw_w, conv_ln_w=conv_ln_w, conv_ln_b=conv_ln_b,
              odd_w_in=odd_w_in, odd_w_out=odd_w_out, pool_w=pool_w, pool_scale=pool_scale, sg_ln_w=sg_ln_w,
              sg_ln_b=sg_ln_b, sg_w=sg_w, sg_b=sg_b, ffn_w_gate=ffn_w_gate, ffn_w_up=ffn_w_up,
              ffn_w_down=ffn_w_down, final_norm_w=final_norm_w)
    mv = dict(zip(WEIGHTS, (m_c_ctx, m_ada_w, m_ada_b, m_norm_w, m_even_w_in, m_even_w_out, m_ret_decay_logit,
                            m_conv_dw_w, m_conv_ln_w, m_conv_ln_b, m_odd_w_in, m_odd_w_out, m_pool_w, m_pool_scale,
                            m_sg_ln_w, m_sg_ln_b, m_sg_w, m_sg_b, m_ffn_w_gate, m_ffn_w_up, m_ffn_w_down,
                            m_final_norm_w)))
    vv = dict(zip(WEIGHTS, (v_c_ctx, v_ada_w, v_ada_b, v_norm_w, v_even_w_in, v_even_w_out, v_ret_decay_logit,
                            v_conv_dw_w, v_conv_ln_w, v_conv_ln_b, v_odd_w_in, v_odd_w_out, v_pool_w, v_pool_scale,
                            v_sg_ln_w, v_sg_ln_b, v_sg_w, v_sg_b, v_ffn_w_gate, v_ffn_w_up, v_ffn_w_down,
                            v_final_norm_w)))
    xi, yi, ci = lax.axis_index("x"), lax.axis_index("y"), lax.axis_index("c")
    chip = 2 * xi + yi
    dev = 4 * xi + 2 * yi + ci
    dm = x.shape[-1]
    lc = ctx.shape[1]
    depth = ada_w.shape[0]
    n_ada = ada_w.shape[-1]

    cw_pad = jnp.pad(conv_dw_w, ((0, 0), (0, 1), (0, 0)))
    vec3 = jnp.stack([pool_scale, sg_ln_w, sg_ln_b])
    pack1 = jnp.concatenate([_rows(c), _rows(norm_w), _rows(cw_pad), _rows(vec3)], axis=0)
    g1 = _all_gather8("gather_small", pack1).reshape(8, 32, dm)
    c_all = g1[:, 0]
    per_chip = g1[0::2]
    norm_full = _unshard(per_chip[:, 8:10].reshape(4, depth, 2, dm // 4), (depth, 2))
    cw_full = _unshard(per_chip[:, 16:24].reshape(4, 2, CONV_K + 1, 128), (2, CONV_K + 1))
    vec_full = _unshard(per_chip[:, 24, :768].reshape(4, 3, 2, 128), (3, 2))

    c16 = jnp.concatenate([c_all, c_ctx[None, :], jnp.zeros((7, dm), F32)], axis=0)
    mod_sh = _ada_mod("ada_mod", c16, ada_w, _my_cols(ada_b, chip, n_ada)[:, None, :])
    g2 = _all_gather8("gather_mod", mod_sh.reshape(depth * 16, n_ada)).reshape(8, depth, 16, n_ada)
    mod_full = _unshard(g2[0::2], (depth, 16))
    mod_x = lax.dynamic_index_in_dim(mod_full, dev, axis=1, keepdims=False).reshape(depth, 6, dm)
    mod_c = mod_full[:, 8].reshape(depth, 6, dm)
    modt = jnp.pad(jnp.stack([mod_c, mod_x], axis=1), ((0, 0), (0, 0), (0, 2), (0, 0)))

    names = list(BIG)
    tr_names = ("gate", "up")
    shard = {k: (jnp.swapaxes(wv[BIG[k]], 1, 2) if k in tr_names else wv[BIG[k]]).astype(BF16) for k in names}
    roles = ("in", "out", "gate", "up", "down")

    def layer_keys(i):
        mixer = ("even_in", "even_out") if i % 2 == 0 else ("odd_in", "odd_out")
        return [(k, i // 2) for k in mixer] + [(k, i) for k in ("gate", "up", "down")]

    def as_used(got):
        return {r: (g if r == "in" else g.reshape(4 * g.shape[1], g.shape[2])) for r, g in zip(roles, got)}

    started = {}

    def start_gather(tag, keys, before):
        srcs = [shard[k][l] for k, l in keys]
        lands = [lax.empty((4,) + s.shape, s.dtype) for s in srcs]
        return _split_start(f"gather_start{tag}", srcs, lands, _gather_copies(len(srcs)), before)

    def get_w(i, after):
        if i > 0:
            got = _split_wait(f"gather_wait{i}", started[i], _gather_copies(len(roles)), after)[1]
            return as_used(got), []
        got = _gather_weights("gather_w0", [shard[k][l] for k, l in layer_keys(0)[:2]], [modt])
        started["0f"] = start_gather("0f", layer_keys(0)[2:], [got[0]])
        last = started["0f"]
        for li in range(1, depth):
            started[li] = last = start_gather(li, layer_keys(li), [got[0], last[4]])
        return as_used(got), [last[4]]

    def get_ffn(i, after):
        if i > 0:
            return {}
        got = _split_wait("gather_wait0f", started["0f"], _gather_copies(3), after)[1]
        return {r: g.reshape(4 * g.shape[1], g.shape[2]) for r, g in zip(roles[2:], got)}

    idx = jnp.stack([ci, chip]).astype(jnp.int32)
    pairs, pending = {}, {}

    def finish_pair(i, after):
        glist, from_sib = _split_wait(f"pair_wait{i}", pairs[i], _pair_copies(len(roles)), after)
        pair = [_add_half(f"rs_add{i}_{r}", gl, a, idx) for r, gl, a in zip(roles, glist, from_sib)]
        lands = [lax.empty((3,) + p.shape[1:], p.dtype) for p in pair]
        st = _split_start(f"rs_start{i}", pair, lands, _scatter_copies(len(roles)), [])
        pending[i] = (glist, from_sib, st)
        return [st[4]]

    def put_g(i, g):
        glist = [g[r].reshape(4, -1, g[r].shape[-1]) for r in roles]
        lands = [lax.empty((4, gl.shape[1] // 2, gl.shape[2]), gl.dtype) for gl in glist]
        pairs[i] = _split_start(f"pair_start{i}", glist, lands, _pair_copies(len(roles)), [])
        tokens = [pairs[i][4]]
        if i + 1 < depth:
            tokens += finish_pair(i + 1, [pairs[i][4]])
        return tokens

    ev = dict(lgb=jnp.broadcast_to(ret_decay_logit.reshape(-1, 2 * HEADS)[:, :, None], (ret_decay_logit.shape[0], 2 * HEADS, HEAD_DIM)),
              cw=cw_full, lnw=conv_ln_w[:, None, :], lnb=conv_ln_b[:, None, :])
    od = dict(pw=pool_w, ps=vec_full[0][:, None, :], lnw=vec_full[1][:, None, :], lnb=vec_full[2][:, None, :],
              sgw=sg_w, sgb=jnp.broadcast_to(sg_b[:, :, :, None], sg_b.shape + (GC,)))
    xs = jnp.concatenate([ctx[0], x[0]], axis=0)
    loss_blk, dxs, small = _local_step(xs, loss_target[0], modt, norm_full[:, :, None, :], final_norm_w[None, :],
                                       get_w, get_ffn, put_g, ev, od, lc)

    misc = jnp.stack([
        small["dfnw"], jnp.broadcast_to(loss_blk[0, 0], (dm,)),
        jnp.concatenate([e["lnw"] for e in small["ev"]]), jnp.concatenate([e["lnb"] for e in small["ev"]]),
        jnp.concatenate([o["ps"] for o in small["od"]]), jnp.concatenate([o["lnw"] for o in small["od"]]),
        jnp.concatenate([o["lnb"] for o in small["od"]]),
        jnp.pad(jnp.concatenate([e["dl"] for e in small["ev"]]), (0, dm - 4 * HEADS)),
        jnp.stack([o["sgb"] for o in small["od"]]).reshape(-1)])
    pack2 = jnp.concatenate([
        _rows(small["dmod"]), _rows(small["dnw"]), _rows(misc), _rows(jnp.stack([e["cw"] for e in small["ev"]])),
        _rows(jnp.stack([o["pw"] for o in small["od"]])), _rows(jnp.stack([o["sgw"] for o in small["od"]]))], axis=0)
    n2 = pack2.shape[0]
    g3 = _all_gather8("gather_grads", pack2)
    tot = _sum8("sum_grads", g3.reshape(8, n2, dm))
    r_mod = depth * 16
    o_nw, o_misc = r_mod, r_mod + 8
    o_cw = o_misc + 16
    o_pw = o_cw + 2 * (CONV_K + 1) // 2
    o_sgw = o_pw + 128
    dmod_sum = tot[:r_mod].reshape(depth, 2, 8, dm)
    dmod_dev = g3.reshape(8, n2, dm)[:, :r_mod].reshape(8, depth, 2, 8, dm)
    dm_x = dmod_dev[:, :, 1, :6].reshape(8, depth, 6 * dm).transpose(1, 0, 2)
    dm_c = dmod_sum[:, 0, :6].reshape(depth, 1, 6 * dm)
    dmod16 = jnp.concatenate([dm_x, dm_c, jnp.zeros((depth, 7, 6 * dm), F32)], axis=1)
    g_ada_b = _rowsum16("ada_b_grad", dmod16)[:, 0]
    g_ada_w, dc16 = _ada_bwd("ada_bwd", c16, _my_cols(dmod16, chip, n_ada), ada_w)
    g4 = _all_gather8("gather_cctx", dc16[8:16]).reshape(8, 8, dm)
    g_c_ctx = _cctx_grad("cctx_grad", g4[0::2, 0], c_ctx[None, :])[0]

    misc_t = tot[o_misc:o_misc + 16]
    half = lambda row: misc_t[row].reshape(2, dm // 2)
    grads = dict(
        c_ctx=g_c_ctx, ada_w=g_ada_w, ada_b=g_ada_b,
        norm_w=_my_cols(tot[o_nw:o_nw + 8].reshape(depth, 2, dm), chip, dm // 4),
        ret_decay_logit=misc_t[7, :4 * HEADS].reshape(ret_decay_logit.shape),
        conv_dw_w=_my_cols(tot[o_cw:o_cw + 2 * (CONV_K + 1) // 2].reshape(2, CONV_K + 1, dm // 2)[:, :CONV_K], chip, 128),
        conv_ln_w=half(2), conv_ln_b=half(3),
        pool_w=tot[o_pw:o_pw + 128].reshape(pool_w.shape),
        pool_scale=_my_cols(half(4), chip, 128), sg_ln_w=_my_cols(half(5), chip, 128), sg_ln_b=_my_cols(half(6), chip, 128),
        sg_w=tot[o_sgw:o_sgw + 128].reshape(sg_w.shape), sg_b=misc_t[8].reshape(sg_b.shape),
        final_norm_w=misc_t[0])
    loss = misc_t[1, 0]

    last_tokens = finish_pair(0, [g_c_ctx])
    deltas, new_m, new_v = {}, {}, {}
    for n in WEIGHTS:
        if n not in BIG.values():
            deltas[n], new_m[n], new_v[n] = _adamw("adamw_" + n, wv[n], grads[n], mv[n], vv[n])
    reduced = {k: None for k in names}
    for i in reversed(range(depth)):
        glist, from_sib, st = pending[i]
        after = last_tokens if i else [deltas["ada_w"]] + [reduced[k] for k, _ in layer_keys(1)]
        slots = _split_wait(f"rs_wait{i}", st, _scatter_copies(len(roles)), after)[1]
        for (k, l), g, a, b in zip(layer_keys(i), glist, from_sib, slots):
            reduced[k] = _sum_final(f"rs_sum_{k}{l}", g, a, b, idx, reduced[k], l, shard[k].shape[0])
    shards = dict(zip(names, _rs_share("rs_share", [reduced[k] for k in names])))

    for k in names:
        n = BIG[k]
        if k in tr_names:
            tr = lambda a: jnp.swapaxes(a, 1, 2)
            outs = _adamw("adamw_" + n, tr(wv[n]), shards[k], tr(mv[n]), tr(vv[n]))
            grads[n] = tr(shards[k])
            deltas[n], new_m[n], new_v[n] = (tr(o) for o in outs)
        else:
            grads[n] = shards[k]
            deltas[n], new_m[n], new_v[n] = _adamw("adamw_" + n, wv[n], grads[n], mv[n], vv[n])
    grad_x = dxs[lc:][None]
    return (loss, grad_x, *[grads[n] for n in WEIGHTS], *[deltas[n] for n in WEIGHTS],
            *[new_m[n] for n in WEIGHTS], *[new_v[n] for n in WEIGHTS])
```

```python
import functools
import math

import jax
import jax.numpy as jnp
from jax import lax
from jax.experimental import pallas as pl
from jax.experimental.pallas import tpu as pltpu

F32 = jnp.float32
BF16 = jnp.bfloat16
MESH = pl.DeviceIdType.MESH

EPS = 1e-6
GRID_W = 64
HEADS = 4
HEAD_DIM = 128
CHUNK = 128
CONV_K = 31
ROPE_BASE = 10000.0
ROPE_PAIRS = (16, 24, 24)
POOL_WINDOWS = (2, 4, 8, 16)
ADAM_LR, ADAM_B1, ADAM_B2, ADAM_EPS, ADAM_WD, ADAM_STEP = 0.001, 0.9, 0.999, 1e-08, 0.01, 10

ROW_TILE = 256
CONV_HALO = 16
POOL_HALO = 8
VMEM_LIMIT = 56 * 1024 * 1024
WGRAD_ROWS = 2304


def _pcall(body, **kw):
    return pl.pallas_call(body, **kw)


def _cp(sem=None, vmem=VMEM_LIMIT):
    if sem is None:
        return pltpu.CompilerParams(vmem_limit_bytes=vmem)
    return pltpu.CompilerParams(dimension_semantics=sem, vmem_limit_bytes=vmem)


def _sds(shape, dtype=F32):
    return jax.ShapeDtypeStruct(tuple(shape), dtype)


def _full(shape):
    nd = len(shape)
    return pl.BlockSpec(tuple(shape), lambda *_: (0,) * nd)


def _sigmoid(x):
    return jax.nn.sigmoid(x)


def _silu(x):
    return x * _sigmoid(x)


def _dsilu(x):
    s = _sigmoid(x)
    return s * (1.0 + x * (1.0 - s))


def _colsum(a):
    return jnp.sum(a, axis=0, keepdims=True)


def _dot(a, b, dn):
    return lax.dot_general(a.astype(BF16), b.astype(BF16), dn, preferred_element_type=F32)


NN = (((1,), (0,)), ((), ()))
NT = (((1,), (1,)), ((), ()))
TN = (((0,), (0,)), ((), ()))


def _mm_tile(t, cap=1152):
    best = 16
    for d in range(16, min(t, cap) + 1, 16):
        if t % d == 0:
            best = d
    return best


def _mm(name, pairs, grid, out_shape, out_spec, dn):
    npairs = len(pairs)
    nk = grid[-1]
    kax = len(grid) - 1
    assert nk == 1 or out_shape.dtype == F32

    def body(*refs):
        ins = refs[:2 * npairs]
        o_ref = refs[2 * npairs]
        tot = None
        for p in range(npairs):
            d = _dot(ins[2 * p][...], ins[2 * p + 1][...], dn)
            tot = d if tot is None else tot + d
        if nk == 1:
            o_ref[...] = tot.astype(o_ref.dtype)
        else:
            k = pl.program_id(kax)

            @pl.when(k == 0)
            def _():
                o_ref[...] = tot

            @pl.when(k != 0)
            def _():
                o_ref[...] += tot

    args, in_specs = [], []
    for a, a_spec, b, b_spec in pairs:
        args += [a, b]
        in_specs += [a_spec, b_spec]
    sem = ("parallel",) * kax + ("arbitrary",)
    return _pcall(body, name=name, grid=grid, in_specs=in_specs, out_specs=out_spec, out_shape=out_shape,
                  compiler_params=_cp(sem))(*args)


def _mm_cols(name, a, w, out_dtype=F32):
    t, k = a.shape
    j, _, n = w.shape
    tm = _mm_tile(t)
    return _mm(name, [(a, pl.BlockSpec((tm, k), lambda i, jj, kk: (i, 0)),
                       w, pl.BlockSpec((None, k, n), lambda i, jj, kk: (jj, 0, 0)))],
               (t // tm, j, 1), _sds((t, j * n), out_dtype), pl.BlockSpec((tm, n), lambda i, jj, kk: (i, jj)), NN)


def _mm_cols_bwd(name, d, w):
    t = d.shape[0]
    j, k, n = w.shape
    tm = _mm_tile(t)
    pairs = [(d, pl.BlockSpec((tm, n), functools.partial(lambda jj, i, u, kk: (i, jj), jj)),
              w, pl.BlockSpec((None, k, n), functools.partial(lambda jj, i, u, kk: (jj, 0, 0), jj))) for jj in range(j)]
    return _mm(name, pairs, (t // tm, 1, 1), _sds((t, k), BF16), pl.BlockSpec((tm, k), lambda i, u, kk: (i, 0)), NT)


def _mm_full(name, a, w, dn, tm=None):
    t, k = a.shape
    n = w.shape[1] if dn is NN else w.shape[0]
    tm = tm or _mm_tile(t)
    return _mm(name, [(a, pl.BlockSpec((tm, k), lambda i, u, kk: (i, 0)), w, _full(w.shape))],
               (t // tm, 1, 1), _sds((t, n)), pl.BlockSpec((tm, n), lambda i, u, kk: (i, 0)), dn)


def _wgrad_cols(name, a, b, j):
    t, k = a.shape
    n = b.shape[1] // j
    tt = _mm_tile(t, 2 * WGRAD_ROWS)
    return _mm(name, [(a, pl.BlockSpec((tt, k), lambda jj, u, kk: (kk, 0)),
                       b, pl.BlockSpec((tt, n), lambda jj, u, kk: (kk, jj)))],
               (j, 1, t // tt), _sds((j, k, n)), pl.BlockSpec((None, k, n), lambda jj, u, kk: (jj, 0, 0)), TN)


def _wgrad_rows(name, a, blk, b):
    t, f = a.shape
    n = b.shape[1]
    tt = _mm_tile(t, WGRAD_ROWS)
    return _mm(name, [(a, pl.BlockSpec((tt, blk), lambda jj, u, kk: (kk, jj)),
                       b, pl.BlockSpec((tt, n), lambda jj, u, kk: (kk, 0)))],
               (f // blk, 1, t // tt), _sds((f, n)), pl.BlockSpec((blk, n), lambda jj, u, kk: (jj, 0)), TN)


def _ffn_tiles(t, f):
    return _mm_tile(t, 288), f


def _ffn_up(name, h, wgt, wut):
    t, k = h.shape
    f = wgt.shape[0]
    tm, tn = _ffn_tiles(t, f)

    def body(h_ref, wg_ref, wu_ref, a_ref, gt_ref, up_ref):
        hv = h_ref[...]
        gt = _dot(hv, wg_ref[...], NT)
        up = _dot(hv, wu_ref[...], NT)
        a_ref[...] = (_silu(gt) * up).astype(BF16)
        gt_ref[...] = gt.astype(BF16)
        up_ref[...] = up.astype(BF16)

    wspec = pl.BlockSpec((tn, k), lambda i, jj: (jj, 0))
    ospec = pl.BlockSpec((tm, tn), lambda i, jj: (i, jj))
    o = _sds((t, f), BF16)
    return _pcall(body, name=name, grid=(t // tm, f // tn),
                  in_specs=[pl.BlockSpec((tm, k), lambda i, jj: (i, 0)), wspec, wspec],
                  out_specs=[ospec, ospec, ospec], out_shape=[o, o, o],
                  compiler_params=_cp(("parallel", "parallel")))(h, wgt, wut)


def _ffn_down_bwd(name, df, wd, gt, up):
    t, dm = df.shape
    f = wd.shape[0]
    tm, tn = _ffn_tiles(t, f)

    def body(df_ref, wd_ref, gt_ref, up_ref, dgt_ref, dup_ref):
        da = _dot(df_ref[...], wd_ref[...], NT)
        g = gt_ref[...].astype(F32)
        u = up_ref[...].astype(F32)
        s = _sigmoid(g)
        dgt_ref[...] = (da * u * (s * (1.0 + g * (1.0 - s)))).astype(BF16)
        dup_ref[...] = (da * (g * s)).astype(BF16)

    aspec = pl.BlockSpec((tm, tn), lambda i, jj: (i, jj))
    o = _sds((t, f), BF16)
    return _pcall(body, name=name, grid=(t // tm, f // tn),
                  in_specs=[pl.BlockSpec((tm, dm), lambda i, jj: (i, 0)),
                            pl.BlockSpec((tn, dm), lambda i, jj: (jj, 0)), aspec, aspec],
                  out_specs=[aspec, aspec], out_shape=[o, o],
                  compiler_params=_cp(("parallel", "parallel")))(df, wd, gt, up)


def _ffn_in_bwd(name, dgt, dup, wgt, wut):
    t, f = dgt.shape
    k = wgt.shape[1]
    tm = _mm_tile(t, 576)
    aspec = pl.BlockSpec((tm, f), lambda i, u, kk: (i, 0))
    wspec = pl.BlockSpec((f, k), lambda i, u, kk: (0, 0))
    return _mm(name, [(dgt, aspec, wgt, wspec), (dup, aspec, wut, wspec)], (t // tm, 1, 1), _sds((t, k), BF16),
               pl.BlockSpec((tm, k), lambda i, u, kk: (i, 0)), NN)


def _modrow(ref, row, is_ctx):
    return jnp.where(is_ctx, ref[0, row:row + 1, :], ref[1, row:row + 1, :])


def _rnm(name, x, delta, mod_g, g_row, mod_n, sh_row, sc_row, nw, nct, deps=()):
    t, dm = x.shape
    tm = ROW_TILE
    has = delta is not None
    nd = len(deps)

    def body(*refs):
        refs = refs[:len(refs) - nd - (2 if has else 1)] + refs[len(refs) - (2 if has else 1):]
        if has:
            x_ref, d_ref, mg_ref, m_ref, nw_ref, xo_ref, h_ref = refs
        else:
            x_ref, m_ref, nw_ref, h_ref = refs
        is_ctx = pl.program_id(0) < nct
        xv = x_ref[...]
        if has:
            xv = xv + _modrow(mg_ref, g_row, is_ctx) * d_ref[...]
            xo_ref[...] = xv
        r = lax.rsqrt(jnp.mean(xv * xv, axis=-1, keepdims=True) + EPS)
        hv = (xv * r * nw_ref[...]) * (1.0 + _modrow(m_ref, sc_row, is_ctx)) + _modrow(m_ref, sh_row, is_ctx)
        h_ref[...] = hv.astype(BF16)

    row = pl.BlockSpec((tm, dm), lambda i: (i, 0))
    ins = [x] + ([delta, mod_g] if has else []) + [mod_n, nw] + list(deps)
    in_specs = ([row] + ([row, _full(mod_g.shape)] if has else []) + [_full(mod_n.shape), _full(nw.shape)]
                + [_full(d.shape) for d in deps])
    outs = ([_sds((t, dm))] if has else []) + [_sds((t, dm), BF16)]
    out_specs = ([row] if has else []) + [row]
    res = _pcall(body, name=name, grid=(t // tm,), in_specs=in_specs, out_specs=out_specs, out_shape=outs,
                 compiler_params=_cp(("parallel",)))(*ins)
    return res if has else (None, res[0])


def _bnm(name, xn, dh, dup, yprev, mod_n, sh_row, sc_row, mod_g, g_row, nw, nct, deps=()):
    t, dm = xn.shape
    tm = ROW_TILE
    has = yprev is not None
    nd = len(deps)

    def body(*refs):
        nout = 3 if has else 2
        refs = refs[:len(refs) - nd - nout] + refs[len(refs) - nout:]
        if has:
            x_ref, dh_ref, du_ref, y_ref, mn_ref, mg_ref, nw_ref, dx_ref, dd_ref, s_ref = refs
        else:
            x_ref, dh_ref, du_ref, mn_ref, nw_ref, dx_ref, s_ref = refs
        i = pl.program_id(0)
        is_ctx = i < nct

        @pl.when(i == 0)
        def _():
            s_ref[...] = jnp.zeros_like(s_ref)

        xv = x_ref[...]
        r = lax.rsqrt(jnp.mean(xv * xv, axis=-1, keepdims=True) + EPS)
        xh = xv * r
        w = nw_ref[...]
        sc1 = 1.0 + _modrow(mn_ref, sc_row, is_ctx)
        dhv = dh_ref[...].astype(F32)
        dxh = dhv * sc1 * w
        dx = r * (dxh - xh * jnp.mean(dxh * xh, axis=-1, keepdims=True)) + du_ref[...]
        dx_ref[...] = dx
        parts = [_colsum(dhv), _colsum(dhv * (xh * w))]
        if has:
            dd_ref[...] = (_modrow(mg_ref, g_row, is_ctx) * dx).astype(BF16)
            parts.append(_colsum(dx * y_ref[...]))
        else:
            parts.append(jnp.zeros((1, dm), F32))
        upd = jnp.concatenate(parts + [jnp.zeros((5, dm), F32)], axis=0)
        dnw = jnp.concatenate([jnp.zeros((3, dm), F32), _colsum(dhv * sc1 * xh), jnp.zeros((4, dm), F32)], axis=0)

        @pl.when(is_ctx)
        def _():
            s_ref[0] += upd
            s_ref[1] += dnw

        @pl.when(jnp.logical_not(is_ctx))
        def _():
            s_ref[1] += upd + dnw

    row = pl.BlockSpec((tm, dm), lambda i: (i, 0))
    ins = [xn, dh, dup] + ([yprev] if has else []) + [mod_n] + ([mod_g] if has else []) + [nw] + list(deps)
    in_specs = ([row, row, row] + ([row] if has else []) + [_full(mod_n.shape)]
                + ([_full(mod_g.shape)] if has else []) + [_full(nw.shape)] + [_full(d.shape) for d in deps])
    outs = [_sds((t, dm))] + ([_sds((t, dm), BF16)] if has else []) + [_sds((2, 8, dm))]
    out_specs = [row] + ([row] if has else []) + [_full((2, 8, dm))]
    res = _pcall(body, name=name, grid=(t // tm,), in_specs=in_specs, out_specs=out_specs, out_shape=outs,
                 compiler_params=_cp(("arbitrary",)))(*ins)
    return res if has else (res[0], None, res[1])


def _fin(name, x1, f, mod, g_row, fw, target, nct):
    t, dm = x1.shape
    tm = ROW_TILE

    def body(x_ref, f_ref, m_ref, fw_ref, t_ref, loss_ref, dx_ref, dd_ref, s_ref):
        i = pl.program_id(0)

        @pl.when(i == 0)
        def _():
            s_ref[...] = jnp.zeros_like(s_ref)
            loss_ref[...] = jnp.zeros_like(loss_ref)

        @pl.when(i < nct)
        def _():
            dx_ref[...] = jnp.zeros_like(dx_ref)
            dd_ref[...] = jnp.zeros_like(dd_ref)

        @pl.when(i >= nct)
        def _():
            g = m_ref[1, g_row:g_row + 1, :]
            fv = f_ref[...]
            xv = x_ref[...] + g * fv
            r = lax.rsqrt(jnp.mean(xv * xv, axis=-1, keepdims=True) + EPS)
            xh = xv * r
            w = fw_ref[...]
            err = xh * w - t_ref[...]
            loss_ref[...] += 0.5 * jnp.sum(err * err) / dm
            dout = err * (1.0 / dm)
            dxh = dout * w
            dx = r * (dxh - xh * jnp.mean(dxh * xh, axis=-1, keepdims=True))
            dx_ref[...] = dx
            dd_ref[...] = (g * dx).astype(BF16)
            s_ref[...] += jnp.concatenate([_colsum(dx * fv), _colsum(dout * xh), jnp.zeros((6, dm), F32)], axis=0)

    row = pl.BlockSpec((tm, dm), lambda i: (i, 0))
    trow = pl.BlockSpec((tm, dm), lambda i: (jnp.maximum(i - nct, 0), 0))
    return _pcall(body, name=name, grid=(t // tm,),
                  in_specs=[row, row, _full(mod.shape), _full(fw.shape), trow],
                  out_specs=[_full((8, 128)), row, row, _full((8, dm))],
                  out_shape=[_sds((8, 128)), _sds((t, dm)), _sds((t, dm), BF16), _sds((8, dm))],
                  compiler_params=_cp(("arbitrary",)))(x1, f, mod, fw, target)


def _rope_tables(t, lc):
    l = t - lc
    rows = l // GRID_W
    grid_r = jnp.broadcast_to(jnp.arange(rows, dtype=F32)[:, None], (rows, GRID_W)).reshape(-1)
    grid_c = jnp.broadcast_to(jnp.arange(GRID_W, dtype=F32)[None, :], (rows, GRID_W)).reshape(-1)

    def angles(p_seq, p_row, p_col):
        parts = []
        for p, n in zip((p_seq, p_row, p_col), ROPE_PAIRS):
            freq = ROPE_BASE ** (-jnp.arange(n, dtype=F32) / n)
            parts.append(p[:, None] * freq[None, :])
        return jnp.concatenate(parts, axis=-1)

    zc = jnp.zeros((lc,), F32)
    ang = jnp.concatenate([angles(jnp.arange(lc, dtype=F32), zc, zc),
                           angles(jnp.full((l,), lc, F32), grid_r, grid_c)], axis=0)
    cos, sin = jnp.cos(ang), jnp.sin(ang)
    return jnp.concatenate([cos, cos], axis=-1), jnp.concatenate([-sin, sin], axis=-1)


def _rope(u, cs, sn):
    return u * cs + pltpu.roll(u, HEAD_DIM // 2, 1) * sn


def _rope_t(d, cs, sn):
    return d * cs + pltpu.roll(d * sn, HEAD_DIM // 2, 1)


def _even_qkv(name, p, cs, sn):
    t = p.shape[0]
    tm = ROW_TILE
    w = HEADS * HEAD_DIM
    scale = HEAD_DIM ** -0.5

    def body(q_ref, k_ref, v_ref, cs_ref, sn_ref, qo_ref, ko_ref, vo_ref):
        c, s = cs_ref[...], sn_ref[...]
        for h in range(HEADS):
            sl = slice(h * HEAD_DIM, (h + 1) * HEAD_DIM)
            qo_ref[:, sl] = (_rope(q_ref[:, sl], c, s) * scale).astype(BF16)
            ko_ref[:, sl] = _rope(k_ref[:, sl], c, s).astype(BF16)
        vo_ref[...] = v_ref[...].astype(BF16)

    col = lambda j: pl.BlockSpec((tm, w), lambda i: (i, j))
    tab = pl.BlockSpec((tm, HEAD_DIM), lambda i: (i, 0))
    o = _sds((t, w), BF16)
    return _pcall(body, name=name, grid=(t // tm,), in_specs=[col(0), col(1), col(2), tab, tab],
                  out_specs=[col(0)] * 3, out_shape=[o, o, o], compiler_params=_cp(("parallel",)))(p, p, p, cs, sn)


def _log_sigmoid_row(x):
    e = jnp.exp(-jnp.abs(x))
    l1p = jnp.where(e < 0.01, e * (1.0 - e * (0.5 - e * (1.0 / 3.0))), jnp.log(1.0 + e))
    return jnp.minimum(x, 0.0) - l1p


def _ret_tables(lgb_ref, dm_ref, xi_ref, zt_ref):
    ri = lax.broadcasted_iota(jnp.int32, (CHUNK, CHUNK), 0).astype(F32)
    ci = lax.broadcasted_iota(jnp.int32, (CHUNK, CHUNK), 1).astype(F32)
    for d in range(2):
        for h in range(HEADS):
            idx = d * HEADS + h
            lg = _log_sigmoid_row(lgb_ref[idx:idx + 1, :])
            if d == 0:
                e, mask = ri - ci, ri >= ci
                xe, ze = ri + 1.0, (CHUNK - 1.0) - ri
            else:
                e, mask = ci - ri - 1.0, ci > ri
                xe, ze = (CHUNK - 1.0) - ri, ri
            dm_ref[idx] = jnp.where(mask, jnp.exp(lg * jnp.where(mask, e, 0.0)), 0.0)
            xi_ref[idx] = jnp.exp(lg * xe)
            zt_ref[idx] = jnp.exp(lg * ze)


def _ret_exponents(d):
    ri = lax.broadcasted_iota(jnp.int32, (CHUNK, CHUNK), 0).astype(F32)
    ci = lax.broadcasted_iota(jnp.int32, (CHUNK, CHUNK), 1).astype(F32)
    if d == 0:
        return ri - ci, ri + 1.0, (CHUNK - 1.0) - ri
    return ci - ri - 1.0, (CHUNK - 1.0) - ri, ri


def _bwd_chunk(n, ncc, nc):
    return jnp.where(n < ncc, ncc - 1 - n, nc - 1 - (n - ncc))


def _retention_fwd(name, q, k, v, lgb, lc):
    t, w = q.shape
    nc, ncc = t // CHUNK, lc // CHUNK
    nh = 2 * HEADS

    def body(qf_ref, kf_ref, vf_ref, qb_ref, kb_ref, vb_ref, lgb_ref, of_ref, ob_ref, ss_ref,
             s_ref, dm_ref, xi_ref, zt_ref):
        n = pl.program_id(0)

        @pl.when(n == 0)
        def _():
            s_ref[...] = jnp.zeros_like(s_ref)
            _ret_tables(lgb_ref, dm_ref, xi_ref, zt_ref)

        for d in range(2):
            q_ref, k_ref, v_ref, o_ref = (qf_ref, kf_ref, vf_ref, of_ref) if d == 0 else (qb_ref, kb_ref, vb_ref, ob_ref)
            for h in range(HEADS):
                idx = d * HEADS + h
                sl = slice(h * HEAD_DIM, (h + 1) * HEAD_DIM)
                qv, kv, vv = q_ref[:, sl], k_ref[:, sl], v_ref[:, sl]
                s = s_ref[idx]
                ss_ref[idx] = s
                a = _dot(qv, kv, NT) * dm_ref[idx]
                o = _dot(a, vv, NN) + _dot(qv.astype(F32) * xi_ref[idx], s, NN)
                o_ref[:, sl] = o
                gc = jnp.exp(_log_sigmoid_row(lgb_ref[idx:idx + 1, :]) * float(CHUNK))
                s_ref[idx] = gc * s + _dot(kv.astype(F32) * zt_ref[idx], vv, TN)

    fspec = pl.BlockSpec((CHUNK, w), lambda n: (n, 0))
    bspec = pl.BlockSpec((CHUNK, w), lambda n: (_bwd_chunk(n, ncc, nc), 0))
    tab = pltpu.VMEM((nh, CHUNK, CHUNK), F32)
    return _pcall(body, name=name, grid=(nc,),
                  in_specs=[fspec] * 3 + [bspec] * 3 + [_full((nh, HEAD_DIM))],
                  out_specs=[fspec, bspec, pl.BlockSpec((None, nh, CHUNK, CHUNK), lambda n: (n, 0, 0, 0))],
                  out_shape=[_sds((t, w)), _sds((t, w)), _sds((nc, nh, CHUNK, CHUNK))],
                  scratch_shapes=[tab, tab, tab, tab],
                  compiler_params=_cp(("arbitrary",)))(q, k, v, q, k, v, lgb)


def _retention_bwd(name, q, k, v, do, ss, lgb, lc):
    t, w = q.shape
    nc, ncc = t // CHUNK, lc // CHUNK
    nh = 2 * HEADS

    def body(qf_ref, kf_ref, vf_ref, gf_ref, qb_ref, kb_ref, vb_ref, gb_ref, ss_ref, lgb_ref,
             dqf_ref, dkf_ref, dvf_ref, dqb_ref, dkb_ref, dvb_ref, dl_ref,
             ds_ref, dm_ref, xi_ref, zt_ref, acc_ref):
        n = pl.program_id(0)

        @pl.when(n == 0)
        def _():
            ds_ref[...] = jnp.zeros_like(ds_ref)
            acc_ref[...] = jnp.zeros_like(acc_ref)
            _ret_tables(lgb_ref, dm_ref, xi_ref, zt_ref)

        for d in range(2):
            if d == 0:
                q_ref, k_ref, v_ref, g_ref, dq_ref, dk_ref, dv_ref = qf_ref, kf_ref, vf_ref, gf_ref, dqf_ref, dkf_ref, dvf_ref
            else:
                q_ref, k_ref, v_ref, g_ref, dq_ref, dk_ref, dv_ref = qb_ref, kb_ref, vb_ref, gb_ref, dqb_ref, dkb_ref, dvb_ref
            ee, xe, ze = _ret_exponents(d)
            for h in range(HEADS):
                idx = d * HEADS + h
                sl = slice(h * HEAD_DIM, (h + 1) * HEAD_DIM)
                qv, kv, vv, gv = q_ref[:, sl], k_ref[:, sl], v_ref[:, sl], g_ref[:, sl]
                s = ss_ref[idx]
                dsp = ds_ref[idx]
                dmat, xi, zt = dm_ref[idx], xi_ref[idx], zt_ref[idx]
                qf32, kf32 = qv.astype(F32), kv.astype(F32)
                a = _dot(qv, kv, NT) * dmat
                dar = _dot(gv, vv, NT)
                da = dar * dmat
                t1 = _dot(gv, s, NT)
                t2 = _dot(vv, dsp, NT)
                dq_ref[:, sl] = _dot(da, kv, NN) + xi * t1
                dk_ref[:, sl] = _dot(da, qv, TN) + zt * t2
                dv_ref[:, sl] = _dot(a, gv, TN) + _dot(kf32 * zt, dsp, NN)
                gc = jnp.exp(_log_sigmoid_row(lgb_ref[idx:idx + 1, :]) * float(CHUNK))
                ds_ref[idx] = gc * dsp + _dot(qf32 * xi, gv, TN)
                acc_ref[idx] += (ee * a * dar + xe * xi * qf32 * t1 + ze * zt * kf32 * t2
                                 + (float(CHUNK) * gc) * dsp * s)

        @pl.when(n == nc - 1)
        def _():
            for idx in range(nh):
                tot = jnp.sum(acc_ref[idx])
                dl_ref[idx:idx + 1, :] = tot * _sigmoid(-lgb_ref[idx:idx + 1, :])

    fmap = lambda n: (nc - 1 - n, 0)
    bmap = lambda n: (_bwd_chunk(nc - 1 - n, ncc, nc), 0)
    fspec = pl.BlockSpec((CHUNK, w), fmap)
    bspec = pl.BlockSpec((CHUNK, w), bmap)
    tab = pltpu.VMEM((nh, CHUNK, CHUNK), F32)
    o = _sds((t, w))
    return _pcall(body, name=name, grid=(nc,),
                  in_specs=[fspec] * 4 + [bspec] * 4
                  + [pl.BlockSpec((None, nh, CHUNK, CHUNK), lambda n: (nc - 1 - n, 0, 0, 0)), _full((nh, HEAD_DIM))],
                  out_specs=[fspec] * 3 + [bspec] * 3 + [_full((nh, HEAD_DIM))],
                  out_shape=[o] * 6 + [_sds((nh, HEAD_DIM))],
                  scratch_shapes=[tab, tab, tab, tab, tab],
                  compiler_params=_cp(("arbitrary",)))(q, k, v, do, q, k, v, do, ss, lgb)


def _halo_specs(tm, halo, t, width, col):
    hb = tm // halo
    last = t // halo - 1
    prev = pl.BlockSpec((halo, width), lambda i: (jnp.maximum(i * hb - 1, 0), col))
    nxt = pl.BlockSpec((halo, width), lambda i: (jnp.minimum((i + 1) * hb, last), col))
    return prev, nxt


def _halo_valid(i, nct, nt):
    vp = jnp.logical_and(i != 0, i != nct)
    vn = jnp.logical_and(i != nct - 1, i != nt - 1)
    return vp, vn


def _fill_window(win_ref, prev, cur, nxt, vp, vn, halo, tm):
    win_ref[0:halo, :] = jnp.where(vp, prev, 0.0)
    win_ref[halo:halo + tm, :] = cur
    win_ref[halo + tm:halo + tm + halo, :] = jnp.where(vn, nxt, 0.0)


CONV_SUB = 64


SUBLANES = 8


def _shift_window(win_ref, sh_ref, tm):
    rows = tm + 2 * CONV_HALO - SUBLANES
    for s in range(SUBLANES):
        sh_ref[s, 0:rows, :] = win_ref[s:s + rows, :]


def _window_rows(sh_ref, start, rows):
    s = start % SUBLANES
    return sh_ref[s, start - s:start - s + rows, :]


def _conv_taps(sh_ref, w_ref, tm, flip):
    outs = []
    for r0 in range(0, tm, CONV_SUB):
        acc = None
        for kk in range(CONV_K):
            wk = (CONV_K - 1 - kk) if flip else kk
            term = w_ref[wk:wk + 1, :] * _window_rows(sh_ref, r0 + kk + 1, CONV_SUB)
            acc = term if acc is None else acc + term
        outs.append(acc)
    return jnp.concatenate(outs, axis=0)


def _head_norm(y):
    r = lax.rsqrt(jnp.mean(y * y, axis=-1, keepdims=True) + EPS)
    return y * r, r


def _ln_stats(y):
    mu = jnp.mean(y, axis=-1, keepdims=True)
    yc = y - mu
    rs = lax.rsqrt(jnp.mean(yc * yc, axis=-1, keepdims=True) + EPS)
    return yc * rs, rs


def _ln_bwd(dyh, yh, rs):
    return rs * (dyh - jnp.mean(dyh, axis=-1, keepdims=True) - yh * jnp.mean(dyh * yh, axis=-1, keepdims=True))


def _even_mix(name, p, of, ob, cw, lnw, lnb, nct):
    t = p.shape[0]
    tm, halo = ROW_TILE, CONV_HALO
    nt = t // tm
    w = HEADS * HEAD_DIM

    def body(g_ref, a_ref, gb_ref, ap_ref, gbp_ref, an_ref, gbn_ref, of_ref, ob_ref, cw_ref, lw_ref, lb_ref,
             mix_ref, yc_ref, win_ref, sh_ref):
        i = pl.program_id(0)
        vp, vn = _halo_valid(i, nct, nt)
        glu = lambda a, b: a * _sigmoid(b)
        _fill_window(win_ref, glu(ap_ref[...], gbp_ref[...]), glu(a_ref[...], gb_ref[...]),
                     glu(an_ref[...], gbn_ref[...]), vp, vn, halo, tm)
        _shift_window(win_ref, sh_ref, tm)
        yc = _conv_taps(sh_ref, cw_ref, tm, False)
        yc_ref[...] = yc
        yh, _ = _ln_stats(yc)
        mix_ref[:, w:2 * w] = _silu(yh * lw_ref[...] + lb_ref[...]).astype(BF16)
        for h in range(HEADS):
            sl = slice(h * HEAD_DIM, (h + 1) * HEAD_DIM)
            yn, _ = _head_norm(of_ref[:, sl] + ob_ref[:, sl])
            mix_ref[:, sl] = (_silu(g_ref[:, sl]) * yn).astype(BF16)

    col = lambda j: pl.BlockSpec((tm, w), lambda i: (i, j))
    ap, an = _halo_specs(tm, halo, t, w, 4)
    gp, gn = _halo_specs(tm, halo, t, w, 5)
    row = pl.BlockSpec((tm, w), lambda i: (i, 0))
    return _pcall(body, name=name, grid=(nt,),
                  in_specs=[col(3), col(4), col(5), ap, gp, an, gn, row, row,
                            _full(cw.shape), _full(lnw.shape), _full(lnb.shape)],
                  out_specs=[pl.BlockSpec((tm, 2 * w), lambda i: (i, 0)), row],
                  out_shape=[_sds((t, 2 * w), BF16), _sds((t, w))],
                  scratch_shapes=[pltpu.VMEM((tm + 2 * halo, w), F32), pltpu.VMEM((SUBLANES, tm + 2 * halo, w), F32)],
                  compiler_params=_cp(("parallel",)))(p, p, p, p, p, p, p, of, ob, cw, lnw, lnb)


def _even_mix_bwd1(name, dmix, p, of, ob, yc, lnw, lnb):
    t = p.shape[0]
    tm = ROW_TILE
    w = HEADS * HEAD_DIM

    def body(dr_ref, dc_ref, g_ref, of_ref, ob_ref, yc_ref, lw_ref, lb_ref, do_ref, dg_ref, dyc_ref, s_ref):
        @pl.when(pl.program_id(0) == 0)
        def _():
            s_ref[...] = jnp.zeros_like(s_ref)

        for h in range(HEADS):
            sl = slice(h * HEAD_DIM, (h + 1) * HEAD_DIM)
            yn, r = _head_norm(of_ref[:, sl] + ob_ref[:, sl])
            gv = g_ref[:, sl]
            dr = dr_ref[:, sl]
            dg_ref[:, sl] = (dr * yn * _dsilu(gv)).astype(BF16)
            dyn = dr * _silu(gv)
            do_ref[:, sl] = (r * (dyn - yn * jnp.mean(dyn * yn, axis=-1, keepdims=True))).astype(BF16)
        yh, rs = _ln_stats(yc_ref[...])
        lw = lw_ref[...]
        dlo = dc_ref[...] * _dsilu(yh * lw + lb_ref[...])
        dyc_ref[...] = _ln_bwd(dlo * lw, yh, rs)
        s_ref[...] += jnp.concatenate([_colsum(dlo * yh), _colsum(dlo), jnp.zeros((6, w), F32)], axis=0)

    col = lambda j: pl.BlockSpec((tm, w), lambda i: (i, j))
    row = pl.BlockSpec((tm, w), lambda i: (i, 0))
    return _pcall(body, name=name, grid=(t // tm,),
                  in_specs=[col(0), col(1), col(3), row, row, row, _full(lnw.shape), _full(lnb.shape)],
                  out_specs=[row, row, row, _full((8, w))],
                  out_shape=[_sds((t, w), BF16), _sds((t, w), BF16), _sds((t, w)), _sds((8, w))],
                  compiler_params=_cp(("arbitrary",)))(dmix, dmix, p, of, ob, yc, lnw, lnb)


def _even_conv_bwd(name, dyc, p, cw, nct):
    t = p.shape[0]
    tm, halo = ROW_TILE, CONV_HALO
    nt = t // tm
    w = HEADS * HEAD_DIM

    def body(d_ref, dp_ref, dn_ref, a_ref, gb_ref, ap_ref, gbp_ref, an_ref, gbn_ref, cw_ref,
             da_ref, dgb_ref, dw_ref, dwin_ref, uwin_ref, dsh_ref, ush_ref):
        i = pl.program_id(0)

        @pl.when(i == 0)
        def _():
            dw_ref[...] = jnp.zeros_like(dw_ref)

        vp, vn = _halo_valid(i, nct, nt)
        glu = lambda a, b: a * _sigmoid(b)
        dcur = d_ref[...]
        _fill_window(dwin_ref, dp_ref[...], dcur, dn_ref[...], vp, vn, halo, tm)
        _fill_window(uwin_ref, glu(ap_ref[...], gbp_ref[...]), glu(a_ref[...], gb_ref[...]),
                     glu(an_ref[...], gbn_ref[...]), vp, vn, halo, tm)
        _shift_window(dwin_ref, dsh_ref, tm)
        _shift_window(uwin_ref, ush_ref, tm)
        du = _conv_taps(dsh_ref, cw_ref, tm, True)
        av = a_ref[...]
        sg = _sigmoid(gb_ref[...])
        da_ref[...] = (du * sg).astype(BF16)
        dgb_ref[...] = (du * av * sg * (1.0 - sg)).astype(BF16)
        rows = [_colsum(dcur * _window_rows(ush_ref, kk + 1, tm)) for kk in range(CONV_K)]
        dw_ref[...] += jnp.concatenate(rows + [jnp.zeros((1, w), F32)], axis=0)

    col = lambda j: pl.BlockSpec((tm, w), lambda i: (i, j))
    row = pl.BlockSpec((tm, w), lambda i: (i, 0))
    dp, dn = _halo_specs(tm, halo, t, w, 0)
    ap, an = _halo_specs(tm, halo, t, w, 4)
    gp, gn = _halo_specs(tm, halo, t, w, 5)
    win = pltpu.VMEM((tm + 2 * halo, w), F32)
    shifted = pltpu.VMEM((SUBLANES, tm + 2 * halo, w), F32)
    return _pcall(body, name=name, grid=(nt,),
                  in_specs=[row, dp, dn, col(4), col(5), ap, gp, an, gn, _full(cw.shape)],
                  out_specs=[row, row, _full((CONV_K + 1, w))],
                  out_shape=[_sds((t, w), BF16), _sds((t, w), BF16), _sds((CONV_K + 1, w))],
                  scratch_shapes=[win, win, shifted, shifted],
                  compiler_params=_cp(("arbitrary",)))(dyc, dyc, dyc, p, p, p, p, p, p, cw)


def _even_dp(name, dqs, dks, dvs, dg, da, dgb, cs, sn):
    t, w = dg.shape
    tm = ROW_TILE
    scale = HEAD_DIM ** -0.5

    def body(dqf_ref, dqb_ref, dkf_ref, dkb_ref, dvf_ref, dvb_ref, dg_ref, da_ref, dgb_ref, cs_ref, sn_ref, dp_ref):
        c, s = cs_ref[...], sn_ref[...]
        for h in range(HEADS):
            sl = slice(h * HEAD_DIM, (h + 1) * HEAD_DIM)
            dp_ref[:, sl] = (_rope_t(dqf_ref[:, sl] + dqb_ref[:, sl], c, s) * scale).astype(BF16)
            dp_ref[:, w + h * HEAD_DIM:w + (h + 1) * HEAD_DIM] = _rope_t(dkf_ref[:, sl] + dkb_ref[:, sl], c, s).astype(BF16)
        dp_ref[:, 2 * w:3 * w] = (dvf_ref[...] + dvb_ref[...]).astype(BF16)
        dp_ref[:, 3 * w:4 * w] = dg_ref[...]
        dp_ref[:, 4 * w:5 * w] = da_ref[...]
        dp_ref[:, 5 * w:6 * w] = dgb_ref[...]

    row = pl.BlockSpec((tm, w), lambda i: (i, 0))
    tab = pl.BlockSpec((tm, HEAD_DIM), lambda i: (i, 0))
    return _pcall(body, name=name, grid=(t // tm,), in_specs=[row] * 9 + [tab, tab],
                  out_specs=pl.BlockSpec((tm, 6 * w), lambda i: (i, 0)), out_shape=_sds((t, 6 * w), BF16),
                  compiler_params=_cp(("parallel",)))(dqs[0], dqs[1], dks[0], dks[1], dvs[0], dvs[1], dg, da, dgb, cs, sn)


GROUPS = 4
GC = 128
INV_SQRT2 = 0.7071067811865476
INV_SQRT_2PI = 0.3989422804014327


def _gelu(x):
    return 0.5 * x * (1.0 + lax.erf(x * INV_SQRT2))


def _dgelu(x):
    return 0.5 * (1.0 + lax.erf(x * INV_SQRT2)) + x * jnp.exp(-0.5 * x * x) * INV_SQRT_2PI


def _pool_count(i, nct, lc, t, tm, rows, row0, left, right):
    is_ctx = i < nct
    seg_start = jnp.where(is_ctx, 0, lc)
    seg_len = jnp.where(is_ctx, lc, t - lc)
    pos = i * tm + row0 - seg_start + lax.broadcasted_iota(jnp.int32, (rows, GC), 0)
    cnt = jnp.minimum(pos + right, seg_len - 1) - jnp.maximum(pos - left, 0) + 1
    return jnp.maximum(cnt, 1).astype(F32)


def _spatial_gate(vln, sgw_ref, sgb_ref, tm):
    cols = []
    for g in range(GROUPS):
        sl = slice(g * GC, (g + 1) * GC)
        parts = [_dot(sgw_ref[g], vln[r0:r0 + CHUNK, sl], NN) + sgb_ref[g] for r0 in range(0, tm, CHUNK)]
        cols.append(jnp.concatenate(parts, axis=0))
    return jnp.concatenate(cols, axis=1)


def _odd_mix(name, p, pw, pscale, lnw, lnb, sgw, sgb, nct, lc):
    t = p.shape[0]
    tm, halo = ROW_TILE, POOL_HALO
    nt = t // tm
    w = GROUPS * GC

    def body(pc_ref, pp_ref, pn_ref, pu_ref, pv_ref, pw_ref, ps_ref, lw_ref, lb_ref, sgw_ref, sgb_ref,
             mix_ref, m_ref, win_ref):
        i = pl.program_id(0)
        vp, vn = _halo_valid(i, nct, nt)
        pc = pc_ref[...]
        _fill_window(win_ref, pp_ref[...], pc, pn_ref[...], vp, vn, halo, tm)
        for g, wd in enumerate(POOL_WINDOWS):
            sl = slice(g * GC, (g + 1) * GC)
            left = wd // 2
            right = wd - 1 - left
            s = None
            for o in range(-left, right + 1):
                term = win_ref[halo + o:halo + o + tm, sl]
                s = term if s is None else s + term
            mg = s / _pool_count(i, nct, lc, t, tm, tm, 0, left, right) - pc[:, sl]
            m_ref[:, sl] = mg
            mix_ref[:, sl] = (_dot(mg, pw_ref[g], NN) * ps_ref[:, sl]).astype(BF16)
        u = _gelu(pu_ref[...])
        vh, _ = _ln_stats(_gelu(pv_ref[...]))
        s = _spatial_gate(vh * lw_ref[...] + lb_ref[...], sgw_ref, sgb_ref, tm)
        mix_ref[:, w:2 * w] = (u * s).astype(BF16)

    col = lambda j: pl.BlockSpec((tm, w), lambda i: (i, j))
    pp, pn = _halo_specs(tm, halo, t, w, 0)
    return _pcall(body, name=name, grid=(nt,),
                  in_specs=[col(0), pp, pn, col(1), col(2), _full(pw.shape), _full(pscale.shape),
                            _full(lnw.shape), _full(lnb.shape), _full(sgw.shape), _full(sgb.shape)],
                  out_specs=[pl.BlockSpec((tm, 2 * w), lambda i: (i, 0)), col(0)],
                  out_shape=[_sds((t, 2 * w), BF16), _sds((t, w))],
                  scratch_shapes=[pltpu.VMEM((tm + 2 * halo, w), F32)],
                  compiler_params=_cp(("parallel",)))(p, p, p, p, p, pw, pscale, lnw, lnb, sgw, sgb)


def _odd_mix_bwd1(name, dmix, p, m, pw, pscale, lnw, lnb, sgw, sgb):
    t = p.shape[0]
    tm = ROW_TILE
    w = GROUPS * GC

    def body(dpo_ref, dsg_ref, pu_ref, pv_ref, m_ref, pw_ref, ps_ref, lw_ref, lb_ref, sgw_ref, sgb_ref,
             dm_ref, dpd_ref, vec_ref, dpw_ref, dsgw_ref, dsgb_ref):
        @pl.when(pl.program_id(0) == 0)
        def _():
            vec_ref[...] = jnp.zeros_like(vec_ref)
            dpw_ref[...] = jnp.zeros_like(dpw_ref)
            dsgw_ref[...] = jnp.zeros_like(dsgw_ref)
            dsgb_ref[...] = jnp.zeros_like(dsgb_ref)

        dscale = []
        for g in range(GROUPS):
            sl = slice(g * GC, (g + 1) * GC)
            mg = m_ref[:, sl]
            dpo = dpo_ref[:, sl]
            dscale.append(_colsum(dpo * _dot(mg, pw_ref[g], NN)))
            dpo = dpo * ps_ref[:, sl]
            dm_ref[:, sl] = _dot(dpo, pw_ref[g], NT)
            dpw_ref[g] += _dot(mg, dpo, TN)
        pu, pv = pu_ref[...], pv_ref[...]
        u = _gelu(pu)
        vh, rs = _ln_stats(_gelu(pv))
        lw = lw_ref[...]
        vln = vh * lw + lb_ref[...]
        s = _spatial_gate(vln, sgw_ref, sgb_ref, tm)
        dsg = dsg_ref[...]
        dpd_ref[:, 0:w] = (dsg * s * _dgelu(pu)).astype(BF16)
        ds = dsg * u
        cols = []
        for g in range(GROUPS):
            sl = slice(g * GC, (g + 1) * GC)
            parts = []
            for r0 in range(0, tm, CHUNK):
                dsc = ds[r0:r0 + CHUNK, sl]
                parts.append(_dot(sgw_ref[g], dsc, TN))
                dsgw_ref[g] += _dot(dsc, vln[r0:r0 + CHUNK, sl], NT)
                dsgb_ref[g] += dsc
            cols.append(jnp.concatenate(parts, axis=0))
        dvln = jnp.concatenate(cols, axis=1)
        dpd_ref[:, w:2 * w] = (_ln_bwd(dvln * lw, vh, rs) * _dgelu(pv)).astype(BF16)
        vec_ref[...] += jnp.concatenate([jnp.concatenate(dscale, axis=1), _colsum(dvln * vh), _colsum(dvln),
                                         jnp.zeros((5, w), F32)], axis=0)

        @pl.when(pl.program_id(0) == t // tm - 1)
        def _():
            for g in range(GROUPS):
                dsgb_ref[g] = jnp.broadcast_to(jnp.sum(dsgb_ref[g], axis=1, keepdims=True), (GC, GC))

    col = lambda j: pl.BlockSpec((tm, w), lambda i: (i, j))
    mat = _full((GROUPS, GC, GC))
    return _pcall(body, name=name, grid=(t // tm,),
                  in_specs=[col(0), col(1), col(1), col(2), col(0), _full(pw.shape), _full(pscale.shape),
                            _full(lnw.shape), _full(lnb.shape), _full(sgw.shape), _full(sgb.shape)],
                  out_specs=[col(0), pl.BlockSpec((tm, 2 * w), lambda i: (i, 0)), _full((8, w)), mat, mat, mat],
                  out_shape=[_sds((t, w)), _sds((t, 2 * w), BF16), _sds((8, w)),
                             _sds((GROUPS, GC, GC)), _sds((GROUPS, GC, GC)), _sds((GROUPS, GC, GC))],
                  compiler_params=_cp(("arbitrary",)))(dmix, dmix, p, p, m, pw, pscale, lnw, lnb, sgw, sgb)


def _odd_dp(name, dm, dpd, nct, lc):
    t, w = dm.shape
    tm, halo = ROW_TILE, POOL_HALO
    nt = t // tm

    def body(d_ref, dp_ref, dn_ref, dpd_ref, o_ref, win_ref):
        i = pl.program_id(0)
        vp, vn = _halo_valid(i, nct, nt)
        dcur = d_ref[...]
        _fill_window(win_ref, dp_ref[...], dcur, dn_ref[...], vp, vn, halo, tm)
        for g, wd in enumerate(POOL_WINDOWS):
            sl = slice(g * GC, (g + 1) * GC)
            left = wd // 2
            right = wd - 1 - left
            win_ref[:, sl] = win_ref[:, sl] / _pool_count(i, nct, lc, t, tm, tm + 2 * halo, -halo, left, right)
            s = None
            for o in range(-right, left + 1):
                term = win_ref[halo + o:halo + o + tm, sl]
                s = term if s is None else s + term
            o_ref[:, sl] = (s - dcur[:, sl]).astype(BF16)
        o_ref[:, w:3 * w] = dpd_ref[...]

    row = pl.BlockSpec((tm, w), lambda i: (i, 0))
    pp, pn = _halo_specs(tm, halo, t, w, 0)
    return _pcall(body, name=name, grid=(nt,),
                  in_specs=[row, pp, pn, pl.BlockSpec((tm, 2 * w), lambda i: (i, 0))],
                  out_specs=pl.BlockSpec((tm, 3 * w), lambda i: (i, 0)), out_shape=_sds((t, 3 * w), BF16),
                  scratch_shapes=[pltpu.VMEM((tm + 2 * halo, w), F32)],
                  compiler_params=_cp(("parallel",)))(dm, dm, dm, dpd)


def _place():
    x, y, c = lax.axis_index("x"), lax.axis_index("y"), lax.axis_index("c")
    chips = [(1 - x, y), (x, 1 - y), (1 - x, 1 - y)]
    return x, y, c, chips


def _chip_index(cx, cy):
    return 2 * cx + cy


def _all_gather8(name, blk, after=()):
    m_per, n = blk.shape
    na = len(after)

    def body(x_ref, *rest):
        out_ref, send_sems, recv_sems, local_sem = rest[na:]
        x, y, c, chips = _place()
        me, sibling = (x, y, c), (x, y, 1 - c)

        def rows(px, py, pc):
            return out_ref.at[pl.ds((4 * px + 2 * py + pc) * m_per, m_per), :]

        def copy(k, block, to, src=None):
            return pltpu.make_async_remote_copy(
                src_ref=rows(*block) if src is None else src, dst_ref=rows(*block),
                send_sem=send_sems.at[k], recv_sem=recv_sems.at[k], device_id=to, device_id_type=MESH)

        mine = pltpu.make_async_copy(x_ref, rows(*me), local_sem)
        mine.start()
        first = [copy(0, me, sibling, src=x_ref)]
        first += [copy(1 + j, me, (*chip, c), src=x_ref) for j, chip in enumerate(chips)]
        for cp in first:
            cp.start()
        passed = [copy(4 + j, (*chip, c), sibling) for j, chip in enumerate(chips)]
        for j, chip in enumerate(chips):
            copy(1 + j, (*chip, c), me).wait_recv()
            passed[j].start()
        copy(0, sibling, me).wait_recv()
        for j, chip in enumerate(chips):
            copy(4 + j, (*chip, 1 - c), me).wait_recv()
        for cp in first + passed:
            cp.wait_send()
        mine.wait()

    return _pcall(body, name=name, out_shape=_sds((8 * m_per, n), blk.dtype),
                  in_specs=[pl.BlockSpec(memory_space=pltpu.VMEM)] + [pl.BlockSpec(memory_space=pl.ANY)] * na,
                  out_specs=pl.BlockSpec(memory_space=pltpu.VMEM),
                  scratch_shapes=[pltpu.SemaphoreType.DMA((7,)), pltpu.SemaphoreType.DMA((7,)), pltpu.SemaphoreType.DMA],
                  compiler_params=_cp())(blk, *after)


ANY = pl.BlockSpec(memory_space=pl.ANY)


def _half(which, rows):
    return pl.ds(pl.multiple_of(which * rows, 16), rows)


def _gather_weights(name, ws, after=()):
    nw = len(ws)
    na = len(after)
    ns = 7

    def body(*refs):
        w_refs, o_refs = refs[:nw], refs[nw + na:2 * nw + na]
        send_sems, recv_sems = refs[2 * nw + na:]
        x, y, c, chips = _place()
        me_chip = _chip_index(x, y)
        sibling = (x, y, 1 - c)

        def rcopy(t, k, src, dst, to):
            return pltpu.make_async_remote_copy(src_ref=src, dst_ref=dst, send_sem=send_sems.at[t * ns + k],
                                                recv_sem=recv_sems.at[t * ns + k], device_id=to, device_id_type=MESH)

        sends = []
        for t in range(nw):
            lh = w_refs[t].shape[0] // 2
            for k, chip in enumerate(chips):
                sends.append(rcopy(t, k, w_refs[t].at[_half(c, lh)], o_refs[t].at[me_chip, _half(c, lh)], (*chip, c)))
                sends[-1].start()
            sends.append(rcopy(t, 6, w_refs[t], o_refs[t].at[me_chip], sibling))
            sends[-1].start()
        for t in range(nw):
            lh = w_refs[t].shape[0] // 2
            for k, chip in enumerate(chips):
                part = o_refs[t].at[_chip_index(*chip), _half(c, lh)]
                rcopy(t, k, part, part, (*chip, c)).wait_recv()
                sends.append(rcopy(t, 3 + k, part, part, sibling))
                sends[-1].start()
        for t in range(nw):
            lh = w_refs[t].shape[0] // 2
            own = o_refs[t].at[me_chip]
            rcopy(t, 6, own, own, sibling).wait_recv()
            for k, chip in enumerate(chips):
                part = o_refs[t].at[_chip_index(*chip), _half(1 - c, lh)]
                rcopy(t, 3 + k, part, part, sibling).wait_recv()
        for cp in sends:
            cp.wait_send()

    return _pcall(body, name=name, out_shape=[_sds((4,) + w.shape, w.dtype) for w in ws],
                  in_specs=[ANY] * (nw + na), out_specs=[ANY] * nw,
                  scratch_shapes=[pltpu.SemaphoreType.DMA((ns * nw,)), pltpu.SemaphoreType.DMA((ns * nw,))],
                  compiler_params=_cp())(*ws, *after)


def _rs_share(name, ss):
    ng = len(ss)

    def body(*refs):
        o_refs = refs[ng:2 * ng]
        send_sems, recv_sems = refs[2 * ng:]
        x, y, c, _ = _place()
        cps = []
        for t in range(ng):
            lh = o_refs[t].shape[1] // 2
            mine = o_refs[t].at[:, _half(c, lh)]
            cp = pltpu.make_async_remote_copy(
                src_ref=mine, dst_ref=mine, send_sem=send_sems.at[t], recv_sem=recv_sems.at[t],
                device_id=(x, y, 1 - c), device_id_type=MESH)
            cp.start()
            cps.append(cp)
        for t in range(ng):
            lh = o_refs[t].shape[1] // 2
            cps[t].wait_send()
            theirs = o_refs[t].at[:, _half(1 - c, lh)]
            pltpu.make_async_remote_copy(
                src_ref=theirs, dst_ref=theirs, send_sem=send_sems.at[t], recv_sem=recv_sems.at[t],
                device_id=(x, y, 1 - c), device_id_type=MESH).wait_recv()

    return _pcall(body, name=name, out_shape=[_sds(s.shape, s.dtype) for s in ss],
                  in_specs=[ANY] * ng, out_specs=[ANY] * ng, input_output_aliases={t: t for t in range(ng)},
                  scratch_shapes=[pltpu.SemaphoreType.DMA((ng,)), pltpu.SemaphoreType.DMA((ng,))],
                  compiler_params=_cp())(*ss)


HBM = pl.BlockSpec(memory_space=pltpu.HBM)
SEMS = pl.BlockSpec(memory_space=pltpu.SEMAPHORE)
EFFECT = pltpu.SideEffectType.DATAFLOW_SIDE_EFFECTING
TOKEN = (8, 128)


def _in_hbm(a):
    return pltpu.with_memory_space_constraint(a, pltpu.HBM)


def _split_start(name, srcs, lands, copies, after):
    ns, nl, na = len(srcs), len(lands), len(after)
    ncopies = len(copies([s for s in srcs], [l for l in lands], probe=True))

    def body(*refs):
        src_refs, land_refs = refs[:ns], refs[ns:ns + nl]
        send_sems, recv_sems = refs[ns + nl + na], refs[ns + nl + na + 1]
        token = refs[-1]
        for k, (src, dst, to) in enumerate(copies(src_refs, land_refs)):
            pltpu.make_async_remote_copy(src_ref=src, dst_ref=dst, send_sem=send_sems.at[k], recv_sem=recv_sems.at[k],
                                         device_id=to, device_id_type=MESH).start()
        token[...] = jnp.zeros_like(token)

    thru = [pltpu.HBM(a.shape, a.dtype) for a in list(srcs) + list(lands)]
    outs = _pcall(body, name=name,
                  out_shape=(pltpu.SemaphoreType.DMA((ncopies,)), pltpu.SemaphoreType.DMA((ncopies,)), *thru, _sds(TOKEN)),
                  in_specs=[HBM] * (ns + nl) + [ANY] * na,
                  out_specs=(SEMS, SEMS, *([HBM] * (ns + nl)), pl.BlockSpec(memory_space=pltpu.VMEM)),
                  input_output_aliases={t: 2 + t for t in range(ns + nl)},
                  compiler_params=pltpu.CompilerParams(has_side_effects=EFFECT))(
        *[_in_hbm(a) for a in list(srcs) + list(lands)], *after)
    return outs[0], outs[1], list(outs[2:2 + ns]), list(outs[2 + ns:2 + ns + nl]), outs[-1]


def _split_wait(name, started, copies, after):
    send_sems, recv_sems, srcs, lands, _ = started
    ns, nl, na = len(srcs), len(lands), len(after)

    def body(*refs):
        src_refs, land_refs = refs[:ns], refs[ns:ns + nl]
        send_sems_ref, recv_sems_ref = refs[ns + nl], refs[ns + nl + 1]
        for k, (src, dst, to) in enumerate(copies(src_refs, land_refs)):
            cp = pltpu.make_async_remote_copy(src_ref=src, dst_ref=dst, send_sem=send_sems_ref.at[k],
                                              recv_sem=recv_sems_ref.at[k], device_id=to, device_id_type=MESH)
            cp.wait_send()
            cp.wait_recv()

    thru = [pltpu.HBM(a.shape, a.dtype) for a in list(srcs) + list(lands)]
    outs = _pcall(body, name=name, out_shape=tuple(thru),
                  in_specs=[HBM] * (ns + nl) + [SEMS, SEMS] + [ANY] * na, out_specs=tuple([HBM] * (ns + nl)),
                  input_output_aliases={t: t for t in range(ns + nl)},
                  compiler_params=pltpu.CompilerParams(has_side_effects=EFFECT))(
        *srcs, *lands, send_sems, recv_sems, *after)
    return list(outs[:ns]), list(outs[ns:])


def _pair_copies(n):
    def copies(src_refs, land_refs, probe=False):
        if probe:
            return [None] * n
        x, y, c, _ = _place()
        return [(src_refs[t].at[:, _half(1 - c, src_refs[t].shape[1] // 2)], land_refs[t], (x, y, 1 - c))
                for t in range(n)]
    return copies


def _gather_copies(n):
    def copies(src_refs, land_refs, probe=False):
        if probe:
            return [None] * (4 * n)
        x, y, c, chips = _place()
        me_chip = _chip_index(x, y)
        out = []
        for t in range(n):
            for to in [(*chip, c) for chip in chips] + [(x, y, 1 - c)]:
                out.append((src_refs[t], land_refs[t].at[me_chip], to))
        return out
    return copies


def _scatter_copies(n):
    def copies(src_refs, land_refs, probe=False):
        if probe:
            return [None] * (3 * n)
        x, y, c, chips = _place()
        out = []
        for t in range(n):
            for k, chip in enumerate(chips):
                out.append((src_refs[t].at[_chip_index(*chip)], land_refs[t].at[k], (*chip, c)))
        return out
    return copies


def _row_block(r, cn):
    if r % 8:
        return r
    best = 8
    for d in range(8, r + 1, 8):
        if r % d == 0 and d * cn * 4 <= (2 << 20):
            best = d
    return best


def _add_half(name, g, a, idx):
    j, rh, cn = a.shape
    tr = _row_block(rh, cn)
    nb = rh // tr

    def body(i_ref, g_ref, a_ref, o_ref):
        o_ref[...] = (g_ref[...] + a_ref[...]).astype(BF16)

    blk = (None, tr, cn)
    gs = pltpu.PrefetchScalarGridSpec(
        num_scalar_prefetch=1, grid=(j, nb),
        in_specs=[pl.BlockSpec(blk, lambda jj, i, i_ref: (jj, i_ref[0] * nb + i, 0)),
                  pl.BlockSpec(blk, lambda jj, i, i_ref: (jj, i, 0))],
        out_specs=pl.BlockSpec(blk, lambda jj, i, i_ref: (jj, i, 0)))
    return _pcall(body, name=name, grid_spec=gs, out_shape=_sds(a.shape, BF16),
                  compiler_params=_cp(("parallel", "parallel")))(idx, g, a)


def _sum_final(name, g, a, b, idx, buf, lyr, nlyr):
    _, r, cn = g.shape
    rh = r // 2
    tr = _row_block(rh, cn)
    nb = rh // tr

    def body(*refs):
        g_ref, a_ref, b_ref = refs[1:4]
        o_ref = refs[-1]
        own = g_ref[...] + a_ref[...]
        o_ref[...] = (own + b_ref[0].astype(F32)) + (b_ref[1].astype(F32) + b_ref[2].astype(F32))

    blk = (None, tr, cn)
    in_specs = [pl.BlockSpec(blk, lambda i, i_ref: (i_ref[1], i_ref[0] * nb + i, 0)),
                pl.BlockSpec(blk, lambda i, i_ref: (i_ref[1], i, 0)),
                pl.BlockSpec((3, tr, cn), lambda i, i_ref: (0, i, 0))]
    args = [idx, g, a, b]
    kw = {}
    if buf is not None:
        in_specs.append(ANY)
        args.append(buf)
        kw["input_output_aliases"] = {4: 0}
    gs = pltpu.PrefetchScalarGridSpec(
        num_scalar_prefetch=1, grid=(nb,), in_specs=in_specs,
        out_specs=pl.BlockSpec(blk, lambda i, i_ref: (lyr, i_ref[0] * nb + i, 0)))
    return _pcall(body, name=name, grid_spec=gs, out_shape=_sds((nlyr, r, cn)),
                  compiler_params=_cp(("parallel",)), **kw)(*args)


def _sum8(name, g):
    _, r, n = g.shape
    tr = 8

    def body(g_ref, o_ref):
        o_ref[...] = ((g_ref[0] + g_ref[1]) + (g_ref[2] + g_ref[3])) + ((g_ref[4] + g_ref[5]) + (g_ref[6] + g_ref[7]))

    return _pcall(body, name=name, grid=(r // tr,), in_specs=[pl.BlockSpec((8, tr, n), lambda i: (0, i, 0))],
                  out_specs=pl.BlockSpec((tr, n), lambda i: (i, 0)), out_shape=_sds((r, n)),
                  compiler_params=_cp(("parallel",)))(g)


def _ada_mod(name, c16, ada_w, bias):
    nl, dm, n = ada_w.shape

    def body(c_ref, w_ref, b_ref, o_ref):
        o_ref[...] = _dot(_silu(c_ref[...]), w_ref[...], NN) + b_ref[...]

    return _pcall(body, name=name, grid=(nl,),
                  in_specs=[_full(c16.shape), pl.BlockSpec((None, dm, n), lambda i: (i, 0, 0)),
                            pl.BlockSpec((None, 1, n), lambda i: (i, 0, 0))],
                  out_specs=pl.BlockSpec((None, 16, n), lambda i: (i, 0, 0)), out_shape=_sds((nl, 16, n)),
                  compiler_params=_cp(("parallel",)))(c16, ada_w, bias)


def _ada_bwd(name, c16, dmod, ada_w):
    nl, dm, n = ada_w.shape

    def body(c_ref, d_ref, w_ref, gw_ref, dc_ref):
        @pl.when(pl.program_id(0) == 0)
        def _():
            dc_ref[...] = jnp.zeros_like(dc_ref)

        dv = d_ref[...]
        gw_ref[...] = _dot(_silu(c_ref[...]), dv, TN)
        dc_ref[...] += _dot(dv, w_ref[...], NT)

    return _pcall(body, name=name, grid=(nl,),
                  in_specs=[_full(c16.shape), pl.BlockSpec((None, 16, n), lambda i: (i, 0, 0)),
                            pl.BlockSpec((None, dm, n), lambda i: (i, 0, 0))],
                  out_specs=[pl.BlockSpec((None, dm, n), lambda i: (i, 0, 0)), _full((16, dm))],
                  out_shape=[_sds((nl, dm, n)), _sds((16, dm))],
                  compiler_params=_cp(("arbitrary",)))(c16, dmod, ada_w)


def _rowsum16(name, dmod):
    nl, _, n = dmod.shape

    def body(d_ref, o_ref):
        o_ref[...] = _colsum(d_ref[...])

    return _pcall(body, name=name, grid=(nl,), in_specs=[pl.BlockSpec((None, 16, n), lambda i: (i, 0, 0))],
                  out_specs=pl.BlockSpec((None, 1, n), lambda i: (i, 0, 0)), out_shape=_sds((nl, 1, n)),
                  compiler_params=_cp(("parallel",)))(dmod)


def _cctx_grad(name, parts, c_ctx):
    def body(p_ref, c_ref, o_ref):
        tot = (p_ref[0:1, :] + p_ref[1:2, :]) + (p_ref[2:3, :] + p_ref[3:4, :])
        o_ref[...] = tot * _dsilu(c_ref[...])

    return _pcall(body, name=name, out_shape=_sds(c_ctx.shape), compiler_params=_cp())(parts, c_ctx)


def _adamw(name, w, g, m, v):
    shape = w.shape
    cn = shape[-1]
    r = math.prod(shape[:-1]) if len(shape) > 1 else 1
    tr = _row_block(r, cn)
    c1 = 1.0 - ADAM_B1 ** ADAM_STEP
    c2 = 1.0 - ADAM_B2 ** ADAM_STEP

    def body(w_ref, g_ref, m_ref, v_ref, d_ref, mo_ref, vo_ref):
        gv = g_ref[...]
        mn = ADAM_B1 * m_ref[...] + (1.0 - ADAM_B1) * gv
        vn = ADAM_B2 * v_ref[...] + (1.0 - ADAM_B2) * (gv * gv)
        d_ref[...] = -ADAM_LR * ((mn / c1) / (jnp.sqrt(vn / c2) + ADAM_EPS) + ADAM_WD * w_ref[...])
        mo_ref[...] = mn
        vo_ref[...] = vn

    blk = pl.BlockSpec((tr, cn), lambda i: (i, 0))
    o = _sds((r, cn))
    outs = _pcall(body, name=name, grid=(r // tr,), in_specs=[blk] * 4, out_specs=[blk] * 3, out_shape=[o, o, o],
                  compiler_params=_cp(("parallel",)))(*[a.reshape(r, cn) for a in (w, g, m, v)])
    return tuple(a.reshape(shape) for a in outs)


def _local_step(xs, target, modt, nw, fnw, get_w, get_ffn, put_g, ev, od, lc):
    t, dm = xs.shape
    nct = lc // ROW_TILE
    depth = nw.shape[0]
    cs, sn = _rope_tables(t, lc)
    saved = []
    x_in, x1p, fp = xs, None, None
    for i in range(depth):
        j, even = i // 2, i % 2 == 0
        tag = f"l{i}"
        w, deps = get_w(i, [fp] if i else [])
        if i == 0:
            _, h = _rnm(tag + "_norm1", x_in, None, None, 0, modt[0], 0, 1, nw[0, 0], nct, deps)
        else:
            x_in, h = _rnm(tag + "_norm1", x1p, fp, modt[i - 1], 5, modt[i], 0, 1, nw[i, 0], nct, deps)
        s = dict(x=x_in, h=h, w=w)
        if even:
            p = _mm_cols(tag + "_in", h, w["in"])
            q, k, v = _even_qkv(tag + "_qkv", p, cs, sn)
            of, ob, ss = _retention_fwd(tag + "_ret", q, k, v, ev["lgb"][j], lc)
            mix, yc = _even_mix(tag + "_mix", p, of, ob, ev["cw"][j], ev["lnw"][j], ev["lnb"][j], nct)
            y = _mm_full(tag + "_out", mix, w["out"], NN)
            s.update(p=p, q=q, k=k, v=v, of=of, ob=ob, ss=ss, yc=yc)
        else:
            p = _mm_cols(tag + "_in", h, w["in"])
            mix, m = _odd_mix(tag + "_mix", p, od["pw"][j], od["ps"][j], od["lnw"][j], od["lnb"][j],
                              od["sgw"][j], od["sgb"][j], nct, lc)
            y = _mm_full(tag + "_out", mix, w["out"], NN)
            s.update(p=p, m=m)
        x1, h2 = _rnm(tag + "_norm2", x_in, y, modt[i], 2, modt[i], 3, 4, nw[i, 1], nct)
        w.update(get_ffn(i, [y]))
        a, gt, up = _ffn_up(tag + "_ffn_up", h2, w["gate"], w["up"])
        f = _mm_full(tag + "_ffn_down", a, w["down"], NN)
        s.update(mix=mix, y=y, x1=x1, h2=h2, a=a, gt=gt, up=up, f=f)
        saved.append(s)
        x1p, fp = x1, f

    loss_blk, dx, df, fin_s = _fin("final", x1p, fp, modt[depth - 1], 5, fnw, target, nct)

    deps = []
    dmod = [[None] * 6 for _ in range(depth)]
    dnw = [[None, None] for _ in range(depth)]
    zero2 = jnp.zeros((2, dm), F32)
    dmod[depth - 1][5] = jnp.stack([zero2[0], fin_s[0]])
    small = dict(dfnw=fin_s[1], ev=[], od=[])
    for i in reversed(range(depth)):
        j, even = i // 2, i % 2 == 0
        tag = f"l{i}b"
        s = saved[i]
        w = s["w"]
        fh = w["down"].shape[0] // 2
        g = {}
        dgt, dup = _ffn_down_bwd(tag + "_ffn_down", df, w["down"], s["gt"], s["up"])
        g["down"] = _wgrad_rows(tag + "_gdown", s["a"], fh, df)
        g["gate"] = _wgrad_rows(tag + "_ggate", dgt, fh, s["h2"])
        g["up"] = _wgrad_rows(tag + "_gup", dup, fh, s["h2"])
        deps = put_g(i, "f", g)
        dh2 = _ffn_in_bwd(tag + "_ffn_in", dgt, dup, w["gate"], w["up"])
        dx1, dy, s2 = _bnm(tag + "_norm2", s["x1"], dh2, dx, s["y"], modt[i], 3, 4, modt[i], 2, nw[i, 1], nct, deps)
        dmod[i][3], dmod[i][4], dmod[i][2] = s2[:, 0], s2[:, 1], s2[:, 2]
        dnw[i][1] = s2[1, 3]
        dmix = _mm_full(tag + "_out", dy, w["out"], NT)
        g["out"] = _wgrad_rows(tag + "_gout", s["mix"], w["out"].shape[0] // 2, dy)
        if even:
            do, dg, dyc, lns = _even_mix_bwd1(tag + "_mix1", dmix, s["p"], s["of"], s["ob"], s["yc"],
                                              ev["lnw"][j], ev["lnb"][j])
            da, dgb, dcw = _even_conv_bwd(tag + "_conv", dyc, s["p"], ev["cw"][j], nct)
            dqf, dkf, dvf, dqb, dkb, dvb, dl = _retention_bwd(tag + "_ret", s["q"], s["k"], s["v"], do, s["ss"],
                                                              ev["lgb"][j], lc)
            dp = _even_dp(tag + "_dp", (dqf, dqb), (dkf, dkb), (dvf, dvb), dg, da, dgb, cs, sn)
            small["ev"].append(dict(lnw=lns[0], lnb=lns[1], cw=dcw, dl=dl[:, 0]))
        else:
            dm_, dpd, vec, dpw, dsgw, dsgb = _odd_mix_bwd1(tag + "_mix1", dmix, s["p"], s["m"], od["pw"][j], od["ps"][j],
                                                           od["lnw"][j], od["lnb"][j], od["sgw"][j], od["sgb"][j])
            dp = _odd_dp(tag + "_dp", dm_, dpd, nct, lc)
            small["od"].append(dict(ps=vec[0], lnw=vec[1], lnb=vec[2], pw=dpw, sgw=dsgw, sgb=dsgb[:, :, 0]))
        dh = _mm_cols_bwd(tag + "_in", dp, w["in"])
        g["in"] = _wgrad_cols(tag + "_gin", s["h"], dp, w["in"].shape[0])
        deps = put_g(i, "m", g)
        if i > 0:
            dx, df, s1 = _bnm(tag + "_norm1", s["x"], dh, dx1, saved[i - 1]["f"], modt[i], 0, 1, modt[i - 1], 5,
                              nw[i, 0], nct, deps)
            dmod[i - 1][5] = s1[:, 2]
        else:
            dx, _, s1 = _bnm(tag + "_norm1", s["x"], dh, dx1, None, modt[0], 0, 1, None, 0, nw[0, 0], nct, deps)
        dmod[i][0], dmod[i][1] = s1[:, 0], s1[:, 1]
        dnw[i][0] = s1[1, 3]
    small["ev"].reverse()
    small["od"].reverse()
    dmod_t = jnp.stack([jnp.concatenate([jnp.stack(rows, axis=1), jnp.zeros((2, 2, dm), F32)], axis=1) for rows in dmod])
    small["dmod"] = dmod_t
    small["dnw"] = jnp.stack([jnp.stack(r) for r in dnw])
    return loss_blk, dx, small


WEIGHTS = ["c_ctx", "ada_w", "ada_b", "norm_w", "even_w_in", "even_w_out", "ret_decay_logit", "conv_dw_w",
           "conv_ln_w", "conv_ln_b", "odd_w_in", "odd_w_out", "pool_w", "pool_scale", "sg_ln_w", "sg_ln_b",
           "sg_w", "sg_b", "ffn_w_gate", "ffn_w_up", "ffn_w_down", "final_norm_w"]
BIG = dict(even_in="even_w_in", even_out="even_w_out", odd_in="odd_w_in", odd_out="odd_w_out",
           gate="ffn_w_gate", up="ffn_w_up", down="ffn_w_down")


def _rows(a, width=1024):
    flat = a.reshape(-1)
    n = flat.shape[0]
    per = 8 * width
    tot = -(-n // per) * per
    return jnp.pad(flat, (0, tot - n)).reshape(tot // width, width)


def _unshard(parts, lead):
    nl = len(lead)
    perm = tuple(range(1, nl + 1)) + (0, nl + 1)
    return parts.transpose(perm).reshape(tuple(lead) + (4 * parts.shape[-1],))


def _my_cols(a, chip, n):
    start = (0,) * (a.ndim - 1) + (chip * n,)
    return lax.dynamic_slice(a, start, a.shape[:-1] + (n,))


def kernel(x, c, ctx, c_ctx, ada_w, ada_b, norm_w, even_w_in, even_w_out, ret_decay_logit, conv_dw_w, conv_ln_w, conv_ln_b, odd_w_in, odd_w_out, pool_w, pool_scale, sg_ln_w, sg_ln_b, sg_w, sg_b, ffn_w_gate, ffn_w_up, ffn_w_down, final_norm_w, loss_target, m_c_ctx, m_ada_w, m_ada_b, m_norm_w, m_even_w_in, m_even_w_out, m_ret_decay_logit, m_conv_dw_w, m_conv_ln_w, m_conv_ln_b, m_odd_w_in, m_odd_w_out, m_pool_w, m_pool_scale, m_sg_ln_w, m_sg_ln_b, m_sg_w, m_sg_b, m_ffn_w_gate, m_ffn_w_up, m_ffn_w_down, m_final_norm_w, v_c_ctx, v_ada_w, v_ada_b, v_norm_w, v_even_w_in, v_even_w_out, v_ret_decay_logit, v_conv_dw_w, v_conv_ln_w, v_conv_ln_b, v_odd_w_in, v_odd_w_out, v_pool_w, v_pool_scale, v_sg_ln_w, v_sg_ln_b, v_sg_w, v_sg_b, v_ffn_w_gate, v_ffn_w_up, v_ffn_w_down, v_final_norm_w):
    wv = dict(c_ctx=c_ctx, ada_w=ada_w, ada_b=ada_b, norm_w=norm_w, even_w_in=even_w_in, even_w_out=even_w_out,
              ret_decay_logit=ret_decay_logit, conv_dw_w=conv_dw_w, conv_ln_w=conv_ln_w, conv_ln_b=conv_ln_b,
              odd_w_in=odd_w_in, odd_w_out=odd_w_out, pool_w=pool_w, pool_scale=pool_scale, sg_ln_w=sg_ln_w,
              sg_ln_b=sg_ln_b, sg_w=sg_w, sg_b=sg_b, ffn_w_gate=ffn_w_gate, ffn_w_up=ffn_w_up,
              ffn_w_down=ffn_w_down, final_norm_w=final_norm_w)
    mv = dict(zip(WEIGHTS, (m_c_ctx, m_ada_w, m_ada_b, m_norm_w, m_even_w_in, m_even_w_out, m_ret_decay_logit,
                            m_conv_dw_w, m_conv_ln_w, m_conv_ln_b, m_odd_w_in, m_odd_w_out, m_pool_w, m_pool_scale,
                            m_sg_ln_w, m_sg_ln_b, m_sg_w, m_sg_b, m_ffn_w_gate, m_ffn_w_up, m_ffn_w_down,
                            m_final_norm_w)))
    vv = dict(zip(WEIGHTS, (v_c_ctx, v_ada_w, v_ada_b, v_norm_w, v_even_w_in, v_even_w_out, v_ret_decay_logit,
                            v_conv_dw_w, v_conv_ln_w, v_conv_ln_b, v_odd_w_in, v_odd_w_out, v_pool_w, v_pool_scale,
                            v_sg_ln_w, v_sg_ln_b, v_sg_w, v_sg_b, v_ffn_w_gate, v_ffn_w_up, v_ffn_w_down,
                            v_final_norm_w)))
    xi, yi, ci = lax.axis_index("x"), lax.axis_index("y"), lax.axis_index("c")
    chip = 2 * xi + yi
    dev = 4 * xi + 2 * yi + ci
    dm = x.shape[-1]
    lc = ctx.shape[1]
    depth = ada_w.shape[0]
    n_ada = ada_w.shape[-1]

    cw_pad = jnp.pad(conv_dw_w, ((0, 0), (0, 1), (0, 0)))
    vec3 = jnp.stack([pool_scale, sg_ln_w, sg_ln_b])
    pack1 = jnp.concatenate([_rows(c), _rows(norm_w), _rows(cw_pad), _rows(vec3)], axis=0)
    g1 = _all_gather8("gather_small", pack1).reshape(8, 32, dm)
    c_all = g1[:, 0]
    per_chip = g1[0::2]
    norm_full = _unshard(per_chip[:, 8:10].reshape(4, depth, 2, dm // 4), (depth, 2))
    cw_full = _unshard(per_chip[:, 16:24].reshape(4, 2, CONV_K + 1, 128), (2, CONV_K + 1))
    vec_full = _unshard(per_chip[:, 24, :768].reshape(4, 3, 2, 128), (3, 2))

    c16 = jnp.concatenate([c_all, c_ctx[None, :], jnp.zeros((7, dm), F32)], axis=0)
    mod_sh = _ada_mod("ada_mod", c16, ada_w, _my_cols(ada_b, chip, n_ada)[:, None, :])
    g2 = _all_gather8("gather_mod", mod_sh.reshape(depth * 16, n_ada)).reshape(8, depth, 16, n_ada)
    mod_full = _unshard(g2[0::2], (depth, 16))
    mod_x = lax.dynamic_index_in_dim(mod_full, dev, axis=1, keepdims=False).reshape(depth, 6, dm)
    mod_c = mod_full[:, 8].reshape(depth, 6, dm)
    modt = jnp.pad(jnp.stack([mod_c, mod_x], axis=1), ((0, 0), (0, 0), (0, 2), (0, 0)))

    names = list(BIG)
    tr_names = ("gate", "up")
    shard = {k: (jnp.swapaxes(wv[BIG[k]], 1, 2) if k in tr_names else wv[BIG[k]]).astype(BF16) for k in names}
    roles = ("in", "out", "gate", "up", "down")

    def layer_keys(i):
        mixer = ("even_in", "even_out") if i % 2 == 0 else ("odd_in", "odd_out")
        return [(k, i // 2) for k in mixer] + [(k, i) for k in ("gate", "up", "down")]

    def as_used(got):
        return {r: (g if r == "in" else g.reshape(4 * g.shape[1], g.shape[2])) for r, g in zip(roles, got)}

    started = {}

    def start_gather(tag, keys, before):
        srcs = [shard[k][l] for k, l in keys]
        lands = [lax.empty((4,) + s.shape, s.dtype) for s in srcs]
        return _split_start(f"gather_start{tag}", srcs, lands, _gather_copies(len(srcs)), before)

    def get_w(i, after):
        if i > 0:
            got = _split_wait(f"gather_wait{i}m", started[i, "m"], _gather_copies(2), after)[1]
            return as_used(got), []
        got = _gather_weights("gather_w0", [shard[k][l] for k, l in layer_keys(0)[:2]], [modt])
        last = None
        for li in range(depth):
            for part, keys in (("m", layer_keys(li)[:2]), ("f", layer_keys(li)[2:])):
                if (li, part) != (0, "m"):
                    started[li, part] = last = start_gather(f"{li}{part}", keys, [got[0]] + ([last[4]] if last else []))
        return as_used(got), [last[4]]

    def get_ffn(i, after):
        got = _split_wait(f"gather_wait{i}f", started[i, "f"], _gather_copies(3), after)[1]
        return {r: g.reshape(4 * g.shape[1], g.shape[2]) for r, g in zip(roles[2:], got)}

    idx = jnp.stack([ci, chip]).astype(jnp.int32)
    pairs, pending, stages = {}, {}, []

    def stage_keys(stage):
        i, part = stage
        return layer_keys(i)[2:] if part == "f" else layer_keys(i)[:2]

    def finish_pair(stage, after):
        tag = f"{stage[0]}{stage[1]}"
        n = len(stage_keys(stage))
        glist, from_sib = _split_wait(f"pair_wait{tag}", pairs[stage], _pair_copies(n), after)
        pair = [_add_half(f"rs_add{tag}_{t}", gl, a, idx) for t, (gl, a) in enumerate(zip(glist, from_sib))]
        lands = [lax.empty((3,) + p.shape[1:], p.dtype) for p in pair]
        st = _split_start(f"rs_start{tag}", pair, lands, _scatter_copies(n), [])
        pending[stage] = (glist, from_sib, st)
        return [st[4]]

    def put_g(i, part, g):
        stage = (i, part)
        glist = [g[r].reshape(4, -1, g[r].shape[-1]) for r in (roles[2:] if part == "f" else roles[:2])]
        lands = [lax.empty((4, gl.shape[1] // 2, gl.shape[2]), gl.dtype) for gl in glist]
        pairs[stage] = _split_start(f"pair_start{i}{part}", glist, lands, _pair_copies(len(glist)), [])
        tokens = [pairs[stage][4]]
        if stages:
            tokens += finish_pair(stages[-1], [pairs[stage][4]])
        stages.append(stage)
        return tokens

    ev = dict(lgb=jnp.broadcast_to(ret_decay_logit.reshape(-1, 2 * HEADS)[:, :, None], (ret_decay_logit.shape[0], 2 * HEADS, HEAD_DIM)),
              cw=cw_full, lnw=conv_ln_w[:, None, :], lnb=conv_ln_b[:, None, :])
    od = dict(pw=pool_w, ps=vec_full[0][:, None, :], lnw=vec_full[1][:, None, :], lnb=vec_full[2][:, None, :],
              sgw=sg_w, sgb=jnp.broadcast_to(sg_b[:, :, :, None], sg_b.shape + (GC,)))
    xs = jnp.concatenate([ctx[0], x[0]], axis=0)
    loss_blk, dxs, small = _local_step(xs, loss_target[0], modt, norm_full[:, :, None, :], final_norm_w[None, :],
                                       get_w, get_ffn, put_g, ev, od, lc)

    misc = jnp.stack([
        small["dfnw"], jnp.broadcast_to(loss_blk[0, 0], (dm,)),
        jnp.concatenate([e["lnw"] for e in small["ev"]]), jnp.concatenate([e["lnb"] for e in small["ev"]]),
        jnp.concatenate([o["ps"] for o in small["od"]]), jnp.concatenate([o["lnw"] for o in small["od"]]),
        jnp.concatenate([o["lnb"] for o in small["od"]]),
        jnp.pad(jnp.concatenate([e["dl"] for e in small["ev"]]), (0, dm - 4 * HEADS)),
        jnp.stack([o["sgb"] for o in small["od"]]).reshape(-1)])
    pack2 = jnp.concatenate([
        _rows(small["dmod"]), _rows(small["dnw"]), _rows(misc), _rows(jnp.stack([e["cw"] for e in small["ev"]])),
        _rows(jnp.stack([o["pw"] for o in small["od"]])), _rows(jnp.stack([o["sgw"] for o in small["od"]]))], axis=0)
    n2 = pack2.shape[0]
    g3 = _all_gather8("gather_grads", pack2)
    tot = _sum8("sum_grads", g3.reshape(8, n2, dm))
    r_mod = depth * 16
    o_nw, o_misc = r_mod, r_mod + 8
    o_cw = o_misc + 16
    o_pw = o_cw + 2 * (CONV_K + 1) // 2
    o_sgw = o_pw + 128
    dmod_sum = tot[:r_mod].reshape(depth, 2, 8, dm)
    dmod_dev = g3.reshape(8, n2, dm)[:, :r_mod].reshape(8, depth, 2, 8, dm)
    dm_x = dmod_dev[:, :, 1, :6].reshape(8, depth, 6 * dm).transpose(1, 0, 2)
    dm_c = dmod_sum[:, 0, :6].reshape(depth, 1, 6 * dm)
    dmod16 = jnp.concatenate([dm_x, dm_c, jnp.zeros((depth, 7, 6 * dm), F32)], axis=1)
    g_ada_b = _rowsum16("ada_b_grad", dmod16)[:, 0]
    g_ada_w, dc16 = _ada_bwd("ada_bwd", c16, _my_cols(dmod16, chip, n_ada), ada_w)
    g4 = _all_gather8("gather_cctx", dc16[8:16]).reshape(8, 8, dm)
    g_c_ctx = _cctx_grad("cctx_grad", g4[0::2, 0], c_ctx[None, :])[0]

    misc_t = tot[o_misc:o_misc + 16]
    half = lambda row: misc_t[row].reshape(2, dm // 2)
    grads = dict(
        c_ctx=g_c_ctx, ada_w=g_ada_w, ada_b=g_ada_b,
        norm_w=_my_cols(tot[o_nw:o_nw + 8].reshape(depth, 2, dm), chip, dm // 4),
        ret_decay_logit=misc_t[7, :4 * HEADS].reshape(ret_decay_logit.shape),
        conv_dw_w=_my_cols(tot[o_cw:o_cw + 2 * (CONV_K + 1) // 2].reshape(2, CONV_K + 1, dm // 2)[:, :CONV_K], chip, 128),
        conv_ln_w=half(2), conv_ln_b=half(3),
        pool_w=tot[o_pw:o_pw + 128].reshape(pool_w.shape),
        pool_scale=_my_cols(half(4), chip, 128), sg_ln_w=_my_cols(half(5), chip, 128), sg_ln_b=_my_cols(half(6), chip, 128),
        sg_w=tot[o_sgw:o_sgw + 128].reshape(sg_w.shape), sg_b=misc_t[8].reshape(sg_b.shape),
        final_norm_w=misc_t[0])
    loss = misc_t[1, 0]

    last_tokens = finish_pair(stages[-1], [g_c_ctx])
    deltas, new_m, new_v = {}, {}, {}
    for n in WEIGHTS:
        if n not in BIG.values():
            deltas[n], new_m[n], new_v[n] = _adamw("adamw_" + n, wv[n], grads[n], mv[n], vv[n])
    reduced = {k: None for k in names}
    for stage in stages:
        glist, from_sib, st = pending[stage]
        last = stage == stages[-1]
        after = [deltas["ada_w"]] + [reduced[k] for k, _ in stage_keys(stages[-2])] if last else last_tokens
        slots = _split_wait(f"rs_wait{stage[0]}{stage[1]}", st, _scatter_copies(len(glist)), after)[1]
        for (k, l), g, a, b in zip(stage_keys(stage), glist, from_sib, slots):
            reduced[k] = _sum_final(f"rs_sum_{k}{l}", g, a, b, idx, reduced[k], l, shard[k].shape[0])
    shards = dict(zip(names, _rs_share("rs_share", [reduced[k] for k in names])))

    for k in names:
        n = BIG[k]
        if k in tr_names:
            tr = lambda a: jnp.swapaxes(a, 1, 2)
            outs = _adamw("adamw_" + n, tr(wv[n]), shards[k], tr(mv[n]), tr(vv[n]))
            grads[n] = tr(shards[k])
            deltas[n], new_m[n], new_v[n] = (tr(o) for o in outs)
        else:
            grads[n] = shards[k]
            deltas[n], new_m[n], new_v[n] = _adamw("adamw_" + n, wv[n], grads[n], mv[n], vv[n])
    grad_x = dxs[lc:][None]
    return (loss, grad_x, *[grads[n] for n in WEIGHTS], *[deltas[n] for n in WEIGHTS],
            *[new_m[n] for n in WEIGHTS], *[new_v[n] for n in WEIGHTS])
```

```python
import functools
import math

import jax
import jax.numpy as jnp
from jax import lax
from jax.experimental import pallas as pl
from jax.experimental.pallas import tpu as pltpu

F32 = jnp.float32
BF16 = jnp.bfloat16
MESH = pl.DeviceIdType.MESH

EPS = 1e-6
GRID_W = 64
HEADS = 4
HEAD_DIM = 128
CHUNK = 128
CONV_K = 31
ROPE_BASE = 10000.0
ROPE_PAIRS = (16, 24, 24)
POOL_WINDOWS = (2, 4, 8, 16)
ADAM_LR, ADAM_B1, ADAM_B2, ADAM_EPS, ADAM_WD, ADAM_STEP = 0.001, 0.9, 0.999, 1e-08, 0.01, 10

ROW_TILE = 256
CONV_HALO = 16
POOL_HALO = 8
VMEM_LIMIT = 56 * 1024 * 1024
WGRAD_ROWS = 2304


def _pcall(body, **kw):
    return pl.pallas_call(body, **kw)


def _cp(sem=None, vmem=VMEM_LIMIT):
    if sem is None:
        return pltpu.CompilerParams(vmem_limit_bytes=vmem)
    return pltpu.CompilerParams(dimension_semantics=sem, vmem_limit_bytes=vmem)


def _sds(shape, dtype=F32):
    return jax.ShapeDtypeStruct(tuple(shape), dtype)


def _full(shape):
    nd = len(shape)
    return pl.BlockSpec(tuple(shape), lambda *_: (0,) * nd)


def _sigmoid(x):
    return jax.nn.sigmoid(x)


def _silu(x):
    return x * _sigmoid(x)


def _dsilu(x):
    s = _sigmoid(x)
    return s * (1.0 + x * (1.0 - s))


def _colsum(a):
    return jnp.sum(a, axis=0, keepdims=True)


def _dot(a, b, dn):
    return lax.dot_general(a.astype(BF16), b.astype(BF16), dn, preferred_element_type=F32)


NN = (((1,), (0,)), ((), ()))
NT = (((1,), (1,)), ((), ()))
TN = (((0,), (0,)), ((), ()))


def _mm_tile(t, cap=1152):
    best = 16
    for d in range(16, min(t, cap) + 1, 16):
        if t % d == 0:
            best = d
    return best


def _mm(name, pairs, grid, out_shape, out_spec, dn):
    npairs = len(pairs)
    nk = grid[-1]
    kax = len(grid) - 1
    assert nk == 1 or out_shape.dtype == F32

    def body(*refs):
        ins = refs[:2 * npairs]
        o_ref = refs[2 * npairs]
        tot = None
        for p in range(npairs):
            d = _dot(ins[2 * p][...], ins[2 * p + 1][...], dn)
            tot = d if tot is None else tot + d
        if nk == 1:
            o_ref[...] = tot.astype(o_ref.dtype)
        else:
            k = pl.program_id(kax)

            @pl.when(k == 0)
            def _():
                o_ref[...] = tot

            @pl.when(k != 0)
            def _():
                o_ref[...] += tot

    args, in_specs = [], []
    for a, a_spec, b, b_spec in pairs:
        args += [a, b]
        in_specs += [a_spec, b_spec]
    sem = ("parallel",) * kax + ("arbitrary",)
    return _pcall(body, name=name, grid=grid, in_specs=in_specs, out_specs=out_spec, out_shape=out_shape,
                  compiler_params=_cp(sem))(*args)


def _mm_cols(name, a, w, out_dtype=F32):
    t, k = a.shape
    j, _, n = w.shape
    tm = _mm_tile(t)
    return _mm(name, [(a, pl.BlockSpec((tm, k), lambda i, jj, kk: (i, 0)),
                       w, pl.BlockSpec((None, k, n), lambda i, jj, kk: (jj, 0, 0)))],
               (t // tm, j, 1), _sds((t, j * n), out_dtype), pl.BlockSpec((tm, n), lambda i, jj, kk: (i, jj)), NN)


def _mm_cols_bwd(name, d, w):
    t = d.shape[0]
    j, k, n = w.shape
    tm = _mm_tile(t)
    pairs = [(d, pl.BlockSpec((tm, n), functools.partial(lambda jj, i, u, kk: (i, jj), jj)),
              w, pl.BlockSpec((None, k, n), functools.partial(lambda jj, i, u, kk: (jj, 0, 0), jj))) for jj in range(j)]
    return _mm(name, pairs, (t // tm, 1, 1), _sds((t, k), BF16), pl.BlockSpec((tm, k), lambda i, u, kk: (i, 0)), NT)


def _mm_full(name, a, w, dn, tm=None):
    t, k = a.shape
    n = w.shape[1] if dn is NN else w.shape[0]
    tm = tm or _mm_tile(t)
    return _mm(name, [(a, pl.BlockSpec((tm, k), lambda i, u, kk: (i, 0)), w, _full(w.shape))],
               (t // tm, 1, 1), _sds((t, n)), pl.BlockSpec((tm, n), lambda i, u, kk: (i, 0)), dn)


def _wgrad_cols(name, a, b, j):
    t, k = a.shape
    n = b.shape[1] // j
    tt = _mm_tile(t, 2 * WGRAD_ROWS)
    return _mm(name, [(a, pl.BlockSpec((tt, k), lambda jj, u, kk: (kk, 0)),
                       b, pl.BlockSpec((tt, n), lambda jj, u, kk: (kk, jj)))],
               (j, 1, t // tt), _sds((j, k, n)), pl.BlockSpec((None, k, n), lambda jj, u, kk: (jj, 0, 0)), TN)


def _wgrad_rows(name, a, blk, b):
    t, f = a.shape
    n = b.shape[1]
    tt = _mm_tile(t, WGRAD_ROWS)
    return _mm(name, [(a, pl.BlockSpec((tt, blk), lambda jj, u, kk: (kk, jj)),
                       b, pl.BlockSpec((tt, n), lambda jj, u, kk: (kk, 0)))],
               (f // blk, 1, t // tt), _sds((f, n)), pl.BlockSpec((blk, n), lambda jj, u, kk: (jj, 0)), TN)


def _ffn_tiles(t, f):
    return _mm_tile(t, 288), f


def _ffn_up(name, h, wgt, wut):
    t, k = h.shape
    f = wgt.shape[0]
    tm, tn = _ffn_tiles(t, f)

    def body(h_ref, wg_ref, wu_ref, a_ref, gt_ref, up_ref):
        hv = h_ref[...]
        gt = _dot(hv, wg_ref[...], NT)
        up = _dot(hv, wu_ref[...], NT)
        a_ref[...] = (_silu(gt) * up).astype(BF16)
        gt_ref[...] = gt.astype(BF16)
        up_ref[...] = up.astype(BF16)

    wspec = pl.BlockSpec((tn, k), lambda i, jj: (jj, 0))
    ospec = pl.BlockSpec((tm, tn), lambda i, jj: (i, jj))
    o = _sds((t, f), BF16)
    return _pcall(body, name=name, grid=(t // tm, f // tn),
                  in_specs=[pl.BlockSpec((tm, k), lambda i, jj: (i, 0)), wspec, wspec],
                  out_specs=[ospec, ospec, ospec], out_shape=[o, o, o],
                  compiler_params=_cp(("parallel", "parallel")))(h, wgt, wut)


def _ffn_down_bwd(name, df, wd, gt, up):
    t, dm = df.shape
    f = wd.shape[0]
    tm, tn = _ffn_tiles(t, f)

    def body(df_ref, wd_ref, gt_ref, up_ref, dgt_ref, dup_ref):
        da = _dot(df_ref[...], wd_ref[...], NT)
        g = gt_ref[...].astype(F32)
        u = up_ref[...].astype(F32)
        s = _sigmoid(g)
        dgt_ref[...] = (da * u * (s * (1.0 + g * (1.0 - s)))).astype(BF16)
        dup_ref[...] = (da * (g * s)).astype(BF16)

    aspec = pl.BlockSpec((tm, tn), lambda i, jj: (i, jj))
    o = _sds((t, f), BF16)
    return _pcall(body, name=name, grid=(t // tm, f // tn),
                  in_specs=[pl.BlockSpec((tm, dm), lambda i, jj: (i, 0)),
                            pl.BlockSpec((tn, dm), lambda i, jj: (jj, 0)), aspec, aspec],
                  out_specs=[aspec, aspec], out_shape=[o, o],
                  compiler_params=_cp(("parallel", "parallel")))(df, wd, gt, up)


def _ffn_in_bwd(name, dgt, dup, wgt, wut):
    t, f = dgt.shape
    k = wgt.shape[1]
    tm = _mm_tile(t, 576)
    aspec = pl.BlockSpec((tm, f), lambda i, u, kk: (i, 0))
    wspec = pl.BlockSpec((f, k), lambda i, u, kk: (0, 0))
    return _mm(name, [(dgt, aspec, wgt, wspec), (dup, aspec, wut, wspec)], (t // tm, 1, 1), _sds((t, k), BF16),
               pl.BlockSpec((tm, k), lambda i, u, kk: (i, 0)), NN)


def _modrow(ref, row, is_ctx):
    return jnp.where(is_ctx, ref[0, row:row + 1, :], ref[1, row:row + 1, :])


def _rnm(name, x, delta, mod_g, g_row, mod_n, sh_row, sc_row, nw, nct, deps=()):
    t, dm = x.shape
    tm = ROW_TILE
    has = delta is not None
    nd = len(deps)

    def body(*refs):
        refs = refs[:len(refs) - nd - (2 if has else 1)] + refs[len(refs) - (2 if has else 1):]
        if has:
            x_ref, d_ref, mg_ref, m_ref, nw_ref, xo_ref, h_ref = refs
        else:
            x_ref, m_ref, nw_ref, h_ref = refs
        is_ctx = pl.program_id(0) < nct
        xv = x_ref[...]
        if has:
            xv = xv + _modrow(mg_ref, g_row, is_ctx) * d_ref[...]
            xo_ref[...] = xv
        r = lax.rsqrt(jnp.mean(xv * xv, axis=-1, keepdims=True) + EPS)
        hv = (xv * r * nw_ref[...]) * (1.0 + _modrow(m_ref, sc_row, is_ctx)) + _modrow(m_ref, sh_row, is_ctx)
        h_ref[...] = hv.astype(BF16)

    row = pl.BlockSpec((tm, dm), lambda i: (i, 0))
    ins = [x] + ([delta, mod_g] if has else []) + [mod_n, nw] + list(deps)
    in_specs = ([row] + ([row, _full(mod_g.shape)] if has else []) + [_full(mod_n.shape), _full(nw.shape)]
                + [_full(d.shape) for d in deps])
    outs = ([_sds((t, dm))] if has else []) + [_sds((t, dm), BF16)]
    out_specs = ([row] if has else []) + [row]
    res = _pcall(body, name=name, grid=(t // tm,), in_specs=in_specs, out_specs=out_specs, out_shape=outs,
                 compiler_params=_cp(("parallel",)))(*ins)
    return res if has else (None, res[0])


def _bnm(name, xn, dh, dup, yprev, mod_n, sh_row, sc_row, mod_g, g_row, nw, nct, deps=()):
    t, dm = xn.shape
    tm = ROW_TILE
    has = yprev is not None
    nd = len(deps)

    def body(*refs):
        nout = 3 if has else 2
        refs = refs[:len(refs) - nd - nout] + refs[len(refs) - nout:]
        if has:
            x_ref, dh_ref, du_ref, y_ref, mn_ref, mg_ref, nw_ref, dx_ref, dd_ref, s_ref = refs
        else:
            x_ref, dh_ref, du_ref, mn_ref, nw_ref, dx_ref, s_ref = refs
        i = pl.program_id(0)
        is_ctx = i < nct

        @pl.when(i == 0)
        def _():
            s_ref[...] = jnp.zeros_like(s_ref)

        xv = x_ref[...]
        r = lax.rsqrt(jnp.mean(xv * xv, axis=-1, keepdims=True) + EPS)
        xh = xv * r
        w = nw_ref[...]
        sc1 = 1.0 + _modrow(mn_ref, sc_row, is_ctx)
        dhv = dh_ref[...].astype(F32)
        dxh = dhv * sc1 * w
        dx = r * (dxh - xh * jnp.mean(dxh * xh, axis=-1, keepdims=True)) + du_ref[...]
        dx_ref[...] = dx
        parts = [_colsum(dhv), _colsum(dhv * (xh * w))]
        if has:
            dd_ref[...] = (_modrow(mg_ref, g_row, is_ctx) * dx).astype(BF16)
            parts.append(_colsum(dx * y_ref[...]))
        else:
            parts.append(jnp.zeros((1, dm), F32))
        upd = jnp.concatenate(parts + [jnp.zeros((5, dm), F32)], axis=0)
        dnw = jnp.concatenate([jnp.zeros((3, dm), F32), _colsum(dhv * sc1 * xh), jnp.zeros((4, dm), F32)], axis=0)

        @pl.when(is_ctx)
        def _():
            s_ref[0] += upd
            s_ref[1] += dnw

        @pl.when(jnp.logical_not(is_ctx))
        def _():
            s_ref[1] += upd + dnw

    row = pl.BlockSpec((tm, dm), lambda i: (i, 0))
    ins = [xn, dh, dup] + ([yprev] if has else []) + [mod_n] + ([mod_g] if has else []) + [nw] + list(deps)
    in_specs = ([row, row, row] + ([row] if has else []) + [_full(mod_n.shape)]
                + ([_full(mod_g.shape)] if has else []) + [_full(nw.shape)] + [_full(d.shape) for d in deps])
    outs = [_sds((t, dm))] + ([_sds((t, dm), BF16)] if has else []) + [_sds((2, 8, dm))]
    out_specs = [row] + ([row] if has else []) + [_full((2, 8, dm))]
    res = _pcall(body, name=name, grid=(t // tm,), in_specs=in_specs, out_specs=out_specs, out_shape=outs,
                 compiler_params=_cp(("arbitrary",)))(*ins)
    return res if has else (res[0], None, res[1])


def _fin(name, x1, f, mod, g_row, fw, target, nct):
    t, dm = x1.shape
    tm = ROW_TILE

    def body(x_ref, f_ref, m_ref, fw_ref, t_ref, loss_ref, dx_ref, dd_ref, s_ref):
        i = pl.program_id(0)

        @pl.when(i == 0)
        def _():
            s_ref[...] = jnp.zeros_like(s_ref)
            loss_ref[...] = jnp.zeros_like(loss_ref)

        @pl.when(i < nct)
        def _():
            dx_ref[...] = jnp.zeros_like(dx_ref)
            dd_ref[...] = jnp.zeros_like(dd_ref)

        @pl.when(i >= nct)
        def _():
            g = m_ref[1, g_row:g_row + 1, :]
            fv = f_ref[...]
            xv = x_ref[...] + g * fv
            r = lax.rsqrt(jnp.mean(xv * xv, axis=-1, keepdims=True) + EPS)
            xh = xv * r
            w = fw_ref[...]
            err = xh * w - t_ref[...]
            loss_ref[...] += 0.5 * jnp.sum(err * err) / dm
            dout = err * (1.0 / dm)
            dxh = dout * w
            dx = r * (dxh - xh * jnp.mean(dxh * xh, axis=-1, keepdims=True))
            dx_ref[...] = dx
            dd_ref[...] = (g * dx).astype(BF16)
            s_ref[...] += jnp.concatenate([_colsum(dx * fv), _colsum(dout * xh), jnp.zeros((6, dm), F32)], axis=0)

    row = pl.BlockSpec((tm, dm), lambda i: (i, 0))
    trow = pl.BlockSpec((tm, dm), lambda i: (jnp.maximum(i - nct, 0), 0))
    return _pcall(body, name=name, grid=(t // tm,),
                  in_specs=[row, row, _full(mod.shape), _full(fw.shape), trow],
                  out_specs=[_full((8, 128)), row, row, _full((8, dm))],
                  out_shape=[_sds((8, 128)), _sds((t, dm)), _sds((t, dm), BF16), _sds((8, dm))],
                  compiler_params=_cp(("arbitrary",)))(x1, f, mod, fw, target)


def _rope_tables(t, lc):
    l = t - lc
    rows = l // GRID_W
    grid_r = jnp.broadcast_to(jnp.arange(rows, dtype=F32)[:, None], (rows, GRID_W)).reshape(-1)
    grid_c = jnp.broadcast_to(jnp.arange(GRID_W, dtype=F32)[None, :], (rows, GRID_W)).reshape(-1)

    def angles(p_seq, p_row, p_col):
        parts = []
        for p, n in zip((p_seq, p_row, p_col), ROPE_PAIRS):
            freq = ROPE_BASE ** (-jnp.arange(n, dtype=F32) / n)
            parts.append(p[:, None] * freq[None, :])
        return jnp.concatenate(parts, axis=-1)

    zc = jnp.zeros((lc,), F32)
    ang = jnp.concatenate([angles(jnp.arange(lc, dtype=F32), zc, zc),
                           angles(jnp.full((l,), lc, F32), grid_r, grid_c)], axis=0)
    cos, sin = jnp.cos(ang), jnp.sin(ang)
    return jnp.concatenate([cos, cos], axis=-1), jnp.concatenate([-sin, sin], axis=-1)


def _rope(u, cs, sn):
    return u * cs + pltpu.roll(u, HEAD_DIM // 2, 1) * sn


def _rope_t(d, cs, sn):
    return d * cs + pltpu.roll(d * sn, HEAD_DIM // 2, 1)


def _even_qkv(name, p, cs, sn):
    t = p.shape[0]
    tm = ROW_TILE
    w = HEADS * HEAD_DIM
    scale = HEAD_DIM ** -0.5

    def body(q_ref, k_ref, v_ref, cs_ref, sn_ref, qo_ref, ko_ref, vo_ref):
        c, s = cs_ref[...], sn_ref[...]
        for h in range(HEADS):
            sl = slice(h * HEAD_DIM, (h + 1) * HEAD_DIM)
            qo_ref[:, sl] = (_rope(q_ref[:, sl], c, s) * scale).astype(BF16)
            ko_ref[:, sl] = _rope(k_ref[:, sl], c, s).astype(BF16)
        vo_ref[...] = v_ref[...].astype(BF16)

    col = lambda j: pl.BlockSpec((tm, w), lambda i: (i, j))
    tab = pl.BlockSpec((tm, HEAD_DIM), lambda i: (i, 0))
    o = _sds((t, w), BF16)
    return _pcall(body, name=name, grid=(t // tm,), in_specs=[col(0), col(1), col(2), tab, tab],
                  out_specs=[col(0)] * 3, out_shape=[o, o, o], compiler_params=_cp(("parallel",)))(p, p, p, cs, sn)


def _log_sigmoid_row(x):
    e = jnp.exp(-jnp.abs(x))
    l1p = jnp.where(e < 0.01, e * (1.0 - e * (0.5 - e * (1.0 / 3.0))), jnp.log(1.0 + e))
    return jnp.minimum(x, 0.0) - l1p


def _ret_tables(lgb_ref, dm_ref, xi_ref, zt_ref):
    ri = lax.broadcasted_iota(jnp.int32, (CHUNK, CHUNK), 0).astype(F32)
    ci = lax.broadcasted_iota(jnp.int32, (CHUNK, CHUNK), 1).astype(F32)
    for d in range(2):
        for h in range(HEADS):
            idx = d * HEADS + h
            lg = _log_sigmoid_row(lgb_ref[idx:idx + 1, :])
            if d == 0:
                e, mask = ri - ci, ri >= ci
                xe, ze = ri + 1.0, (CHUNK - 1.0) - ri
            else:
                e, mask = ci - ri - 1.0, ci > ri
                xe, ze = (CHUNK - 1.0) - ri, ri
            dm_ref[idx] = jnp.where(mask, jnp.exp(lg * jnp.where(mask, e, 0.0)), 0.0)
            xi_ref[idx] = jnp.exp(lg * xe)
            zt_ref[idx] = jnp.exp(lg * ze)


def _ret_exponents(d):
    ri = lax.broadcasted_iota(jnp.int32, (CHUNK, CHUNK), 0).astype(F32)
    ci = lax.broadcasted_iota(jnp.int32, (CHUNK, CHUNK), 1).astype(F32)
    if d == 0:
        return ri - ci, ri + 1.0, (CHUNK - 1.0) - ri
    return ci - ri - 1.0, (CHUNK - 1.0) - ri, ri


RET_SUB = 2


def _bwd_chunk(n, ncc, nc):
    return jnp.where(n < ncc, ncc - 1 - n, nc - 1 - (n - ncc))


def _retention_fwd(name, q, k, v, lgb, lc):
    t, w = q.shape
    nc = t // CHUNK
    rows_per = RET_SUB * CHUNK
    nb, ncb = t // rows_per, lc // rows_per
    nh = 2 * HEADS

    def body(qf_ref, kf_ref, vf_ref, qb_ref, kb_ref, vb_ref, lgb_ref, of_ref, ob_ref, ss_ref,
             s_ref, dm_ref, xi_ref, zt_ref):
        n = pl.program_id(0)

        @pl.when(n == 0)
        def _():
            s_ref[...] = jnp.zeros_like(s_ref)
            _ret_tables(lgb_ref, dm_ref, xi_ref, zt_ref)

        where = []
        for u in range(RET_SUB):
            for d in range(2):
                refs = (qf_ref, kf_ref, vf_ref, of_ref) if d == 0 else (qb_ref, kb_ref, vb_ref, ob_ref)
                r0 = (u if d == 0 else RET_SUB - 1 - u) * CHUNK
                for h in range(HEADS):
                    where.append((u, d * HEADS + h, refs, slice(r0, r0 + CHUNK), slice(h * HEAD_DIM, (h + 1) * HEAD_DIM)))
        qs = [refs[0][rows, sl] for _, _, refs, rows, sl in where]
        ks = [refs[1][rows, sl] for _, _, refs, rows, sl in where]
        vs = [refs[2][rows, sl] for _, _, refs, rows, sl in where]
        sc = [_dot(qv, kv, NT) for qv, kv in zip(qs, ks)]
        upd = [_dot(kv.astype(F32) * zt_ref[idx], vv, TN) for (_, idx, *_), kv, vv in zip(where, ks, vs)]
        cur = [s_ref[idx] for idx in range(nh)]
        gcs = [jnp.exp(_log_sigmoid_row(lgb_ref[idx:idx + 1, :]) * float(CHUNK)) for idx in range(nh)]
        st = []
        for (u, idx, *_), du in zip(where, upd):
            st.append(cur[idx])
            ss_ref[u, idx] = cur[idx]
            cur[idx] = gcs[idx] * cur[idx] + du
        for idx in range(nh):
            s_ref[idx] = cur[idx]
        inter = [_dot(qv.astype(F32) * xi_ref[idx], s, NN) for (_, idx, *_), qv, s in zip(where, qs, st)]
        intra = [_dot(a * dm_ref[idx], vv, NN) for (_, idx, *_), a, vv in zip(where, sc, vs)]
        for (_, _, refs, rows, sl), o1, o2 in zip(where, intra, inter):
            refs[3][rows, sl] = o1 + o2

    fspec = pl.BlockSpec((rows_per, w), lambda n: (n, 0))
    bspec = pl.BlockSpec((rows_per, w), lambda n: (_bwd_chunk(n, ncb, nb), 0))
    tab = pltpu.VMEM((nh, CHUNK, CHUNK), F32)
    return _pcall(body, name=name, grid=(nb,),
                  in_specs=[fspec] * 3 + [bspec] * 3 + [_full((nh, HEAD_DIM))],
                  out_specs=[fspec, bspec, pl.BlockSpec((RET_SUB, nh, CHUNK, CHUNK), lambda n: (n, 0, 0, 0))],
                  out_shape=[_sds((t, w)), _sds((t, w)), _sds((nc, nh, CHUNK, CHUNK))],
                  scratch_shapes=[tab, tab, tab, tab],
                  compiler_params=_cp(("arbitrary",)))(q, k, v, q, k, v, lgb)


def _retention_bwd(name, q, k, v, do, ss, lgb, lc):
    t, w = q.shape
    rows_per = RET_SUB * CHUNK
    nb, ncb = t // rows_per, lc // rows_per
    nh = 2 * HEADS

    def body(qf_ref, kf_ref, vf_ref, gf_ref, qb_ref, kb_ref, vb_ref, gb_ref, ss_ref, lgb_ref,
             dqf_ref, dkf_ref, dvf_ref, dqb_ref, dkb_ref, dvb_ref, dl_ref,
             ds_ref, dm_ref, xi_ref, zt_ref, acc_ref):
        n = pl.program_id(0)

        @pl.when(n == 0)
        def _():
            ds_ref[...] = jnp.zeros_like(ds_ref)
            acc_ref[...] = jnp.zeros_like(acc_ref)
            _ret_tables(lgb_ref, dm_ref, xi_ref, zt_ref)

        where = []
        for u in reversed(range(RET_SUB)):
            for d in range(2):
                refs = ((qf_ref, kf_ref, vf_ref, gf_ref, dqf_ref, dkf_ref, dvf_ref) if d == 0
                        else (qb_ref, kb_ref, vb_ref, gb_ref, dqb_ref, dkb_ref, dvb_ref))
                r0 = (u if d == 0 else RET_SUB - 1 - u) * CHUNK
                for h in range(HEADS):
                    where.append((u, d * HEADS + h, d, refs, slice(r0, r0 + CHUNK), slice(h * HEAD_DIM, (h + 1) * HEAD_DIM)))
        qs = [refs[0][rows, sl] for *_, refs, rows, sl in where]
        ks = [refs[1][rows, sl] for *_, refs, rows, sl in where]
        vs = [refs[2][rows, sl] for *_, refs, rows, sl in where]
        gs = [refs[3][rows, sl] for *_, refs, rows, sl in where]
        st = [ss_ref[u, idx] for u, idx, *_ in where]
        sc = [_dot(qv, kv, NT) for qv, kv in zip(qs, ks)]
        dar = [_dot(gv, vv, NT) for gv, vv in zip(gs, vs)]
        t1 = [_dot(gv, s, NT) for gv, s in zip(gs, st)]
        dsn = [_dot(qv.astype(F32) * xi_ref[idx], gv, TN) for (_, idx, *_), qv, gv in zip(where, qs, gs)]
        cur = [ds_ref[idx] for idx in range(nh)]
        gcs = [jnp.exp(_log_sigmoid_row(lgb_ref[idx:idx + 1, :]) * float(CHUNK)) for idx in range(nh)]
        dsps = []
        for (_, idx, *_), x in zip(where, dsn):
            dsps.append(cur[idx])
            cur[idx] = gcs[idx] * cur[idx] + x
        for idx in range(nh):
            ds_ref[idx] = cur[idx]
        t2 = [_dot(vv, dsp, NT) for vv, dsp in zip(vs, dsps)]
        dv2 = [_dot(kv.astype(F32) * zt_ref[idx], dsp, NN) for (_, idx, *_), kv, dsp in zip(where, ks, dsps)]
        a = [x * dm_ref[idx] for (_, idx, *_), x in zip(where, sc)]
        da = [x * dm_ref[idx] for (_, idx, *_), x in zip(where, dar)]
        dq = [_dot(x, kv, NN) for x, kv in zip(da, ks)]
        dk = [_dot(x, qv, TN) for x, qv in zip(da, qs)]
        dv1 = [_dot(x, gv, TN) for x, gv in zip(a, gs)]
        for i8, (_, idx, d, refs, rows, sl) in enumerate(where):
            ee, xe, ze = _ret_exponents(d)
            xi, zt = xi_ref[idx], zt_ref[idx]
            qf32, kf32 = qs[i8].astype(F32), ks[i8].astype(F32)
            refs[4][rows, sl] = dq[i8] + xi * t1[i8]
            refs[5][rows, sl] = dk[i8] + zt * t2[i8]
            refs[6][rows, sl] = dv1[i8] + dv2[i8]
            acc_ref[idx] += (ee * a[i8] * dar[i8] + xe * xi * qf32 * t1[i8] + ze * zt * kf32 * t2[i8]
                             + (float(CHUNK) * gcs[idx]) * dsps[i8] * st[i8])

        @pl.when(n == nb - 1)
        def _():
            for idx in range(nh):
                tot = jnp.sum(acc_ref[idx])
                dl_ref[idx:idx + 1, :] = tot * _sigmoid(-lgb_ref[idx:idx + 1, :])

    fmap = lambda n: (nb - 1 - n, 0)
    bmap = lambda n: (_bwd_chunk(nb - 1 - n, ncb, nb), 0)
    fspec = pl.BlockSpec((rows_per, w), fmap)
    bspec = pl.BlockSpec((rows_per, w), bmap)
    tab = pltpu.VMEM((nh, CHUNK, CHUNK), F32)
    o = _sds((t, w))
    return _pcall(body, name=name, grid=(nb,),
                  in_specs=[fspec] * 4 + [bspec] * 4
                  + [pl.BlockSpec((RET_SUB, nh, CHUNK, CHUNK), lambda n: (nb - 1 - n, 0, 0, 0)), _full((nh, HEAD_DIM))],
                  out_specs=[fspec] * 3 + [bspec] * 3 + [_full((nh, HEAD_DIM))],
                  out_shape=[o] * 6 + [_sds((nh, HEAD_DIM))],
                  scratch_shapes=[tab, tab, tab, tab, tab],
                  compiler_params=_cp(("arbitrary",)))(q, k, v, do, q, k, v, do, ss, lgb)


def _halo_specs(tm, halo, t, width, col):
    hb = tm // halo
    last = t // halo - 1
    prev = pl.BlockSpec((halo, width), lambda i: (jnp.maximum(i * hb - 1, 0), col))
    nxt = pl.BlockSpec((halo, width), lambda i: (jnp.minimum((i + 1) * hb, last), col))
    return prev, nxt


def _halo_valid(i, nct, nt):
    vp = jnp.logical_and(i != 0, i != nct)
    vn = jnp.logical_and(i != nct - 1, i != nt - 1)
    return vp, vn


def _fill_window(win_ref, prev, cur, nxt, vp, vn, halo, tm):
    win_ref[0:halo, :] = jnp.where(vp, prev, 0.0)
    win_ref[halo:halo + tm, :] = cur
    win_ref[halo + tm:halo + tm + halo, :] = jnp.where(vn, nxt, 0.0)


CONV_SUB = 64


SUBLANES = 8


def _shift_window(win_ref, sh_ref, tm):
    rows = tm + 2 * CONV_HALO - SUBLANES
    for s in range(SUBLANES):
        sh_ref[s, 0:rows, :] = win_ref[s:s + rows, :]


def _window_rows(sh_ref, start, rows):
    s = start % SUBLANES
    return sh_ref[s, start - s:start - s + rows, :]


def _conv_taps(sh_ref, w_ref, tm, flip):
    outs = []
    for r0 in range(0, tm, CONV_SUB):
        acc = None
        for kk in range(CONV_K):
            wk = (CONV_K - 1 - kk) if flip else kk
            term = w_ref[wk:wk + 1, :] * _window_rows(sh_ref, r0 + kk + 1, CONV_SUB)
            acc = term if acc is None else acc + term
        outs.append(acc)
    return jnp.concatenate(outs, axis=0)


def _head_norm(y):
    r = lax.rsqrt(jnp.mean(y * y, axis=-1, keepdims=True) + EPS)
    return y * r, r


def _ln_stats(y):
    mu = jnp.mean(y, axis=-1, keepdims=True)
    yc = y - mu
    rs = lax.rsqrt(jnp.mean(yc * yc, axis=-1, keepdims=True) + EPS)
    return yc * rs, rs


def _ln_bwd(dyh, yh, rs):
    return rs * (dyh - jnp.mean(dyh, axis=-1, keepdims=True) - yh * jnp.mean(dyh * yh, axis=-1, keepdims=True))


def _even_mix(name, p, of, ob, cw, lnw, lnb, nct):
    t = p.shape[0]
    tm, halo = ROW_TILE, CONV_HALO
    nt = t // tm
    w = HEADS * HEAD_DIM

    def body(g_ref, a_ref, gb_ref, ap_ref, gbp_ref, an_ref, gbn_ref, of_ref, ob_ref, cw_ref, lw_ref, lb_ref,
             mix_ref, yc_ref, win_ref, sh_ref):
        i = pl.program_id(0)
        vp, vn = _halo_valid(i, nct, nt)
        glu = lambda a, b: a * _sigmoid(b)
        _fill_window(win_ref, glu(ap_ref[...], gbp_ref[...]), glu(a_ref[...], gb_ref[...]),
                     glu(an_ref[...], gbn_ref[...]), vp, vn, halo, tm)
        _shift_window(win_ref, sh_ref, tm)
        yc = _conv_taps(sh_ref, cw_ref, tm, False)
        yc_ref[...] = yc
        yh, _ = _ln_stats(yc)
        mix_ref[:, w:2 * w] = _silu(yh * lw_ref[...] + lb_ref[...]).astype(BF16)
        for h in range(HEADS):
            sl = slice(h * HEAD_DIM, (h + 1) * HEAD_DIM)
            yn, _ = _head_norm(of_ref[:, sl] + ob_ref[:, sl])
            mix_ref[:, sl] = (_silu(g_ref[:, sl]) * yn).astype(BF16)

    col = lambda j: pl.BlockSpec((tm, w), lambda i: (i, j))
    ap, an = _halo_specs(tm, halo, t, w, 4)
    gp, gn = _halo_specs(tm, halo, t, w, 5)
    row = pl.BlockSpec((tm, w), lambda i: (i, 0))
    return _pcall(body, name=name, grid=(nt,),
                  in_specs=[col(3), col(4), col(5), ap, gp, an, gn, row, row,
                            _full(cw.shape), _full(lnw.shape), _full(lnb.shape)],
                  out_specs=[pl.BlockSpec((tm, 2 * w), lambda i: (i, 0)), row],
                  out_shape=[_sds((t, 2 * w), BF16), _sds((t, w))],
                  scratch_shapes=[pltpu.VMEM((tm + 2 * halo, w), F32), pltpu.VMEM((SUBLANES, tm + 2 * halo, w), F32)],
                  compiler_params=_cp(("parallel",)))(p, p, p, p, p, p, p, of, ob, cw, lnw, lnb)


def _even_mix_bwd1(name, dmix, p, of, ob, yc, lnw, lnb):
    t = p.shape[0]
    tm = ROW_TILE
    w = HEADS * HEAD_DIM

    def body(dr_ref, dc_ref, g_ref, of_ref, ob_ref, yc_ref, lw_ref, lb_ref, do_ref, dg_ref, dyc_ref, s_ref):
        @pl.when(pl.program_id(0) == 0)
        def _():
            s_ref[...] = jnp.zeros_like(s_ref)

        for h in range(HEADS):
            sl = slice(h * HEAD_DIM, (h + 1) * HEAD_DIM)
            yn, r = _head_norm(of_ref[:, sl] + ob_ref[:, sl])
            gv = g_ref[:, sl]
            dr = dr_ref[:, sl]
            dg_ref[:, sl] = (dr * yn * _dsilu(gv)).astype(BF16)
            dyn = dr * _silu(gv)
            do_ref[:, sl] = (r * (dyn - yn * jnp.mean(dyn * yn, axis=-1, keepdims=True))).astype(BF16)
        yh, rs = _ln_stats(yc_ref[...])
        lw = lw_ref[...]
        dlo = dc_ref[...] * _dsilu(yh * lw + lb_ref[...])
        dyc_ref[...] = _ln_bwd(dlo * lw, yh, rs)
        s_ref[...] += jnp.concatenate([_colsum(dlo * yh), _colsum(dlo), jnp.zeros((6, w), F32)], axis=0)

    col = lambda j: pl.BlockSpec((tm, w), lambda i: (i, j))
    row = pl.BlockSpec((tm, w), lambda i: (i, 0))
    return _pcall(body, name=name, grid=(t // tm,),
                  in_specs=[col(0), col(1), col(3), row, row, row, _full(lnw.shape), _full(lnb.shape)],
                  out_specs=[row, row, row, _full((8, w))],
                  out_shape=[_sds((t, w), BF16), _sds((t, w), BF16), _sds((t, w)), _sds((8, w))],
                  compiler_params=_cp(("arbitrary",)))(dmix, dmix, p, of, ob, yc, lnw, lnb)


def _even_conv_bwd(name, dyc, p, cw, nct):
    t = p.shape[0]
    tm, halo = ROW_TILE, CONV_HALO
    nt = t // tm
    w = HEADS * HEAD_DIM

    def body(d_ref, dp_ref, dn_ref, a_ref, gb_ref, ap_ref, gbp_ref, an_ref, gbn_ref, cw_ref,
             da_ref, dgb_ref, dw_ref, dwin_ref, uwin_ref, dsh_ref, ush_ref):
        i = pl.program_id(0)

        @pl.when(i == 0)
        def _():
            dw_ref[...] = jnp.zeros_like(dw_ref)

        vp, vn = _halo_valid(i, nct, nt)
        glu = lambda a, b: a * _sigmoid(b)
        dcur = d_ref[...]
        _fill_window(dwin_ref, dp_ref[...], dcur, dn_ref[...], vp, vn, halo, tm)
        _fill_window(uwin_ref, glu(ap_ref[...], gbp_ref[...]), glu(a_ref[...], gb_ref[...]),
                     glu(an_ref[...], gbn_ref[...]), vp, vn, halo, tm)
        _shift_window(dwin_ref, dsh_ref, tm)
        _shift_window(uwin_ref, ush_ref, tm)
        du = _conv_taps(dsh_ref, cw_ref, tm, True)
        av = a_ref[...]
        sg = _sigmoid(gb_ref[...])
        da_ref[...] = (du * sg).astype(BF16)
        dgb_ref[...] = (du * av * sg * (1.0 - sg)).astype(BF16)
        rows = [_colsum(dcur * _window_rows(ush_ref, kk + 1, tm)) for kk in range(CONV_K)]
        dw_ref[...] += jnp.concatenate(rows + [jnp.zeros((1, w), F32)], axis=0)

    col = lambda j: pl.BlockSpec((tm, w), lambda i: (i, j))
    row = pl.BlockSpec((tm, w), lambda i: (i, 0))
    dp, dn = _halo_specs(tm, halo, t, w, 0)
    ap, an = _halo_specs(tm, halo, t, w, 4)
    gp, gn = _halo_specs(tm, halo, t, w, 5)
    win = pltpu.VMEM((tm + 2 * halo, w), F32)
    shifted = pltpu.VMEM((SUBLANES, tm + 2 * halo, w), F32)
    return _pcall(body, name=name, grid=(nt,),
                  in_specs=[row, dp, dn, col(4), col(5), ap, gp, an, gn, _full(cw.shape)],
                  out_specs=[row, row, _full((CONV_K + 1, w))],
                  out_shape=[_sds((t, w), BF16), _sds((t, w), BF16), _sds((CONV_K + 1, w))],
                  scratch_shapes=[win, win, shifted, shifted],
                  compiler_params=_cp(("arbitrary",)))(dyc, dyc, dyc, p, p, p, p, p, p, cw)


def _even_dp(name, dqs, dks, dvs, dg, da, dgb, cs, sn):
    t, w = dg.shape
    tm = ROW_TILE
    scale = HEAD_DIM ** -0.5

    def body(dqf_ref, dqb_ref, dkf_ref, dkb_ref, dvf_ref, dvb_ref, dg_ref, da_ref, dgb_ref, cs_ref, sn_ref, dp_ref):
        c, s = cs_ref[...], sn_ref[...]
        for h in range(HEADS):
            sl = slice(h * HEAD_DIM, (h + 1) * HEAD_DIM)
            dp_ref[:, sl] = (_rope_t(dqf_ref[:, sl] + dqb_ref[:, sl], c, s) * scale).astype(BF16)
            dp_ref[:, w + h * HEAD_DIM:w + (h + 1) * HEAD_DIM] = _rope_t(dkf_ref[:, sl] + dkb_ref[:, sl], c, s).astype(BF16)
        dp_ref[:, 2 * w:3 * w] = (dvf_ref[...] + dvb_ref[...]).astype(BF16)
        dp_ref[:, 3 * w:4 * w] = dg_ref[...]
        dp_ref[:, 4 * w:5 * w] = da_ref[...]
        dp_ref[:, 5 * w:6 * w] = dgb_ref[...]

    row = pl.BlockSpec((tm, w), lambda i: (i, 0))
    tab = pl.BlockSpec((tm, HEAD_DIM), lambda i: (i, 0))
    return _pcall(body, name=name, grid=(t // tm,), in_specs=[row] * 9 + [tab, tab],
                  out_specs=pl.BlockSpec((tm, 6 * w), lambda i: (i, 0)), out_shape=_sds((t, 6 * w), BF16),
                  compiler_params=_cp(("parallel",)))(dqs[0], dqs[1], dks[0], dks[1], dvs[0], dvs[1], dg, da, dgb, cs, sn)


GROUPS = 4
GC = 128
INV_SQRT2 = 0.7071067811865476
INV_SQRT_2PI = 0.3989422804014327


def _gelu(x):
    return 0.5 * x * (1.0 + lax.erf(x * INV_SQRT2))


def _dgelu(x):
    return 0.5 * (1.0 + lax.erf(x * INV_SQRT2)) + x * jnp.exp(-0.5 * x * x) * INV_SQRT_2PI


def _pool_count(i, nct, lc, t, tm, rows, row0, left, right):
    is_ctx = i < nct
    seg_start = jnp.where(is_ctx, 0, lc)
    seg_len = jnp.where(is_ctx, lc, t - lc)
    pos = i * tm + row0 - seg_start + lax.broadcasted_iota(jnp.int32, (rows, GC), 0)
    cnt = jnp.minimum(pos + right, seg_len - 1) - jnp.maximum(pos - left, 0) + 1
    return jnp.maximum(cnt, 1).astype(F32)


def _spatial_gate(vln, sgw_ref, sgb_ref, tm):
    cols = []
    for g in range(GROUPS):
        sl = slice(g * GC, (g + 1) * GC)
        parts = [_dot(sgw_ref[g], vln[r0:r0 + CHUNK, sl], NN) + sgb_ref[g] for r0 in range(0, tm, CHUNK)]
        cols.append(jnp.concatenate(parts, axis=0))
    return jnp.concatenate(cols, axis=1)


def _odd_mix(name, p, pw, pscale, lnw, lnb, sgw, sgb, nct, lc):
    t = p.shape[0]
    tm, halo = ROW_TILE, POOL_HALO
    nt = t // tm
    w = GROUPS * GC

    def body(pc_ref, pp_ref, pn_ref, pu_ref, pv_ref, pw_ref, ps_ref, lw_ref, lb_ref, sgw_ref, sgb_ref,
             mix_ref, m_ref, win_ref):
        i = pl.program_id(0)
        vp, vn = _halo_valid(i, nct, nt)
        pc = pc_ref[...]
        _fill_window(win_ref, pp_ref[...], pc, pn_ref[...], vp, vn, halo, tm)
        for g, wd in enumerate(POOL_WINDOWS):
            sl = slice(g * GC, (g + 1) * GC)
            left = wd // 2
            right = wd - 1 - left
            s = None
            for o in range(-left, right + 1):
                term = win_ref[halo + o:halo + o + tm, sl]
                s = term if s is None else s + term
            mg = s / _pool_count(i, nct, lc, t, tm, tm, 0, left, right) - pc[:, sl]
            m_ref[:, sl] = mg
            mix_ref[:, sl] = (_dot(mg, pw_ref[g], NN) * ps_ref[:, sl]).astype(BF16)
        u = _gelu(pu_ref[...])
        vh, _ = _ln_stats(_gelu(pv_ref[...]))
        s = _spatial_gate(vh * lw_ref[...] + lb_ref[...], sgw_ref, sgb_ref, tm)
        mix_ref[:, w:2 * w] = (u * s).astype(BF16)

    col = lambda j: pl.BlockSpec((tm, w), lambda i: (i, j))
    pp, pn = _halo_specs(tm, halo, t, w, 0)
    return _pcall(body, name=name, grid=(nt,),
                  in_specs=[col(0), pp, pn, col(1), col(2), _full(pw.shape), _full(pscale.shape),
                            _full(lnw.shape), _full(lnb.shape), _full(sgw.shape), _full(sgb.shape)],
                  out_specs=[pl.BlockSpec((tm, 2 * w), lambda i: (i, 0)), col(0)],
                  out_shape=[_sds((t, 2 * w), BF16), _sds((t, w))],
                  scratch_shapes=[pltpu.VMEM((tm + 2 * halo, w), F32)],
                  compiler_params=_cp(("parallel",)))(p, p, p, p, p, pw, pscale, lnw, lnb, sgw, sgb)


def _odd_mix_bwd1(name, dmix, p, m, pw, pscale, lnw, lnb, sgw, sgb):
    t = p.shape[0]
    tm = ROW_TILE
    w = GROUPS * GC

    def body(dpo_ref, dsg_ref, pu_ref, pv_ref, m_ref, pw_ref, ps_ref, lw_ref, lb_ref, sgw_ref, sgb_ref,
             dm_ref, dpd_ref, vec_ref, dpw_ref, dsgw_ref, dsgb_ref):
        @pl.when(pl.program_id(0) == 0)
        def _():
            vec_ref[...] = jnp.zeros_like(vec_ref)
            dpw_ref[...] = jnp.zeros_like(dpw_ref)
            dsgw_ref[...] = jnp.zeros_like(dsgw_ref)
            dsgb_ref[...] = jnp.zeros_like(dsgb_ref)

        dscale = []
        for g in range(GROUPS):
            sl = slice(g * GC, (g + 1) * GC)
            mg = m_ref[:, sl]
            dpo = dpo_ref[:, sl]
            dscale.append(_colsum(dpo * _dot(mg, pw_ref[g], NN)))
            dpo = dpo * ps_ref[:, sl]
            dm_ref[:, sl] = _dot(dpo, pw_ref[g], NT)
            dpw_ref[g] += _dot(mg, dpo, TN)
        pu, pv = pu_ref[...], pv_ref[...]
        u = _gelu(pu)
        vh, rs = _ln_stats(_gelu(pv))
        lw = lw_ref[...]
        vln = vh * lw + lb_ref[...]
        s = _spatial_gate(vln, sgw_ref, sgb_ref, tm)
        dsg = dsg_ref[...]
        dpd_ref[:, 0:w] = (dsg * s * _dgelu(pu)).astype(BF16)
        ds = dsg * u
        cols = []
        for g in range(GROUPS):
            sl = slice(g * GC, (g + 1) * GC)
            parts = []
            for r0 in range(0, tm, CHUNK):
                dsc = ds[r0:r0 + CHUNK, sl]
                parts.append(_dot(sgw_ref[g], dsc, TN))
                dsgw_ref[g] += _dot(dsc, vln[r0:r0 + CHUNK, sl], NT)
                dsgb_ref[g] += dsc
            cols.append(jnp.concatenate(parts, axis=0))
        dvln = jnp.concatenate(cols, axis=1)
        dpd_ref[:, w:2 * w] = (_ln_bwd(dvln * lw, vh, rs) * _dgelu(pv)).astype(BF16)
        vec_ref[...] += jnp.concatenate([jnp.concatenate(dscale, axis=1), _colsum(dvln * vh), _colsum(dvln),
                                         jnp.zeros((5, w), F32)], axis=0)

        @pl.when(pl.program_id(0) == t // tm - 1)
        def _():
            for g in range(GROUPS):
                dsgb_ref[g] = jnp.broadcast_to(jnp.sum(dsgb_ref[g], axis=1, keepdims=True), (GC, GC))

    col = lambda j: pl.BlockSpec((tm, w), lambda i: (i, j))
    mat = _full((GROUPS, GC, GC))
    return _pcall(body, name=name, grid=(t // tm,),
                  in_specs=[col(0), col(1), col(1), col(2), col(0), _full(pw.shape), _full(pscale.shape),
                            _full(lnw.shape), _full(lnb.shape), _full(sgw.shape), _full(sgb.shape)],
                  out_specs=[col(0), pl.BlockSpec((tm, 2 * w), lambda i: (i, 0)), _full((8, w)), mat, mat, mat],
                  out_shape=[_sds((t, w)), _sds((t, 2 * w), BF16), _sds((8, w)),
                             _sds((GROUPS, GC, GC)), _sds((GROUPS, GC, GC)), _sds((GROUPS, GC, GC))],
                  compiler_params=_cp(("arbitrary",)))(dmix, dmix, p, p, m, pw, pscale, lnw, lnb, sgw, sgb)


def _odd_dp(name, dm, dpd, nct, lc):
    t, w = dm.shape
    tm, halo = ROW_TILE, POOL_HALO
    nt = t // tm

    def body(d_ref, dp_ref, dn_ref, dpd_ref, o_ref, win_ref):
        i = pl.program_id(0)
        vp, vn = _halo_valid(i, nct, nt)
        dcur = d_ref[...]
        _fill_window(win_ref, dp_ref[...], dcur, dn_ref[...], vp, vn, halo, tm)
        for g, wd in enumerate(POOL_WINDOWS):
            sl = slice(g * GC, (g + 1) * GC)
            left = wd // 2
            right = wd - 1 - left
            win_ref[:, sl] = win_ref[:, sl] / _pool_count(i, nct, lc, t, tm, tm + 2 * halo, -halo, left, right)
            s = None
            for o in range(-right, left + 1):
                term = win_ref[halo + o:halo + o + tm, sl]
                s = term if s is None else s + term
            o_ref[:, sl] = (s - dcur[:, sl]).astype(BF16)
        o_ref[:, w:3 * w] = dpd_ref[...]

    row = pl.BlockSpec((tm, w), lambda i: (i, 0))
    pp, pn = _halo_specs(tm, halo, t, w, 0)
    return _pcall(body, name=name, grid=(nt,),
                  in_specs=[row, pp, pn, pl.BlockSpec((tm, 2 * w), lambda i: (i, 0))],
                  out_specs=pl.BlockSpec((tm, 3 * w), lambda i: (i, 0)), out_shape=_sds((t, 3 * w), BF16),
                  scratch_shapes=[pltpu.VMEM((tm + 2 * halo, w), F32)],
                  compiler_params=_cp(("parallel",)))(dm, dm, dm, dpd)


def _place():
    x, y, c = lax.axis_index("x"), lax.axis_index("y"), lax.axis_index("c")
    chips = [(1 - x, y), (x, 1 - y), (1 - x, 1 - y)]
    return x, y, c, chips


def _chip_index(cx, cy):
    return 2 * cx + cy


def _all_gather8(name, blk, after=()):
    m_per, n = blk.shape
    na = len(after)

    def body(x_ref, *rest):
        out_ref, send_sems, recv_sems, local_sem = rest[na:]
        x, y, c, chips = _place()
        me, sibling = (x, y, c), (x, y, 1 - c)

        def rows(px, py, pc):
            return out_ref.at[pl.ds((4 * px + 2 * py + pc) * m_per, m_per), :]

        def copy(k, block, to, src=None):
            return pltpu.make_async_remote_copy(
                src_ref=rows(*block) if src is None else src, dst_ref=rows(*block),
                send_sem=send_sems.at[k], recv_sem=recv_sems.at[k], device_id=to, device_id_type=MESH)

        mine = pltpu.make_async_copy(x_ref, rows(*me), local_sem)
        mine.start()
        first = [copy(0, me, sibling, src=x_ref)]
        first += [copy(1 + j, me, (*chip, c), src=x_ref) for j, chip in enumerate(chips)]
        for cp in first:
            cp.start()
        passed = [copy(4 + j, (*chip, c), sibling) for j, chip in enumerate(chips)]
        for j, chip in enumerate(chips):
            copy(1 + j, (*chip, c), me).wait_recv()
            passed[j].start()
        copy(0, sibling, me).wait_recv()
        for j, chip in enumerate(chips):
            copy(4 + j, (*chip, 1 - c), me).wait_recv()
        for cp in first + passed:
            cp.wait_send()
        mine.wait()

    return _pcall(body, name=name, out_shape=_sds((8 * m_per, n), blk.dtype),
                  in_specs=[pl.BlockSpec(memory_space=pltpu.VMEM)] + [pl.BlockSpec(memory_space=pl.ANY)] * na,
                  out_specs=pl.BlockSpec(memory_space=pltpu.VMEM),
                  scratch_shapes=[pltpu.SemaphoreType.DMA((7,)), pltpu.SemaphoreType.DMA((7,)), pltpu.SemaphoreType.DMA],
                  compiler_params=_cp())(blk, *after)


ANY = pl.BlockSpec(memory_space=pl.ANY)


def _half(which, rows):
    return pl.ds(pl.multiple_of(which * rows, 16), rows)


def _gather_weights(name, ws, after=()):
    nw = len(ws)
    na = len(after)
    ns = 7

    def body(*refs):
        w_refs, o_refs = refs[:nw], refs[nw + na:2 * nw + na]
        send_sems, recv_sems = refs[2 * nw + na:]
        x, y, c, chips = _place()
        me_chip = _chip_index(x, y)
        sibling = (x, y, 1 - c)

        def rcopy(t, k, src, dst, to):
            return pltpu.make_async_remote_copy(src_ref=src, dst_ref=dst, send_sem=send_sems.at[t * ns + k],
                                                recv_sem=recv_sems.at[t * ns + k], device_id=to, device_id_type=MESH)

        sends = []
        for t in range(nw):
            lh = w_refs[t].shape[0] // 2
            for k, chip in enumerate(chips):
                sends.append(rcopy(t, k, w_refs[t].at[_half(c, lh)], o_refs[t].at[me_chip, _half(c, lh)], (*chip, c)))
                sends[-1].start()
            sends.append(rcopy(t, 6, w_refs[t], o_refs[t].at[me_chip], sibling))
            sends[-1].start()
        for t in range(nw):
            lh = w_refs[t].shape[0] // 2
            for k, chip in enumerate(chips):
                part = o_refs[t].at[_chip_index(*chip), _half(c, lh)]
                rcopy(t, k, part, part, (*chip, c)).wait_recv()
                sends.append(rcopy(t, 3 + k, part, part, sibling))
                sends[-1].start()
        for t in range(nw):
            lh = w_refs[t].shape[0] // 2
            own = o_refs[t].at[me_chip]
            rcopy(t, 6, own, own, sibling).wait_recv()
            for k, chip in enumerate(chips):
                part = o_refs[t].at[_chip_index(*chip), _half(1 - c, lh)]
                rcopy(t, 3 + k, part, part, sibling).wait_recv()
        for cp in sends:
            cp.wait_send()

    return _pcall(body, name=name, out_shape=[_sds((4,) + w.shape, w.dtype) for w in ws],
                  in_specs=[ANY] * (nw + na), out_specs=[ANY] * nw,
                  scratch_shapes=[pltpu.SemaphoreType.DMA((ns * nw,)), pltpu.SemaphoreType.DMA((ns * nw,))],
                  compiler_params=_cp())(*ws, *after)


def _rs_share(name, ss):
    ng = len(ss)

    def body(*refs):
        o_refs = refs[ng:2 * ng]
        send_sems, recv_sems = refs[2 * ng:]
        x, y, c, _ = _place()
        cps = []
        for t in range(ng):
            lh = o_refs[t].shape[1] // 2
            mine = o_refs[t].at[:, _half(c, lh)]
            cp = pltpu.make_async_remote_copy(
                src_ref=mine, dst_ref=mine, send_sem=send_sems.at[t], recv_sem=recv_sems.at[t],
                device_id=(x, y, 1 - c), device_id_type=MESH)
            cp.start()
            cps.append(cp)
        for t in range(ng):
            lh = o_refs[t].shape[1] // 2
            cps[t].wait_send()
            theirs = o_refs[t].at[:, _half(1 - c, lh)]
            pltpu.make_async_remote_copy(
                src_ref=theirs, dst_ref=theirs, send_sem=send_sems.at[t], recv_sem=recv_sems.at[t],
                device_id=(x, y, 1 - c), device_id_type=MESH).wait_recv()

    return _pcall(body, name=name, out_shape=[_sds(s.shape, s.dtype) for s in ss],
                  in_specs=[ANY] * ng, out_specs=[ANY] * ng, input_output_aliases={t: t for t in range(ng)},
                  scratch_shapes=[pltpu.SemaphoreType.DMA((ng,)), pltpu.SemaphoreType.DMA((ng,))],
                  compiler_params=_cp())(*ss)


HBM = pl.BlockSpec(memory_space=pltpu.HBM)
SEMS = pl.BlockSpec(memory_space=pltpu.SEMAPHORE)
EFFECT = pltpu.SideEffectType.DATAFLOW_SIDE_EFFECTING
TOKEN = (8, 128)


def _in_hbm(a):
    return pltpu.with_memory_space_constraint(a, pltpu.HBM)


def _split_start(name, srcs, lands, copies, after):
    ns, nl, na = len(srcs), len(lands), len(after)
    ncopies = len(copies([s for s in srcs], [l for l in lands], probe=True))

    def body(*refs):
        src_refs, land_refs = refs[:ns], refs[ns:ns + nl]
        send_sems, recv_sems = refs[ns + nl + na], refs[ns + nl + na + 1]
        token = refs[-1]
        for k, (src, dst, to) in enumerate(copies(src_refs, land_refs)):
            pltpu.make_async_remote_copy(src_ref=src, dst_ref=dst, send_sem=send_sems.at[k], recv_sem=recv_sems.at[k],
                                         device_id=to, device_id_type=MESH).start()
        token[...] = jnp.zeros_like(token)

    thru = [pltpu.HBM(a.shape, a.dtype) for a in list(srcs) + list(lands)]
    outs = _pcall(body, name=name,
                  out_shape=(pltpu.SemaphoreType.DMA((ncopies,)), pltpu.SemaphoreType.DMA((ncopies,)), *thru, _sds(TOKEN)),
                  in_specs=[HBM] * (ns + nl) + [ANY] * na,
                  out_specs=(SEMS, SEMS, *([HBM] * (ns + nl)), pl.BlockSpec(memory_space=pltpu.VMEM)),
                  input_output_aliases={t: 2 + t for t in range(ns + nl)},
                  compiler_params=pltpu.CompilerParams(has_side_effects=EFFECT))(
        *[_in_hbm(a) for a in list(srcs) + list(lands)], *after)
    return outs[0], outs[1], list(outs[2:2 + ns]), list(outs[2 + ns:2 + ns + nl]), outs[-1]


def _split_wait(name, started, copies, after):
    send_sems, recv_sems, srcs, lands, _ = started
    ns, nl, na = len(srcs), len(lands), len(after)

    def body(*refs):
        src_refs, land_refs = refs[:ns], refs[ns:ns + nl]
        send_sems_ref, recv_sems_ref = refs[ns + nl], refs[ns + nl + 1]
        for k, (src, dst, to) in enumerate(copies(src_refs, land_refs)):
            cp = pltpu.make_async_remote_copy(src_ref=src, dst_ref=dst, send_sem=send_sems_ref.at[k],
                                              recv_sem=recv_sems_ref.at[k], device_id=to, device_id_type=MESH)
            cp.wait_send()
            cp.wait_recv()

    thru = [pltpu.HBM(a.shape, a.dtype) for a in list(srcs) + list(lands)]
    outs = _pcall(body, name=name, out_shape=tuple(thru),
                  in_specs=[HBM] * (ns + nl) + [SEMS, SEMS] + [ANY] * na, out_specs=tuple([HBM] * (ns + nl)),
                  input_output_aliases={t: t for t in range(ns + nl)},
                  compiler_params=pltpu.CompilerParams(has_side_effects=EFFECT))(
        *srcs, *lands, send_sems, recv_sems, *after)
    return list(outs[:ns]), list(outs[ns:])


def _pair_copies(n):
    def copies(src_refs, land_refs, probe=False):
        if probe:
            return [None] * n
        x, y, c, _ = _place()
        return [(src_refs[t].at[:, _half(1 - c, src_refs[t].shape[1] // 2)], land_refs[t], (x, y, 1 - c))
                for t in range(n)]
    return copies


def _gather_copies(n):
    def copies(src_refs, land_refs, probe=False):
        if probe:
            return [None] * (4 * n)
        x, y, c, chips = _place()
        me_chip = _chip_index(x, y)
        out = []
        for t in range(n):
            for to in [(*chip, c) for chip in chips] + [(x, y, 1 - c)]:
                out.append((src_refs[t], land_refs[t].at[me_chip], to))
        return out
    return copies


def _scatter_copies(n):
    def copies(src_refs, land_refs, probe=False):
        if probe:
            return [None] * (3 * n)
        x, y, c, chips = _place()
        out = []
        for t in range(n):
            for k, chip in enumerate(chips):
                out.append((src_refs[t].at[_chip_index(*chip)], land_refs[t].at[k], (*chip, c)))
        return out
    return copies


def _row_block(r, cn):
    if r % 8:
        return r
    best = 8
    for d in range(8, r + 1, 8):
        if r % d == 0 and d * cn * 4 <= (2 << 20):
            best = d
    return best


def _add_half(name, g, a, idx):
    j, rh, cn = a.shape
    tr = _row_block(rh, cn)
    nb = rh // tr

    def body(i_ref, g_ref, a_ref, o_ref):
        o_ref[...] = (g_ref[...] + a_ref[...]).astype(BF16)

    blk = (None, tr, cn)
    gs = pltpu.PrefetchScalarGridSpec(
        num_scalar_prefetch=1, grid=(j, nb),
        in_specs=[pl.BlockSpec(blk, lambda jj, i, i_ref: (jj, i_ref[0] * nb + i, 0)),
                  pl.BlockSpec(blk, lambda jj, i, i_ref: (jj, i, 0))],
        out_specs=pl.BlockSpec(blk, lambda jj, i, i_ref: (jj, i, 0)))
    return _pcall(body, name=name, grid_spec=gs, out_shape=_sds(a.shape, BF16),
                  compiler_params=_cp(("parallel", "parallel")))(idx, g, a)


def _sum_final(name, g, a, b, idx, buf, lyr, nlyr):
    _, r, cn = g.shape
    rh = r // 2
    tr = _row_block(rh, cn)
    nb = rh // tr

    def body(*refs):
        g_ref, a_ref, b_ref = refs[1:4]
        o_ref = refs[-1]
        own = g_ref[...] + a_ref[...]
        o_ref[...] = (own + b_ref[0].astype(F32)) + (b_ref[1].astype(F32) + b_ref[2].astype(F32))

    blk = (None, tr, cn)
    in_specs = [pl.BlockSpec(blk, lambda i, i_ref: (i_ref[1], i_ref[0] * nb + i, 0)),
                pl.BlockSpec(blk, lambda i, i_ref: (i_ref[1], i, 0)),
                pl.BlockSpec((3, tr, cn), lambda i, i_ref: (0, i, 0))]
    args = [idx, g, a, b]
    kw = {}
    if buf is not None:
        in_specs.append(ANY)
        args.append(buf)
        kw["input_output_aliases"] = {4: 0}
    gs = pltpu.PrefetchScalarGridSpec(
        num_scalar_prefetch=1, grid=(nb,), in_specs=in_specs,
        out_specs=pl.BlockSpec(blk, lambda i, i_ref: (lyr, i_ref[0] * nb + i, 0)))
    return _pcall(body, name=name, grid_spec=gs, out_shape=_sds((nlyr, r, cn)),
                  compiler_params=_cp(("parallel",)), **kw)(*args)


def _sum8(name, g):
    _, r, n = g.shape
    tr = 8

    def body(g_ref, o_ref):
        o_ref[...] = ((g_ref[0] + g_ref[1]) + (g_ref[2] + g_ref[3])) + ((g_ref[4] + g_ref[5]) + (g_ref[6] + g_ref[7]))

    return _pcall(body, name=name, grid=(r // tr,), in_specs=[pl.BlockSpec((8, tr, n), lambda i: (0, i, 0))],
                  out_specs=pl.BlockSpec((tr, n), lambda i: (i, 0)), out_shape=_sds((r, n)),
                  compiler_params=_cp(("parallel",)))(g)


def _ada_mod(name, c16, ada_w, bias):
    nl, dm, n = ada_w.shape

    def body(c_ref, w_ref, b_ref, o_ref):
        o_ref[...] = _dot(_silu(c_ref[...]), w_ref[...], NN) + b_ref[...]

    return _pcall(body, name=name, grid=(nl,),
                  in_specs=[_full(c16.shape), pl.BlockSpec((None, dm, n), lambda i: (i, 0, 0)),
                            pl.BlockSpec((None, 1, n), lambda i: (i, 0, 0))],
                  out_specs=pl.BlockSpec((None, 16, n), lambda i: (i, 0, 0)), out_shape=_sds((nl, 16, n)),
                  compiler_params=_cp(("parallel",)))(c16, ada_w, bias)


def _ada_bwd(name, c16, dmod, ada_w):
    nl, dm, n = ada_w.shape

    def body(c_ref, d_ref, w_ref, gw_ref, dc_ref):
        @pl.when(pl.program_id(0) == 0)
        def _():
            dc_ref[...] = jnp.zeros_like(dc_ref)

        dv = d_ref[...]
        gw_ref[...] = _dot(_silu(c_ref[...]), dv, TN)
        dc_ref[...] += _dot(dv, w_ref[...], NT)

    return _pcall(body, name=name, grid=(nl,),
                  in_specs=[_full(c16.shape), pl.BlockSpec((None, 16, n), lambda i: (i, 0, 0)),
                            pl.BlockSpec((None, dm, n), lambda i: (i, 0, 0))],
                  out_specs=[pl.BlockSpec((None, dm, n), lambda i: (i, 0, 0)), _full((16, dm))],
                  out_shape=[_sds((nl, dm, n)), _sds((16, dm))],
                  compiler_params=_cp(("arbitrary",)))(c16, dmod, ada_w)


def _rowsum16(name, dmod):
    nl, _, n = dmod.shape

    def body(d_ref, o_ref):
        o_ref[...] = _colsum(d_ref[...])

    return _pcall(body, name=name, grid=(nl,), in_specs=[pl.BlockSpec((None, 16, n), lambda i: (i, 0, 0))],
                  out_specs=pl.BlockSpec((None, 1, n), lambda i: (i, 0, 0)), out_shape=_sds((nl, 1, n)),
                  compiler_params=_cp(("parallel",)))(dmod)


def _cctx_grad(name, parts, c_ctx):
    def body(p_ref, c_ref, o_ref):
        tot = (p_ref[0:1, :] + p_ref[1:2, :]) + (p_ref[2:3, :] + p_ref[3:4, :])
        o_ref[...] = tot * _dsilu(c_ref[...])

    return _pcall(body, name=name, out_shape=_sds(c_ctx.shape), compiler_params=_cp())(parts, c_ctx)


def _adamw(name, w, g, m, v):
    shape = w.shape
    cn = shape[-1]
    r = math.prod(shape[:-1]) if len(shape) > 1 else 1
    tr = _row_block(r, cn)
    c1 = 1.0 - ADAM_B1 ** ADAM_STEP
    c2 = 1.0 - ADAM_B2 ** ADAM_STEP

    def body(w_ref, g_ref, m_ref, v_ref, d_ref, mo_ref, vo_ref):
        gv = g_ref[...]
        mn = ADAM_B1 * m_ref[...] + (1.0 - ADAM_B1) * gv
        vn = ADAM_B2 * v_ref[...] + (1.0 - ADAM_B2) * (gv * gv)
        d_ref[...] = -ADAM_LR * ((mn / c1) / (jnp.sqrt(vn / c2) + ADAM_EPS) + ADAM_WD * w_ref[...])
        mo_ref[...] = mn
        vo_ref[...] = vn

    blk = pl.BlockSpec((tr, cn), lambda i: (i, 0))
    o = _sds((r, cn))
    outs = _pcall(body, name=name, grid=(r // tr,), in_specs=[blk] * 4, out_specs=[blk] * 3, out_shape=[o, o, o],
                  compiler_params=_cp(("parallel",)))(*[a.reshape(r, cn) for a in (w, g, m, v)])
    return tuple(a.reshape(shape) for a in outs)


def _local_step(xs, target, modt, nw, fnw, get_w, get_ffn, put_g, ev, od, lc):
    t, dm = xs.shape
    nct = lc // ROW_TILE
    depth = nw.shape[0]
    cs, sn = _rope_tables(t, lc)
    saved = []
    x_in, x1p, fp = xs, None, None
    for i in range(depth):
        j, even = i // 2, i % 2 == 0
        tag = f"l{i}"
        w, deps = get_w(i, [fp] if i else [])
        if i == 0:
            _, h = _rnm(tag + "_norm1", x_in, None, None, 0, modt[0], 0, 1, nw[0, 0], nct, deps)
        else:
            x_in, h = _rnm(tag + "_norm1", x1p, fp, modt[i - 1], 5, modt[i], 0, 1, nw[i, 0], nct, deps)
        s = dict(x=x_in, h=h, w=w)
        if even:
            p = _mm_cols(tag + "_in", h, w["in"])
            q, k, v = _even_qkv(tag + "_qkv", p, cs, sn)
            of, ob, ss = _retention_fwd(tag + "_ret", q, k, v, ev["lgb"][j], lc)
            mix, yc = _even_mix(tag + "_mix", p, of, ob, ev["cw"][j], ev["lnw"][j], ev["lnb"][j], nct)
            y = _mm_full(tag + "_out", mix, w["out"], NN)
            s.update(p=p, q=q, k=k, v=v, of=of, ob=ob, ss=ss, yc=yc)
        else:
            p = _mm_cols(tag + "_in", h, w["in"])
            mix, m = _odd_mix(tag + "_mix", p, od["pw"][j], od["ps"][j], od["lnw"][j], od["lnb"][j],
                              od["sgw"][j], od["sgb"][j], nct, lc)
            y = _mm_full(tag + "_out", mix, w["out"], NN)
            s.update(p=p, m=m)
        x1, h2 = _rnm(tag + "_norm2", x_in, y, modt[i], 2, modt[i], 3, 4, nw[i, 1], nct)
        w.update(get_ffn(i, [y]))
        a, gt, up = _ffn_up(tag + "_ffn_up", h2, w["gate"], w["up"])
        f = _mm_full(tag + "_ffn_down", a, w["down"], NN)
        s.update(mix=mix, y=y, x1=x1, h2=h2, a=a, gt=gt, up=up, f=f)
        saved.append(s)
        x1p, fp = x1, f

    loss_blk, dx, df, fin_s = _fin("final", x1p, fp, modt[depth - 1], 5, fnw, target, nct)

    deps = []
    dmod = [[None] * 6 for _ in range(depth)]
    dnw = [[None, None] for _ in range(depth)]
    zero2 = jnp.zeros((2, dm), F32)
    dmod[depth - 1][5] = jnp.stack([zero2[0], fin_s[0]])
    small = dict(dfnw=fin_s[1], ev=[], od=[])
    for i in reversed(range(depth)):
        j, even = i // 2, i % 2 == 0
        tag = f"l{i}b"
        s = saved[i]
        w = s["w"]
        fh = w["down"].shape[0] // 2
        g = {}
        dgt, dup = _ffn_down_bwd(tag + "_ffn_down", df, w["down"], s["gt"], s["up"])
        g["down"] = _wgrad_rows(tag + "_gdown", s["a"], fh, df)
        g["gate"] = _wgrad_rows(tag + "_ggate", dgt, fh, s["h2"])
        g["up"] = _wgrad_rows(tag + "_gup", dup, fh, s["h2"])
        deps = put_g(i, "f", g)
        dh2 = _ffn_in_bwd(tag + "_ffn_in", dgt, dup, w["gate"], w["up"])
        dx1, dy, s2 = _bnm(tag + "_norm2", s["x1"], dh2, dx, s["y"], modt[i], 3, 4, modt[i], 2, nw[i, 1], nct, deps)
        dmod[i][3], dmod[i][4], dmod[i][2] = s2[:, 0], s2[:, 1], s2[:, 2]
        dnw[i][1] = s2[1, 3]
        dmix = _mm_full(tag + "_out", dy, w["out"], NT)
        g["out"] = _wgrad_rows(tag + "_gout", s["mix"], w["out"].shape[0] // 2, dy)
        if even:
            do, dg, dyc, lns = _even_mix_bwd1(tag + "_mix1", dmix, s["p"], s["of"], s["ob"], s["yc"],
                                              ev["lnw"][j], ev["lnb"][j])
            da, dgb, dcw = _even_conv_bwd(tag + "_conv", dyc, s["p"], ev["cw"][j], nct)
            dqf, dkf, dvf, dqb, dkb, dvb, dl = _retention_bwd(tag + "_ret", s["q"], s["k"], s["v"], do, s["ss"],
                                                              ev["lgb"][j], lc)
            dp = _even_dp(tag + "_dp", (dqf, dqb), (dkf, dkb), (dvf, dvb), dg, da, dgb, cs, sn)
            small["ev"].append(dict(lnw=lns[0], lnb=lns[1], cw=dcw, dl=dl[:, 0]))
        else:
            dm_, dpd, vec, dpw, dsgw, dsgb = _odd_mix_bwd1(tag + "_mix1", dmix, s["p"], s["m"], od["pw"][j], od["ps"][j],
                                                           od["lnw"][j], od["lnb"][j], od["sgw"][j], od["sgb"][j])
            dp = _odd_dp(tag + "_dp", dm_, dpd, nct, lc)
            small["od"].append(dict(ps=vec[0], lnw=vec[1], lnb=vec[2], pw=dpw, sgw=dsgw, sgb=dsgb[:, :, 0]))
        dh = _mm_cols_bwd(tag + "_in", dp, w["in"])
        g["in"] = _wgrad_cols(tag + "_gin", s["h"], dp, w["in"].shape[0])
        deps = put_g(i, "m", g)
        if i > 0:
            dx, df, s1 = _bnm(tag + "_norm1", s["x"], dh, dx1, saved[i - 1]["f"], modt[i], 0, 1, modt[i - 1], 5,
                              nw[i, 0], nct, deps)
            dmod[i - 1][5] = s1[:, 2]
        else:
            dx, _, s1 = _bnm(tag + "_norm1", s["x"], dh, dx1, None, modt[0], 0, 1, None, 0, nw[0, 0], nct, deps)
        dmod[i][0], dmod[i][1] = s1[:, 0], s1[:, 1]
        dnw[i][0] = s1[1, 3]
    small["ev"].reverse()
    small["od"].reverse()
    dmod_t = jnp.stack([jnp.concatenate([jnp.stack(rows, axis=1), jnp.zeros((2, 2, dm), F32)], axis=1) for rows in dmod])
    small["dmod"] = dmod_t
    small["dnw"] = jnp.stack([jnp.stack(r) for r in dnw])
    return loss_blk, dx, small


WEIGHTS = ["c_ctx", "ada_w", "ada_b", "norm_w", "even_w_in", "even_w_out", "ret_decay_logit", "conv_dw_w",
           "conv_ln_w", "conv_ln_b", "odd_w_in", "odd_w_out", "pool_w", "pool_scale", "sg_ln_w", "sg_ln_b",
           "sg_w", "sg_b", "ffn_w_gate", "ffn_w_up", "ffn_w_down", "final_norm_w"]
BIG = dict(even_in="even_w_in", even_out="even_w_out", odd_in="odd_w_in", odd_out="odd_w_out",
           gate="ffn_w_gate", up="ffn_w_up", down="ffn_w_down")


def _rows(a, width=1024):
    flat = a.reshape(-1)
    n = flat.shape[0]
    per = 8 * width
    tot = -(-n // per) * per
    return jnp.pad(flat, (0, tot - n)).reshape(tot // width, width)


def _unshard(parts, lead):
    nl = len(lead)
    perm = tuple(range(1, nl + 1)) + (0, nl + 1)
    return parts.transpose(perm).reshape(tuple(lead) + (4 * parts.shape[-1],))


def _my_cols(a, chip, n):
    start = (0,) * (a.ndim - 1) + (chip * n,)
    return lax.dynamic_slice(a, start, a.shape[:-1] + (n,))


def kernel(x, c, ctx, c_ctx, ada_w, ada_b, norm_w, even_w_in, even_w_out, ret_decay_logit, conv_dw_w, conv_ln_w, conv_ln_b, odd_w_in, odd_w_out, pool_w, pool_scale, sg_ln_w, sg_ln_b, sg_w, sg_b, ffn_w_gate, ffn_w_up, ffn_w_down, final_norm_w, loss_target, m_c_ctx, m_ada_w, m_ada_b, m_norm_w, m_even_w_in, m_even_w_out, m_ret_decay_logit, m_conv_dw_w, m_conv_ln_w, m_conv_ln_b, m_odd_w_in, m_odd_w_out, m_pool_w, m_pool_scale, m_sg_ln_w, m_sg_ln_b, m_sg_w, m_sg_b, m_ffn_w_gate, m_ffn_w_up, m_ffn_w_down, m_final_norm_w, v_c_ctx, v_ada_w, v_ada_b, v_norm_w, v_even_w_in, v_even_w_out, v_ret_decay_logit, v_conv_dw_w, v_conv_ln_w, v_conv_ln_b, v_odd_w_in, v_odd_w_out, v_pool_w, v_pool_scale, v_sg_ln_w, v_sg_ln_b, v_sg_w, v_sg_b, v_ffn_w_gate, v_ffn_w_up, v_ffn_w_down, v_final_norm_w):
    wv = dict(c_ctx=c_ctx, ada_w=ada_w, ada_b=ada_b, norm_w=norm_w, even_w_in=even_w_in, even_w_out=even_w_out,
              ret_decay_logit=ret_decay_logit, conv_dw_w=conv_dw_w, conv_ln_w=conv_ln_w, conv_ln_b=conv_ln_b,
              odd_w_in=odd_w_in, odd_w_out=odd_w_out, pool_w=pool_w, pool_scale=pool_scale, sg_ln_w=sg_ln_w,
              sg_ln_b=sg_ln_b, sg_w=sg_w, sg_b=sg_b, ffn_w_gate=ffn_w_gate, ffn_w_up=ffn_w_up,
              ffn_w_down=ffn_w_down, final_norm_w=final_norm_w)
    mv = dict(zip(WEIGHTS, (m_c_ctx, m_ada_w, m_ada_b, m_norm_w, m_even_w_in, m_even_w_out, m_ret_decay_logit,
                            m_conv_dw_w, m_conv_ln_w, m_conv_ln_b, m_odd_w_in, m_odd_w_out, m_pool_w, m_pool_scale,
                            m_sg_ln_w, m_sg_ln_b, m_sg_w, m_sg_b, m_ffn_w_gate, m_ffn_w_up, m_ffn_w_down,
                            m_final_norm_w)))
    vv = dict(zip(WEIGHTS, (v_c_ctx, v_ada_w, v_ada_b, v_norm_w, v_even_w_in, v_even_w_out, v_ret_decay_logit,
                            v_conv_dw_w, v_conv_ln_w, v_conv_ln_b, v_odd_w_in, v_odd_w_out, v_pool_w, v_pool_scale,
                            v_sg_ln_w, v_sg_ln_b, v_sg_w, v_sg_b, v_ffn_w_gate, v_ffn_w_up, v_ffn_w_down,
                            v_final_norm_w)))
    xi, yi, ci = lax.axis_index("x"), lax.axis_index("y"), lax.axis_index("c")
    chip = 2 * xi + yi
    dev = 4 * xi + 2 * yi + ci
    dm = x.shape[-1]
    lc = ctx.shape[1]
    depth = ada_w.shape[0]
    n_ada = ada_w.shape[-1]

    cw_pad = jnp.pad(conv_dw_w, ((0, 0), (0, 1), (0, 0)))
    vec3 = jnp.stack([pool_scale, sg_ln_w, sg_ln_b])
    pack1 = jnp.concatenate([_rows(c), _rows(norm_w), _rows(cw_pad), _rows(vec3)], axis=0)
    g1 = _all_gather8("gather_small", pack1).reshape(8, 32, dm)
    c_all = g1[:, 0]
    per_chip = g1[0::2]
    norm_full = _unshard(per_chip[:, 8:10].reshape(4, depth, 2, dm // 4), (depth, 2))
    cw_full = _unshard(per_chip[:, 16:24].reshape(4, 2, CONV_K + 1, 128), (2, CONV_K + 1))
    vec_full = _unshard(per_chip[:, 24, :768].reshape(4, 3, 2, 128), (3, 2))

    c16 = jnp.concatenate([c_all, c_ctx[None, :], jnp.zeros((7, dm), F32)], axis=0)
    mod_sh = _ada_mod("ada_mod", c16, ada_w, _my_cols(ada_b, chip, n_ada)[:, None, :])
    g2 = _all_gather8("gather_mod", mod_sh.reshape(depth * 16, n_ada)).reshape(8, depth, 16, n_ada)
    mod_full = _unshard(g2[0::2], (depth, 16))
    mod_x = lax.dynamic_index_in_dim(mod_full, dev, axis=1, keepdims=False).reshape(depth, 6, dm)
    mod_c = mod_full[:, 8].reshape(depth, 6, dm)
    modt = jnp.pad(jnp.stack([mod_c, mod_x], axis=1), ((0, 0), (0, 0), (0, 2), (0, 0)))

    names = list(BIG)
    tr_names = ("gate", "up")
    shard = {k: (jnp.swapaxes(wv[BIG[k]], 1, 2) if k in tr_names else wv[BIG[k]]).astype(BF16) for k in names}
    roles = ("in", "out", "gate", "up", "down")

    def layer_keys(i):
        mixer = ("even_in", "even_out") if i % 2 == 0 else ("odd_in", "odd_out")
        return [(k, i // 2) for k in mixer] + [(k, i) for k in ("gate", "up", "down")]

    def as_used(got):
        return {r: (g if r == "in" else g.reshape(4 * g.shape[1], g.shape[2])) for r, g in zip(roles, got)}

    started = {}

    def start_gather(tag, keys, before):
        srcs = [shard[k][l] for k, l in keys]
        lands = [lax.empty((4,) + s.shape, s.dtype) for s in srcs]
        return _split_start(f"gather_start{tag}", srcs, lands, _gather_copies(len(srcs)), before)

    def get_w(i, after):
        if i > 0:
            got = _split_wait(f"gather_wait{i}m", started[i, "m"], _gather_copies(2), after)[1]
            return as_used(got), []
        got = _gather_weights("gather_w0", [shard[k][l] for k, l in layer_keys(0)[:2]], [modt])
        last = None
        for li in range(depth):
            for part, keys in (("m", layer_keys(li)[:2]), ("f", layer_keys(li)[2:])):
                if (li, part) != (0, "m"):
                    started[li, part] = last = start_gather(f"{li}{part}", keys, [got[0]] + ([last[4]] if last else []))
        return as_used(got), [last[4]]

    def get_ffn(i, after):
        got = _split_wait(f"gather_wait{i}f", started[i, "f"], _gather_copies(3), after)[1]
        return {r: g.reshape(4 * g.shape[1], g.shape[2]) for r, g in zip(roles[2:], got)}

    idx = jnp.stack([ci, chip]).astype(jnp.int32)
    pairs, pending, stages = {}, {}, []

    def stage_keys(stage):
        i, part = stage
        return layer_keys(i)[2:] if part == "f" else layer_keys(i)[:2]

    def finish_pair(stage, after):
        tag = f"{stage[0]}{stage[1]}"
        n = len(stage_keys(stage))
        glist, from_sib = _split_wait(f"pair_wait{tag}", pairs[stage], _pair_copies(n), after)
        pair = [_add_half(f"rs_add{tag}_{t}", gl, a, idx) for t, (gl, a) in enumerate(zip(glist, from_sib))]
        lands = [lax.empty((3,) + p.shape[1:], p.dtype) for p in pair]
        st = _split_start(f"rs_start{tag}", pair, lands, _scatter_copies(n), [])
        pending[stage] = (glist, from_sib, st)
        return [st[4]]

    def put_g(i, part, g):
        stage = (i, part)
        glist = [g[r].reshape(4, -1, g[r].shape[-1]) for r in (roles[2:] if part == "f" else roles[:2])]
        lands = [lax.empty((4, gl.shape[1] // 2, gl.shape[2]), gl.dtype) for gl in glist]
        pairs[stage] = _split_start(f"pair_start{i}{part}", glist, lands, _pair_copies(len(glist)), [])
        tokens = [pairs[stage][4]]
        if stages:
            tokens += finish_pair(stages[-1], [pairs[stage][4]])
        stages.append(stage)
        return tokens

    ev = dict(lgb=jnp.broadcast_to(ret_decay_logit.reshape(-1, 2 * HEADS)[:, :, None], (ret_decay_logit.shape[0], 2 * HEADS, HEAD_DIM)),
              cw=cw_full, lnw=conv_ln_w[:, None, :], lnb=conv_ln_b[:, None, :])
    od = dict(pw=pool_w, ps=vec_full[0][:, None, :], lnw=vec_full[1][:, None, :], lnb=vec_full[2][:, None, :],
              sgw=sg_w, sgb=jnp.broadcast_to(sg_b[:, :, :, None], sg_b.shape + (GC,)))
    xs = jnp.concatenate([ctx[0], x[0]], axis=0)
    loss_blk, dxs, small = _local_step(xs, loss_target[0], modt, norm_full[:, :, None, :], final_norm_w[None, :],
                                       get_w, get_ffn, put_g, ev, od, lc)

    misc = jnp.stack([
        small["dfnw"], jnp.broadcast_to(loss_blk[0, 0], (dm,)),
        jnp.concatenate([e["lnw"] for e in small["ev"]]), jnp.concatenate([e["lnb"] for e in small["ev"]]),
        jnp.concatenate([o["ps"] for o in small["od"]]), jnp.concatenate([o["lnw"] for o in small["od"]]),
        jnp.concatenate([o["lnb"] for o in small["od"]]),
        jnp.pad(jnp.concatenate([e["dl"] for e in small["ev"]]), (0, dm - 4 * HEADS)),
        jnp.stack([o["sgb"] for o in small["od"]]).reshape(-1)])
    pack2 = jnp.concatenate([
        _rows(small["dmod"]), _rows(small["dnw"]), _rows(misc), _rows(jnp.stack([e["cw"] for e in small["ev"]])),
        _rows(jnp.stack([o["pw"] for o in small["od"]])), _rows(jnp.stack([o["sgw"] for o in small["od"]]))], axis=0)
    n2 = pack2.shape[0]
    g3 = _all_gather8("gather_grads", pack2)
    tot = _sum8("sum_grads", g3.reshape(8, n2, dm))
    r_mod = depth * 16
    o_nw, o_misc = r_mod, r_mod + 8
    o_cw = o_misc + 16
    o_pw = o_cw + 2 * (CONV_K + 1) // 2
    o_sgw = o_pw + 128
    dmod_sum = tot[:r_mod].reshape(depth, 2, 8, dm)
    dmod_dev = g3.reshape(8, n2, dm)[:, :r_mod].reshape(8, depth, 2, 8, dm)
    dm_x = dmod_dev[:, :, 1, :6].reshape(8, depth, 6 * dm).transpose(1, 0, 2)
    dm_c = dmod_sum[:, 0, :6].reshape(depth, 1, 6 * dm)
    dmod16 = jnp.concatenate([dm_x, dm_c, jnp.zeros((depth, 7, 6 * dm), F32)], axis=1)
    g_ada_b = _rowsum16("ada_b_grad", dmod16)[:, 0]
    g_ada_w, dc16 = _ada_bwd("ada_bwd", c16, _my_cols(dmod16, chip, n_ada), ada_w)
    g4 = _all_gather8("gather_cctx", dc16[8:16]).reshape(8, 8, dm)
    g_c_ctx = _cctx_grad("cctx_grad", g4[0::2, 0], c_ctx[None, :])[0]

    misc_t = tot[o_misc:o_misc + 16]
    half = lambda row: misc_t[row].reshape(2, dm // 2)
    grads = dict(
        c_ctx=g_c_ctx, ada_w=g_ada_w, ada_b=g_ada_b,
        norm_w=_my_cols(tot[o_nw:o_nw + 8].reshape(depth, 2, dm), chip, dm // 4),
        ret_decay_logit=misc_t[7, :4 * HEADS].reshape(ret_decay_logit.shape),
        conv_dw_w=_my_cols(tot[o_cw:o_cw + 2 * (CONV_K + 1) // 2].reshape(2, CONV_K + 1, dm // 2)[:, :CONV_K], chip, 128),
        conv_ln_w=half(2), conv_ln_b=half(3),
        pool_w=tot[o_pw:o_pw + 128].reshape(pool_w.shape),
        pool_scale=_my_cols(half(4), chip, 128), sg_ln_w=_my_cols(half(5), chip, 128), sg_ln_b=_my_cols(half(6), chip, 128),
        sg_w=tot[o_sgw:o_sgw + 128].reshape(sg_w.shape), sg_b=misc_t[8].reshape(sg_b.shape),
        final_norm_w=misc_t[0])
    loss = misc_t[1, 0]

    last_tokens = finish_pair(stages[-1], [g_c_ctx])
    deltas, new_m, new_v = {}, {}, {}
    for n in WEIGHTS:
        if n not in BIG.values():
            deltas[n], new_m[n], new_v[n] = _adamw("adamw_" + n, wv[n], grads[n], mv[n], vv[n])
    reduced = {k: None for k in names}
    for stage in stages:
        glist, from_sib, st = pending[stage]
        last = stage == stages[-1]
        after = [deltas["ada_w"]] + [reduced[k] for k, _ in stage_keys(stages[-2])] if last else last_tokens
        slots = _split_wait(f"rs_wait{stage[0]}{stage[1]}", st, _scatter_copies(len(glist)), after)[1]
        for (k, l), g, a, b in zip(stage_keys(stage), glist, from_sib, slots):
            reduced[k] = _sum_final(f"rs_sum_{k}{l}", g, a, b, idx, reduced[k], l, shard[k].shape[0])
    shards = dict(zip(names, _rs_share("rs_share", [reduced[k] for k in names])))

    for k in names:
        n = BIG[k]
        if k in tr_names:
            tr = lambda a: jnp.swapaxes(a, 1, 2)
            outs = _adamw("adamw_" + n, tr(wv[n]), shards[k], tr(mv[n]), tr(vv[n]))
            grads[n] = tr(shards[k])
            deltas[n], new_m[n], new_v[n] = (tr(o) for o in outs)
        else:
            grads[n] = shards[k]
            deltas[n], new_m[n], new_v[n] = _adamw("adamw_" + n, wv[n], grads[n], mv[n], vv[n])
    grad_x = dxs[lc:][None]
    return (loss, grad_x, *[grads[n] for n in WEIGHTS], *[deltas[n] for n in WEIGHTS],
            *[new_m[n] for n in WEIGHTS], *[new_v[n] for n in WEIGHTS])
```

```python
import functools
import math

import jax
import jax.numpy as jnp
from jax import lax
from jax.experimental import pallas as pl
from jax.experimental.pallas import tpu as pltpu

F32 = jnp.float32
BF16 = jnp.bfloat16
MESH = pl.DeviceIdType.MESH

EPS = 1e-6
GRID_W = 64
HEADS = 4
HEAD_DIM = 128
CHUNK = 128
CONV_K = 31
ROPE_BASE = 10000.0
ROPE_PAIRS = (16, 24, 24)
POOL_WINDOWS = (2, 4, 8, 16)
ADAM_LR, ADAM_B1, ADAM_B2, ADAM_EPS, ADAM_WD, ADAM_STEP = 0.001, 0.9, 0.999, 1e-08, 0.01, 10

ROW_TILE = 256
CONV_HALO = 16
POOL_HALO = 8
VMEM_LIMIT = 56 * 1024 * 1024
WGRAD_ROWS = 2304


def _pcall(body, **kw):
    return pl.pallas_call(body, **kw)


def _cp(sem=None, vmem=VMEM_LIMIT):
    if sem is None:
        return pltpu.CompilerParams(vmem_limit_bytes=vmem)
    return pltpu.CompilerParams(dimension_semantics=sem, vmem_limit_bytes=vmem)


def _sds(shape, dtype=F32):
    return jax.ShapeDtypeStruct(tuple(shape), dtype)


def _full(shape):
    nd = len(shape)
    return pl.BlockSpec(tuple(shape), lambda *_: (0,) * nd)


def _sigmoid(x):
    return jax.nn.sigmoid(x)


def _silu(x):
    return x * _sigmoid(x)


def _dsilu(x):
    s = _sigmoid(x)
    return s * (1.0 + x * (1.0 - s))


def _colsum(a):
    return jnp.sum(a, axis=0, keepdims=True)


def _dot(a, b, dn):
    return lax.dot_general(a.astype(BF16), b.astype(BF16), dn, preferred_element_type=F32)


NN = (((1,), (0,)), ((), ()))
NT = (((1,), (1,)), ((), ()))
TN = (((0,), (0,)), ((), ()))


def _mm_tile(t, cap=1152):
    best = 16
    for d in range(16, min(t, cap) + 1, 16):
        if t % d == 0:
            best = d
    return best


def _mm(name, pairs, grid, out_shape, out_spec, dn):
    npairs = len(pairs)
    nk = grid[-1]
    kax = len(grid) - 1
    assert nk == 1 or out_shape.dtype == F32

    def body(*refs):
        ins = refs[:2 * npairs]
        o_ref = refs[2 * npairs]
        tot = None
        for p in range(npairs):
            d = _dot(ins[2 * p][...], ins[2 * p + 1][...], dn)
            tot = d if tot is None else tot + d
        if nk == 1:
            o_ref[...] = tot.astype(o_ref.dtype)
        else:
            k = pl.program_id(kax)

            @pl.when(k == 0)
            def _():
                o_ref[...] = tot

            @pl.when(k != 0)
            def _():
                o_ref[...] += tot

    args, in_specs = [], []
    for a, a_spec, b, b_spec in pairs:
        args += [a, b]
        in_specs += [a_spec, b_spec]
    sem = ("parallel",) * kax + ("arbitrary",)
    return _pcall(body, name=name, grid=grid, in_specs=in_specs, out_specs=out_spec, out_shape=out_shape,
                  compiler_params=_cp(sem))(*args)


def _mm_cols(name, a, w, out_dtype=F32):
    t, k = a.shape
    j, _, n = w.shape
    tm = _mm_tile(t)
    return _mm(name, [(a, pl.BlockSpec((tm, k), lambda i, jj, kk: (i, 0)),
                       w, pl.BlockSpec((None, k, n), lambda i, jj, kk: (jj, 0, 0)))],
               (t // tm, j, 1), _sds((t, j * n), out_dtype), pl.BlockSpec((tm, n), lambda i, jj, kk: (i, jj)), NN)


def _mm_cols_bwd(name, d, w):
    t = d.shape[0]
    j, k, n = w.shape
    tm = _mm_tile(t)
    pairs = [(d, pl.BlockSpec((tm, n), functools.partial(lambda jj, i, u, kk: (i, jj), jj)),
              w, pl.BlockSpec((None, k, n), functools.partial(lambda jj, i, u, kk: (jj, 0, 0), jj))) for jj in range(j)]
    return _mm(name, pairs, (t // tm, 1, 1), _sds((t, k), BF16), pl.BlockSpec((tm, k), lambda i, u, kk: (i, 0)), NT)


def _mm_full(name, a, w, dn, tm=None):
    t, k = a.shape
    n = w.shape[1] if dn is NN else w.shape[0]
    tm = tm or _mm_tile(t)
    return _mm(name, [(a, pl.BlockSpec((tm, k), lambda i, u, kk: (i, 0)), w, _full(w.shape))],
               (t // tm, 1, 1), _sds((t, n)), pl.BlockSpec((tm, n), lambda i, u, kk: (i, 0)), dn)


def _wgrad_cols(name, a, b, j):
    t, k = a.shape
    n = b.shape[1] // j
    tt = _mm_tile(t, 2 * WGRAD_ROWS)
    return _mm(name, [(a, pl.BlockSpec((tt, k), lambda jj, u, kk: (kk, 0)),
                       b, pl.BlockSpec((tt, n), lambda jj, u, kk: (kk, jj)))],
               (j, 1, t // tt), _sds((j, k, n)), pl.BlockSpec((None, k, n), lambda jj, u, kk: (jj, 0, 0)), TN)


def _wgrad_rows(name, a, blk, b):
    t, f = a.shape
    n = b.shape[1]
    tt = _mm_tile(t, WGRAD_ROWS)
    return _mm(name, [(a, pl.BlockSpec((tt, blk), lambda jj, u, kk: (kk, jj)),
                       b, pl.BlockSpec((tt, n), lambda jj, u, kk: (kk, 0)))],
               (f // blk, 1, t // tt), _sds((f, n)), pl.BlockSpec((blk, n), lambda jj, u, kk: (jj, 0)), TN)


def _ffn_tiles(t, f):
    return _mm_tile(t, 288), f


def _ffn_up(name, h, wgt, wut):
    t, k = h.shape
    f = wgt.shape[0]
    tm, tn = _ffn_tiles(t, f)

    def body(h_ref, wg_ref, wu_ref, a_ref, gt_ref, up_ref):
        hv = h_ref[...]
        gt = _dot(hv, wg_ref[...], NT)
        up = _dot(hv, wu_ref[...], NT)
        a_ref[...] = (_silu(gt) * up).astype(BF16)
        gt_ref[...] = gt.astype(BF16)
        up_ref[...] = up.astype(BF16)

    wspec = pl.BlockSpec((tn, k), lambda i, jj: (jj, 0))
    ospec = pl.BlockSpec((tm, tn), lambda i, jj: (i, jj))
    o = _sds((t, f), BF16)
    return _pcall(body, name=name, grid=(t // tm, f // tn),
                  in_specs=[pl.BlockSpec((tm, k), lambda i, jj: (i, 0)), wspec, wspec],
                  out_specs=[ospec, ospec, ospec], out_shape=[o, o, o],
                  compiler_params=_cp(("parallel", "parallel")))(h, wgt, wut)


def _ffn_down_bwd(name, df, wd, gt, up):
    t, dm = df.shape
    f = wd.shape[0]
    tm, tn = _ffn_tiles(t, f)

    def body(df_ref, wd_ref, gt_ref, up_ref, dgt_ref, dup_ref):
        da = _dot(df_ref[...], wd_ref[...], NT)
        g = gt_ref[...].astype(F32)
        u = up_ref[...].astype(F32)
        s = _sigmoid(g)
        dgt_ref[...] = (da * u * (s * (1.0 + g * (1.0 - s)))).astype(BF16)
        dup_ref[...] = (da * (g * s)).astype(BF16)

    aspec = pl.BlockSpec((tm, tn), lambda i, jj: (i, jj))
    o = _sds((t, f), BF16)
    return _pcall(body, name=name, grid=(t // tm, f // tn),
                  in_specs=[pl.BlockSpec((tm, dm), lambda i, jj: (i, 0)),
                            pl.BlockSpec((tn, dm), lambda i, jj: (jj, 0)), aspec, aspec],
                  out_specs=[aspec, aspec], out_shape=[o, o],
                  compiler_params=_cp(("parallel", "parallel")))(df, wd, gt, up)


def _ffn_in_bwd(name, dgt, dup, wgt, wut):
    t, f = dgt.shape
    k = wgt.shape[1]
    tm = _mm_tile(t, 576)
    aspec = pl.BlockSpec((tm, f), lambda i, u, kk: (i, 0))
    wspec = pl.BlockSpec((f, k), lambda i, u, kk: (0, 0))
    return _mm(name, [(dgt, aspec, wgt, wspec), (dup, aspec, wut, wspec)], (t // tm, 1, 1), _sds((t, k), BF16),
               pl.BlockSpec((tm, k), lambda i, u, kk: (i, 0)), NN)


def _modrow(ref, row, is_ctx):
    return jnp.where(is_ctx, ref[0, row:row + 1, :], ref[1, row:row + 1, :])


def _rnm(name, x, delta, mod_g, g_row, mod_n, sh_row, sc_row, nw, nct, deps=()):
    t, dm = x.shape
    tm = ROW_TILE
    has = delta is not None
    nd = len(deps)

    def body(*refs):
        refs = refs[:len(refs) - nd - (2 if has else 1)] + refs[len(refs) - (2 if has else 1):]
        if has:
            x_ref, d_ref, mg_ref, m_ref, nw_ref, xo_ref, h_ref = refs
        else:
            x_ref, m_ref, nw_ref, h_ref = refs
        is_ctx = pl.program_id(0) < nct
        xv = x_ref[...]
        if has:
            xv = xv + _modrow(mg_ref, g_row, is_ctx) * d_ref[...]
            xo_ref[...] = xv
        r = lax.rsqrt(jnp.mean(xv * xv, axis=-1, keepdims=True) + EPS)
        hv = (xv * r * nw_ref[...]) * (1.0 + _modrow(m_ref, sc_row, is_ctx)) + _modrow(m_ref, sh_row, is_ctx)
        h_ref[...] = hv.astype(BF16)

    row = pl.BlockSpec((tm, dm), lambda i: (i, 0))
    ins = [x] + ([delta, mod_g] if has else []) + [mod_n, nw] + list(deps)
    in_specs = ([row] + ([row, _full(mod_g.shape)] if has else []) + [_full(mod_n.shape), _full(nw.shape)]
                + [_full(d.shape) for d in deps])
    outs = ([_sds((t, dm))] if has else []) + [_sds((t, dm), BF16)]
    out_specs = ([row] if has else []) + [row]
    res = _pcall(body, name=name, grid=(t // tm,), in_specs=in_specs, out_specs=out_specs, out_shape=outs,
                 compiler_params=_cp(("parallel",)))(*ins)
    return res if has else (None, res[0])


def _bnm(name, xn, dh, dup, yprev, mod_n, sh_row, sc_row, mod_g, g_row, nw, nct, deps=()):
    t, dm = xn.shape
    tm = ROW_TILE
    has = yprev is not None
    nd = len(deps)

    def body(*refs):
        nout = 3 if has else 2
        refs = refs[:len(refs) - nd - nout] + refs[len(refs) - nout:]
        if has:
            x_ref, dh_ref, du_ref, y_ref, mn_ref, mg_ref, nw_ref, dx_ref, dd_ref, s_ref = refs
        else:
            x_ref, dh_ref, du_ref, mn_ref, nw_ref, dx_ref, s_ref = refs
        i = pl.program_id(0)
        is_ctx = i < nct

        @pl.when(i == 0)
        def _():
            s_ref[...] = jnp.zeros_like(s_ref)

        xv = x_ref[...]
        r = lax.rsqrt(jnp.mean(xv * xv, axis=-1, keepdims=True) + EPS)
        xh = xv * r
        w = nw_ref[...]
        sc1 = 1.0 + _modrow(mn_ref, sc_row, is_ctx)
        dhv = dh_ref[...].astype(F32)
        dxh = dhv * sc1 * w
        dx = r * (dxh - xh * jnp.mean(dxh * xh, axis=-1, keepdims=True)) + du_ref[...]
        dx_ref[...] = dx
        parts = [_colsum(dhv), _colsum(dhv * (xh * w))]
        if has:
            dd_ref[...] = (_modrow(mg_ref, g_row, is_ctx) * dx).astype(BF16)
            parts.append(_colsum(dx * y_ref[...]))
        else:
            parts.append(jnp.zeros((1, dm), F32))
        upd = jnp.concatenate(parts + [jnp.zeros((5, dm), F32)], axis=0)
        dnw = jnp.concatenate([jnp.zeros((3, dm), F32), _colsum(dhv * sc1 * xh), jnp.zeros((4, dm), F32)], axis=0)

        @pl.when(is_ctx)
        def _():
            s_ref[0] += upd
            s_ref[1] += dnw

        @pl.when(jnp.logical_not(is_ctx))
        def _():
            s_ref[1] += upd + dnw

    row = pl.BlockSpec((tm, dm), lambda i: (i, 0))
    ins = [xn, dh, dup] + ([yprev] if has else []) + [mod_n] + ([mod_g] if has else []) + [nw] + list(deps)
    in_specs = ([row, row, row] + ([row] if has else []) + [_full(mod_n.shape)]
                + ([_full(mod_g.shape)] if has else []) + [_full(nw.shape)] + [_full(d.shape) for d in deps])
    if has:
        outs = [_sds((t, dm)), _sds((t, dm), BF16), _sds((2, 8, dm))]
        out_specs = [row, row, _full((2, 8, dm))]
    else:
        outs = [_sds((t - nct * tm, dm)), _sds((2, 8, dm))]
        out_specs = [pl.BlockSpec((tm, dm), lambda i: (jnp.maximum(i - nct, 0), 0)), _full((2, 8, dm))]
    res = _pcall(body, name=name, grid=(t // tm,), in_specs=in_specs, out_specs=out_specs, out_shape=outs,
                 compiler_params=_cp(("arbitrary",)))(*ins)
    return res if has else (res[0], None, res[1])


def _fin(name, x1, f, mod, g_row, fw, target, nct):
    t, dm = x1.shape
    tm = ROW_TILE

    def body(x_ref, f_ref, m_ref, fw_ref, t_ref, loss_ref, dx_ref, dd_ref, s_ref):
        i = pl.program_id(0)

        @pl.when(i == 0)
        def _():
            s_ref[...] = jnp.zeros_like(s_ref)
            loss_ref[...] = jnp.zeros_like(loss_ref)

        @pl.when(i < nct)
        def _():
            dx_ref[...] = jnp.zeros_like(dx_ref)
            dd_ref[...] = jnp.zeros_like(dd_ref)

        @pl.when(i >= nct)
        def _():
            g = m_ref[1, g_row:g_row + 1, :]
            fv = f_ref[...]
            xv = x_ref[...] + g * fv
            r = lax.rsqrt(jnp.mean(xv * xv, axis=-1, keepdims=True) + EPS)
            xh = xv * r
            w = fw_ref[...]
            err = xh * w - t_ref[...]
            loss_ref[...] += 0.5 * jnp.sum(err * err) / dm
            dout = err * (1.0 / dm)
            dxh = dout * w
            dx = r * (dxh - xh * jnp.mean(dxh * xh, axis=-1, keepdims=True))
            dx_ref[...] = dx
            dd_ref[...] = (g * dx).astype(BF16)
            s_ref[...] += jnp.concatenate([_colsum(dx * fv), _colsum(dout * xh), jnp.zeros((6, dm), F32)], axis=0)

    row = pl.BlockSpec((tm, dm), lambda i: (i, 0))
    trow = pl.BlockSpec((tm, dm), lambda i: (jnp.maximum(i - nct, 0), 0))
    return _pcall(body, name=name, grid=(t // tm,),
                  in_specs=[row, row, _full(mod.shape), _full(fw.shape), trow],
                  out_specs=[_full((8, 128)), row, row, _full((8, dm))],
                  out_shape=[_sds((8, 128)), _sds((t, dm)), _sds((t, dm), BF16), _sds((8, dm))],
                  compiler_params=_cp(("arbitrary",)))(x1, f, mod, fw, target)


def _rope_tables(t, lc):
    l = t - lc
    rows = l // GRID_W
    grid_r = jnp.broadcast_to(jnp.arange(rows, dtype=F32)[:, None], (rows, GRID_W)).reshape(-1)
    grid_c = jnp.broadcast_to(jnp.arange(GRID_W, dtype=F32)[None, :], (rows, GRID_W)).reshape(-1)

    def angles(p_seq, p_row, p_col):
        parts = []
        for p, n in zip((p_seq, p_row, p_col), ROPE_PAIRS):
            freq = ROPE_BASE ** (-jnp.arange(n, dtype=F32) / n)
            parts.append(p[:, None] * freq[None, :])
        return jnp.concatenate(parts, axis=-1)

    zc = jnp.zeros((lc,), F32)
    ang = jnp.concatenate([angles(jnp.arange(lc, dtype=F32), zc, zc),
                           angles(jnp.full((l,), lc, F32), grid_r, grid_c)], axis=0)
    cos, sin = jnp.cos(ang), jnp.sin(ang)
    return jnp.concatenate([cos, cos], axis=-1), jnp.concatenate([-sin, sin], axis=-1)


def _rope(u, cs, sn):
    return u * cs + pltpu.roll(u, HEAD_DIM // 2, 1) * sn


def _rope_t(d, cs, sn):
    return d * cs + pltpu.roll(d * sn, HEAD_DIM // 2, 1)


def _even_qkv(name, p, cs, sn):
    t = p.shape[0]
    tm = ROW_TILE
    w = HEADS * HEAD_DIM
    scale = HEAD_DIM ** -0.5

    def body(q_ref, k_ref, v_ref, cs_ref, sn_ref, qo_ref, ko_ref, vo_ref):
        c, s = cs_ref[...], sn_ref[...]
        for h in range(HEADS):
            sl = slice(h * HEAD_DIM, (h + 1) * HEAD_DIM)
            qo_ref[:, sl] = (_rope(q_ref[:, sl], c, s) * scale).astype(BF16)
            ko_ref[:, sl] = _rope(k_ref[:, sl], c, s).astype(BF16)
        vo_ref[...] = v_ref[...].astype(BF16)

    col = lambda j: pl.BlockSpec((tm, w), lambda i: (i, j))
    tab = pl.BlockSpec((tm, HEAD_DIM), lambda i: (i, 0))
    o = _sds((t, w), BF16)
    return _pcall(body, name=name, grid=(t // tm,), in_specs=[col(0), col(1), col(2), tab, tab],
                  out_specs=[col(0)] * 3, out_shape=[o, o, o], compiler_params=_cp(("parallel",)))(p, p, p, cs, sn)


def _log_sigmoid_row(x):
    e = jnp.exp(-jnp.abs(x))
    l1p = jnp.where(e < 0.01, e * (1.0 - e * (0.5 - e * (1.0 / 3.0))), jnp.log(1.0 + e))
    return jnp.minimum(x, 0.0) - l1p


def _ret_tables(lgb_ref, dm_ref, xi_ref, zt_ref):
    ri = lax.broadcasted_iota(jnp.int32, (CHUNK, CHUNK), 0).astype(F32)
    ci = lax.broadcasted_iota(jnp.int32, (CHUNK, CHUNK), 1).astype(F32)
    for d in range(2):
        for h in range(HEADS):
            idx = d * HEADS + h
            lg = _log_sigmoid_row(lgb_ref[idx:idx + 1, :])
            if d == 0:
                e, mask = ri - ci, ri >= ci
                xe, ze = ri + 1.0, (CHUNK - 1.0) - ri
            else:
                e, mask = ci - ri - 1.0, ci > ri
                xe, ze = (CHUNK - 1.0) - ri, ri
            dm_ref[idx] = jnp.where(mask, jnp.exp(lg * jnp.where(mask, e, 0.0)), 0.0)
            xi_ref[idx] = jnp.exp(lg * xe)
            zt_ref[idx] = jnp.exp(lg * ze)


def _ret_exponents(d):
    ri = lax.broadcasted_iota(jnp.int32, (CHUNK, CHUNK), 0).astype(F32)
    ci = lax.broadcasted_iota(jnp.int32, (CHUNK, CHUNK), 1).astype(F32)
    if d == 0:
        return ri - ci, ri + 1.0, (CHUNK - 1.0) - ri
    return ci - ri - 1.0, (CHUNK - 1.0) - ri, ri


RET_SUB = 2


def _bwd_chunk(n, ncc, nc):
    return jnp.where(n < ncc, ncc - 1 - n, nc - 1 - (n - ncc))


def _retention_fwd(name, q, k, v, lgb, lc):
    t, w = q.shape
    nc = t // CHUNK
    rows_per = RET_SUB * CHUNK
    nb, ncb = t // rows_per, lc // rows_per
    nh = 2 * HEADS

    def body(qf_ref, kf_ref, vf_ref, qb_ref, kb_ref, vb_ref, lgb_ref, of_ref, ob_ref, ss_ref,
             s_ref, dm_ref, xi_ref, zt_ref):
        n = pl.program_id(0)

        @pl.when(n == 0)
        def _():
            s_ref[...] = jnp.zeros_like(s_ref)
            _ret_tables(lgb_ref, dm_ref, xi_ref, zt_ref)

        where = []
        for u in range(RET_SUB):
            for d in range(2):
                refs = (qf_ref, kf_ref, vf_ref, of_ref) if d == 0 else (qb_ref, kb_ref, vb_ref, ob_ref)
                r0 = (u if d == 0 else RET_SUB - 1 - u) * CHUNK
                for h in range(HEADS):
                    where.append((u, d * HEADS + h, refs, slice(r0, r0 + CHUNK), slice(h * HEAD_DIM, (h + 1) * HEAD_DIM)))
        qs = [refs[0][rows, sl] for _, _, refs, rows, sl in where]
        ks = [refs[1][rows, sl] for _, _, refs, rows, sl in where]
        vs = [refs[2][rows, sl] for _, _, refs, rows, sl in where]
        sc = [_dot(qv, kv, NT) for qv, kv in zip(qs, ks)]
        upd = [_dot(kv.astype(F32) * zt_ref[idx], vv, TN) for (_, idx, *_), kv, vv in zip(where, ks, vs)]
        cur = [s_ref[idx] for idx in range(nh)]
        gcs = [jnp.exp(_log_sigmoid_row(lgb_ref[idx:idx + 1, :]) * float(CHUNK)) for idx in range(nh)]
        st = []
        for (u, idx, *_), du in zip(where, upd):
            st.append(cur[idx])
            ss_ref[u, idx] = cur[idx]
            cur[idx] = gcs[idx] * cur[idx] + du
        for idx in range(nh):
            s_ref[idx] = cur[idx]
        inter = [_dot(qv.astype(F32) * xi_ref[idx], s, NN) for (_, idx, *_), qv, s in zip(where, qs, st)]
        intra = [_dot(a * dm_ref[idx], vv, NN) for (_, idx, *_), a, vv in zip(where, sc, vs)]
        for (_, _, refs, rows, sl), o1, o2 in zip(where, intra, inter):
            refs[3][rows, sl] = o1 + o2

    fspec = pl.BlockSpec((rows_per, w), lambda n: (n, 0))
    bspec = pl.BlockSpec((rows_per, w), lambda n: (_bwd_chunk(n, ncb, nb), 0))
    tab = pltpu.VMEM((nh, CHUNK, CHUNK), F32)
    return _pcall(body, name=name, grid=(nb,),
                  in_specs=[fspec] * 3 + [bspec] * 3 + [_full((nh, HEAD_DIM))],
                  out_specs=[fspec, bspec, pl.BlockSpec((RET_SUB, nh, CHUNK, CHUNK), lambda n: (n, 0, 0, 0))],
                  out_shape=[_sds((t, w)), _sds((t, w)), _sds((nc, nh, CHUNK, CHUNK))],
                  scratch_shapes=[tab, tab, tab, tab],
                  compiler_params=_cp(("arbitrary",)))(q, k, v, q, k, v, lgb)


def _retention_bwd(name, q, k, v, do, ss, lgb, lc):
    t, w = q.shape
    rows_per = RET_SUB * CHUNK
    nb, ncb = t // rows_per, lc // rows_per
    nh = 2 * HEADS

    def body(qf_ref, kf_ref, vf_ref, gf_ref, qb_ref, kb_ref, vb_ref, gb_ref, ss_ref, lgb_ref,
             dqf_ref, dkf_ref, dvf_ref, dqb_ref, dkb_ref, dvb_ref, dl_ref,
             ds_ref, dm_ref, xi_ref, zt_ref, acc_ref):
        n = pl.program_id(0)

        @pl.when(n == 0)
        def _():
            ds_ref[...] = jnp.zeros_like(ds_ref)
            acc_ref[...] = jnp.zeros_like(acc_ref)
            _ret_tables(lgb_ref, dm_ref, xi_ref, zt_ref)

        where = []
        for u in reversed(range(RET_SUB)):
            for d in range(2):
                refs = ((qf_ref, kf_ref, vf_ref, gf_ref, dqf_ref, dkf_ref, dvf_ref) if d == 0
                        else (qb_ref, kb_ref, vb_ref, gb_ref, dqb_ref, dkb_ref, dvb_ref))
                r0 = (u if d == 0 else RET_SUB - 1 - u) * CHUNK
                for h in range(HEADS):
                    where.append((u, d * HEADS + h, d, refs, slice(r0, r0 + CHUNK), slice(h * HEAD_DIM, (h + 1) * HEAD_DIM)))
        qs = [refs[0][rows, sl] for *_, refs, rows, sl in where]
        ks = [refs[1][rows, sl] for *_, refs, rows, sl in where]
        vs = [refs[2][rows, sl] for *_, refs, rows, sl in where]
        gs = [refs[3][rows, sl] for *_, refs, rows, sl in where]
        st = [ss_ref[u, idx] for u, idx, *_ in where]
        sc = [_dot(qv, kv, NT) for qv, kv in zip(qs, ks)]
        dar = [_dot(gv, vv, NT) for gv, vv in zip(gs, vs)]
        t1 = [_dot(gv, s, NT) for gv, s in zip(gs, st)]
        dsn = [_dot(qv.astype(F32) * xi_ref[idx], gv, TN) for (_, idx, *_), qv, gv in zip(where, qs, gs)]
        cur = [ds_ref[idx] for idx in range(nh)]
        gcs = [jnp.exp(_log_sigmoid_row(lgb_ref[idx:idx + 1, :]) * float(CHUNK)) for idx in range(nh)]
        dsps = []
        for (_, idx, *_), x in zip(where, dsn):
            dsps.append(cur[idx])
            cur[idx] = gcs[idx] * cur[idx] + x
        for idx in range(nh):
            ds_ref[idx] = cur[idx]
        t2 = [_dot(vv, dsp, NT) for vv, dsp in zip(vs, dsps)]
        dv2 = [_dot(kv.astype(F32) * zt_ref[idx], dsp, NN) for (_, idx, *_), kv, dsp in zip(where, ks, dsps)]
        a = [x * dm_ref[idx] for (_, idx, *_), x in zip(where, sc)]
        da = [x * dm_ref[idx] for (_, idx, *_), x in zip(where, dar)]
        dq = [_dot(x, kv, NN) for x, kv in zip(da, ks)]
        dk = [_dot(x, qv, TN) for x, qv in zip(da, qs)]
        dv1 = [_dot(x, gv, TN) for x, gv in zip(a, gs)]
        for i8, (_, idx, d, refs, rows, sl) in enumerate(where):
            ee, xe, ze = _ret_exponents(d)
            xi, zt = xi_ref[idx], zt_ref[idx]
            qf32, kf32 = qs[i8].astype(F32), ks[i8].astype(F32)
            refs[4][rows, sl] = dq[i8] + xi * t1[i8]
            refs[5][rows, sl] = dk[i8] + zt * t2[i8]
            refs[6][rows, sl] = dv1[i8] + dv2[i8]
            acc_ref[idx] += (ee * a[i8] * dar[i8] + xe * xi * qf32 * t1[i8] + ze * zt * kf32 * t2[i8]
                             + (float(CHUNK) * gcs[idx]) * dsps[i8] * st[i8])

        @pl.when(n == nb - 1)
        def _():
            for idx in range(nh):
                tot = jnp.sum(acc_ref[idx])
                dl_ref[idx:idx + 1, :] = tot * _sigmoid(-lgb_ref[idx:idx + 1, :])

    fmap = lambda n: (nb - 1 - n, 0)
    bmap = lambda n: (_bwd_chunk(nb - 1 - n, ncb, nb), 0)
    fspec = pl.BlockSpec((rows_per, w), fmap)
    bspec = pl.BlockSpec((rows_per, w), bmap)
    tab = pltpu.VMEM((nh, CHUNK, CHUNK), F32)
    o = _sds((t, w))
    return _pcall(body, name=name, grid=(nb,),
                  in_specs=[fspec] * 4 + [bspec] * 4
                  + [pl.BlockSpec((RET_SUB, nh, CHUNK, CHUNK), lambda n: (nb - 1 - n, 0, 0, 0)), _full((nh, HEAD_DIM))],
                  out_specs=[fspec] * 3 + [bspec] * 3 + [_full((nh, HEAD_DIM))],
                  out_shape=[o] * 6 + [_sds((nh, HEAD_DIM))],
                  scratch_shapes=[tab, tab, tab, tab, tab],
                  compiler_params=_cp(("arbitrary",)))(q, k, v, do, q, k, v, do, ss, lgb)


def _halo_specs(tm, halo, t, width, col):
    hb = tm // halo
    last = t // halo - 1
    prev = pl.BlockSpec((halo, width), lambda i: (jnp.maximum(i * hb - 1, 0), col))
    nxt = pl.BlockSpec((halo, width), lambda i: (jnp.minimum((i + 1) * hb, last), col))
    return prev, nxt


def _halo_valid(i, nct, nt):
    vp = jnp.logical_and(i != 0, i != nct)
    vn = jnp.logical_and(i != nct - 1, i != nt - 1)
    return vp, vn


def _fill_window(win_ref, prev, cur, nxt, vp, vn, halo, tm):
    win_ref[0:halo, :] = jnp.where(vp, prev, 0.0)
    win_ref[halo:halo + tm, :] = cur
    win_ref[halo + tm:halo + tm + halo, :] = jnp.where(vn, nxt, 0.0)


CONV_SUB = 64


SUBLANES = 8


def _shift_window(win_ref, sh_ref, tm):
    rows = tm + 2 * CONV_HALO - SUBLANES
    for s in range(SUBLANES):
        sh_ref[s, 0:rows, :] = win_ref[s:s + rows, :]


def _window_rows(sh_ref, start, rows):
    s = start % SUBLANES
    return sh_ref[s, start - s:start - s + rows, :]


def _conv_taps(sh_ref, w_ref, tm, flip):
    outs = []
    for r0 in range(0, tm, CONV_SUB):
        acc = None
        for kk in range(CONV_K):
            wk = (CONV_K - 1 - kk) if flip else kk
            term = w_ref[wk:wk + 1, :] * _window_rows(sh_ref, r0 + kk + 1, CONV_SUB)
            acc = term if acc is None else acc + term
        outs.append(acc)
    return jnp.concatenate(outs, axis=0)


def _head_norm(y):
    r = lax.rsqrt(jnp.mean(y * y, axis=-1, keepdims=True) + EPS)
    return y * r, r


def _ln_stats(y):
    mu = jnp.mean(y, axis=-1, keepdims=True)
    yc = y - mu
    rs = lax.rsqrt(jnp.mean(yc * yc, axis=-1, keepdims=True) + EPS)
    return yc * rs, rs


def _ln_bwd(dyh, yh, rs):
    return rs * (dyh - jnp.mean(dyh, axis=-1, keepdims=True) - yh * jnp.mean(dyh * yh, axis=-1, keepdims=True))


def _even_mix(name, p, of, ob, cw, lnw, lnb, nct):
    t = p.shape[0]
    tm, halo = ROW_TILE, CONV_HALO
    nt = t // tm
    w = HEADS * HEAD_DIM

    def body(g_ref, a_ref, gb_ref, ap_ref, gbp_ref, an_ref, gbn_ref, of_ref, ob_ref, cw_ref, lw_ref, lb_ref,
             mix_ref, yc_ref, win_ref, sh_ref):
        i = pl.program_id(0)
        vp, vn = _halo_valid(i, nct, nt)
        glu = lambda a, b: a * _sigmoid(b)
        _fill_window(win_ref, glu(ap_ref[...], gbp_ref[...]), glu(a_ref[...], gb_ref[...]),
                     glu(an_ref[...], gbn_ref[...]), vp, vn, halo, tm)
        _shift_window(win_ref, sh_ref, tm)
        yc = _conv_taps(sh_ref, cw_ref, tm, False)
        yc_ref[...] = yc
        yh, _ = _ln_stats(yc)
        mix_ref[:, w:2 * w] = _silu(yh * lw_ref[...] + lb_ref[...]).astype(BF16)
        for h in range(HEADS):
            sl = slice(h * HEAD_DIM, (h + 1) * HEAD_DIM)
            yn, _ = _head_norm(of_ref[:, sl] + ob_ref[:, sl])
            mix_ref[:, sl] = (_silu(g_ref[:, sl]) * yn).astype(BF16)

    col = lambda j: pl.BlockSpec((tm, w), lambda i: (i, j))
    ap, an = _halo_specs(tm, halo, t, w, 4)
    gp, gn = _halo_specs(tm, halo, t, w, 5)
    row = pl.BlockSpec((tm, w), lambda i: (i, 0))
    return _pcall(body, name=name, grid=(nt,),
                  in_specs=[col(3), col(4), col(5), ap, gp, an, gn, row, row,
                            _full(cw.shape), _full(lnw.shape), _full(lnb.shape)],
                  out_specs=[pl.BlockSpec((tm, 2 * w), lambda i: (i, 0)), row],
                  out_shape=[_sds((t, 2 * w), BF16), _sds((t, w))],
                  scratch_shapes=[pltpu.VMEM((tm + 2 * halo, w), F32), pltpu.VMEM((SUBLANES, tm + 2 * halo, w), F32)],
                  compiler_params=_cp(("parallel",)))(p, p, p, p, p, p, p, of, ob, cw, lnw, lnb)


def _even_mix_bwd1(name, dmix, p, of, ob, yc, lnw, lnb):
    t = p.shape[0]
    tm = ROW_TILE
    w = HEADS * HEAD_DIM

    def body(dr_ref, dc_ref, g_ref, of_ref, ob_ref, yc_ref, lw_ref, lb_ref, do_ref, dg_ref, dyc_ref, s_ref):
        @pl.when(pl.program_id(0) == 0)
        def _():
            s_ref[...] = jnp.zeros_like(s_ref)

        for h in range(HEADS):
            sl = slice(h * HEAD_DIM, (h + 1) * HEAD_DIM)
            yn, r = _head_norm(of_ref[:, sl] + ob_ref[:, sl])
            gv = g_ref[:, sl]
            dr = dr_ref[:, sl]
            dg_ref[:, sl] = (dr * yn * _dsilu(gv)).astype(BF16)
            dyn = dr * _silu(gv)
            do_ref[:, sl] = (r * (dyn - yn * jnp.mean(dyn * yn, axis=-1, keepdims=True))).astype(BF16)
        yh, rs = _ln_stats(yc_ref[...])
        lw = lw_ref[...]
        dlo = dc_ref[...] * _dsilu(yh * lw + lb_ref[...])
        dyc_ref[...] = _ln_bwd(dlo * lw, yh, rs)
        s_ref[...] += jnp.concatenate([_colsum(dlo * yh), _colsum(dlo), jnp.zeros((6, w), F32)], axis=0)

    col = lambda j: pl.BlockSpec((tm, w), lambda i: (i, j))
    row = pl.BlockSpec((tm, w), lambda i: (i, 0))
    return _pcall(body, name=name, grid=(t // tm,),
                  in_specs=[col(0), col(1), col(3), row, row, row, _full(lnw.shape), _full(lnb.shape)],
                  out_specs=[row, row, row, _full((8, w))],
                  out_shape=[_sds((t, w), BF16), _sds((t, w), BF16), _sds((t, w)), _sds((8, w))],
                  compiler_params=_cp(("arbitrary",)))(dmix, dmix, p, of, ob, yc, lnw, lnb)


def _even_conv_bwd(name, dyc, p, cw, nct):
    t = p.shape[0]
    tm, halo = ROW_TILE, CONV_HALO
    nt = t // tm
    w = HEADS * HEAD_DIM

    def body(d_ref, dp_ref, dn_ref, a_ref, gb_ref, ap_ref, gbp_ref, an_ref, gbn_ref, cw_ref,
             da_ref, dgb_ref, dw_ref, dwin_ref, uwin_ref, dsh_ref, ush_ref):
        i = pl.program_id(0)

        @pl.when(i == 0)
        def _():
            dw_ref[...] = jnp.zeros_like(dw_ref)

        vp, vn = _halo_valid(i, nct, nt)
        glu = lambda a, b: a * _sigmoid(b)
        dcur = d_ref[...]
        _fill_window(dwin_ref, dp_ref[...], dcur, dn_ref[...], vp, vn, halo, tm)
        _fill_window(uwin_ref, glu(ap_ref[...], gbp_ref[...]), glu(a_ref[...], gb_ref[...]),
                     glu(an_ref[...], gbn_ref[...]), vp, vn, halo, tm)
        _shift_window(dwin_ref, dsh_ref, tm)
        _shift_window(uwin_ref, ush_ref, tm)
        du = _conv_taps(dsh_ref, cw_ref, tm, True)
        av = a_ref[...]
        sg = _sigmoid(gb_ref[...])
        da_ref[...] = (du * sg).astype(BF16)
        dgb_ref[...] = (du * av * sg * (1.0 - sg)).astype(BF16)
        rows = [_colsum(dcur * _window_rows(ush_ref, kk + 1, tm)) for kk in range(CONV_K)]
        dw_ref[...] += jnp.concatenate(rows + [jnp.zeros((1, w), F32)], axis=0)

    col = lambda j: pl.BlockSpec((tm, w), lambda i: (i, j))
    row = pl.BlockSpec((tm, w), lambda i: (i, 0))
    dp, dn = _halo_specs(tm, halo, t, w, 0)
    ap, an = _halo_specs(tm, halo, t, w, 4)
    gp, gn = _halo_specs(tm, halo, t, w, 5)
    win = pltpu.VMEM((tm + 2 * halo, w), F32)
    shifted = pltpu.VMEM((SUBLANES, tm + 2 * halo, w), F32)
    return _pcall(body, name=name, grid=(nt,),
                  in_specs=[row, dp, dn, col(4), col(5), ap, gp, an, gn, _full(cw.shape)],
                  out_specs=[row, row, _full((CONV_K + 1, w))],
                  out_shape=[_sds((t, w), BF16), _sds((t, w), BF16), _sds((CONV_K + 1, w))],
                  scratch_shapes=[win, win, shifted, shifted],
                  compiler_params=_cp(("arbitrary",)))(dyc, dyc, dyc, p, p, p, p, p, p, cw)


def _even_dp(name, dqs, dks, dvs, dg, da, dgb, cs, sn):
    t, w = dg.shape
    tm = ROW_TILE
    scale = HEAD_DIM ** -0.5

    def body(dqf_ref, dqb_ref, dkf_ref, dkb_ref, dvf_ref, dvb_ref, dg_ref, da_ref, dgb_ref, cs_ref, sn_ref, dp_ref):
        c, s = cs_ref[...], sn_ref[...]
        for h in range(HEADS):
            sl = slice(h * HEAD_DIM, (h + 1) * HEAD_DIM)
            dp_ref[:, sl] = (_rope_t(dqf_ref[:, sl] + dqb_ref[:, sl], c, s) * scale).astype(BF16)
            dp_ref[:, w + h * HEAD_DIM:w + (h + 1) * HEAD_DIM] = _rope_t(dkf_ref[:, sl] + dkb_ref[:, sl], c, s).astype(BF16)
        dp_ref[:, 2 * w:3 * w] = (dvf_ref[...] + dvb_ref[...]).astype(BF16)
        dp_ref[:, 3 * w:4 * w] = dg_ref[...]
        dp_ref[:, 4 * w:5 * w] = da_ref[...]
        dp_ref[:, 5 * w:6 * w] = dgb_ref[...]

    row = pl.BlockSpec((tm, w), lambda i: (i, 0))
    tab = pl.BlockSpec((tm, HEAD_DIM), lambda i: (i, 0))
    return _pcall(body, name=name, grid=(t // tm,), in_specs=[row] * 9 + [tab, tab],
                  out_specs=pl.BlockSpec((tm, 6 * w), lambda i: (i, 0)), out_shape=_sds((t, 6 * w), BF16),
                  compiler_params=_cp(("parallel",)))(dqs[0], dqs[1], dks[0], dks[1], dvs[0], dvs[1], dg, da, dgb, cs, sn)


GROUPS = 4
GC = 128
INV_SQRT2 = 0.7071067811865476
INV_SQRT_2PI = 0.3989422804014327


def _gelu(x):
    return 0.5 * x * (1.0 + lax.erf(x * INV_SQRT2))


def _dgelu(x):
    return 0.5 * (1.0 + lax.erf(x * INV_SQRT2)) + x * jnp.exp(-0.5 * x * x) * INV_SQRT_2PI


def _pool_count(i, nct, lc, t, tm, rows, row0, left, right):
    is_ctx = i < nct
    seg_start = jnp.where(is_ctx, 0, lc)
    seg_len = jnp.where(is_ctx, lc, t - lc)
    pos = i * tm + row0 - seg_start + lax.broadcasted_iota(jnp.int32, (rows, GC), 0)
    cnt = jnp.minimum(pos + right, seg_len - 1) - jnp.maximum(pos - left, 0) + 1
    return jnp.maximum(cnt, 1).astype(F32)


def _spatial_gate(vln, sgw_ref, sgb_ref, tm):
    cols = []
    for g in range(GROUPS):
        sl = slice(g * GC, (g + 1) * GC)
        parts = [_dot(sgw_ref[g], vln[r0:r0 + CHUNK, sl], NN) + sgb_ref[g] for r0 in range(0, tm, CHUNK)]
        cols.append(jnp.concatenate(parts, axis=0))
    return jnp.concatenate(cols, axis=1)


def _odd_mix(name, p, pw, pscale, lnw, lnb, sgw, sgb, nct, lc):
    t = p.shape[0]
    tm, halo = ROW_TILE, POOL_HALO
    nt = t // tm
    w = GROUPS * GC

    def body(pc_ref, pp_ref, pn_ref, pu_ref, pv_ref, pw_ref, ps_ref, lw_ref, lb_ref, sgw_ref, sgb_ref,
             mix_ref, m_ref, win_ref):
        i = pl.program_id(0)
        vp, vn = _halo_valid(i, nct, nt)
        pc = pc_ref[...]
        _fill_window(win_ref, pp_ref[...], pc, pn_ref[...], vp, vn, halo, tm)
        for g, wd in enumerate(POOL_WINDOWS):
            sl = slice(g * GC, (g + 1) * GC)
            left = wd // 2
            right = wd - 1 - left
            s = None
            for o in range(-left, right + 1):
                term = win_ref[halo + o:halo + o + tm, sl]
                s = term if s is None else s + term
            mg = s / _pool_count(i, nct, lc, t, tm, tm, 0, left, right) - pc[:, sl]
            m_ref[:, sl] = mg
            mix_ref[:, sl] = (_dot(mg, pw_ref[g], NN) * ps_ref[:, sl]).astype(BF16)
        u = _gelu(pu_ref[...])
        vh, _ = _ln_stats(_gelu(pv_ref[...]))
        s = _spatial_gate(vh * lw_ref[...] + lb_ref[...], sgw_ref, sgb_ref, tm)
        mix_ref[:, w:2 * w] = (u * s).astype(BF16)

    col = lambda j: pl.BlockSpec((tm, w), lambda i: (i, j))
    pp, pn = _halo_specs(tm, halo, t, w, 0)
    return _pcall(body, name=name, grid=(nt,),
                  in_specs=[col(0), pp, pn, col(1), col(2), _full(pw.shape), _full(pscale.shape),
                            _full(lnw.shape), _full(lnb.shape), _full(sgw.shape), _full(sgb.shape)],
                  out_specs=[pl.BlockSpec((tm, 2 * w), lambda i: (i, 0)), col(0)],
                  out_shape=[_sds((t, 2 * w), BF16), _sds((t, w))],
                  scratch_shapes=[pltpu.VMEM((tm + 2 * halo, w), F32)],
                  compiler_params=_cp(("parallel",)))(p, p, p, p, p, pw, pscale, lnw, lnb, sgw, sgb)


def _odd_mix_bwd1(name, dmix, p, m, pw, pscale, lnw, lnb, sgw, sgb):
    t = p.shape[0]
    tm = ROW_TILE
    w = GROUPS * GC

    def body(dpo_ref, dsg_ref, pu_ref, pv_ref, m_ref, pw_ref, ps_ref, lw_ref, lb_ref, sgw_ref, sgb_ref,
             dm_ref, dpd_ref, vec_ref, dpw_ref, dsgw_ref, dsgb_ref):
        @pl.when(pl.program_id(0) == 0)
        def _():
            vec_ref[...] = jnp.zeros_like(vec_ref)
            dpw_ref[...] = jnp.zeros_like(dpw_ref)
            dsgw_ref[...] = jnp.zeros_like(dsgw_ref)
            dsgb_ref[...] = jnp.zeros_like(dsgb_ref)

        dscale = []
        for g in range(GROUPS):
            sl = slice(g * GC, (g + 1) * GC)
            mg = m_ref[:, sl]
            dpo = dpo_ref[:, sl]
            dscale.append(_colsum(dpo * _dot(mg, pw_ref[g], NN)))
            dpo = dpo * ps_ref[:, sl]
            dm_ref[:, sl] = _dot(dpo, pw_ref[g], NT)
            dpw_ref[g] += _dot(mg, dpo, TN)
        pu, pv = pu_ref[...], pv_ref[...]
        u = _gelu(pu)
        vh, rs = _ln_stats(_gelu(pv))
        lw = lw_ref[...]
        vln = vh * lw + lb_ref[...]
        s = _spatial_gate(vln, sgw_ref, sgb_ref, tm)
        dsg = dsg_ref[...]
        dpd_ref[:, 0:w] = (dsg * s * _dgelu(pu)).astype(BF16)
        ds = dsg * u
        cols = []
        for g in range(GROUPS):
            sl = slice(g * GC, (g + 1) * GC)
            parts = []
            for r0 in range(0, tm, CHUNK):
                dsc = ds[r0:r0 + CHUNK, sl]
                parts.append(_dot(sgw_ref[g], dsc, TN))
                dsgw_ref[g] += _dot(dsc, vln[r0:r0 + CHUNK, sl], NT)
                dsgb_ref[g] += dsc
            cols.append(jnp.concatenate(parts, axis=0))
        dvln = jnp.concatenate(cols, axis=1)
        dpd_ref[:, w:2 * w] = (_ln_bwd(dvln * lw, vh, rs) * _dgelu(pv)).astype(BF16)
        vec_ref[...] += jnp.concatenate([jnp.concatenate(dscale, axis=1), _colsum(dvln * vh), _colsum(dvln),
                                         jnp.zeros((5, w), F32)], axis=0)

        @pl.when(pl.program_id(0) == t // tm - 1)
        def _():
            for g in range(GROUPS):
                dsgb_ref[g] = jnp.broadcast_to(jnp.sum(dsgb_ref[g], axis=1, keepdims=True), (GC, GC))

    col = lambda j: pl.BlockSpec((tm, w), lambda i: (i, j))
    mat = _full((GROUPS, GC, GC))
    return _pcall(body, name=name, grid=(t // tm,),
                  in_specs=[col(0), col(1), col(1), col(2), col(0), _full(pw.shape), _full(pscale.shape),
                            _full(lnw.shape), _full(lnb.shape), _full(sgw.shape), _full(sgb.shape)],
                  out_specs=[col(0), pl.BlockSpec((tm, 2 * w), lambda i: (i, 0)), _full((8, w)), mat, mat, mat],
                  out_shape=[_sds((t, w)), _sds((t, 2 * w), BF16), _sds((8, w)),
                             _sds((GROUPS, GC, GC)), _sds((GROUPS, GC, GC)), _sds((GROUPS, GC, GC))],
                  compiler_params=_cp(("arbitrary",)))(dmix, dmix, p, p, m, pw, pscale, lnw, lnb, sgw, sgb)


def _odd_dp(name, dm, dpd, nct, lc):
    t, w = dm.shape
    tm, halo = ROW_TILE, POOL_HALO
    nt = t // tm

    def body(d_ref, dp_ref, dn_ref, dpd_ref, o_ref, win_ref):
        i = pl.program_id(0)
        vp, vn = _halo_valid(i, nct, nt)
        dcur = d_ref[...]
        _fill_window(win_ref, dp_ref[...], dcur, dn_ref[...], vp, vn, halo, tm)
        for g, wd in enumerate(POOL_WINDOWS):
            sl = slice(g * GC, (g + 1) * GC)
            left = wd // 2
            right = wd - 1 - left
            win_ref[:, sl] = win_ref[:, sl] / _pool_count(i, nct, lc, t, tm, tm + 2 * halo, -halo, left, right)
            s = None
            for o in range(-right, left + 1):
                term = win_ref[halo + o:halo + o + tm, sl]
                s = term if s is None else s + term
            o_ref[:, sl] = (s - dcur[:, sl]).astype(BF16)
        o_ref[:, w:3 * w] = dpd_ref[...]

    row = pl.BlockSpec((tm, w), lambda i: (i, 0))
    pp, pn = _halo_specs(tm, halo, t, w, 0)
    return _pcall(body, name=name, grid=(nt,),
                  in_specs=[row, pp, pn, pl.BlockSpec((tm, 2 * w), lambda i: (i, 0))],
                  out_specs=pl.BlockSpec((tm, 3 * w), lambda i: (i, 0)), out_shape=_sds((t, 3 * w), BF16),
                  scratch_shapes=[pltpu.VMEM((tm + 2 * halo, w), F32)],
                  compiler_params=_cp(("parallel",)))(dm, dm, dm, dpd)


def _place():
    x, y, c = lax.axis_index("x"), lax.axis_index("y"), lax.axis_index("c")
    chips = [(1 - x, y), (x, 1 - y), (1 - x, 1 - y)]
    return x, y, c, chips


def _chip_index(cx, cy):
    return 2 * cx + cy


def _all_gather8(name, blk, after=()):
    m_per, n = blk.shape
    na = len(after)

    def body(x_ref, *rest):
        out_ref, send_sems, recv_sems, local_sem = rest[na:]
        x, y, c, chips = _place()
        me, sibling = (x, y, c), (x, y, 1 - c)

        def rows(px, py, pc):
            return out_ref.at[pl.ds((4 * px + 2 * py + pc) * m_per, m_per), :]

        def copy(k, block, to, src=None):
            return pltpu.make_async_remote_copy(
                src_ref=rows(*block) if src is None else src, dst_ref=rows(*block),
                send_sem=send_sems.at[k], recv_sem=recv_sems.at[k], device_id=to, device_id_type=MESH)

        mine = pltpu.make_async_copy(x_ref, rows(*me), local_sem)
        mine.start()
        first = [copy(0, me, sibling, src=x_ref)]
        first += [copy(1 + j, me, (*chip, c), src=x_ref) for j, chip in enumerate(chips)]
        for cp in first:
            cp.start()
        passed = [copy(4 + j, (*chip, c), sibling) for j, chip in enumerate(chips)]
        for j, chip in enumerate(chips):
            copy(1 + j, (*chip, c), me).wait_recv()
            passed[j].start()
        copy(0, sibling, me).wait_recv()
        for j, chip in enumerate(chips):
            copy(4 + j, (*chip, 1 - c), me).wait_recv()
        for cp in first + passed:
            cp.wait_send()
        mine.wait()

    return _pcall(body, name=name, out_shape=_sds((8 * m_per, n), blk.dtype),
                  in_specs=[pl.BlockSpec(memory_space=pltpu.VMEM)] + [pl.BlockSpec(memory_space=pl.ANY)] * na,
                  out_specs=pl.BlockSpec(memory_space=pltpu.VMEM),
                  scratch_shapes=[pltpu.SemaphoreType.DMA((7,)), pltpu.SemaphoreType.DMA((7,)), pltpu.SemaphoreType.DMA],
                  compiler_params=_cp())(blk, *after)


ANY = pl.BlockSpec(memory_space=pl.ANY)


def _half(which, rows):
    return pl.ds(pl.multiple_of(which * rows, 16), rows)


def _gather_weights(name, ws, after=()):
    nw = len(ws)
    na = len(after)
    ns = 7

    def body(*refs):
        w_refs, o_refs = refs[:nw], refs[nw + na:2 * nw + na]
        send_sems, recv_sems = refs[2 * nw + na:]
        x, y, c, chips = _place()
        me_chip = _chip_index(x, y)
        sibling = (x, y, 1 - c)

        def rcopy(t, k, src, dst, to):
            return pltpu.make_async_remote_copy(src_ref=src, dst_ref=dst, send_sem=send_sems.at[t * ns + k],
                                                recv_sem=recv_sems.at[t * ns + k], device_id=to, device_id_type=MESH)

        sends = []
        for t in range(nw):
            lh = w_refs[t].shape[0] // 2
            for k, chip in enumerate(chips):
                sends.append(rcopy(t, k, w_refs[t].at[_half(c, lh)], o_refs[t].at[me_chip, _half(c, lh)], (*chip, c)))
                sends[-1].start()
            sends.append(rcopy(t, 6, w_refs[t], o_refs[t].at[me_chip], sibling))
            sends[-1].start()
        for t in range(nw):
            lh = w_refs[t].shape[0] // 2
            for k, chip in enumerate(chips):
                part = o_refs[t].at[_chip_index(*chip), _half(c, lh)]
                rcopy(t, k, part, part, (*chip, c)).wait_recv()
                sends.append(rcopy(t, 3 + k, part, part, sibling))
                sends[-1].start()
        for t in range(nw):
            lh = w_refs[t].shape[0] // 2
            own = o_refs[t].at[me_chip]
            rcopy(t, 6, own, own, sibling).wait_recv()
            for k, chip in enumerate(chips):
                part = o_refs[t].at[_chip_index(*chip), _half(1 - c, lh)]
                rcopy(t, 3 + k, part, part, sibling).wait_recv()
        for cp in sends:
            cp.wait_send()

    return _pcall(body, name=name, out_shape=[_sds((4,) + w.shape, w.dtype) for w in ws],
                  in_specs=[ANY] * (nw + na), out_specs=[ANY] * nw,
                  scratch_shapes=[pltpu.SemaphoreType.DMA((ns * nw,)), pltpu.SemaphoreType.DMA((ns * nw,))],
                  compiler_params=_cp())(*ws, *after)


def _rs_share(name, ss):
    ng = len(ss)

    def body(*refs):
        o_refs = refs[ng:2 * ng]
        send_sems, recv_sems = refs[2 * ng:]
        x, y, c, _ = _place()
        cps = []
        for t in range(ng):
            lh = o_refs[t].shape[1] // 2
            mine = o_refs[t].at[:, _half(c, lh)]
            cp = pltpu.make_async_remote_copy(
                src_ref=mine, dst_ref=mine, send_sem=send_sems.at[t], recv_sem=recv_sems.at[t],
                device_id=(x, y, 1 - c), device_id_type=MESH)
            cp.start()
            cps.append(cp)
        for t in range(ng):
            lh = o_refs[t].shape[1] // 2
            cps[t].wait_send()
            theirs = o_refs[t].at[:, _half(1 - c, lh)]
            pltpu.make_async_remote_copy(
                src_ref=theirs, dst_ref=theirs, send_sem=send_sems.at[t], recv_sem=recv_sems.at[t],
                device_id=(x, y, 1 - c), device_id_type=MESH).wait_recv()

    return _pcall(body, name=name, out_shape=[_sds(s.shape, s.dtype) for s in ss],
                  in_specs=[ANY] * ng, out_specs=[ANY] * ng, input_output_aliases={t: t for t in range(ng)},
                  scratch_shapes=[pltpu.SemaphoreType.DMA((ng,)), pltpu.SemaphoreType.DMA((ng,))],
                  compiler_params=_cp())(*ss)


HBM = pl.BlockSpec(memory_space=pltpu.HBM)
SEMS = pl.BlockSpec(memory_space=pltpu.SEMAPHORE)
EFFECT = pltpu.SideEffectType.DATAFLOW_SIDE_EFFECTING
TOKEN = (8, 128)


def _in_hbm(a):
    return pltpu.with_memory_space_constraint(a, pltpu.HBM)


def _split_start(name, srcs, lands, copies, after):
    ns, nl, na = len(srcs), len(lands), len(after)
    ncopies = len(copies([s for s in srcs], [l for l in lands], probe=True))

    def body(*refs):
        src_refs, land_refs = refs[:ns], refs[ns:ns + nl]
        send_sems, recv_sems = refs[ns + nl + na], refs[ns + nl + na + 1]
        token = refs[-1]
        for k, (src, dst, to) in enumerate(copies(src_refs, land_refs)):
            pltpu.make_async_remote_copy(src_ref=src, dst_ref=dst, send_sem=send_sems.at[k], recv_sem=recv_sems.at[k],
                                         device_id=to, device_id_type=MESH).start()
        token[...] = jnp.zeros_like(token)

    thru = [pltpu.HBM(a.shape, a.dtype) for a in list(srcs) + list(lands)]
    outs = _pcall(body, name=name,
                  out_shape=(pltpu.SemaphoreType.DMA((ncopies,)), pltpu.SemaphoreType.DMA((ncopies,)), *thru, _sds(TOKEN)),
                  in_specs=[HBM] * (ns + nl) + [ANY] * na,
                  out_specs=(SEMS, SEMS, *([HBM] * (ns + nl)), pl.BlockSpec(memory_space=pltpu.VMEM)),
                  input_output_aliases={t: 2 + t for t in range(ns + nl)},
                  compiler_params=pltpu.CompilerParams(has_side_effects=EFFECT))(
        *[_in_hbm(a) for a in list(srcs) + list(lands)], *after)
    return outs[0], outs[1], list(outs[2:2 + ns]), list(outs[2 + ns:2 + ns + nl]), outs[-1]


def _split_wait(name, started, copies, after):
    send_sems, recv_sems, srcs, lands, _ = started
    ns, nl, na = len(srcs), len(lands), len(after)

    def body(*refs):
        src_refs, land_refs = refs[:ns], refs[ns:ns + nl]
        send_sems_ref, recv_sems_ref = refs[ns + nl], refs[ns + nl + 1]
        for k, (src, dst, to) in enumerate(copies(src_refs, land_refs)):
            cp = pltpu.make_async_remote_copy(src_ref=src, dst_ref=dst, send_sem=send_sems_ref.at[k],
                                              recv_sem=recv_sems_ref.at[k], device_id=to, device_id_type=MESH)
            cp.wait_send()
            cp.wait_recv()

    thru = [pltpu.HBM(a.shape, a.dtype) for a in list(srcs) + list(lands)]
    outs = _pcall(body, name=name, out_shape=tuple(thru),
                  in_specs=[HBM] * (ns + nl) + [SEMS, SEMS] + [ANY] * na, out_specs=tuple([HBM] * (ns + nl)),
                  input_output_aliases={t: t for t in range(ns + nl)},
                  compiler_params=pltpu.CompilerParams(has_side_effects=EFFECT))(
        *srcs, *lands, send_sems, recv_sems, *after)
    return list(outs[:ns]), list(outs[ns:])


def _pair_copies(n):
    def copies(src_refs, land_refs, probe=False):
        if probe:
            return [None] * n
        x, y, c, _ = _place()
        return [(src_refs[t].at[:, _half(1 - c, src_refs[t].shape[1] // 2)], land_refs[t], (x, y, 1 - c))
                for t in range(n)]
    return copies


def _gather_copies(n):
    def copies(src_refs, land_refs, probe=False):
        if probe:
            return [None] * (4 * n)
        x, y, c, chips = _place()
        me_chip = _chip_index(x, y)
        out = []
        for t in range(n):
            for to in [(*chip, c) for chip in chips] + [(x, y, 1 - c)]:
                out.append((src_refs[t], land_refs[t].at[me_chip], to))
        return out
    return copies


def _scatter_copies(n):
    def copies(src_refs, land_refs, probe=False):
        if probe:
            return [None] * (3 * n)
        x, y, c, chips = _place()
        out = []
        for t in range(n):
            for k, chip in enumerate(chips):
                out.append((src_refs[t].at[_chip_index(*chip)], land_refs[t].at[k], (*chip, c)))
        return out
    return copies


def _row_block(r, cn):
    if r % 8:
        return r
    best = 8
    for d in range(8, r + 1, 8):
        if r % d == 0 and d * cn * 4 <= (2 << 20):
            best = d
    return best


def _add_half(name, gl, al, idx):
    n = len(gl)
    j, rh, cn = al[0].shape
    tr = _row_block(rh, cn)
    nb = rh // tr

    def body(i_ref, *refs):
        for t in range(n):
            refs[2 * n + t][...] = (refs[t][...] + refs[n + t][...]).astype(BF16)

    blk = (None, tr, cn)
    gspec = pl.BlockSpec(blk, lambda jj, i, i_ref: (jj, i_ref[0] * nb + i, 0))
    aspec = pl.BlockSpec(blk, lambda jj, i, i_ref: (jj, i, 0))
    gs = pltpu.PrefetchScalarGridSpec(num_scalar_prefetch=1, grid=(j, nb), in_specs=[gspec] * n + [aspec] * n,
                                      out_specs=[aspec] * n)
    return _pcall(body, name=name, grid_spec=gs, out_shape=[_sds(al[0].shape, BF16)] * n,
                  compiler_params=_cp(("parallel", "parallel")))(idx, *gl, *al)


def _sum_final(name, gl, al, bl, idx, bufs, lyr, nlyr):
    n = len(gl)
    _, r, cn = gl[0].shape
    rh = r // 2
    tr = _row_block(rh, cn)
    nb = rh // tr
    has = bufs[0] is not None

    def body(i_ref, *refs):
        outs = refs[len(refs) - n:]
        for t in range(n):
            own = refs[t][...] + refs[n + t][...]
            b_ref = refs[2 * n + t]
            outs[t][...] = (own + b_ref[0].astype(F32)) + (b_ref[1].astype(F32) + b_ref[2].astype(F32))

    blk = (None, tr, cn)
    in_specs = ([pl.BlockSpec(blk, lambda i, i_ref: (i_ref[1], i_ref[0] * nb + i, 0))] * n
                + [pl.BlockSpec(blk, lambda i, i_ref: (i_ref[1], i, 0))] * n
                + [pl.BlockSpec((3, tr, cn), lambda i, i_ref: (0, i, 0))] * n)
    args = [idx, *gl, *al, *bl]
    kw = {}
    if has:
        in_specs += [ANY] * n
        args += list(bufs)
        kw["input_output_aliases"] = {1 + 3 * n + t: t for t in range(n)}
    gs = pltpu.PrefetchScalarGridSpec(
        num_scalar_prefetch=1, grid=(nb,), in_specs=in_specs,
        out_specs=[pl.BlockSpec(blk, lambda i, i_ref: (lyr, i_ref[0] * nb + i, 0))] * n)
    return _pcall(body, name=name, grid_spec=gs, out_shape=[_sds((nlyr, r, cn))] * n,
                  compiler_params=_cp(("parallel",)), **kw)(*args)


def _sum8(name, g):
    _, r, n = g.shape
    tr = 8

    def body(g_ref, o_ref):
        o_ref[...] = ((g_ref[0] + g_ref[1]) + (g_ref[2] + g_ref[3])) + ((g_ref[4] + g_ref[5]) + (g_ref[6] + g_ref[7]))

    return _pcall(body, name=name, grid=(r // tr,), in_specs=[pl.BlockSpec((8, tr, n), lambda i: (0, i, 0))],
                  out_specs=pl.BlockSpec((tr, n), lambda i: (i, 0)), out_shape=_sds((r, n)),
                  compiler_params=_cp(("parallel",)))(g)


def _ada_mod(name, c16, ada_w, bias):
    nl, dm, n = ada_w.shape

    def body(c_ref, w_ref, b_ref, o_ref):
        o_ref[...] = _dot(_silu(c_ref[...]), w_ref[...], NN) + b_ref[...]

    return _pcall(body, name=name, grid=(nl,),
                  in_specs=[_full(c16.shape), pl.BlockSpec((None, dm, n), lambda i: (i, 0, 0)),
                            pl.BlockSpec((None, 1, n), lambda i: (i, 0, 0))],
                  out_specs=pl.BlockSpec((None, 16, n), lambda i: (i, 0, 0)), out_shape=_sds((nl, 16, n)),
                  compiler_params=_cp(("parallel",)))(c16, ada_w, bias)


def _ada_bwd(name, c16, dmod, ada_w):
    nl, dm, n = ada_w.shape

    def body(c_ref, d_ref, w_ref, gw_ref, dc_ref):
        @pl.when(pl.program_id(0) == 0)
        def _():
            dc_ref[...] = jnp.zeros_like(dc_ref)

        dv = d_ref[...]
        gw_ref[...] = _dot(_silu(c_ref[...]), dv, TN)
        dc_ref[...] += _dot(dv, w_ref[...], NT)

    return _pcall(body, name=name, grid=(nl,),
                  in_specs=[_full(c16.shape), pl.BlockSpec((None, 16, n), lambda i: (i, 0, 0)),
                            pl.BlockSpec((None, dm, n), lambda i: (i, 0, 0))],
                  out_specs=[pl.BlockSpec((None, dm, n), lambda i: (i, 0, 0)), _full((16, dm))],
                  out_shape=[_sds((nl, dm, n)), _sds((16, dm))],
                  compiler_params=_cp(("arbitrary",)))(c16, dmod, ada_w)


def _rowsum16(name, dmod):
    nl, _, n = dmod.shape

    def body(d_ref, o_ref):
        o_ref[...] = _colsum(d_ref[...])

    return _pcall(body, name=name, grid=(nl,), in_specs=[pl.BlockSpec((None, 16, n), lambda i: (i, 0, 0))],
                  out_specs=pl.BlockSpec((None, 1, n), lambda i: (i, 0, 0)), out_shape=_sds((nl, 1, n)),
                  compiler_params=_cp(("parallel",)))(dmod)


def _cctx_grad(name, parts, c_ctx):
    def body(p_ref, c_ref, o_ref):
        tot = (p_ref[0:1, :] + p_ref[1:2, :]) + (p_ref[2:3, :] + p_ref[3:4, :])
        o_ref[...] = tot * _dsilu(c_ref[...])

    return _pcall(body, name=name, out_shape=_sds(c_ctx.shape), compiler_params=_cp())(parts, c_ctx)


def _adamw(name, w, g, m, v, with_grad=False):
    shape = w.shape
    cn = shape[-1]
    r = math.prod(shape[:-1]) if len(shape) > 1 else 1
    tr = _row_block(r, cn)
    c1 = 1.0 - ADAM_B1 ** ADAM_STEP
    c2 = 1.0 - ADAM_B2 ** ADAM_STEP
    nout = 4 if with_grad else 3

    def body(w_ref, g_ref, m_ref, v_ref, d_ref, mo_ref, vo_ref, *rest):
        gv = g_ref[...]
        mn = ADAM_B1 * m_ref[...] + (1.0 - ADAM_B1) * gv
        vn = ADAM_B2 * v_ref[...] + (1.0 - ADAM_B2) * (gv * gv)
        d_ref[...] = -ADAM_LR * ((mn / c1) / (jnp.sqrt(vn / c2) + ADAM_EPS) + ADAM_WD * w_ref[...])
        mo_ref[...] = mn
        vo_ref[...] = vn
        if with_grad:
            rest[0][...] = gv

    blk = pl.BlockSpec((tr, cn), lambda i: (i, 0))
    o = _sds((r, cn))
    outs = _pcall(body, name=name, grid=(r // tr,), in_specs=[blk] * 4, out_specs=[blk] * nout, out_shape=[o] * nout,
                  compiler_params=_cp(("parallel",)))(*[a.reshape(r, cn) for a in (w, g, m, v)])
    return tuple(a.reshape(shape) for a in outs)


def _local_step(xs, target, modt, nw, fnw, get_w, get_ffn, put_g, ev, od, lc):
    t, dm = xs.shape
    nct = lc // ROW_TILE
    depth = nw.shape[0]
    cs, sn = _rope_tables(t, lc)
    saved = []
    x_in, x1p, fp = xs, None, None
    for i in range(depth):
        j, even = i // 2, i % 2 == 0
        tag = f"l{i}"
        w, deps = get_w(i, [fp] if i else [])
        if i == 0:
            _, h = _rnm(tag + "_norm1", x_in, None, None, 0, modt[0], 0, 1, nw[0, 0], nct, deps)
        else:
            x_in, h = _rnm(tag + "_norm1", x1p, fp, modt[i - 1], 5, modt[i], 0, 1, nw[i, 0], nct, deps)
        s = dict(x=x_in, h=h, w=w)
        if even:
            p = _mm_cols(tag + "_in", h, w["in"])
            q, k, v = _even_qkv(tag + "_qkv", p, cs, sn)
            of, ob, ss = _retention_fwd(tag + "_ret", q, k, v, ev["lgb"][j], lc)
            mix, yc = _even_mix(tag + "_mix", p, of, ob, ev["cw"][j], ev["lnw"][j], ev["lnb"][j], nct)
            y = _mm_full(tag + "_out", mix, w["out"], NN)
            s.update(p=p, q=q, k=k, v=v, of=of, ob=ob, ss=ss, yc=yc)
        else:
            p = _mm_cols(tag + "_in", h, w["in"])
            mix, m = _odd_mix(tag + "_mix", p, od["pw"][j], od["ps"][j], od["lnw"][j], od["lnb"][j],
                              od["sgw"][j], od["sgb"][j], nct, lc)
            y = _mm_full(tag + "_out", mix, w["out"], NN)
            s.update(p=p, m=m)
        x1, h2 = _rnm(tag + "_norm2", x_in, y, modt[i], 2, modt[i], 3, 4, nw[i, 1], nct)
        w.update(get_ffn(i, [y]))
        a, gt, up = _ffn_up(tag + "_ffn_up", h2, w["gate"], w["up"])
        f = _mm_full(tag + "_ffn_down", a, w["down"], NN)
        s.update(mix=mix, y=y, x1=x1, h2=h2, a=a, gt=gt, up=up, f=f)
        saved.append(s)
        x1p, fp = x1, f

    loss_blk, dx, df, fin_s = _fin("final", x1p, fp, modt[depth - 1], 5, fnw, target, nct)

    deps = []
    dmod = [[None] * 6 for _ in range(depth)]
    dnw = [[None, None] for _ in range(depth)]
    zero2 = jnp.zeros((2, dm), F32)
    dmod[depth - 1][5] = jnp.stack([zero2[0], fin_s[0]])
    small = dict(dfnw=fin_s[1], ev=[], od=[])
    for i in reversed(range(depth)):
        j, even = i // 2, i % 2 == 0
        tag = f"l{i}b"
        s = saved[i]
        w = s["w"]
        fh = w["down"].shape[0] // 2
        g = {}
        dgt, dup = _ffn_down_bwd(tag + "_ffn_down", df, w["down"], s["gt"], s["up"])
        g["down"] = _wgrad_rows(tag + "_gdown", s["a"], fh, df)
        g["gate"] = _wgrad_rows(tag + "_ggate", dgt, fh, s["h2"])
        g["up"] = _wgrad_rows(tag + "_gup", dup, fh, s["h2"])
        deps = put_g(i, "f", g)
        dh2 = _ffn_in_bwd(tag + "_ffn_in", dgt, dup, w["gate"], w["up"])
        dx1, dy, s2 = _bnm(tag + "_norm2", s["x1"], dh2, dx, s["y"], modt[i], 3, 4, modt[i], 2, nw[i, 1], nct, deps)
        dmod[i][3], dmod[i][4], dmod[i][2] = s2[:, 0], s2[:, 1], s2[:, 2]
        dnw[i][1] = s2[1, 3]
        dmix = _mm_full(tag + "_out", dy, w["out"], NT)
        g["out"] = _wgrad_rows(tag + "_gout", s["mix"], w["out"].shape[0] // 2, dy)
        if even:
            do, dg, dyc, lns = _even_mix_bwd1(tag + "_mix1", dmix, s["p"], s["of"], s["ob"], s["yc"],
                                              ev["lnw"][j], ev["lnb"][j])
            da, dgb, dcw = _even_conv_bwd(tag + "_conv", dyc, s["p"], ev["cw"][j], nct)
            dqf, dkf, dvf, dqb, dkb, dvb, dl = _retention_bwd(tag + "_ret", s["q"], s["k"], s["v"], do, s["ss"],
                                                              ev["lgb"][j], lc)
            dp = _even_dp(tag + "_dp", (dqf, dqb), (dkf, dkb), (dvf, dvb), dg, da, dgb, cs, sn)
            small["ev"].append(dict(lnw=lns[0], lnb=lns[1], cw=dcw, dl=dl[:, 0]))
        else:
            dm_, dpd, vec, dpw, dsgw, dsgb = _odd_mix_bwd1(tag + "_mix1", dmix, s["p"], s["m"], od["pw"][j], od["ps"][j],
                                                           od["lnw"][j], od["lnb"][j], od["sgw"][j], od["sgb"][j])
            dp = _odd_dp(tag + "_dp", dm_, dpd, nct, lc)
            small["od"].append(dict(ps=vec[0], lnw=vec[1], lnb=vec[2], pw=dpw, sgw=dsgw, sgb=dsgb[:, :, 0]))
        dh = _mm_cols_bwd(tag + "_in", dp, w["in"])
        g["in"] = _wgrad_cols(tag + "_gin", s["h"], dp, w["in"].shape[0])
        deps = put_g(i, "m", g)
        if i > 0:
            dx, df, s1 = _bnm(tag + "_norm1", s["x"], dh, dx1, saved[i - 1]["f"], modt[i], 0, 1, modt[i - 1], 5,
                              nw[i, 0], nct, deps)
            dmod[i - 1][5] = s1[:, 2]
        else:
            dx, _, s1 = _bnm(tag + "_norm1", s["x"], dh, dx1, None, modt[0], 0, 1, None, 0, nw[0, 0], nct, deps)
        dmod[i][0], dmod[i][1] = s1[:, 0], s1[:, 1]
        dnw[i][0] = s1[1, 3]
    small["ev"].reverse()
    small["od"].reverse()
    dmod_t = jnp.stack([jnp.concatenate([jnp.stack(rows, axis=1), jnp.zeros((2, 2, dm), F32)], axis=1) for rows in dmod])
    small["dmod"] = dmod_t
    small["dnw"] = jnp.stack([jnp.stack(r) for r in dnw])
    return loss_blk, dx, small


WEIGHTS = ["c_ctx", "ada_w", "ada_b", "norm_w", "even_w_in", "even_w_out", "ret_decay_logit", "conv_dw_w",
           "conv_ln_w", "conv_ln_b", "odd_w_in", "odd_w_out", "pool_w", "pool_scale", "sg_ln_w", "sg_ln_b",
           "sg_w", "sg_b", "ffn_w_gate", "ffn_w_up", "ffn_w_down", "final_norm_w"]
BIG = dict(even_in="even_w_in", even_out="even_w_out", odd_in="odd_w_in", odd_out="odd_w_out",
           gate="ffn_w_gate", up="ffn_w_up", down="ffn_w_down")


def _rows(a, width=1024):
    flat = a.reshape(-1)
    n = flat.shape[0]
    per = 8 * width
    tot = -(-n // per) * per
    return jnp.pad(flat, (0, tot - n)).reshape(tot // width, width)


def _unshard(parts, lead):
    nl = len(lead)
    perm = tuple(range(1, nl + 1)) + (0, nl + 1)
    return parts.transpose(perm).reshape(tuple(lead) + (4 * parts.shape[-1],))


def _my_cols(a, chip, n):
    start = (0,) * (a.ndim - 1) + (chip * n,)
    return lax.dynamic_slice(a, start, a.shape[:-1] + (n,))


def kernel(x, c, ctx, c_ctx, ada_w, ada_b, norm_w, even_w_in, even_w_out, ret_decay_logit, conv_dw_w, conv_ln_w, conv_ln_b, odd_w_in, odd_w_out, pool_w, pool_scale, sg_ln_w, sg_ln_b, sg_w, sg_b, ffn_w_gate, ffn_w_up, ffn_w_down, final_norm_w, loss_target, m_c_ctx, m_ada_w, m_ada_b, m_norm_w, m_even_w_in, m_even_w_out, m_ret_decay_logit, m_conv_dw_w, m_conv_ln_w, m_conv_ln_b, m_odd_w_in, m_odd_w_out, m_pool_w, m_pool_scale, m_sg_ln_w, m_sg_ln_b, m_sg_w, m_sg_b, m_ffn_w_gate, m_ffn_w_up, m_ffn_w_down, m_final_norm_w, v_c_ctx, v_ada_w, v_ada_b, v_norm_w, v_even_w_in, v_even_w_out, v_ret_decay_logit, v_conv_dw_w, v_conv_ln_w, v_conv_ln_b, v_odd_w_in, v_odd_w_out, v_pool_w, v_pool_scale, v_sg_ln_w, v_sg_ln_b, v_sg_w, v_sg_b, v_ffn_w_gate, v_ffn_w_up, v_ffn_w_down, v_final_norm_w):
    wv = dict(c_ctx=c_ctx, ada_w=ada_w, ada_b=ada_b, norm_w=norm_w, even_w_in=even_w_in, even_w_out=even_w_out,
              ret_decay_logit=ret_decay_logit, conv_dw_w=conv_dw_w, conv_ln_w=conv_ln_w, conv_ln_b=conv_ln_b,
              odd_w_in=odd_w_in, odd_w_out=odd_w_out, pool_w=pool_w, pool_scale=pool_scale, sg_ln_w=sg_ln_w,
              sg_ln_b=sg_ln_b, sg_w=sg_w, sg_b=sg_b, ffn_w_gate=ffn_w_gate, ffn_w_up=ffn_w_up,
              ffn_w_down=ffn_w_down, final_norm_w=final_norm_w)
    mv = dict(zip(WEIGHTS, (m_c_ctx, m_ada_w, m_ada_b, m_norm_w, m_even_w_in, m_even_w_out, m_ret_decay_logit,
                            m_conv_dw_w, m_conv_ln_w, m_conv_ln_b, m_odd_w_in, m_odd_w_out, m_pool_w, m_pool_scale,
                            m_sg_ln_w, m_sg_ln_b, m_sg_w, m_sg_b, m_ffn_w_gate, m_ffn_w_up, m_ffn_w_down,
                            m_final_norm_w)))
    vv = dict(zip(WEIGHTS, (v_c_ctx, v_ada_w, v_ada_b, v_norm_w, v_even_w_in, v_even_w_out, v_ret_decay_logit,
                            v_conv_dw_w, v_conv_ln_w, v_conv_ln_b, v_odd_w_in, v_odd_w_out, v_pool_w, v_pool_scale,
                            v_sg_ln_w, v_sg_ln_b, v_sg_w, v_sg_b, v_ffn_w_gate, v_ffn_w_up, v_ffn_w_down,
                            v_final_norm_w)))
    xi, yi, ci = lax.axis_index("x"), lax.axis_index("y"), lax.axis_index("c")
    chip = 2 * xi + yi
    dev = 4 * xi + 2 * yi + ci
    dm = x.shape[-1]
    lc = ctx.shape[1]
    depth = ada_w.shape[0]
    n_ada = ada_w.shape[-1]

    cw_pad = jnp.pad(conv_dw_w, ((0, 0), (0, 1), (0, 0)))
    vec3 = jnp.stack([pool_scale, sg_ln_w, sg_ln_b])
    pack1 = jnp.concatenate([_rows(c), _rows(norm_w), _rows(cw_pad), _rows(vec3)], axis=0)
    g1 = _all_gather8("gather_small", pack1).reshape(8, 32, dm)
    c_all = g1[:, 0]
    per_chip = g1[0::2]
    norm_full = _unshard(per_chip[:, 8:10].reshape(4, depth, 2, dm // 4), (depth, 2))
    cw_full = _unshard(per_chip[:, 16:24].reshape(4, 2, CONV_K + 1, 128), (2, CONV_K + 1))
    vec_full = _unshard(per_chip[:, 24, :768].reshape(4, 3, 2, 128), (3, 2))

    c16 = jnp.concatenate([c_all, c_ctx[None, :], jnp.zeros((7, dm), F32)], axis=0)
    mod_sh = _ada_mod("ada_mod", c16, ada_w, _my_cols(ada_b, chip, n_ada)[:, None, :])
    g2 = _all_gather8("gather_mod", mod_sh.reshape(depth * 16, n_ada)).reshape(8, depth, 16, n_ada)
    mod_full = _unshard(g2[0::2], (depth, 16))
    mod_x = lax.dynamic_index_in_dim(mod_full, dev, axis=1, keepdims=False).reshape(depth, 6, dm)
    mod_c = mod_full[:, 8].reshape(depth, 6, dm)
    modt = jnp.pad(jnp.stack([mod_c, mod_x], axis=1), ((0, 0), (0, 0), (0, 2), (0, 0)))

    names = list(BIG)
    tr_names = ("gate", "up")
    shard = {k: (jnp.swapaxes(wv[BIG[k]], 1, 2) if k in tr_names else wv[BIG[k]]).astype(BF16) for k in names}
    roles = ("in", "out", "gate", "up", "down")

    def layer_keys(i):
        mixer = ("even_in", "even_out") if i % 2 == 0 else ("odd_in", "odd_out")
        return [(k, i // 2) for k in mixer] + [(k, i) for k in ("gate", "up", "down")]

    def as_used(got):
        return {r: (g if r == "in" else g.reshape(4 * g.shape[1], g.shape[2])) for r, g in zip(roles, got)}

    started = {}

    def start_gather(tag, keys, before):
        srcs = [shard[k][l] for k, l in keys]
        lands = [lax.empty((4,) + s.shape, s.dtype) for s in srcs]
        return _split_start(f"gather_start{tag}", srcs, lands, _gather_copies(len(srcs)), before)

    def get_w(i, after):
        if i > 0:
            got = _split_wait(f"gather_wait{i}m", started[i, "m"], _gather_copies(2), after)[1]
            return as_used(got), []
        got = _gather_weights("gather_w0", [shard[k][l] for k, l in layer_keys(0)[:2]], [modt])
        last = None
        for li in range(depth):
            for part, keys in (("m", layer_keys(li)[:2]), ("f", layer_keys(li)[2:])):
                if (li, part) != (0, "m"):
                    started[li, part] = last = start_gather(f"{li}{part}", keys, [got[0]] + ([last[4]] if last else []))
        return as_used(got), [last[4]]

    def get_ffn(i, after):
        got = _split_wait(f"gather_wait{i}f", started[i, "f"], _gather_copies(3), after)[1]
        return {r: g.reshape(4 * g.shape[1], g.shape[2]) for r, g in zip(roles[2:], got)}

    idx = jnp.stack([ci, chip]).astype(jnp.int32)
    pairs, pending, stages = {}, {}, []

    def stage_keys(stage):
        i, part = stage
        return layer_keys(i)[2:] if part == "f" else layer_keys(i)[:2]

    def finish_pair(stage, after):
        tag = f"{stage[0]}{stage[1]}"
        n = len(stage_keys(stage))
        glist, from_sib = _split_wait(f"pair_wait{tag}", pairs[stage], _pair_copies(n), after)
        if stage[1] == "f":
            pair = _add_half(f"rs_add{tag}", glist, from_sib, idx)
        else:
            pair = [_add_half(f"rs_add{tag}_{t}", [gl], [a], idx)[0] for t, (gl, a) in enumerate(zip(glist, from_sib))]
        lands = [lax.empty((3,) + p.shape[1:], p.dtype) for p in pair]
        st = _split_start(f"rs_start{tag}", pair, lands, _scatter_copies(n), [])
        pending[stage] = (glist, from_sib, st)
        return [st[4]]

    def put_g(i, part, g):
        stage = (i, part)
        glist = [g[r].reshape(4, -1, g[r].shape[-1]) for r in (roles[2:] if part == "f" else roles[:2])]
        lands = [lax.empty((4, gl.shape[1] // 2, gl.shape[2]), gl.dtype) for gl in glist]
        pairs[stage] = _split_start(f"pair_start{i}{part}", glist, lands, _pair_copies(len(glist)), [])
        tokens = [pairs[stage][4]]
        if stages:
            tokens += finish_pair(stages[-1], [pairs[stage][4]])
        stages.append(stage)
        return tokens

    ev = dict(lgb=jnp.broadcast_to(ret_decay_logit.reshape(-1, 2 * HEADS)[:, :, None], (ret_decay_logit.shape[0], 2 * HEADS, HEAD_DIM)),
              cw=cw_full, lnw=conv_ln_w[:, None, :], lnb=conv_ln_b[:, None, :])
    od = dict(pw=pool_w, ps=vec_full[0][:, None, :], lnw=vec_full[1][:, None, :], lnb=vec_full[2][:, None, :],
              sgw=sg_w, sgb=jnp.broadcast_to(sg_b[:, :, :, None], sg_b.shape + (GC,)))
    xs = jnp.concatenate([ctx[0], x[0]], axis=0)
    loss_blk, dxs, small = _local_step(xs, loss_target[0], modt, norm_full[:, :, None, :], final_norm_w[None, :],
                                       get_w, get_ffn, put_g, ev, od, lc)

    misc = jnp.stack([
        small["dfnw"], jnp.broadcast_to(loss_blk[0, 0], (dm,)),
        jnp.concatenate([e["lnw"] for e in small["ev"]]), jnp.concatenate([e["lnb"] for e in small["ev"]]),
        jnp.concatenate([o["ps"] for o in small["od"]]), jnp.concatenate([o["lnw"] for o in small["od"]]),
        jnp.concatenate([o["lnb"] for o in small["od"]]),
        jnp.pad(jnp.concatenate([e["dl"] for e in small["ev"]]), (0, dm - 4 * HEADS)),
        jnp.stack([o["sgb"] for o in small["od"]]).reshape(-1)])
    pack2 = jnp.concatenate([
        _rows(small["dmod"]), _rows(small["dnw"]), _rows(misc), _rows(jnp.stack([e["cw"] for e in small["ev"]])),
        _rows(jnp.stack([o["pw"] for o in small["od"]])), _rows(jnp.stack([o["sgw"] for o in small["od"]]))], axis=0)
    n2 = pack2.shape[0]
    g3 = _all_gather8("gather_grads", pack2)
    tot = _sum8("sum_grads", g3.reshape(8, n2, dm))
    r_mod = depth * 16
    o_nw, o_misc = r_mod, r_mod + 8
    o_cw = o_misc + 16
    o_pw = o_cw + 2 * (CONV_K + 1) // 2
    o_sgw = o_pw + 128
    dmod_sum = tot[:r_mod].reshape(depth, 2, 8, dm)
    dmod_dev = g3.reshape(8, n2, dm)[:, :r_mod].reshape(8, depth, 2, 8, dm)
    dm_x = dmod_dev[:, :, 1, :6].reshape(8, depth, 6 * dm).transpose(1, 0, 2)
    dm_c = dmod_sum[:, 0, :6].reshape(depth, 1, 6 * dm)
    dmod16 = jnp.concatenate([dm_x, dm_c, jnp.zeros((depth, 7, 6 * dm), F32)], axis=1)
    g_ada_b = _rowsum16("ada_b_grad", dmod16)[:, 0]
    g_ada_w, dc16 = _ada_bwd("ada_bwd", c16, _my_cols(dmod16, chip, n_ada), ada_w)
    g4 = _all_gather8("gather_cctx", dc16[8:16]).reshape(8, 8, dm)
    g_c_ctx = _cctx_grad("cctx_grad", g4[0::2, 0], c_ctx[None, :])[0]

    misc_t = tot[o_misc:o_misc + 16]
    half = lambda row: misc_t[row].reshape(2, dm // 2)
    grads = dict(
        c_ctx=g_c_ctx, ada_w=g_ada_w, ada_b=g_ada_b,
        norm_w=_my_cols(tot[o_nw:o_nw + 8].reshape(depth, 2, dm), chip, dm // 4),
        ret_decay_logit=misc_t[7, :4 * HEADS].reshape(ret_decay_logit.shape),
        conv_dw_w=_my_cols(tot[o_cw:o_cw + 2 * (CONV_K + 1) // 2].reshape(2, CONV_K + 1, dm // 2)[:, :CONV_K], chip, 128),
        conv_ln_w=half(2), conv_ln_b=half(3),
        pool_w=tot[o_pw:o_pw + 128].reshape(pool_w.shape),
        pool_scale=_my_cols(half(4), chip, 128), sg_ln_w=_my_cols(half(5), chip, 128), sg_ln_b=_my_cols(half(6), chip, 128),
        sg_w=tot[o_sgw:o_sgw + 128].reshape(sg_w.shape), sg_b=misc_t[8].reshape(sg_b.shape),
        final_norm_w=misc_t[0])
    loss = misc_t[1, 0]

    last_tokens = finish_pair(stages[-1], [g_c_ctx])
    deltas, new_m, new_v = {}, {}, {}
    for n in WEIGHTS:
        if n not in BIG.values():
            deltas[n], new_m[n], new_v[n] = _adamw("adamw_" + n, wv[n], grads[n], mv[n], vv[n])
    reduced = {k: None for k in names}
    for stage in stages:
        glist, from_sib, st = pending[stage]
        last = stage == stages[-1]
        after = [deltas["ada_w"]] + [reduced[k] for k, _ in stage_keys(stages[-2])] if last else last_tokens
        slots = _split_wait(f"rs_wait{stage[0]}{stage[1]}", st, _scatter_copies(len(glist)), after)[1]
        keys = stage_keys(stage)
        groups = [range(len(keys))] if stage[1] == "f" else [[t] for t in range(len(keys))]
        for grp in groups:
            ks = [keys[t][0] for t in grp]
            lyr = keys[grp[0]][1]
            outs = _sum_final(f"rs_sum_{ks[0]}{lyr}", [glist[t] for t in grp], [from_sib[t] for t in grp],
                              [slots[t] for t in grp], idx, [reduced[k] for k in ks], lyr, shard[ks[0]].shape[0])
            for k, o in zip(ks, outs):
                reduced[k] = o
    shards = dict(zip(names, _rs_share("rs_share", [reduced[k] for k in names])))

    for k in names:
        n = BIG[k]
        tr = (lambda a: jnp.swapaxes(a, 1, 2)) if k in tr_names else (lambda a: a)
        outs = _adamw("adamw_" + n, tr(wv[n]), shards[k], tr(mv[n]), tr(vv[n]), with_grad=True)
        deltas[n], new_m[n], new_v[n], grads[n] = (tr(o) for o in outs)
    grad_x = dxs[None]
    return (loss, grad_x, *[grads[n] for n in WEIGHTS], *[deltas[n] for n in WEIGHTS],
            *[new_m[n] for n in WEIGHTS], *[new_v[n] for n in WEIGHTS])
```

```python
import functools
import math

import jax
import jax.numpy as jnp
from jax import lax
from jax.experimental import pallas as pl
from jax.experimental.pallas import tpu as pltpu

F32 = jnp.float32
BF16 = jnp.bfloat16
MESH = pl.DeviceIdType.MESH

EPS = 1e-6
GRID_W = 64
HEADS = 4
HEAD_DIM = 128
CHUNK = 128
CONV_K = 31
ROPE_BASE = 10000.0
ROPE_PAIRS = (16, 24, 24)
POOL_WINDOWS = (2, 4, 8, 16)
ADAM_LR, ADAM_B1, ADAM_B2, ADAM_EPS, ADAM_WD, ADAM_STEP = 0.001, 0.9, 0.999, 1e-08, 0.01, 10

ROW_TILE = 256
CONV_HALO = 16
POOL_HALO = 8
VMEM_LIMIT = 56 * 1024 * 1024
WGRAD_ROWS = 2304


def _pcall(body, **kw):
    return pl.pallas_call(body, **kw)


def _cp(sem=None, vmem=VMEM_LIMIT):
    if sem is None:
        return pltpu.CompilerParams(vmem_limit_bytes=vmem)
    return pltpu.CompilerParams(dimension_semantics=sem, vmem_limit_bytes=vmem)


def _sds(shape, dtype=F32):
    return jax.ShapeDtypeStruct(tuple(shape), dtype)


def _full(shape):
    nd = len(shape)
    return pl.BlockSpec(tuple(shape), lambda *_: (0,) * nd)


def _sigmoid(x):
    return jax.nn.sigmoid(x)


def _silu(x):
    return x * _sigmoid(x)


def _dsilu(x):
    s = _sigmoid(x)
    return s * (1.0 + x * (1.0 - s))


def _colsum(a):
    return jnp.sum(a, axis=0, keepdims=True)


def _dot(a, b, dn):
    return lax.dot_general(a.astype(BF16), b.astype(BF16), dn, preferred_element_type=F32)


NN = (((1,), (0,)), ((), ()))
NT = (((1,), (1,)), ((), ()))
TN = (((0,), (0,)), ((), ()))


def _mm_tile(t, cap=1152):
    best = 16
    for d in range(16, min(t, cap) + 1, 16):
        if t % d == 0:
            best = d
    return best


def _mm(name, pairs, grid, out_shape, out_spec, dn):
    npairs = len(pairs)
    nk = grid[-1]
    kax = len(grid) - 1
    assert nk == 1 or out_shape.dtype == F32

    def body(*refs):
        ins = refs[:2 * npairs]
        o_ref = refs[2 * npairs]
        tot = None
        for p in range(npairs):
            d = _dot(ins[2 * p][...], ins[2 * p + 1][...], dn)
            tot = d if tot is None else tot + d
        if nk == 1:
            o_ref[...] = tot.astype(o_ref.dtype)
        else:
            k = pl.program_id(kax)

            @pl.when(k == 0)
            def _():
                o_ref[...] = tot

            @pl.when(k != 0)
            def _():
                o_ref[...] += tot

    args, in_specs = [], []
    for a, a_spec, b, b_spec in pairs:
        args += [a, b]
        in_specs += [a_spec, b_spec]
    sem = ("parallel",) * kax + ("arbitrary",)
    return _pcall(body, name=name, grid=grid, in_specs=in_specs, out_specs=out_spec, out_shape=out_shape,
                  compiler_params=_cp(sem))(*args)


def _mm_cols(name, a, w, out_dtype=F32):
    t, k = a.shape
    j, _, n = w.shape
    tm = _mm_tile(t)
    return _mm(name, [(a, pl.BlockSpec((tm, k), lambda i, jj, kk: (i, 0)),
                       w, pl.BlockSpec((None, k, n), lambda i, jj, kk: (jj, 0, 0)))],
               (t // tm, j, 1), _sds((t, j * n), out_dtype), pl.BlockSpec((tm, n), lambda i, jj, kk: (i, jj)), NN)


def _mm_cols_bwd(name, d, w):
    t = d.shape[0]
    j, k, n = w.shape
    tm = _mm_tile(t)
    pairs = [(d, pl.BlockSpec((tm, n), functools.partial(lambda jj, i, u, kk: (i, jj), jj)),
              w, pl.BlockSpec((None, k, n), functools.partial(lambda jj, i, u, kk: (jj, 0, 0), jj))) for jj in range(j)]
    return _mm(name, pairs, (t // tm, 1, 1), _sds((t, k), BF16), pl.BlockSpec((tm, k), lambda i, u, kk: (i, 0)), NT)


def _mm_full(name, a, w, dn, tm=None):
    t, k = a.shape
    n = w.shape[1] if dn is NN else w.shape[0]
    tm = tm or _mm_tile(t)
    return _mm(name, [(a, pl.BlockSpec((tm, k), lambda i, u, kk: (i, 0)), w, _full(w.shape))],
               (t // tm, 1, 1), _sds((t, n)), pl.BlockSpec((tm, n), lambda i, u, kk: (i, 0)), dn)


def _wgrad_cols(name, a, b, j):
    t, k = a.shape
    n = b.shape[1] // j
    tt = _mm_tile(t, 2 * WGRAD_ROWS)
    return _mm(name, [(a, pl.BlockSpec((tt, k), lambda jj, u, kk: (kk, 0)),
                       b, pl.BlockSpec((tt, n), lambda jj, u, kk: (kk, jj)))],
               (j, 1, t // tt), _sds((j, k, n)), pl.BlockSpec((None, k, n), lambda jj, u, kk: (jj, 0, 0)), TN)


def _wgrad_rows(name, a, blk, b):
    t, f = a.shape
    n = b.shape[1]
    tt = _mm_tile(t, WGRAD_ROWS)
    return _mm(name, [(a, pl.BlockSpec((tt, blk), lambda jj, u, kk: (kk, jj)),
                       b, pl.BlockSpec((tt, n), lambda jj, u, kk: (kk, 0)))],
               (f // blk, 1, t // tt), _sds((f, n)), pl.BlockSpec((blk, n), lambda jj, u, kk: (jj, 0)), TN)


def _ffn_tiles(t, f):
    return _mm_tile(t, 288), f


def _ffn_up(name, h, wgt, wut):
    t, k = h.shape
    f = wgt.shape[0]
    tm, tn = _ffn_tiles(t, f)

    def body(h_ref, wg_ref, wu_ref, a_ref, gt_ref, up_ref):
        hv = h_ref[...]
        gt = _dot(hv, wg_ref[...], NT)
        up = _dot(hv, wu_ref[...], NT)
        a_ref[...] = (_silu(gt) * up).astype(BF16)
        gt_ref[...] = gt.astype(BF16)
        up_ref[...] = up.astype(BF16)

    wspec = pl.BlockSpec((tn, k), lambda i, jj: (jj, 0))
    ospec = pl.BlockSpec((tm, tn), lambda i, jj: (i, jj))
    o = _sds((t, f), BF16)
    return _pcall(body, name=name, grid=(t // tm, f // tn),
                  in_specs=[pl.BlockSpec((tm, k), lambda i, jj: (i, 0)), wspec, wspec],
                  out_specs=[ospec, ospec, ospec], out_shape=[o, o, o],
                  compiler_params=_cp(("parallel", "parallel")))(h, wgt, wut)


def _ffn_down_bwd(name, df, wd, gt, up):
    t, dm = df.shape
    f = wd.shape[0]
    tm, tn = _ffn_tiles(t, f)

    def body(df_ref, wd_ref, gt_ref, up_ref, dgt_ref, dup_ref):
        da = _dot(df_ref[...], wd_ref[...], NT)
        g = gt_ref[...].astype(F32)
        u = up_ref[...].astype(F32)
        s = _sigmoid(g)
        dgt_ref[...] = (da * u * (s * (1.0 + g * (1.0 - s)))).astype(BF16)
        dup_ref[...] = (da * (g * s)).astype(BF16)

    aspec = pl.BlockSpec((tm, tn), lambda i, jj: (i, jj))
    o = _sds((t, f), BF16)
    return _pcall(body, name=name, grid=(t // tm, f // tn),
                  in_specs=[pl.BlockSpec((tm, dm), lambda i, jj: (i, 0)),
                            pl.BlockSpec((tn, dm), lambda i, jj: (jj, 0)), aspec, aspec],
                  out_specs=[aspec, aspec], out_shape=[o, o],
                  compiler_params=_cp(("parallel", "parallel")))(df, wd, gt, up)


def _ffn_in_bwd(name, dgt, dup, wgt, wut):
    t, f = dgt.shape
    k = wgt.shape[1]
    tm = _mm_tile(t, 576)
    aspec = pl.BlockSpec((tm, f), lambda i, u, kk: (i, 0))
    wspec = pl.BlockSpec((f, k), lambda i, u, kk: (0, 0))
    return _mm(name, [(dgt, aspec, wgt, wspec), (dup, aspec, wut, wspec)], (t // tm, 1, 1), _sds((t, k), BF16),
               pl.BlockSpec((tm, k), lambda i, u, kk: (i, 0)), NN)


def _modrow(ref, row, is_ctx):
    return jnp.where(is_ctx, ref[0, row:row + 1, :], ref[1, row:row + 1, :])


def _rnm(name, x, delta, mod_g, g_row, mod_n, sh_row, sc_row, nw, nct, deps=()):
    t, dm = x.shape
    tm = ROW_TILE
    has = delta is not None
    nd = len(deps)

    def body(*refs):
        refs = refs[:len(refs) - nd - (2 if has else 1)] + refs[len(refs) - (2 if has else 1):]
        if has:
            x_ref, d_ref, mg_ref, m_ref, nw_ref, xo_ref, h_ref = refs
        else:
            x_ref, m_ref, nw_ref, h_ref = refs
        is_ctx = pl.program_id(0) < nct
        xv = x_ref[...]
        if has:
            xv = xv + _modrow(mg_ref, g_row, is_ctx) * d_ref[...]
            xo_ref[...] = xv
        r = lax.rsqrt(jnp.mean(xv * xv, axis=-1, keepdims=True) + EPS)
        hv = (xv * r * nw_ref[...]) * (1.0 + _modrow(m_ref, sc_row, is_ctx)) + _modrow(m_ref, sh_row, is_ctx)
        h_ref[...] = hv.astype(BF16)

    row = pl.BlockSpec((tm, dm), lambda i: (i, 0))
    ins = [x] + ([delta, mod_g] if has else []) + [mod_n, nw] + list(deps)
    in_specs = ([row] + ([row, _full(mod_g.shape)] if has else []) + [_full(mod_n.shape), _full(nw.shape)]
                + [_full(d.shape) for d in deps])
    outs = ([_sds((t, dm))] if has else []) + [_sds((t, dm), BF16)]
    out_specs = ([row] if has else []) + [row]
    res = _pcall(body, name=name, grid=(t // tm,), in_specs=in_specs, out_specs=out_specs, out_shape=outs,
                 compiler_params=_cp(("parallel",)))(*ins)
    return res if has else (None, res[0])


def _bnm(name, xn, dh, dup, yprev, mod_n, sh_row, sc_row, mod_g, g_row, nw, nct, deps=()):
    t, dm = xn.shape
    tm = ROW_TILE
    has = yprev is not None
    nd = len(deps)

    def body(*refs):
        nout = 3 if has else 2
        refs = refs[:len(refs) - nd - nout] + refs[len(refs) - nout:]
        if has:
            x_ref, dh_ref, du_ref, y_ref, mn_ref, mg_ref, nw_ref, dx_ref, dd_ref, s_ref = refs
        else:
            x_ref, dh_ref, du_ref, mn_ref, nw_ref, dx_ref, s_ref = refs
        i = pl.program_id(0)
        is_ctx = i < nct

        @pl.when(i == 0)
        def _():
            s_ref[...] = jnp.zeros_like(s_ref)

        xv = x_ref[...]
        r = lax.rsqrt(jnp.mean(xv * xv, axis=-1, keepdims=True) + EPS)
        xh = xv * r
        w = nw_ref[...]
        sc1 = 1.0 + _modrow(mn_ref, sc_row, is_ctx)
        dhv = dh_ref[...].astype(F32)
        dxh = dhv * sc1 * w
        dx = r * (dxh - xh * jnp.mean(dxh * xh, axis=-1, keepdims=True)) + du_ref[...]
        dx_ref[...] = dx
        parts = [_colsum(dhv), _colsum(dhv * (xh * w))]
        if has:
            dd_ref[...] = (_modrow(mg_ref, g_row, is_ctx) * dx).astype(BF16)
            parts.append(_colsum(dx * y_ref[...]))
        else:
            parts.append(jnp.zeros((1, dm), F32))
        upd = jnp.concatenate(parts + [jnp.zeros((5, dm), F32)], axis=0)
        dnw = jnp.concatenate([jnp.zeros((3, dm), F32), _colsum(dhv * sc1 * xh), jnp.zeros((4, dm), F32)], axis=0)

        @pl.when(is_ctx)
        def _():
            s_ref[0] += upd
            s_ref[1] += dnw

        @pl.when(jnp.logical_not(is_ctx))
        def _():
            s_ref[1] += upd + dnw

    row = pl.BlockSpec((tm, dm), lambda i: (i, 0))
    ins = [xn, dh, dup] + ([yprev] if has else []) + [mod_n] + ([mod_g] if has else []) + [nw] + list(deps)
    in_specs = ([row, row, row] + ([row] if has else []) + [_full(mod_n.shape)]
                + ([_full(mod_g.shape)] if has else []) + [_full(nw.shape)] + [_full(d.shape) for d in deps])
    if has:
        outs = [_sds((t, dm)), _sds((t, dm), BF16), _sds((2, 8, dm))]
        out_specs = [row, row, _full((2, 8, dm))]
    else:
        outs = [_sds((t - nct * tm, dm)), _sds((2, 8, dm))]
        out_specs = [pl.BlockSpec((tm, dm), lambda i: (jnp.maximum(i - nct, 0), 0)), _full((2, 8, dm))]
    res = _pcall(body, name=name, grid=(t // tm,), in_specs=in_specs, out_specs=out_specs, out_shape=outs,
                 compiler_params=_cp(("arbitrary",)))(*ins)
    return res if has else (res[0], None, res[1])


def _fin(name, x1, f, mod, g_row, fw, target, nct):
    t, dm = x1.shape
    tm = ROW_TILE

    def body(x_ref, f_ref, m_ref, fw_ref, t_ref, loss_ref, dx_ref, dd_ref, s_ref):
        i = pl.program_id(0)

        @pl.when(i == 0)
        def _():
            s_ref[...] = jnp.zeros_like(s_ref)
            loss_ref[...] = jnp.zeros_like(loss_ref)

        @pl.when(i < nct)
        def _():
            dx_ref[...] = jnp.zeros_like(dx_ref)
            dd_ref[...] = jnp.zeros_like(dd_ref)

        @pl.when(i >= nct)
        def _():
            g = m_ref[1, g_row:g_row + 1, :]
            fv = f_ref[...]
            xv = x_ref[...] + g * fv
            r = lax.rsqrt(jnp.mean(xv * xv, axis=-1, keepdims=True) + EPS)
            xh = xv * r
            w = fw_ref[...]
            err = xh * w - t_ref[...]
            loss_ref[...] += 0.5 * jnp.sum(err * err) / dm
            dout = err * (1.0 / dm)
            dxh = dout * w
            dx = r * (dxh - xh * jnp.mean(dxh * xh, axis=-1, keepdims=True))
            dx_ref[...] = dx
            dd_ref[...] = (g * dx).astype(BF16)
            s_ref[...] += jnp.concatenate([_colsum(dx * fv), _colsum(dout * xh), jnp.zeros((6, dm), F32)], axis=0)

    row = pl.BlockSpec((tm, dm), lambda i: (i, 0))
    trow = pl.BlockSpec((tm, dm), lambda i: (jnp.maximum(i - nct, 0), 0))
    return _pcall(body, name=name, grid=(t // tm,),
                  in_specs=[row, row, _full(mod.shape), _full(fw.shape), trow],
                  out_specs=[_full((8, 128)), row, row, _full((8, dm))],
                  out_shape=[_sds((8, 128)), _sds((t, dm)), _sds((t, dm), BF16), _sds((8, dm))],
                  compiler_params=_cp(("arbitrary",)))(x1, f, mod, fw, target)


def _rope_tables(t, lc):
    l = t - lc
    rows = l // GRID_W
    grid_r = jnp.broadcast_to(jnp.arange(rows, dtype=F32)[:, None], (rows, GRID_W)).reshape(-1)
    grid_c = jnp.broadcast_to(jnp.arange(GRID_W, dtype=F32)[None, :], (rows, GRID_W)).reshape(-1)

    def angles(p_seq, p_row, p_col):
        parts = []
        for p, n in zip((p_seq, p_row, p_col), ROPE_PAIRS):
            freq = ROPE_BASE ** (-jnp.arange(n, dtype=F32) / n)
            parts.append(p[:, None] * freq[None, :])
        return jnp.concatenate(parts, axis=-1)

    zc = jnp.zeros((lc,), F32)
    ang = jnp.concatenate([angles(jnp.arange(lc, dtype=F32), zc, zc),
                           angles(jnp.full((l,), lc, F32), grid_r, grid_c)], axis=0)
    cos, sin = jnp.cos(ang), jnp.sin(ang)
    return jnp.concatenate([cos, cos], axis=-1), jnp.concatenate([-sin, sin], axis=-1)


def _rope(u, cs, sn):
    return u * cs + pltpu.roll(u, HEAD_DIM // 2, 1) * sn


def _rope_t(d, cs, sn):
    return d * cs + pltpu.roll(d * sn, HEAD_DIM // 2, 1)


def _even_qkv(name, p, cs, sn):
    t = p.shape[0]
    tm = ROW_TILE
    w = HEADS * HEAD_DIM
    scale = HEAD_DIM ** -0.5

    def body(q_ref, k_ref, v_ref, cs_ref, sn_ref, qo_ref, ko_ref, vo_ref):
        c, s = cs_ref[...], sn_ref[...]
        for h in range(HEADS):
            sl = slice(h * HEAD_DIM, (h + 1) * HEAD_DIM)
            qo_ref[:, sl] = (_rope(q_ref[:, sl], c, s) * scale).astype(BF16)
            ko_ref[:, sl] = _rope(k_ref[:, sl], c, s).astype(BF16)
        vo_ref[...] = v_ref[...].astype(BF16)

    col = lambda j: pl.BlockSpec((tm, w), lambda i: (i, j))
    tab = pl.BlockSpec((tm, HEAD_DIM), lambda i: (i, 0))
    o = _sds((t, w), BF16)
    return _pcall(body, name=name, grid=(t // tm,), in_specs=[col(0), col(1), col(2), tab, tab],
                  out_specs=[col(0)] * 3, out_shape=[o, o, o], compiler_params=_cp(("parallel",)))(p, p, p, cs, sn)


def _log_sigmoid_row(x):
    e = jnp.exp(-jnp.abs(x))
    l1p = jnp.where(e < 0.01, e * (1.0 - e * (0.5 - e * (1.0 / 3.0))), jnp.log(1.0 + e))
    return jnp.minimum(x, 0.0) - l1p


def _ret_tables(lgb_ref, dm_ref, xi_ref, zt_ref):
    ri = lax.broadcasted_iota(jnp.int32, (CHUNK, CHUNK), 0).astype(F32)
    ci = lax.broadcasted_iota(jnp.int32, (CHUNK, CHUNK), 1).astype(F32)
    for d in range(2):
        for h in range(HEADS):
            idx = d * HEADS + h
            lg = _log_sigmoid_row(lgb_ref[idx:idx + 1, :])
            if d == 0:
                e, mask = ri - ci, ri >= ci
                xe, ze = ri + 1.0, (CHUNK - 1.0) - ri
            else:
                e, mask = ci - ri - 1.0, ci > ri
                xe, ze = (CHUNK - 1.0) - ri, ri
            dm_ref[idx] = jnp.where(mask, jnp.exp(lg * jnp.where(mask, e, 0.0)), 0.0)
            xi_ref[idx] = jnp.exp(lg * xe)
            zt_ref[idx] = jnp.exp(lg * ze)


def _ret_exponents(d):
    ri = lax.broadcasted_iota(jnp.int32, (CHUNK, CHUNK), 0).astype(F32)
    ci = lax.broadcasted_iota(jnp.int32, (CHUNK, CHUNK), 1).astype(F32)
    if d == 0:
        return ri - ci, ri + 1.0, (CHUNK - 1.0) - ri
    return ci - ri - 1.0, (CHUNK - 1.0) - ri, ri


RET_SUB = 2


def _bwd_chunk(n, ncc, nc):
    return jnp.where(n < ncc, ncc - 1 - n, nc - 1 - (n - ncc))


def _retention_fwd(name, q, k, v, lgb, lc):
    t, w = q.shape
    nc = t // CHUNK
    rows_per = RET_SUB * CHUNK
    nb, ncb = t // rows_per, lc // rows_per
    nh = 2 * HEADS

    def body(qf_ref, kf_ref, vf_ref, qb_ref, kb_ref, vb_ref, lgb_ref, of_ref, ob_ref, ss_ref,
             s_ref, dm_ref, xi_ref, zt_ref):
        n = pl.program_id(0)

        @pl.when(n == 0)
        def _():
            s_ref[...] = jnp.zeros_like(s_ref)
            _ret_tables(lgb_ref, dm_ref, xi_ref, zt_ref)

        where = []
        for u in range(RET_SUB):
            for d in range(2):
                refs = (qf_ref, kf_ref, vf_ref, of_ref) if d == 0 else (qb_ref, kb_ref, vb_ref, ob_ref)
                r0 = (u if d == 0 else RET_SUB - 1 - u) * CHUNK
                for h in range(HEADS):
                    where.append((u, d * HEADS + h, refs, slice(r0, r0 + CHUNK), slice(h * HEAD_DIM, (h + 1) * HEAD_DIM)))
        qs = [refs[0][rows, sl] for _, _, refs, rows, sl in where]
        ks = [refs[1][rows, sl] for _, _, refs, rows, sl in where]
        vs = [refs[2][rows, sl] for _, _, refs, rows, sl in where]
        sc = [_dot(qv, kv, NT) for qv, kv in zip(qs, ks)]
        upd = [_dot(kv.astype(F32) * zt_ref[idx], vv, TN) for (_, idx, *_), kv, vv in zip(where, ks, vs)]
        cur = [s_ref[idx] for idx in range(nh)]
        gcs = [jnp.exp(_log_sigmoid_row(lgb_ref[idx:idx + 1, :]) * float(CHUNK)) for idx in range(nh)]
        st = []
        for (u, idx, *_), du in zip(where, upd):
            st.append(cur[idx])
            ss_ref[u, idx] = cur[idx]
            cur[idx] = gcs[idx] * cur[idx] + du
        for idx in range(nh):
            s_ref[idx] = cur[idx]
        inter = [_dot(qv.astype(F32) * xi_ref[idx], s, NN) for (_, idx, *_), qv, s in zip(where, qs, st)]
        intra = [_dot(a * dm_ref[idx], vv, NN) for (_, idx, *_), a, vv in zip(where, sc, vs)]
        for (_, _, refs, rows, sl), o1, o2 in zip(where, intra, inter):
            refs[3][rows, sl] = o1 + o2

    fspec = pl.BlockSpec((rows_per, w), lambda n: (n, 0))
    bspec = pl.BlockSpec((rows_per, w), lambda n: (_bwd_chunk(n, ncb, nb), 0))
    tab = pltpu.VMEM((nh, CHUNK, CHUNK), F32)
    return _pcall(body, name=name, grid=(nb,),
                  in_specs=[fspec] * 3 + [bspec] * 3 + [_full((nh, HEAD_DIM))],
                  out_specs=[fspec, bspec, pl.BlockSpec((RET_SUB, nh, CHUNK, CHUNK), lambda n: (n, 0, 0, 0))],
                  out_shape=[_sds((t, w)), _sds((t, w)), _sds((nc, nh, CHUNK, CHUNK))],
                  scratch_shapes=[tab, tab, tab, tab],
                  compiler_params=_cp(("arbitrary",)))(q, k, v, q, k, v, lgb)


def _retention_bwd(name, q, k, v, do, ss, lgb, lc):
    t, w = q.shape
    rows_per = RET_SUB * CHUNK
    nb, ncb = t // rows_per, lc // rows_per
    nh = 2 * HEADS

    def body(qf_ref, kf_ref, vf_ref, gf_ref, qb_ref, kb_ref, vb_ref, gb_ref, ss_ref, lgb_ref,
             dqf_ref, dkf_ref, dvf_ref, dqb_ref, dkb_ref, dvb_ref, dl_ref,
             ds_ref, dm_ref, xi_ref, zt_ref, acc_ref):
        n = pl.program_id(0)

        @pl.when(n == 0)
        def _():
            ds_ref[...] = jnp.zeros_like(ds_ref)
            acc_ref[...] = jnp.zeros_like(acc_ref)
            _ret_tables(lgb_ref, dm_ref, xi_ref, zt_ref)

        where = []
        for u in reversed(range(RET_SUB)):
            for d in range(2):
                refs = ((qf_ref, kf_ref, vf_ref, gf_ref, dqf_ref, dkf_ref, dvf_ref) if d == 0
                        else (qb_ref, kb_ref, vb_ref, gb_ref, dqb_ref, dkb_ref, dvb_ref))
                r0 = (u if d == 0 else RET_SUB - 1 - u) * CHUNK
                for h in range(HEADS):
                    where.append((u, d * HEADS + h, d, refs, slice(r0, r0 + CHUNK), slice(h * HEAD_DIM, (h + 1) * HEAD_DIM)))
        qs = [refs[0][rows, sl] for *_, refs, rows, sl in where]
        ks = [refs[1][rows, sl] for *_, refs, rows, sl in where]
        vs = [refs[2][rows, sl] for *_, refs, rows, sl in where]
        gs = [refs[3][rows, sl] for *_, refs, rows, sl in where]
        st = [ss_ref[u, idx] for u, idx, *_ in where]
        sc = [_dot(qv, kv, NT) for qv, kv in zip(qs, ks)]
        dar = [_dot(gv, vv, NT) for gv, vv in zip(gs, vs)]
        t1 = [_dot(gv, s, NT) for gv, s in zip(gs, st)]
        dsn = [_dot(qv.astype(F32) * xi_ref[idx], gv, TN) for (_, idx, *_), qv, gv in zip(where, qs, gs)]
        cur = [ds_ref[idx] for idx in range(nh)]
        gcs = [jnp.exp(_log_sigmoid_row(lgb_ref[idx:idx + 1, :]) * float(CHUNK)) for idx in range(nh)]
        dsps = []
        for (_, idx, *_), x in zip(where, dsn):
            dsps.append(cur[idx])
            cur[idx] = gcs[idx] * cur[idx] + x
        for idx in range(nh):
            ds_ref[idx] = cur[idx]
        t2 = [_dot(vv, dsp, NT) for vv, dsp in zip(vs, dsps)]
        dv2 = [_dot(kv.astype(F32) * zt_ref[idx], dsp, NN) for (_, idx, *_), kv, dsp in zip(where, ks, dsps)]
        a = [x * dm_ref[idx] for (_, idx, *_), x in zip(where, sc)]
        da = [x * dm_ref[idx] for (_, idx, *_), x in zip(where, dar)]
        dq = [_dot(x, kv, NN) for x, kv in zip(da, ks)]
        dk = [_dot(x, qv, TN) for x, qv in zip(da, qs)]
        dv1 = [_dot(x, gv, TN) for x, gv in zip(a, gs)]
        for i8, (_, idx, d, refs, rows, sl) in enumerate(where):
            ee, xe, ze = _ret_exponents(d)
            xi, zt = xi_ref[idx], zt_ref[idx]
            qf32, kf32 = qs[i8].astype(F32), ks[i8].astype(F32)
            refs[4][rows, sl] = dq[i8] + xi * t1[i8]
            refs[5][rows, sl] = dk[i8] + zt * t2[i8]
            refs[6][rows, sl] = dv1[i8] + dv2[i8]
            acc_ref[idx] += (ee * a[i8] * dar[i8] + xe * xi * qf32 * t1[i8] + ze * zt * kf32 * t2[i8]
                             + (float(CHUNK) * gcs[idx]) * dsps[i8] * st[i8])

        @pl.when(n == nb - 1)
        def _():
            for idx in range(nh):
                tot = jnp.sum(acc_ref[idx])
                dl_ref[idx:idx + 1, :] = tot * _sigmoid(-lgb_ref[idx:idx + 1, :])

    fmap = lambda n: (nb - 1 - n, 0)
    bmap = lambda n: (_bwd_chunk(nb - 1 - n, ncb, nb), 0)
    fspec = pl.BlockSpec((rows_per, w), fmap)
    bspec = pl.BlockSpec((rows_per, w), bmap)
    tab = pltpu.VMEM((nh, CHUNK, CHUNK), F32)
    o = _sds((t, w))
    return _pcall(body, name=name, grid=(nb,),
                  in_specs=[fspec] * 4 + [bspec] * 4
                  + [pl.BlockSpec((RET_SUB, nh, CHUNK, CHUNK), lambda n: (nb - 1 - n, 0, 0, 0)), _full((nh, HEAD_DIM))],
                  out_specs=[fspec] * 3 + [bspec] * 3 + [_full((nh, HEAD_DIM))],
                  out_shape=[o] * 6 + [_sds((nh, HEAD_DIM))],
                  scratch_shapes=[tab, tab, tab, tab, tab],
                  compiler_params=_cp(("arbitrary",)))(q, k, v, do, q, k, v, do, ss, lgb)


def _halo_specs(tm, halo, t, width, col):
    hb = tm // halo
    last = t // halo - 1
    prev = pl.BlockSpec((halo, width), lambda i: (jnp.maximum(i * hb - 1, 0), col))
    nxt = pl.BlockSpec((halo, width), lambda i: (jnp.minimum((i + 1) * hb, last), col))
    return prev, nxt


def _halo_valid(i, nct, nt):
    vp = jnp.logical_and(i != 0, i != nct)
    vn = jnp.logical_and(i != nct - 1, i != nt - 1)
    return vp, vn


def _fill_window(win_ref, prev, cur, nxt, vp, vn, halo, tm):
    win_ref[0:halo, :] = jnp.where(vp, prev, 0.0)
    win_ref[halo:halo + tm, :] = cur
    win_ref[halo + tm:halo + tm + halo, :] = jnp.where(vn, nxt, 0.0)


CONV_SUB = 64


SUBLANES = 8


def _shift_window(win_ref, sh_ref, tm):
    rows = tm + 2 * CONV_HALO - SUBLANES
    for s in range(SUBLANES):
        sh_ref[s, 0:rows, :] = win_ref[s:s + rows, :]


def _window_rows(sh_ref, start, rows):
    s = start % SUBLANES
    return sh_ref[s, start - s:start - s + rows, :]


def _conv_taps(sh_ref, w_ref, tm, flip):
    outs = []
    for r0 in range(0, tm, CONV_SUB):
        acc = None
        for kk in range(CONV_K):
            wk = (CONV_K - 1 - kk) if flip else kk
            term = w_ref[wk:wk + 1, :] * _window_rows(sh_ref, r0 + kk + 1, CONV_SUB)
            acc = term if acc is None else acc + term
        outs.append(acc)
    return jnp.concatenate(outs, axis=0)


def _head_norm(y):
    r = lax.rsqrt(jnp.mean(y * y, axis=-1, keepdims=True) + EPS)
    return y * r, r


def _ln_stats(y):
    mu = jnp.mean(y, axis=-1, keepdims=True)
    yc = y - mu
    rs = lax.rsqrt(jnp.mean(yc * yc, axis=-1, keepdims=True) + EPS)
    return yc * rs, rs


def _ln_bwd(dyh, yh, rs):
    return rs * (dyh - jnp.mean(dyh, axis=-1, keepdims=True) - yh * jnp.mean(dyh * yh, axis=-1, keepdims=True))


def _even_mix(name, p, of, ob, cw, lnw, lnb, nct):
    t = p.shape[0]
    tm, halo = ROW_TILE, CONV_HALO
    nt = t // tm
    w = HEADS * HEAD_DIM

    def body(g_ref, a_ref, gb_ref, ap_ref, gbp_ref, an_ref, gbn_ref, of_ref, ob_ref, cw_ref, lw_ref, lb_ref,
             mix_ref, yc_ref, win_ref, sh_ref):
        i = pl.program_id(0)
        vp, vn = _halo_valid(i, nct, nt)
        glu = lambda a, b: a * _sigmoid(b)
        _fill_window(win_ref, glu(ap_ref[...], gbp_ref[...]), glu(a_ref[...], gb_ref[...]),
                     glu(an_ref[...], gbn_ref[...]), vp, vn, halo, tm)
        _shift_window(win_ref, sh_ref, tm)
        yc = _conv_taps(sh_ref, cw_ref, tm, False)
        yc_ref[...] = yc
        yh, _ = _ln_stats(yc)
        mix_ref[:, w:2 * w] = _silu(yh * lw_ref[...] + lb_ref[...]).astype(BF16)
        for h in range(HEADS):
            sl = slice(h * HEAD_DIM, (h + 1) * HEAD_DIM)
            yn, _ = _head_norm(of_ref[:, sl] + ob_ref[:, sl])
            mix_ref[:, sl] = (_silu(g_ref[:, sl]) * yn).astype(BF16)

    col = lambda j: pl.BlockSpec((tm, w), lambda i: (i, j))
    ap, an = _halo_specs(tm, halo, t, w, 4)
    gp, gn = _halo_specs(tm, halo, t, w, 5)
    row = pl.BlockSpec((tm, w), lambda i: (i, 0))
    return _pcall(body, name=name, grid=(nt,),
                  in_specs=[col(3), col(4), col(5), ap, gp, an, gn, row, row,
                            _full(cw.shape), _full(lnw.shape), _full(lnb.shape)],
                  out_specs=[pl.BlockSpec((tm, 2 * w), lambda i: (i, 0)), row],
                  out_shape=[_sds((t, 2 * w), BF16), _sds((t, w))],
                  scratch_shapes=[pltpu.VMEM((tm + 2 * halo, w), F32), pltpu.VMEM((SUBLANES, tm + 2 * halo, w), F32)],
                  compiler_params=_cp(("parallel",)))(p, p, p, p, p, p, p, of, ob, cw, lnw, lnb)


def _even_mix_bwd1(name, dmix, p, of, ob, yc, lnw, lnb):
    t = p.shape[0]
    tm = ROW_TILE
    w = HEADS * HEAD_DIM

    def body(dr_ref, dc_ref, g_ref, of_ref, ob_ref, yc_ref, lw_ref, lb_ref, do_ref, dg_ref, dyc_ref, s_ref):
        @pl.when(pl.program_id(0) == 0)
        def _():
            s_ref[...] = jnp.zeros_like(s_ref)

        for h in range(HEADS):
            sl = slice(h * HEAD_DIM, (h + 1) * HEAD_DIM)
            yn, r = _head_norm(of_ref[:, sl] + ob_ref[:, sl])
            gv = g_ref[:, sl]
            dr = dr_ref[:, sl]
            dg_ref[:, sl] = (dr * yn * _dsilu(gv)).astype(BF16)
            dyn = dr * _silu(gv)
            do_ref[:, sl] = (r * (dyn - yn * jnp.mean(dyn * yn, axis=-1, keepdims=True))).astype(BF16)
        yh, rs = _ln_stats(yc_ref[...])
        lw = lw_ref[...]
        dlo = dc_ref[...] * _dsilu(yh * lw + lb_ref[...])
        dyc_ref[...] = _ln_bwd(dlo * lw, yh, rs)
        s_ref[...] += jnp.concatenate([_colsum(dlo * yh), _colsum(dlo), jnp.zeros((6, w), F32)], axis=0)

    col = lambda j: pl.BlockSpec((tm, w), lambda i: (i, j))
    row = pl.BlockSpec((tm, w), lambda i: (i, 0))
    return _pcall(body, name=name, grid=(t // tm,),
                  in_specs=[col(0), col(1), col(3), row, row, row, _full(lnw.shape), _full(lnb.shape)],
                  out_specs=[row, row, row, _full((8, w))],
                  out_shape=[_sds((t, w), BF16), _sds((t, w), BF16), _sds((t, w)), _sds((8, w))],
                  compiler_params=_cp(("arbitrary",)))(dmix, dmix, p, of, ob, yc, lnw, lnb)


def _even_conv_bwd(name, dyc, p, cw, nct):
    t = p.shape[0]
    tm, halo = ROW_TILE, CONV_HALO
    nt = t // tm
    w = HEADS * HEAD_DIM

    def body(d_ref, dp_ref, dn_ref, a_ref, gb_ref, ap_ref, gbp_ref, an_ref, gbn_ref, cw_ref,
             da_ref, dgb_ref, dw_ref, dwin_ref, uwin_ref, dsh_ref, ush_ref):
        i = pl.program_id(0)

        @pl.when(i == 0)
        def _():
            dw_ref[...] = jnp.zeros_like(dw_ref)

        vp, vn = _halo_valid(i, nct, nt)
        glu = lambda a, b: a * _sigmoid(b)
        dcur = d_ref[...]
        _fill_window(dwin_ref, dp_ref[...], dcur, dn_ref[...], vp, vn, halo, tm)
        _fill_window(uwin_ref, glu(ap_ref[...], gbp_ref[...]), glu(a_ref[...], gb_ref[...]),
                     glu(an_ref[...], gbn_ref[...]), vp, vn, halo, tm)
        _shift_window(dwin_ref, dsh_ref, tm)
        _shift_window(uwin_ref, ush_ref, tm)
        du = _conv_taps(dsh_ref, cw_ref, tm, True)
        av = a_ref[...]
        sg = _sigmoid(gb_ref[...])
        da_ref[...] = (du * sg).astype(BF16)
        dgb_ref[...] = (du * av * sg * (1.0 - sg)).astype(BF16)
        rows = [_colsum(dcur * _window_rows(ush_ref, kk + 1, tm)) for kk in range(CONV_K)]
        dw_ref[...] += jnp.concatenate(rows + [jnp.zeros((1, w), F32)], axis=0)

    col = lambda j: pl.BlockSpec((tm, w), lambda i: (i, j))
    row = pl.BlockSpec((tm, w), lambda i: (i, 0))
    dp, dn = _halo_specs(tm, halo, t, w, 0)
    ap, an = _halo_specs(tm, halo, t, w, 4)
    gp, gn = _halo_specs(tm, halo, t, w, 5)
    win = pltpu.VMEM((tm + 2 * halo, w), F32)
    shifted = pltpu.VMEM((SUBLANES, tm + 2 * halo, w), F32)
    return _pcall(body, name=name, grid=(nt,),
                  in_specs=[row, dp, dn, col(4), col(5), ap, gp, an, gn, _full(cw.shape)],
                  out_specs=[row, row, _full((CONV_K + 1, w))],
                  out_shape=[_sds((t, w), BF16), _sds((t, w), BF16), _sds((CONV_K + 1, w))],
                  scratch_shapes=[win, win, shifted, shifted],
                  compiler_params=_cp(("arbitrary",)))(dyc, dyc, dyc, p, p, p, p, p, p, cw)


def _even_dp(name, dqs, dks, dvs, dg, da, dgb, cs, sn):
    t, w = dg.shape
    tm = ROW_TILE
    scale = HEAD_DIM ** -0.5

    def body(dqf_ref, dqb_ref, dkf_ref, dkb_ref, dvf_ref, dvb_ref, dg_ref, da_ref, dgb_ref, cs_ref, sn_ref, dp_ref):
        c, s = cs_ref[...], sn_ref[...]
        for h in range(HEADS):
            sl = slice(h * HEAD_DIM, (h + 1) * HEAD_DIM)
            dp_ref[:, sl] = (_rope_t(dqf_ref[:, sl] + dqb_ref[:, sl], c, s) * scale).astype(BF16)
            dp_ref[:, w + h * HEAD_DIM:w + (h + 1) * HEAD_DIM] = _rope_t(dkf_ref[:, sl] + dkb_ref[:, sl], c, s).astype(BF16)
        dp_ref[:, 2 * w:3 * w] = (dvf_ref[...] + dvb_ref[...]).astype(BF16)
        dp_ref[:, 3 * w:4 * w] = dg_ref[...]
        dp_ref[:, 4 * w:5 * w] = da_ref[...]
        dp_ref[:, 5 * w:6 * w] = dgb_ref[...]

    row = pl.BlockSpec((tm, w), lambda i: (i, 0))
    tab = pl.BlockSpec((tm, HEAD_DIM), lambda i: (i, 0))
    return _pcall(body, name=name, grid=(t // tm,), in_specs=[row] * 9 + [tab, tab],
                  out_specs=pl.BlockSpec((tm, 6 * w), lambda i: (i, 0)), out_shape=_sds((t, 6 * w), BF16),
                  compiler_params=_cp(("parallel",)))(dqs[0], dqs[1], dks[0], dks[1], dvs[0], dvs[1], dg, da, dgb, cs, sn)


GROUPS = 4
GC = 128
INV_SQRT2 = 0.7071067811865476
INV_SQRT_2PI = 0.3989422804014327


def _gelu(x):
    return 0.5 * x * (1.0 + lax.erf(x * INV_SQRT2))


def _dgelu(x):
    return 0.5 * (1.0 + lax.erf(x * INV_SQRT2)) + x * jnp.exp(-0.5 * x * x) * INV_SQRT_2PI


def _pool_count(i, nct, lc, t, tm, rows, row0, left, right):
    is_ctx = i < nct
    seg_start = jnp.where(is_ctx, 0, lc)
    seg_len = jnp.where(is_ctx, lc, t - lc)
    pos = i * tm + row0 - seg_start + lax.broadcasted_iota(jnp.int32, (rows, GC), 0)
    cnt = jnp.minimum(pos + right, seg_len - 1) - jnp.maximum(pos - left, 0) + 1
    return jnp.maximum(cnt, 1).astype(F32)


def _spatial_gate(vln, sgw_ref, sgb_ref, tm):
    cols = []
    for g in range(GROUPS):
        sl = slice(g * GC, (g + 1) * GC)
        parts = [_dot(sgw_ref[g], vln[r0:r0 + CHUNK, sl], NN) + sgb_ref[g] for r0 in range(0, tm, CHUNK)]
        cols.append(jnp.concatenate(parts, axis=0))
    return jnp.concatenate(cols, axis=1)


def _odd_mix(name, p, pw, pscale, lnw, lnb, sgw, sgb, nct, lc):
    t = p.shape[0]
    tm, halo = ROW_TILE, POOL_HALO
    nt = t // tm
    w = GROUPS * GC

    def body(pc_ref, pp_ref, pn_ref, pu_ref, pv_ref, pw_ref, ps_ref, lw_ref, lb_ref, sgw_ref, sgb_ref,
             mix_ref, m_ref, win_ref):
        i = pl.program_id(0)
        vp, vn = _halo_valid(i, nct, nt)
        pc = pc_ref[...]
        _fill_window(win_ref, pp_ref[...], pc, pn_ref[...], vp, vn, halo, tm)
        for g, wd in enumerate(POOL_WINDOWS):
            sl = slice(g * GC, (g + 1) * GC)
            left = wd // 2
            right = wd - 1 - left
            s = None
            for o in range(-left, right + 1):
                term = win_ref[halo + o:halo + o + tm, sl]
                s = term if s is None else s + term
            mg = s / _pool_count(i, nct, lc, t, tm, tm, 0, left, right) - pc[:, sl]
            m_ref[:, sl] = mg
            mix_ref[:, sl] = (_dot(mg, pw_ref[g], NN) * ps_ref[:, sl]).astype(BF16)
        u = _gelu(pu_ref[...])
        vh, _ = _ln_stats(_gelu(pv_ref[...]))
        s = _spatial_gate(vh * lw_ref[...] + lb_ref[...], sgw_ref, sgb_ref, tm)
        mix_ref[:, w:2 * w] = (u * s).astype(BF16)

    col = lambda j: pl.BlockSpec((tm, w), lambda i: (i, j))
    pp, pn = _halo_specs(tm, halo, t, w, 0)
    return _pcall(body, name=name, grid=(nt,),
                  in_specs=[col(0), pp, pn, col(1), col(2), _full(pw.shape), _full(pscale.shape),
                            _full(lnw.shape), _full(lnb.shape), _full(sgw.shape), _full(sgb.shape)],
                  out_specs=[pl.BlockSpec((tm, 2 * w), lambda i: (i, 0)), col(0)],
                  out_shape=[_sds((t, 2 * w), BF16), _sds((t, w))],
                  scratch_shapes=[pltpu.VMEM((tm + 2 * halo, w), F32)],
                  compiler_params=_cp(("parallel",)))(p, p, p, p, p, pw, pscale, lnw, lnb, sgw, sgb)


def _odd_mix_bwd1(name, dmix, p, m, pw, pscale, lnw, lnb, sgw, sgb):
    t = p.shape[0]
    tm = ROW_TILE
    w = GROUPS * GC

    def body(dpo_ref, dsg_ref, pu_ref, pv_ref, m_ref, pw_ref, ps_ref, lw_ref, lb_ref, sgw_ref, sgb_ref,
             dm_ref, dpd_ref, vec_ref, dpw_ref, dsgw_ref, dsgb_ref):
        @pl.when(pl.program_id(0) == 0)
        def _():
            vec_ref[...] = jnp.zeros_like(vec_ref)
            dpw_ref[...] = jnp.zeros_like(dpw_ref)
            dsgw_ref[...] = jnp.zeros_like(dsgw_ref)
            dsgb_ref[...] = jnp.zeros_like(dsgb_ref)

        dscale = []
        for g in range(GROUPS):
            sl = slice(g * GC, (g + 1) * GC)
            mg = m_ref[:, sl]
            dpo = dpo_ref[:, sl]
            dscale.append(_colsum(dpo * _dot(mg, pw_ref[g], NN)))
            dpo = dpo * ps_ref[:, sl]
            dm_ref[:, sl] = _dot(dpo, pw_ref[g], NT)
            dpw_ref[g] += _dot(mg, dpo, TN)
        pu, pv = pu_ref[...], pv_ref[...]
        u = _gelu(pu)
        vh, rs = _ln_stats(_gelu(pv))
        lw = lw_ref[...]
        vln = vh * lw + lb_ref[...]
        s = _spatial_gate(vln, sgw_ref, sgb_ref, tm)
        dsg = dsg_ref[...]
        dpd_ref[:, 0:w] = (dsg * s * _dgelu(pu)).astype(BF16)
        ds = dsg * u
        cols = []
        for g in range(GROUPS):
            sl = slice(g * GC, (g + 1) * GC)
            parts = []
            for r0 in range(0, tm, CHUNK):
                dsc = ds[r0:r0 + CHUNK, sl]
                parts.append(_dot(sgw_ref[g], dsc, TN))
                dsgw_ref[g] += _dot(dsc, vln[r0:r0 + CHUNK, sl], NT)
                dsgb_ref[g] += dsc
            cols.append(jnp.concatenate(parts, axis=0))
        dvln = jnp.concatenate(cols, axis=1)
        dpd_ref[:, w:2 * w] = (_ln_bwd(dvln * lw, vh, rs) * _dgelu(pv)).astype(BF16)
        vec_ref[...] += jnp.concatenate([jnp.concatenate(dscale, axis=1), _colsum(dvln * vh), _colsum(dvln),
                                         jnp.zeros((5, w), F32)], axis=0)

        @pl.when(pl.program_id(0) == t // tm - 1)
        def _():
            for g in range(GROUPS):
                dsgb_ref[g] = jnp.broadcast_to(jnp.sum(dsgb_ref[g], axis=1, keepdims=True), (GC, GC))

    col = lambda j: pl.BlockSpec((tm, w), lambda i: (i, j))
    mat = _full((GROUPS, GC, GC))
    return _pcall(body, name=name, grid=(t // tm,),
                  in_specs=[col(0), col(1), col(1), col(2), col(0), _full(pw.shape), _full(pscale.shape),
                            _full(lnw.shape), _full(lnb.shape), _full(sgw.shape), _full(sgb.shape)],
                  out_specs=[col(0), pl.BlockSpec((tm, 2 * w), lambda i: (i, 0)), _full((8, w)), mat, mat, mat],
                  out_shape=[_sds((t, w)), _sds((t, 2 * w), BF16), _sds((8, w)),
                             _sds((GROUPS, GC, GC)), _sds((GROUPS, GC, GC)), _sds((GROUPS, GC, GC))],
                  compiler_params=_cp(("arbitrary",)))(dmix, dmix, p, p, m, pw, pscale, lnw, lnb, sgw, sgb)


def _odd_dp(name, dm, dpd, nct, lc):
    t, w = dm.shape
    tm, halo = ROW_TILE, POOL_HALO
    nt = t // tm

    def body(d_ref, dp_ref, dn_ref, dpd_ref, o_ref, win_ref):
        i = pl.program_id(0)
        vp, vn = _halo_valid(i, nct, nt)
        dcur = d_ref[...]
        _fill_window(win_ref, dp_ref[...], dcur, dn_ref[...], vp, vn, halo, tm)
        for g, wd in enumerate(POOL_WINDOWS):
            sl = slice(g * GC, (g + 1) * GC)
            left = wd // 2
            right = wd - 1 - left
            win_ref[:, sl] = win_ref[:, sl] / _pool_count(i, nct, lc, t, tm, tm + 2 * halo, -halo, left, right)
            s = None
            for o in range(-right, left + 1):
                term = win_ref[halo + o:halo + o + tm, sl]
                s = term if s is None else s + term
            o_ref[:, sl] = (s - dcur[:, sl]).astype(BF16)
        o_ref[:, w:3 * w] = dpd_ref[...]

    row = pl.BlockSpec((tm, w), lambda i: (i, 0))
    pp, pn = _halo_specs(tm, halo, t, w, 0)
    return _pcall(body, name=name, grid=(nt,),
                  in_specs=[row, pp, pn, pl.BlockSpec((tm, 2 * w), lambda i: (i, 0))],
                  out_specs=pl.BlockSpec((tm, 3 * w), lambda i: (i, 0)), out_shape=_sds((t, 3 * w), BF16),
                  scratch_shapes=[pltpu.VMEM((tm + 2 * halo, w), F32)],
                  compiler_params=_cp(("parallel",)))(dm, dm, dm, dpd)


def _place():
    x, y, c = lax.axis_index("x"), lax.axis_index("y"), lax.axis_index("c")
    chips = [(1 - x, y), (x, 1 - y), (1 - x, 1 - y)]
    return x, y, c, chips


def _chip_index(cx, cy):
    return 2 * cx + cy


def _all_gather8(name, blk, after=()):
    m_per, n = blk.shape
    na = len(after)

    def body(x_ref, *rest):
        out_ref, send_sems, recv_sems, local_sem = rest[na:]
        x, y, c, chips = _place()
        me, sibling = (x, y, c), (x, y, 1 - c)

        def rows(px, py, pc):
            return out_ref.at[pl.ds((4 * px + 2 * py + pc) * m_per, m_per), :]

        def copy(k, block, to, src=None):
            return pltpu.make_async_remote_copy(
                src_ref=rows(*block) if src is None else src, dst_ref=rows(*block),
                send_sem=send_sems.at[k], recv_sem=recv_sems.at[k], device_id=to, device_id_type=MESH)

        mine = pltpu.make_async_copy(x_ref, rows(*me), local_sem)
        mine.start()
        first = [copy(0, me, sibling, src=x_ref)]
        first += [copy(1 + j, me, (*chip, c), src=x_ref) for j, chip in enumerate(chips)]
        for cp in first:
            cp.start()
        passed = [copy(4 + j, (*chip, c), sibling) for j, chip in enumerate(chips)]
        for j, chip in enumerate(chips):
            copy(1 + j, (*chip, c), me).wait_recv()
            passed[j].start()
        copy(0, sibling, me).wait_recv()
        for j, chip in enumerate(chips):
            copy(4 + j, (*chip, 1 - c), me).wait_recv()
        for cp in first + passed:
            cp.wait_send()
        mine.wait()

    return _pcall(body, name=name, out_shape=_sds((8 * m_per, n), blk.dtype),
                  in_specs=[pl.BlockSpec(memory_space=pltpu.VMEM)] + [pl.BlockSpec(memory_space=pl.ANY)] * na,
                  out_specs=pl.BlockSpec(memory_space=pltpu.VMEM),
                  scratch_shapes=[pltpu.SemaphoreType.DMA((7,)), pltpu.SemaphoreType.DMA((7,)), pltpu.SemaphoreType.DMA],
                  compiler_params=_cp())(blk, *after)


ANY = pl.BlockSpec(memory_space=pl.ANY)


def _half(which, rows):
    return pl.ds(pl.multiple_of(which * rows, 16), rows)


def _gather_weights(name, ws, after=()):
    nw = len(ws)
    na = len(after)
    ns = 7

    def body(*refs):
        w_refs, o_refs = refs[:nw], refs[nw + na:2 * nw + na]
        send_sems, recv_sems = refs[2 * nw + na:]
        x, y, c, chips = _place()
        me_chip = _chip_index(x, y)
        sibling = (x, y, 1 - c)

        def rcopy(t, k, src, dst, to):
            return pltpu.make_async_remote_copy(src_ref=src, dst_ref=dst, send_sem=send_sems.at[t * ns + k],
                                                recv_sem=recv_sems.at[t * ns + k], device_id=to, device_id_type=MESH)

        sends = []
        for t in range(nw):
            lh = w_refs[t].shape[0] // 2
            for k, chip in enumerate(chips):
                sends.append(rcopy(t, k, w_refs[t].at[_half(c, lh)], o_refs[t].at[me_chip, _half(c, lh)], (*chip, c)))
                sends[-1].start()
            sends.append(rcopy(t, 6, w_refs[t], o_refs[t].at[me_chip], sibling))
            sends[-1].start()
        for t in range(nw):
            lh = w_refs[t].shape[0] // 2
            for k, chip in enumerate(chips):
                part = o_refs[t].at[_chip_index(*chip), _half(c, lh)]
                rcopy(t, k, part, part, (*chip, c)).wait_recv()
                sends.append(rcopy(t, 3 + k, part, part, sibling))
                sends[-1].start()
        for t in range(nw):
            lh = w_refs[t].shape[0] // 2
            own = o_refs[t].at[me_chip]
            rcopy(t, 6, own, own, sibling).wait_recv()
            for k, chip in enumerate(chips):
                part = o_refs[t].at[_chip_index(*chip), _half(1 - c, lh)]
                rcopy(t, 3 + k, part, part, sibling).wait_recv()
        for cp in sends:
            cp.wait_send()

    return _pcall(body, name=name, out_shape=[_sds((4,) + w.shape, w.dtype) for w in ws],
                  in_specs=[ANY] * (nw + na), out_specs=[ANY] * nw,
                  scratch_shapes=[pltpu.SemaphoreType.DMA((ns * nw,)), pltpu.SemaphoreType.DMA((ns * nw,))],
                  compiler_params=_cp())(*ws, *after)


def _rs_share(name, ss):
    ng = len(ss)

    def body(*refs):
        o_refs = refs[ng:2 * ng]
        send_sems, recv_sems = refs[2 * ng:]
        x, y, c, _ = _place()
        cps = []
        for t in range(ng):
            lh = o_refs[t].shape[1] // 2
            mine = o_refs[t].at[:, _half(c, lh)]
            cp = pltpu.make_async_remote_copy(
                src_ref=mine, dst_ref=mine, send_sem=send_sems.at[t], recv_sem=recv_sems.at[t],
                device_id=(x, y, 1 - c), device_id_type=MESH)
            cp.start()
            cps.append(cp)
        for t in range(ng):
            lh = o_refs[t].shape[1] // 2
            cps[t].wait_send()
            theirs = o_refs[t].at[:, _half(1 - c, lh)]
            pltpu.make_async_remote_copy(
                src_ref=theirs, dst_ref=theirs, send_sem=send_sems.at[t], recv_sem=recv_sems.at[t],
                device_id=(x, y, 1 - c), device_id_type=MESH).wait_recv()

    return _pcall(body, name=name, out_shape=[_sds(s.shape, s.dtype) for s in ss],
                  in_specs=[ANY] * ng, out_specs=[ANY] * ng, input_output_aliases={t: t for t in range(ng)},
                  scratch_shapes=[pltpu.SemaphoreType.DMA((ng,)), pltpu.SemaphoreType.DMA((ng,))],
                  compiler_params=_cp())(*ss)


HBM = pl.BlockSpec(memory_space=pltpu.HBM)
SEMS = pl.BlockSpec(memory_space=pltpu.SEMAPHORE)
EFFECT = pltpu.SideEffectType.DATAFLOW_SIDE_EFFECTING
TOKEN = (8, 128)


def _in_hbm(a):
    return pltpu.with_memory_space_constraint(a, pltpu.HBM)


def _split_start(name, srcs, lands, copies, after):
    ns, nl, na = len(srcs), len(lands), len(after)
    ncopies = len(copies([s for s in srcs], [l for l in lands], probe=True))

    def body(*refs):
        src_refs, land_refs = refs[:ns], refs[ns:ns + nl]
        send_sems, recv_sems = refs[ns + nl + na], refs[ns + nl + na + 1]
        token = refs[-1]
        for k, (src, dst, to) in enumerate(copies(src_refs, land_refs)):
            pltpu.make_async_remote_copy(src_ref=src, dst_ref=dst, send_sem=send_sems.at[k], recv_sem=recv_sems.at[k],
                                         device_id=to, device_id_type=MESH).start()
        token[...] = jnp.zeros_like(token)

    thru = [pltpu.HBM(a.shape, a.dtype) for a in list(srcs) + list(lands)]
    outs = _pcall(body, name=name,
                  out_shape=(pltpu.SemaphoreType.DMA((ncopies,)), pltpu.SemaphoreType.DMA((ncopies,)), *thru, _sds(TOKEN)),
                  in_specs=[HBM] * (ns + nl) + [ANY] * na,
                  out_specs=(SEMS, SEMS, *([HBM] * (ns + nl)), pl.BlockSpec(memory_space=pltpu.VMEM)),
                  input_output_aliases={t: 2 + t for t in range(ns + nl)},
                  compiler_params=pltpu.CompilerParams(has_side_effects=EFFECT))(
        *[_in_hbm(a) for a in list(srcs) + list(lands)], *after)
    return outs[0], outs[1], list(outs[2:2 + ns]), list(outs[2 + ns:2 + ns + nl]), outs[-1]


def _split_wait(name, started, copies, after, first=0):
    send_sems, recv_sems, srcs, lands, _ = started
    ns, nl, na = len(srcs), len(lands), len(after)

    def body(*refs):
        src_refs, land_refs = refs[:ns], refs[ns:ns + nl]
        send_sems_ref, recv_sems_ref = refs[ns + nl], refs[ns + nl + 1]
        for k, (src, dst, to) in enumerate(copies(src_refs, land_refs)):
            cp = pltpu.make_async_remote_copy(src_ref=src, dst_ref=dst, send_sem=send_sems_ref.at[first + k],
                                              recv_sem=recv_sems_ref.at[first + k], device_id=to, device_id_type=MESH)
            cp.wait_send()
            cp.wait_recv()

    thru = [pltpu.HBM(a.shape, a.dtype) for a in list(srcs) + list(lands)]
    outs = _pcall(body, name=name, out_shape=tuple(thru),
                  in_specs=[HBM] * (ns + nl) + [SEMS, SEMS] + [ANY] * na, out_specs=tuple([HBM] * (ns + nl)),
                  input_output_aliases={t: t for t in range(ns + nl)},
                  compiler_params=pltpu.CompilerParams(has_side_effects=EFFECT))(
        *srcs, *lands, send_sems, recv_sems, *after)
    return list(outs[:ns]), list(outs[ns:])


def _pair_copies(n):
    def copies(src_refs, land_refs, probe=False):
        if probe:
            return [None] * n
        x, y, c, _ = _place()
        return [(src_refs[t].at[:, _half(1 - c, src_refs[t].shape[1] // 2)], land_refs[t], (x, y, 1 - c))
                for t in range(n)]
    return copies


def _gather_copies(n):
    def copies(src_refs, land_refs, probe=False):
        if probe:
            return [None] * (4 * n)
        x, y, c, chips = _place()
        me_chip = _chip_index(x, y)
        out = []
        for t in range(n):
            for to in [(*chip, c) for chip in chips] + [(x, y, 1 - c)]:
                out.append((src_refs[t], land_refs[t].at[me_chip], to))
        return out
    return copies


def _scatter_copies(n):
    def copies(src_refs, land_refs, probe=False):
        if probe:
            return [None] * (3 * n)
        x, y, c, chips = _place()
        out = []
        for t in range(n):
            for k, chip in enumerate(chips):
                out.append((src_refs[t].at[_chip_index(*chip)], land_refs[t].at[k], (*chip, c)))
        return out
    return copies


def _row_block(r, cn):
    if r % 8:
        return r
    best = 8
    for d in range(8, r + 1, 8):
        if r % d == 0 and d * cn * 4 <= (2 << 20):
            best = d
    return best


def _add_half(name, gl, al, idx):
    n = len(gl)
    j, rh, cn = al[0].shape
    tr = _row_block(rh, cn)
    nb = rh // tr

    def body(i_ref, *refs):
        for t in range(n):
            refs[2 * n + t][...] = (refs[t][...] + refs[n + t][...]).astype(BF16)

    blk = (None, tr, cn)
    gspec = pl.BlockSpec(blk, lambda jj, i, i_ref: (jj, i_ref[0] * nb + i, 0))
    aspec = pl.BlockSpec(blk, lambda jj, i, i_ref: (jj, i, 0))
    gs = pltpu.PrefetchScalarGridSpec(num_scalar_prefetch=1, grid=(j, nb), in_specs=[gspec] * n + [aspec] * n,
                                      out_specs=[aspec] * n)
    return _pcall(body, name=name, grid_spec=gs, out_shape=[_sds(al[0].shape, BF16)] * n,
                  compiler_params=_cp(("parallel", "parallel")))(idx, *gl, *al)


def _sum_final(name, gl, al, bl, idx, bufs, lyr, nlyr):
    n = len(gl)
    _, r, cn = gl[0].shape
    rh = r // 2
    tr = _row_block(rh, cn)
    nb = rh // tr
    has = bufs[0] is not None

    def body(i_ref, *refs):
        outs = refs[len(refs) - n:]
        for t in range(n):
            own = refs[t][...] + refs[n + t][...]
            b_ref = refs[2 * n + t]
            outs[t][...] = (own + b_ref[0].astype(F32)) + (b_ref[1].astype(F32) + b_ref[2].astype(F32))

    blk = (None, tr, cn)
    in_specs = ([pl.BlockSpec(blk, lambda i, i_ref: (i_ref[1], i_ref[0] * nb + i, 0))] * n
                + [pl.BlockSpec(blk, lambda i, i_ref: (i_ref[1], i, 0))] * n
                + [pl.BlockSpec((3, tr, cn), lambda i, i_ref: (0, i, 0))] * n)
    args = [idx, *gl, *al, *bl]
    kw = {}
    if has:
        in_specs += [ANY] * n
        args += list(bufs)
        kw["input_output_aliases"] = {1 + 3 * n + t: t for t in range(n)}
    gs = pltpu.PrefetchScalarGridSpec(
        num_scalar_prefetch=1, grid=(nb,), in_specs=in_specs,
        out_specs=[pl.BlockSpec(blk, lambda i, i_ref: (lyr, i_ref[0] * nb + i, 0))] * n)
    return _pcall(body, name=name, grid_spec=gs, out_shape=[_sds((nlyr, r, cn))] * n,
                  compiler_params=_cp(("parallel",)), **kw)(*args)


def _sum8(name, g):
    _, r, n = g.shape
    tr = 8

    def body(g_ref, o_ref):
        o_ref[...] = ((g_ref[0] + g_ref[1]) + (g_ref[2] + g_ref[3])) + ((g_ref[4] + g_ref[5]) + (g_ref[6] + g_ref[7]))

    return _pcall(body, name=name, grid=(r // tr,), in_specs=[pl.BlockSpec((8, tr, n), lambda i: (0, i, 0))],
                  out_specs=pl.BlockSpec((tr, n), lambda i: (i, 0)), out_shape=_sds((r, n)),
                  compiler_params=_cp(("parallel",)))(g)


def _ada_mod(name, c16, ada_w, bias):
    nl, dm, n = ada_w.shape

    def body(c_ref, w_ref, b_ref, o_ref):
        o_ref[...] = _dot(_silu(c_ref[...]), w_ref[...], NN) + b_ref[...]

    return _pcall(body, name=name, grid=(nl,),
                  in_specs=[_full(c16.shape), pl.BlockSpec((None, dm, n), lambda i: (i, 0, 0)),
                            pl.BlockSpec((None, 1, n), lambda i: (i, 0, 0))],
                  out_specs=pl.BlockSpec((None, 16, n), lambda i: (i, 0, 0)), out_shape=_sds((nl, 16, n)),
                  compiler_params=_cp(("parallel",)))(c16, ada_w, bias)


def _ada_bwd(name, c16, dmod, ada_w):
    nl, dm, n = ada_w.shape

    def body(c_ref, d_ref, w_ref, gw_ref, dc_ref):
        @pl.when(pl.program_id(0) == 0)
        def _():
            dc_ref[...] = jnp.zeros_like(dc_ref)

        dv = d_ref[...]
        gw_ref[...] = _dot(_silu(c_ref[...]), dv, TN)
        dc_ref[...] += _dot(dv, w_ref[...], NT)

    return _pcall(body, name=name, grid=(nl,),
                  in_specs=[_full(c16.shape), pl.BlockSpec((None, 16, n), lambda i: (i, 0, 0)),
                            pl.BlockSpec((None, dm, n), lambda i: (i, 0, 0))],
                  out_specs=[pl.BlockSpec((None, dm, n), lambda i: (i, 0, 0)), _full((16, dm))],
                  out_shape=[_sds((nl, dm, n)), _sds((16, dm))],
                  compiler_params=_cp(("arbitrary",)))(c16, dmod, ada_w)


def _rowsum16(name, dmod):
    nl, _, n = dmod.shape

    def body(d_ref, o_ref):
        o_ref[...] = _colsum(d_ref[...])

    return _pcall(body, name=name, grid=(nl,), in_specs=[pl.BlockSpec((None, 16, n), lambda i: (i, 0, 0))],
                  out_specs=pl.BlockSpec((None, 1, n), lambda i: (i, 0, 0)), out_shape=_sds((nl, 1, n)),
                  compiler_params=_cp(("parallel",)))(dmod)


def _cctx_grad(name, parts, c_ctx):
    def body(p_ref, c_ref, o_ref):
        tot = (p_ref[0:1, :] + p_ref[1:2, :]) + (p_ref[2:3, :] + p_ref[3:4, :])
        o_ref[...] = tot * _dsilu(c_ref[...])

    return _pcall(body, name=name, out_shape=_sds(c_ctx.shape), compiler_params=_cp())(parts, c_ctx)


def _adamw(name, w, g, m, v, with_grad=False):
    shape = w.shape
    cn = shape[-1]
    r = math.prod(shape[:-1]) if len(shape) > 1 else 1
    tr = _row_block(r, cn)
    c1 = 1.0 - ADAM_B1 ** ADAM_STEP
    c2 = 1.0 - ADAM_B2 ** ADAM_STEP
    nout = 4 if with_grad else 3

    def body(w_ref, g_ref, m_ref, v_ref, d_ref, mo_ref, vo_ref, *rest):
        gv = g_ref[...]
        mn = ADAM_B1 * m_ref[...] + (1.0 - ADAM_B1) * gv
        vn = ADAM_B2 * v_ref[...] + (1.0 - ADAM_B2) * (gv * gv)
        d_ref[...] = -ADAM_LR * ((mn / c1) / (jnp.sqrt(vn / c2) + ADAM_EPS) + ADAM_WD * w_ref[...])
        mo_ref[...] = mn
        vo_ref[...] = vn
        if with_grad:
            rest[0][...] = gv

    blk = pl.BlockSpec((tr, cn), lambda i: (i, 0))
    o = _sds((r, cn))
    outs = _pcall(body, name=name, grid=(r // tr,), in_specs=[blk] * 4, out_specs=[blk] * nout, out_shape=[o] * nout,
                  compiler_params=_cp(("parallel",)))(*[a.reshape(r, cn) for a in (w, g, m, v)])
    return tuple(a.reshape(shape) for a in outs)


def _local_step(xs, target, modt, nw, fnw, get_w, get_ffn, put_g, ev, od, lc):
    t, dm = xs.shape
    nct = lc // ROW_TILE
    depth = nw.shape[0]
    cs, sn = _rope_tables(t, lc)
    saved = []
    x_in, x1p, fp = xs, None, None
    for i in range(depth):
        j, even = i // 2, i % 2 == 0
        tag = f"l{i}"
        w, deps = get_w(i, [fp] if i else [])
        if i == 0:
            _, h = _rnm(tag + "_norm1", x_in, None, None, 0, modt[0], 0, 1, nw[0, 0], nct, deps)
        else:
            x_in, h = _rnm(tag + "_norm1", x1p, fp, modt[i - 1], 5, modt[i], 0, 1, nw[i, 0], nct, deps)
        s = dict(x=x_in, h=h, w=w)
        if even:
            p = _mm_cols(tag + "_in", h, w["in"])
            q, k, v = _even_qkv(tag + "_qkv", p, cs, sn)
            of, ob, ss = _retention_fwd(tag + "_ret", q, k, v, ev["lgb"][j], lc)
            mix, yc = _even_mix(tag + "_mix", p, of, ob, ev["cw"][j], ev["lnw"][j], ev["lnb"][j], nct)
            y = _mm_full(tag + "_out", mix, w["out"], NN)
            s.update(p=p, q=q, k=k, v=v, of=of, ob=ob, ss=ss, yc=yc)
        else:
            p = _mm_cols(tag + "_in", h, w["in"])
            mix, m = _odd_mix(tag + "_mix", p, od["pw"][j], od["ps"][j], od["lnw"][j], od["lnb"][j],
                              od["sgw"][j], od["sgb"][j], nct, lc)
            y = _mm_full(tag + "_out", mix, w["out"], NN)
            s.update(p=p, m=m)
        x1, h2 = _rnm(tag + "_norm2", x_in, y, modt[i], 2, modt[i], 3, 4, nw[i, 1], nct)
        w.update(get_ffn(i, [y]))
        a, gt, up = _ffn_up(tag + "_ffn_up", h2, w["gate"], w["up"])
        f = _mm_full(tag + "_ffn_down", a, w["down"], NN)
        s.update(mix=mix, y=y, x1=x1, h2=h2, a=a, gt=gt, up=up, f=f)
        saved.append(s)
        x1p, fp = x1, f

    loss_blk, dx, df, fin_s = _fin("final", x1p, fp, modt[depth - 1], 5, fnw, target, nct)

    deps = []
    dmod = [[None] * 6 for _ in range(depth)]
    dnw = [[None, None] for _ in range(depth)]
    zero2 = jnp.zeros((2, dm), F32)
    dmod[depth - 1][5] = jnp.stack([zero2[0], fin_s[0]])
    small = dict(dfnw=fin_s[1], ev=[], od=[])
    for i in reversed(range(depth)):
        j, even = i // 2, i % 2 == 0
        tag = f"l{i}b"
        s = saved[i]
        w = s["w"]
        fh = w["down"].shape[0] // 2
        g = {}
        dgt, dup = _ffn_down_bwd(tag + "_ffn_down", df, w["down"], s["gt"], s["up"])
        g["down"] = _wgrad_rows(tag + "_gdown", s["a"], fh, df)
        g["gate"] = _wgrad_rows(tag + "_ggate", dgt, fh, s["h2"])
        g["up"] = _wgrad_rows(tag + "_gup", dup, fh, s["h2"])
        deps = put_g(i, "f", g)
        dh2 = _ffn_in_bwd(tag + "_ffn_in", dgt, dup, w["gate"], w["up"])
        dx1, dy, s2 = _bnm(tag + "_norm2", s["x1"], dh2, dx, s["y"], modt[i], 3, 4, modt[i], 2, nw[i, 1], nct, deps)
        dmod[i][3], dmod[i][4], dmod[i][2] = s2[:, 0], s2[:, 1], s2[:, 2]
        dnw[i][1] = s2[1, 3]
        dmix = _mm_full(tag + "_out", dy, w["out"], NT)
        g["out"] = _wgrad_rows(tag + "_gout", s["mix"], w["out"].shape[0] // 2, dy)
        if even:
            do, dg, dyc, lns = _even_mix_bwd1(tag + "_mix1", dmix, s["p"], s["of"], s["ob"], s["yc"],
                                              ev["lnw"][j], ev["lnb"][j])
            da, dgb, dcw = _even_conv_bwd(tag + "_conv", dyc, s["p"], ev["cw"][j], nct)
            dqf, dkf, dvf, dqb, dkb, dvb, dl = _retention_bwd(tag + "_ret", s["q"], s["k"], s["v"], do, s["ss"],
                                                              ev["lgb"][j], lc)
            dp = _even_dp(tag + "_dp", (dqf, dqb), (dkf, dkb), (dvf, dvb), dg, da, dgb, cs, sn)
            small["ev"].append(dict(lnw=lns[0], lnb=lns[1], cw=dcw, dl=dl[:, 0]))
        else:
            dm_, dpd, vec, dpw, dsgw, dsgb = _odd_mix_bwd1(tag + "_mix1", dmix, s["p"], s["m"], od["pw"][j], od["ps"][j],
                                                           od["lnw"][j], od["lnb"][j], od["sgw"][j], od["sgb"][j])
            dp = _odd_dp(tag + "_dp", dm_, dpd, nct, lc)
            small["od"].append(dict(ps=vec[0], lnw=vec[1], lnb=vec[2], pw=dpw, sgw=dsgw, sgb=dsgb[:, :, 0]))
        dh = _mm_cols_bwd(tag + "_in", dp, w["in"])
        g["in"] = _wgrad_cols(tag + "_gin", s["h"], dp, w["in"].shape[0])
        deps = put_g(i, "m", g)
        if i > 0:
            dx, df, s1 = _bnm(tag + "_norm1", s["x"], dh, dx1, saved[i - 1]["f"], modt[i], 0, 1, modt[i - 1], 5,
                              nw[i, 0], nct, deps)
            dmod[i - 1][5] = s1[:, 2]
        else:
            dx, _, s1 = _bnm(tag + "_norm1", s["x"], dh, dx1, None, modt[0], 0, 1, None, 0, nw[0, 0], nct, deps)
        dmod[i][0], dmod[i][1] = s1[:, 0], s1[:, 1]
        dnw[i][0] = s1[1, 3]
    small["ev"].reverse()
    small["od"].reverse()
    dmod_t = jnp.stack([jnp.concatenate([jnp.stack(rows, axis=1), jnp.zeros((2, 2, dm), F32)], axis=1) for rows in dmod])
    small["dmod"] = dmod_t
    small["dnw"] = jnp.stack([jnp.stack(r) for r in dnw])
    return loss_blk, dx, small


WEIGHTS = ["c_ctx", "ada_w", "ada_b", "norm_w", "even_w_in", "even_w_out", "ret_decay_logit", "conv_dw_w",
           "conv_ln_w", "conv_ln_b", "odd_w_in", "odd_w_out", "pool_w", "pool_scale", "sg_ln_w", "sg_ln_b",
           "sg_w", "sg_b", "ffn_w_gate", "ffn_w_up", "ffn_w_down", "final_norm_w"]
BIG = dict(even_in="even_w_in", even_out="even_w_out", odd_in="odd_w_in", odd_out="odd_w_out",
           gate="ffn_w_gate", up="ffn_w_up", down="ffn_w_down")


def _rows(a, width=1024):
    flat = a.reshape(-1)
    n = flat.shape[0]
    per = 8 * width
    tot = -(-n // per) * per
    return jnp.pad(flat, (0, tot - n)).reshape(tot // width, width)


def _unshard(parts, lead):
    nl = len(lead)
    perm = tuple(range(1, nl + 1)) + (0, nl + 1)
    return parts.transpose(perm).reshape(tuple(lead) + (4 * parts.shape[-1],))


def _my_cols(a, chip, n):
    start = (0,) * (a.ndim - 1) + (chip * n,)
    return lax.dynamic_slice(a, start, a.shape[:-1] + (n,))


def kernel(x, c, ctx, c_ctx, ada_w, ada_b, norm_w, even_w_in, even_w_out, ret_decay_logit, conv_dw_w, conv_ln_w, conv_ln_b, odd_w_in, odd_w_out, pool_w, pool_scale, sg_ln_w, sg_ln_b, sg_w, sg_b, ffn_w_gate, ffn_w_up, ffn_w_down, final_norm_w, loss_target, m_c_ctx, m_ada_w, m_ada_b, m_norm_w, m_even_w_in, m_even_w_out, m_ret_decay_logit, m_conv_dw_w, m_conv_ln_w, m_conv_ln_b, m_odd_w_in, m_odd_w_out, m_pool_w, m_pool_scale, m_sg_ln_w, m_sg_ln_b, m_sg_w, m_sg_b, m_ffn_w_gate, m_ffn_w_up, m_ffn_w_down, m_final_norm_w, v_c_ctx, v_ada_w, v_ada_b, v_norm_w, v_even_w_in, v_even_w_out, v_ret_decay_logit, v_conv_dw_w, v_conv_ln_w, v_conv_ln_b, v_odd_w_in, v_odd_w_out, v_pool_w, v_pool_scale, v_sg_ln_w, v_sg_ln_b, v_sg_w, v_sg_b, v_ffn_w_gate, v_ffn_w_up, v_ffn_w_down, v_final_norm_w):
    wv = dict(c_ctx=c_ctx, ada_w=ada_w, ada_b=ada_b, norm_w=norm_w, even_w_in=even_w_in, even_w_out=even_w_out,
              ret_decay_logit=ret_decay_logit, conv_dw_w=conv_dw_w, conv_ln_w=conv_ln_w, conv_ln_b=conv_ln_b,
              odd_w_in=odd_w_in, odd_w_out=odd_w_out, pool_w=pool_w, pool_scale=pool_scale, sg_ln_w=sg_ln_w,
              sg_ln_b=sg_ln_b, sg_w=sg_w, sg_b=sg_b, ffn_w_gate=ffn_w_gate, ffn_w_up=ffn_w_up,
              ffn_w_down=ffn_w_down, final_norm_w=final_norm_w)
    mv = dict(zip(WEIGHTS, (m_c_ctx, m_ada_w, m_ada_b, m_norm_w, m_even_w_in, m_even_w_out, m_ret_decay_logit,
                            m_conv_dw_w, m_conv_ln_w, m_conv_ln_b, m_odd_w_in, m_odd_w_out, m_pool_w, m_pool_scale,
                            m_sg_ln_w, m_sg_ln_b, m_sg_w, m_sg_b, m_ffn_w_gate, m_ffn_w_up, m_ffn_w_down,
                            m_final_norm_w)))
    vv = dict(zip(WEIGHTS, (v_c_ctx, v_ada_w, v_ada_b, v_norm_w, v_even_w_in, v_even_w_out, v_ret_decay_logit,
                            v_conv_dw_w, v_conv_ln_w, v_conv_ln_b, v_odd_w_in, v_odd_w_out, v_pool_w, v_pool_scale,
                            v_sg_ln_w, v_sg_ln_b, v_sg_w, v_sg_b, v_ffn_w_gate, v_ffn_w_up, v_ffn_w_down,
                            v_final_norm_w)))
    xi, yi, ci = lax.axis_index("x"), lax.axis_index("y"), lax.axis_index("c")
    chip = 2 * xi + yi
    dev = 4 * xi + 2 * yi + ci
    dm = x.shape[-1]
    lc = ctx.shape[1]
    depth = ada_w.shape[0]
    n_ada = ada_w.shape[-1]

    cw_pad = jnp.pad(conv_dw_w, ((0, 0), (0, 1), (0, 0)))
    vec3 = jnp.stack([pool_scale, sg_ln_w, sg_ln_b])
    pack1 = jnp.concatenate([_rows(c), _rows(norm_w), _rows(cw_pad), _rows(vec3)], axis=0)
    g1 = _all_gather8("gather_small", pack1).reshape(8, 32, dm)
    c_all = g1[:, 0]
    per_chip = g1[0::2]
    norm_full = _unshard(per_chip[:, 8:10].reshape(4, depth, 2, dm // 4), (depth, 2))
    cw_full = _unshard(per_chip[:, 16:24].reshape(4, 2, CONV_K + 1, 128), (2, CONV_K + 1))
    vec_full = _unshard(per_chip[:, 24, :768].reshape(4, 3, 2, 128), (3, 2))

    c16 = jnp.concatenate([c_all, c_ctx[None, :], jnp.zeros((7, dm), F32)], axis=0)
    mod_sh = _ada_mod("ada_mod", c16, ada_w, _my_cols(ada_b, chip, n_ada)[:, None, :])
    g2 = _all_gather8("gather_mod", mod_sh.reshape(depth * 16, n_ada)).reshape(8, depth, 16, n_ada)
    mod_full = _unshard(g2[0::2], (depth, 16))
    mod_x = lax.dynamic_index_in_dim(mod_full, dev, axis=1, keepdims=False).reshape(depth, 6, dm)
    mod_c = mod_full[:, 8].reshape(depth, 6, dm)
    modt = jnp.pad(jnp.stack([mod_c, mod_x], axis=1), ((0, 0), (0, 0), (0, 2), (0, 0)))

    names = list(BIG)
    tr_names = ("gate", "up")
    shard = {k: (jnp.swapaxes(wv[BIG[k]], 1, 2) if k in tr_names else wv[BIG[k]]).astype(BF16) for k in names}
    roles = ("in", "out", "gate", "up", "down")

    def layer_keys(i):
        mixer = ("even_in", "even_out") if i % 2 == 0 else ("odd_in", "odd_out")
        return [(k, i // 2) for k in mixer] + [(k, i) for k in ("gate", "up", "down")]

    def as_used(got):
        return {r: (g if r == "in" else g.reshape(4 * g.shape[1], g.shape[2])) for r, g in zip(roles, got)}

    started = {}

    def get_w(i, after):
        if i > 0:
            part, first = started[i, "m"]
            got = _split_wait(f"gather_wait{i}m", part, _gather_copies(2), after, first)[1]
            return as_used(got), []
        got = _gather_weights("gather_w0", [shard[k][l] for k, l in layer_keys(0)[:2]], [modt])
        parts = [(li, p, layer_keys(li)[:2] if p == "m" else layer_keys(li)[2:])
                 for li in range(depth) for p in "mf" if (li, p) != (0, "m")]
        srcs = [shard[k][l] for _, _, keys in parts for k, l in keys]
        lands = [lax.empty((4,) + s.shape, s.dtype) for s in srcs]
        send_sems, recv_sems, srcs, lands, token = _split_start("gather_start", srcs, lands, _gather_copies(len(srcs)), [got[0]])
        t0 = 0
        for li, p, keys in parts:
            t1 = t0 + len(keys)
            started[li, p] = ((send_sems, recv_sems, srcs[t0:t1], lands[t0:t1], token), 4 * t0)
            t0 = t1
        return as_used(got), [token]

    def get_ffn(i, after):
        part, first = started[i, "f"]
        got = _split_wait(f"gather_wait{i}f", part, _gather_copies(3), after, first)[1]
        return {r: g.reshape(4 * g.shape[1], g.shape[2]) for r, g in zip(roles[2:], got)}

    idx = jnp.stack([ci, chip]).astype(jnp.int32)
    pairs, pending, stages = {}, {}, []

    def stage_keys(stage):
        i, part = stage
        return layer_keys(i)[2:] if part == "f" else layer_keys(i)[:2]

    def finish_pair(stage, after, glist=()):
        tag = f"{stage[0]}{stage[1]}"
        n, m = len(stage_keys(stage)), len(glist)
        part, first = pairs[stage]
        g_prev, from_sib = _split_wait(f"pair_wait{tag}", part, _pair_copies(n), after, first)
        if stage[1] == "f":
            pair = _add_half(f"rs_add{tag}", g_prev, from_sib, idx)
        else:
            pair = [_add_half(f"rs_add{tag}_{t}", [gl], [a], idx)[0] for t, (gl, a) in enumerate(zip(g_prev, from_sib))]
        lands = ([lax.empty((3,) + p.shape[1:], p.dtype) for p in pair]
                 + [lax.empty((4, gl.shape[1] // 2, gl.shape[2]), gl.dtype) for gl in glist])

        def copies(src_refs, land_refs, probe=False):
            return (_scatter_copies(n)(src_refs[:n], land_refs[:n], probe=probe)
                    + (_pair_copies(m)(src_refs[n:], land_refs[n:], probe=probe) if m else []))

        send_sems, recv_sems, srcs, lands, token = _split_start(f"rs_start{tag}", list(pair) + list(glist), lands, copies, [])
        pending[stage] = (g_prev, from_sib, (send_sems, recv_sems, srcs[:n], lands[:n], token))
        return (send_sems, recv_sems, srcs[n:], lands[n:], token), 3 * n

    def put_g(i, part, g):
        stage = (i, part)
        glist = [g[r].reshape(4, -1, g[r].shape[-1]) for r in (roles[2:] if part == "f" else roles[:2])]
        if stages:
            pairs[stage] = finish_pair(stages[-1], [glist[0]], glist)
        else:
            lands = [lax.empty((4, gl.shape[1] // 2, gl.shape[2]), gl.dtype) for gl in glist]
            pairs[stage] = (_split_start(f"pair_start{i}{part}", glist, lands, _pair_copies(len(glist)), []), 0)
        stages.append(stage)
        return [pairs[stage][0][4]]

    ev = dict(lgb=jnp.broadcast_to(ret_decay_logit.reshape(-1, 2 * HEADS)[:, :, None], (ret_decay_logit.shape[0], 2 * HEADS, HEAD_DIM)),
              cw=cw_full, lnw=conv_ln_w[:, None, :], lnb=conv_ln_b[:, None, :])
    od = dict(pw=pool_w, ps=vec_full[0][:, None, :], lnw=vec_full[1][:, None, :], lnb=vec_full[2][:, None, :],
              sgw=sg_w, sgb=jnp.broadcast_to(sg_b[:, :, :, None], sg_b.shape + (GC,)))
    xs = jnp.concatenate([ctx[0], x[0]], axis=0)
    loss_blk, dxs, small = _local_step(xs, loss_target[0], modt, norm_full[:, :, None, :], final_norm_w[None, :],
                                       get_w, get_ffn, put_g, ev, od, lc)

    misc = jnp.stack([
        small["dfnw"], jnp.broadcast_to(loss_blk[0, 0], (dm,)),
        jnp.concatenate([e["lnw"] for e in small["ev"]]), jnp.concatenate([e["lnb"] for e in small["ev"]]),
        jnp.concatenate([o["ps"] for o in small["od"]]), jnp.concatenate([o["lnw"] for o in small["od"]]),
        jnp.concatenate([o["lnb"] for o in small["od"]]),
        jnp.pad(jnp.concatenate([e["dl"] for e in small["ev"]]), (0, dm - 4 * HEADS)),
        jnp.stack([o["sgb"] for o in small["od"]]).reshape(-1)])
    pack2 = jnp.concatenate([
        _rows(small["dmod"]), _rows(small["dnw"]), _rows(misc), _rows(jnp.stack([e["cw"] for e in small["ev"]])),
        _rows(jnp.stack([o["pw"] for o in small["od"]])), _rows(jnp.stack([o["sgw"] for o in small["od"]]))], axis=0)
    n2 = pack2.shape[0]
    g3 = _all_gather8("gather_grads", pack2)
    tot = _sum8("sum_grads", g3.reshape(8, n2, dm))
    r_mod = depth * 16
    o_nw, o_misc = r_mod, r_mod + 8
    o_cw = o_misc + 16
    o_pw = o_cw + 2 * (CONV_K + 1) // 2
    o_sgw = o_pw + 128
    dmod_sum = tot[:r_mod].reshape(depth, 2, 8, dm)
    dmod_dev = g3.reshape(8, n2, dm)[:, :r_mod].reshape(8, depth, 2, 8, dm)
    dm_x = dmod_dev[:, :, 1, :6].reshape(8, depth, 6 * dm).transpose(1, 0, 2)
    dm_c = dmod_sum[:, 0, :6].reshape(depth, 1, 6 * dm)
    dmod16 = jnp.concatenate([dm_x, dm_c, jnp.zeros((depth, 7, 6 * dm), F32)], axis=1)
    g_ada_b = _rowsum16("ada_b_grad", dmod16)[:, 0]
    g_ada_w, dc16 = _ada_bwd("ada_bwd", c16, _my_cols(dmod16, chip, n_ada), ada_w)
    g4 = _all_gather8("gather_cctx", dc16[8:16]).reshape(8, 8, dm)
    g_c_ctx = _cctx_grad("cctx_grad", g4[0::2, 0], c_ctx[None, :])[0]

    misc_t = tot[o_misc:o_misc + 16]
    half = lambda row: misc_t[row].reshape(2, dm // 2)
    grads = dict(
        c_ctx=g_c_ctx, ada_w=g_ada_w, ada_b=g_ada_b,
        norm_w=_my_cols(tot[o_nw:o_nw + 8].reshape(depth, 2, dm), chip, dm // 4),
        ret_decay_logit=misc_t[7, :4 * HEADS].reshape(ret_decay_logit.shape),
        conv_dw_w=_my_cols(tot[o_cw:o_cw + 2 * (CONV_K + 1) // 2].reshape(2, CONV_K + 1, dm // 2)[:, :CONV_K], chip, 128),
        conv_ln_w=half(2), conv_ln_b=half(3),
        pool_w=tot[o_pw:o_pw + 128].reshape(pool_w.shape),
        pool_scale=_my_cols(half(4), chip, 128), sg_ln_w=_my_cols(half(5), chip, 128), sg_ln_b=_my_cols(half(6), chip, 128),
        sg_w=tot[o_sgw:o_sgw + 128].reshape(sg_w.shape), sg_b=misc_t[8].reshape(sg_b.shape),
        final_norm_w=misc_t[0])
    loss = misc_t[1, 0]

    last_tokens = [finish_pair(stages[-1], [g_c_ctx])[0][4]]
    deltas, new_m, new_v = {}, {}, {}
    for n in WEIGHTS:
        if n not in BIG.values():
            deltas[n], new_m[n], new_v[n] = _adamw("adamw_" + n, wv[n], grads[n], mv[n], vv[n])
    reduced = {k: None for k in names}
    for stage in stages:
        glist, from_sib, st = pending[stage]
        last = stage == stages[-1]
        after = [deltas["ada_w"]] + [reduced[k] for k, _ in stage_keys(stages[-2])] if last else last_tokens
        slots = _split_wait(f"rs_wait{stage[0]}{stage[1]}", st, _scatter_copies(len(glist)), after)[1]
        keys = stage_keys(stage)
        groups = [range(len(keys))] if stage[1] == "f" else [[t] for t in range(len(keys))]
        for grp in groups:
            ks = [keys[t][0] for t in grp]
            lyr = keys[grp[0]][1]
            outs = _sum_final(f"rs_sum_{ks[0]}{lyr}", [glist[t] for t in grp], [from_sib[t] for t in grp],
                              [slots[t] for t in grp], idx, [reduced[k] for k in ks], lyr, shard[ks[0]].shape[0])
            for k, o in zip(ks, outs):
                reduced[k] = o
    shards = dict(zip(names, _rs_share("rs_share", [reduced[k] for k in names])))

    for k in names:
        n = BIG[k]
        tr = (lambda a: jnp.swapaxes(a, 1, 2)) if k in tr_names else (lambda a: a)
        outs = _adamw("adamw_" + n, tr(wv[n]), shards[k], tr(mv[n]), tr(vv[n]), with_grad=True)
        deltas[n], new_m[n], new_v[n], grads[n] = (tr(o) for o in outs)
    grad_x = dxs[None]
    return (loss, grad_x, *[grads[n] for n in WEIGHTS], *[deltas[n] for n in WEIGHTS],
            *[new_m[n] for n in WEIGHTS], *[new_v[n] for n in WEIGHTS])
```

```python
import functools
import math

import jax
import jax.numpy as jnp
from jax import lax
from jax.experimental import pallas as pl
from jax.experimental.pallas import tpu as pltpu

F32 = jnp.float32
BF16 = jnp.bfloat16
MESH = pl.DeviceIdType.MESH

EPS = 1e-6
GRID_W = 64
HEADS = 4
HEAD_DIM = 128
CHUNK = 128
CONV_K = 31
ROPE_BASE = 10000.0
ROPE_PAIRS = (16, 24, 24)
POOL_WINDOWS = (2, 4, 8, 16)
ADAM_LR, ADAM_B1, ADAM_B2, ADAM_EPS, ADAM_WD, ADAM_STEP = 0.001, 0.9, 0.999, 1e-08, 0.01, 10

ROW_TILE = 256
WIDE_ROWS = 576
CONV_HALO = 16
POOL_HALO = 8
VMEM_LIMIT = 56 * 1024 * 1024
WGRAD_ROWS = 2304


def _pcall(body, **kw):
    return pl.pallas_call(body, **kw)


def _cp(sem=None, vmem=VMEM_LIMIT):
    if sem is None:
        return pltpu.CompilerParams(vmem_limit_bytes=vmem)
    return pltpu.CompilerParams(dimension_semantics=sem, vmem_limit_bytes=vmem)


def _sds(shape, dtype=F32):
    return jax.ShapeDtypeStruct(tuple(shape), dtype)


def _full(shape):
    nd = len(shape)
    return pl.BlockSpec(tuple(shape), lambda *_: (0,) * nd)


def _sigmoid(x):
    return jax.nn.sigmoid(x)


def _silu(x):
    return x * _sigmoid(x)


def _dsilu(x):
    s = _sigmoid(x)
    return s * (1.0 + x * (1.0 - s))


def _colsum(a):
    return jnp.sum(a, axis=0, keepdims=True)


def _dot(a, b, dn):
    return lax.dot_general(a.astype(BF16), b.astype(BF16), dn, preferred_element_type=F32)


NN = (((1,), (0,)), ((), ()))
NT = (((1,), (1,)), ((), ()))
TN = (((0,), (0,)), ((), ()))


def _mm_tile(t, cap=1152):
    best = 16
    for d in range(16, min(t, cap) + 1, 16):
        if t % d == 0:
            best = d
    return best


def _mm(name, pairs, grid, out_shape, out_spec, dn):
    npairs = len(pairs)
    nk = grid[-1]
    kax = len(grid) - 1
    assert nk == 1 or out_shape.dtype == F32

    def body(*refs):
        ins = refs[:2 * npairs]
        o_ref = refs[2 * npairs]
        tot = None
        for p in range(npairs):
            d = _dot(ins[2 * p][...], ins[2 * p + 1][...], dn)
            tot = d if tot is None else tot + d
        if nk == 1:
            o_ref[...] = tot.astype(o_ref.dtype)
        else:
            k = pl.program_id(kax)

            @pl.when(k == 0)
            def _():
                o_ref[...] = tot

            @pl.when(k != 0)
            def _():
                o_ref[...] += tot

    args, in_specs = [], []
    for a, a_spec, b, b_spec in pairs:
        args += [a, b]
        in_specs += [a_spec, b_spec]
    sem = ("parallel",) * kax + ("arbitrary",)
    return _pcall(body, name=name, grid=grid, in_specs=in_specs, out_specs=out_spec, out_shape=out_shape,
                  compiler_params=_cp(sem))(*args)


def _mm_cols(name, a, w, out_dtype=F32):
    t, k = a.shape
    j, _, n = w.shape
    tm = _mm_tile(t)
    return _mm(name, [(a, pl.BlockSpec((tm, k), lambda i, jj, kk: (i, 0)),
                       w, pl.BlockSpec((None, k, n), lambda i, jj, kk: (jj, 0, 0)))],
               (t // tm, j, 1), _sds((t, j * n), out_dtype), pl.BlockSpec((tm, n), lambda i, jj, kk: (i, jj)), NN)


def _mm_cols_bwd(name, d, w):
    t = d.shape[0]
    j, k, n = w.shape
    tm = _mm_tile(t)
    pairs = [(d, pl.BlockSpec((tm, n), functools.partial(lambda jj, i, u, kk: (i, jj), jj)),
              w, pl.BlockSpec((None, k, n), functools.partial(lambda jj, i, u, kk: (jj, 0, 0), jj))) for jj in range(j)]
    return _mm(name, pairs, (t // tm, 1, 1), _sds((t, k), BF16), pl.BlockSpec((tm, k), lambda i, u, kk: (i, 0)), NT)


def _mm_full(name, a, w, dn, tm=None):
    t, k = a.shape
    n = w.shape[1] if dn is NN else w.shape[0]
    tm = tm or _mm_tile(t)
    return _mm(name, [(a, pl.BlockSpec((tm, k), lambda i, u, kk: (i, 0)), w, _full(w.shape))],
               (t // tm, 1, 1), _sds((t, n)), pl.BlockSpec((tm, n), lambda i, u, kk: (i, 0)), dn)


def _wgrad_cols(name, a, b, j):
    t, k = a.shape
    n = b.shape[1] // j
    tt = _mm_tile(t, 2 * WGRAD_ROWS)
    return _mm(name, [(a, pl.BlockSpec((tt, k), lambda jj, u, kk: (kk, 0)),
                       b, pl.BlockSpec((tt, n), lambda jj, u, kk: (kk, jj)))],
               (j, 1, t // tt), _sds((j, k, n)), pl.BlockSpec((None, k, n), lambda jj, u, kk: (jj, 0, 0)), TN)


def _wgrad_rows(name, a, blk, b):
    t, f = a.shape
    n = b.shape[1]
    tt = _mm_tile(t, WGRAD_ROWS)
    return _mm(name, [(a, pl.BlockSpec((tt, blk), lambda jj, u, kk: (kk, jj)),
                       b, pl.BlockSpec((tt, n), lambda jj, u, kk: (kk, 0)))],
               (f // blk, 1, t // tt), _sds((f, n)), pl.BlockSpec((blk, n), lambda jj, u, kk: (jj, 0)), TN)


def _ffn_tiles(t, f):
    return _mm_tile(t, 288), f


def _ffn_up(name, h, wgt, wut):
    t, k = h.shape
    f = wgt.shape[0]
    tm, tn = _ffn_tiles(t, f)

    def body(h_ref, wg_ref, wu_ref, a_ref, gt_ref, up_ref):
        hv = h_ref[...]
        gt = _dot(hv, wg_ref[...], NT)
        up = _dot(hv, wu_ref[...], NT)
        a_ref[...] = (_silu(gt) * up).astype(BF16)
        gt_ref[...] = gt.astype(BF16)
        up_ref[...] = up.astype(BF16)

    wspec = pl.BlockSpec((tn, k), lambda i, jj: (jj, 0))
    ospec = pl.BlockSpec((tm, tn), lambda i, jj: (i, jj))
    o = _sds((t, f), BF16)
    return _pcall(body, name=name, grid=(t // tm, f // tn),
                  in_specs=[pl.BlockSpec((tm, k), lambda i, jj: (i, 0)), wspec, wspec],
                  out_specs=[ospec, ospec, ospec], out_shape=[o, o, o],
                  compiler_params=_cp(("parallel", "parallel")))(h, wgt, wut)


def _ffn_down_bwd(name, df, wd, gt, up):
    t, dm = df.shape
    f = wd.shape[0]
    tm, tn = _ffn_tiles(t, f)

    def body(df_ref, wd_ref, gt_ref, up_ref, dgt_ref, dup_ref):
        da = _dot(df_ref[...], wd_ref[...], NT)
        g = gt_ref[...].astype(F32)
        u = up_ref[...].astype(F32)
        s = _sigmoid(g)
        dgt_ref[...] = (da * u * (s * (1.0 + g * (1.0 - s)))).astype(BF16)
        dup_ref[...] = (da * (g * s)).astype(BF16)

    aspec = pl.BlockSpec((tm, tn), lambda i, jj: (i, jj))
    o = _sds((t, f), BF16)
    return _pcall(body, name=name, grid=(t // tm, f // tn),
                  in_specs=[pl.BlockSpec((tm, dm), lambda i, jj: (i, 0)),
                            pl.BlockSpec((tn, dm), lambda i, jj: (jj, 0)), aspec, aspec],
                  out_specs=[aspec, aspec], out_shape=[o, o],
                  compiler_params=_cp(("parallel", "parallel")))(df, wd, gt, up)


def _ffn_in_bwd(name, dgt, dup, wgt, wut):
    t, f = dgt.shape
    k = wgt.shape[1]
    tm = _mm_tile(t, 576)
    aspec = pl.BlockSpec((tm, f), lambda i, u, kk: (i, 0))
    wspec = pl.BlockSpec((f, k), lambda i, u, kk: (0, 0))
    return _mm(name, [(dgt, aspec, wgt, wspec), (dup, aspec, wut, wspec)], (t // tm, 1, 1), _sds((t, k), BF16),
               pl.BlockSpec((tm, k), lambda i, u, kk: (i, 0)), NN)


def _modrow(ref, row, is_ctx):
    return jnp.where(is_ctx, ref[0, row:row + 1, :], ref[1, row:row + 1, :])


def _rnm(name, x, delta, mod_g, g_row, mod_n, sh_row, sc_row, nw, nct, deps=()):
    t, dm = x.shape
    tm = ROW_TILE
    has = delta is not None
    nd = len(deps)

    def body(*refs):
        refs = refs[:len(refs) - nd - (2 if has else 1)] + refs[len(refs) - (2 if has else 1):]
        if has:
            x_ref, d_ref, mg_ref, m_ref, nw_ref, xo_ref, h_ref = refs
        else:
            x_ref, m_ref, nw_ref, h_ref = refs
        is_ctx = pl.program_id(0) < nct
        xv = x_ref[...]
        if has:
            xv = xv + _modrow(mg_ref, g_row, is_ctx) * d_ref[...]
            xo_ref[...] = xv
        r = lax.rsqrt(jnp.mean(xv * xv, axis=-1, keepdims=True) + EPS)
        hv = (xv * r * nw_ref[...]) * (1.0 + _modrow(m_ref, sc_row, is_ctx)) + _modrow(m_ref, sh_row, is_ctx)
        h_ref[...] = hv.astype(BF16)

    row = pl.BlockSpec((tm, dm), lambda i: (i, 0))
    ins = [x] + ([delta, mod_g] if has else []) + [mod_n, nw] + list(deps)
    in_specs = ([row] + ([row, _full(mod_g.shape)] if has else []) + [_full(mod_n.shape), _full(nw.shape)]
                + [_full(d.shape) for d in deps])
    outs = ([_sds((t, dm))] if has else []) + [_sds((t, dm), BF16)]
    out_specs = ([row] if has else []) + [row]
    res = _pcall(body, name=name, grid=(t // tm,), in_specs=in_specs, out_specs=out_specs, out_shape=outs,
                 compiler_params=_cp(("parallel",)))(*ins)
    return res if has else (None, res[0])


def _bnm(name, xn, dh, dup, yprev, mod_n, sh_row, sc_row, mod_g, g_row, nw, nct, deps=()):
    t, dm = xn.shape
    tm = ROW_TILE
    has = yprev is not None
    nd = len(deps)

    def body(*refs):
        nout = 3 if has else 2
        refs = refs[:len(refs) - nd - nout] + refs[len(refs) - nout:]
        if has:
            x_ref, dh_ref, du_ref, y_ref, mn_ref, mg_ref, nw_ref, dx_ref, dd_ref, s_ref = refs
        else:
            x_ref, dh_ref, du_ref, mn_ref, nw_ref, dx_ref, s_ref = refs
        i = pl.program_id(0)
        is_ctx = i < nct

        @pl.when(i == 0)
        def _():
            s_ref[...] = jnp.zeros_like(s_ref)

        xv = x_ref[...]
        r = lax.rsqrt(jnp.mean(xv * xv, axis=-1, keepdims=True) + EPS)
        xh = xv * r
        w = nw_ref[...]
        sc1 = 1.0 + _modrow(mn_ref, sc_row, is_ctx)
        dhv = dh_ref[...].astype(F32)
        dxh = dhv * sc1 * w
        dx = r * (dxh - xh * jnp.mean(dxh * xh, axis=-1, keepdims=True)) + du_ref[...]
        dx_ref[...] = dx
        parts = [_colsum(dhv), _colsum(dhv * (xh * w))]
        if has:
            dd_ref[...] = (_modrow(mg_ref, g_row, is_ctx) * dx).astype(BF16)
            parts.append(_colsum(dx * y_ref[...]))
        else:
            parts.append(jnp.zeros((1, dm), F32))
        upd = jnp.concatenate(parts + [jnp.zeros((5, dm), F32)], axis=0)
        dnw = jnp.concatenate([jnp.zeros((3, dm), F32), _colsum(dhv * sc1 * xh), jnp.zeros((4, dm), F32)], axis=0)

        @pl.when(is_ctx)
        def _():
            s_ref[0] += upd
            s_ref[1] += dnw

        @pl.when(jnp.logical_not(is_ctx))
        def _():
            s_ref[1] += upd + dnw

    row = pl.BlockSpec((tm, dm), lambda i: (i, 0))
    ins = [xn, dh, dup] + ([yprev] if has else []) + [mod_n] + ([mod_g] if has else []) + [nw] + list(deps)
    in_specs = ([row, row, row] + ([row] if has else []) + [_full(mod_n.shape)]
                + ([_full(mod_g.shape)] if has else []) + [_full(nw.shape)] + [_full(d.shape) for d in deps])
    if has:
        outs = [_sds((t, dm)), _sds((t, dm), BF16), _sds((2, 8, dm))]
        out_specs = [row, row, _full((2, 8, dm))]
    else:
        outs = [_sds((t - nct * tm, dm)), _sds((2, 8, dm))]
        out_specs = [pl.BlockSpec((tm, dm), lambda i: (jnp.maximum(i - nct, 0), 0)), _full((2, 8, dm))]
    res = _pcall(body, name=name, grid=(t // tm,), in_specs=in_specs, out_specs=out_specs, out_shape=outs,
                 compiler_params=_cp(("arbitrary",)))(*ins)
    return res if has else (res[0], None, res[1])


def _fin(name, x1, f, mod, g_row, fw, target, nct):
    t, dm = x1.shape
    tm = ROW_TILE

    def body(x_ref, f_ref, m_ref, fw_ref, t_ref, loss_ref, dx_ref, dd_ref, s_ref):
        i = pl.program_id(0)

        @pl.when(i == 0)
        def _():
            s_ref[...] = jnp.zeros_like(s_ref)
            loss_ref[...] = jnp.zeros_like(loss_ref)

        @pl.when(i < nct)
        def _():
            dx_ref[...] = jnp.zeros_like(dx_ref)
            dd_ref[...] = jnp.zeros_like(dd_ref)

        @pl.when(i >= nct)
        def _():
            g = m_ref[1, g_row:g_row + 1, :]
            fv = f_ref[...]
            xv = x_ref[...] + g * fv
            r = lax.rsqrt(jnp.mean(xv * xv, axis=-1, keepdims=True) + EPS)
            xh = xv * r
            w = fw_ref[...]
            err = xh * w - t_ref[...]
            loss_ref[...] += 0.5 * jnp.sum(err * err) / dm
            dout = err * (1.0 / dm)
            dxh = dout * w
            dx = r * (dxh - xh * jnp.mean(dxh * xh, axis=-1, keepdims=True))
            dx_ref[...] = dx
            dd_ref[...] = (g * dx).astype(BF16)
            s_ref[...] += jnp.concatenate([_colsum(dx * fv), _colsum(dout * xh), jnp.zeros((6, dm), F32)], axis=0)

    row = pl.BlockSpec((tm, dm), lambda i: (i, 0))
    trow = pl.BlockSpec((tm, dm), lambda i: (jnp.maximum(i - nct, 0), 0))
    return _pcall(body, name=name, grid=(t // tm,),
                  in_specs=[row, row, _full(mod.shape), _full(fw.shape), trow],
                  out_specs=[_full((8, 128)), row, row, _full((8, dm))],
                  out_shape=[_sds((8, 128)), _sds((t, dm)), _sds((t, dm), BF16), _sds((8, dm))],
                  compiler_params=_cp(("arbitrary",)))(x1, f, mod, fw, target)


def _rope_tables(t, lc):
    l = t - lc
    rows = l // GRID_W
    grid_r = jnp.broadcast_to(jnp.arange(rows, dtype=F32)[:, None], (rows, GRID_W)).reshape(-1)
    grid_c = jnp.broadcast_to(jnp.arange(GRID_W, dtype=F32)[None, :], (rows, GRID_W)).reshape(-1)

    def angles(p_seq, p_row, p_col):
        parts = []
        for p, n in zip((p_seq, p_row, p_col), ROPE_PAIRS):
            freq = ROPE_BASE ** (-jnp.arange(n, dtype=F32) / n)
            parts.append(p[:, None] * freq[None, :])
        return jnp.concatenate(parts, axis=-1)

    zc = jnp.zeros((lc,), F32)
    ang = jnp.concatenate([angles(jnp.arange(lc, dtype=F32), zc, zc),
                           angles(jnp.full((l,), lc, F32), grid_r, grid_c)], axis=0)
    cos, sin = jnp.cos(ang), jnp.sin(ang)
    return jnp.concatenate([cos, cos], axis=-1), jnp.concatenate([-sin, sin], axis=-1)


def _rope(u, cs, sn):
    return u * cs + pltpu.roll(u, HEAD_DIM // 2, 1) * sn


def _rope_t(d, cs, sn):
    return d * cs + pltpu.roll(d * sn, HEAD_DIM // 2, 1)


def _even_qkv(name, p, cs, sn):
    t = p.shape[0]
    tm = _mm_tile(t, WIDE_ROWS)
    w = HEADS * HEAD_DIM
    scale = HEAD_DIM ** -0.5

    def body(q_ref, k_ref, v_ref, cs_ref, sn_ref, qo_ref, ko_ref, vo_ref):
        c, s = cs_ref[...], sn_ref[...]
        for h in range(HEADS):
            sl = slice(h * HEAD_DIM, (h + 1) * HEAD_DIM)
            qo_ref[:, sl] = (_rope(q_ref[:, sl], c, s) * scale).astype(BF16)
            ko_ref[:, sl] = _rope(k_ref[:, sl], c, s).astype(BF16)
        vo_ref[...] = v_ref[...].astype(BF16)

    col = lambda j: pl.BlockSpec((tm, w), lambda i: (i, j))
    tab = pl.BlockSpec((tm, HEAD_DIM), lambda i: (i, 0))
    o = _sds((t, w), BF16)
    return _pcall(body, name=name, grid=(t // tm,), in_specs=[col(0), col(1), col(2), tab, tab],
                  out_specs=[col(0)] * 3, out_shape=[o, o, o], compiler_params=_cp(("parallel",)))(p, p, p, cs, sn)


def _log_sigmoid_row(x):
    e = jnp.exp(-jnp.abs(x))
    l1p = jnp.where(e < 0.01, e * (1.0 - e * (0.5 - e * (1.0 / 3.0))), jnp.log(1.0 + e))
    return jnp.minimum(x, 0.0) - l1p


def _ret_tables(lgb_ref, dm_ref, xi_ref, zt_ref):
    ri = lax.broadcasted_iota(jnp.int32, (CHUNK, CHUNK), 0).astype(F32)
    ci = lax.broadcasted_iota(jnp.int32, (CHUNK, CHUNK), 1).astype(F32)
    for d in range(2):
        for h in range(HEADS):
            idx = d * HEADS + h
            lg = _log_sigmoid_row(lgb_ref[idx:idx + 1, :])
            if d == 0:
                e, mask = ri - ci, ri >= ci
                xe, ze = ri + 1.0, (CHUNK - 1.0) - ri
            else:
                e, mask = ci - ri - 1.0, ci > ri
                xe, ze = (CHUNK - 1.0) - ri, ri
            dm_ref[idx] = jnp.where(mask, jnp.exp(lg * jnp.where(mask, e, 0.0)), 0.0)
            xi_ref[idx] = jnp.exp(lg * xe)
            zt_ref[idx] = jnp.exp(lg * ze)


def _ret_exponents(d):
    ri = lax.broadcasted_iota(jnp.int32, (CHUNK, CHUNK), 0).astype(F32)
    ci = lax.broadcasted_iota(jnp.int32, (CHUNK, CHUNK), 1).astype(F32)
    if d == 0:
        return ri - ci, ri + 1.0, (CHUNK - 1.0) - ri
    return ci - ri - 1.0, (CHUNK - 1.0) - ri, ri


RET_SUB = 2


def _bwd_chunk(n, ncc, nc):
    return jnp.where(n < ncc, ncc - 1 - n, nc - 1 - (n - ncc))


def _retention_fwd(name, q, k, v, lgb, lc):
    t, w = q.shape
    nc = t // CHUNK
    rows_per = RET_SUB * CHUNK
    nb, ncb = t // rows_per, lc // rows_per
    nh = 2 * HEADS

    def body(qf_ref, kf_ref, vf_ref, qb_ref, kb_ref, vb_ref, lgb_ref, of_ref, ob_ref, ss_ref,
             s_ref, dm_ref, xi_ref, zt_ref):
        n = pl.program_id(0)

        @pl.when(n == 0)
        def _():
            s_ref[...] = jnp.zeros_like(s_ref)
            _ret_tables(lgb_ref, dm_ref, xi_ref, zt_ref)

        where = []
        for u in range(RET_SUB):
            for d in range(2):
                refs = (qf_ref, kf_ref, vf_ref, of_ref) if d == 0 else (qb_ref, kb_ref, vb_ref, ob_ref)
                r0 = (u if d == 0 else RET_SUB - 1 - u) * CHUNK
                for h in range(HEADS):
                    where.append((u, d * HEADS + h, refs, slice(r0, r0 + CHUNK), slice(h * HEAD_DIM, (h + 1) * HEAD_DIM)))
        qs = [refs[0][rows, sl] for _, _, refs, rows, sl in where]
        ks = [refs[1][rows, sl] for _, _, refs, rows, sl in where]
        vs = [refs[2][rows, sl] for _, _, refs, rows, sl in where]
        sc = [_dot(qv, kv, NT) for qv, kv in zip(qs, ks)]
        upd = [_dot(kv.astype(F32) * zt_ref[idx], vv, TN) for (_, idx, *_), kv, vv in zip(where, ks, vs)]
        cur = [s_ref[idx] for idx in range(nh)]
        gcs = [jnp.exp(_log_sigmoid_row(lgb_ref[idx:idx + 1, :]) * float(CHUNK)) for idx in range(nh)]
        st = []
        for (u, idx, *_), du in zip(where, upd):
            st.append(cur[idx])
            ss_ref[u, idx] = cur[idx]
            cur[idx] = gcs[idx] * cur[idx] + du
        for idx in range(nh):
            s_ref[idx] = cur[idx]
        inter = [_dot(qv.astype(F32) * xi_ref[idx], s, NN) for (_, idx, *_), qv, s in zip(where, qs, st)]
        intra = [_dot(a * dm_ref[idx], vv, NN) for (_, idx, *_), a, vv in zip(where, sc, vs)]
        for (_, _, refs, rows, sl), o1, o2 in zip(where, intra, inter):
            refs[3][rows, sl] = o1 + o2

    fspec = pl.BlockSpec((rows_per, w), lambda n: (n, 0))
    bspec = pl.BlockSpec((rows_per, w), lambda n: (_bwd_chunk(n, ncb, nb), 0))
    tab = pltpu.VMEM((nh, CHUNK, CHUNK), F32)
    return _pcall(body, name=name, grid=(nb,),
                  in_specs=[fspec] * 3 + [bspec] * 3 + [_full((nh, HEAD_DIM))],
                  out_specs=[fspec, bspec, pl.BlockSpec((RET_SUB, nh, CHUNK, CHUNK), lambda n: (n, 0, 0, 0))],
                  out_shape=[_sds((t, w)), _sds((t, w)), _sds((nc, nh, CHUNK, CHUNK))],
                  scratch_shapes=[tab, tab, tab, tab],
                  compiler_params=_cp(("arbitrary",)))(q, k, v, q, k, v, lgb)


def _retention_bwd(name, q, k, v, do, ss, lgb, lc):
    t, w = q.shape
    rows_per = RET_SUB * CHUNK
    nb, ncb = t // rows_per, lc // rows_per
    nh = 2 * HEADS

    def body(qf_ref, kf_ref, vf_ref, gf_ref, qb_ref, kb_ref, vb_ref, gb_ref, ss_ref, lgb_ref,
             dqf_ref, dkf_ref, dvf_ref, dqb_ref, dkb_ref, dvb_ref, dl_ref,
             ds_ref, dm_ref, xi_ref, zt_ref, acc_ref):
        n = pl.program_id(0)

        @pl.when(n == 0)
        def _():
            ds_ref[...] = jnp.zeros_like(ds_ref)
            acc_ref[...] = jnp.zeros_like(acc_ref)
            _ret_tables(lgb_ref, dm_ref, xi_ref, zt_ref)

        where = []
        for u in reversed(range(RET_SUB)):
            for d in range(2):
                refs = ((qf_ref, kf_ref, vf_ref, gf_ref, dqf_ref, dkf_ref, dvf_ref) if d == 0
                        else (qb_ref, kb_ref, vb_ref, gb_ref, dqb_ref, dkb_ref, dvb_ref))
                r0 = (u if d == 0 else RET_SUB - 1 - u) * CHUNK
                for h in range(HEADS):
                    where.append((u, d * HEADS + h, d, refs, slice(r0, r0 + CHUNK), slice(h * HEAD_DIM, (h + 1) * HEAD_DIM)))
        qs = [refs[0][rows, sl] for *_, refs, rows, sl in where]
        ks = [refs[1][rows, sl] for *_, refs, rows, sl in where]
        vs = [refs[2][rows, sl] for *_, refs, rows, sl in where]
        gs = [refs[3][rows, sl] for *_, refs, rows, sl in where]
        st = [ss_ref[u, idx] for u, idx, *_ in where]
        sc = [_dot(qv, kv, NT) for qv, kv in zip(qs, ks)]
        dar = [_dot(gv, vv, NT) for gv, vv in zip(gs, vs)]
        t1 = [_dot(gv, s, NT) for gv, s in zip(gs, st)]
        dsn = [_dot(qv.astype(F32) * xi_ref[idx], gv, TN) for (_, idx, *_), qv, gv in zip(where, qs, gs)]
        cur = [ds_ref[idx] for idx in range(nh)]
        gcs = [jnp.exp(_log_sigmoid_row(lgb_ref[idx:idx + 1, :]) * float(CHUNK)) for idx in range(nh)]
        dsps = []
        for (_, idx, *_), x in zip(where, dsn):
            dsps.append(cur[idx])
            cur[idx] = gcs[idx] * cur[idx] + x
        for idx in range(nh):
            ds_ref[idx] = cur[idx]
        t2 = [_dot(vv, dsp, NT) for vv, dsp in zip(vs, dsps)]
        dv2 = [_dot(kv.astype(F32) * zt_ref[idx], dsp, NN) for (_, idx, *_), kv, dsp in zip(where, ks, dsps)]
        a = [x * dm_ref[idx] for (_, idx, *_), x in zip(where, sc)]
        da = [x * dm_ref[idx] for (_, idx, *_), x in zip(where, dar)]
        dq = [_dot(x, kv, NN) for x, kv in zip(da, ks)]
        dk = [_dot(x, qv, TN) for x, qv in zip(da, qs)]
        dv1 = [_dot(x, gv, TN) for x, gv in zip(a, gs)]
        for i8, (_, idx, d, refs, rows, sl) in enumerate(where):
            ee, xe, ze = _ret_exponents(d)
            xi, zt = xi_ref[idx], zt_ref[idx]
            qf32, kf32 = qs[i8].astype(F32), ks[i8].astype(F32)
            refs[4][rows, sl] = dq[i8] + xi * t1[i8]
            refs[5][rows, sl] = dk[i8] + zt * t2[i8]
            refs[6][rows, sl] = dv1[i8] + dv2[i8]
            acc_ref[idx] += (ee * a[i8] * dar[i8] + xe * xi * qf32 * t1[i8] + ze * zt * kf32 * t2[i8]
                             + (float(CHUNK) * gcs[idx]) * dsps[i8] * st[i8])

        @pl.when(n == nb - 1)
        def _():
            for idx in range(nh):
                tot = jnp.sum(acc_ref[idx])
                dl_ref[idx:idx + 1, :] = tot * _sigmoid(-lgb_ref[idx:idx + 1, :])

    fmap = lambda n: (nb - 1 - n, 0)
    bmap = lambda n: (_bwd_chunk(nb - 1 - n, ncb, nb), 0)
    fspec = pl.BlockSpec((rows_per, w), fmap)
    bspec = pl.BlockSpec((rows_per, w), bmap)
    tab = pltpu.VMEM((nh, CHUNK, CHUNK), F32)
    o = _sds((t, w))
    return _pcall(body, name=name, grid=(nb,),
                  in_specs=[fspec] * 4 + [bspec] * 4
                  + [pl.BlockSpec((RET_SUB, nh, CHUNK, CHUNK), lambda n: (nb - 1 - n, 0, 0, 0)), _full((nh, HEAD_DIM))],
                  out_specs=[fspec] * 3 + [bspec] * 3 + [_full((nh, HEAD_DIM))],
                  out_shape=[o] * 6 + [_sds((nh, HEAD_DIM))],
                  scratch_shapes=[tab, tab, tab, tab, tab],
                  compiler_params=_cp(("arbitrary",)))(q, k, v, do, q, k, v, do, ss, lgb)


def _halo_specs(tm, halo, t, width, col):
    hb = tm // halo
    last = t // halo - 1
    prev = pl.BlockSpec((halo, width), lambda i: (jnp.maximum(i * hb - 1, 0), col))
    nxt = pl.BlockSpec((halo, width), lambda i: (jnp.minimum((i + 1) * hb, last), col))
    return prev, nxt


def _halo_valid(i, nct, nt):
    vp = jnp.logical_and(i != 0, i != nct)
    vn = jnp.logical_and(i != nct - 1, i != nt - 1)
    return vp, vn


def _fill_window(win_ref, prev, cur, nxt, vp, vn, halo, tm):
    win_ref[0:halo, :] = jnp.where(vp, prev, 0.0)
    win_ref[halo:halo + tm, :] = cur
    win_ref[halo + tm:halo + tm + halo, :] = jnp.where(vn, nxt, 0.0)


CONV_SUB = 64


SUBLANES = 8


def _shift_window(win_ref, sh_ref, tm):
    rows = tm + 2 * CONV_HALO - SUBLANES
    for s in range(SUBLANES):
        sh_ref[s, 0:rows, :] = win_ref[s:s + rows, :]


def _window_rows(sh_ref, start, rows):
    s = start % SUBLANES
    return sh_ref[s, start - s:start - s + rows, :]


def _conv_taps(sh_ref, w_ref, tm, flip):
    outs = []
    for r0 in range(0, tm, CONV_SUB):
        acc = None
        for kk in range(CONV_K):
            wk = (CONV_K - 1 - kk) if flip else kk
            term = w_ref[wk:wk + 1, :] * _window_rows(sh_ref, r0 + kk + 1, CONV_SUB)
            acc = term if acc is None else acc + term
        outs.append(acc)
    return jnp.concatenate(outs, axis=0)


def _head_norm(y):
    r = lax.rsqrt(jnp.mean(y * y, axis=-1, keepdims=True) + EPS)
    return y * r, r


def _ln_stats(y):
    mu = jnp.mean(y, axis=-1, keepdims=True)
    yc = y - mu
    rs = lax.rsqrt(jnp.mean(yc * yc, axis=-1, keepdims=True) + EPS)
    return yc * rs, rs


def _ln_bwd(dyh, yh, rs):
    return rs * (dyh - jnp.mean(dyh, axis=-1, keepdims=True) - yh * jnp.mean(dyh * yh, axis=-1, keepdims=True))


def _even_mix(name, p, of, ob, cw, lnw, lnb, nct):
    t = p.shape[0]
    tm, halo = ROW_TILE, CONV_HALO
    nt = t // tm
    w = HEADS * HEAD_DIM

    def body(g_ref, a_ref, gb_ref, ap_ref, gbp_ref, an_ref, gbn_ref, of_ref, ob_ref, cw_ref, lw_ref, lb_ref,
             mix_ref, yc_ref, win_ref, sh_ref):
        i = pl.program_id(0)
        vp, vn = _halo_valid(i, nct, nt)
        glu = lambda a, b: a * _sigmoid(b)
        _fill_window(win_ref, glu(ap_ref[...], gbp_ref[...]), glu(a_ref[...], gb_ref[...]),
                     glu(an_ref[...], gbn_ref[...]), vp, vn, halo, tm)
        _shift_window(win_ref, sh_ref, tm)
        yc = _conv_taps(sh_ref, cw_ref, tm, False)
        yc_ref[...] = yc
        yh, _ = _ln_stats(yc)
        mix_ref[:, w:2 * w] = _silu(yh * lw_ref[...] + lb_ref[...]).astype(BF16)
        for h in range(HEADS):
            sl = slice(h * HEAD_DIM, (h + 1) * HEAD_DIM)
            yn, _ = _head_norm(of_ref[:, sl] + ob_ref[:, sl])
            mix_ref[:, sl] = (_silu(g_ref[:, sl]) * yn).astype(BF16)

    col = lambda j: pl.BlockSpec((tm, w), lambda i: (i, j))
    ap, an = _halo_specs(tm, halo, t, w, 4)
    gp, gn = _halo_specs(tm, halo, t, w, 5)
    row = pl.BlockSpec((tm, w), lambda i: (i, 0))
    return _pcall(body, name=name, grid=(nt,),
                  in_specs=[col(3), col(4), col(5), ap, gp, an, gn, row, row,
                            _full(cw.shape), _full(lnw.shape), _full(lnb.shape)],
                  out_specs=[pl.BlockSpec((tm, 2 * w), lambda i: (i, 0)), row],
                  out_shape=[_sds((t, 2 * w), BF16), _sds((t, w))],
                  scratch_shapes=[pltpu.VMEM((tm + 2 * halo, w), F32), pltpu.VMEM((SUBLANES, tm + 2 * halo, w), F32)],
                  compiler_params=_cp(("parallel",)))(p, p, p, p, p, p, p, of, ob, cw, lnw, lnb)


def _even_mix_bwd1(name, dmix, p, of, ob, yc, lnw, lnb):
    t = p.shape[0]
    tm = _mm_tile(t, WIDE_ROWS)
    w = HEADS * HEAD_DIM

    def body(dr_ref, dc_ref, g_ref, of_ref, ob_ref, yc_ref, lw_ref, lb_ref, do_ref, dg_ref, dyc_ref, s_ref):
        @pl.when(pl.program_id(0) == 0)
        def _():
            s_ref[...] = jnp.zeros_like(s_ref)

        for h in range(HEADS):
            sl = slice(h * HEAD_DIM, (h + 1) * HEAD_DIM)
            yn, r = _head_norm(of_ref[:, sl] + ob_ref[:, sl])
            gv = g_ref[:, sl]
            dr = dr_ref[:, sl]
            dg_ref[:, sl] = (dr * yn * _dsilu(gv)).astype(BF16)
            dyn = dr * _silu(gv)
            do_ref[:, sl] = (r * (dyn - yn * jnp.mean(dyn * yn, axis=-1, keepdims=True))).astype(BF16)
        yh, rs = _ln_stats(yc_ref[...])
        lw = lw_ref[...]
        dlo = dc_ref[...] * _dsilu(yh * lw + lb_ref[...])
        dyc_ref[...] = _ln_bwd(dlo * lw, yh, rs)
        s_ref[...] += jnp.concatenate([_colsum(dlo * yh), _colsum(dlo), jnp.zeros((6, w), F32)], axis=0)

    col = lambda j: pl.BlockSpec((tm, w), lambda i: (i, j))
    row = pl.BlockSpec((tm, w), lambda i: (i, 0))
    return _pcall(body, name=name, grid=(t // tm,),
                  in_specs=[col(0), col(1), col(3), row, row, row, _full(lnw.shape), _full(lnb.shape)],
                  out_specs=[row, row, row, _full((8, w))],
                  out_shape=[_sds((t, w), BF16), _sds((t, w), BF16), _sds((t, w)), _sds((8, w))],
                  compiler_params=_cp(("arbitrary",)))(dmix, dmix, p, of, ob, yc, lnw, lnb)


def _even_conv_bwd(name, dyc, p, cw, nct):
    t = p.shape[0]
    tm, halo = ROW_TILE, CONV_HALO
    nt = t // tm
    w = HEADS * HEAD_DIM

    def body(d_ref, dp_ref, dn_ref, a_ref, gb_ref, ap_ref, gbp_ref, an_ref, gbn_ref, cw_ref,
             da_ref, dgb_ref, dw_ref, dwin_ref, uwin_ref, dsh_ref, ush_ref):
        i = pl.program_id(0)

        @pl.when(i == 0)
        def _():
            dw_ref[...] = jnp.zeros_like(dw_ref)

        vp, vn = _halo_valid(i, nct, nt)
        glu = lambda a, b: a * _sigmoid(b)
        dcur = d_ref[...]
        _fill_window(dwin_ref, dp_ref[...], dcur, dn_ref[...], vp, vn, halo, tm)
        _fill_window(uwin_ref, glu(ap_ref[...], gbp_ref[...]), glu(a_ref[...], gb_ref[...]),
                     glu(an_ref[...], gbn_ref[...]), vp, vn, halo, tm)
        _shift_window(dwin_ref, dsh_ref, tm)
        _shift_window(uwin_ref, ush_ref, tm)
        du = _conv_taps(dsh_ref, cw_ref, tm, True)
        av = a_ref[...]
        sg = _sigmoid(gb_ref[...])
        da_ref[...] = (du * sg).astype(BF16)
        dgb_ref[...] = (du * av * sg * (1.0 - sg)).astype(BF16)
        rows = [_colsum(dcur * _window_rows(ush_ref, kk + 1, tm)) for kk in range(CONV_K)]
        dw_ref[...] += jnp.concatenate(rows + [jnp.zeros((1, w), F32)], axis=0)

    col = lambda j: pl.BlockSpec((tm, w), lambda i: (i, j))
    row = pl.BlockSpec((tm, w), lambda i: (i, 0))
    dp, dn = _halo_specs(tm, halo, t, w, 0)
    ap, an = _halo_specs(tm, halo, t, w, 4)
    gp, gn = _halo_specs(tm, halo, t, w, 5)
    win = pltpu.VMEM((tm + 2 * halo, w), F32)
    shifted = pltpu.VMEM((SUBLANES, tm + 2 * halo, w), F32)
    return _pcall(body, name=name, grid=(nt,),
                  in_specs=[row, dp, dn, col(4), col(5), ap, gp, an, gn, _full(cw.shape)],
                  out_specs=[row, row, _full((CONV_K + 1, w))],
                  out_shape=[_sds((t, w), BF16), _sds((t, w), BF16), _sds((CONV_K + 1, w))],
                  scratch_shapes=[win, win, shifted, shifted],
                  compiler_params=_cp(("arbitrary",)))(dyc, dyc, dyc, p, p, p, p, p, p, cw)


def _even_dp(name, dqs, dks, dvs, dg, da, dgb, cs, sn):
    t, w = dg.shape
    tm = _mm_tile(t, WIDE_ROWS)
    scale = HEAD_DIM ** -0.5

    def body(dqf_ref, dqb_ref, dkf_ref, dkb_ref, dvf_ref, dvb_ref, dg_ref, da_ref, dgb_ref, cs_ref, sn_ref, dp_ref):
        c, s = cs_ref[...], sn_ref[...]
        for h in range(HEADS):
            sl = slice(h * HEAD_DIM, (h + 1) * HEAD_DIM)
            dp_ref[:, sl] = (_rope_t(dqf_ref[:, sl] + dqb_ref[:, sl], c, s) * scale).astype(BF16)
            dp_ref[:, w + h * HEAD_DIM:w + (h + 1) * HEAD_DIM] = _rope_t(dkf_ref[:, sl] + dkb_ref[:, sl], c, s).astype(BF16)
        dp_ref[:, 2 * w:3 * w] = (dvf_ref[...] + dvb_ref[...]).astype(BF16)
        dp_ref[:, 3 * w:4 * w] = dg_ref[...]
        dp_ref[:, 4 * w:5 * w] = da_ref[...]
        dp_ref[:, 5 * w:6 * w] = dgb_ref[...]

    row = pl.BlockSpec((tm, w), lambda i: (i, 0))
    tab = pl.BlockSpec((tm, HEAD_DIM), lambda i: (i, 0))
    return _pcall(body, name=name, grid=(t // tm,), in_specs=[row] * 9 + [tab, tab],
                  out_specs=pl.BlockSpec((tm, 6 * w), lambda i: (i, 0)), out_shape=_sds((t, 6 * w), BF16),
                  compiler_params=_cp(("parallel",)))(dqs[0], dqs[1], dks[0], dks[1], dvs[0], dvs[1], dg, da, dgb, cs, sn)


GROUPS = 4
GC = 128
INV_SQRT2 = 0.7071067811865476
INV_SQRT_2PI = 0.3989422804014327


def _gelu(x):
    return 0.5 * x * (1.0 + lax.erf(x * INV_SQRT2))


def _dgelu(x):
    return 0.5 * (1.0 + lax.erf(x * INV_SQRT2)) + x * jnp.exp(-0.5 * x * x) * INV_SQRT_2PI


def _pool_count(i, nct, lc, t, tm, rows, row0, left, right):
    is_ctx = i < nct
    seg_start = jnp.where(is_ctx, 0, lc)
    seg_len = jnp.where(is_ctx, lc, t - lc)
    pos = i * tm + row0 - seg_start + lax.broadcasted_iota(jnp.int32, (rows, GC), 0)
    cnt = jnp.minimum(pos + right, seg_len - 1) - jnp.maximum(pos - left, 0) + 1
    return jnp.maximum(cnt, 1).astype(F32)


def _spatial_gate(vln, sgw_ref, sgb_ref, tm):
    cols = []
    for g in range(GROUPS):
        sl = slice(g * GC, (g + 1) * GC)
        parts = [_dot(sgw_ref[g], vln[r0:r0 + CHUNK, sl], NN) + sgb_ref[g] for r0 in range(0, tm, CHUNK)]
        cols.append(jnp.concatenate(parts, axis=0))
    return jnp.concatenate(cols, axis=1)


def _odd_mix(name, p, pw, pscale, lnw, lnb, sgw, sgb, nct, lc):
    t = p.shape[0]
    tm, halo = ROW_TILE, POOL_HALO
    nt = t // tm
    w = GROUPS * GC

    def body(pc_ref, pp_ref, pn_ref, pu_ref, pv_ref, pw_ref, ps_ref, lw_ref, lb_ref, sgw_ref, sgb_ref,
             mix_ref, m_ref, win_ref):
        i = pl.program_id(0)
        vp, vn = _halo_valid(i, nct, nt)
        pc = pc_ref[...]
        _fill_window(win_ref, pp_ref[...], pc, pn_ref[...], vp, vn, halo, tm)
        for g, wd in enumerate(POOL_WINDOWS):
            sl = slice(g * GC, (g + 1) * GC)
            left = wd // 2
            right = wd - 1 - left
            s = None
            for o in range(-left, right + 1):
                term = win_ref[halo + o:halo + o + tm, sl]
                s = term if s is None else s + term
            mg = s / _pool_count(i, nct, lc, t, tm, tm, 0, left, right) - pc[:, sl]
            m_ref[:, sl] = mg
            mix_ref[:, sl] = (_dot(mg, pw_ref[g], NN) * ps_ref[:, sl]).astype(BF16)
        u = _gelu(pu_ref[...])
        vh, _ = _ln_stats(_gelu(pv_ref[...]))
        s = _spatial_gate(vh * lw_ref[...] + lb_ref[...], sgw_ref, sgb_ref, tm)
        mix_ref[:, w:2 * w] = (u * s).astype(BF16)

    col = lambda j: pl.BlockSpec((tm, w), lambda i: (i, j))
    pp, pn = _halo_specs(tm, halo, t, w, 0)
    return _pcall(body, name=name, grid=(nt,),
                  in_specs=[col(0), pp, pn, col(1), col(2), _full(pw.shape), _full(pscale.shape),
                            _full(lnw.shape), _full(lnb.shape), _full(sgw.shape), _full(sgb.shape)],
                  out_specs=[pl.BlockSpec((tm, 2 * w), lambda i: (i, 0)), col(0)],
                  out_shape=[_sds((t, 2 * w), BF16), _sds((t, w))],
                  scratch_shapes=[pltpu.VMEM((tm + 2 * halo, w), F32)],
                  compiler_params=_cp(("parallel",)))(p, p, p, p, p, pw, pscale, lnw, lnb, sgw, sgb)


def _odd_mix_bwd1(name, dmix, p, m, pw, pscale, lnw, lnb, sgw, sgb):
    t = p.shape[0]
    tm = ROW_TILE
    w = GROUPS * GC

    def body(dpo_ref, dsg_ref, pu_ref, pv_ref, m_ref, pw_ref, ps_ref, lw_ref, lb_ref, sgw_ref, sgb_ref,
             dm_ref, dpd_ref, vec_ref, dpw_ref, dsgw_ref, dsgb_ref):
        @pl.when(pl.program_id(0) == 0)
        def _():
            vec_ref[...] = jnp.zeros_like(vec_ref)
            dpw_ref[...] = jnp.zeros_like(dpw_ref)
            dsgw_ref[...] = jnp.zeros_like(dsgw_ref)
            dsgb_ref[...] = jnp.zeros_like(dsgb_ref)

        dscale = []
        for g in range(GROUPS):
            sl = slice(g * GC, (g + 1) * GC)
            mg = m_ref[:, sl]
            dpo = dpo_ref[:, sl]
            dscale.append(_colsum(dpo * _dot(mg, pw_ref[g], NN)))
            dpo = dpo * ps_ref[:, sl]
            dm_ref[:, sl] = _dot(dpo, pw_ref[g], NT)
            dpw_ref[g] += _dot(mg, dpo, TN)
        pu, pv = pu_ref[...], pv_ref[...]
        u = _gelu(pu)
        vh, rs = _ln_stats(_gelu(pv))
        lw = lw_ref[...]
        vln = vh * lw + lb_ref[...]
        s = _spatial_gate(vln, sgw_ref, sgb_ref, tm)
        dsg = dsg_ref[...]
        dpd_ref[:, 0:w] = (dsg * s * _dgelu(pu)).astype(BF16)
        ds = dsg * u
        cols = []
        for g in range(GROUPS):
            sl = slice(g * GC, (g + 1) * GC)
            parts = []
            for r0 in range(0, tm, CHUNK):
                dsc = ds[r0:r0 + CHUNK, sl]
                parts.append(_dot(sgw_ref[g], dsc, TN))
                dsgw_ref[g] += _dot(dsc, vln[r0:r0 + CHUNK, sl], NT)
                dsgb_ref[g] += dsc
            cols.append(jnp.concatenate(parts, axis=0))
        dvln = jnp.concatenate(cols, axis=1)
        dpd_ref[:, w:2 * w] = (_ln_bwd(dvln * lw, vh, rs) * _dgelu(pv)).astype(BF16)
        vec_ref[...] += jnp.concatenate([jnp.concatenate(dscale, axis=1), _colsum(dvln * vh), _colsum(dvln),
                                         jnp.zeros((5, w), F32)], axis=0)

        @pl.when(pl.program_id(0) == t // tm - 1)
        def _():
            for g in range(GROUPS):
                dsgb_ref[g] = jnp.broadcast_to(jnp.sum(dsgb_ref[g], axis=1, keepdims=True), (GC, GC))

    col = lambda j: pl.BlockSpec((tm, w), lambda i: (i, j))
    mat = _full((GROUPS, GC, GC))
    return _pcall(body, name=name, grid=(t // tm,),
                  in_specs=[col(0), col(1), col(1), col(2), col(0), _full(pw.shape), _full(pscale.shape),
                            _full(lnw.shape), _full(lnb.shape), _full(sgw.shape), _full(sgb.shape)],
                  out_specs=[col(0), pl.BlockSpec((tm, 2 * w), lambda i: (i, 0)), _full((8, w)), mat, mat, mat],
                  out_shape=[_sds((t, w)), _sds((t, 2 * w), BF16), _sds((8, w)),
                             _sds((GROUPS, GC, GC)), _sds((GROUPS, GC, GC)), _sds((GROUPS, GC, GC))],
                  compiler_params=_cp(("arbitrary",)))(dmix, dmix, p, p, m, pw, pscale, lnw, lnb, sgw, sgb)


def _odd_dp(name, dm, dpd, nct, lc):
    t, w = dm.shape
    tm, halo = ROW_TILE, POOL_HALO
    nt = t // tm

    def body(d_ref, dp_ref, dn_ref, dpd_ref, o_ref, win_ref):
        i = pl.program_id(0)
        vp, vn = _halo_valid(i, nct, nt)
        dcur = d_ref[...]
        _fill_window(win_ref, dp_ref[...], dcur, dn_ref[...], vp, vn, halo, tm)
        for g, wd in enumerate(POOL_WINDOWS):
            sl = slice(g * GC, (g + 1) * GC)
            left = wd // 2
            right = wd - 1 - left
            win_ref[:, sl] = win_ref[:, sl] / _pool_count(i, nct, lc, t, tm, tm + 2 * halo, -halo, left, right)
            s = None
            for o in range(-right, left + 1):
                term = win_ref[halo + o:halo + o + tm, sl]
                s = term if s is None else s + term
            o_ref[:, sl] = (s - dcur[:, sl]).astype(BF16)
        o_ref[:, w:3 * w] = dpd_ref[...]

    row = pl.BlockSpec((tm, w), lambda i: (i, 0))
    pp, pn = _halo_specs(tm, halo, t, w, 0)
    return _pcall(body, name=name, grid=(nt,),
                  in_specs=[row, pp, pn, pl.BlockSpec((tm, 2 * w), lambda i: (i, 0))],
                  out_specs=pl.BlockSpec((tm, 3 * w), lambda i: (i, 0)), out_shape=_sds((t, 3 * w), BF16),
                  scratch_shapes=[pltpu.VMEM((tm + 2 * halo, w), F32)],
                  compiler_params=_cp(("parallel",)))(dm, dm, dm, dpd)


def _place():
    x, y, c = lax.axis_index("x"), lax.axis_index("y"), lax.axis_index("c")
    chips = [(1 - x, y), (x, 1 - y), (1 - x, 1 - y)]
    return x, y, c, chips


def _chip_index(cx, cy):
    return 2 * cx + cy


def _all_gather8(name, blk, after=()):
    m_per, n = blk.shape
    na = len(after)

    def body(x_ref, *rest):
        out_ref, send_sems, recv_sems, local_sem = rest[na:]
        x, y, c, chips = _place()
        me, sibling = (x, y, c), (x, y, 1 - c)

        def rows(px, py, pc):
            return out_ref.at[pl.ds((4 * px + 2 * py + pc) * m_per, m_per), :]

        def copy(k, block, to, src=None):
            return pltpu.make_async_remote_copy(
                src_ref=rows(*block) if src is None else src, dst_ref=rows(*block),
                send_sem=send_sems.at[k], recv_sem=recv_sems.at[k], device_id=to, device_id_type=MESH)

        mine = pltpu.make_async_copy(x_ref, rows(*me), local_sem)
        mine.start()
        first = [copy(0, me, sibling, src=x_ref)]
        first += [copy(1 + j, me, (*chip, c), src=x_ref) for j, chip in enumerate(chips)]
        for cp in first:
            cp.start()
        passed = [copy(4 + j, (*chip, c), sibling) for j, chip in enumerate(chips)]
        for j, chip in enumerate(chips):
            copy(1 + j, (*chip, c), me).wait_recv()
            passed[j].start()
        copy(0, sibling, me).wait_recv()
        for j, chip in enumerate(chips):
            copy(4 + j, (*chip, 1 - c), me).wait_recv()
        for cp in first + passed:
            cp.wait_send()
        mine.wait()

    return _pcall(body, name=name, out_shape=_sds((8 * m_per, n), blk.dtype),
                  in_specs=[pl.BlockSpec(memory_space=pltpu.VMEM)] + [pl.BlockSpec(memory_space=pl.ANY)] * na,
                  out_specs=pl.BlockSpec(memory_space=pltpu.VMEM),
                  scratch_shapes=[pltpu.SemaphoreType.DMA((7,)), pltpu.SemaphoreType.DMA((7,)), pltpu.SemaphoreType.DMA],
                  compiler_params=_cp())(blk, *after)


ANY = pl.BlockSpec(memory_space=pl.ANY)


def _half(which, rows):
    return pl.ds(pl.multiple_of(which * rows, 16), rows)


def _gather_weights(name, ws, after=()):
    nw = len(ws)
    na = len(after)
    ns = 7

    def body(*refs):
        w_refs, o_refs = refs[:nw], refs[nw + na:2 * nw + na]
        send_sems, recv_sems = refs[2 * nw + na:]
        x, y, c, chips = _place()
        me_chip = _chip_index(x, y)
        sibling = (x, y, 1 - c)

        def rcopy(t, k, src, dst, to):
            return pltpu.make_async_remote_copy(src_ref=src, dst_ref=dst, send_sem=send_sems.at[t * ns + k],
                                                recv_sem=recv_sems.at[t * ns + k], device_id=to, device_id_type=MESH)

        sends = []
        for t in range(nw):
            lh = w_refs[t].shape[0] // 2
            for k, chip in enumerate(chips):
                sends.append(rcopy(t, k, w_refs[t].at[_half(c, lh)], o_refs[t].at[me_chip, _half(c, lh)], (*chip, c)))
                sends[-1].start()
            sends.append(rcopy(t, 6, w_refs[t], o_refs[t].at[me_chip], sibling))
            sends[-1].start()
        for t in range(nw):
            lh = w_refs[t].shape[0] // 2
            for k, chip in enumerate(chips):
                part = o_refs[t].at[_chip_index(*chip), _half(c, lh)]
                rcopy(t, k, part, part, (*chip, c)).wait_recv()
                sends.append(rcopy(t, 3 + k, part, part, sibling))
                sends[-1].start()
        for t in range(nw):
            lh = w_refs[t].shape[0] // 2
            own = o_refs[t].at[me_chip]
            rcopy(t, 6, own, own, sibling).wait_recv()
            for k, chip in enumerate(chips):
                part = o_refs[t].at[_chip_index(*chip), _half(1 - c, lh)]
                rcopy(t, 3 + k, part, part, sibling).wait_recv()
        for cp in sends:
            cp.wait_send()

    return _pcall(body, name=name, out_shape=[_sds((4,) + w.shape, w.dtype) for w in ws],
                  in_specs=[ANY] * (nw + na), out_specs=[ANY] * nw,
                  scratch_shapes=[pltpu.SemaphoreType.DMA((ns * nw,)), pltpu.SemaphoreType.DMA((ns * nw,))],
                  compiler_params=_cp())(*ws, *after)


def _rs_share(name, ss):
    ng = len(ss)

    def body(*refs):
        o_refs = refs[ng:2 * ng]
        send_sems, recv_sems = refs[2 * ng:]
        x, y, c, _ = _place()
        cps = []
        for t in range(ng):
            lh = o_refs[t].shape[1] // 2
            mine = o_refs[t].at[:, _half(c, lh)]
            cp = pltpu.make_async_remote_copy(
                src_ref=mine, dst_ref=mine, send_sem=send_sems.at[t], recv_sem=recv_sems.at[t],
                device_id=(x, y, 1 - c), device_id_type=MESH)
            cp.start()
            cps.append(cp)
        for t in range(ng):
            lh = o_refs[t].shape[1] // 2
            cps[t].wait_send()
            theirs = o_refs[t].at[:, _half(1 - c, lh)]
            pltpu.make_async_remote_copy(
                src_ref=theirs, dst_ref=theirs, send_sem=send_sems.at[t], recv_sem=recv_sems.at[t],
                device_id=(x, y, 1 - c), device_id_type=MESH).wait_recv()

    return _pcall(body, name=name, out_shape=[_sds(s.shape, s.dtype) for s in ss],
                  in_specs=[ANY] * ng, out_specs=[ANY] * ng, input_output_aliases={t: t for t in range(ng)},
                  scratch_shapes=[pltpu.SemaphoreType.DMA((ng,)), pltpu.SemaphoreType.DMA((ng,))],
                  compiler_params=_cp())(*ss)


HBM = pl.BlockSpec(memory_space=pltpu.HBM)
SEMS = pl.BlockSpec(memory_space=pltpu.SEMAPHORE)
EFFECT = pltpu.SideEffectType.DATAFLOW_SIDE_EFFECTING
TOKEN = (8, 128)


def _in_hbm(a):
    return pltpu.with_memory_space_constraint(a, pltpu.HBM)


def _split_start(name, srcs, lands, copies, after):
    ns, nl, na = len(srcs), len(lands), len(after)
    ncopies = len(copies([s for s in srcs], [l for l in lands], probe=True))

    def body(*refs):
        src_refs, land_refs = refs[:ns], refs[ns:ns + nl]
        send_sems, recv_sems = refs[ns + nl + na], refs[ns + nl + na + 1]
        token = refs[-1]
        for k, (src, dst, to) in enumerate(copies(src_refs, land_refs)):
            pltpu.make_async_remote_copy(src_ref=src, dst_ref=dst, send_sem=send_sems.at[k], recv_sem=recv_sems.at[k],
                                         device_id=to, device_id_type=MESH).start()
        token[...] = jnp.zeros_like(token)

    thru = [pltpu.HBM(a.shape, a.dtype) for a in list(srcs) + list(lands)]
    outs = _pcall(body, name=name,
                  out_shape=(pltpu.SemaphoreType.DMA((ncopies,)), pltpu.SemaphoreType.DMA((ncopies,)), *thru, _sds(TOKEN)),
                  in_specs=[HBM] * (ns + nl) + [ANY] * na,
                  out_specs=(SEMS, SEMS, *([HBM] * (ns + nl)), pl.BlockSpec(memory_space=pltpu.VMEM)),
                  input_output_aliases={t: 2 + t for t in range(ns + nl)},
                  compiler_params=pltpu.CompilerParams(has_side_effects=EFFECT))(
        *[_in_hbm(a) for a in list(srcs) + list(lands)], *after)
    return outs[0], outs[1], list(outs[2:2 + ns]), list(outs[2 + ns:2 + ns + nl]), outs[-1]


def _split_wait(name, started, copies, after, first=0):
    send_sems, recv_sems, srcs, lands, _ = started
    ns, nl, na = len(srcs), len(lands), len(after)

    def body(*refs):
        src_refs, land_refs = refs[:ns], refs[ns:ns + nl]
        send_sems_ref, recv_sems_ref = refs[ns + nl], refs[ns + nl + 1]
        for k, (src, dst, to) in enumerate(copies(src_refs, land_refs)):
            cp = pltpu.make_async_remote_copy(src_ref=src, dst_ref=dst, send_sem=send_sems_ref.at[first + k],
                                              recv_sem=recv_sems_ref.at[first + k], device_id=to, device_id_type=MESH)
            cp.wait_send()
            cp.wait_recv()

    thru = [pltpu.HBM(a.shape, a.dtype) for a in list(srcs) + list(lands)]
    outs = _pcall(body, name=name, out_shape=tuple(thru),
                  in_specs=[HBM] * (ns + nl) + [SEMS, SEMS] + [ANY] * na, out_specs=tuple([HBM] * (ns + nl)),
                  input_output_aliases={t: t for t in range(ns + nl)},
                  compiler_params=pltpu.CompilerParams(has_side_effects=EFFECT))(
        *srcs, *lands, send_sems, recv_sems, *after)
    return list(outs[:ns]), list(outs[ns:])


def _pair_copies(n):
    def copies(src_refs, land_refs, probe=False):
        if probe:
            return [None] * n
        x, y, c, _ = _place()
        return [(src_refs[t].at[:, _half(1 - c, src_refs[t].shape[1] // 2)], land_refs[t], (x, y, 1 - c))
                for t in range(n)]
    return copies


def _gather_copies(n):
    def copies(src_refs, land_refs, probe=False):
        if probe:
            return [None] * (4 * n)
        x, y, c, chips = _place()
        me_chip = _chip_index(x, y)
        out = []
        for t in range(n):
            for to in [(*chip, c) for chip in chips] + [(x, y, 1 - c)]:
                out.append((src_refs[t], land_refs[t].at[me_chip], to))
        return out
    return copies


def _scatter_copies(n):
    def copies(src_refs, land_refs, probe=False):
        if probe:
            return [None] * (3 * n)
        x, y, c, chips = _place()
        out = []
        for t in range(n):
            for k, chip in enumerate(chips):
                out.append((src_refs[t].at[_chip_index(*chip)], land_refs[t].at[k], (*chip, c)))
        return out
    return copies


def _row_block(r, cn):
    if r % 8:
        return r
    best = 8
    for d in range(8, r + 1, 8):
        if r % d == 0 and d * cn * 4 <= (2 << 20):
            best = d
    return best


def _add_half(name, gl, al, idx):
    n = len(gl)
    j, rh, cn = al[0].shape
    tr = _row_block(rh, cn)
    nb = rh // tr

    def body(i_ref, *refs):
        for t in range(n):
            refs[2 * n + t][...] = (refs[t][...] + refs[n + t][...]).astype(BF16)

    blk = (None, tr, cn)
    gspec = pl.BlockSpec(blk, lambda jj, i, i_ref: (jj, i_ref[0] * nb + i, 0))
    aspec = pl.BlockSpec(blk, lambda jj, i, i_ref: (jj, i, 0))
    gs = pltpu.PrefetchScalarGridSpec(num_scalar_prefetch=1, grid=(j, nb), in_specs=[gspec] * n + [aspec] * n,
                                      out_specs=[aspec] * n)
    return _pcall(body, name=name, grid_spec=gs, out_shape=[_sds(al[0].shape, BF16)] * n,
                  compiler_params=_cp(("parallel", "parallel")))(idx, *gl, *al)


def _sum_final(name, gl, al, bl, idx, bufs, lyr, nlyr):
    n = len(gl)
    _, r, cn = gl[0].shape
    rh = r // 2
    tr = _row_block(rh, cn)
    nb = rh // tr
    has = bufs[0] is not None

    def body(i_ref, *refs):
        outs = refs[len(refs) - n:]
        for t in range(n):
            own = refs[t][...] + refs[n + t][...]
            b_ref = refs[2 * n + t]
            outs[t][...] = (own + b_ref[0].astype(F32)) + (b_ref[1].astype(F32) + b_ref[2].astype(F32))

    blk = (None, tr, cn)
    in_specs = ([pl.BlockSpec(blk, lambda i, i_ref: (i_ref[1], i_ref[0] * nb + i, 0))] * n
                + [pl.BlockSpec(blk, lambda i, i_ref: (i_ref[1], i, 0))] * n
                + [pl.BlockSpec((3, tr, cn), lambda i, i_ref: (0, i, 0))] * n)
    args = [idx, *gl, *al, *bl]
    kw = {}
    if has:
        in_specs += [ANY] * n
        args += list(bufs)
        kw["input_output_aliases"] = {1 + 3 * n + t: t for t in range(n)}
    gs = pltpu.PrefetchScalarGridSpec(
        num_scalar_prefetch=1, grid=(nb,), in_specs=in_specs,
        out_specs=[pl.BlockSpec(blk, lambda i, i_ref: (lyr, i_ref[0] * nb + i, 0))] * n)
    return _pcall(body, name=name, grid_spec=gs, out_shape=[_sds((nlyr, r, cn))] * n,
                  compiler_params=_cp(("parallel",)), **kw)(*args)


def _sum8(name, g):
    _, r, n = g.shape
    tr = 8

    def body(g_ref, o_ref):
        o_ref[...] = ((g_ref[0] + g_ref[1]) + (g_ref[2] + g_ref[3])) + ((g_ref[4] + g_ref[5]) + (g_ref[6] + g_ref[7]))

    return _pcall(body, name=name, grid=(r // tr,), in_specs=[pl.BlockSpec((8, tr, n), lambda i: (0, i, 0))],
                  out_specs=pl.BlockSpec((tr, n), lambda i: (i, 0)), out_shape=_sds((r, n)),
                  compiler_params=_cp(("parallel",)))(g)


def _ada_mod(name, c16, ada_w, bias):
    nl, dm, n = ada_w.shape

    def body(c_ref, w_ref, b_ref, o_ref):
        o_ref[...] = _dot(_silu(c_ref[...]), w_ref[...], NN) + b_ref[...]

    return _pcall(body, name=name, grid=(nl,),
                  in_specs=[_full(c16.shape), pl.BlockSpec((None, dm, n), lambda i: (i, 0, 0)),
                            pl.BlockSpec((None, 1, n), lambda i: (i, 0, 0))],
                  out_specs=pl.BlockSpec((None, 16, n), lambda i: (i, 0, 0)), out_shape=_sds((nl, 16, n)),
                  compiler_params=_cp(("parallel",)))(c16, ada_w, bias)


def _ada_bwd(name, c16, dmod, ada_w):
    nl, dm, n = ada_w.shape

    def body(c_ref, d_ref, w_ref, gw_ref, dc_ref):
        @pl.when(pl.program_id(0) == 0)
        def _():
            dc_ref[...] = jnp.zeros_like(dc_ref)

        dv = d_ref[...]
        gw_ref[...] = _dot(_silu(c_ref[...]), dv, TN)
        dc_ref[...] += _dot(dv, w_ref[...], NT)

    return _pcall(body, name=name, grid=(nl,),
                  in_specs=[_full(c16.shape), pl.BlockSpec((None, 16, n), lambda i: (i, 0, 0)),
                            pl.BlockSpec((None, dm, n), lambda i: (i, 0, 0))],
                  out_specs=[pl.BlockSpec((None, dm, n), lambda i: (i, 0, 0)), _full((16, dm))],
                  out_shape=[_sds((nl, dm, n)), _sds((16, dm))],
                  compiler_params=_cp(("arbitrary",)))(c16, dmod, ada_w)


def _rowsum16(name, dmod):
    nl, _, n = dmod.shape

    def body(d_ref, o_ref):
        o_ref[...] = _colsum(d_ref[...])

    return _pcall(body, name=name, grid=(nl,), in_specs=[pl.BlockSpec((None, 16, n), lambda i: (i, 0, 0))],
                  out_specs=pl.BlockSpec((None, 1, n), lambda i: (i, 0, 0)), out_shape=_sds((nl, 1, n)),
                  compiler_params=_cp(("parallel",)))(dmod)


def _cctx_grad(name, parts, c_ctx):
    def body(p_ref, c_ref, o_ref):
        tot = (p_ref[0:1, :] + p_ref[1:2, :]) + (p_ref[2:3, :] + p_ref[3:4, :])
        o_ref[...] = tot * _dsilu(c_ref[...])

    return _pcall(body, name=name, out_shape=_sds(c_ctx.shape), compiler_params=_cp())(parts, c_ctx)


def _adamw(name, w, g, m, v, with_grad=False):
    shape = w.shape
    cn = shape[-1]
    r = math.prod(shape[:-1]) if len(shape) > 1 else 1
    tr = _row_block(r, cn)
    c1 = 1.0 - ADAM_B1 ** ADAM_STEP
    c2 = 1.0 - ADAM_B2 ** ADAM_STEP
    nout = 4 if with_grad else 3

    def body(w_ref, g_ref, m_ref, v_ref, d_ref, mo_ref, vo_ref, *rest):
        gv = g_ref[...]
        mn = ADAM_B1 * m_ref[...] + (1.0 - ADAM_B1) * gv
        vn = ADAM_B2 * v_ref[...] + (1.0 - ADAM_B2) * (gv * gv)
        d_ref[...] = -ADAM_LR * ((mn / c1) / (jnp.sqrt(vn / c2) + ADAM_EPS) + ADAM_WD * w_ref[...])
        mo_ref[...] = mn
        vo_ref[...] = vn
        if with_grad:
            rest[0][...] = gv

    blk = pl.BlockSpec((tr, cn), lambda i: (i, 0))
    o = _sds((r, cn))
    outs = _pcall(body, name=name, grid=(r // tr,), in_specs=[blk] * 4, out_specs=[blk] * nout, out_shape=[o] * nout,
                  compiler_params=_cp(("parallel",)))(*[a.reshape(r, cn) for a in (w, g, m, v)])
    return tuple(a.reshape(shape) for a in outs)


def _local_step(xs, target, modt, nw, fnw, get_w, get_ffn, put_g, ev, od, lc):
    t, dm = xs.shape
    nct = lc // ROW_TILE
    depth = nw.shape[0]
    cs, sn = _rope_tables(t, lc)
    saved = []
    x_in, x1p, fp = xs, None, None
    for i in range(depth):
        j, even = i // 2, i % 2 == 0
        tag = f"l{i}"
        w, deps = get_w(i, [fp] if i else [])
        if i == 0:
            _, h = _rnm(tag + "_norm1", x_in, None, None, 0, modt[0], 0, 1, nw[0, 0], nct, deps)
        else:
            x_in, h = _rnm(tag + "_norm1", x1p, fp, modt[i - 1], 5, modt[i], 0, 1, nw[i, 0], nct, deps)
        s = dict(x=x_in, h=h, w=w)
        if even:
            p = _mm_cols(tag + "_in", h, w["in"])
            q, k, v = _even_qkv(tag + "_qkv", p, cs, sn)
            of, ob, ss = _retention_fwd(tag + "_ret", q, k, v, ev["lgb"][j], lc)
            mix, yc = _even_mix(tag + "_mix", p, of, ob, ev["cw"][j], ev["lnw"][j], ev["lnb"][j], nct)
            y = _mm_full(tag + "_out", mix, w["out"], NN)
            s.update(p=p, q=q, k=k, v=v, of=of, ob=ob, ss=ss, yc=yc)
        else:
            p = _mm_cols(tag + "_in", h, w["in"])
            mix, m = _odd_mix(tag + "_mix", p, od["pw"][j], od["ps"][j], od["lnw"][j], od["lnb"][j],
                              od["sgw"][j], od["sgb"][j], nct, lc)
            y = _mm_full(tag + "_out", mix, w["out"], NN)
            s.update(p=p, m=m)
        x1, h2 = _rnm(tag + "_norm2", x_in, y, modt[i], 2, modt[i], 3, 4, nw[i, 1], nct)
        w.update(get_ffn(i, [y]))
        a, gt, up = _ffn_up(tag + "_ffn_up", h2, w["gate"], w["up"])
        f = _mm_full(tag + "_ffn_down", a, w["down"], NN)
        s.update(mix=mix, y=y, x1=x1, h2=h2, a=a, gt=gt, up=up, f=f)
        saved.append(s)
        x1p, fp = x1, f

    loss_blk, dx, df, fin_s = _fin("final", x1p, fp, modt[depth - 1], 5, fnw, target, nct)

    deps = []
    dmod = [[None] * 6 for _ in range(depth)]
    dnw = [[None, None] for _ in range(depth)]
    zero2 = jnp.zeros((2, dm), F32)
    dmod[depth - 1][5] = jnp.stack([zero2[0], fin_s[0]])
    small = dict(dfnw=fin_s[1], ev=[], od=[])
    for i in reversed(range(depth)):
        j, even = i // 2, i % 2 == 0
        tag = f"l{i}b"
        s = saved[i]
        w = s["w"]
        fh = w["down"].shape[0] // 2
        g = {}
        dgt, dup = _ffn_down_bwd(tag + "_ffn_down", df, w["down"], s["gt"], s["up"])
        g["down"] = _wgrad_rows(tag + "_gdown", s["a"], fh, df)
        g["gate"] = _wgrad_rows(tag + "_ggate", dgt, fh, s["h2"])
        g["up"] = _wgrad_rows(tag + "_gup", dup, fh, s["h2"])
        deps = put_g(i, "f", g)
        dh2 = _ffn_in_bwd(tag + "_ffn_in", dgt, dup, w["gate"], w["up"])
        dx1, dy, s2 = _bnm(tag + "_norm2", s["x1"], dh2, dx, s["y"], modt[i], 3, 4, modt[i], 2, nw[i, 1], nct, deps)
        dmod[i][3], dmod[i][4], dmod[i][2] = s2[:, 0], s2[:, 1], s2[:, 2]
        dnw[i][1] = s2[1, 3]
        dmix = _mm_full(tag + "_out", dy, w["out"], NT)
        g["out"] = _wgrad_rows(tag + "_gout", s["mix"], w["out"].shape[0] // 2, dy)
        if even:
            do, dg, dyc, lns = _even_mix_bwd1(tag + "_mix1", dmix, s["p"], s["of"], s["ob"], s["yc"],
                                              ev["lnw"][j], ev["lnb"][j])
            da, dgb, dcw = _even_conv_bwd(tag + "_conv", dyc, s["p"], ev["cw"][j], nct)
            dqf, dkf, dvf, dqb, dkb, dvb, dl = _retention_bwd(tag + "_ret", s["q"], s["k"], s["v"], do, s["ss"],
                                                              ev["lgb"][j], lc)
            dp = _even_dp(tag + "_dp", (dqf, dqb), (dkf, dkb), (dvf, dvb), dg, da, dgb, cs, sn)
            small["ev"].append(dict(lnw=lns[0], lnb=lns[1], cw=dcw, dl=dl[:, 0]))
        else:
            dm_, dpd, vec, dpw, dsgw, dsgb = _odd_mix_bwd1(tag + "_mix1", dmix, s["p"], s["m"], od["pw"][j], od["ps"][j],
                                                           od["lnw"][j], od["lnb"][j], od["sgw"][j], od["sgb"][j])
            dp = _odd_dp(tag + "_dp", dm_, dpd, nct, lc)
            small["od"].append(dict(ps=vec[0], lnw=vec[1], lnb=vec[2], pw=dpw, sgw=dsgw, sgb=dsgb[:, :, 0]))
        dh = _mm_cols_bwd(tag + "_in", dp, w["in"])
        g["in"] = _wgrad_cols(tag + "_gin", s["h"], dp, w["in"].shape[0])
        deps = put_g(i, "m", g)
        if i > 0:
            dx, df, s1 = _bnm(tag + "_norm1", s["x"], dh, dx1, saved[i - 1]["f"], modt[i], 0, 1, modt[i - 1], 5,
                              nw[i, 0], nct, deps)
            dmod[i - 1][5] = s1[:, 2]
        else:
            dx, _, s1 = _bnm(tag + "_norm1", s["x"], dh, dx1, None, modt[0], 0, 1, None, 0, nw[0, 0], nct, deps)
        dmod[i][0], dmod[i][1] = s1[:, 0], s1[:, 1]
        dnw[i][0] = s1[1, 3]
    small["ev"].reverse()
    small["od"].reverse()
    dmod_t = jnp.stack([jnp.concatenate([jnp.stack(rows, axis=1), jnp.zeros((2, 2, dm), F32)], axis=1) for rows in dmod])
    small["dmod"] = dmod_t
    small["dnw"] = jnp.stack([jnp.stack(r) for r in dnw])
    return loss_blk, dx, small


WEIGHTS = ["c_ctx", "ada_w", "ada_b", "norm_w", "even_w_in", "even_w_out", "ret_decay_logit", "conv_dw_w",
           "conv_ln_w", "conv_ln_b", "odd_w_in", "odd_w_out", "pool_w", "pool_scale", "sg_ln_w", "sg_ln_b",
           "sg_w", "sg_b", "ffn_w_gate", "ffn_w_up", "ffn_w_down", "final_norm_w"]
BIG = dict(even_in="even_w_in", even_out="even_w_out", odd_in="odd_w_in", odd_out="odd_w_out",
           gate="ffn_w_gate", up="ffn_w_up", down="ffn_w_down")


def _rows(a, width=1024):
    flat = a.reshape(-1)
    n = flat.shape[0]
    per = 8 * width
    tot = -(-n // per) * per
    return jnp.pad(flat, (0, tot - n)).reshape(tot // width, width)


def _unshard(parts, lead):
    nl = len(lead)
    perm = tuple(range(1, nl + 1)) + (0, nl + 1)
    return parts.transpose(perm).reshape(tuple(lead) + (4 * parts.shape[-1],))


def _my_cols(a, chip, n):
    start = (0,) * (a.ndim - 1) + (chip * n,)
    return lax.dynamic_slice(a, start, a.shape[:-1] + (n,))


def kernel(x, c, ctx, c_ctx, ada_w, ada_b, norm_w, even_w_in, even_w_out, ret_decay_logit, conv_dw_w, conv_ln_w, conv_ln_b, odd_w_in, odd_w_out, pool_w, pool_scale, sg_ln_w, sg_ln_b, sg_w, sg_b, ffn_w_gate, ffn_w_up, ffn_w_down, final_norm_w, loss_target, m_c_ctx, m_ada_w, m_ada_b, m_norm_w, m_even_w_in, m_even_w_out, m_ret_decay_logit, m_conv_dw_w, m_conv_ln_w, m_conv_ln_b, m_odd_w_in, m_odd_w_out, m_pool_w, m_pool_scale, m_sg_ln_w, m_sg_ln_b, m_sg_w, m_sg_b, m_ffn_w_gate, m_ffn_w_up, m_ffn_w_down, m_final_norm_w, v_c_ctx, v_ada_w, v_ada_b, v_norm_w, v_even_w_in, v_even_w_out, v_ret_decay_logit, v_conv_dw_w, v_conv_ln_w, v_conv_ln_b, v_odd_w_in, v_odd_w_out, v_pool_w, v_pool_scale, v_sg_ln_w, v_sg_ln_b, v_sg_w, v_sg_b, v_ffn_w_gate, v_ffn_w_up, v_ffn_w_down, v_final_norm_w):
    wv = dict(c_ctx=c_ctx, ada_w=ada_w, ada_b=ada_b, norm_w=norm_w, even_w_in=even_w_in, even_w_out=even_w_out,
              ret_decay_logit=ret_decay_logit, conv_dw_w=conv_dw_w, conv_ln_w=conv_ln_w, conv_ln_b=conv_ln_b,
              odd_w_in=odd_w_in, odd_w_out=odd_w_out, pool_w=pool_w, pool_scale=pool_scale, sg_ln_w=sg_ln_w,
              sg_ln_b=sg_ln_b, sg_w=sg_w, sg_b=sg_b, ffn_w_gate=ffn_w_gate, ffn_w_up=ffn_w_up,
              ffn_w_down=ffn_w_down, final_norm_w=final_norm_w)
    mv = dict(zip(WEIGHTS, (m_c_ctx, m_ada_w, m_ada_b, m_norm_w, m_even_w_in, m_even_w_out, m_ret_decay_logit,
                            m_conv_dw_w, m_conv_ln_w, m_conv_ln_b, m_odd_w_in, m_odd_w_out, m_pool_w, m_pool_scale,
                            m_sg_ln_w, m_sg_ln_b, m_sg_w, m_sg_b, m_ffn_w_gate, m_ffn_w_up, m_ffn_w_down,
                            m_final_norm_w)))
    vv = dict(zip(WEIGHTS, (v_c_ctx, v_ada_w, v_ada_b, v_norm_w, v_even_w_in, v_even_w_out, v_ret_decay_logit,
                            v_conv_dw_w, v_conv_ln_w, v_conv_ln_b, v_odd_w_in, v_odd_w_out, v_pool_w, v_pool_scale,
                            v_sg_ln_w, v_sg_ln_b, v_sg_w, v_sg_b, v_ffn_w_gate, v_ffn_w_up, v_ffn_w_down,
                            v_final_norm_w)))
    xi, yi, ci = lax.axis_index("x"), lax.axis_index("y"), lax.axis_index("c")
    chip = 2 * xi + yi
    dev = 4 * xi + 2 * yi + ci
    dm = x.shape[-1]
    lc = ctx.shape[1]
    depth = ada_w.shape[0]
    n_ada = ada_w.shape[-1]

    cw_pad = jnp.pad(conv_dw_w, ((0, 0), (0, 1), (0, 0)))
    vec3 = jnp.stack([pool_scale, sg_ln_w, sg_ln_b])
    pack1 = jnp.concatenate([_rows(c), _rows(norm_w), _rows(cw_pad), _rows(vec3)], axis=0)
    g1 = _all_gather8("gather_small", pack1).reshape(8, 32, dm)
    c_all = g1[:, 0]
    per_chip = g1[0::2]
    norm_full = _unshard(per_chip[:, 8:10].reshape(4, depth, 2, dm // 4), (depth, 2))
    cw_full = _unshard(per_chip[:, 16:24].reshape(4, 2, CONV_K + 1, 128), (2, CONV_K + 1))
    vec_full = _unshard(per_chip[:, 24, :768].reshape(4, 3, 2, 128), (3, 2))

    c16 = jnp.concatenate([c_all, c_ctx[None, :], jnp.zeros((7, dm), F32)], axis=0)
    mod_sh = _ada_mod("ada_mod", c16, ada_w, _my_cols(ada_b, chip, n_ada)[:, None, :])
    g2 = _all_gather8("gather_mod", mod_sh.reshape(depth * 16, n_ada)).reshape(8, depth, 16, n_ada)
    mod_full = _unshard(g2[0::2], (depth, 16))
    mod_x = lax.dynamic_index_in_dim(mod_full, dev, axis=1, keepdims=False).reshape(depth, 6, dm)
    mod_c = mod_full[:, 8].reshape(depth, 6, dm)
    modt = jnp.pad(jnp.stack([mod_c, mod_x], axis=1), ((0, 0), (0, 0), (0, 2), (0, 0)))

    names = list(BIG)
    tr_names = ("gate", "up")
    shard = {k: (jnp.swapaxes(wv[BIG[k]], 1, 2) if k in tr_names else wv[BIG[k]]).astype(BF16) for k in names}
    roles = ("in", "out", "gate", "up", "down")

    def layer_keys(i):
        mixer = ("even_in", "even_out") if i % 2 == 0 else ("odd_in", "odd_out")
        return [(k, i // 2) for k in mixer] + [(k, i) for k in ("gate", "up", "down")]

    def as_used(got):
        return {r: (g if r == "in" else g.reshape(4 * g.shape[1], g.shape[2])) for r, g in zip(roles, got)}

    started = {}

    def get_w(i, after):
        if i > 0:
            part, first = started[i, "m"]
            got = _split_wait(f"gather_wait{i}m", part, _gather_copies(2), after, first)[1]
            return as_used(got), []
        got = _gather_weights("gather_w0", [shard[k][l] for k, l in layer_keys(0)[:2]], [modt])
        parts = [(li, p, layer_keys(li)[:2] if p == "m" else layer_keys(li)[2:])
                 for li in range(depth) for p in "mf" if (li, p) != (0, "m")]
        srcs = [shard[k][l] for _, _, keys in parts for k, l in keys]
        lands = [lax.empty((4,) + s.shape, s.dtype) for s in srcs]
        send_sems, recv_sems, srcs, lands, token = _split_start("gather_start", srcs, lands, _gather_copies(len(srcs)), [got[0]])
        t0 = 0
        for li, p, keys in parts:
            t1 = t0 + len(keys)
            started[li, p] = ((send_sems, recv_sems, srcs[t0:t1], lands[t0:t1], token), 4 * t0)
            t0 = t1
        return as_used(got), [token]

    def get_ffn(i, after):
        part, first = started[i, "f"]
        got = _split_wait(f"gather_wait{i}f", part, _gather_copies(3), after, first)[1]
        return {r: g.reshape(4 * g.shape[1], g.shape[2]) for r, g in zip(roles[2:], got)}

    idx = jnp.stack([ci, chip]).astype(jnp.int32)
    pairs, pending, stages = {}, {}, []

    def stage_keys(stage):
        i, part = stage
        return layer_keys(i)[2:] if part == "f" else layer_keys(i)[:2]

    def finish_pair(stage, after, glist=()):
        tag = f"{stage[0]}{stage[1]}"
        n, m = len(stage_keys(stage)), len(glist)
        part, first = pairs[stage]
        g_prev, from_sib = _split_wait(f"pair_wait{tag}", part, _pair_copies(n), after, first)
        if stage[1] == "f":
            pair = _add_half(f"rs_add{tag}", g_prev, from_sib, idx)
        else:
            pair = [_add_half(f"rs_add{tag}_{t}", [gl], [a], idx)[0] for t, (gl, a) in enumerate(zip(g_prev, from_sib))]
        lands = ([lax.empty((3,) + p.shape[1:], p.dtype) for p in pair]
                 + [lax.empty((4, gl.shape[1] // 2, gl.shape[2]), gl.dtype) for gl in glist])

        def copies(src_refs, land_refs, probe=False):
            return (_scatter_copies(n)(src_refs[:n], land_refs[:n], probe=probe)
                    + (_pair_copies(m)(src_refs[n:], land_refs[n:], probe=probe) if m else []))

        send_sems, recv_sems, srcs, lands, token = _split_start(f"rs_start{tag}", list(pair) + list(glist), lands, copies, [])
        pending[stage] = (g_prev, from_sib, (send_sems, recv_sems, srcs[:n], lands[:n], token))
        return (send_sems, recv_sems, srcs[n:], lands[n:], token), 3 * n

    def put_g(i, part, g):
        stage = (i, part)
        glist = [g[r].reshape(4, -1, g[r].shape[-1]) for r in (roles[2:] if part == "f" else roles[:2])]
        if stages:
            pairs[stage] = finish_pair(stages[-1], [glist[0]], glist)
        else:
            lands = [lax.empty((4, gl.shape[1] // 2, gl.shape[2]), gl.dtype) for gl in glist]
            pairs[stage] = (_split_start(f"pair_start{i}{part}", glist, lands, _pair_copies(len(glist)), []), 0)
        stages.append(stage)
        return [pairs[stage][0][4]]

    ev = dict(lgb=jnp.broadcast_to(ret_decay_logit.reshape(-1, 2 * HEADS)[:, :, None], (ret_decay_logit.shape[0], 2 * HEADS, HEAD_DIM)),
              cw=cw_full, lnw=conv_ln_w[:, None, :], lnb=conv_ln_b[:, None, :])
    od = dict(pw=pool_w, ps=vec_full[0][:, None, :], lnw=vec_full[1][:, None, :], lnb=vec_full[2][:, None, :],
              sgw=sg_w, sgb=jnp.broadcast_to(sg_b[:, :, :, None], sg_b.shape + (GC,)))
    xs = jnp.concatenate([ctx[0], x[0]], axis=0)
    loss_blk, dxs, small = _local_step(xs, loss_target[0], modt, norm_full[:, :, None, :], final_norm_w[None, :],
                                       get_w, get_ffn, put_g, ev, od, lc)

    misc = jnp.stack([
        small["dfnw"], jnp.broadcast_to(loss_blk[0, 0], (dm,)),
        jnp.concatenate([e["lnw"] for e in small["ev"]]), jnp.concatenate([e["lnb"] for e in small["ev"]]),
        jnp.concatenate([o["ps"] for o in small["od"]]), jnp.concatenate([o["lnw"] for o in small["od"]]),
        jnp.concatenate([o["lnb"] for o in small["od"]]),
        jnp.pad(jnp.concatenate([e["dl"] for e in small["ev"]]), (0, dm - 4 * HEADS)),
        jnp.stack([o["sgb"] for o in small["od"]]).reshape(-1)])
    pack2 = jnp.concatenate([
        _rows(small["dmod"]), _rows(small["dnw"]), _rows(misc), _rows(jnp.stack([e["cw"] for e in small["ev"]])),
        _rows(jnp.stack([o["pw"] for o in small["od"]])), _rows(jnp.stack([o["sgw"] for o in small["od"]]))], axis=0)
    n2 = pack2.shape[0]
    g3 = _all_gather8("gather_grads", pack2)
    tot = _sum8("sum_grads", g3.reshape(8, n2, dm))
    r_mod = depth * 16
    o_nw, o_misc = r_mod, r_mod + 8
    o_cw = o_misc + 16
    o_pw = o_cw + 2 * (CONV_K + 1) // 2
    o_sgw = o_pw + 128
    dmod_sum = tot[:r_mod].reshape(depth, 2, 8, dm)
    dmod_dev = g3.reshape(8, n2, dm)[:, :r_mod].reshape(8, depth, 2, 8, dm)
    dm_x = dmod_dev[:, :, 1, :6].reshape(8, depth, 6 * dm).transpose(1, 0, 2)
    dm_c = dmod_sum[:, 0, :6].reshape(depth, 1, 6 * dm)
    dmod16 = jnp.concatenate([dm_x, dm_c, jnp.zeros((depth, 7, 6 * dm), F32)], axis=1)
    g_ada_b = _rowsum16("ada_b_grad", dmod16)[:, 0]
    g_ada_w, dc16 = _ada_bwd("ada_bwd", c16, _my_cols(dmod16, chip, n_ada), ada_w)
    g4 = _all_gather8("gather_cctx", dc16[8:16]).reshape(8, 8, dm)
    g_c_ctx = _cctx_grad("cctx_grad", g4[0::2, 0], c_ctx[None, :])[0]

    misc_t = tot[o_misc:o_misc + 16]
    half = lambda row: misc_t[row].reshape(2, dm // 2)
    grads = dict(
        c_ctx=g_c_ctx, ada_w=g_ada_w, ada_b=g_ada_b,
        norm_w=_my_cols(tot[o_nw:o_nw + 8].reshape(depth, 2, dm), chip, dm // 4),
        ret_decay_logit=misc_t[7, :4 * HEADS].reshape(ret_decay_logit.shape),
        conv_dw_w=_my_cols(tot[o_cw:o_cw + 2 * (CONV_K + 1) // 2].reshape(2, CONV_K + 1, dm // 2)[:, :CONV_K], chip, 128),
        conv_ln_w=half(2), conv_ln_b=half(3),
        pool_w=tot[o_pw:o_pw + 128].reshape(pool_w.shape),
        pool_scale=_my_cols(half(4), chip, 128), sg_ln_w=_my_cols(half(5), chip, 128), sg_ln_b=_my_cols(half(6), chip, 128),
        sg_w=tot[o_sgw:o_sgw + 128].reshape(sg_w.shape), sg_b=misc_t[8].reshape(sg_b.shape),
        final_norm_w=misc_t[0])
    loss = misc_t[1, 0]

    last_tokens = [finish_pair(stages[-1], [g_c_ctx])[0][4]]
    deltas, new_m, new_v = {}, {}, {}
    for n in WEIGHTS:
        if n not in BIG.values():
            deltas[n], new_m[n], new_v[n] = _adamw("adamw_" + n, wv[n], grads[n], mv[n], vv[n])
    reduced = {k: None for k in names}
    for stage in stages:
        glist, from_sib, st = pending[stage]
        last = stage == stages[-1]
        after = [deltas["ada_w"]] + [reduced[k] for k, _ in stage_keys(stages[-2])] if last else last_tokens
        slots = _split_wait(f"rs_wait{stage[0]}{stage[1]}", st, _scatter_copies(len(glist)), after)[1]
        keys = stage_keys(stage)
        groups = [range(len(keys))] if stage[1] == "f" else [[t] for t in range(len(keys))]
        for grp in groups:
            ks = [keys[t][0] for t in grp]
            lyr = keys[grp[0]][1]
            outs = _sum_final(f"rs_sum_{ks[0]}{lyr}", [glist[t] for t in grp], [from_sib[t] for t in grp],
                              [slots[t] for t in grp], idx, [reduced[k] for k in ks], lyr, shard[ks[0]].shape[0])
            for k, o in zip(ks, outs):
                reduced[k] = o
    shards = dict(zip(names, _rs_share("rs_share", [reduced[k] for k in names])))

    for k in names:
        n = BIG[k]
        tr = (lambda a: jnp.swapaxes(a, 1, 2)) if k in tr_names else (lambda a: a)
        outs = _adamw("adamw_" + n, tr(wv[n]), shards[k], tr(mv[n]), tr(vv[n]), with_grad=True)
        deltas[n], new_m[n], new_v[n], grads[n] = (tr(o) for o in outs)
    grad_x = dxs[None]
    return (loss, grad_x, *[grads[n] for n in WEIGHTS], *[deltas[n] for n in WEIGHTS],
            *[new_m[n] for n in WEIGHTS], *[new_v[n] for n in WEIGHTS])
```

```python
import functools
import math

import jax
import jax.numpy as jnp
from jax import lax
from jax.experimental import pallas as pl
from jax.experimental.pallas import tpu as pltpu

F32 = jnp.float32
BF16 = jnp.bfloat16
MESH = pl.DeviceIdType.MESH

EPS = 1e-6
GRID_W = 64
HEADS = 4
HEAD_DIM = 128
CHUNK = 128
CONV_K = 31
ROPE_BASE = 10000.0
ROPE_PAIRS = (16, 24, 24)
POOL_WINDOWS = (2, 4, 8, 16)
ADAM_LR, ADAM_B1, ADAM_B2, ADAM_EPS, ADAM_WD, ADAM_STEP = 0.001, 0.9, 0.999, 1e-08, 0.01, 10

ROW_TILE = 256
WIDE_ROWS = 576
CONV_HALO = 16
POOL_HALO = 8
VMEM_LIMIT = 56 * 1024 * 1024
WGRAD_ROWS = 2304


def _pcall(body, **kw):
    return pl.pallas_call(body, **kw)


def _cp(sem=None, vmem=VMEM_LIMIT):
    if sem is None:
        return pltpu.CompilerParams(vmem_limit_bytes=vmem)
    return pltpu.CompilerParams(dimension_semantics=sem, vmem_limit_bytes=vmem)


def _sds(shape, dtype=F32):
    return jax.ShapeDtypeStruct(tuple(shape), dtype)


def _full(shape):
    nd = len(shape)
    return pl.BlockSpec(tuple(shape), lambda *_: (0,) * nd)


def _sigmoid(x):
    return jax.nn.sigmoid(x)


def _silu(x):
    return x * _sigmoid(x)


def _dsilu(x):
    s = _sigmoid(x)
    return s * (1.0 + x * (1.0 - s))


def _colsum(a):
    return jnp.sum(a, axis=0, keepdims=True)


def _dot(a, b, dn):
    return lax.dot_general(a.astype(BF16), b.astype(BF16), dn, preferred_element_type=F32)


NN = (((1,), (0,)), ((), ()))
NT = (((1,), (1,)), ((), ()))
TN = (((0,), (0,)), ((), ()))


def _mm_tile(t, cap=1152):
    best = 16
    for d in range(16, min(t, cap) + 1, 16):
        if t % d == 0:
            best = d
    return best


def _mm(name, pairs, grid, out_shape, out_spec, dn):
    npairs = len(pairs)
    nk = grid[-1]
    kax = len(grid) - 1
    assert nk == 1 or out_shape.dtype == F32

    def body(*refs):
        ins = refs[:2 * npairs]
        o_ref = refs[2 * npairs]
        tot = None
        for p in range(npairs):
            d = _dot(ins[2 * p][...], ins[2 * p + 1][...], dn)
            tot = d if tot is None else tot + d
        if nk == 1:
            o_ref[...] = tot.astype(o_ref.dtype)
        else:
            k = pl.program_id(kax)

            @pl.when(k == 0)
            def _():
                o_ref[...] = tot

            @pl.when(k != 0)
            def _():
                o_ref[...] += tot

    args, in_specs = [], []
    for a, a_spec, b, b_spec in pairs:
        args += [a, b]
        in_specs += [a_spec, b_spec]
    sem = ("parallel",) * kax + ("arbitrary",)
    return _pcall(body, name=name, grid=grid, in_specs=in_specs, out_specs=out_spec, out_shape=out_shape,
                  compiler_params=_cp(sem))(*args)


def _mm_cols(name, a, w, out_dtype=F32):
    t, k = a.shape
    j, _, n = w.shape
    tm = _mm_tile(t)
    return _mm(name, [(a, pl.BlockSpec((tm, k), lambda i, jj, kk: (i, 0)),
                       w, pl.BlockSpec((None, k, n), lambda i, jj, kk: (jj, 0, 0)))],
               (t // tm, j, 1), _sds((t, j * n), out_dtype), pl.BlockSpec((tm, n), lambda i, jj, kk: (i, jj)), NN)


def _mm_cols_bwd(name, d, w):
    t = d.shape[0]
    j, k, n = w.shape
    tm = _mm_tile(t)
    pairs = [(d, pl.BlockSpec((tm, n), functools.partial(lambda jj, i, u, kk: (i, jj), jj)),
              w, pl.BlockSpec((None, k, n), functools.partial(lambda jj, i, u, kk: (jj, 0, 0), jj))) for jj in range(j)]
    return _mm(name, pairs, (t // tm, 1, 1), _sds((t, k), BF16), pl.BlockSpec((tm, k), lambda i, u, kk: (i, 0)), NT)


def _mm_full(name, a, w, dn, tm=None):
    t, k = a.shape
    n = w.shape[1] if dn is NN else w.shape[0]
    tm = tm or _mm_tile(t)
    return _mm(name, [(a, pl.BlockSpec((tm, k), lambda i, u, kk: (i, 0)), w, _full(w.shape))],
               (t // tm, 1, 1), _sds((t, n)), pl.BlockSpec((tm, n), lambda i, u, kk: (i, 0)), dn)


def _wgrad_cols(name, a, b, j):
    t, k = a.shape
    n = b.shape[1] // j
    tt = _mm_tile(t, 2 * WGRAD_ROWS)
    return _mm(name, [(a, pl.BlockSpec((tt, k), lambda jj, u, kk: (kk, 0)),
                       b, pl.BlockSpec((tt, n), lambda jj, u, kk: (kk, jj)))],
               (j, 1, t // tt), _sds((j, k, n)), pl.BlockSpec((None, k, n), lambda jj, u, kk: (jj, 0, 0)), TN)


def _wgrad_rows(name, a, blk, b):
    t, f = a.shape
    n = b.shape[1]
    tt = _mm_tile(t, WGRAD_ROWS)
    return _mm(name, [(a, pl.BlockSpec((tt, blk), lambda jj, u, kk: (kk, jj)),
                       b, pl.BlockSpec((tt, n), lambda jj, u, kk: (kk, 0)))],
               (f // blk, 1, t // tt), _sds((f, n)), pl.BlockSpec((blk, n), lambda jj, u, kk: (jj, 0)), TN)


def _ffn_tiles(t, f):
    return _mm_tile(t, 288), f


def _ffn_up(name, h, wgt, wut):
    t, k = h.shape
    f = wgt.shape[0]
    tm, tn = _ffn_tiles(t, f)

    def body(h_ref, wg_ref, wu_ref, a_ref, gt_ref, up_ref):
        hv = h_ref[...]
        gt = _dot(hv, wg_ref[...], NT)
        up = _dot(hv, wu_ref[...], NT)
        a_ref[...] = (_silu(gt) * up).astype(BF16)
        gt_ref[...] = gt.astype(BF16)
        up_ref[...] = up.astype(BF16)

    wspec = pl.BlockSpec((tn, k), lambda i, jj: (jj, 0))
    ospec = pl.BlockSpec((tm, tn), lambda i, jj: (i, jj))
    o = _sds((t, f), BF16)
    return _pcall(body, name=name, grid=(t // tm, f // tn),
                  in_specs=[pl.BlockSpec((tm, k), lambda i, jj: (i, 0)), wspec, wspec],
                  out_specs=[ospec, ospec, ospec], out_shape=[o, o, o],
                  compiler_params=_cp(("parallel", "parallel")))(h, wgt, wut)


def _ffn_down_bwd(name, df, wd, gt, up):
    t, dm = df.shape
    f = wd.shape[0]
    tm, tn = _ffn_tiles(t, f)

    def body(df_ref, wd_ref, gt_ref, up_ref, dgt_ref, dup_ref):
        da = _dot(df_ref[...], wd_ref[...], NT)
        g = gt_ref[...].astype(F32)
        u = up_ref[...].astype(F32)
        s = _sigmoid(g)
        dgt_ref[...] = (da * u * (s * (1.0 + g * (1.0 - s)))).astype(BF16)
        dup_ref[...] = (da * (g * s)).astype(BF16)

    aspec = pl.BlockSpec((tm, tn), lambda i, jj: (i, jj))
    o = _sds((t, f), BF16)
    return _pcall(body, name=name, grid=(t // tm, f // tn),
                  in_specs=[pl.BlockSpec((tm, dm), lambda i, jj: (i, 0)),
                            pl.BlockSpec((tn, dm), lambda i, jj: (jj, 0)), aspec, aspec],
                  out_specs=[aspec, aspec], out_shape=[o, o],
                  compiler_params=_cp(("parallel", "parallel")))(df, wd, gt, up)


def _ffn_in_bwd(name, dgt, dup, wgt, wut):
    t, f = dgt.shape
    k = wgt.shape[1]
    tm = _mm_tile(t, 576)
    aspec = pl.BlockSpec((tm, f), lambda i, u, kk: (i, 0))
    wspec = pl.BlockSpec((f, k), lambda i, u, kk: (0, 0))
    return _mm(name, [(dgt, aspec, wgt, wspec), (dup, aspec, wut, wspec)], (t // tm, 1, 1), _sds((t, k), BF16),
               pl.BlockSpec((tm, k), lambda i, u, kk: (i, 0)), NN)


def _modrow(ref, row, is_ctx):
    return jnp.where(is_ctx, ref[0, row:row + 1, :], ref[1, row:row + 1, :])


def _rnm(name, x, delta, mod_g, g_row, mod_n, sh_row, sc_row, nw, nct, deps=()):
    t, dm = x.shape
    tm = _mm_tile(t, WIDE_ROWS)
    has = delta is not None
    nd = len(deps)

    def body(*refs):
        refs = refs[:len(refs) - nd - (2 if has else 1)] + refs[len(refs) - (2 if has else 1):]
        if has:
            x_ref, d_ref, mg_ref, m_ref, nw_ref, xo_ref, h_ref = refs
        else:
            x_ref, m_ref, nw_ref, h_ref = refs
        is_ctx = pl.program_id(0) * tm + lax.broadcasted_iota(jnp.int32, (tm, 1), 0) < nct * ROW_TILE
        xv = x_ref[...]
        if has:
            xv = xv + _modrow(mg_ref, g_row, is_ctx) * d_ref[...]
            xo_ref[...] = xv
        r = lax.rsqrt(jnp.mean(xv * xv, axis=-1, keepdims=True) + EPS)
        hv = (xv * r * nw_ref[...]) * (1.0 + _modrow(m_ref, sc_row, is_ctx)) + _modrow(m_ref, sh_row, is_ctx)
        h_ref[...] = hv.astype(BF16)

    row = pl.BlockSpec((tm, dm), lambda i: (i, 0))
    ins = [x] + ([delta, mod_g] if has else []) + [mod_n, nw] + list(deps)
    in_specs = ([row] + ([row, _full(mod_g.shape)] if has else []) + [_full(mod_n.shape), _full(nw.shape)]
                + [_full(d.shape) for d in deps])
    outs = ([_sds((t, dm))] if has else []) + [_sds((t, dm), BF16)]
    out_specs = ([row] if has else []) + [row]
    res = _pcall(body, name=name, grid=(t // tm,), in_specs=in_specs, out_specs=out_specs, out_shape=outs,
                 compiler_params=_cp(("parallel",)))(*ins)
    return res if has else (None, res[0])


def _bnm(name, xn, dh, dup, yprev, mod_n, sh_row, sc_row, mod_g, g_row, nw, nct, deps=()):
    t, dm = xn.shape
    tm = ROW_TILE
    has = yprev is not None
    nd = len(deps)

    def body(*refs):
        nout = 3 if has else 2
        refs = refs[:len(refs) - nd - nout] + refs[len(refs) - nout:]
        if has:
            x_ref, dh_ref, du_ref, y_ref, mn_ref, mg_ref, nw_ref, dx_ref, dd_ref, s_ref = refs
        else:
            x_ref, dh_ref, du_ref, mn_ref, nw_ref, dx_ref, s_ref = refs
        i = pl.program_id(0)
        is_ctx = i < nct

        @pl.when(i == 0)
        def _():
            s_ref[...] = jnp.zeros_like(s_ref)

        xv = x_ref[...]
        r = lax.rsqrt(jnp.mean(xv * xv, axis=-1, keepdims=True) + EPS)
        xh = xv * r
        w = nw_ref[...]
        sc1 = 1.0 + _modrow(mn_ref, sc_row, is_ctx)
        dhv = dh_ref[...].astype(F32)
        dxh = dhv * sc1 * w
        dx = r * (dxh - xh * jnp.mean(dxh * xh, axis=-1, keepdims=True)) + du_ref[...]
        dx_ref[...] = dx
        parts = [_colsum(dhv), _colsum(dhv * (xh * w))]
        if has:
            dd_ref[...] = (_modrow(mg_ref, g_row, is_ctx) * dx).astype(BF16)
            parts.append(_colsum(dx * y_ref[...]))
        else:
            parts.append(jnp.zeros((1, dm), F32))
        upd = jnp.concatenate(parts + [jnp.zeros((5, dm), F32)], axis=0)
        dnw = jnp.concatenate([jnp.zeros((3, dm), F32), _colsum(dhv * sc1 * xh), jnp.zeros((4, dm), F32)], axis=0)

        @pl.when(is_ctx)
        def _():
            s_ref[0] += upd
            s_ref[1] += dnw

        @pl.when(jnp.logical_not(is_ctx))
        def _():
            s_ref[1] += upd + dnw

    row = pl.BlockSpec((tm, dm), lambda i: (i, 0))
    ins = [xn, dh, dup] + ([yprev] if has else []) + [mod_n] + ([mod_g] if has else []) + [nw] + list(deps)
    in_specs = ([row, row, row] + ([row] if has else []) + [_full(mod_n.shape)]
                + ([_full(mod_g.shape)] if has else []) + [_full(nw.shape)] + [_full(d.shape) for d in deps])
    if has:
        outs = [_sds((t, dm)), _sds((t, dm), BF16), _sds((2, 8, dm))]
        out_specs = [row, row, _full((2, 8, dm))]
    else:
        outs = [_sds((t - nct * tm, dm)), _sds((2, 8, dm))]
        out_specs = [pl.BlockSpec((tm, dm), lambda i: (jnp.maximum(i - nct, 0), 0)), _full((2, 8, dm))]
    res = _pcall(body, name=name, grid=(t // tm,), in_specs=in_specs, out_specs=out_specs, out_shape=outs,
                 compiler_params=_cp(("arbitrary",)))(*ins)
    return res if has else (res[0], None, res[1])


def _fin(name, x1, f, mod, g_row, fw, target, nct):
    t, dm = x1.shape
    tm = ROW_TILE

    def body(x_ref, f_ref, m_ref, fw_ref, t_ref, loss_ref, dx_ref, dd_ref, s_ref):
        i = pl.program_id(0)

        @pl.when(i == 0)
        def _():
            s_ref[...] = jnp.zeros_like(s_ref)
            loss_ref[...] = jnp.zeros_like(loss_ref)

        @pl.when(i < nct)
        def _():
            dx_ref[...] = jnp.zeros_like(dx_ref)
            dd_ref[...] = jnp.zeros_like(dd_ref)

        @pl.when(i >= nct)
        def _():
            g = m_ref[1, g_row:g_row + 1, :]
            fv = f_ref[...]
            xv = x_ref[...] + g * fv
            r = lax.rsqrt(jnp.mean(xv * xv, axis=-1, keepdims=True) + EPS)
            xh = xv * r
            w = fw_ref[...]
            err = xh * w - t_ref[...]
            loss_ref[...] += 0.5 * jnp.sum(err * err) / dm
            dout = err * (1.0 / dm)
            dxh = dout * w
            dx = r * (dxh - xh * jnp.mean(dxh * xh, axis=-1, keepdims=True))
            dx_ref[...] = dx
            dd_ref[...] = (g * dx).astype(BF16)
            s_ref[...] += jnp.concatenate([_colsum(dx * fv), _colsum(dout * xh), jnp.zeros((6, dm), F32)], axis=0)

    row = pl.BlockSpec((tm, dm), lambda i: (i, 0))
    trow = pl.BlockSpec((tm, dm), lambda i: (jnp.maximum(i - nct, 0), 0))
    return _pcall(body, name=name, grid=(t // tm,),
                  in_specs=[row, row, _full(mod.shape), _full(fw.shape), trow],
                  out_specs=[_full((8, 128)), row, row, _full((8, dm))],
                  out_shape=[_sds((8, 128)), _sds((t, dm)), _sds((t, dm), BF16), _sds((8, dm))],
                  compiler_params=_cp(("arbitrary",)))(x1, f, mod, fw, target)


def _rope_tables(t, lc):
    l = t - lc
    rows = l // GRID_W
    grid_r = jnp.broadcast_to(jnp.arange(rows, dtype=F32)[:, None], (rows, GRID_W)).reshape(-1)
    grid_c = jnp.broadcast_to(jnp.arange(GRID_W, dtype=F32)[None, :], (rows, GRID_W)).reshape(-1)

    def angles(p_seq, p_row, p_col):
        parts = []
        for p, n in zip((p_seq, p_row, p_col), ROPE_PAIRS):
            freq = ROPE_BASE ** (-jnp.arange(n, dtype=F32) / n)
            parts.append(p[:, None] * freq[None, :])
        return jnp.concatenate(parts, axis=-1)

    zc = jnp.zeros((lc,), F32)
    ang = jnp.concatenate([angles(jnp.arange(lc, dtype=F32), zc, zc),
                           angles(jnp.full((l,), lc, F32), grid_r, grid_c)], axis=0)
    cos, sin = jnp.cos(ang), jnp.sin(ang)
    return jnp.concatenate([cos, cos], axis=-1), jnp.concatenate([-sin, sin], axis=-1)


def _rope(u, cs, sn):
    return u * cs + pltpu.roll(u, HEAD_DIM // 2, 1) * sn


def _rope_t(d, cs, sn):
    return d * cs + pltpu.roll(d * sn, HEAD_DIM // 2, 1)


def _even_qkv(name, p, cs, sn):
    t = p.shape[0]
    tm = _mm_tile(t, WIDE_ROWS)
    w = HEADS * HEAD_DIM
    scale = HEAD_DIM ** -0.5

    def body(q_ref, k_ref, v_ref, cs_ref, sn_ref, qo_ref, ko_ref, vo_ref):
        c, s = cs_ref[...], sn_ref[...]
        for h in range(HEADS):
            sl = slice(h * HEAD_DIM, (h + 1) * HEAD_DIM)
            qo_ref[:, sl] = (_rope(q_ref[:, sl], c, s) * scale).astype(BF16)
            ko_ref[:, sl] = _rope(k_ref[:, sl], c, s).astype(BF16)
        vo_ref[...] = v_ref[...].astype(BF16)

    col = lambda j: pl.BlockSpec((tm, w), lambda i: (i, j))
    tab = pl.BlockSpec((tm, HEAD_DIM), lambda i: (i, 0))
    o = _sds((t, w), BF16)
    return _pcall(body, name=name, grid=(t // tm,), in_specs=[col(0), col(1), col(2), tab, tab],
                  out_specs=[col(0)] * 3, out_shape=[o, o, o], compiler_params=_cp(("parallel",)))(p, p, p, cs, sn)


def _log_sigmoid_row(x):
    e = jnp.exp(-jnp.abs(x))
    l1p = jnp.where(e < 0.01, e * (1.0 - e * (0.5 - e * (1.0 / 3.0))), jnp.log(1.0 + e))
    return jnp.minimum(x, 0.0) - l1p


def _ret_tables(lgb_ref, dm_ref, xi_ref, zt_ref):
    ri = lax.broadcasted_iota(jnp.int32, (CHUNK, CHUNK), 0).astype(F32)
    ci = lax.broadcasted_iota(jnp.int32, (CHUNK, CHUNK), 1).astype(F32)
    for d in range(2):
        for h in range(HEADS):
            idx = d * HEADS + h
            lg = _log_sigmoid_row(lgb_ref[idx:idx + 1, :])
            if d == 0:
                e, mask = ri - ci, ri >= ci
                xe, ze = ri + 1.0, (CHUNK - 1.0) - ri
            else:
                e, mask = ci - ri - 1.0, ci > ri
                xe, ze = (CHUNK - 1.0) - ri, ri
            dm_ref[idx] = jnp.where(mask, jnp.exp(lg * jnp.where(mask, e, 0.0)), 0.0)
            xi_ref[idx] = jnp.exp(lg * xe)
            zt_ref[idx] = jnp.exp(lg * ze)


def _ret_exponents(d):
    ri = lax.broadcasted_iota(jnp.int32, (CHUNK, CHUNK), 0).astype(F32)
    ci = lax.broadcasted_iota(jnp.int32, (CHUNK, CHUNK), 1).astype(F32)
    if d == 0:
        return ri - ci, ri + 1.0, (CHUNK - 1.0) - ri
    return ci - ri - 1.0, (CHUNK - 1.0) - ri, ri


RET_SUB = 2


def _bwd_chunk(n, ncc, nc):
    return jnp.where(n < ncc, ncc - 1 - n, nc - 1 - (n - ncc))


def _retention_fwd(name, q, k, v, lgb, lc):
    t, w = q.shape
    nc = t // CHUNK
    rows_per = RET_SUB * CHUNK
    nb, ncb = t // rows_per, lc // rows_per
    nh = 2 * HEADS

    def body(qf_ref, kf_ref, vf_ref, qb_ref, kb_ref, vb_ref, lgb_ref, of_ref, ob_ref, ss_ref,
             s_ref, dm_ref, xi_ref, zt_ref):
        n = pl.program_id(0)

        @pl.when(n == 0)
        def _():
            s_ref[...] = jnp.zeros_like(s_ref)
            _ret_tables(lgb_ref, dm_ref, xi_ref, zt_ref)

        where = []
        for u in range(RET_SUB):
            for d in range(2):
                refs = (qf_ref, kf_ref, vf_ref, of_ref) if d == 0 else (qb_ref, kb_ref, vb_ref, ob_ref)
                r0 = (u if d == 0 else RET_SUB - 1 - u) * CHUNK
                for h in range(HEADS):
                    where.append((u, d * HEADS + h, refs, slice(r0, r0 + CHUNK), slice(h * HEAD_DIM, (h + 1) * HEAD_DIM)))
        qs = [refs[0][rows, sl] for _, _, refs, rows, sl in where]
        ks = [refs[1][rows, sl] for _, _, refs, rows, sl in where]
        vs = [refs[2][rows, sl] for _, _, refs, rows, sl in where]
        sc = [_dot(qv, kv, NT) for qv, kv in zip(qs, ks)]
        upd = [_dot(kv.astype(F32) * zt_ref[idx], vv, TN) for (_, idx, *_), kv, vv in zip(where, ks, vs)]
        cur = [s_ref[idx] for idx in range(nh)]
        gcs = [jnp.exp(_log_sigmoid_row(lgb_ref[idx:idx + 1, :]) * float(CHUNK)) for idx in range(nh)]
        st = []
        for (u, idx, *_), du in zip(where, upd):
            st.append(cur[idx])
            ss_ref[u, idx] = cur[idx]
            cur[idx] = gcs[idx] * cur[idx] + du
        for idx in range(nh):
            s_ref[idx] = cur[idx]
        inter = [_dot(qv.astype(F32) * xi_ref[idx], s, NN) for (_, idx, *_), qv, s in zip(where, qs, st)]
        intra = [_dot(a * dm_ref[idx], vv, NN) for (_, idx, *_), a, vv in zip(where, sc, vs)]
        for (_, _, refs, rows, sl), o1, o2 in zip(where, intra, inter):
            refs[3][rows, sl] = o1 + o2

    fspec = pl.BlockSpec((rows_per, w), lambda n: (n, 0))
    bspec = pl.BlockSpec((rows_per, w), lambda n: (_bwd_chunk(n, ncb, nb), 0))
    tab = pltpu.VMEM((nh, CHUNK, CHUNK), F32)
    return _pcall(body, name=name, grid=(nb,),
                  in_specs=[fspec] * 3 + [bspec] * 3 + [_full((nh, HEAD_DIM))],
                  out_specs=[fspec, bspec, pl.BlockSpec((RET_SUB, nh, CHUNK, CHUNK), lambda n: (n, 0, 0, 0))],
                  out_shape=[_sds((t, w)), _sds((t, w)), _sds((nc, nh, CHUNK, CHUNK))],
                  scratch_shapes=[tab, tab, tab, tab],
                  compiler_params=_cp(("arbitrary",)))(q, k, v, q, k, v, lgb)


def _retention_bwd(name, q, k, v, do, ss, lgb, lc):
    t, w = q.shape
    rows_per = RET_SUB * CHUNK
    nb, ncb = t // rows_per, lc // rows_per
    nh = 2 * HEADS

    def body(qf_ref, kf_ref, vf_ref, gf_ref, qb_ref, kb_ref, vb_ref, gb_ref, ss_ref, lgb_ref,
             dqf_ref, dkf_ref, dvf_ref, dqb_ref, dkb_ref, dvb_ref, dl_ref,
             ds_ref, dm_ref, xi_ref, zt_ref, acc_ref):
        n = pl.program_id(0)

        @pl.when(n == 0)
        def _():
            ds_ref[...] = jnp.zeros_like(ds_ref)
            acc_ref[...] = jnp.zeros_like(acc_ref)
            _ret_tables(lgb_ref, dm_ref, xi_ref, zt_ref)

        where = []
        for u in reversed(range(RET_SUB)):
            for d in range(2):
                refs = ((qf_ref, kf_ref, vf_ref, gf_ref, dqf_ref, dkf_ref, dvf_ref) if d == 0
                        else (qb_ref, kb_ref, vb_ref, gb_ref, dqb_ref, dkb_ref, dvb_ref))
                r0 = (u if d == 0 else RET_SUB - 1 - u) * CHUNK
                for h in range(HEADS):
                    where.append((u, d * HEADS + h, d, refs, slice(r0, r0 + CHUNK), slice(h * HEAD_DIM, (h + 1) * HEAD_DIM)))
        qs = [refs[0][rows, sl] for *_, refs, rows, sl in where]
        ks = [refs[1][rows, sl] for *_, refs, rows, sl in where]
        vs = [refs[2][rows, sl] for *_, refs, rows, sl in where]
        gs = [refs[3][rows, sl] for *_, refs, rows, sl in where]
        st = [ss_ref[u, idx] for u, idx, *_ in where]
        sc = [_dot(qv, kv, NT) for qv, kv in zip(qs, ks)]
        dar = [_dot(gv, vv, NT) for gv, vv in zip(gs, vs)]
        t1 = [_dot(gv, s, NT) for gv, s in zip(gs, st)]
        dsn = [_dot(qv.astype(F32) * xi_ref[idx], gv, TN) for (_, idx, *_), qv, gv in zip(where, qs, gs)]
        cur = [ds_ref[idx] for idx in range(nh)]
        gcs = [jnp.exp(_log_sigmoid_row(lgb_ref[idx:idx + 1, :]) * float(CHUNK)) for idx in range(nh)]
        dsps = []
        for (_, idx, *_), x in zip(where, dsn):
            dsps.append(cur[idx])
            cur[idx] = gcs[idx] * cur[idx] + x
        for idx in range(nh):
            ds_ref[idx] = cur[idx]
        t2 = [_dot(vv, dsp, NT) for vv, dsp in zip(vs, dsps)]
        dv2 = [_dot(kv.astype(F32) * zt_ref[idx], dsp, NN) for (_, idx, *_), kv, dsp in zip(where, ks, dsps)]
        a = [x * dm_ref[idx] for (_, idx, *_), x in zip(where, sc)]
        da = [x * dm_ref[idx] for (_, idx, *_), x in zip(where, dar)]
        dq = [_dot(x, kv, NN) for x, kv in zip(da, ks)]
        dk = [_dot(x, qv, TN) for x, qv in zip(da, qs)]
        dv1 = [_dot(x, gv, TN) for x, gv in zip(a, gs)]
        for i8, (_, idx, d, refs, rows, sl) in enumerate(where):
            ee, xe, ze = _ret_exponents(d)
            xi, zt = xi_ref[idx], zt_ref[idx]
            qf32, kf32 = qs[i8].astype(F32), ks[i8].astype(F32)
            refs[4][rows, sl] = dq[i8] + xi * t1[i8]
            refs[5][rows, sl] = dk[i8] + zt * t2[i8]
            refs[6][rows, sl] = dv1[i8] + dv2[i8]
            acc_ref[idx] += (ee * a[i8] * dar[i8] + xe * xi * qf32 * t1[i8] + ze * zt * kf32 * t2[i8]
                             + (float(CHUNK) * gcs[idx]) * dsps[i8] * st[i8])

        @pl.when(n == nb - 1)
        def _():
            for idx in range(nh):
                tot = jnp.sum(acc_ref[idx])
                dl_ref[idx:idx + 1, :] = tot * _sigmoid(-lgb_ref[idx:idx + 1, :])

    fmap = lambda n: (nb - 1 - n, 0)
    bmap = lambda n: (_bwd_chunk(nb - 1 - n, ncb, nb), 0)
    fspec = pl.BlockSpec((rows_per, w), fmap)
    bspec = pl.BlockSpec((rows_per, w), bmap)
    tab = pltpu.VMEM((nh, CHUNK, CHUNK), F32)
    o = _sds((t, w))
    return _pcall(body, name=name, grid=(nb,),
                  in_specs=[fspec] * 4 + [bspec] * 4
                  + [pl.BlockSpec((RET_SUB, nh, CHUNK, CHUNK), lambda n: (nb - 1 - n, 0, 0, 0)), _full((nh, HEAD_DIM))],
                  out_specs=[fspec] * 3 + [bspec] * 3 + [_full((nh, HEAD_DIM))],
                  out_shape=[o] * 6 + [_sds((nh, HEAD_DIM))],
                  scratch_shapes=[tab, tab, tab, tab, tab],
                  compiler_params=_cp(("arbitrary",)))(q, k, v, do, q, k, v, do, ss, lgb)


def _halo_specs(tm, halo, t, width, col):
    hb = tm // halo
    last = t // halo - 1
    prev = pl.BlockSpec((halo, width), lambda i: (jnp.maximum(i * hb - 1, 0), col))
    nxt = pl.BlockSpec((halo, width), lambda i: (jnp.minimum((i + 1) * hb, last), col))
    return prev, nxt


def _halo_valid(i, nct, nt):
    vp = jnp.logical_and(i != 0, i != nct)
    vn = jnp.logical_and(i != nct - 1, i != nt - 1)
    return vp, vn


def _fill_window(win_ref, prev, cur, nxt, vp, vn, halo, tm):
    win_ref[0:halo, :] = jnp.where(vp, prev, 0.0)
    win_ref[halo:halo + tm, :] = cur
    win_ref[halo + tm:halo + tm + halo, :] = jnp.where(vn, nxt, 0.0)


CONV_SUB = 64


SUBLANES = 8


def _shift_window(win_ref, sh_ref, tm):
    rows = tm + 2 * CONV_HALO - SUBLANES
    for s in range(SUBLANES):
        sh_ref[s, 0:rows, :] = win_ref[s:s + rows, :]


def _window_rows(sh_ref, start, rows):
    s = start % SUBLANES
    return sh_ref[s, start - s:start - s + rows, :]


def _conv_taps(sh_ref, w_ref, tm, flip):
    outs = []
    for r0 in range(0, tm, CONV_SUB):
        acc = None
        for kk in range(CONV_K):
            wk = (CONV_K - 1 - kk) if flip else kk
            term = w_ref[wk:wk + 1, :] * _window_rows(sh_ref, r0 + kk + 1, CONV_SUB)
            acc = term if acc is None else acc + term
        outs.append(acc)
    return jnp.concatenate(outs, axis=0)


def _head_norm(y):
    r = lax.rsqrt(jnp.mean(y * y, axis=-1, keepdims=True) + EPS)
    return y * r, r


def _ln_stats(y):
    mu = jnp.mean(y, axis=-1, keepdims=True)
    yc = y - mu
    rs = lax.rsqrt(jnp.mean(yc * yc, axis=-1, keepdims=True) + EPS)
    return yc * rs, rs


def _ln_bwd(dyh, yh, rs):
    return rs * (dyh - jnp.mean(dyh, axis=-1, keepdims=True) - yh * jnp.mean(dyh * yh, axis=-1, keepdims=True))


def _even_mix(name, p, of, ob, cw, lnw, lnb, nct):
    t = p.shape[0]
    tm, halo = ROW_TILE, CONV_HALO
    nt = t // tm
    w = HEADS * HEAD_DIM

    def body(g_ref, a_ref, gb_ref, ap_ref, gbp_ref, an_ref, gbn_ref, of_ref, ob_ref, cw_ref, lw_ref, lb_ref,
             mix_ref, yc_ref, win_ref, sh_ref):
        i = pl.program_id(0)
        vp, vn = _halo_valid(i, nct, nt)
        glu = lambda a, b: a * _sigmoid(b)
        _fill_window(win_ref, glu(ap_ref[...], gbp_ref[...]), glu(a_ref[...], gb_ref[...]),
                     glu(an_ref[...], gbn_ref[...]), vp, vn, halo, tm)
        _shift_window(win_ref, sh_ref, tm)
        yc = _conv_taps(sh_ref, cw_ref, tm, False)
        yc_ref[...] = yc
        yh, _ = _ln_stats(yc)
        mix_ref[:, w:2 * w] = _silu(yh * lw_ref[...] + lb_ref[...]).astype(BF16)
        for h in range(HEADS):
            sl = slice(h * HEAD_DIM, (h + 1) * HEAD_DIM)
            yn, _ = _head_norm(of_ref[:, sl] + ob_ref[:, sl])
            mix_ref[:, sl] = (_silu(g_ref[:, sl]) * yn).astype(BF16)

    col = lambda j: pl.BlockSpec((tm, w), lambda i: (i, j))
    ap, an = _halo_specs(tm, halo, t, w, 4)
    gp, gn = _halo_specs(tm, halo, t, w, 5)
    row = pl.BlockSpec((tm, w), lambda i: (i, 0))
    return _pcall(body, name=name, grid=(nt,),
                  in_specs=[col(3), col(4), col(5), ap, gp, an, gn, row, row,
                            _full(cw.shape), _full(lnw.shape), _full(lnb.shape)],
                  out_specs=[pl.BlockSpec((tm, 2 * w), lambda i: (i, 0)), row],
                  out_shape=[_sds((t, 2 * w), BF16), _sds((t, w))],
                  scratch_shapes=[pltpu.VMEM((tm + 2 * halo, w), F32), pltpu.VMEM((SUBLANES, tm + 2 * halo, w), F32)],
                  compiler_params=_cp(("parallel",)))(p, p, p, p, p, p, p, of, ob, cw, lnw, lnb)


def _even_mix_bwd1(name, dmix, p, of, ob, yc, lnw, lnb):
    t = p.shape[0]
    tm = _mm_tile(t, WIDE_ROWS)
    w = HEADS * HEAD_DIM

    def body(dr_ref, dc_ref, g_ref, of_ref, ob_ref, yc_ref, lw_ref, lb_ref, do_ref, dg_ref, dyc_ref, s_ref):
        @pl.when(pl.program_id(0) == 0)
        def _():
            s_ref[...] = jnp.zeros_like(s_ref)

        for h in range(HEADS):
            sl = slice(h * HEAD_DIM, (h + 1) * HEAD_DIM)
            yn, r = _head_norm(of_ref[:, sl] + ob_ref[:, sl])
            gv = g_ref[:, sl]
            dr = dr_ref[:, sl]
            dg_ref[:, sl] = (dr * yn * _dsilu(gv)).astype(BF16)
            dyn = dr * _silu(gv)
            do_ref[:, sl] = (r * (dyn - yn * jnp.mean(dyn * yn, axis=-1, keepdims=True))).astype(BF16)
        yh, rs = _ln_stats(yc_ref[...])
        lw = lw_ref[...]
        dlo = dc_ref[...] * _dsilu(yh * lw + lb_ref[...])
        dyc_ref[...] = _ln_bwd(dlo * lw, yh, rs)
        s_ref[...] += jnp.concatenate([_colsum(dlo * yh), _colsum(dlo), jnp.zeros((6, w), F32)], axis=0)

    col = lambda j: pl.BlockSpec((tm, w), lambda i: (i, j))
    row = pl.BlockSpec((tm, w), lambda i: (i, 0))
    return _pcall(body, name=name, grid=(t // tm,),
                  in_specs=[col(0), col(1), col(3), row, row, row, _full(lnw.shape), _full(lnb.shape)],
                  out_specs=[row, row, row, _full((8, w))],
                  out_shape=[_sds((t, w), BF16), _sds((t, w), BF16), _sds((t, w)), _sds((8, w))],
                  compiler_params=_cp(("arbitrary",)))(dmix, dmix, p, of, ob, yc, lnw, lnb)


def _even_conv_bwd(name, dyc, p, cw, nct):
    t = p.shape[0]
    tm, halo = ROW_TILE, CONV_HALO
    nt = t // tm
    w = HEADS * HEAD_DIM

    def body(d_ref, dp_ref, dn_ref, a_ref, gb_ref, ap_ref, gbp_ref, an_ref, gbn_ref, cw_ref,
             da_ref, dgb_ref, dw_ref, dwin_ref, uwin_ref, dsh_ref, ush_ref):
        i = pl.program_id(0)

        @pl.when(i == 0)
        def _():
            dw_ref[...] = jnp.zeros_like(dw_ref)

        vp, vn = _halo_valid(i, nct, nt)
        glu = lambda a, b: a * _sigmoid(b)
        dcur = d_ref[...]
        _fill_window(dwin_ref, dp_ref[...], dcur, dn_ref[...], vp, vn, halo, tm)
        _fill_window(uwin_ref, glu(ap_ref[...], gbp_ref[...]), glu(a_ref[...], gb_ref[...]),
                     glu(an_ref[...], gbn_ref[...]), vp, vn, halo, tm)
        _shift_window(dwin_ref, dsh_ref, tm)
        _shift_window(uwin_ref, ush_ref, tm)
        du = _conv_taps(dsh_ref, cw_ref, tm, True)
        av = a_ref[...]
        sg = _sigmoid(gb_ref[...])
        da_ref[...] = (du * sg).astype(BF16)
        dgb_ref[...] = (du * av * sg * (1.0 - sg)).astype(BF16)
        rows = [_colsum(dcur * _window_rows(ush_ref, kk + 1, tm)) for kk in range(CONV_K)]
        dw_ref[...] += jnp.concatenate(rows + [jnp.zeros((1, w), F32)], axis=0)

    col = lambda j: pl.BlockSpec((tm, w), lambda i: (i, j))
    row = pl.BlockSpec((tm, w), lambda i: (i, 0))
    dp, dn = _halo_specs(tm, halo, t, w, 0)
    ap, an = _halo_specs(tm, halo, t, w, 4)
    gp, gn = _halo_specs(tm, halo, t, w, 5)
    win = pltpu.VMEM((tm + 2 * halo, w), F32)
    shifted = pltpu.VMEM((SUBLANES, tm + 2 * halo, w), F32)
    return _pcall(body, name=name, grid=(nt,),
                  in_specs=[row, dp, dn, col(4), col(5), ap, gp, an, gn, _full(cw.shape)],
                  out_specs=[row, row, _full((CONV_K + 1, w))],
                  out_shape=[_sds((t, w), BF16), _sds((t, w), BF16), _sds((CONV_K + 1, w))],
                  scratch_shapes=[win, win, shifted, shifted],
                  compiler_params=_cp(("arbitrary",)))(dyc, dyc, dyc, p, p, p, p, p, p, cw)


def _even_dp(name, dqs, dks, dvs, dg, da, dgb, cs, sn):
    t, w = dg.shape
    tm = _mm_tile(t, WIDE_ROWS)
    scale = HEAD_DIM ** -0.5

    def body(dqf_ref, dqb_ref, dkf_ref, dkb_ref, dvf_ref, dvb_ref, dg_ref, da_ref, dgb_ref, cs_ref, sn_ref, dp_ref):
        c, s = cs_ref[...], sn_ref[...]
        for h in range(HEADS):
            sl = slice(h * HEAD_DIM, (h + 1) * HEAD_DIM)
            dp_ref[:, sl] = (_rope_t(dqf_ref[:, sl] + dqb_ref[:, sl], c, s) * scale).astype(BF16)
            dp_ref[:, w + h * HEAD_DIM:w + (h + 1) * HEAD_DIM] = _rope_t(dkf_ref[:, sl] + dkb_ref[:, sl], c, s).astype(BF16)
        dp_ref[:, 2 * w:3 * w] = (dvf_ref[...] + dvb_ref[...]).astype(BF16)
        dp_ref[:, 3 * w:4 * w] = dg_ref[...]
        dp_ref[:, 4 * w:5 * w] = da_ref[...]
        dp_ref[:, 5 * w:6 * w] = dgb_ref[...]

    row = pl.BlockSpec((tm, w), lambda i: (i, 0))
    tab = pl.BlockSpec((tm, HEAD_DIM), lambda i: (i, 0))
    return _pcall(body, name=name, grid=(t // tm,), in_specs=[row] * 9 + [tab, tab],
                  out_specs=pl.BlockSpec((tm, 6 * w), lambda i: (i, 0)), out_shape=_sds((t, 6 * w), BF16),
                  compiler_params=_cp(("parallel",)))(dqs[0], dqs[1], dks[0], dks[1], dvs[0], dvs[1], dg, da, dgb, cs, sn)


GROUPS = 4
GC = 128
INV_SQRT2 = 0.7071067811865476
INV_SQRT_2PI = 0.3989422804014327


def _gelu(x):
    return 0.5 * x * (1.0 + lax.erf(x * INV_SQRT2))


def _dgelu(x):
    return 0.5 * (1.0 + lax.erf(x * INV_SQRT2)) + x * jnp.exp(-0.5 * x * x) * INV_SQRT_2PI


def _pool_count(i, nct, lc, t, tm, rows, row0, left, right):
    is_ctx = i < nct
    seg_start = jnp.where(is_ctx, 0, lc)
    seg_len = jnp.where(is_ctx, lc, t - lc)
    pos = i * tm + row0 - seg_start + lax.broadcasted_iota(jnp.int32, (rows, GC), 0)
    cnt = jnp.minimum(pos + right, seg_len - 1) - jnp.maximum(pos - left, 0) + 1
    return jnp.maximum(cnt, 1).astype(F32)


def _spatial_gate(vln, sgw_ref, sgb_ref, tm):
    cols = []
    for g in range(GROUPS):
        sl = slice(g * GC, (g + 1) * GC)
        parts = [_dot(sgw_ref[g], vln[r0:r0 + CHUNK, sl], NN) + sgb_ref[g] for r0 in range(0, tm, CHUNK)]
        cols.append(jnp.concatenate(parts, axis=0))
    return jnp.concatenate(cols, axis=1)


def _odd_mix(name, p, pw, pscale, lnw, lnb, sgw, sgb, nct, lc):
    t = p.shape[0]
    tm, halo = ROW_TILE, POOL_HALO
    nt = t // tm
    w = GROUPS * GC

    def body(pc_ref, pp_ref, pn_ref, pu_ref, pv_ref, pw_ref, ps_ref, lw_ref, lb_ref, sgw_ref, sgb_ref,
             mix_ref, m_ref, win_ref):
        i = pl.program_id(0)
        vp, vn = _halo_valid(i, nct, nt)
        pc = pc_ref[...]
        _fill_window(win_ref, pp_ref[...], pc, pn_ref[...], vp, vn, halo, tm)
        for g, wd in enumerate(POOL_WINDOWS):
            sl = slice(g * GC, (g + 1) * GC)
            left = wd // 2
            right = wd - 1 - left
            s = None
            for o in range(-left, right + 1):
                term = win_ref[halo + o:halo + o + tm, sl]
                s = term if s is None else s + term
            mg = s / _pool_count(i, nct, lc, t, tm, tm, 0, left, right) - pc[:, sl]
            m_ref[:, sl] = mg
            mix_ref[:, sl] = (_dot(mg, pw_ref[g], NN) * ps_ref[:, sl]).astype(BF16)
        u = _gelu(pu_ref[...])
        vh, _ = _ln_stats(_gelu(pv_ref[...]))
        s = _spatial_gate(vh * lw_ref[...] + lb_ref[...], sgw_ref, sgb_ref, tm)
        mix_ref[:, w:2 * w] = (u * s).astype(BF16)

    col = lambda j: pl.BlockSpec((tm, w), lambda i: (i, j))
    pp, pn = _halo_specs(tm, halo, t, w, 0)
    return _pcall(body, name=name, grid=(nt,),
                  in_specs=[col(0), pp, pn, col(1), col(2), _full(pw.shape), _full(pscale.shape),
                            _full(lnw.shape), _full(lnb.shape), _full(sgw.shape), _full(sgb.shape)],
                  out_specs=[pl.BlockSpec((tm, 2 * w), lambda i: (i, 0)), col(0)],
                  out_shape=[_sds((t, 2 * w), BF16), _sds((t, w))],
                  scratch_shapes=[pltpu.VMEM((tm + 2 * halo, w), F32)],
                  compiler_params=_cp(("parallel",)))(p, p, p, p, p, pw, pscale, lnw, lnb, sgw, sgb)


def _odd_mix_bwd1(name, dmix, p, m, pw, pscale, lnw, lnb, sgw, sgb):
    t = p.shape[0]
    tm = ROW_TILE
    w = GROUPS * GC

    def body(dpo_ref, dsg_ref, pu_ref, pv_ref, m_ref, pw_ref, ps_ref, lw_ref, lb_ref, sgw_ref, sgb_ref,
             dm_ref, dpd_ref, vec_ref, dpw_ref, dsgw_ref, dsgb_ref):
        @pl.when(pl.program_id(0) == 0)
        def _():
            vec_ref[...] = jnp.zeros_like(vec_ref)
            dpw_ref[...] = jnp.zeros_like(dpw_ref)
            dsgw_ref[...] = jnp.zeros_like(dsgw_ref)
            dsgb_ref[...] = jnp.zeros_like(dsgb_ref)

        dscale = []
        for g in range(GROUPS):
            sl = slice(g * GC, (g + 1) * GC)
            mg = m_ref[:, sl]
            dpo = dpo_ref[:, sl]
            dscale.append(_colsum(dpo * _dot(mg, pw_ref[g], NN)))
            dpo = dpo * ps_ref[:, sl]
            dm_ref[:, sl] = _dot(dpo, pw_ref[g], NT)
            dpw_ref[g] += _dot(mg, dpo, TN)
        pu, pv = pu_ref[...], pv_ref[...]
        u = _gelu(pu)
        vh, rs = _ln_stats(_gelu(pv))
        lw = lw_ref[...]
        vln = vh * lw + lb_ref[...]
        s = _spatial_gate(vln, sgw_ref, sgb_ref, tm)
        dsg = dsg_ref[...]
        dpd_ref[:, 0:w] = (dsg * s * _dgelu(pu)).astype(BF16)
        ds = dsg * u
        cols = []
        for g in range(GROUPS):
            sl = slice(g * GC, (g + 1) * GC)
            parts = []
            for r0 in range(0, tm, CHUNK):
                dsc = ds[r0:r0 + CHUNK, sl]
                parts.append(_dot(sgw_ref[g], dsc, TN))
                dsgw_ref[g] += _dot(dsc, vln[r0:r0 + CHUNK, sl], NT)
                dsgb_ref[g] += dsc
            cols.append(jnp.concatenate(parts, axis=0))
        dvln = jnp.concatenate(cols, axis=1)
        dpd_ref[:, w:2 * w] = (_ln_bwd(dvln * lw, vh, rs) * _dgelu(pv)).astype(BF16)
        vec_ref[...] += jnp.concatenate([jnp.concatenate(dscale, axis=1), _colsum(dvln * vh), _colsum(dvln),
                                         jnp.zeros((5, w), F32)], axis=0)

        @pl.when(pl.program_id(0) == t // tm - 1)
        def _():
            for g in range(GROUPS):
                dsgb_ref[g] = jnp.broadcast_to(jnp.sum(dsgb_ref[g], axis=1, keepdims=True), (GC, GC))

    col = lambda j: pl.BlockSpec((tm, w), lambda i: (i, j))
    mat = _full((GROUPS, GC, GC))
    return _pcall(body, name=name, grid=(t // tm,),
                  in_specs=[col(0), col(1), col(1), col(2), col(0), _full(pw.shape), _full(pscale.shape),
                            _full(lnw.shape), _full(lnb.shape), _full(sgw.shape), _full(sgb.shape)],
                  out_specs=[col(0), pl.BlockSpec((tm, 2 * w), lambda i: (i, 0)), _full((8, w)), mat, mat, mat],
                  out_shape=[_sds((t, w)), _sds((t, 2 * w), BF16), _sds((8, w)),
                             _sds((GROUPS, GC, GC)), _sds((GROUPS, GC, GC)), _sds((GROUPS, GC, GC))],
                  compiler_params=_cp(("arbitrary",)))(dmix, dmix, p, p, m, pw, pscale, lnw, lnb, sgw, sgb)


def _odd_dp(name, dm, dpd, nct, lc):
    t, w = dm.shape
    tm, halo = ROW_TILE, POOL_HALO
    nt = t // tm

    def body(d_ref, dp_ref, dn_ref, dpd_ref, o_ref, win_ref):
        i = pl.program_id(0)
        vp, vn = _halo_valid(i, nct, nt)
        dcur = d_ref[...]
        _fill_window(win_ref, dp_ref[...], dcur, dn_ref[...], vp, vn, halo, tm)
        for g, wd in enumerate(POOL_WINDOWS):
            sl = slice(g * GC, (g + 1) * GC)
            left = wd // 2
            right = wd - 1 - left
            win_ref[:, sl] = win_ref[:, sl] / _pool_count(i, nct, lc, t, tm, tm + 2 * halo, -halo, left, right)
            s = None
            for o in range(-right, left + 1):
                term = win_ref[halo + o:halo + o + tm, sl]
                s = term if s is None else s + term
            o_ref[:, sl] = (s - dcur[:, sl]).astype(BF16)
        o_ref[:, w:3 * w] = dpd_ref[...]

    row = pl.BlockSpec((tm, w), lambda i: (i, 0))
    pp, pn = _halo_specs(tm, halo, t, w, 0)
    return _pcall(body, name=name, grid=(nt,),
                  in_specs=[row, pp, pn, pl.BlockSpec((tm, 2 * w), lambda i: (i, 0))],
                  out_specs=pl.BlockSpec((tm, 3 * w), lambda i: (i, 0)), out_shape=_sds((t, 3 * w), BF16),
                  scratch_shapes=[pltpu.VMEM((tm + 2 * halo, w), F32)],
                  compiler_params=_cp(("parallel",)))(dm, dm, dm, dpd)


def _place():
    x, y, c = lax.axis_index("x"), lax.axis_index("y"), lax.axis_index("c")
    chips = [(1 - x, y), (x, 1 - y), (1 - x, 1 - y)]
    return x, y, c, chips


def _chip_index(cx, cy):
    return 2 * cx + cy


def _all_gather8(name, blk, after=()):
    m_per, n = blk.shape
    na = len(after)

    def body(x_ref, *rest):
        out_ref, send_sems, recv_sems, local_sem = rest[na:]
        x, y, c, chips = _place()
        me, sibling = (x, y, c), (x, y, 1 - c)

        def rows(px, py, pc):
            return out_ref.at[pl.ds((4 * px + 2 * py + pc) * m_per, m_per), :]

        def copy(k, block, to, src=None):
            return pltpu.make_async_remote_copy(
                src_ref=rows(*block) if src is None else src, dst_ref=rows(*block),
                send_sem=send_sems.at[k], recv_sem=recv_sems.at[k], device_id=to, device_id_type=MESH)

        mine = pltpu.make_async_copy(x_ref, rows(*me), local_sem)
        mine.start()
        first = [copy(0, me, sibling, src=x_ref)]
        first += [copy(1 + j, me, (*chip, c), src=x_ref) for j, chip in enumerate(chips)]
        for cp in first:
            cp.start()
        passed = [copy(4 + j, (*chip, c), sibling) for j, chip in enumerate(chips)]
        for j, chip in enumerate(chips):
            copy(1 + j, (*chip, c), me).wait_recv()
            passed[j].start()
        copy(0, sibling, me).wait_recv()
        for j, chip in enumerate(chips):
            copy(4 + j, (*chip, 1 - c), me).wait_recv()
        for cp in first + passed:
            cp.wait_send()
        mine.wait()

    return _pcall(body, name=name, out_shape=_sds((8 * m_per, n), blk.dtype),
                  in_specs=[pl.BlockSpec(memory_space=pltpu.VMEM)] + [pl.BlockSpec(memory_space=pl.ANY)] * na,
                  out_specs=pl.BlockSpec(memory_space=pltpu.VMEM),
                  scratch_shapes=[pltpu.SemaphoreType.DMA((7,)), pltpu.SemaphoreType.DMA((7,)), pltpu.SemaphoreType.DMA],
                  compiler_params=_cp())(blk, *after)


ANY = pl.BlockSpec(memory_space=pl.ANY)


def _half(which, rows):
    return pl.ds(pl.multiple_of(which * rows, 16), rows)


def _gather_weights(name, ws, after=()):
    nw = len(ws)
    na = len(after)
    ns = 7

    def body(*refs):
        w_refs, o_refs = refs[:nw], refs[nw + na:2 * nw + na]
        send_sems, recv_sems = refs[2 * nw + na:]
        x, y, c, chips = _place()
        me_chip = _chip_index(x, y)
        sibling = (x, y, 1 - c)

        def rcopy(t, k, src, dst, to):
            return pltpu.make_async_remote_copy(src_ref=src, dst_ref=dst, send_sem=send_sems.at[t * ns + k],
                                                recv_sem=recv_sems.at[t * ns + k], device_id=to, device_id_type=MESH)

        sends = []
        for t in range(nw):
            lh = w_refs[t].shape[0] // 2
            for k, chip in enumerate(chips):
                sends.append(rcopy(t, k, w_refs[t].at[_half(c, lh)], o_refs[t].at[me_chip, _half(c, lh)], (*chip, c)))
                sends[-1].start()
            sends.append(rcopy(t, 6, w_refs[t], o_refs[t].at[me_chip], sibling))
            sends[-1].start()
        for t in range(nw):
            lh = w_refs[t].shape[0] // 2
            for k, chip in enumerate(chips):
                part = o_refs[t].at[_chip_index(*chip), _half(c, lh)]
                rcopy(t, k, part, part, (*chip, c)).wait_recv()
                sends.append(rcopy(t, 3 + k, part, part, sibling))
                sends[-1].start()
        for t in range(nw):
            lh = w_refs[t].shape[0] // 2
            own = o_refs[t].at[me_chip]
            rcopy(t, 6, own, own, sibling).wait_recv()
            for k, chip in enumerate(chips):
                part = o_refs[t].at[_chip_index(*chip), _half(1 - c, lh)]
                rcopy(t, 3 + k, part, part, sibling).wait_recv()
        for cp in sends:
            cp.wait_send()

    return _pcall(body, name=name, out_shape=[_sds((4,) + w.shape, w.dtype) for w in ws],
                  in_specs=[ANY] * (nw + na), out_specs=[ANY] * nw,
                  scratch_shapes=[pltpu.SemaphoreType.DMA((ns * nw,)), pltpu.SemaphoreType.DMA((ns * nw,))],
                  compiler_params=_cp())(*ws, *after)


def _rs_share(name, ss):
    ng = len(ss)

    def body(*refs):
        o_refs = refs[ng:2 * ng]
        send_sems, recv_sems = refs[2 * ng:]
        x, y, c, _ = _place()
        cps = []
        for t in range(ng):
            lh = o_refs[t].shape[1] // 2
            mine = o_refs[t].at[:, _half(c, lh)]
            cp = pltpu.make_async_remote_copy(
                src_ref=mine, dst_ref=mine, send_sem=send_sems.at[t], recv_sem=recv_sems.at[t],
                device_id=(x, y, 1 - c), device_id_type=MESH)
            cp.start()
            cps.append(cp)
        for t in range(ng):
            lh = o_refs[t].shape[1] // 2
            cps[t].wait_send()
            theirs = o_refs[t].at[:, _half(1 - c, lh)]
            pltpu.make_async_remote_copy(
                src_ref=theirs, dst_ref=theirs, send_sem=send_sems.at[t], recv_sem=recv_sems.at[t],
                device_id=(x, y, 1 - c), device_id_type=MESH).wait_recv()

    return _pcall(body, name=name, out_shape=[_sds(s.shape, s.dtype) for s in ss],
                  in_specs=[ANY] * ng, out_specs=[ANY] * ng, input_output_aliases={t: t for t in range(ng)},
                  scratch_shapes=[pltpu.SemaphoreType.DMA((ng,)), pltpu.SemaphoreType.DMA((ng,))],
                  compiler_params=_cp())(*ss)


HBM = pl.BlockSpec(memory_space=pltpu.HBM)
SEMS = pl.BlockSpec(memory_space=pltpu.SEMAPHORE)
EFFECT = pltpu.SideEffectType.DATAFLOW_SIDE_EFFECTING
TOKEN = (8, 128)


def _in_hbm(a):
    return pltpu.with_memory_space_constraint(a, pltpu.HBM)


def _split_start(name, srcs, lands, copies, after):
    ns, nl, na = len(srcs), len(lands), len(after)
    ncopies = len(copies([s for s in srcs], [l for l in lands], probe=True))

    def body(*refs):
        src_refs, land_refs = refs[:ns], refs[ns:ns + nl]
        send_sems, recv_sems = refs[ns + nl + na], refs[ns + nl + na + 1]
        token = refs[-1]
        for k, (src, dst, to) in enumerate(copies(src_refs, land_refs)):
            pltpu.make_async_remote_copy(src_ref=src, dst_ref=dst, send_sem=send_sems.at[k], recv_sem=recv_sems.at[k],
                                         device_id=to, device_id_type=MESH).start()
        token[...] = jnp.zeros_like(token)

    thru = [pltpu.HBM(a.shape, a.dtype) for a in list(srcs) + list(lands)]
    outs = _pcall(body, name=name,
                  out_shape=(pltpu.SemaphoreType.DMA((ncopies,)), pltpu.SemaphoreType.DMA((ncopies,)), *thru, _sds(TOKEN)),
                  in_specs=[HBM] * (ns + nl) + [ANY] * na,
                  out_specs=(SEMS, SEMS, *([HBM] * (ns + nl)), pl.BlockSpec(memory_space=pltpu.VMEM)),
                  input_output_aliases={t: 2 + t for t in range(ns + nl)},
                  compiler_params=pltpu.CompilerParams(has_side_effects=EFFECT))(
        *[_in_hbm(a) for a in list(srcs) + list(lands)], *after)
    return outs[0], outs[1], list(outs[2:2 + ns]), list(outs[2 + ns:2 + ns + nl]), outs[-1]


def _split_wait(name, started, copies, after, first=0):
    send_sems, recv_sems, srcs, lands, _ = started
    ns, nl, na = len(srcs), len(lands), len(after)

    def body(*refs):
        src_refs, land_refs = refs[:ns], refs[ns:ns + nl]
        send_sems_ref, recv_sems_ref = refs[ns + nl], refs[ns + nl + 1]
        for k, (src, dst, to) in enumerate(copies(src_refs, land_refs)):
            cp = pltpu.make_async_remote_copy(src_ref=src, dst_ref=dst, send_sem=send_sems_ref.at[first + k],
                                              recv_sem=recv_sems_ref.at[first + k], device_id=to, device_id_type=MESH)
            cp.wait_send()
            cp.wait_recv()

    thru = [pltpu.HBM(a.shape, a.dtype) for a in list(srcs) + list(lands)]
    outs = _pcall(body, name=name, out_shape=tuple(thru),
                  in_specs=[HBM] * (ns + nl) + [SEMS, SEMS] + [ANY] * na, out_specs=tuple([HBM] * (ns + nl)),
                  input_output_aliases={t: t for t in range(ns + nl)},
                  compiler_params=pltpu.CompilerParams(has_side_effects=EFFECT))(
        *srcs, *lands, send_sems, recv_sems, *after)
    return list(outs[:ns]), list(outs[ns:])


def _pair_copies(n):
    def copies(src_refs, land_refs, probe=False):
        if probe:
            return [None] * n
        x, y, c, _ = _place()
        return [(src_refs[t].at[:, _half(1 - c, src_refs[t].shape[1] // 2)], land_refs[t], (x, y, 1 - c))
                for t in range(n)]
    return copies


def _gather_copies(n):
    def copies(src_refs, land_refs, probe=False):
        if probe:
            return [None] * (4 * n)
        x, y, c, chips = _place()
        me_chip = _chip_index(x, y)
        out = []
        for t in range(n):
            for to in [(*chip, c) for chip in chips] + [(x, y, 1 - c)]:
                out.append((src_refs[t], land_refs[t].at[me_chip], to))
        return out
    return copies


def _scatter_copies(n):
    def copies(src_refs, land_refs, probe=False):
        if probe:
            return [None] * (3 * n)
        x, y, c, chips = _place()
        out = []
        for t in range(n):
            for k, chip in enumerate(chips):
                out.append((src_refs[t].at[_chip_index(*chip)], land_refs[t].at[k], (*chip, c)))
        return out
    return copies


def _row_block(r, cn):
    if r % 8:
        return r
    best = 8
    for d in range(8, r + 1, 8):
        if r % d == 0 and d * cn * 4 <= (2 << 20):
            best = d
    return best


def _add_half(name, gl, al, idx):
    n = len(gl)
    j, rh, cn = al[0].shape
    tr = _row_block(rh, cn)
    nb = rh // tr

    def body(i_ref, *refs):
        for t in range(n):
            refs[2 * n + t][...] = (refs[t][...] + refs[n + t][...]).astype(BF16)

    blk = (None, tr, cn)
    gspec = pl.BlockSpec(blk, lambda jj, i, i_ref: (jj, i_ref[0] * nb + i, 0))
    aspec = pl.BlockSpec(blk, lambda jj, i, i_ref: (jj, i, 0))
    gs = pltpu.PrefetchScalarGridSpec(num_scalar_prefetch=1, grid=(j, nb), in_specs=[gspec] * n + [aspec] * n,
                                      out_specs=[aspec] * n)
    return _pcall(body, name=name, grid_spec=gs, out_shape=[_sds(al[0].shape, BF16)] * n,
                  compiler_params=_cp(("parallel", "parallel")))(idx, *gl, *al)


def _sum_final(name, gl, al, bl, idx, bufs, lyr, nlyr):
    n = len(gl)
    _, r, cn = gl[0].shape
    rh = r // 2
    tr = _row_block(rh, cn)
    nb = rh // tr
    has = bufs[0] is not None

    def body(i_ref, *refs):
        outs = refs[len(refs) - n:]
        for t in range(n):
            own = refs[t][...] + refs[n + t][...]
            b_ref = refs[2 * n + t]
            outs[t][...] = (own + b_ref[0].astype(F32)) + (b_ref[1].astype(F32) + b_ref[2].astype(F32))

    blk = (None, tr, cn)
    in_specs = ([pl.BlockSpec(blk, lambda i, i_ref: (i_ref[1], i_ref[0] * nb + i, 0))] * n
                + [pl.BlockSpec(blk, lambda i, i_ref: (i_ref[1], i, 0))] * n
                + [pl.BlockSpec((3, tr, cn), lambda i, i_ref: (0, i, 0))] * n)
    args = [idx, *gl, *al, *bl]
    kw = {}
    if has:
        in_specs += [ANY] * n
        args += list(bufs)
        kw["input_output_aliases"] = {1 + 3 * n + t: t for t in range(n)}
    gs = pltpu.PrefetchScalarGridSpec(
        num_scalar_prefetch=1, grid=(nb,), in_specs=in_specs,
        out_specs=[pl.BlockSpec(blk, lambda i, i_ref: (lyr, i_ref[0] * nb + i, 0))] * n)
    return _pcall(body, name=name, grid_spec=gs, out_shape=[_sds((nlyr, r, cn))] * n,
                  compiler_params=_cp(("parallel",)), **kw)(*args)


def _sum8(name, g):
    _, r, n = g.shape
    tr = 8

    def body(g_ref, o_ref):
        o_ref[...] = ((g_ref[0] + g_ref[1]) + (g_ref[2] + g_ref[3])) + ((g_ref[4] + g_ref[5]) + (g_ref[6] + g_ref[7]))

    return _pcall(body, name=name, grid=(r // tr,), in_specs=[pl.BlockSpec((8, tr, n), lambda i: (0, i, 0))],
                  out_specs=pl.BlockSpec((tr, n), lambda i: (i, 0)), out_shape=_sds((r, n)),
                  compiler_params=_cp(("parallel",)))(g)


def _ada_mod(name, c16, ada_w, bias):
    nl, dm, n = ada_w.shape

    def body(c_ref, w_ref, b_ref, o_ref):
        o_ref[...] = _dot(_silu(c_ref[...]), w_ref[...], NN) + b_ref[...]

    return _pcall(body, name=name, grid=(nl,),
                  in_specs=[_full(c16.shape), pl.BlockSpec((None, dm, n), lambda i: (i, 0, 0)),
                            pl.BlockSpec((None, 1, n), lambda i: (i, 0, 0))],
                  out_specs=pl.BlockSpec((None, 16, n), lambda i: (i, 0, 0)), out_shape=_sds((nl, 16, n)),
                  compiler_params=_cp(("parallel",)))(c16, ada_w, bias)


def _ada_bwd(name, c16, dmod, ada_w):
    nl, dm, n = ada_w.shape

    def body(c_ref, d_ref, w_ref, gw_ref, dc_ref):
        @pl.when(pl.program_id(0) == 0)
        def _():
            dc_ref[...] = jnp.zeros_like(dc_ref)

        dv = d_ref[...]
        gw_ref[...] = _dot(_silu(c_ref[...]), dv, TN)
        dc_ref[...] += _dot(dv, w_ref[...], NT)

    return _pcall(body, name=name, grid=(nl,),
                  in_specs=[_full(c16.shape), pl.BlockSpec((None, 16, n), lambda i: (i, 0, 0)),
                            pl.BlockSpec((None, dm, n), lambda i: (i, 0, 0))],
                  out_specs=[pl.BlockSpec((None, dm, n), lambda i: (i, 0, 0)), _full((16, dm))],
                  out_shape=[_sds((nl, dm, n)), _sds((16, dm))],
                  compiler_params=_cp(("arbitrary",)))(c16, dmod, ada_w)


def _rowsum16(name, dmod):
    nl, _, n = dmod.shape

    def body(d_ref, o_ref):
        o_ref[...] = _colsum(d_ref[...])

    return _pcall(body, name=name, grid=(nl,), in_specs=[pl.BlockSpec((None, 16, n), lambda i: (i, 0, 0))],
                  out_specs=pl.BlockSpec((None, 1, n), lambda i: (i, 0, 0)), out_shape=_sds((nl, 1, n)),
                  compiler_params=_cp(("parallel",)))(dmod)


def _cctx_grad(name, parts, c_ctx):
    def body(p_ref, c_ref, o_ref):
        tot = (p_ref[0:1, :] + p_ref[1:2, :]) + (p_ref[2:3, :] + p_ref[3:4, :])
        o_ref[...] = tot * _dsilu(c_ref[...])

    return _pcall(body, name=name, out_shape=_sds(c_ctx.shape), compiler_params=_cp())(parts, c_ctx)


def _adamw(name, w, g, m, v, with_grad=False):
    shape = w.shape
    cn = shape[-1]
    r = math.prod(shape[:-1]) if len(shape) > 1 else 1
    tr = _row_block(r, cn)
    c1 = 1.0 - ADAM_B1 ** ADAM_STEP
    c2 = 1.0 - ADAM_B2 ** ADAM_STEP
    nout = 4 if with_grad else 3

    def body(w_ref, g_ref, m_ref, v_ref, d_ref, mo_ref, vo_ref, *rest):
        gv = g_ref[...]
        mn = ADAM_B1 * m_ref[...] + (1.0 - ADAM_B1) * gv
        vn = ADAM_B2 * v_ref[...] + (1.0 - ADAM_B2) * (gv * gv)
        d_ref[...] = -ADAM_LR * ((mn / c1) / (jnp.sqrt(vn / c2) + ADAM_EPS) + ADAM_WD * w_ref[...])
        mo_ref[...] = mn
        vo_ref[...] = vn
        if with_grad:
            rest[0][...] = gv

    blk = pl.BlockSpec((tr, cn), lambda i: (i, 0))
    o = _sds((r, cn))
    outs = _pcall(body, name=name, grid=(r // tr,), in_specs=[blk] * 4, out_specs=[blk] * nout, out_shape=[o] * nout,
                  compiler_params=_cp(("parallel",)))(*[a.reshape(r, cn) for a in (w, g, m, v)])
    return tuple(a.reshape(shape) for a in outs)


def _local_step(xs, target, modt, nw, fnw, get_w, get_ffn, put_g, ev, od, lc):
    t, dm = xs.shape
    nct = lc // ROW_TILE
    depth = nw.shape[0]
    cs, sn = _rope_tables(t, lc)
    saved = []
    x_in, x1p, fp = xs, None, None
    for i in range(depth):
        j, even = i // 2, i % 2 == 0
        tag = f"l{i}"
        w, deps = get_w(i, [fp] if i else [])
        if i == 0:
            _, h = _rnm(tag + "_norm1", x_in, None, None, 0, modt[0], 0, 1, nw[0, 0], nct, deps)
        else:
            x_in, h = _rnm(tag + "_norm1", x1p, fp, modt[i - 1], 5, modt[i], 0, 1, nw[i, 0], nct, deps)
        s = dict(x=x_in, h=h, w=w)
        if even:
            p = _mm_cols(tag + "_in", h, w["in"])
            q, k, v = _even_qkv(tag + "_qkv", p, cs, sn)
            of, ob, ss = _retention_fwd(tag + "_ret", q, k, v, ev["lgb"][j], lc)
            mix, yc = _even_mix(tag + "_mix", p, of, ob, ev["cw"][j], ev["lnw"][j], ev["lnb"][j], nct)
            y = _mm_full(tag + "_out", mix, w["out"], NN)
            s.update(p=p, q=q, k=k, v=v, of=of, ob=ob, ss=ss, yc=yc)
        else:
            p = _mm_cols(tag + "_in", h, w["in"])
            mix, m = _odd_mix(tag + "_mix", p, od["pw"][j], od["ps"][j], od["lnw"][j], od["lnb"][j],
                              od["sgw"][j], od["sgb"][j], nct, lc)
            y = _mm_full(tag + "_out", mix, w["out"], NN)
            s.update(p=p, m=m)
        x1, h2 = _rnm(tag + "_norm2", x_in, y, modt[i], 2, modt[i], 3, 4, nw[i, 1], nct)
        w.update(get_ffn(i, [y]))
        a, gt, up = _ffn_up(tag + "_ffn_up", h2, w["gate"], w["up"])
        f = _mm_full(tag + "_ffn_down", a, w["down"], NN)
        s.update(mix=mix, y=y, x1=x1, h2=h2, a=a, gt=gt, up=up, f=f)
        saved.append(s)
        x1p, fp = x1, f

    loss_blk, dx, df, fin_s = _fin("final", x1p, fp, modt[depth - 1], 5, fnw, target, nct)

    deps = []
    dmod = [[None] * 6 for _ in range(depth)]
    dnw = [[None, None] for _ in range(depth)]
    zero2 = jnp.zeros((2, dm), F32)
    dmod[depth - 1][5] = jnp.stack([zero2[0], fin_s[0]])
    small = dict(dfnw=fin_s[1], ev=[], od=[])
    for i in reversed(range(depth)):
        j, even = i // 2, i % 2 == 0
        tag = f"l{i}b"
        s = saved[i]
        w = s["w"]
        fh = w["down"].shape[0] // 2
        g = {}
        dgt, dup = _ffn_down_bwd(tag + "_ffn_down", df, w["down"], s["gt"], s["up"])
        g["down"] = _wgrad_rows(tag + "_gdown", s["a"], fh, df)
        g["gate"] = _wgrad_rows(tag + "_ggate", dgt, fh, s["h2"])
        g["up"] = _wgrad_rows(tag + "_gup", dup, fh, s["h2"])
        deps = put_g(i, "f", g)
        dh2 = _ffn_in_bwd(tag + "_ffn_in", dgt, dup, w["gate"], w["up"])
        dx1, dy, s2 = _bnm(tag + "_norm2", s["x1"], dh2, dx, s["y"], modt[i], 3, 4, modt[i], 2, nw[i, 1], nct, deps)
        dmod[i][3], dmod[i][4], dmod[i][2] = s2[:, 0], s2[:, 1], s2[:, 2]
        dnw[i][1] = s2[1, 3]
        dmix = _mm_full(tag + "_out", dy, w["out"], NT)
        g["out"] = _wgrad_rows(tag + "_gout", s["mix"], w["out"].shape[0] // 2, dy)
        if even:
            do, dg, dyc, lns = _even_mix_bwd1(tag + "_mix1", dmix, s["p"], s["of"], s["ob"], s["yc"],
                                              ev["lnw"][j], ev["lnb"][j])
            da, dgb, dcw = _even_conv_bwd(tag + "_conv", dyc, s["p"], ev["cw"][j], nct)
            dqf, dkf, dvf, dqb, dkb, dvb, dl = _retention_bwd(tag + "_ret", s["q"], s["k"], s["v"], do, s["ss"],
                                                              ev["lgb"][j], lc)
            dp = _even_dp(tag + "_dp", (dqf, dqb), (dkf, dkb), (dvf, dvb), dg, da, dgb, cs, sn)
            small["ev"].append(dict(lnw=lns[0], lnb=lns[1], cw=dcw, dl=dl[:, 0]))
        else:
            dm_, dpd, vec, dpw, dsgw, dsgb = _odd_mix_bwd1(tag + "_mix1", dmix, s["p"], s["m"], od["pw"][j], od["ps"][j],
                                                           od["lnw"][j], od["lnb"][j], od["sgw"][j], od["sgb"][j])
            dp = _odd_dp(tag + "_dp", dm_, dpd, nct, lc)
            small["od"].append(dict(ps=vec[0], lnw=vec[1], lnb=vec[2], pw=dpw, sgw=dsgw, sgb=dsgb[:, :, 0]))
        dh = _mm_cols_bwd(tag + "_in", dp, w["in"])
        g["in"] = _wgrad_cols(tag + "_gin", s["h"], dp, w["in"].shape[0])
        deps = put_g(i, "m", g)
        if i > 0:
            dx, df, s1 = _bnm(tag + "_norm1", s["x"], dh, dx1, saved[i - 1]["f"], modt[i], 0, 1, modt[i - 1], 5,
                              nw[i, 0], nct, deps)
            dmod[i - 1][5] = s1[:, 2]
        else:
            dx, _, s1 = _bnm(tag + "_norm1", s["x"], dh, dx1, None, modt[0], 0, 1, None, 0, nw[0, 0], nct, deps)
        dmod[i][0], dmod[i][1] = s1[:, 0], s1[:, 1]
        dnw[i][0] = s1[1, 3]
    small["ev"].reverse()
    small["od"].reverse()
    dmod_t = jnp.stack([jnp.concatenate([jnp.stack(rows, axis=1), jnp.zeros((2, 2, dm), F32)], axis=1) for rows in dmod])
    small["dmod"] = dmod_t
    small["dnw"] = jnp.stack([jnp.stack(r) for r in dnw])
    return loss_blk, dx, small


WEIGHTS = ["c_ctx", "ada_w", "ada_b", "norm_w", "even_w_in", "even_w_out", "ret_decay_logit", "conv_dw_w",
           "conv_ln_w", "conv_ln_b", "odd_w_in", "odd_w_out", "pool_w", "pool_scale", "sg_ln_w", "sg_ln_b",
           "sg_w", "sg_b", "ffn_w_gate", "ffn_w_up", "ffn_w_down", "final_norm_w"]
BIG = dict(even_in="even_w_in", even_out="even_w_out", odd_in="odd_w_in", odd_out="odd_w_out",
           gate="ffn_w_gate", up="ffn_w_up", down="ffn_w_down")


def _rows(a, width=1024):
    flat = a.reshape(-1)
    n = flat.shape[0]
    per = 8 * width
    tot = -(-n // per) * per
    return jnp.pad(flat, (0, tot - n)).reshape(tot // width, width)


def _unshard(parts, lead):
    nl = len(lead)
    perm = tuple(range(1, nl + 1)) + (0, nl + 1)
    return parts.transpose(perm).reshape(tuple(lead) + (4 * parts.shape[-1],))


def _my_cols(a, chip, n):
    start = (0,) * (a.ndim - 1) + (chip * n,)
    return lax.dynamic_slice(a, start, a.shape[:-1] + (n,))


def kernel(x, c, ctx, c_ctx, ada_w, ada_b, norm_w, even_w_in, even_w_out, ret_decay_logit, conv_dw_w, conv_ln_w, conv_ln_b, odd_w_in, odd_w_out, pool_w, pool_scale, sg_ln_w, sg_ln_b, sg_w, sg_b, ffn_w_gate, ffn_w_up, ffn_w_down, final_norm_w, loss_target, m_c_ctx, m_ada_w, m_ada_b, m_norm_w, m_even_w_in, m_even_w_out, m_ret_decay_logit, m_conv_dw_w, m_conv_ln_w, m_conv_ln_b, m_odd_w_in, m_odd_w_out, m_pool_w, m_pool_scale, m_sg_ln_w, m_sg_ln_b, m_sg_w, m_sg_b, m_ffn_w_gate, m_ffn_w_up, m_ffn_w_down, m_final_norm_w, v_c_ctx, v_ada_w, v_ada_b, v_norm_w, v_even_w_in, v_even_w_out, v_ret_decay_logit, v_conv_dw_w, v_conv_ln_w, v_conv_ln_b, v_odd_w_in, v_odd_w_out, v_pool_w, v_pool_scale, v_sg_ln_w, v_sg_ln_b, v_sg_w, v_sg_b, v_ffn_w_gate, v_ffn_w_up, v_ffn_w_down, v_final_norm_w):
    wv = dict(c_ctx=c_ctx, ada_w=ada_w, ada_b=ada_b, norm_w=norm_w, even_w_in=even_w_in, even_w_out=even_w_out,
              ret_decay_logit=ret_decay_logit, conv_dw_w=conv_dw_w, conv_ln_w=conv_ln_w, conv_ln_b=conv_ln_b,
              odd_w_in=odd_w_in, odd_w_out=odd_w_out, pool_w=pool_w, pool_scale=pool_scale, sg_ln_w=sg_ln_w,
              sg_ln_b=sg_ln_b, sg_w=sg_w, sg_b=sg_b, ffn_w_gate=ffn_w_gate, ffn_w_up=ffn_w_up,
              ffn_w_down=ffn_w_down, final_norm_w=final_norm_w)
    mv = dict(zip(WEIGHTS, (m_c_ctx, m_ada_w, m_ada_b, m_norm_w, m_even_w_in, m_even_w_out, m_ret_decay_logit,
                            m_conv_dw_w, m_conv_ln_w, m_conv_ln_b, m_odd_w_in, m_odd_w_out, m_pool_w, m_pool_scale,
                            m_sg_ln_w, m_sg_ln_b, m_sg_w, m_sg_b, m_ffn_w_gate, m_ffn_w_up, m_ffn_w_down,
                            m_final_norm_w)))
    vv = dict(zip(WEIGHTS, (v_c_ctx, v_ada_w, v_ada_b, v_norm_w, v_even_w_in, v_even_w_out, v_ret_decay_logit,
                            v_conv_dw_w, v_conv_ln_w, v_conv_ln_b, v_odd_w_in, v_odd_w_out, v_pool_w, v_pool_scale,
                            v_sg_ln_w, v_sg_ln_b, v_sg_w, v_sg_b, v_ffn_w_gate, v_ffn_w_up, v_ffn_w_down,
                            v_final_norm_w)))
    xi, yi, ci = lax.axis_index("x"), lax.axis_index("y"), lax.axis_index("c")
    chip = 2 * xi + yi
    dev = 4 * xi + 2 * yi + ci
    dm = x.shape[-1]
    lc = ctx.shape[1]
    depth = ada_w.shape[0]
    n_ada = ada_w.shape[-1]

    cw_pad = jnp.pad(conv_dw_w, ((0, 0), (0, 1), (0, 0)))
    vec3 = jnp.stack([pool_scale, sg_ln_w, sg_ln_b])
    pack1 = jnp.concatenate([_rows(c), _rows(norm_w), _rows(cw_pad), _rows(vec3)], axis=0)
    g1 = _all_gather8("gather_small", pack1).reshape(8, 32, dm)
    c_all = g1[:, 0]
    per_chip = g1[0::2]
    norm_full = _unshard(per_chip[:, 8:10].reshape(4, depth, 2, dm // 4), (depth, 2))
    cw_full = _unshard(per_chip[:, 16:24].reshape(4, 2, CONV_K + 1, 128), (2, CONV_K + 1))
    vec_full = _unshard(per_chip[:, 24, :768].reshape(4, 3, 2, 128), (3, 2))

    c16 = jnp.concatenate([c_all, c_ctx[None, :], jnp.zeros((7, dm), F32)], axis=0)
    mod_sh = _ada_mod("ada_mod", c16, ada_w, _my_cols(ada_b, chip, n_ada)[:, None, :])
    g2 = _all_gather8("gather_mod", mod_sh.reshape(depth * 16, n_ada)).reshape(8, depth, 16, n_ada)
    mod_full = _unshard(g2[0::2], (depth, 16))
    mod_x = lax.dynamic_index_in_dim(mod_full, dev, axis=1, keepdims=False).reshape(depth, 6, dm)
    mod_c = mod_full[:, 8].reshape(depth, 6, dm)
    modt = jnp.pad(jnp.stack([mod_c, mod_x], axis=1), ((0, 0), (0, 0), (0, 2), (0, 0)))

    names = list(BIG)
    tr_names = ("gate", "up")
    shard = {k: (jnp.swapaxes(wv[BIG[k]], 1, 2) if k in tr_names else wv[BIG[k]]).astype(BF16) for k in names}
    roles = ("in", "out", "gate", "up", "down")

    def layer_keys(i):
        mixer = ("even_in", "even_out") if i % 2 == 0 else ("odd_in", "odd_out")
        return [(k, i // 2) for k in mixer] + [(k, i) for k in ("gate", "up", "down")]

    def as_used(got):
        return {r: (g if r == "in" else g.reshape(4 * g.shape[1], g.shape[2])) for r, g in zip(roles, got)}

    started = {}

    def get_w(i, after):
        if i > 0:
            part, first = started[i, "m"]
            got = _split_wait(f"gather_wait{i}m", part, _gather_copies(2), after, first)[1]
            return as_used(got), []
        got = _gather_weights("gather_w0", [shard[k][l] for k, l in layer_keys(0)[:2]], [modt])
        parts = [(li, p, layer_keys(li)[:2] if p == "m" else layer_keys(li)[2:])
                 for li in range(depth) for p in "mf" if (li, p) != (0, "m")]
        srcs = [shard[k][l] for _, _, keys in parts for k, l in keys]
        lands = [lax.empty((4,) + s.shape, s.dtype) for s in srcs]
        send_sems, recv_sems, srcs, lands, token = _split_start("gather_start", srcs, lands, _gather_copies(len(srcs)), [got[0]])
        t0 = 0
        for li, p, keys in parts:
            t1 = t0 + len(keys)
            started[li, p] = ((send_sems, recv_sems, srcs[t0:t1], lands[t0:t1], token), 4 * t0)
            t0 = t1
        return as_used(got), [token]

    def get_ffn(i, after):
        part, first = started[i, "f"]
        got = _split_wait(f"gather_wait{i}f", part, _gather_copies(3), after, first)[1]
        return {r: g.reshape(4 * g.shape[1], g.shape[2]) for r, g in zip(roles[2:], got)}

    idx = jnp.stack([ci, chip]).astype(jnp.int32)
    pairs, pending, stages = {}, {}, []

    def stage_keys(stage):
        i, part = stage
        return layer_keys(i)[2:] if part == "f" else layer_keys(i)[:2]

    def finish_pair(stage, after, glist=()):
        tag = f"{stage[0]}{stage[1]}"
        n, m = len(stage_keys(stage)), len(glist)
        part, first = pairs[stage]
        g_prev, from_sib = _split_wait(f"pair_wait{tag}", part, _pair_copies(n), after, first)
        if stage[1] == "f":
            pair = _add_half(f"rs_add{tag}", g_prev, from_sib, idx)
        else:
            pair = [_add_half(f"rs_add{tag}_{t}", [gl], [a], idx)[0] for t, (gl, a) in enumerate(zip(g_prev, from_sib))]
        lands = ([lax.empty((3,) + p.shape[1:], p.dtype) for p in pair]
                 + [lax.empty((4, gl.shape[1] // 2, gl.shape[2]), gl.dtype) for gl in glist])

        def copies(src_refs, land_refs, probe=False):
            return (_scatter_copies(n)(src_refs[:n], land_refs[:n], probe=probe)
                    + (_pair_copies(m)(src_refs[n:], land_refs[n:], probe=probe) if m else []))

        send_sems, recv_sems, srcs, lands, token = _split_start(f"rs_start{tag}", list(pair) + list(glist), lands, copies, [])
        pending[stage] = (g_prev, from_sib, (send_sems, recv_sems, srcs[:n], lands[:n], token))
        return (send_sems, recv_sems, srcs[n:], lands[n:], token), 3 * n

    def put_g(i, part, g):
        stage = (i, part)
        glist = [g[r].reshape(4, -1, g[r].shape[-1]) for r in (roles[2:] if part == "f" else roles[:2])]
        if stages:
            pairs[stage] = finish_pair(stages[-1], [glist[0]], glist)
        else:
            lands = [lax.empty((4, gl.shape[1] // 2, gl.shape[2]), gl.dtype) for gl in glist]
            pairs[stage] = (_split_start(f"pair_start{i}{part}", glist, lands, _pair_copies(len(glist)), []), 0)
        stages.append(stage)
        return [pairs[stage][0][4]]

    ev = dict(lgb=jnp.broadcast_to(ret_decay_logit.reshape(-1, 2 * HEADS)[:, :, None], (ret_decay_logit.shape[0], 2 * HEADS, HEAD_DIM)),
              cw=cw_full, lnw=conv_ln_w[:, None, :], lnb=conv_ln_b[:, None, :])
    od = dict(pw=pool_w, ps=vec_full[0][:, None, :], lnw=vec_full[1][:, None, :], lnb=vec_full[2][:, None, :],
              sgw=sg_w, sgb=jnp.broadcast_to(sg_b[:, :, :, None], sg_b.shape + (GC,)))
    xs = jnp.concatenate([ctx[0], x[0]], axis=0)
    loss_blk, dxs, small = _local_step(xs, loss_target[0], modt, norm_full[:, :, None, :], final_norm_w[None, :],
                                       get_w, get_ffn, put_g, ev, od, lc)

    misc = jnp.stack([
        small["dfnw"], jnp.broadcast_to(loss_blk[0, 0], (dm,)),
        jnp.concatenate([e["lnw"] for e in small["ev"]]), jnp.concatenate([e["lnb"] for e in small["ev"]]),
        jnp.concatenate([o["ps"] for o in small["od"]]), jnp.concatenate([o["lnw"] for o in small["od"]]),
        jnp.concatenate([o["lnb"] for o in small["od"]]),
        jnp.pad(jnp.concatenate([e["dl"] for e in small["ev"]]), (0, dm - 4 * HEADS)),
        jnp.stack([o["sgb"] for o in small["od"]]).reshape(-1)])
    pack2 = jnp.concatenate([
        _rows(small["dmod"]), _rows(small["dnw"]), _rows(misc), _rows(jnp.stack([e["cw"] for e in small["ev"]])),
        _rows(jnp.stack([o["pw"] for o in small["od"]])), _rows(jnp.stack([o["sgw"] for o in small["od"]]))], axis=0)
    n2 = pack2.shape[0]
    g3 = _all_gather8("gather_grads", pack2)
    tot = _sum8("sum_grads", g3.reshape(8, n2, dm))
    r_mod = depth * 16
    o_nw, o_misc = r_mod, r_mod + 8
    o_cw = o_misc + 16
    o_pw = o_cw + 2 * (CONV_K + 1) // 2
    o_sgw = o_pw + 128
    dmod_sum = tot[:r_mod].reshape(depth, 2, 8, dm)
    dmod_dev = g3.reshape(8, n2, dm)[:, :r_mod].reshape(8, depth, 2, 8, dm)
    dm_x = dmod_dev[:, :, 1, :6].reshape(8, depth, 6 * dm).transpose(1, 0, 2)
    dm_c = dmod_sum[:, 0, :6].reshape(depth, 1, 6 * dm)
    dmod16 = jnp.concatenate([dm_x, dm_c, jnp.zeros((depth, 7, 6 * dm), F32)], axis=1)
    g_ada_b = _rowsum16("ada_b_grad", dmod16)[:, 0]
    g_ada_w, dc16 = _ada_bwd("ada_bwd", c16, _my_cols(dmod16, chip, n_ada), ada_w)
    g4 = _all_gather8("gather_cctx", dc16[8:16]).reshape(8, 8, dm)
    g_c_ctx = _cctx_grad("cctx_grad", g4[0::2, 0], c_ctx[None, :])[0]

    misc_t = tot[o_misc:o_misc + 16]
    half = lambda row: misc_t[row].reshape(2, dm // 2)
    grads = dict(
        c_ctx=g_c_ctx, ada_w=g_ada_w, ada_b=g_ada_b,
        norm_w=_my_cols(tot[o_nw:o_nw + 8].reshape(depth, 2, dm), chip, dm // 4),
        ret_decay_logit=misc_t[7, :4 * HEADS].reshape(ret_decay_logit.shape),
        conv_dw_w=_my_cols(tot[o_cw:o_cw + 2 * (CONV_K + 1) // 2].reshape(2, CONV_K + 1, dm // 2)[:, :CONV_K], chip, 128),
        conv_ln_w=half(2), conv_ln_b=half(3),
        pool_w=tot[o_pw:o_pw + 128].reshape(pool_w.shape),
        pool_scale=_my_cols(half(4), chip, 128), sg_ln_w=_my_cols(half(5), chip, 128), sg_ln_b=_my_cols(half(6), chip, 128),
        sg_w=tot[o_sgw:o_sgw + 128].reshape(sg_w.shape), sg_b=misc_t[8].reshape(sg_b.shape),
        final_norm_w=misc_t[0])
    loss = misc_t[1, 0]

    last_tokens = [finish_pair(stages[-1], [g_c_ctx])[0][4]]
    deltas, new_m, new_v = {}, {}, {}
    for n in WEIGHTS:
        if n not in BIG.values():
            deltas[n], new_m[n], new_v[n] = _adamw("adamw_" + n, wv[n], grads[n], mv[n], vv[n])
    reduced = {k: None for k in names}
    for stage in stages:
        glist, from_sib, st = pending[stage]
        last = stage == stages[-1]
        after = [deltas["ada_w"]] + [reduced[k] for k, _ in stage_keys(stages[-2])] if last else last_tokens
        slots = _split_wait(f"rs_wait{stage[0]}{stage[1]}", st, _scatter_copies(len(glist)), after)[1]
        keys = stage_keys(stage)
        groups = [range(len(keys))] if stage[1] == "f" else [[t] for t in range(len(keys))]
        for grp in groups:
            ks = [keys[t][0] for t in grp]
            lyr = keys[grp[0]][1]
            outs = _sum_final(f"rs_sum_{ks[0]}{lyr}", [glist[t] for t in grp], [from_sib[t] for t in grp],
                              [slots[t] for t in grp], idx, [reduced[k] for k in ks], lyr, shard[ks[0]].shape[0])
            for k, o in zip(ks, outs):
                reduced[k] = o
    shards = dict(zip(names, _rs_share("rs_share", [reduced[k] for k in names])))

    for k in names:
        n = BIG[k]
        tr = (lambda a: jnp.swapaxes(a, 1, 2)) if k in tr_names else (lambda a: a)
        outs = _adamw("adamw_" + n, tr(wv[n]), shards[k], tr(mv[n]), tr(vv[n]), with_grad=True)
        deltas[n], new_m[n], new_v[n], grads[n] = (tr(o) for o in outs)
    grad_x = dxs[None]
    return (loss, grad_x, *[grads[n] for n in WEIGHTS], *[deltas[n] for n in WEIGHTS],
            *[new_m[n] for n in WEIGHTS], *[new_v[n] for n in WEIGHTS])
```

```python
import functools
import math

import jax
import jax.numpy as jnp
from jax import lax
from jax.experimental import pallas as pl
from jax.experimental.pallas import tpu as pltpu

F32 = jnp.float32
BF16 = jnp.bfloat16
MESH = pl.DeviceIdType.MESH

EPS = 1e-6
GRID_W = 64
HEADS = 4
HEAD_DIM = 128
CHUNK = 128
CONV_K = 31
ROPE_BASE = 10000.0
ROPE_PAIRS = (16, 24, 24)
POOL_WINDOWS = (2, 4, 8, 16)
ADAM_LR, ADAM_B1, ADAM_B2, ADAM_EPS, ADAM_WD, ADAM_STEP = 0.001, 0.9, 0.999, 1e-08, 0.01, 10

ROW_TILE = 256
WIDE_ROWS = 576
CONV_HALO = 16
POOL_HALO = 8
VMEM_LIMIT = 56 * 1024 * 1024
WGRAD_ROWS = 2304


def _pcall(body, **kw):
    return pl.pallas_call(body, **kw)


def _cp(sem=None, vmem=VMEM_LIMIT):
    if sem is None:
        return pltpu.CompilerParams(vmem_limit_bytes=vmem)
    return pltpu.CompilerParams(dimension_semantics=sem, vmem_limit_bytes=vmem)


def _sds(shape, dtype=F32):
    return jax.ShapeDtypeStruct(tuple(shape), dtype)


def _full(shape):
    nd = len(shape)
    return pl.BlockSpec(tuple(shape), lambda *_: (0,) * nd)


def _sigmoid(x):
    return jax.nn.sigmoid(x)


def _silu(x):
    return x * _sigmoid(x)


def _dsilu(x):
    s = _sigmoid(x)
    return s * (1.0 + x * (1.0 - s))


def _colsum(a):
    return jnp.sum(a, axis=0, keepdims=True)


def _dot(a, b, dn):
    return lax.dot_general(a.astype(BF16), b.astype(BF16), dn, preferred_element_type=F32)


NN = (((1,), (0,)), ((), ()))
NT = (((1,), (1,)), ((), ()))
TN = (((0,), (0,)), ((), ()))


def _mm_tile(t, cap=1152):
    best = 16
    for d in range(16, min(t, cap) + 1, 16):
        if t % d == 0:
            best = d
    return best


def _mm(name, pairs, grid, out_shape, out_spec, dn):
    npairs = len(pairs)
    nk = grid[-1]
    kax = len(grid) - 1
    assert nk == 1 or out_shape.dtype == F32

    def body(*refs):
        ins = refs[:2 * npairs]
        o_ref = refs[2 * npairs]
        tot = None
        for p in range(npairs):
            d = _dot(ins[2 * p][...], ins[2 * p + 1][...], dn)
            tot = d if tot is None else tot + d
        if nk == 1:
            o_ref[...] = tot.astype(o_ref.dtype)
        else:
            k = pl.program_id(kax)

            @pl.when(k == 0)
            def _():
                o_ref[...] = tot

            @pl.when(k != 0)
            def _():
                o_ref[...] += tot

    args, in_specs = [], []
    for a, a_spec, b, b_spec in pairs:
        args += [a, b]
        in_specs += [a_spec, b_spec]
    sem = ("parallel",) * kax + ("arbitrary",)
    return _pcall(body, name=name, grid=grid, in_specs=in_specs, out_specs=out_spec, out_shape=out_shape,
                  compiler_params=_cp(sem))(*args)


def _mm_cols(name, a, w, out_dtype=F32):
    t, k = a.shape
    j, _, n = w.shape
    tm = _mm_tile(t)
    return _mm(name, [(a, pl.BlockSpec((tm, k), lambda i, jj, kk: (i, 0)),
                       w, pl.BlockSpec((None, k, n), lambda i, jj, kk: (jj, 0, 0)))],
               (t // tm, j, 1), _sds((t, j * n), out_dtype), pl.BlockSpec((tm, n), lambda i, jj, kk: (i, jj)), NN)


def _mm_cols_bwd(name, d, w):
    t = d.shape[0]
    j, k, n = w.shape
    tm = _mm_tile(t)
    pairs = [(d, pl.BlockSpec((tm, n), functools.partial(lambda jj, i, u, kk: (i, jj), jj)),
              w, pl.BlockSpec((None, k, n), functools.partial(lambda jj, i, u, kk: (jj, 0, 0), jj))) for jj in range(j)]
    return _mm(name, pairs, (t // tm, 1, 1), _sds((t, k), BF16), pl.BlockSpec((tm, k), lambda i, u, kk: (i, 0)), NT)


def _mm_full(name, a, w, dn, tm=None):
    t, k = a.shape
    n = w.shape[1] if dn is NN else w.shape[0]
    tm = tm or _mm_tile(t)
    return _mm(name, [(a, pl.BlockSpec((tm, k), lambda i, u, kk: (i, 0)), w, _full(w.shape))],
               (t // tm, 1, 1), _sds((t, n)), pl.BlockSpec((tm, n), lambda i, u, kk: (i, 0)), dn)


def _wgrad_cols(name, a, b, j):
    t, k = a.shape
    n = b.shape[1] // j
    tt = _mm_tile(t, 2 * WGRAD_ROWS)
    return _mm(name, [(a, pl.BlockSpec((tt, k), lambda jj, u, kk: (kk, 0)),
                       b, pl.BlockSpec((tt, n), lambda jj, u, kk: (kk, jj)))],
               (j, 1, t // tt), _sds((j, k, n)), pl.BlockSpec((None, k, n), lambda jj, u, kk: (jj, 0, 0)), TN)


def _wgrad_rows(name, a, blk, b):
    t, f = a.shape
    n = b.shape[1]
    tt = _mm_tile(t, WGRAD_ROWS)
    return _mm(name, [(a, pl.BlockSpec((tt, blk), lambda jj, u, kk: (kk, jj)),
                       b, pl.BlockSpec((tt, n), lambda jj, u, kk: (kk, 0)))],
               (f // blk, 1, t // tt), _sds((f, n)), pl.BlockSpec((blk, n), lambda jj, u, kk: (jj, 0)), TN)


def _ffn_tiles(t, f):
    return _mm_tile(t, 288), f


def _ffn_up(name, h, wgt, wut):
    t, k = h.shape
    f = wgt.shape[0]
    tm, tn = _ffn_tiles(t, f)

    def body(h_ref, wg_ref, wu_ref, a_ref, gt_ref, up_ref):
        hv = h_ref[...]
        gt = _dot(hv, wg_ref[...], NT)
        up = _dot(hv, wu_ref[...], NT)
        a_ref[...] = (_silu(gt) * up).astype(BF16)
        gt_ref[...] = gt.astype(BF16)
        up_ref[...] = up.astype(BF16)

    wspec = pl.BlockSpec((tn, k), lambda i, jj: (jj, 0))
    ospec = pl.BlockSpec((tm, tn), lambda i, jj: (i, jj))
    o = _sds((t, f), BF16)
    return _pcall(body, name=name, grid=(t // tm, f // tn),
                  in_specs=[pl.BlockSpec((tm, k), lambda i, jj: (i, 0)), wspec, wspec],
                  out_specs=[ospec, ospec, ospec], out_shape=[o, o, o],
                  compiler_params=_cp(("parallel", "parallel")))(h, wgt, wut)


def _ffn_down_bwd(name, df, wd, gt, up):
    t, dm = df.shape
    f = wd.shape[0]
    tm, tn = _ffn_tiles(t, f)

    def body(df_ref, wd_ref, gt_ref, up_ref, dgt_ref, dup_ref):
        da = _dot(df_ref[...], wd_ref[...], NT)
        g = gt_ref[...].astype(F32)
        u = up_ref[...].astype(F32)
        s = _sigmoid(g)
        dgt_ref[...] = (da * u * (s * (1.0 + g * (1.0 - s)))).astype(BF16)
        dup_ref[...] = (da * (g * s)).astype(BF16)

    aspec = pl.BlockSpec((tm, tn), lambda i, jj: (i, jj))
    o = _sds((t, f), BF16)
    return _pcall(body, name=name, grid=(t // tm, f // tn),
                  in_specs=[pl.BlockSpec((tm, dm), lambda i, jj: (i, 0)),
                            pl.BlockSpec((tn, dm), lambda i, jj: (jj, 0)), aspec, aspec],
                  out_specs=[aspec, aspec], out_shape=[o, o],
                  compiler_params=_cp(("parallel", "parallel")))(df, wd, gt, up)


def _ffn_in_bwd(name, dgt, dup, wgt, wut):
    t, f = dgt.shape
    k = wgt.shape[1]
    tm = _mm_tile(t, 576)
    aspec = pl.BlockSpec((tm, f), lambda i, u, kk: (i, 0))
    wspec = pl.BlockSpec((f, k), lambda i, u, kk: (0, 0))
    return _mm(name, [(dgt, aspec, wgt, wspec), (dup, aspec, wut, wspec)], (t // tm, 1, 1), _sds((t, k), BF16),
               pl.BlockSpec((tm, k), lambda i, u, kk: (i, 0)), NN)


def _modrow(ref, row, is_ctx):
    return jnp.where(is_ctx, ref[0, row:row + 1, :], ref[1, row:row + 1, :])


def _rnm(name, x, delta, mod_g, g_row, mod_n, sh_row, sc_row, nw, nct, deps=()):
    t, dm = x.shape
    tm = _mm_tile(t, WIDE_ROWS)
    has = delta is not None
    nd = len(deps)

    def body(*refs):
        refs = refs[:len(refs) - nd - (2 if has else 1)] + refs[len(refs) - (2 if has else 1):]
        if has:
            x_ref, d_ref, mg_ref, m_ref, nw_ref, xo_ref, h_ref = refs
        else:
            x_ref, m_ref, nw_ref, h_ref = refs
        is_ctx = pl.program_id(0) * tm + lax.broadcasted_iota(jnp.int32, (tm, 1), 0) < nct * ROW_TILE
        xv = x_ref[...]
        if has:
            xv = xv + _modrow(mg_ref, g_row, is_ctx) * d_ref[...]
            xo_ref[...] = xv
        r = lax.rsqrt(jnp.mean(xv * xv, axis=-1, keepdims=True) + EPS)
        hv = (xv * r * nw_ref[...]) * (1.0 + _modrow(m_ref, sc_row, is_ctx)) + _modrow(m_ref, sh_row, is_ctx)
        h_ref[...] = hv.astype(BF16)

    row = pl.BlockSpec((tm, dm), lambda i: (i, 0))
    ins = [x] + ([delta, mod_g] if has else []) + [mod_n, nw] + list(deps)
    in_specs = ([row] + ([row, _full(mod_g.shape)] if has else []) + [_full(mod_n.shape), _full(nw.shape)]
                + [_full(d.shape) for d in deps])
    outs = ([_sds((t, dm))] if has else []) + [_sds((t, dm), BF16)]
    out_specs = ([row] if has else []) + [row]
    res = _pcall(body, name=name, grid=(t // tm,), in_specs=in_specs, out_specs=out_specs, out_shape=outs,
                 compiler_params=_cp(("parallel",)))(*ins)
    return res if has else (None, res[0])


def _bnm(name, xn, dh, dup, yprev, mod_n, sh_row, sc_row, mod_g, g_row, nw, nct, deps=()):
    t, dm = xn.shape
    tm = ROW_TILE
    has = yprev is not None
    nd = len(deps)

    def body(*refs):
        nout = 3 if has else 2
        refs = refs[:len(refs) - nd - nout] + refs[len(refs) - nout:]
        if has:
            x_ref, dh_ref, du_ref, y_ref, mn_ref, mg_ref, nw_ref, dx_ref, dd_ref, s_ref = refs
        else:
            x_ref, dh_ref, du_ref, mn_ref, nw_ref, dx_ref, s_ref = refs
        i = pl.program_id(0)
        is_ctx = i < nct

        @pl.when(i == 0)
        def _():
            s_ref[...] = jnp.zeros_like(s_ref)

        xv = x_ref[...]
        r = lax.rsqrt(jnp.mean(xv * xv, axis=-1, keepdims=True) + EPS)
        xh = xv * r
        w = nw_ref[...]
        sc1 = 1.0 + _modrow(mn_ref, sc_row, is_ctx)
        dhv = dh_ref[...].astype(F32)
        dxh = dhv * sc1 * w
        dx = r * (dxh - xh * jnp.mean(dxh * xh, axis=-1, keepdims=True)) + du_ref[...]
        dx_ref[...] = dx
        parts = [_colsum(dhv), _colsum(dhv * (xh * w))]
        if has:
            dd_ref[...] = (_modrow(mg_ref, g_row, is_ctx) * dx).astype(BF16)
            parts.append(_colsum(dx * y_ref[...]))
        else:
            parts.append(jnp.zeros((1, dm), F32))
        upd = jnp.concatenate(parts + [jnp.zeros((5, dm), F32)], axis=0)
        dnw = jnp.concatenate([jnp.zeros((3, dm), F32), _colsum(dhv * sc1 * xh), jnp.zeros((4, dm), F32)], axis=0)

        @pl.when(is_ctx)
        def _():
            s_ref[0] += upd
            s_ref[1] += dnw

        @pl.when(jnp.logical_not(is_ctx))
        def _():
            s_ref[1] += upd + dnw

    row = pl.BlockSpec((tm, dm), lambda i: (i, 0))
    ins = [xn, dh, dup] + ([yprev] if has else []) + [mod_n] + ([mod_g] if has else []) + [nw] + list(deps)
    in_specs = ([row, row, row] + ([row] if has else []) + [_full(mod_n.shape)]
                + ([_full(mod_g.shape)] if has else []) + [_full(nw.shape)] + [_full(d.shape) for d in deps])
    if has:
        outs = [_sds((t, dm)), _sds((t, dm), BF16), _sds((2, 8, dm))]
        out_specs = [row, row, _full((2, 8, dm))]
    else:
        outs = [_sds((t - nct * tm, dm)), _sds((2, 8, dm))]
        out_specs = [pl.BlockSpec((tm, dm), lambda i: (jnp.maximum(i - nct, 0), 0)), _full((2, 8, dm))]
    res = _pcall(body, name=name, grid=(t // tm,), in_specs=in_specs, out_specs=out_specs, out_shape=outs,
                 compiler_params=_cp(("arbitrary",)))(*ins)
    return res if has else (res[0], None, res[1])


def _fin(name, x1, f, mod, g_row, fw, target, nct):
    t, dm = x1.shape
    tm = ROW_TILE

    def body(x_ref, f_ref, m_ref, fw_ref, t_ref, loss_ref, dx_ref, dd_ref, s_ref):
        i = pl.program_id(0)

        @pl.when(i == 0)
        def _():
            s_ref[...] = jnp.zeros_like(s_ref)
            loss_ref[...] = jnp.zeros_like(loss_ref)

        @pl.when(i < nct)
        def _():
            dx_ref[...] = jnp.zeros_like(dx_ref)
            dd_ref[...] = jnp.zeros_like(dd_ref)

        @pl.when(i >= nct)
        def _():
            g = m_ref[1, g_row:g_row + 1, :]
            fv = f_ref[...]
            xv = x_ref[...] + g * fv
            r = lax.rsqrt(jnp.mean(xv * xv, axis=-1, keepdims=True) + EPS)
            xh = xv * r
            w = fw_ref[...]
            err = xh * w - t_ref[...]
            loss_ref[...] += 0.5 * jnp.sum(err * err) / dm
            dout = err * (1.0 / dm)
            dxh = dout * w
            dx = r * (dxh - xh * jnp.mean(dxh * xh, axis=-1, keepdims=True))
            dx_ref[...] = dx
            dd_ref[...] = (g * dx).astype(BF16)
            s_ref[...] += jnp.concatenate([_colsum(dx * fv), _colsum(dout * xh), jnp.zeros((6, dm), F32)], axis=0)

    row = pl.BlockSpec((tm, dm), lambda i: (i, 0))
    trow = pl.BlockSpec((tm, dm), lambda i: (jnp.maximum(i - nct, 0), 0))
    return _pcall(body, name=name, grid=(t // tm,),
                  in_specs=[row, row, _full(mod.shape), _full(fw.shape), trow],
                  out_specs=[_full((8, 128)), row, row, _full((8, dm))],
                  out_shape=[_sds((8, 128)), _sds((t, dm)), _sds((t, dm), BF16), _sds((8, dm))],
                  compiler_params=_cp(("arbitrary",)))(x1, f, mod, fw, target)


def _rope_tables(t, lc):
    l = t - lc
    rows = l // GRID_W
    grid_r = jnp.broadcast_to(jnp.arange(rows, dtype=F32)[:, None], (rows, GRID_W)).reshape(-1)
    grid_c = jnp.broadcast_to(jnp.arange(GRID_W, dtype=F32)[None, :], (rows, GRID_W)).reshape(-1)

    def angles(p_seq, p_row, p_col):
        parts = []
        for p, n in zip((p_seq, p_row, p_col), ROPE_PAIRS):
            freq = ROPE_BASE ** (-jnp.arange(n, dtype=F32) / n)
            parts.append(p[:, None] * freq[None, :])
        return jnp.concatenate(parts, axis=-1)

    zc = jnp.zeros((lc,), F32)
    ang = jnp.concatenate([angles(jnp.arange(lc, dtype=F32), zc, zc),
                           angles(jnp.full((l,), lc, F32), grid_r, grid_c)], axis=0)
    cos, sin = jnp.cos(ang), jnp.sin(ang)
    return jnp.concatenate([cos, cos], axis=-1), jnp.concatenate([-sin, sin], axis=-1)


def _rope(u, cs, sn):
    return u * cs + pltpu.roll(u, HEAD_DIM // 2, 1) * sn


def _rope_t(d, cs, sn):
    return d * cs + pltpu.roll(d * sn, HEAD_DIM // 2, 1)


def _even_qkv(name, p, cs, sn):
    t = p.shape[0]
    tm = _mm_tile(t, WIDE_ROWS)
    w = HEADS * HEAD_DIM
    scale = HEAD_DIM ** -0.5

    def body(q_ref, k_ref, v_ref, cs_ref, sn_ref, qo_ref, ko_ref, vo_ref):
        c, s = cs_ref[...], sn_ref[...]
        for h in range(HEADS):
            sl = slice(h * HEAD_DIM, (h + 1) * HEAD_DIM)
            qo_ref[:, sl] = (_rope(q_ref[:, sl], c, s) * scale).astype(BF16)
            ko_ref[:, sl] = _rope(k_ref[:, sl], c, s).astype(BF16)
        vo_ref[...] = v_ref[...].astype(BF16)

    col = lambda j: pl.BlockSpec((tm, w), lambda i: (i, j))
    tab = pl.BlockSpec((tm, HEAD_DIM), lambda i: (i, 0))
    o = _sds((t, w), BF16)
    return _pcall(body, name=name, grid=(t // tm,), in_specs=[col(0), col(1), col(2), tab, tab],
                  out_specs=[col(0)] * 3, out_shape=[o, o, o], compiler_params=_cp(("parallel",)))(p, p, p, cs, sn)


def _log_sigmoid_row(x):
    e = jnp.exp(-jnp.abs(x))
    l1p = jnp.where(e < 0.01, e * (1.0 - e * (0.5 - e * (1.0 / 3.0))), jnp.log(1.0 + e))
    return jnp.minimum(x, 0.0) - l1p


def _ret_tables(lgb_ref, dm_ref, xi_ref, zt_ref):
    ri = lax.broadcasted_iota(jnp.int32, (CHUNK, CHUNK), 0).astype(F32)
    ci = lax.broadcasted_iota(jnp.int32, (CHUNK, CHUNK), 1).astype(F32)
    for d in range(2):
        for h in range(HEADS):
            idx = d * HEADS + h
            lg = _log_sigmoid_row(lgb_ref[idx:idx + 1, :])
            if d == 0:
                e, mask = ri - ci, ri >= ci
                xe, ze = ri + 1.0, (CHUNK - 1.0) - ri
            else:
                e, mask = ci - ri - 1.0, ci > ri
                xe, ze = (CHUNK - 1.0) - ri, ri
            dm_ref[idx] = jnp.where(mask, jnp.exp(lg * jnp.where(mask, e, 0.0)), 0.0)
            xi_ref[idx] = jnp.exp(lg * xe)
            zt_ref[idx] = jnp.exp(lg * ze)


def _ret_exponents(d):
    ri = lax.broadcasted_iota(jnp.int32, (CHUNK, CHUNK), 0).astype(F32)
    ci = lax.broadcasted_iota(jnp.int32, (CHUNK, CHUNK), 1).astype(F32)
    if d == 0:
        return ri - ci, ri + 1.0, (CHUNK - 1.0) - ri
    return ci - ri - 1.0, (CHUNK - 1.0) - ri, ri


RET_SUB = 2


def _bwd_chunk(n, ncc, nc):
    return jnp.where(n < ncc, ncc - 1 - n, nc - 1 - (n - ncc))


def _retention_fwd(name, q, k, v, lgb, lc):
    t, w = q.shape
    nc = t // CHUNK
    rows_per = RET_SUB * CHUNK
    nb, ncb = t // rows_per, lc // rows_per
    nh = 2 * HEADS

    def body(qf_ref, kf_ref, vf_ref, qb_ref, kb_ref, vb_ref, lgb_ref, of_ref, ob_ref, ss_ref,
             s_ref, dm_ref, xi_ref, zt_ref):
        n = pl.program_id(0)

        @pl.when(n == 0)
        def _():
            s_ref[...] = jnp.zeros_like(s_ref)
            _ret_tables(lgb_ref, dm_ref, xi_ref, zt_ref)

        where = []
        for u in range(RET_SUB):
            for d in range(2):
                refs = (qf_ref, kf_ref, vf_ref, of_ref) if d == 0 else (qb_ref, kb_ref, vb_ref, ob_ref)
                r0 = (u if d == 0 else RET_SUB - 1 - u) * CHUNK
                for h in range(HEADS):
                    where.append((u, d * HEADS + h, refs, slice(r0, r0 + CHUNK), slice(h * HEAD_DIM, (h + 1) * HEAD_DIM)))
        qs = [refs[0][rows, sl] for _, _, refs, rows, sl in where]
        ks = [refs[1][rows, sl] for _, _, refs, rows, sl in where]
        vs = [refs[2][rows, sl] for _, _, refs, rows, sl in where]
        sc = [_dot(qv, kv, NT) for qv, kv in zip(qs, ks)]
        upd = [_dot(kv.astype(F32) * zt_ref[idx], vv, TN) for (_, idx, *_), kv, vv in zip(where, ks, vs)]
        cur = [s_ref[idx] for idx in range(nh)]
        gcs = [jnp.exp(_log_sigmoid_row(lgb_ref[idx:idx + 1, :]) * float(CHUNK)) for idx in range(nh)]
        st = []
        for (u, idx, *_), du in zip(where, upd):
            st.append(cur[idx])
            ss_ref[u, idx] = cur[idx]
            cur[idx] = gcs[idx] * cur[idx] + du
        for idx in range(nh):
            s_ref[idx] = cur[idx]
        inter = [_dot(qv.astype(F32) * xi_ref[idx], s, NN) for (_, idx, *_), qv, s in zip(where, qs, st)]
        intra = [_dot(a * dm_ref[idx], vv, NN) for (_, idx, *_), a, vv in zip(where, sc, vs)]
        for (_, _, refs, rows, sl), o1, o2 in zip(where, intra, inter):
            refs[3][rows, sl] = o1 + o2

    fspec = pl.BlockSpec((rows_per, w), lambda n: (n, 0))
    bspec = pl.BlockSpec((rows_per, w), lambda n: (_bwd_chunk(n, ncb, nb), 0))
    tab = pltpu.VMEM((nh, CHUNK, CHUNK), F32)
    return _pcall(body, name=name, grid=(nb,),
                  in_specs=[fspec] * 3 + [bspec] * 3 + [_full((nh, HEAD_DIM))],
                  out_specs=[fspec, bspec, pl.BlockSpec((RET_SUB, nh, CHUNK, CHUNK), lambda n: (n, 0, 0, 0))],
                  out_shape=[_sds((t, w)), _sds((t, w)), _sds((nc, nh, CHUNK, CHUNK))],
                  scratch_shapes=[tab, tab, tab, tab],
                  compiler_params=_cp(("arbitrary",)))(q, k, v, q, k, v, lgb)


def _retention_bwd(name, q, k, v, do, ss, lgb, lc):
    t, w = q.shape
    rows_per = RET_SUB * CHUNK
    nb, ncb = t // rows_per, lc // rows_per
    nh = 2 * HEADS

    def body(qf_ref, kf_ref, vf_ref, gf_ref, qb_ref, kb_ref, vb_ref, gb_ref, ss_ref, lgb_ref,
             dqf_ref, dkf_ref, dvf_ref, dqb_ref, dkb_ref, dvb_ref, dl_ref,
             ds_ref, dm_ref, xi_ref, zt_ref, acc_ref):
        n = pl.program_id(0)

        @pl.when(n == 0)
        def _():
            ds_ref[...] = jnp.zeros_like(ds_ref)
            acc_ref[...] = jnp.zeros_like(acc_ref)
            _ret_tables(lgb_ref, dm_ref, xi_ref, zt_ref)

        where = []
        for u in reversed(range(RET_SUB)):
            for d in range(2):
                refs = ((qf_ref, kf_ref, vf_ref, gf_ref, dqf_ref, dkf_ref, dvf_ref) if d == 0
                        else (qb_ref, kb_ref, vb_ref, gb_ref, dqb_ref, dkb_ref, dvb_ref))
                r0 = (u if d == 0 else RET_SUB - 1 - u) * CHUNK
                for h in range(HEADS):
                    where.append((u, d * HEADS + h, d, refs, slice(r0, r0 + CHUNK), slice(h * HEAD_DIM, (h + 1) * HEAD_DIM)))
        qs = [refs[0][rows, sl] for *_, refs, rows, sl in where]
        ks = [refs[1][rows, sl] for *_, refs, rows, sl in where]
        vs = [refs[2][rows, sl] for *_, refs, rows, sl in where]
        gs = [refs[3][rows, sl] for *_, refs, rows, sl in where]
        st = [ss_ref[u, idx] for u, idx, *_ in where]
        sc = [_dot(qv, kv, NT) for qv, kv in zip(qs, ks)]
        dar = [_dot(gv, vv, NT) for gv, vv in zip(gs, vs)]
        t1 = [_dot(gv, s, NT) for gv, s in zip(gs, st)]
        dsn = [_dot(qv.astype(F32) * xi_ref[idx], gv, TN) for (_, idx, *_), qv, gv in zip(where, qs, gs)]
        cur = [ds_ref[idx] for idx in range(nh)]
        gcs = [jnp.exp(_log_sigmoid_row(lgb_ref[idx:idx + 1, :]) * float(CHUNK)) for idx in range(nh)]
        dsps = []
        for (_, idx, *_), x in zip(where, dsn):
            dsps.append(cur[idx])
            cur[idx] = gcs[idx] * cur[idx] + x
        for idx in range(nh):
            ds_ref[idx] = cur[idx]
        t2 = [_dot(vv, dsp, NT) for vv, dsp in zip(vs, dsps)]
        dv2 = [_dot(kv.astype(F32) * zt_ref[idx], dsp, NN) for (_, idx, *_), kv, dsp in zip(where, ks, dsps)]
        a = [x * dm_ref[idx] for (_, idx, *_), x in zip(where, sc)]
        da = [x * dm_ref[idx] for (_, idx, *_), x in zip(where, dar)]
        dq = [_dot(x, kv, NN) for x, kv in zip(da, ks)]
        dk = [_dot(x, qv, TN) for x, qv in zip(da, qs)]
        dv1 = [_dot(x, gv, TN) for x, gv in zip(a, gs)]
        for i8, (_, idx, d, refs, rows, sl) in enumerate(where):
            ee, xe, ze = _ret_exponents(d)
            xi, zt = xi_ref[idx], zt_ref[idx]
            qf32, kf32 = qs[i8].astype(F32), ks[i8].astype(F32)
            refs[4][rows, sl] = dq[i8] + xi * t1[i8]
            refs[5][rows, sl] = dk[i8] + zt * t2[i8]
            refs[6][rows, sl] = dv1[i8] + dv2[i8]
            acc_ref[idx] += (ee * a[i8] * dar[i8] + xe * xi * qf32 * t1[i8] + ze * zt * kf32 * t2[i8]
                             + (float(CHUNK) * gcs[idx]) * dsps[i8] * st[i8])

        @pl.when(n == nb - 1)
        def _():
            for idx in range(nh):
                tot = jnp.sum(acc_ref[idx])
                dl_ref[idx:idx + 1, :] = tot * _sigmoid(-lgb_ref[idx:idx + 1, :])

    fmap = lambda n: (nb - 1 - n, 0)
    bmap = lambda n: (_bwd_chunk(nb - 1 - n, ncb, nb), 0)
    fspec = pl.BlockSpec((rows_per, w), fmap)
    bspec = pl.BlockSpec((rows_per, w), bmap)
    tab = pltpu.VMEM((nh, CHUNK, CHUNK), F32)
    o = _sds((t, w))
    return _pcall(body, name=name, grid=(nb,),
                  in_specs=[fspec] * 4 + [bspec] * 4
                  + [pl.BlockSpec((RET_SUB, nh, CHUNK, CHUNK), lambda n: (nb - 1 - n, 0, 0, 0)), _full((nh, HEAD_DIM))],
                  out_specs=[fspec] * 3 + [bspec] * 3 + [_full((nh, HEAD_DIM))],
                  out_shape=[o] * 6 + [_sds((nh, HEAD_DIM))],
                  scratch_shapes=[tab, tab, tab, tab, tab],
                  compiler_params=_cp(("arbitrary",)))(q, k, v, do, q, k, v, do, ss, lgb)


def _halo_specs(tm, halo, t, width, col):
    hb = tm // halo
    last = t // halo - 1
    prev = pl.BlockSpec((halo, width), lambda i: (jnp.maximum(i * hb - 1, 0), col))
    nxt = pl.BlockSpec((halo, width), lambda i: (jnp.minimum((i + 1) * hb, last), col))
    return prev, nxt


def _halo_valid(i, nct, nt):
    vp = jnp.logical_and(i != 0, i != nct)
    vn = jnp.logical_and(i != nct - 1, i != nt - 1)
    return vp, vn


def _fill_window(win_ref, prev, cur, nxt, vp, vn, halo, tm):
    win_ref[0:halo, :] = jnp.where(vp, prev, 0.0)
    win_ref[halo:halo + tm, :] = cur
    win_ref[halo + tm:halo + tm + halo, :] = jnp.where(vn, nxt, 0.0)


CONV_SUB = 64


SUBLANES = 8


def _shift_window(win_ref, sh_ref, tm):
    rows = tm + 2 * CONV_HALO - SUBLANES
    for s in range(SUBLANES):
        sh_ref[s, 0:rows, :] = win_ref[s:s + rows, :]


def _window_rows(sh_ref, start, rows):
    s = start % SUBLANES
    return sh_ref[s, start - s:start - s + rows, :]


def _conv_taps(sh_ref, w_ref, tm, flip):
    outs = []
    for r0 in range(0, tm, CONV_SUB):
        acc = None
        for kk in range(CONV_K):
            wk = (CONV_K - 1 - kk) if flip else kk
            term = w_ref[wk:wk + 1, :] * _window_rows(sh_ref, r0 + kk + 1, CONV_SUB)
            acc = term if acc is None else acc + term
        outs.append(acc)
    return jnp.concatenate(outs, axis=0)


def _head_norm(y):
    r = lax.rsqrt(jnp.mean(y * y, axis=-1, keepdims=True) + EPS)
    return y * r, r


def _ln_stats(y):
    mu = jnp.mean(y, axis=-1, keepdims=True)
    yc = y - mu
    rs = lax.rsqrt(jnp.mean(yc * yc, axis=-1, keepdims=True) + EPS)
    return yc * rs, rs


def _ln_bwd(dyh, yh, rs):
    return rs * (dyh - jnp.mean(dyh, axis=-1, keepdims=True) - yh * jnp.mean(dyh * yh, axis=-1, keepdims=True))


def _even_mix(name, p, of, ob, cw, lnw, lnb, nct):
    t = p.shape[0]
    tm, halo = ROW_TILE, CONV_HALO
    nt = t // tm
    w = HEADS * HEAD_DIM

    def body(g_ref, a_ref, gb_ref, ap_ref, gbp_ref, an_ref, gbn_ref, of_ref, ob_ref, cw_ref, lw_ref, lb_ref,
             mix_ref, yc_ref, win_ref, sh_ref):
        i = pl.program_id(0)
        vp, vn = _halo_valid(i, nct, nt)
        glu = lambda a, b: a * _sigmoid(b)
        _fill_window(win_ref, glu(ap_ref[...], gbp_ref[...]), glu(a_ref[...], gb_ref[...]),
                     glu(an_ref[...], gbn_ref[...]), vp, vn, halo, tm)
        _shift_window(win_ref, sh_ref, tm)
        yc = _conv_taps(sh_ref, cw_ref, tm, False)
        yc_ref[...] = yc
        yh, _ = _ln_stats(yc)
        mix_ref[:, w:2 * w] = _silu(yh * lw_ref[...] + lb_ref[...]).astype(BF16)
        for h in range(HEADS):
            sl = slice(h * HEAD_DIM, (h + 1) * HEAD_DIM)
            yn, _ = _head_norm(of_ref[:, sl] + ob_ref[:, sl])
            mix_ref[:, sl] = (_silu(g_ref[:, sl]) * yn).astype(BF16)

    col = lambda j: pl.BlockSpec((tm, w), lambda i: (i, j))
    ap, an = _halo_specs(tm, halo, t, w, 4)
    gp, gn = _halo_specs(tm, halo, t, w, 5)
    row = pl.BlockSpec((tm, w), lambda i: (i, 0))
    return _pcall(body, name=name, grid=(nt,),
                  in_specs=[col(3), col(4), col(5), ap, gp, an, gn, row, row,
                            _full(cw.shape), _full(lnw.shape), _full(lnb.shape)],
                  out_specs=[pl.BlockSpec((tm, 2 * w), lambda i: (i, 0)), row],
                  out_shape=[_sds((t, 2 * w), BF16), _sds((t, w))],
                  scratch_shapes=[pltpu.VMEM((tm + 2 * halo, w), F32), pltpu.VMEM((SUBLANES, tm + 2 * halo, w), F32)],
                  compiler_params=_cp(("parallel",)))(p, p, p, p, p, p, p, of, ob, cw, lnw, lnb)


def _even_mix_bwd1(name, dmix, p, of, ob, yc, lnw, lnb):
    t = p.shape[0]
    tm = _mm_tile(t, WIDE_ROWS)
    w = HEADS * HEAD_DIM

    def body(dr_ref, dc_ref, g_ref, of_ref, ob_ref, yc_ref, lw_ref, lb_ref, do_ref, dg_ref, dyc_ref, s_ref):
        @pl.when(pl.program_id(0) == 0)
        def _():
            s_ref[...] = jnp.zeros_like(s_ref)

        for h in range(HEADS):
            sl = slice(h * HEAD_DIM, (h + 1) * HEAD_DIM)
            yn, r = _head_norm(of_ref[:, sl] + ob_ref[:, sl])
            gv = g_ref[:, sl]
            dr = dr_ref[:, sl]
            dg_ref[:, sl] = (dr * yn * _dsilu(gv)).astype(BF16)
            dyn = dr * _silu(gv)
            do_ref[:, sl] = (r * (dyn - yn * jnp.mean(dyn * yn, axis=-1, keepdims=True))).astype(BF16)
        yh, rs = _ln_stats(yc_ref[...])
        lw = lw_ref[...]
        dlo = dc_ref[...] * _dsilu(yh * lw + lb_ref[...])
        dyc_ref[...] = _ln_bwd(dlo * lw, yh, rs)
        s_ref[...] += jnp.concatenate([_colsum(dlo * yh), _colsum(dlo), jnp.zeros((6, w), F32)], axis=0)

    col = lambda j: pl.BlockSpec((tm, w), lambda i: (i, j))
    row = pl.BlockSpec((tm, w), lambda i: (i, 0))
    return _pcall(body, name=name, grid=(t // tm,),
                  in_specs=[col(0), col(1), col(3), row, row, row, _full(lnw.shape), _full(lnb.shape)],
                  out_specs=[row, row, row, _full((8, w))],
                  out_shape=[_sds((t, w), BF16), _sds((t, w), BF16), _sds((t, w)), _sds((8, w))],
                  compiler_params=_cp(("arbitrary",)))(dmix, dmix, p, of, ob, yc, lnw, lnb)


def _even_conv_bwd(name, dyc, p, cw, nct):
    t = p.shape[0]
    tm, halo = ROW_TILE, CONV_HALO
    nt = t // tm
    wf = HEADS * HEAD_DIM
    w = wf // 2
    nh = wf // w

    def body(d_ref, dp_ref, dn_ref, a_ref, gb_ref, ap_ref, gbp_ref, an_ref, gbn_ref, cw_ref,
             da_ref, dgb_ref, dw_ref, dwin_ref, uwin_ref, dsh_ref, ush_ref):
        i = pl.program_id(1)

        @pl.when(i == 0)
        def _():
            dw_ref[...] = jnp.zeros_like(dw_ref)

        vp, vn = _halo_valid(i, nct, nt)
        glu = lambda a, b: a * _sigmoid(b)
        dcur = d_ref[...]
        _fill_window(dwin_ref, dp_ref[...], dcur, dn_ref[...], vp, vn, halo, tm)
        _fill_window(uwin_ref, glu(ap_ref[...], gbp_ref[...]), glu(a_ref[...], gb_ref[...]),
                     glu(an_ref[...], gbn_ref[...]), vp, vn, halo, tm)
        _shift_window(dwin_ref, dsh_ref, tm)
        _shift_window(uwin_ref, ush_ref, tm)
        du = _conv_taps(dsh_ref, cw_ref, tm, True)
        av = a_ref[...]
        sg = _sigmoid(gb_ref[...])
        da_ref[...] = (du * sg).astype(BF16)
        dgb_ref[...] = (du * av * sg * (1.0 - sg)).astype(BF16)
        rows = [_colsum(dcur * _window_rows(ush_ref, kk + 1, tm)) for kk in range(CONV_K)]
        dw_ref[...] += jnp.concatenate(rows + [jnp.zeros((1, w), F32)], axis=0)

    hb, last = tm // halo, t // halo - 1
    col = lambda j: pl.BlockSpec((tm, w), lambda c, i: (i, j * nh + c))
    prev = lambda j: pl.BlockSpec((halo, w), lambda c, i: (jnp.maximum(i * hb - 1, 0), j * nh + c))
    nxt = lambda j: pl.BlockSpec((halo, w), lambda c, i: (jnp.minimum((i + 1) * hb, last), j * nh + c))
    wspec = pl.BlockSpec((CONV_K + 1, w), lambda c, i: (0, c))
    win = pltpu.VMEM((tm + 2 * halo, w), F32)
    shifted = pltpu.VMEM((SUBLANES, tm + 2 * halo, w), F32)
    return _pcall(body, name=name, grid=(nh, nt),
                  in_specs=[col(0), prev(0), nxt(0), col(4), col(5), prev(4), prev(5), nxt(4), nxt(5), wspec],
                  out_specs=[col(0), col(0), wspec],
                  out_shape=[_sds((t, wf), BF16), _sds((t, wf), BF16), _sds((CONV_K + 1, wf))],
                  scratch_shapes=[win, win, shifted, shifted],
                  compiler_params=_cp(("arbitrary", "arbitrary")))(dyc, dyc, dyc, p, p, p, p, p, p, cw)


def _even_dp(name, dqs, dks, dvs, dg, da, dgb, cs, sn):
    t, w = dg.shape
    tm = _mm_tile(t, WIDE_ROWS)
    scale = HEAD_DIM ** -0.5

    def body(dqf_ref, dqb_ref, dkf_ref, dkb_ref, dvf_ref, dvb_ref, dg_ref, da_ref, dgb_ref, cs_ref, sn_ref, dp_ref):
        c, s = cs_ref[...], sn_ref[...]
        for h in range(HEADS):
            sl = slice(h * HEAD_DIM, (h + 1) * HEAD_DIM)
            dp_ref[:, sl] = (_rope_t(dqf_ref[:, sl] + dqb_ref[:, sl], c, s) * scale).astype(BF16)
            dp_ref[:, w + h * HEAD_DIM:w + (h + 1) * HEAD_DIM] = _rope_t(dkf_ref[:, sl] + dkb_ref[:, sl], c, s).astype(BF16)
        dp_ref[:, 2 * w:3 * w] = (dvf_ref[...] + dvb_ref[...]).astype(BF16)
        dp_ref[:, 3 * w:4 * w] = dg_ref[...]
        dp_ref[:, 4 * w:5 * w] = da_ref[...]
        dp_ref[:, 5 * w:6 * w] = dgb_ref[...]

    row = pl.BlockSpec((tm, w), lambda i: (i, 0))
    tab = pl.BlockSpec((tm, HEAD_DIM), lambda i: (i, 0))
    return _pcall(body, name=name, grid=(t // tm,), in_specs=[row] * 9 + [tab, tab],
                  out_specs=pl.BlockSpec((tm, 6 * w), lambda i: (i, 0)), out_shape=_sds((t, 6 * w), BF16),
                  compiler_params=_cp(("parallel",)))(dqs[0], dqs[1], dks[0], dks[1], dvs[0], dvs[1], dg, da, dgb, cs, sn)


GROUPS = 4
GC = 128
INV_SQRT2 = 0.7071067811865476
INV_SQRT_2PI = 0.3989422804014327


def _gelu(x):
    return 0.5 * x * (1.0 + lax.erf(x * INV_SQRT2))


def _dgelu(x):
    return 0.5 * (1.0 + lax.erf(x * INV_SQRT2)) + x * jnp.exp(-0.5 * x * x) * INV_SQRT_2PI


def _pool_count(i, nct, lc, t, tm, rows, row0, left, right):
    is_ctx = i < nct
    seg_start = jnp.where(is_ctx, 0, lc)
    seg_len = jnp.where(is_ctx, lc, t - lc)
    pos = i * tm + row0 - seg_start + lax.broadcasted_iota(jnp.int32, (rows, GC), 0)
    cnt = jnp.minimum(pos + right, seg_len - 1) - jnp.maximum(pos - left, 0) + 1
    return jnp.maximum(cnt, 1).astype(F32)


def _spatial_gate(vln, sgw_ref, sgb_ref, tm):
    cols = []
    for g in range(GROUPS):
        sl = slice(g * GC, (g + 1) * GC)
        parts = [_dot(sgw_ref[g], vln[r0:r0 + CHUNK, sl], NN) + sgb_ref[g] for r0 in range(0, tm, CHUNK)]
        cols.append(jnp.concatenate(parts, axis=0))
    return jnp.concatenate(cols, axis=1)


def _odd_mix(name, p, pw, pscale, lnw, lnb, sgw, sgb, nct, lc):
    t = p.shape[0]
    tm, halo = ROW_TILE, POOL_HALO
    nt = t // tm
    w = GROUPS * GC

    def body(pc_ref, pp_ref, pn_ref, pu_ref, pv_ref, pw_ref, ps_ref, lw_ref, lb_ref, sgw_ref, sgb_ref,
             mix_ref, m_ref, win_ref):
        i = pl.program_id(0)
        vp, vn = _halo_valid(i, nct, nt)
        pc = pc_ref[...]
        _fill_window(win_ref, pp_ref[...], pc, pn_ref[...], vp, vn, halo, tm)
        for g, wd in enumerate(POOL_WINDOWS):
            sl = slice(g * GC, (g + 1) * GC)
            left = wd // 2
            right = wd - 1 - left
            s = None
            for o in range(-left, right + 1):
                term = win_ref[halo + o:halo + o + tm, sl]
                s = term if s is None else s + term
            mg = s / _pool_count(i, nct, lc, t, tm, tm, 0, left, right) - pc[:, sl]
            m_ref[:, sl] = mg
            mix_ref[:, sl] = (_dot(mg, pw_ref[g], NN) * ps_ref[:, sl]).astype(BF16)
        u = _gelu(pu_ref[...])
        vh, _ = _ln_stats(_gelu(pv_ref[...]))
        s = _spatial_gate(vh * lw_ref[...] + lb_ref[...], sgw_ref, sgb_ref, tm)
        mix_ref[:, w:2 * w] = (u * s).astype(BF16)

    col = lambda j: pl.BlockSpec((tm, w), lambda i: (i, j))
    pp, pn = _halo_specs(tm, halo, t, w, 0)
    return _pcall(body, name=name, grid=(nt,),
                  in_specs=[col(0), pp, pn, col(1), col(2), _full(pw.shape), _full(pscale.shape),
                            _full(lnw.shape), _full(lnb.shape), _full(sgw.shape), _full(sgb.shape)],
                  out_specs=[pl.BlockSpec((tm, 2 * w), lambda i: (i, 0)), col(0)],
                  out_shape=[_sds((t, 2 * w), BF16), _sds((t, w))],
                  scratch_shapes=[pltpu.VMEM((tm + 2 * halo, w), F32)],
                  compiler_params=_cp(("parallel",)))(p, p, p, p, p, pw, pscale, lnw, lnb, sgw, sgb)


def _odd_mix_bwd1(name, dmix, p, m, pw, pscale, lnw, lnb, sgw, sgb):
    t = p.shape[0]
    tm = ROW_TILE
    w = GROUPS * GC

    def body(dpo_ref, dsg_ref, pu_ref, pv_ref, m_ref, pw_ref, ps_ref, lw_ref, lb_ref, sgw_ref, sgb_ref,
             dm_ref, dpd_ref, vec_ref, dpw_ref, dsgw_ref, dsgb_ref):
        @pl.when(pl.program_id(0) == 0)
        def _():
            vec_ref[...] = jnp.zeros_like(vec_ref)
            dpw_ref[...] = jnp.zeros_like(dpw_ref)
            dsgw_ref[...] = jnp.zeros_like(dsgw_ref)
            dsgb_ref[...] = jnp.zeros_like(dsgb_ref)

        dscale = []
        for g in range(GROUPS):
            sl = slice(g * GC, (g + 1) * GC)
            mg = m_ref[:, sl]
            dpo = dpo_ref[:, sl]
            dscale.append(_colsum(dpo * _dot(mg, pw_ref[g], NN)))
            dpo = dpo * ps_ref[:, sl]
            dm_ref[:, sl] = _dot(dpo, pw_ref[g], NT)
            dpw_ref[g] += _dot(mg, dpo, TN)
        pu, pv = pu_ref[...], pv_ref[...]
        u = _gelu(pu)
        vh, rs = _ln_stats(_gelu(pv))
        lw = lw_ref[...]
        vln = vh * lw + lb_ref[...]
        s = _spatial_gate(vln, sgw_ref, sgb_ref, tm)
        dsg = dsg_ref[...]
        dpd_ref[:, 0:w] = (dsg * s * _dgelu(pu)).astype(BF16)
        ds = dsg * u
        cols = []
        for g in range(GROUPS):
            sl = slice(g * GC, (g + 1) * GC)
            parts = []
            for r0 in range(0, tm, CHUNK):
                dsc = ds[r0:r0 + CHUNK, sl]
                parts.append(_dot(sgw_ref[g], dsc, TN))
                dsgw_ref[g] += _dot(dsc, vln[r0:r0 + CHUNK, sl], NT)
                dsgb_ref[g] += dsc
            cols.append(jnp.concatenate(parts, axis=0))
        dvln = jnp.concatenate(cols, axis=1)
        dpd_ref[:, w:2 * w] = (_ln_bwd(dvln * lw, vh, rs) * _dgelu(pv)).astype(BF16)
        vec_ref[...] += jnp.concatenate([jnp.concatenate(dscale, axis=1), _colsum(dvln * vh), _colsum(dvln),
                                         jnp.zeros((5, w), F32)], axis=0)

        @pl.when(pl.program_id(0) == t // tm - 1)
        def _():
            for g in range(GROUPS):
                dsgb_ref[g] = jnp.broadcast_to(jnp.sum(dsgb_ref[g], axis=1, keepdims=True), (GC, GC))

    col = lambda j: pl.BlockSpec((tm, w), lambda i: (i, j))
    mat = _full((GROUPS, GC, GC))
    return _pcall(body, name=name, grid=(t // tm,),
                  in_specs=[col(0), col(1), col(1), col(2), col(0), _full(pw.shape), _full(pscale.shape),
                            _full(lnw.shape), _full(lnb.shape), _full(sgw.shape), _full(sgb.shape)],
                  out_specs=[col(0), pl.BlockSpec((tm, 2 * w), lambda i: (i, 0)), _full((8, w)), mat, mat, mat],
                  out_shape=[_sds((t, w)), _sds((t, 2 * w), BF16), _sds((8, w)),
                             _sds((GROUPS, GC, GC)), _sds((GROUPS, GC, GC)), _sds((GROUPS, GC, GC))],
                  compiler_params=_cp(("arbitrary",)))(dmix, dmix, p, p, m, pw, pscale, lnw, lnb, sgw, sgb)


def _odd_dp(name, dm, dpd, nct, lc):
    t, w = dm.shape
    tm, halo = ROW_TILE, POOL_HALO
    nt = t // tm

    def body(d_ref, dp_ref, dn_ref, dpd_ref, o_ref, win_ref):
        i = pl.program_id(0)
        vp, vn = _halo_valid(i, nct, nt)
        dcur = d_ref[...]
        _fill_window(win_ref, dp_ref[...], dcur, dn_ref[...], vp, vn, halo, tm)
        for g, wd in enumerate(POOL_WINDOWS):
            sl = slice(g * GC, (g + 1) * GC)
            left = wd // 2
            right = wd - 1 - left
            win_ref[:, sl] = win_ref[:, sl] / _pool_count(i, nct, lc, t, tm, tm + 2 * halo, -halo, left, right)
            s = None
            for o in range(-right, left + 1):
                term = win_ref[halo + o:halo + o + tm, sl]
                s = term if s is None else s + term
            o_ref[:, sl] = (s - dcur[:, sl]).astype(BF16)
        o_ref[:, w:3 * w] = dpd_ref[...]

    row = pl.BlockSpec((tm, w), lambda i: (i, 0))
    pp, pn = _halo_specs(tm, halo, t, w, 0)
    return _pcall(body, name=name, grid=(nt,),
                  in_specs=[row, pp, pn, pl.BlockSpec((tm, 2 * w), lambda i: (i, 0))],
                  out_specs=pl.BlockSpec((tm, 3 * w), lambda i: (i, 0)), out_shape=_sds((t, 3 * w), BF16),
                  scratch_shapes=[pltpu.VMEM((tm + 2 * halo, w), F32)],
                  compiler_params=_cp(("parallel",)))(dm, dm, dm, dpd)


def _place():
    x, y, c = lax.axis_index("x"), lax.axis_index("y"), lax.axis_index("c")
    chips = [(1 - x, y), (x, 1 - y), (1 - x, 1 - y)]
    return x, y, c, chips


def _chip_index(cx, cy):
    return 2 * cx + cy


def _all_gather8(name, blk, after=()):
    m_per, n = blk.shape
    na = len(after)

    def body(x_ref, *rest):
        out_ref, send_sems, recv_sems, local_sem = rest[na:]
        x, y, c, chips = _place()
        me, sibling = (x, y, c), (x, y, 1 - c)

        def rows(px, py, pc):
            return out_ref.at[pl.ds((4 * px + 2 * py + pc) * m_per, m_per), :]

        def copy(k, block, to, src=None):
            return pltpu.make_async_remote_copy(
                src_ref=rows(*block) if src is None else src, dst_ref=rows(*block),
                send_sem=send_sems.at[k], recv_sem=recv_sems.at[k], device_id=to, device_id_type=MESH)

        mine = pltpu.make_async_copy(x_ref, rows(*me), local_sem)
        mine.start()
        first = [copy(0, me, sibling, src=x_ref)]
        first += [copy(1 + j, me, (*chip, c), src=x_ref) for j, chip in enumerate(chips)]
        for cp in first:
            cp.start()
        passed = [copy(4 + j, (*chip, c), sibling) for j, chip in enumerate(chips)]
        for j, chip in enumerate(chips):
            copy(1 + j, (*chip, c), me).wait_recv()
            passed[j].start()
        copy(0, sibling, me).wait_recv()
        for j, chip in enumerate(chips):
            copy(4 + j, (*chip, 1 - c), me).wait_recv()
        for cp in first + passed:
            cp.wait_send()
        mine.wait()

    return _pcall(body, name=name, out_shape=_sds((8 * m_per, n), blk.dtype),
                  in_specs=[pl.BlockSpec(memory_space=pltpu.VMEM)] + [pl.BlockSpec(memory_space=pl.ANY)] * na,
                  out_specs=pl.BlockSpec(memory_space=pltpu.VMEM),
                  scratch_shapes=[pltpu.SemaphoreType.DMA((7,)), pltpu.SemaphoreType.DMA((7,)), pltpu.SemaphoreType.DMA],
                  compiler_params=_cp())(blk, *after)


ANY = pl.BlockSpec(memory_space=pl.ANY)


def _half(which, rows):
    return pl.ds(pl.multiple_of(which * rows, 16), rows)


def _gather_weights(name, ws, after=()):
    nw = len(ws)
    na = len(after)
    ns = 7

    def body(*refs):
        w_refs, o_refs = refs[:nw], refs[nw + na:2 * nw + na]
        send_sems, recv_sems = refs[2 * nw + na:]
        x, y, c, chips = _place()
        me_chip = _chip_index(x, y)
        sibling = (x, y, 1 - c)

        def rcopy(t, k, src, dst, to):
            return pltpu.make_async_remote_copy(src_ref=src, dst_ref=dst, send_sem=send_sems.at[t * ns + k],
                                                recv_sem=recv_sems.at[t * ns + k], device_id=to, device_id_type=MESH)

        sends = []
        for t in range(nw):
            lh = w_refs[t].shape[0] // 2
            for k, chip in enumerate(chips):
                sends.append(rcopy(t, k, w_refs[t].at[_half(c, lh)], o_refs[t].at[me_chip, _half(c, lh)], (*chip, c)))
                sends[-1].start()
            sends.append(rcopy(t, 6, w_refs[t], o_refs[t].at[me_chip], sibling))
            sends[-1].start()
        for t in range(nw):
            lh = w_refs[t].shape[0] // 2
            for k, chip in enumerate(chips):
                part = o_refs[t].at[_chip_index(*chip), _half(c, lh)]
                rcopy(t, k, part, part, (*chip, c)).wait_recv()
                sends.append(rcopy(t, 3 + k, part, part, sibling))
                sends[-1].start()
        for t in range(nw):
            lh = w_refs[t].shape[0] // 2
            own = o_refs[t].at[me_chip]
            rcopy(t, 6, own, own, sibling).wait_recv()
            for k, chip in enumerate(chips):
                part = o_refs[t].at[_chip_index(*chip), _half(1 - c, lh)]
                rcopy(t, 3 + k, part, part, sibling).wait_recv()
        for cp in sends:
            cp.wait_send()

    return _pcall(body, name=name, out_shape=[_sds((4,) + w.shape, w.dtype) for w in ws],
                  in_specs=[ANY] * (nw + na), out_specs=[ANY] * nw,
                  scratch_shapes=[pltpu.SemaphoreType.DMA((ns * nw,)), pltpu.SemaphoreType.DMA((ns * nw,))],
                  compiler_params=_cp())(*ws, *after)


def _rs_share(name, ss):
    ng = len(ss)

    def body(*refs):
        o_refs = refs[ng:2 * ng]
        send_sems, recv_sems = refs[2 * ng:]
        x, y, c, _ = _place()
        cps = []
        for t in range(ng):
            lh = o_refs[t].shape[1] // 2
            mine = o_refs[t].at[:, _half(c, lh)]
            cp = pltpu.make_async_remote_copy(
                src_ref=mine, dst_ref=mine, send_sem=send_sems.at[t], recv_sem=recv_sems.at[t],
                device_id=(x, y, 1 - c), device_id_type=MESH)
            cp.start()
            cps.append(cp)
        for t in range(ng):
            lh = o_refs[t].shape[1] // 2
            cps[t].wait_send()
            theirs = o_refs[t].at[:, _half(1 - c, lh)]
            pltpu.make_async_remote_copy(
                src_ref=theirs, dst_ref=theirs, send_sem=send_sems.at[t], recv_sem=recv_sems.at[t],
                device_id=(x, y, 1 - c), device_id_type=MESH).wait_recv()

    return _pcall(body, name=name, out_shape=[_sds(s.shape, s.dtype) for s in ss],
                  in_specs=[ANY] * ng, out_specs=[ANY] * ng, input_output_aliases={t: t for t in range(ng)},
                  scratch_shapes=[pltpu.SemaphoreType.DMA((ng,)), pltpu.SemaphoreType.DMA((ng,))],
                  compiler_params=_cp())(*ss)


HBM = pl.BlockSpec(memory_space=pltpu.HBM)
SEMS = pl.BlockSpec(memory_space=pltpu.SEMAPHORE)
EFFECT = pltpu.SideEffectType.DATAFLOW_SIDE_EFFECTING
TOKEN = (8, 128)


def _in_hbm(a):
    return pltpu.with_memory_space_constraint(a, pltpu.HBM)


def _split_start(name, srcs, lands, copies, after):
    ns, nl, na = len(srcs), len(lands), len(after)
    ncopies = len(copies([s for s in srcs], [l for l in lands], probe=True))

    def body(*refs):
        src_refs, land_refs = refs[:ns], refs[ns:ns + nl]
        send_sems, recv_sems = refs[ns + nl + na], refs[ns + nl + na + 1]
        token = refs[-1]
        for k, (src, dst, to) in enumerate(copies(src_refs, land_refs)):
            pltpu.make_async_remote_copy(src_ref=src, dst_ref=dst, send_sem=send_sems.at[k], recv_sem=recv_sems.at[k],
                                         device_id=to, device_id_type=MESH).start()
        token[...] = jnp.zeros_like(token)

    thru = [pltpu.HBM(a.shape, a.dtype) for a in list(srcs) + list(lands)]
    outs = _pcall(body, name=name,
                  out_shape=(pltpu.SemaphoreType.DMA((ncopies,)), pltpu.SemaphoreType.DMA((ncopies,)), *thru, _sds(TOKEN)),
                  in_specs=[HBM] * (ns + nl) + [ANY] * na,
                  out_specs=(SEMS, SEMS, *([HBM] * (ns + nl)), pl.BlockSpec(memory_space=pltpu.VMEM)),
                  input_output_aliases={t: 2 + t for t in range(ns + nl)},
                  compiler_params=pltpu.CompilerParams(has_side_effects=EFFECT))(
        *[_in_hbm(a) for a in list(srcs) + list(lands)], *after)
    return outs[0], outs[1], list(outs[2:2 + ns]), list(outs[2 + ns:2 + ns + nl]), outs[-1]


def _split_wait(name, started, copies, after, first=0):
    send_sems, recv_sems, srcs, lands, _ = started
    ns, nl, na = len(srcs), len(lands), len(after)

    def body(*refs):
        src_refs, land_refs = refs[:ns], refs[ns:ns + nl]
        send_sems_ref, recv_sems_ref = refs[ns + nl], refs[ns + nl + 1]
        for k, (src, dst, to) in enumerate(copies(src_refs, land_refs)):
            cp = pltpu.make_async_remote_copy(src_ref=src, dst_ref=dst, send_sem=send_sems_ref.at[first + k],
                                              recv_sem=recv_sems_ref.at[first + k], device_id=to, device_id_type=MESH)
            cp.wait_send()
            cp.wait_recv()

    thru = [pltpu.HBM(a.shape, a.dtype) for a in list(srcs) + list(lands)]
    outs = _pcall(body, name=name, out_shape=tuple(thru),
                  in_specs=[HBM] * (ns + nl) + [SEMS, SEMS] + [ANY] * na, out_specs=tuple([HBM] * (ns + nl)),
                  input_output_aliases={t: t for t in range(ns + nl)},
                  compiler_params=pltpu.CompilerParams(has_side_effects=EFFECT))(
        *srcs, *lands, send_sems, recv_sems, *after)
    return list(outs[:ns]), list(outs[ns:])


def _pair_copies(n):
    def copies(src_refs, land_refs, probe=False):
        if probe:
            return [None] * n
        x, y, c, _ = _place()
        return [(src_refs[t].at[:, _half(1 - c, src_refs[t].shape[1] // 2)], land_refs[t], (x, y, 1 - c))
                for t in range(n)]
    return copies


def _gather_copies(n):
    def copies(src_refs, land_refs, probe=False):
        if probe:
            return [None] * (4 * n)
        x, y, c, chips = _place()
        me_chip = _chip_index(x, y)
        out = []
        for t in range(n):
            for to in [(*chip, c) for chip in chips] + [(x, y, 1 - c)]:
                out.append((src_refs[t], land_refs[t].at[me_chip], to))
        return out
    return copies


def _scatter_copies(n):
    def copies(src_refs, land_refs, probe=False):
        if probe:
            return [None] * (3 * n)
        x, y, c, chips = _place()
        out = []
        for t in range(n):
            for k, chip in enumerate(chips):
                out.append((src_refs[t].at[_chip_index(*chip)], land_refs[t].at[k], (*chip, c)))
        return out
    return copies


def _row_block(r, cn):
    if r % 8:
        return r
    best = 8
    for d in range(8, r + 1, 8):
        if r % d == 0 and d * cn * 4 <= (2 << 20):
            best = d
    return best


def _add_half(name, gl, al, idx):
    n = len(gl)
    j, rh, cn = al[0].shape
    tr = _row_block(rh, cn)
    nb = rh // tr

    def body(i_ref, *refs):
        for t in range(n):
            refs[2 * n + t][...] = (refs[t][...] + refs[n + t][...]).astype(BF16)

    blk = (None, tr, cn)
    gspec = pl.BlockSpec(blk, lambda jj, i, i_ref: (jj, i_ref[0] * nb + i, 0))
    aspec = pl.BlockSpec(blk, lambda jj, i, i_ref: (jj, i, 0))
    gs = pltpu.PrefetchScalarGridSpec(num_scalar_prefetch=1, grid=(j, nb), in_specs=[gspec] * n + [aspec] * n,
                                      out_specs=[aspec] * n)
    return _pcall(body, name=name, grid_spec=gs, out_shape=[_sds(al[0].shape, BF16)] * n,
                  compiler_params=_cp(("parallel", "parallel")))(idx, *gl, *al)


def _sum_final(name, gl, al, bl, idx, bufs, lyr, nlyr):
    n = len(gl)
    _, r, cn = gl[0].shape
    rh = r // 2
    tr = _row_block(rh, cn)
    nb = rh // tr
    has = bufs[0] is not None

    def body(i_ref, *refs):
        outs = refs[len(refs) - n:]
        for t in range(n):
            own = refs[t][...] + refs[n + t][...]
            b_ref = refs[2 * n + t]
            outs[t][...] = (own + b_ref[0].astype(F32)) + (b_ref[1].astype(F32) + b_ref[2].astype(F32))

    blk = (None, tr, cn)
    in_specs = ([pl.BlockSpec(blk, lambda i, i_ref: (i_ref[1], i_ref[0] * nb + i, 0))] * n
                + [pl.BlockSpec(blk, lambda i, i_ref: (i_ref[1], i, 0))] * n
                + [pl.BlockSpec((3, tr, cn), lambda i, i_ref: (0, i, 0))] * n)
    args = [idx, *gl, *al, *bl]
    kw = {}
    if has:
        in_specs += [ANY] * n
        args += list(bufs)
        kw["input_output_aliases"] = {1 + 3 * n + t: t for t in range(n)}
    gs = pltpu.PrefetchScalarGridSpec(
        num_scalar_prefetch=1, grid=(nb,), in_specs=in_specs,
        out_specs=[pl.BlockSpec(blk, lambda i, i_ref: (lyr, i_ref[0] * nb + i, 0))] * n)
    return _pcall(body, name=name, grid_spec=gs, out_shape=[_sds((nlyr, r, cn))] * n,
                  compiler_params=_cp(("parallel",)), **kw)(*args)


def _sum8(name, g):
    _, r, n = g.shape
    tr = 8

    def body(g_ref, o_ref):
        o_ref[...] = ((g_ref[0] + g_ref[1]) + (g_ref[2] + g_ref[3])) + ((g_ref[4] + g_ref[5]) + (g_ref[6] + g_ref[7]))

    return _pcall(body, name=name, grid=(r // tr,), in_specs=[pl.BlockSpec((8, tr, n), lambda i: (0, i, 0))],
                  out_specs=pl.BlockSpec((tr, n), lambda i: (i, 0)), out_shape=_sds((r, n)),
                  compiler_params=_cp(("parallel",)))(g)


def _ada_mod(name, c16, ada_w, bias):
    nl, dm, n = ada_w.shape

    def body(c_ref, w_ref, b_ref, o_ref):
        o_ref[...] = _dot(_silu(c_ref[...]), w_ref[...], NN) + b_ref[...]

    return _pcall(body, name=name, grid=(nl,),
                  in_specs=[_full(c16.shape), pl.BlockSpec((None, dm, n), lambda i: (i, 0, 0)),
                            pl.BlockSpec((None, 1, n), lambda i: (i, 0, 0))],
                  out_specs=pl.BlockSpec((None, 16, n), lambda i: (i, 0, 0)), out_shape=_sds((nl, 16, n)),
                  compiler_params=_cp(("parallel",)))(c16, ada_w, bias)


def _ada_bwd(name, c16, dmod, ada_w):
    nl, dm, n = ada_w.shape

    def body(c_ref, d_ref, w_ref, gw_ref, dc_ref):
        @pl.when(pl.program_id(0) == 0)
        def _():
            dc_ref[...] = jnp.zeros_like(dc_ref)

        dv = d_ref[...]
        gw_ref[...] = _dot(_silu(c_ref[...]), dv, TN)
        dc_ref[...] += _dot(dv, w_ref[...], NT)

    return _pcall(body, name=name, grid=(nl,),
                  in_specs=[_full(c16.shape), pl.BlockSpec((None, 16, n), lambda i: (i, 0, 0)),
                            pl.BlockSpec((None, dm, n), lambda i: (i, 0, 0))],
                  out_specs=[pl.BlockSpec((None, dm, n), lambda i: (i, 0, 0)), _full((16, dm))],
                  out_shape=[_sds((nl, dm, n)), _sds((16, dm))],
                  compiler_params=_cp(("arbitrary",)))(c16, dmod, ada_w)


def _rowsum16(name, dmod):
    nl, _, n = dmod.shape

    def body(d_ref, o_ref):
        o_ref[...] = _colsum(d_ref[...])

    return _pcall(body, name=name, grid=(nl,), in_specs=[pl.BlockSpec((None, 16, n), lambda i: (i, 0, 0))],
                  out_specs=pl.BlockSpec((None, 1, n), lambda i: (i, 0, 0)), out_shape=_sds((nl, 1, n)),
                  compiler_params=_cp(("parallel",)))(dmod)


def _cctx_grad(name, parts, c_ctx):
    def body(p_ref, c_ref, o_ref):
        tot = (p_ref[0:1, :] + p_ref[1:2, :]) + (p_ref[2:3, :] + p_ref[3:4, :])
        o_ref[...] = tot * _dsilu(c_ref[...])

    return _pcall(body, name=name, out_shape=_sds(c_ctx.shape), compiler_params=_cp())(parts, c_ctx)


def _adamw(name, w, g, m, v, with_grad=False):
    shape = w.shape
    cn = shape[-1]
    r = math.prod(shape[:-1]) if len(shape) > 1 else 1
    tr = _row_block(r, cn)
    c1 = 1.0 - ADAM_B1 ** ADAM_STEP
    c2 = 1.0 - ADAM_B2 ** ADAM_STEP
    nout = 4 if with_grad else 3

    def body(w_ref, g_ref, m_ref, v_ref, d_ref, mo_ref, vo_ref, *rest):
        gv = g_ref[...]
        mn = ADAM_B1 * m_ref[...] + (1.0 - ADAM_B1) * gv
        vn = ADAM_B2 * v_ref[...] + (1.0 - ADAM_B2) * (gv * gv)
        d_ref[...] = -ADAM_LR * ((mn / c1) / (jnp.sqrt(vn / c2) + ADAM_EPS) + ADAM_WD * w_ref[...])
        mo_ref[...] = mn
        vo_ref[...] = vn
        if with_grad:
            rest[0][...] = gv

    blk = pl.BlockSpec((tr, cn), lambda i: (i, 0))
    o = _sds((r, cn))
    outs = _pcall(body, name=name, grid=(r // tr,), in_specs=[blk] * 4, out_specs=[blk] * nout, out_shape=[o] * nout,
                  compiler_params=_cp(("parallel",)))(*[a.reshape(r, cn) for a in (w, g, m, v)])
    return tuple(a.reshape(shape) for a in outs)


def _local_step(xs, target, modt, nw, fnw, get_w, get_ffn, put_g, ev, od, lc):
    t, dm = xs.shape
    nct = lc // ROW_TILE
    depth = nw.shape[0]
    cs, sn = _rope_tables(t, lc)
    saved = []
    x_in, x1p, fp = xs, None, None
    for i in range(depth):
        j, even = i // 2, i % 2 == 0
        tag = f"l{i}"
        w, deps = get_w(i, [fp] if i else [])
        if i == 0:
            _, h = _rnm(tag + "_norm1", x_in, None, None, 0, modt[0], 0, 1, nw[0, 0], nct, deps)
        else:
            x_in, h = _rnm(tag + "_norm1", x1p, fp, modt[i - 1], 5, modt[i], 0, 1, nw[i, 0], nct, deps)
        s = dict(x=x_in, h=h, w=w)
        if even:
            p = _mm_cols(tag + "_in", h, w["in"])
            q, k, v = _even_qkv(tag + "_qkv", p, cs, sn)
            of, ob, ss = _retention_fwd(tag + "_ret", q, k, v, ev["lgb"][j], lc)
            mix, yc = _even_mix(tag + "_mix", p, of, ob, ev["cw"][j], ev["lnw"][j], ev["lnb"][j], nct)
            y = _mm_full(tag + "_out", mix, w["out"], NN)
            s.update(p=p, q=q, k=k, v=v, of=of, ob=ob, ss=ss, yc=yc)
        else:
            p = _mm_cols(tag + "_in", h, w["in"])
            mix, m = _odd_mix(tag + "_mix", p, od["pw"][j], od["ps"][j], od["lnw"][j], od["lnb"][j],
                              od["sgw"][j], od["sgb"][j], nct, lc)
            y = _mm_full(tag + "_out", mix, w["out"], NN)
            s.update(p=p, m=m)
        x1, h2 = _rnm(tag + "_norm2", x_in, y, modt[i], 2, modt[i], 3, 4, nw[i, 1], nct)
        w.update(get_ffn(i, [y]))
        a, gt, up = _ffn_up(tag + "_ffn_up", h2, w["gate"], w["up"])
        f = _mm_full(tag + "_ffn_down", a, w["down"], NN)
        s.update(mix=mix, y=y, x1=x1, h2=h2, a=a, gt=gt, up=up, f=f)
        saved.append(s)
        x1p, fp = x1, f

    loss_blk, dx, df, fin_s = _fin("final", x1p, fp, modt[depth - 1], 5, fnw, target, nct)

    deps = []
    dmod = [[None] * 6 for _ in range(depth)]
    dnw = [[None, None] for _ in range(depth)]
    zero2 = jnp.zeros((2, dm), F32)
    dmod[depth - 1][5] = jnp.stack([zero2[0], fin_s[0]])
    small = dict(dfnw=fin_s[1], ev=[], od=[])
    for i in reversed(range(depth)):
        j, even = i // 2, i % 2 == 0
        tag = f"l{i}b"
        s = saved[i]
        w = s["w"]
        fh = w["down"].shape[0] // 2
        g = {}
        dgt, dup = _ffn_down_bwd(tag + "_ffn_down", df, w["down"], s["gt"], s["up"])
        g["down"] = _wgrad_rows(tag + "_gdown", s["a"], fh, df)
        g["gate"] = _wgrad_rows(tag + "_ggate", dgt, fh, s["h2"])
        g["up"] = _wgrad_rows(tag + "_gup", dup, fh, s["h2"])
        deps = put_g(i, "f", g)
        dh2 = _ffn_in_bwd(tag + "_ffn_in", dgt, dup, w["gate"], w["up"])
        dx1, dy, s2 = _bnm(tag + "_norm2", s["x1"], dh2, dx, s["y"], modt[i], 3, 4, modt[i], 2, nw[i, 1], nct, deps)
        dmod[i][3], dmod[i][4], dmod[i][2] = s2[:, 0], s2[:, 1], s2[:, 2]
        dnw[i][1] = s2[1, 3]
        dmix = _mm_full(tag + "_out", dy, w["out"], NT)
        g["out"] = _wgrad_rows(tag + "_gout", s["mix"], w["out"].shape[0] // 2, dy)
        if even:
            do, dg, dyc, lns = _even_mix_bwd1(tag + "_mix1", dmix, s["p"], s["of"], s["ob"], s["yc"],
                                              ev["lnw"][j], ev["lnb"][j])
            da, dgb, dcw = _even_conv_bwd(tag + "_conv", dyc, s["p"], ev["cw"][j], nct)
            dqf, dkf, dvf, dqb, dkb, dvb, dl = _retention_bwd(tag + "_ret", s["q"], s["k"], s["v"], do, s["ss"],
                                                              ev["lgb"][j], lc)
            dp = _even_dp(tag + "_dp", (dqf, dqb), (dkf, dkb), (dvf, dvb), dg, da, dgb, cs, sn)
            small["ev"].append(dict(lnw=lns[0], lnb=lns[1], cw=dcw, dl=dl[:, 0]))
        else:
            dm_, dpd, vec, dpw, dsgw, dsgb = _odd_mix_bwd1(tag + "_mix1", dmix, s["p"], s["m"], od["pw"][j], od["ps"][j],
                                                           od["lnw"][j], od["lnb"][j], od["sgw"][j], od["sgb"][j])
            dp = _odd_dp(tag + "_dp", dm_, dpd, nct, lc)
            small["od"].append(dict(ps=vec[0], lnw=vec[1], lnb=vec[2], pw=dpw, sgw=dsgw, sgb=dsgb[:, :, 0]))
        dh = _mm_cols_bwd(tag + "_in", dp, w["in"])
        g["in"] = _wgrad_cols(tag + "_gin", s["h"], dp, w["in"].shape[0])
        deps = put_g(i, "m", g)
        if i > 0:
            dx, df, s1 = _bnm(tag + "_norm1", s["x"], dh, dx1, saved[i - 1]["f"], modt[i], 0, 1, modt[i - 1], 5,
                              nw[i, 0], nct, deps)
            dmod[i - 1][5] = s1[:, 2]
        else:
            dx, _, s1 = _bnm(tag + "_norm1", s["x"], dh, dx1, None, modt[0], 0, 1, None, 0, nw[0, 0], nct, deps)
        dmod[i][0], dmod[i][1] = s1[:, 0], s1[:, 1]
        dnw[i][0] = s1[1, 3]
    small["ev"].reverse()
    small["od"].reverse()
    dmod_t = jnp.stack([jnp.concatenate([jnp.stack(rows, axis=1), jnp.zeros((2, 2, dm), F32)], axis=1) for rows in dmod])
    small["dmod"] = dmod_t
    small["dnw"] = jnp.stack([jnp.stack(r) for r in dnw])
    return loss_blk, dx, small


WEIGHTS = ["c_ctx", "ada_w", "ada_b", "norm_w", "even_w_in", "even_w_out", "ret_decay_logit", "conv_dw_w",
           "conv_ln_w", "conv_ln_b", "odd_w_in", "odd_w_out", "pool_w", "pool_scale", "sg_ln_w", "sg_ln_b",
           "sg_w", "sg_b", "ffn_w_gate", "ffn_w_up", "ffn_w_down", "final_norm_w"]
BIG = dict(even_in="even_w_in", even_out="even_w_out", odd_in="odd_w_in", odd_out="odd_w_out",
           gate="ffn_w_gate", up="ffn_w_up", down="ffn_w_down")


def _rows(a, width=1024):
    flat = a.reshape(-1)
    n = flat.shape[0]
    per = 8 * width
    tot = -(-n // per) * per
    return jnp.pad(flat, (0, tot - n)).reshape(tot // width, width)


def _unshard(parts, lead):
    nl = len(lead)
    perm = tuple(range(1, nl + 1)) + (0, nl + 1)
    return parts.transpose(perm).reshape(tuple(lead) + (4 * parts.shape[-1],))


def _my_cols(a, chip, n):
    start = (0,) * (a.ndim - 1) + (chip * n,)
    return lax.dynamic_slice(a, start, a.shape[:-1] + (n,))


def kernel(x, c, ctx, c_ctx, ada_w, ada_b, norm_w, even_w_in, even_w_out, ret_decay_logit, conv_dw_w, conv_ln_w, conv_ln_b, odd_w_in, odd_w_out, pool_w, pool_scale, sg_ln_w, sg_ln_b, sg_w, sg_b, ffn_w_gate, ffn_w_up, ffn_w_down, final_norm_w, loss_target, m_c_ctx, m_ada_w, m_ada_b, m_norm_w, m_even_w_in, m_even_w_out, m_ret_decay_logit, m_conv_dw_w, m_conv_ln_w, m_conv_ln_b, m_odd_w_in, m_odd_w_out, m_pool_w, m_pool_scale, m_sg_ln_w, m_sg_ln_b, m_sg_w, m_sg_b, m_ffn_w_gate, m_ffn_w_up, m_ffn_w_down, m_final_norm_w, v_c_ctx, v_ada_w, v_ada_b, v_norm_w, v_even_w_in, v_even_w_out, v_ret_decay_logit, v_conv_dw_w, v_conv_ln_w, v_conv_ln_b, v_odd_w_in, v_odd_w_out, v_pool_w, v_pool_scale, v_sg_ln_w, v_sg_ln_b, v_sg_w, v_sg_b, v_ffn_w_gate, v_ffn_w_up, v_ffn_w_down, v_final_norm_w):
    wv = dict(c_ctx=c_ctx, ada_w=ada_w, ada_b=ada_b, norm_w=norm_w, even_w_in=even_w_in, even_w_out=even_w_out,
              ret_decay_logit=ret_decay_logit, conv_dw_w=conv_dw_w, conv_ln_w=conv_ln_w, conv_ln_b=conv_ln_b,
              odd_w_in=odd_w_in, odd_w_out=odd_w_out, pool_w=pool_w, pool_scale=pool_scale, sg_ln_w=sg_ln_w,
              sg_ln_b=sg_ln_b, sg_w=sg_w, sg_b=sg_b, ffn_w_gate=ffn_w_gate, ffn_w_up=ffn_w_up,
              ffn_w_down=ffn_w_down, final_norm_w=final_norm_w)
    mv = dict(zip(WEIGHTS, (m_c_ctx, m_ada_w, m_ada_b, m_norm_w, m_even_w_in, m_even_w_out, m_ret_decay_logit,
                            m_conv_dw_w, m_conv_ln_w, m_conv_ln_b, m_odd_w_in, m_odd_w_out, m_pool_w, m_pool_scale,
                            m_sg_ln_w, m_sg_ln_b, m_sg_w, m_sg_b, m_ffn_w_gate, m_ffn_w_up, m_ffn_w_down,
                            m_final_norm_w)))
    vv = dict(zip(WEIGHTS, (v_c_ctx, v_ada_w, v_ada_b, v_norm_w, v_even_w_in, v_even_w_out, v_ret_decay_logit,
                            v_conv_dw_w, v_conv_ln_w, v_conv_ln_b, v_odd_w_in, v_odd_w_out, v_pool_w, v_pool_scale,
                            v_sg_ln_w, v_sg_ln_b, v_sg_w, v_sg_b, v_ffn_w_gate, v_ffn_w_up, v_ffn_w_down,
                            v_final_norm_w)))
    xi, yi, ci = lax.axis_index("x"), lax.axis_index("y"), lax.axis_index("c")
    chip = 2 * xi + yi
    dev = 4 * xi + 2 * yi + ci
    dm = x.shape[-1]
    lc = ctx.shape[1]
    depth = ada_w.shape[0]
    n_ada = ada_w.shape[-1]

    cw_pad = jnp.pad(conv_dw_w, ((0, 0), (0, 1), (0, 0)))
    vec3 = jnp.stack([pool_scale, sg_ln_w, sg_ln_b])
    pack1 = jnp.concatenate([_rows(c), _rows(norm_w), _rows(cw_pad), _rows(vec3)], axis=0)
    g1 = _all_gather8("gather_small", pack1).reshape(8, 32, dm)
    c_all = g1[:, 0]
    per_chip = g1[0::2]
    norm_full = _unshard(per_chip[:, 8:10].reshape(4, depth, 2, dm // 4), (depth, 2))
    cw_full = _unshard(per_chip[:, 16:24].reshape(4, 2, CONV_K + 1, 128), (2, CONV_K + 1))
    vec_full = _unshard(per_chip[:, 24, :768].reshape(4, 3, 2, 128), (3, 2))

    c16 = jnp.concatenate([c_all, c_ctx[None, :], jnp.zeros((7, dm), F32)], axis=0)
    mod_sh = _ada_mod("ada_mod", c16, ada_w, _my_cols(ada_b, chip, n_ada)[:, None, :])
    g2 = _all_gather8("gather_mod", mod_sh.reshape(depth * 16, n_ada)).reshape(8, depth, 16, n_ada)
    mod_full = _unshard(g2[0::2], (depth, 16))
    mod_x = lax.dynamic_index_in_dim(mod_full, dev, axis=1, keepdims=False).reshape(depth, 6, dm)
    mod_c = mod_full[:, 8].reshape(depth, 6, dm)
    modt = jnp.pad(jnp.stack([mod_c, mod_x], axis=1), ((0, 0), (0, 0), (0, 2), (0, 0)))

    names = list(BIG)
    tr_names = ("gate", "up")
    shard = {k: (jnp.swapaxes(wv[BIG[k]], 1, 2) if k in tr_names else wv[BIG[k]]).astype(BF16) for k in names}
    roles = ("in", "out", "gate", "up", "down")

    def layer_keys(i):
        mixer = ("even_in", "even_out") if i % 2 == 0 else ("odd_in", "odd_out")
        return [(k, i // 2) for k in mixer] + [(k, i) for k in ("gate", "up", "down")]

    def as_used(got):
        return {r: (g if r == "in" else g.reshape(4 * g.shape[1], g.shape[2])) for r, g in zip(roles, got)}

    started = {}

    def get_w(i, after):
        if i > 0:
            part, first = started[i, "m"]
            got = _split_wait(f"gather_wait{i}m", part, _gather_copies(2), after, first)[1]
            return as_used(got), []
        got = _gather_weights("gather_w0", [shard[k][l] for k, l in layer_keys(0)[:2]], [modt])
        parts = [(li, p, layer_keys(li)[:2] if p == "m" else layer_keys(li)[2:])
                 for li in range(depth) for p in "mf" if (li, p) != (0, "m")]
        srcs = [shard[k][l] for _, _, keys in parts for k, l in keys]
        lands = [lax.empty((4,) + s.shape, s.dtype) for s in srcs]
        send_sems, recv_sems, srcs, lands, token = _split_start("gather_start", srcs, lands, _gather_copies(len(srcs)), [got[0]])
        t0 = 0
        for li, p, keys in parts:
            t1 = t0 + len(keys)
            started[li, p] = ((send_sems, recv_sems, srcs[t0:t1], lands[t0:t1], token), 4 * t0)
            t0 = t1
        return as_used(got), [token]

    def get_ffn(i, after):
        part, first = started[i, "f"]
        got = _split_wait(f"gather_wait{i}f", part, _gather_copies(3), after, first)[1]
        return {r: g.reshape(4 * g.shape[1], g.shape[2]) for r, g in zip(roles[2:], got)}

    idx = jnp.stack([ci, chip]).astype(jnp.int32)
    pairs, pending, stages = {}, {}, []

    def stage_keys(stage):
        i, part = stage
        return layer_keys(i)[2:] if part == "f" else layer_keys(i)[:2]

    def finish_pair(stage, after, glist=()):
        tag = f"{stage[0]}{stage[1]}"
        n, m = len(stage_keys(stage)), len(glist)
        part, first = pairs[stage]
        g_prev, from_sib = _split_wait(f"pair_wait{tag}", part, _pair_copies(n), after, first)
        if stage[1] == "f":
            pair = _add_half(f"rs_add{tag}", g_prev, from_sib, idx)
        else:
            pair = [_add_half(f"rs_add{tag}_{t}", [gl], [a], idx)[0] for t, (gl, a) in enumerate(zip(g_prev, from_sib))]
        lands = ([lax.empty((3,) + p.shape[1:], p.dtype) for p in pair]
                 + [lax.empty((4, gl.shape[1] // 2, gl.shape[2]), gl.dtype) for gl in glist])

        def copies(src_refs, land_refs, probe=False):
            return (_scatter_copies(n)(src_refs[:n], land_refs[:n], probe=probe)
                    + (_pair_copies(m)(src_refs[n:], land_refs[n:], probe=probe) if m else []))

        send_sems, recv_sems, srcs, lands, token = _split_start(f"rs_start{tag}", list(pair) + list(glist), lands, copies, [])
        pending[stage] = (g_prev, from_sib, (send_sems, recv_sems, srcs[:n], lands[:n], token))
        return (send_sems, recv_sems, srcs[n:], lands[n:], token), 3 * n

    def put_g(i, part, g):
        stage = (i, part)
        glist = [g[r].reshape(4, -1, g[r].shape[-1]) for r in (roles[2:] if part == "f" else roles[:2])]
        if stages:
            pairs[stage] = finish_pair(stages[-1], [glist[0]], glist)
        else:
            lands = [lax.empty((4, gl.shape[1] // 2, gl.shape[2]), gl.dtype) for gl in glist]
            pairs[stage] = (_split_start(f"pair_start{i}{part}", glist, lands, _pair_copies(len(glist)), []), 0)
        stages.append(stage)
        return [pairs[stage][0][4]]

    ev = dict(lgb=jnp.broadcast_to(ret_decay_logit.reshape(-1, 2 * HEADS)[:, :, None], (ret_decay_logit.shape[0], 2 * HEADS, HEAD_DIM)),
              cw=cw_full, lnw=conv_ln_w[:, None, :], lnb=conv_ln_b[:, None, :])
    od = dict(pw=pool_w, ps=vec_full[0][:, None, :], lnw=vec_full[1][:, None, :], lnb=vec_full[2][:, None, :],
              sgw=sg_w, sgb=jnp.broadcast_to(sg_b[:, :, :, None], sg_b.shape + (GC,)))
    xs = jnp.concatenate([ctx[0], x[0]], axis=0)
    loss_blk, dxs, small = _local_step(xs, loss_target[0], modt, norm_full[:, :, None, :], final_norm_w[None, :],
                                       get_w, get_ffn, put_g, ev, od, lc)

    misc = jnp.stack([
        small["dfnw"], jnp.broadcast_to(loss_blk[0, 0], (dm,)),
        jnp.concatenate([e["lnw"] for e in small["ev"]]), jnp.concatenate([e["lnb"] for e in small["ev"]]),
        jnp.concatenate([o["ps"] for o in small["od"]]), jnp.concatenate([o["lnw"] for o in small["od"]]),
        jnp.concatenate([o["lnb"] for o in small["od"]]),
        jnp.pad(jnp.concatenate([e["dl"] for e in small["ev"]]), (0, dm - 4 * HEADS)),
        jnp.stack([o["sgb"] for o in small["od"]]).reshape(-1)])
    pack2 = jnp.concatenate([
        _rows(small["dmod"]), _rows(small["dnw"]), _rows(misc), _rows(jnp.stack([e["cw"] for e in small["ev"]])),
        _rows(jnp.stack([o["pw"] for o in small["od"]])), _rows(jnp.stack([o["sgw"] for o in small["od"]]))], axis=0)
    n2 = pack2.shape[0]
    g3 = _all_gather8("gather_grads", pack2)
    tot = _sum8("sum_grads", g3.reshape(8, n2, dm))
    r_mod = depth * 16
    o_nw, o_misc = r_mod, r_mod + 8
    o_cw = o_misc + 16
    o_pw = o_cw + 2 * (CONV_K + 1) // 2
    o_sgw = o_pw + 128
    dmod_sum = tot[:r_mod].reshape(depth, 2, 8, dm)
    dmod_dev = g3.reshape(8, n2, dm)[:, :r_mod].reshape(8, depth, 2, 8, dm)
    dm_x = dmod_dev[:, :, 1, :6].reshape(8, depth, 6 * dm).transpose(1, 0, 2)
    dm_c = dmod_sum[:, 0, :6].reshape(depth, 1, 6 * dm)
    dmod16 = jnp.concatenate([dm_x, dm_c, jnp.zeros((depth, 7, 6 * dm), F32)], axis=1)
    g_ada_b = _rowsum16("ada_b_grad", dmod16)[:, 0]
    g_ada_w, dc16 = _ada_bwd("ada_bwd", c16, _my_cols(dmod16, chip, n_ada), ada_w)
    g4 = _all_gather8("gather_cctx", dc16[8:16]).reshape(8, 8, dm)
    g_c_ctx = _cctx_grad("cctx_grad", g4[0::2, 0], c_ctx[None, :])[0]

    misc_t = tot[o_misc:o_misc + 16]
    half = lambda row: misc_t[row].reshape(2, dm // 2)
    grads = dict(
        c_ctx=g_c_ctx, ada_w=g_ada_w, ada_b=g_ada_b,
        norm_w=_my_cols(tot[o_nw:o_nw + 8].reshape(depth, 2, dm), chip, dm // 4),
        ret_decay_logit=misc_t[7, :4 * HEADS].reshape(ret_decay_logit.shape),
        conv_dw_w=_my_cols(tot[o_cw:o_cw + 2 * (CONV_K + 1) // 2].reshape(2, CONV_K + 1, dm // 2)[:, :CONV_K], chip, 128),
        conv_ln_w=half(2), conv_ln_b=half(3),
        pool_w=tot[o_pw:o_pw + 128].reshape(pool_w.shape),
        pool_scale=_my_cols(half(4), chip, 128), sg_ln_w=_my_cols(half(5), chip, 128), sg_ln_b=_my_cols(half(6), chip, 128),
        sg_w=tot[o_sgw:o_sgw + 128].reshape(sg_w.shape), sg_b=misc_t[8].reshape(sg_b.shape),
        final_norm_w=misc_t[0])
    loss = misc_t[1, 0]

    last_tokens = [finish_pair(stages[-1], [g_c_ctx])[0][4]]
    deltas, new_m, new_v = {}, {}, {}
    for n in WEIGHTS:
        if n not in BIG.values():
            deltas[n], new_m[n], new_v[n] = _adamw("adamw_" + n, wv[n], grads[n], mv[n], vv[n])
    reduced = {k: None for k in names}
    for stage in stages:
        glist, from_sib, st = pending[stage]
        last = stage == stages[-1]
        after = [deltas["ada_w"]] + [reduced[k] for k, _ in stage_keys(stages[-2])] if last else last_tokens
        slots = _split_wait(f"rs_wait{stage[0]}{stage[1]}", st, _scatter_copies(len(glist)), after)[1]
        keys = stage_keys(stage)
        groups = [range(len(keys))] if stage[1] == "f" else [[t] for t in range(len(keys))]
        for grp in groups:
            ks = [keys[t][0] for t in grp]
            lyr = keys[grp[0]][1]
            outs = _sum_final(f"rs_sum_{ks[0]}{lyr}", [glist[t] for t in grp], [from_sib[t] for t in grp],
                              [slots[t] for t in grp], idx, [reduced[k] for k in ks], lyr, shard[ks[0]].shape[0])
            for k, o in zip(ks, outs):
                reduced[k] = o
    shards = dict(zip(names, _rs_share("rs_share", [reduced[k] for k in names])))

    for k in names:
        n = BIG[k]
        tr = (lambda a: jnp.swapaxes(a, 1, 2)) if k in tr_names else (lambda a: a)
        outs = _adamw("adamw_" + n, tr(wv[n]), shards[k], tr(mv[n]), tr(vv[n]), with_grad=True)
        deltas[n], new_m[n], new_v[n], grads[n] = (tr(o) for o in outs)
    grad_x = dxs[None]
    return (loss, grad_x, *[grads[n] for n in WEIGHTS], *[deltas[n] for n in WEIGHTS],
            *[new_m[n] for n in WEIGHTS], *[new_v[n] for n in WEIGHTS])
```
